```python
import math
import jax, jax.numpy as jnp
from jax import lax
import numpy as np

D_MODEL = 1024
BATCH = 8
SEQ = 8192
DEPTH = 1

SSM_WIDTH = D_MODEL // 2
SSM_GROUP = 16
SSM_GROUPS = SSM_WIDTH // SSM_GROUP
SSM_STATE = 64
CONV_WIDTH = D_MODEL // 2
CONV_KERNEL = 31
FFN_HIDDEN = ((8 * D_MODEL // 3 + 255) // 256) * 256
FFN_KERNEL = 3
N_COND = 6
IN_PROJ_WIDTH = SSM_WIDTH + 2 * CONV_WIDTH + 2 * D_MODEL
DEEPNORM_ALPHA = (2.0 * DEPTH) ** 0.25
DEEPNORM_BETA = (8.0 * DEPTH) ** -0.25
LN_EPS = 1e-5
DT_MIN = 1e-3
DT_MAX = 1e-1

kernel_name = "hybrid_s5_conformer_convffn_deepnorm_adaln"


def _layer_norm_plain(x):
    xf = x.astype(jnp.float32)
    mu = jnp.mean(xf, axis=-1, keepdims=True)
    var = jnp.mean(jnp.square(xf - mu), axis=-1, keepdims=True)
    return ((xf - mu) * lax.rsqrt(var + LN_EPS)).astype(x.dtype)


def _layer_norm_affine(x, g, b):
    xf = x.astype(jnp.float32)
    mu = jnp.mean(xf, axis=-1, keepdims=True)
    var = jnp.mean(jnp.square(xf - mu), axis=-1, keepdims=True)
    y = (xf - mu) * lax.rsqrt(var + LN_EPS) * g.astype(jnp.float32) + b.astype(jnp.float32)
    return y.astype(x.dtype)


def _modulate(h, shift, scale):
    return h * (1 + scale[:, None, :]) + shift[:, None, :]


def _causal_depthwise_conv(x, w, b):
    k = w.shape[0]
    y = lax.conv_general_dilated(
        x, w.astype(x.dtype), window_strides=(1,), padding=[(k - 1, 0)],
        dimension_numbers=("NWC", "WIO", "NWC"), feature_group_count=x.shape[-1])
    return y + b


def _complex_affine_combine(e1, e2):
    a1r, a1i, b1r, b1i = e1
    a2r, a2i, b2r, b2i = e2
    ar = a2r * a1r - a2i * a1i
    ai = a2r * a1i + a2i * a1r
    br = a2r * b1r - a2i * b1i + b2r
    bi = a2r * b1i + a2i * b1r + b2i
    return ar, ai, br, bi


def _s5_branch(u, lam_re, lam_im, log_dt, b_re, b_im, c_re, c_im, d, w_a, w_b):
    bsz, seq = u.shape[0], u.shape[1]
    uf = u.astype(jnp.float32).reshape(bsz, seq, SSM_GROUPS, SSM_GROUP)
    lr = jnp.minimum(lam_re.astype(jnp.float32), -1e-4)
    li = lam_im.astype(jnp.float32)
    dt = jnp.exp(log_dt.astype(jnp.float32))[:, None]
    mag = jnp.exp(lr * dt)
    ang = li * dt
    lbr, lbi = mag * jnp.cos(ang), mag * jnp.sin(ang)
    num_r, num_i = lbr - 1.0, lbi
    den = lr * lr + li * li
    coef_r = (num_r * lr + num_i * li) / den
    coef_i = (num_i * lr - num_r * li) / den
    br, bi = b_re.astype(jnp.float32), b_im.astype(jnp.float32)
    bbar_r = coef_r[..., None] * br - coef_i[..., None] * bi
    bbar_i = coef_r[..., None] * bi + coef_i[..., None] * br
    bu_r = jnp.einsum("bsgp,gnp->bsgn", uf, bbar_r)
    bu_i = jnp.einsum("bsgp,gnp->bsgn", uf, bbar_i)
    a_r = jnp.broadcast_to(lbr, (1, seq, SSM_GROUPS, SSM_STATE))
    a_i = jnp.broadcast_to(lbi, (1, seq, SSM_GROUPS, SSM_STATE))
    _, _, h_r, h_i = lax.associative_scan(_complex_affine_combine, (a_r, a_i, bu_r, bu_i), axis=1)
    y = (jnp.einsum("bsgn,gpn->bsgp", h_r, c_re.astype(jnp.float32))
         - jnp.einsum("bsgn,gpn->bsgp", h_i, c_im.astype(jnp.float32))
         + d.astype(jnp.float32) * uf)
    y = jax.nn.gelu(y.reshape(bsz, seq, SSM_WIDTH), approximate=False).astype(u.dtype)
    return (y @ w_a) * jax.nn.sigmoid(y @ w_b)


def _conformer_branch(a, g, dw_w, dw_b, ln_g, ln_b, w_pw):
    v = a * jax.nn.sigmoid(g)
    v = _causal_depthwise_conv(v, dw_w, dw_b)
    v = _layer_norm_affine(v, ln_g, ln_b)
    return jax.nn.silu(v) @ w_pw


def _token_mixer(h, w_in, b_in, lam_re, lam_im, log_dt, b_re, b_im, c_re, c_im, d,
                 glu_w_a, glu_w_b, cv_dw_w, cv_dw_b, cv_ln_g, cv_ln_b, cv_w_pw, w_out):
    p = h @ w_in + b_in
    o1 = SSM_WIDTH
    o2 = o1 + CONV_WIDTH
    o3 = o2 + CONV_WIDTH
    o4 = o3 + D_MODEL
    u_ssm, cv_a, cv_g, gate_ssm, gate_cv = p[..., :o1], p[..., o1:o2], p[..., o2:o3], p[..., o3:o4], p[..., o4:]
    y_ssm = _s5_branch(u_ssm, lam_re, lam_im, log_dt, b_re, b_im, c_re, c_im, d, glu_w_a, glu_w_b)
    y_cv = _conformer_branch(cv_a, cv_g, cv_dw_w, cv_dw_b, cv_ln_g, cv_ln_b, cv_w_pw)
    merged = jax.nn.sigmoid(gate_ssm) * y_ssm + jax.nn.sigmoid(gate_cv) * y_cv
    return merged @ w_out


def _conv_ffn(h, w_up, dw_w, dw_b, w_down):
    up = _causal_depthwise_conv(h @ w_up, dw_w, dw_b)
    a, v = up[..., :FFN_HIDDEN], up[..., FFN_HIDDEN:]
    return (jax.nn.gelu(a, approximate=False) * v) @ w_down


def _fwd_setup_inputs(seed: int = 0) -> dict:
    key = jax.random.key(seed)
    ks = jax.random.split(key, 32)
    L = DEPTH
    f32 = jnp.float32

    def nrm(k, shape, scale):
        return jax.random.normal(k, shape, f32) * scale

    lam_im_init = jnp.pi * jnp.arange(SSM_STATE, dtype=f32)
    return {
        "x": nrm(ks[0], (BATCH, SEQ, D_MODEL), 1.0),
        "c": nrm(ks[1], (BATCH, D_MODEL), 1.0),
        "w_cond": nrm(ks[2], (L, D_MODEL, N_COND * D_MODEL), D_MODEL ** -0.5),
        "b_cond": nrm(ks[3], (L, N_COND * D_MODEL), 0.02),
        "w_in": nrm(ks[4], (L, D_MODEL, IN_PROJ_WIDTH), D_MODEL ** -0.5),
        "b_in": nrm(ks[5], (L, IN_PROJ_WIDTH), 0.02),
        "ssm_lambda_re": -0.5 + nrm(ks[6], (L, SSM_GROUPS, SSM_STATE), 0.01),
        "ssm_lambda_im": lam_im_init + nrm(ks[7], (L, SSM_GROUPS, SSM_STATE), 0.01),
        "ssm_log_dt": jax.random.uniform(ks[8], (L, SSM_GROUPS), f32, math.log(DT_MIN), math.log(DT_MAX)),
        "ssm_b_re": nrm(ks[9], (L, SSM_GROUPS, SSM_STATE, SSM_GROUP), (2 * SSM_GROUP) ** -0.5),
        "ssm_b_im": nrm(ks[10], (L, SSM_GROUPS, SSM_STATE, SSM_GROUP), (2 * SSM_GROUP) ** -0.5),
        "ssm_c_re": nrm(ks[11], (L, SSM_GROUPS, SSM_GROUP, SSM_STATE), (2 * SSM_STATE) ** -0.5),
        "ssm_c_im": nrm(ks[12], (L, SSM_GROUPS, SSM_GROUP, SSM_STATE), (2 * SSM_STATE) ** -0.5),
        "ssm_d": nrm(ks[13], (L, SSM_GROUPS, SSM_GROUP), 1.0),
        "ssm_glu_w_a": nrm(ks[14], (L, SSM_WIDTH, D_MODEL), SSM_WIDTH ** -0.5),
        "ssm_glu_w_b": nrm(ks[15], (L, SSM_WIDTH, D_MODEL), SSM_WIDTH ** -0.5),
        "cv_dw_w": nrm(ks[16], (L, CONV_KERNEL, 1, CONV_WIDTH), CONV_KERNEL ** -0.5),
        "cv_dw_b": nrm(ks[17], (L, CONV_WIDTH), 0.02),
        "cv_ln_g": 1.0 + nrm(ks[18], (L, CONV_WIDTH), 0.02),
        "cv_ln_b": nrm(ks[19], (L, CONV_WIDTH), 0.02),
        "cv_w_pw": nrm(ks[20], (L, CONV_WIDTH, D_MODEL), CONV_WIDTH ** -0.5),
        "w_out": nrm(ks[21], (L, D_MODEL, D_MODEL), D_MODEL ** -0.5 * DEEPNORM_BETA),
        "ln1_g": 1.0 + nrm(ks[22], (L, D_MODEL), 0.02),
        "ln1_b": nrm(ks[23], (L, D_MODEL), 0.02),
        "ffn_w_up": nrm(ks[24], (L, D_MODEL, 2 * FFN_HIDDEN), D_MODEL ** -0.5),
        "ffn_dw_w": nrm(ks[25], (L, FFN_KERNEL, 1, 2 * FFN_HIDDEN), FFN_KERNEL ** -0.5),
        "ffn_dw_b": nrm(ks[26], (L, 2 * FFN_HIDDEN), 0.02),
        "ffn_w_down": nrm(ks[27], (L, FFN_HIDDEN, D_MODEL), FFN_HIDDEN ** -0.5 * DEEPNORM_BETA),
        "ln2_g": 1.0 + nrm(ks[28], (L, D_MODEL), 0.02),
        "ln2_b": nrm(ks[29], (L, D_MODEL), 0.02),
    }


def _fwd_reference(x, c, w_cond, b_cond, w_in, b_in, ssm_lambda_re, ssm_lambda_im, ssm_log_dt,
              ssm_b_re, ssm_b_im, ssm_c_re, ssm_c_im, ssm_d, ssm_glu_w_a, ssm_glu_w_b,
              cv_dw_w, cv_dw_b, cv_ln_g, cv_ln_b, cv_w_pw, w_out, ln1_g, ln1_b,
              ffn_w_up, ffn_dw_w, ffn_dw_b, ffn_w_down, ln2_g, ln2_b):
    c_act = jax.nn.silu(c)
    for l in range(DEPTH):
        mod = c_act @ w_cond[l] + b_cond[l]
        sh1, sc1, g1, sh2, sc2, g2 = jnp.split(mod, N_COND, axis=-1)
        h = _modulate(_layer_norm_plain(x), sh1, sc1)
        y = _token_mixer(h, w_in[l], b_in[l], ssm_lambda_re[l], ssm_lambda_im[l], ssm_log_dt[l],
                         ssm_b_re[l], ssm_b_im[l], ssm_c_re[l], ssm_c_im[l], ssm_d[l],
                         ssm_glu_w_a[l], ssm_glu_w_b[l], cv_dw_w[l], cv_dw_b[l],
                         cv_ln_g[l], cv_ln_b[l], cv_w_pw[l], w_out[l])
        x = _layer_norm_affine(DEEPNORM_ALPHA * x + g1[:, None, :] * y, ln1_g[l], ln1_b[l])
        h = _modulate(_layer_norm_plain(x), sh2, sc2)
        y = _conv_ffn(h, ffn_w_up[l], ffn_dw_w[l], ffn_dw_b[l], ffn_w_down[l])
        x = _layer_norm_affine(DEEPNORM_ALPHA * x + g2[:, None, :] * y, ln2_g[l], ln2_b[l])
    return x


import jax as _jax
import jax.numpy as _jnp

TWIN_FORMAT = 'train_step'
FWD_PARAMS = ['x', 'c', 'w_cond', 'b_cond', 'w_in', 'b_in', 'ssm_lambda_re', 'ssm_lambda_im', 'ssm_log_dt', 'ssm_b_re', 'ssm_b_im', 'ssm_c_re', 'ssm_c_im', 'ssm_d', 'ssm_glu_w_a', 'ssm_glu_w_b', 'cv_dw_w', 'cv_dw_b', 'cv_ln_g', 'cv_ln_b', 'cv_w_pw', 'w_out', 'ln1_g', 'ln1_b', 'ffn_w_up', 'ffn_dw_w', 'ffn_dw_b', 'ffn_w_down', 'ln2_g', 'ln2_b']
TWIN_WEIGHTS = ['w_cond', 'b_cond', 'w_in', 'b_in', 'ssm_lambda_re', 'ssm_lambda_im', 'ssm_log_dt', 'ssm_b_re', 'ssm_b_im', 'ssm_c_re', 'ssm_c_im', 'ssm_d', 'ssm_glu_w_a', 'ssm_glu_w_b', 'cv_dw_w', 'cv_dw_b', 'cv_ln_g', 'cv_ln_b', 'cv_w_pw', 'w_out', 'ln1_g', 'ln1_b', 'ffn_w_up', 'ffn_dw_w', 'ffn_dw_b', 'ffn_w_down', 'ln2_g', 'ln2_b']
TWIN_DIFF_INPUT = 'x'
TWIN_INPUTS = ['x', 'c', 'w_cond', 'b_cond', 'w_in', 'b_in', 'ssm_lambda_re', 'ssm_lambda_im', 'ssm_log_dt', 'ssm_b_re', 'ssm_b_im', 'ssm_c_re', 'ssm_c_im', 'ssm_d', 'ssm_glu_w_a', 'ssm_glu_w_b', 'cv_dw_w', 'cv_dw_b', 'cv_ln_g', 'cv_ln_b', 'cv_w_pw', 'w_out', 'ln1_g', 'ln1_b', 'ffn_w_up', 'ffn_dw_w', 'ffn_dw_b', 'ffn_w_down', 'ln2_g', 'ln2_b', 'loss_target', 'm_w_cond', 'm_b_cond', 'm_w_in', 'm_b_in', 'm_ssm_lambda_re', 'm_ssm_lambda_im', 'm_ssm_log_dt', 'm_ssm_b_re', 'm_ssm_b_im', 'm_ssm_c_re', 'm_ssm_c_im', 'm_ssm_d', 'm_ssm_glu_w_a', 'm_ssm_glu_w_b', 'm_cv_dw_w', 'm_cv_dw_b', 'm_cv_ln_g', 'm_cv_ln_b', 'm_cv_w_pw', 'm_w_out', 'm_ln1_g', 'm_ln1_b', 'm_ffn_w_up', 'm_ffn_dw_w', 'm_ffn_dw_b', 'm_ffn_w_down', 'm_ln2_g', 'm_ln2_b', 'v_w_cond', 'v_b_cond', 'v_w_in', 'v_b_in', 'v_ssm_lambda_re', 'v_ssm_lambda_im', 'v_ssm_log_dt', 'v_ssm_b_re', 'v_ssm_b_im', 'v_ssm_c_re', 'v_ssm_c_im', 'v_ssm_d', 'v_ssm_glu_w_a', 'v_ssm_glu_w_b', 'v_cv_dw_w', 'v_cv_dw_b', 'v_cv_ln_g', 'v_cv_ln_b', 'v_cv_w_pw', 'v_w_out', 'v_ln1_g', 'v_ln1_b', 'v_ffn_w_up', 'v_ffn_dw_w', 'v_ffn_dw_b', 'v_ffn_w_down', 'v_ln2_g', 'v_ln2_b']
TWIN_OUTPUTS = ['loss', 'grad_x', 'grad_w_cond', 'grad_b_cond', 'grad_w_in', 'grad_b_in', 'grad_ssm_lambda_re', 'grad_ssm_lambda_im', 'grad_ssm_log_dt', 'grad_ssm_b_re', 'grad_ssm_b_im', 'grad_ssm_c_re', 'grad_ssm_c_im', 'grad_ssm_d', 'grad_ssm_glu_w_a', 'grad_ssm_glu_w_b', 'grad_cv_dw_w', 'grad_cv_dw_b', 'grad_cv_ln_g', 'grad_cv_ln_b', 'grad_cv_w_pw', 'grad_w_out', 'grad_ln1_g', 'grad_ln1_b', 'grad_ffn_w_up', 'grad_ffn_dw_w', 'grad_ffn_dw_b', 'grad_ffn_w_down', 'grad_ln2_g', 'grad_ln2_b', 'delta_w_cond', 'delta_b_cond', 'delta_w_in', 'delta_b_in', 'delta_ssm_lambda_re', 'delta_ssm_lambda_im', 'delta_ssm_log_dt', 'delta_ssm_b_re', 'delta_ssm_b_im', 'delta_ssm_c_re', 'delta_ssm_c_im', 'delta_ssm_d', 'delta_ssm_glu_w_a', 'delta_ssm_glu_w_b', 'delta_cv_dw_w', 'delta_cv_dw_b', 'delta_cv_ln_g', 'delta_cv_ln_b', 'delta_cv_w_pw', 'delta_w_out', 'delta_ln1_g', 'delta_ln1_b', 'delta_ffn_w_up', 'delta_ffn_dw_w', 'delta_ffn_dw_b', 'delta_ffn_w_down', 'delta_ln2_g', 'delta_ln2_b', 'new_m_w_cond', 'new_m_b_cond', 'new_m_w_in', 'new_m_b_in', 'new_m_ssm_lambda_re', 'new_m_ssm_lambda_im', 'new_m_ssm_log_dt', 'new_m_ssm_b_re', 'new_m_ssm_b_im', 'new_m_ssm_c_re', 'new_m_ssm_c_im', 'new_m_ssm_d', 'new_m_ssm_glu_w_a', 'new_m_ssm_glu_w_b', 'new_m_cv_dw_w', 'new_m_cv_dw_b', 'new_m_cv_ln_g', 'new_m_cv_ln_b', 'new_m_cv_w_pw', 'new_m_w_out', 'new_m_ln1_g', 'new_m_ln1_b', 'new_m_ffn_w_up', 'new_m_ffn_dw_w', 'new_m_ffn_dw_b', 'new_m_ffn_w_down', 'new_m_ln2_g', 'new_m_ln2_b', 'new_v_w_cond', 'new_v_b_cond', 'new_v_w_in', 'new_v_b_in', 'new_v_ssm_lambda_re', 'new_v_ssm_lambda_im', 'new_v_ssm_log_dt', 'new_v_ssm_b_re', 'new_v_ssm_b_im', 'new_v_ssm_c_re', 'new_v_ssm_c_im', 'new_v_ssm_d', 'new_v_ssm_glu_w_a', 'new_v_ssm_glu_w_b', 'new_v_cv_dw_w', 'new_v_cv_dw_b', 'new_v_cv_ln_g', 'new_v_cv_ln_b', 'new_v_cv_w_pw', 'new_v_w_out', 'new_v_ln1_g', 'new_v_ln1_b', 'new_v_ffn_w_up', 'new_v_ffn_dw_w', 'new_v_ffn_dw_b', 'new_v_ffn_w_down', 'new_v_ln2_g', 'new_v_ln2_b']
TWIN_LEAF_KINDS = {'loss': 'loss', 'grad_x': 'grad_x', 'grad_w_cond': 'grad_w', 'grad_b_cond': 'grad_w', 'grad_w_in': 'grad_w', 'grad_b_in': 'grad_w', 'grad_ssm_lambda_re': 'grad_w', 'grad_ssm_lambda_im': 'grad_w', 'grad_ssm_log_dt': 'grad_w', 'grad_ssm_b_re': 'grad_w', 'grad_ssm_b_im': 'grad_w', 'grad_ssm_c_re': 'grad_w', 'grad_ssm_c_im': 'grad_w', 'grad_ssm_d': 'grad_w', 'grad_ssm_glu_w_a': 'grad_w', 'grad_ssm_glu_w_b': 'grad_w', 'grad_cv_dw_w': 'grad_w', 'grad_cv_dw_b': 'grad_w', 'grad_cv_ln_g': 'grad_w', 'grad_cv_ln_b': 'grad_w', 'grad_cv_w_pw': 'grad_w', 'grad_w_out': 'grad_w', 'grad_ln1_g': 'grad_w', 'grad_ln1_b': 'grad_w', 'grad_ffn_w_up': 'grad_w', 'grad_ffn_dw_w': 'grad_w', 'grad_ffn_dw_b': 'grad_w', 'grad_ffn_w_down': 'grad_w', 'grad_ln2_g': 'grad_w', 'grad_ln2_b': 'grad_w', 'delta_w_cond': 'delta_w', 'delta_b_cond': 'delta_w', 'delta_w_in': 'delta_w', 'delta_b_in': 'delta_w', 'delta_ssm_lambda_re': 'delta_w', 'delta_ssm_lambda_im': 'delta_w', 'delta_ssm_log_dt': 'delta_w', 'delta_ssm_b_re': 'delta_w', 'delta_ssm_b_im': 'delta_w', 'delta_ssm_c_re': 'delta_w', 'delta_ssm_c_im': 'delta_w', 'delta_ssm_d': 'delta_w', 'delta_ssm_glu_w_a': 'delta_w', 'delta_ssm_glu_w_b': 'delta_w', 'delta_cv_dw_w': 'delta_w', 'delta_cv_dw_b': 'delta_w', 'delta_cv_ln_g': 'delta_w', 'delta_cv_ln_b': 'delta_w', 'delta_cv_w_pw': 'delta_w', 'delta_w_out': 'delta_w', 'delta_ln1_g': 'delta_w', 'delta_ln1_b': 'delta_w', 'delta_ffn_w_up': 'delta_w', 'delta_ffn_dw_w': 'delta_w', 'delta_ffn_dw_b': 'delta_w', 'delta_ffn_w_down': 'delta_w', 'delta_ln2_g': 'delta_w', 'delta_ln2_b': 'delta_w', 'new_m_w_cond': 'new_m', 'new_m_b_cond': 'new_m', 'new_m_w_in': 'new_m', 'new_m_b_in': 'new_m', 'new_m_ssm_lambda_re': 'new_m', 'new_m_ssm_lambda_im': 'new_m', 'new_m_ssm_log_dt': 'new_m', 'new_m_ssm_b_re': 'new_m', 'new_m_ssm_b_im': 'new_m', 'new_m_ssm_c_re': 'new_m', 'new_m_ssm_c_im': 'new_m', 'new_m_ssm_d': 'new_m', 'new_m_ssm_glu_w_a': 'new_m', 'new_m_ssm_glu_w_b': 'new_m', 'new_m_cv_dw_w': 'new_m', 'new_m_cv_dw_b': 'new_m', 'new_m_cv_ln_g': 'new_m', 'new_m_cv_ln_b': 'new_m', 'new_m_cv_w_pw': 'new_m', 'new_m_w_out': 'new_m', 'new_m_ln1_g': 'new_m', 'new_m_ln1_b': 'new_m', 'new_m_ffn_w_up': 'new_m', 'new_m_ffn_dw_w': 'new_m', 'new_m_ffn_dw_b': 'new_m', 'new_m_ffn_w_down': 'new_m', 'new_m_ln2_g': 'new_m', 'new_m_ln2_b': 'new_m', 'new_v_w_cond': 'new_v', 'new_v_b_cond': 'new_v', 'new_v_w_in': 'new_v', 'new_v_b_in': 'new_v', 'new_v_ssm_lambda_re': 'new_v', 'new_v_ssm_lambda_im': 'new_v', 'new_v_ssm_log_dt': 'new_v', 'new_v_ssm_b_re': 'new_v', 'new_v_ssm_b_im': 'new_v', 'new_v_ssm_c_re': 'new_v', 'new_v_ssm_c_im': 'new_v', 'new_v_ssm_d': 'new_v', 'new_v_ssm_glu_w_a': 'new_v', 'new_v_ssm_glu_w_b': 'new_v', 'new_v_cv_dw_w': 'new_v', 'new_v_cv_dw_b': 'new_v', 'new_v_cv_ln_g': 'new_v', 'new_v_cv_ln_b': 'new_v', 'new_v_cv_w_pw': 'new_v', 'new_v_w_out': 'new_v', 'new_v_ln1_g': 'new_v', 'new_v_ln1_b': 'new_v', 'new_v_ffn_w_up': 'new_v', 'new_v_ffn_dw_w': 'new_v', 'new_v_ffn_dw_b': 'new_v', 'new_v_ffn_w_down': 'new_v', 'new_v_ln2_g': 'new_v', 'new_v_ln2_b': 'new_v'}


def _forward(args):
    return _fwd_reference(*[args[k] for k in FWD_PARAMS])


def _output_shape():
    out = _jax.eval_shape(lambda: _forward(_fwd_setup_inputs(0)))
    return out.shape, out.dtype

N_MICROBATCH = 1
ADAM_LR = 0.001
ADAM_B1 = 0.9
ADAM_B2 = 0.999
ADAM_EPS = 1e-08
ADAM_WD = 0.01
ADAM_STEP = 10
PER_EXAMPLE_BATCH_AXIS = {'x': 0, 'c': 0, 'loss_target': 0}
SHARED_INPUTS = []
_WEIGHT_DTYPES = {'w_cond': _jnp.float32, 'b_cond': _jnp.float32, 'w_in': _jnp.float32, 'b_in': _jnp.float32, 'ssm_lambda_re': _jnp.float32, 'ssm_lambda_im': _jnp.float32, 'ssm_log_dt': _jnp.float32, 'ssm_b_re': _jnp.float32, 'ssm_b_im': _jnp.float32, 'ssm_c_re': _jnp.float32, 'ssm_c_im': _jnp.float32, 'ssm_d': _jnp.float32, 'ssm_glu_w_a': _jnp.float32, 'ssm_glu_w_b': _jnp.float32, 'cv_dw_w': _jnp.float32, 'cv_dw_b': _jnp.float32, 'cv_ln_g': _jnp.float32, 'cv_ln_b': _jnp.float32, 'cv_w_pw': _jnp.float32, 'w_out': _jnp.float32, 'ln1_g': _jnp.float32, 'ln1_b': _jnp.float32, 'ffn_w_up': _jnp.float32, 'ffn_dw_w': _jnp.float32, 'ffn_dw_b': _jnp.float32, 'ffn_w_down': _jnp.float32, 'ln2_g': _jnp.float32, 'ln2_b': _jnp.float32}
MOMENT_SCALE = {'w_cond': 6.142250e-02, 'b_cond': 1.078939e-01, 'w_in': 2.396167e-02, 'b_in': 2.289681e-02, 'ssm_lambda_re': 4.992491e-03, 'ssm_lambda_im': 5.162632e-03, 'ssm_log_dt': 1.154879e+00, 'ssm_b_re': 2.438584e-03, 'ssm_b_im': 3.102479e-03, 'ssm_c_re': 4.474586e-03, 'ssm_c_im': 5.394074e-03, 'ssm_d': 3.966650e-02, 'ssm_glu_w_a': 2.855711e-02, 'ssm_glu_w_b': 9.559532e-03, 'cv_dw_w': 4.325733e-02, 'cv_dw_b': 6.031578e-02, 'cv_ln_g': 5.013153e-02, 'cv_ln_b': 5.875451e-02, 'cv_w_pw': 3.047861e-02, 'w_out': 6.996365e-02, 'ln1_g': 1.829920e+00, 'ln1_b': 6.925640e-01, 'ffn_w_up': 5.421763e-02, 'ffn_dw_w': 5.384528e-02, 'ffn_dw_b': 4.886764e-02, 'ffn_w_down': 1.457506e-01, 'ln2_g': 6.414330e+01, 'ln2_b': 3.534783e+00}


def _to_microbatches(a, axis):
    t = _jnp.moveaxis(a, axis, 0)
    t = t.reshape((N_MICROBATCH, t.shape[0] // N_MICROBATCH) + t.shape[1:])
    return _jnp.moveaxis(t, 1, axis + 1)


def setup_inputs(seed: int = 0) -> dict:
    inp = _fwd_setup_inputs(seed)
    key = _jax.random.fold_in(_jax.random.key(seed), 7919)
    shape, _ = _output_shape()
    out = dict(inp)
    out["loss_target"] = _jax.random.normal(_jax.random.fold_in(key, 0), shape, _jnp.float32)
    for i, name in enumerate(TWIN_WEIGHTS):
        w = inp[name].astype(_jnp.float32)
        if MOMENT_SCALE is None:
            s = _jnp.sqrt(_jnp.mean(_jnp.square(w)) + 1e-30)
        else:
            s = MOMENT_SCALE[name]
        km, kv = _jax.random.split(_jax.random.fold_in(key, i + 1))
        out[name] = w
        out["m_" + name] = s * _jax.random.normal(km, w.shape, _jnp.float32)
        out["v_" + name] = (s * s) * _jax.random.uniform(kv, w.shape, _jnp.float32, 0.5, 1.5)
    if N_MICROBATCH > 1:
        for name, axis in PER_EXAMPLE_BATCH_AXIS.items():
            out[name] = _to_microbatches(out[name], axis)
    return {'x': out['x'], 'c': out['c'], 'w_cond': out['w_cond'], 'b_cond': out['b_cond'], 'w_in': out['w_in'], 'b_in': out['b_in'], 'ssm_lambda_re': out['ssm_lambda_re'], 'ssm_lambda_im': out['ssm_lambda_im'], 'ssm_log_dt': out['ssm_log_dt'], 'ssm_b_re': out['ssm_b_re'], 'ssm_b_im': out['ssm_b_im'], 'ssm_c_re': out['ssm_c_re'], 'ssm_c_im': out['ssm_c_im'], 'ssm_d': out['ssm_d'], 'ssm_glu_w_a': out['ssm_glu_w_a'], 'ssm_glu_w_b': out['ssm_glu_w_b'], 'cv_dw_w': out['cv_dw_w'], 'cv_dw_b': out['cv_dw_b'], 'cv_ln_g': out['cv_ln_g'], 'cv_ln_b': out['cv_ln_b'], 'cv_w_pw': out['cv_w_pw'], 'w_out': out['w_out'], 'ln1_g': out['ln1_g'], 'ln1_b': out['ln1_b'], 'ffn_w_up': out['ffn_w_up'], 'ffn_dw_w': out['ffn_dw_w'], 'ffn_dw_b': out['ffn_dw_b'], 'ffn_w_down': out['ffn_w_down'], 'ln2_g': out['ln2_g'], 'ln2_b': out['ln2_b'], 'loss_target': out['loss_target'], 'm_w_cond': out['m_w_cond'], 'm_b_cond': out['m_b_cond'], 'm_w_in': out['m_w_in'], 'm_b_in': out['m_b_in'], 'm_ssm_lambda_re': out['m_ssm_lambda_re'], 'm_ssm_lambda_im': out['m_ssm_lambda_im'], 'm_ssm_log_dt': out['m_ssm_log_dt'], 'm_ssm_b_re': out['m_ssm_b_re'], 'm_ssm_b_im': out['m_ssm_b_im'], 'm_ssm_c_re': out['m_ssm_c_re'], 'm_ssm_c_im': out['m_ssm_c_im'], 'm_ssm_d': out['m_ssm_d'], 'm_ssm_glu_w_a': out['m_ssm_glu_w_a'], 'm_ssm_glu_w_b': out['m_ssm_glu_w_b'], 'm_cv_dw_w': out['m_cv_dw_w'], 'm_cv_dw_b': out['m_cv_dw_b'], 'm_cv_ln_g': out['m_cv_ln_g'], 'm_cv_ln_b': out['m_cv_ln_b'], 'm_cv_w_pw': out['m_cv_w_pw'], 'm_w_out': out['m_w_out'], 'm_ln1_g': out['m_ln1_g'], 'm_ln1_b': out['m_ln1_b'], 'm_ffn_w_up': out['m_ffn_w_up'], 'm_ffn_dw_w': out['m_ffn_dw_w'], 'm_ffn_dw_b': out['m_ffn_dw_b'], 'm_ffn_w_down': out['m_ffn_w_down'], 'm_ln2_g': out['m_ln2_g'], 'm_ln2_b': out['m_ln2_b'], 'v_w_cond': out['v_w_cond'], 'v_b_cond': out['v_b_cond'], 'v_w_in': out['v_w_in'], 'v_b_in': out['v_b_in'], 'v_ssm_lambda_re': out['v_ssm_lambda_re'], 'v_ssm_lambda_im': out['v_ssm_lambda_im'], 'v_ssm_log_dt': out['v_ssm_log_dt'], 'v_ssm_b_re': out['v_ssm_b_re'], 'v_ssm_b_im': out['v_ssm_b_im'], 'v_ssm_c_re': out['v_ssm_c_re'], 'v_ssm_c_im': out['v_ssm_c_im'], 'v_ssm_d': out['v_ssm_d'], 'v_ssm_glu_w_a': out['v_ssm_glu_w_a'], 'v_ssm_glu_w_b': out['v_ssm_glu_w_b'], 'v_cv_dw_w': out['v_cv_dw_w'], 'v_cv_dw_b': out['v_cv_dw_b'], 'v_cv_ln_g': out['v_cv_ln_g'], 'v_cv_ln_b': out['v_cv_ln_b'], 'v_cv_w_pw': out['v_cv_w_pw'], 'v_w_out': out['v_w_out'], 'v_ln1_g': out['v_ln1_g'], 'v_ln1_b': out['v_ln1_b'], 'v_ffn_w_up': out['v_ffn_w_up'], 'v_ffn_dw_w': out['v_ffn_dw_w'], 'v_ffn_dw_b': out['v_ffn_dw_b'], 'v_ffn_w_down': out['v_ffn_w_down'], 'v_ln2_g': out['v_ln2_g'], 'v_ln2_b': out['v_ln2_b']}


def _loss(weights, diff, rest, loss_target):
    with _jax.named_scope("forward"):
        args = {**rest, TWIN_DIFF_INPUT: diff, **{k: w.astype(_WEIGHT_DTYPES[k]) for k, w in weights.items()}}
        y = _forward(args)
    with _jax.named_scope("loss_head"):
        err = _jnp.square(y.astype(_jnp.float32) - loss_target)
        return 0.5 * _jnp.sum(_jnp.mean(err, axis=-1)) if err.ndim else 0.5 * err


def _adamw(w, g, m, v):
    m = ADAM_B1 * m + (1.0 - ADAM_B1) * g
    v = ADAM_B2 * v + (1.0 - ADAM_B2) * _jnp.square(g)
    m_hat = m / (1.0 - ADAM_B1 ** ADAM_STEP)
    v_hat = v / (1.0 - ADAM_B2 ** ADAM_STEP)
    delta = -ADAM_LR * (m_hat / (_jnp.sqrt(v_hat) + ADAM_EPS) + ADAM_WD * w)
    return delta, m, v


def reference(x, c, w_cond, b_cond, w_in, b_in, ssm_lambda_re, ssm_lambda_im, ssm_log_dt, ssm_b_re, ssm_b_im, ssm_c_re, ssm_c_im, ssm_d, ssm_glu_w_a, ssm_glu_w_b, cv_dw_w, cv_dw_b, cv_ln_g, cv_ln_b, cv_w_pw, w_out, ln1_g, ln1_b, ffn_w_up, ffn_dw_w, ffn_dw_b, ffn_w_down, ln2_g, ln2_b, loss_target, m_w_cond, m_b_cond, m_w_in, m_b_in, m_ssm_lambda_re, m_ssm_lambda_im, m_ssm_log_dt, m_ssm_b_re, m_ssm_b_im, m_ssm_c_re, m_ssm_c_im, m_ssm_d, m_ssm_glu_w_a, m_ssm_glu_w_b, m_cv_dw_w, m_cv_dw_b, m_cv_ln_g, m_cv_ln_b, m_cv_w_pw, m_w_out, m_ln1_g, m_ln1_b, m_ffn_w_up, m_ffn_dw_w, m_ffn_dw_b, m_ffn_w_down, m_ln2_g, m_ln2_b, v_w_cond, v_b_cond, v_w_in, v_b_in, v_ssm_lambda_re, v_ssm_lambda_im, v_ssm_log_dt, v_ssm_b_re, v_ssm_b_im, v_ssm_c_re, v_ssm_c_im, v_ssm_d, v_ssm_glu_w_a, v_ssm_glu_w_b, v_cv_dw_w, v_cv_dw_b, v_cv_ln_g, v_cv_ln_b, v_cv_w_pw, v_w_out, v_ln1_g, v_ln1_b, v_ffn_w_up, v_ffn_dw_w, v_ffn_dw_b, v_ffn_w_down, v_ln2_g, v_ln2_b):
    given = dict(x=x, c=c, w_cond=w_cond, b_cond=b_cond, w_in=w_in, b_in=b_in, ssm_lambda_re=ssm_lambda_re, ssm_lambda_im=ssm_lambda_im, ssm_log_dt=ssm_log_dt, ssm_b_re=ssm_b_re, ssm_b_im=ssm_b_im, ssm_c_re=ssm_c_re, ssm_c_im=ssm_c_im, ssm_d=ssm_d, ssm_glu_w_a=ssm_glu_w_a, ssm_glu_w_b=ssm_glu_w_b, cv_dw_w=cv_dw_w, cv_dw_b=cv_dw_b, cv_ln_g=cv_ln_g, cv_ln_b=cv_ln_b, cv_w_pw=cv_w_pw, w_out=w_out, ln1_g=ln1_g, ln1_b=ln1_b, ffn_w_up=ffn_w_up, ffn_dw_w=ffn_dw_w, ffn_dw_b=ffn_dw_b, ffn_w_down=ffn_w_down, ln2_g=ln2_g, ln2_b=ln2_b, loss_target=loss_target, m_w_cond=m_w_cond, m_b_cond=m_b_cond, m_w_in=m_w_in, m_b_in=m_b_in, m_ssm_lambda_re=m_ssm_lambda_re, m_ssm_lambda_im=m_ssm_lambda_im, m_ssm_log_dt=m_ssm_log_dt, m_ssm_b_re=m_ssm_b_re, m_ssm_b_im=m_ssm_b_im, m_ssm_c_re=m_ssm_c_re, m_ssm_c_im=m_ssm_c_im, m_ssm_d=m_ssm_d, m_ssm_glu_w_a=m_ssm_glu_w_a, m_ssm_glu_w_b=m_ssm_glu_w_b, m_cv_dw_w=m_cv_dw_w, m_cv_dw_b=m_cv_dw_b, m_cv_ln_g=m_cv_ln_g, m_cv_ln_b=m_cv_ln_b, m_cv_w_pw=m_cv_w_pw, m_w_out=m_w_out, m_ln1_g=m_ln1_g, m_ln1_b=m_ln1_b, m_ffn_w_up=m_ffn_w_up, m_ffn_dw_w=m_ffn_dw_w, m_ffn_dw_b=m_ffn_dw_b, m_ffn_w_down=m_ffn_w_down, m_ln2_g=m_ln2_g, m_ln2_b=m_ln2_b, v_w_cond=v_w_cond, v_b_cond=v_b_cond, v_w_in=v_w_in, v_b_in=v_b_in, v_ssm_lambda_re=v_ssm_lambda_re, v_ssm_lambda_im=v_ssm_lambda_im, v_ssm_log_dt=v_ssm_log_dt, v_ssm_b_re=v_ssm_b_re, v_ssm_b_im=v_ssm_b_im, v_ssm_c_re=v_ssm_c_re, v_ssm_c_im=v_ssm_c_im, v_ssm_d=v_ssm_d, v_ssm_glu_w_a=v_ssm_glu_w_a, v_ssm_glu_w_b=v_ssm_glu_w_b, v_cv_dw_w=v_cv_dw_w, v_cv_dw_b=v_cv_dw_b, v_cv_ln_g=v_cv_ln_g, v_cv_ln_b=v_cv_ln_b, v_cv_w_pw=v_cv_w_pw, v_w_out=v_w_out, v_ln1_g=v_ln1_g, v_ln1_b=v_ln1_b, v_ffn_w_up=v_ffn_w_up, v_ffn_dw_w=v_ffn_dw_w, v_ffn_dw_b=v_ffn_dw_b, v_ffn_w_down=v_ffn_w_down, v_ln2_g=v_ln2_g, v_ln2_b=v_ln2_b)
    weights = {n: given[n] for n in TWIN_WEIGHTS}
    shared = {n: given[n] for n in SHARED_INPUTS}
    per_example = {n: given[n] for n in ['x', 'c']}
    grad_fn = _jax.value_and_grad(_loss, argnums=(0, 1))

    def one_microbatch(ex, loss_target):
        ex = dict(ex)
        diff = ex.pop(TWIN_DIFF_INPUT)
        return grad_fn(weights, diff, {**shared, **ex}, loss_target)

    if N_MICROBATCH == 1:
        loss, (grad_w, grad_x) = one_microbatch(per_example, given["loss_target"])
    else:
        def body(carry, xs):
            loss_sum, grad_sum = carry
            l_k, (gw_k, gx_k) = one_microbatch(xs[0], xs[1])
            with _jax.named_scope("update"):
                return (loss_sum + l_k, _jax.tree.map(_jnp.add, grad_sum, gw_k)), gx_k

        init = (_jnp.zeros((), _jnp.float32), _jax.tree.map(_jnp.zeros_like, weights))
        (loss, grad_w), grad_x = _jax.lax.scan(body, init, (per_example, given["loss_target"]))
    with _jax.named_scope("update"):
        delta_w, new_m, new_v = {}, {}, {}
        for n in TWIN_WEIGHTS:
            delta_w[n], new_m[n], new_v[n] = _adamw(weights[n], grad_w[n], given["m_" + n], given["v_" + n])
    return (loss, grad_x, *[grad_w[n] for n in TWIN_WEIGHTS], *[delta_w[n] for n in TWIN_WEIGHTS],
            *[new_m[n] for n in TWIN_WEIGHTS], *[new_v[n] for n in TWIN_WEIGHTS])
```

```python
import functools
import math

import jax
import jax.numpy as jnp
from jax import lax
from jax.experimental import pallas as pl
from jax.experimental.pallas import tpu as pltpu

f32 = jnp.float32
bf16 = jnp.bfloat16

NDEV = 8
LANES = 128
SUBLANES = 8
SSM_GROUP = 16
SSM_STATE = 64
QW = 128
QS = 512
CONV_K = 31
CONV_HALO = 32
FFN_K = 3
FFN_HALO = 8
LN_EPS = 1e-5
ADAM_LR, ADAM_B1, ADAM_B2, ADAM_EPS, ADAM_WD, ADAM_STEP = 0.001, 0.9, 0.999, 1e-08, 0.01, 10
VMEM_LIMIT = 56 * 1024 * 1024
W_TILE_BYTES = 6 * 1024 * 1024
SUM_ROWS = 512
EW_BLOCK_BYTES = 2 * 1024 * 1024
INV_SQRT2 = 1.0 / math.sqrt(2.0)
INV_SQRT_2PI = 1.0 / math.sqrt(2.0 * math.pi)
MESH = pl.DeviceIdType.MESH


def _tile(n, want):
    t = min(n, want)
    while n % t:
        t //= 2
    return t


def _col_tile(n, rows, bytes_per):
    best = LANES if n % LANES == 0 else n
    for t in range(LANES, n + 1, LANES):
        if n % t == 0 and rows * t * bytes_per <= W_TILE_BYTES:
            best = t
    return best


def _params(*sem):
    return pltpu.CompilerParams(dimension_semantics=sem, vmem_limit_bytes=VMEM_LIMIT)


def _row(i):
    return (0, 0)


def _full(shape):
    nd = len(shape)
    return pl.BlockSpec(shape, lambda *a: (0,) * nd)


def _ln(x):
    mu = jnp.mean(x, axis=-1, keepdims=True)
    xc = x - mu
    var = jnp.mean(xc * xc, axis=-1, keepdims=True)
    rstd = lax.rsqrt(var + LN_EPS)
    return xc * rstd, rstd


def _ln_bwd(dxhat, xhat, rstd):
    return rstd * (dxhat - jnp.mean(dxhat, axis=-1, keepdims=True) - xhat * jnp.mean(dxhat * xhat, axis=-1, keepdims=True))


def _sig(x):
    return 1.0 / (1.0 + jnp.exp(-x))


def _gelu(x):
    return 0.5 * x * (1.0 + lax.erf(x * INV_SQRT2))


def _gelu_grad(x):
    return 0.5 * (1.0 + lax.erf(x * INV_SQRT2)) + x * jnp.exp(-0.5 * x * x) * INV_SQRT_2PI


def _colsum(x):
    return jnp.sum(x, axis=0, keepdims=True)


def _mm(pairs, bias=None, *, trans_w=False, out_dtype=f32, name):
    n_p = len(pairs)
    m = pairs[0][0].shape[0]
    n = pairs[0][1].shape[0 if trans_w else 1]
    ktot = sum(x.shape[1] for x, _ in pairs)
    tm = _tile(m, 512)
    tn = _col_tile(n, ktot, 2)
    dn = (((1,), (1,)), ((), ())) if trans_w else (((1,), (0,)), ((), ()))

    def body(*refs):
        o_ref = refs[-1]
        acc = None
        for xr, wr in zip(refs[:n_p], refs[n_p:2 * n_p]):
            r = lax.dot_general(xr[...].astype(bf16), wr[...].astype(bf16), dn, preferred_element_type=f32)
            acc = r if acc is None else acc + r
        if bias is not None:
            acc = acc + refs[2 * n_p][...]
        o_ref[...] = acc.astype(out_dtype)

    in_specs = [pl.BlockSpec((tm, x.shape[1]), lambda j, i: (i, 0)) for x, _ in pairs]
    if trans_w:
        in_specs += [pl.BlockSpec((tn, w.shape[1]), lambda j, i: (j, 0)) for _, w in pairs]
    else:
        in_specs += [pl.BlockSpec((w.shape[0], tn), lambda j, i: (0, j)) for _, w in pairs]
    args = [x for x, _ in pairs] + [w for _, w in pairs]
    if bias is not None:
        in_specs.append(pl.BlockSpec((1, tn), lambda j, i: (0, j)))
        args.append(bias)
    return pl.pallas_call(
        body, name=name, grid=(n // tn, m // tm), in_specs=in_specs,
        out_specs=pl.BlockSpec((tm, tn), lambda j, i: (i, j)),
        out_shape=jax.ShapeDtypeStruct((m, n), out_dtype),
        compiler_params=_params("parallel", "arbitrary"),
    )(*args)


def _mm_tn(x, dy, *, name):
    m, k = x.shape
    n = dy.shape[1]
    tm = _tile(m, 512)
    tn = _col_tile(n, k, 4)

    def body(x_ref, dy_ref, o_ref):
        @pl.when(pl.program_id(1) == 0)
        def _():
            o_ref[...] = jnp.zeros_like(o_ref)

        o_ref[...] += lax.dot_general(x_ref[...].astype(bf16), dy_ref[...].astype(bf16), (((0,), (0,)), ((), ())),
                                      preferred_element_type=f32)

    return pl.pallas_call(
        body, name=name, grid=(n // tn, m // tm),
        in_specs=[pl.BlockSpec((tm, k), lambda j, i: (i, 0)), pl.BlockSpec((tm, tn), lambda j, i: (i, j))],
        out_specs=pl.BlockSpec((k, tn), lambda j, i: (0, j)),
        out_shape=jax.ShapeDtypeStruct((k, n), f32),
        compiler_params=_params("parallel", "arbitrary"),
    )(x, dy)


def _exchange(x, *, scatter, name):
    r = x.shape[-2]

    def body(x_ref, o_ref, send_sems, recv_sems, local_sem):
        ix, iy, ic = lax.axis_index("x"), lax.axis_index("y"), lax.axis_index("c")
        me = 4 * ix + 2 * iy + ic
        mine = pltpu.make_async_copy(x_ref.at[me] if scatter else x_ref, o_ref.at[me], local_sem)
        mine.start()
        copies = []
        for k in range(1, NDEV):
            px = 1 - ix if k & 4 else ix
            py = 1 - iy if k & 2 else iy
            pc = 1 - ic if k & 1 else ic
            src = x_ref.at[4 * px + 2 * py + pc] if scatter else x_ref
            cp = pltpu.make_async_remote_copy(src_ref=src, dst_ref=o_ref.at[me], send_sem=send_sems.at[k - 1],
                                              recv_sem=recv_sems.at[k - 1], device_id=(px, py, pc), device_id_type=MESH)
            cp.start()
            copies.append(cp)
        for k, cp in enumerate(copies, start=1):
            px = 1 - ix if k & 4 else ix
            py = 1 - iy if k & 2 else iy
            pc = 1 - ic if k & 1 else ic
            pltpu.make_async_remote_copy(src_ref=x_ref.at[me] if scatter else x_ref, dst_ref=o_ref.at[4 * px + 2 * py + pc],
                                         send_sem=send_sems.at[k - 1], recv_sem=recv_sems.at[k - 1], device_id=(px, py, pc),
                                         device_id_type=MESH).wait_recv()
        for cp in copies:
            cp.wait_send()
        mine.wait()

    hbm = pl.BlockSpec(memory_space=pltpu.HBM)
    return pl.pallas_call(
        body, name=name, in_specs=[hbm], out_specs=hbm,
        out_shape=jax.ShapeDtypeStruct((NDEV, r, LANES), x.dtype),
        scratch_shapes=[pltpu.SemaphoreType.DMA((NDEV - 1,)), pltpu.SemaphoreType.DMA((NDEV - 1,)), pltpu.SemaphoreType.DMA],
    )(x)


def _sum_parts(parts, *, name):
    r = parts.shape[1]
    tr = r if r <= 4 * SUM_ROWS else SUM_ROWS
    assert r % tr == 0

    def body(p_ref, o_ref):
        acc = p_ref[0]
        for j in range(1, NDEV):
            acc = acc + p_ref[j]
        o_ref[...] = acc

    return pl.pallas_call(
        body, name=name, grid=(r // tr,),
        in_specs=[pl.BlockSpec((NDEV, tr, LANES), lambda i: (0, i, 0))],
        out_specs=pl.BlockSpec((tr, LANES), lambda i: (i, 0)),
        out_shape=jax.ShapeDtypeStruct((r, LANES), f32),
        compiler_params=_params("parallel"),
    )(parts)


def _pack(arrs):
    flat = jnp.concatenate([a.reshape(-1) for a in arrs])
    pad = (-flat.shape[0]) % (SUBLANES * LANES)
    return jnp.pad(flat, (0, pad)).reshape(-1, LANES)


def _unpack(flat, shapes):
    out, off = [], 0
    for s in shapes:
        n = math.prod(s)
        out.append(flat[..., off:off + n].reshape(flat.shape[:-1] + tuple(s)))
        off += n
    return out


def _bf16_words(w):
    return lax.bitcast_convert_type(w.astype(bf16).reshape(-1, 2), f32)


def _words_bf16(words, shape):
    return lax.bitcast_convert_type(words, bf16).reshape(shape)


def _adamw(w, g, m, v, *, name):
    r, c = w.shape
    tr = r
    while tr * c * 4 > EW_BLOCK_BYTES and tr % (2 * SUBLANES) == 0:
        tr //= 2

    def body(w_ref, g_ref, m_ref, v_ref, d_ref, nm_ref, nv_ref):
        gg = g_ref[...]
        nm = ADAM_B1 * m_ref[...] + (1.0 - ADAM_B1) * gg
        nv = ADAM_B2 * v_ref[...] + (1.0 - ADAM_B2) * (gg * gg)
        m_hat = nm / (1.0 - ADAM_B1 ** ADAM_STEP)
        v_hat = nv / (1.0 - ADAM_B2 ** ADAM_STEP)
        d_ref[...] = -ADAM_LR * (m_hat / (jnp.sqrt(v_hat) + ADAM_EPS) + ADAM_WD * w_ref[...])
        nm_ref[...] = nm
        nv_ref[...] = nv

    spec = pl.BlockSpec((tr, c), lambda i: (i, 0))
    shp = jax.ShapeDtypeStruct((r, c), f32)
    return pl.pallas_call(
        body, name=name, grid=(r // tr,), in_specs=[spec] * 4, out_specs=[spec] * 3, out_shape=[shp] * 3,
        compiler_params=_params("parallel"),
    )(w, g, m, v)


def _cond_fwd(c_all, w, b, *, name):
    nb, n = c_all.shape[0], w.shape[1]

    def body(c_ref, w_ref, b_ref, o_ref):
        cc = c_ref[...]
        o_ref[...] = jnp.dot(cc * _sig(cc), w_ref[...], preferred_element_type=f32,
                             precision=lax.Precision.HIGHEST) + b_ref[...]

    return pl.pallas_call(body, name=name, out_shape=jax.ShapeDtypeStruct((nb, n), f32),
                          compiler_params=_params())(c_all, w, b)


def _cond_bwd(c_all, dmod, *, name):
    d, n = c_all.shape[1], dmod.shape[1]

    def body(c_ref, g_ref, o_ref):
        cc = c_ref[...]
        o_ref[...] = lax.dot_general(cc * _sig(cc), g_ref[...], (((0,), (0,)), ((), ())), preferred_element_type=f32,
                                     precision=lax.Precision.HIGHEST)

    return pl.pallas_call(body, name=name, out_shape=jax.ShapeDtypeStruct((d, n), f32),
                          compiler_params=_params())(c_all, dmod)


def _ssm_disc(lam_re, lam_im, log_dt):
    lr = jnp.minimum(lam_re, -1e-4)
    li = lam_im
    dt = jnp.exp(log_dt)
    mag = jnp.exp(lr * dt)
    ang = li * dt
    lbr, lbi = mag * jnp.cos(ang), mag * jnp.sin(ang)
    num_r, num_i = lbr - 1.0, lbi
    den = lr * lr + li * li
    return lbr, lbi, (num_r * lr + num_i * li) / den, (num_i * lr - num_r * li) / den


def _ssm_prep(lam_re, lam_im, log_dt, *, name):
    def body(a, b, c, o1, o2, o3, o4):
        o1[...], o2[...], o3[...], o4[...] = _ssm_disc(a[...], b[...], c[...])

    shp = jax.ShapeDtypeStruct(lam_re.shape, f32)
    return pl.pallas_call(body, name=name, out_shape=[shp] * 4, compiler_params=_params())(lam_re, lam_im, log_dt)


def _ssm_prep_bwd(lam_re, lam_im, log_dt, cts, *, name):
    def body(a, b, c, g1, g2, g3, g4, o1, o2, o3):
        _, vjp = jax.vjp(_ssm_disc, a[...], b[...], c[...])
        o1[...], o2[...], o3[...] = vjp((g1[...], g2[...], g3[...], g4[...]))

    shp = jax.ShapeDtypeStruct(lam_re.shape, f32)
    return pl.pallas_call(body, name=name, out_shape=[shp, shp, jax.ShapeDtypeStruct(log_dt.shape, f32)],
                          compiler_params=_params())(lam_re, lam_im, log_dt, *cts)


def _chain_carries(loc_r, loc_i, pr, pi_, forward):
    row = lax.broadcasted_iota(jnp.int32, loc_r.shape, 0)
    shift = 1 if forward else SUBLANES - 1
    order = range(1, SUBLANES) if forward else range(SUBLANES - 2, -1, -1)
    er, ei = loc_r, loc_i
    for k in order:
        sr, si = pltpu.roll(er, shift, 0), pltpu.roll(ei, shift, 0)
        er = jnp.where(row == k, loc_r + pr * sr - pi_ * si, er)
        ei = jnp.where(row == k, loc_i + pr * si + pi_ * sr, ei)
    edge = 0 if forward else SUBLANES - 1
    return (jnp.where(row == edge, 0.0, pltpu.roll(er, shift, 0)), jnp.where(row == edge, 0.0, pltpu.roll(ei, shift, 0)))


def _chunk_power(ar, ai, chunk_len):
    pr, pi_ = ar, ai
    for _ in range(int(math.log2(chunk_len))):
        pr, pi_ = pr * pr - pi_ * pi_, 2.0 * pr * pi_
    return pr, pi_


def _ssm_mats(bre_ref, bim_ref, cre_ref, cim_ref, cfr_ref, cfi_ref, bbar_s, cmat_s, nq):
    for q in range(nq):
        cr, ci, br, bi = cfr_ref[q], cfi_ref[q], bre_ref[q], bim_ref[q]
        bbar_s[q, :, 0:QS] = (cr * br - ci * bi).astype(bf16)
        bbar_s[q, :, QS:2 * QS] = (cr * bi + ci * br).astype(bf16)
        cmat_s[q, 0:QS, :] = cre_ref[q].astype(bf16)
        cmat_s[q, QS:2 * QS, :] = (-cim_ref[q]).astype(bf16)


def _ssm_fwd(u_p, ar, ai, bre, bim, cre, cim, cfr, cfi, dvec, *, name):
    s, sw = u_p.shape
    nq = sw // QW
    st = nq * 2 * QS
    tb = _tile(s, 256)
    nb, nt, chunk_len = s // tb, tb // SUBLANES, s // SUBLANES
    assert chunk_len & (chunk_len - 1) == 0

    def body(u_ref, ar_ref, ai_ref, bre_ref, bim_ref, cre_ref, cim_ref, cfr_ref, cfi_ref, d_ref,
             h_out, yraw_out, y_out, buf, hc, bbar_s, cmat_s):
        ph, i = pl.program_id(0), pl.program_id(1)

        @pl.when(i == 0)
        def _():
            _ssm_mats(bre_ref, bim_ref, cre_ref, cim_ref, cfr_ref, cfi_ref, bbar_s, cmat_s, nq)

        @pl.when((ph == 0) & (i == 0))
        def _():
            hc[...] = jnp.zeros_like(hc)

        @pl.when((ph == 1) & (i == 0))
        def _():
            for q in range(nq):
                o = q * 2 * QS
                pr, pi_ = _chunk_power(ar_ref[q], ai_ref[q], chunk_len)
                sr, si = _chain_carries(hc[:, o:o + QS], hc[:, o + QS:o + 2 * QS], pr, pi_, True)
                hc[:, o:o + QS] = sr
                hc[:, o + QS:o + 2 * QS] = si

        for q in range(nq):
            o = q * 2 * QS
            buf[:, o:o + 2 * QS] = jnp.dot(u_ref[:, q * QW:(q + 1) * QW].astype(bf16), bbar_s[q], preferred_element_type=f32)

        for q in range(nq):
            o = q * 2 * QS
            a_r = jnp.broadcast_to(ar_ref[q], (SUBLANES, QS))
            a_i = jnp.broadcast_to(ai_ref[q], (SUBLANES, QS))

            def step(t, carry, o=o, a_r=a_r, a_i=a_i):
                hr, hi = carry
                r0 = pl.multiple_of(t * SUBLANES, SUBLANES)
                nr = a_r * hr - a_i * hi + buf[pl.ds(r0, SUBLANES), o:o + QS]
                ni = a_r * hi + a_i * hr + buf[pl.ds(r0, SUBLANES), o + QS:o + 2 * QS]
                buf[pl.ds(r0, SUBLANES), o:o + QS] = nr
                buf[pl.ds(r0, SUBLANES), o + QS:o + 2 * QS] = ni
                return nr, ni

            hr, hi = lax.fori_loop(0, nt, step, (hc[:, o:o + QS], hc[:, o + QS:o + 2 * QS]))
            hc[:, o:o + QS] = hr
            hc[:, o + QS:o + 2 * QS] = hi

        @pl.when(ph == 1)
        def _():
            for q in range(nq):
                o = q * 2 * QS
                hq = buf[:, o:o + 2 * QS].astype(bf16)
                h_out[:, o:o + 2 * QS] = hq
                uq = u_ref[:, q * QW:(q + 1) * QW]
                yq = jnp.dot(hq, cmat_s[q], preferred_element_type=f32) + d_ref[:, q * QW:(q + 1) * QW] * uq
                yraw_out[:, q * QW:(q + 1) * QW] = yq
                y_out[:, q * QW:(q + 1) * QW] = _gelu(yq).astype(bf16)

    blk = lambda ph, i: (i, 0)
    oblk = lambda ph, i: (i * ph, 0)
    return pl.pallas_call(
        body, name=name, grid=(2, nb),
        in_specs=[pl.BlockSpec((tb, sw), blk), _full(ar.shape), _full(ai.shape), _full(bre.shape), _full(bim.shape),
                  _full(cre.shape), _full(cim.shape), _full(cfr.shape), _full(cfi.shape), _full(dvec.shape)],
        out_specs=[pl.BlockSpec((tb, st), oblk), pl.BlockSpec((tb, sw), oblk), pl.BlockSpec((tb, sw), oblk)],
        out_shape=[jax.ShapeDtypeStruct((s, st), bf16), jax.ShapeDtypeStruct((s, sw), f32), jax.ShapeDtypeStruct((s, sw), bf16)],
        scratch_shapes=[pltpu.VMEM((tb, st), f32), pltpu.VMEM((SUBLANES, st), f32),
                        pltpu.VMEM((nq, QW, 2 * QS), bf16), pltpu.VMEM((nq, 2 * QS, QW), bf16)],
        compiler_params=_params("arbitrary", "arbitrary"),
    )(u_p, ar, ai, bre, bim, cre, cim, cfr, cfi, dvec)


def _ssm_bwd(dy_p, yraw_p, u_p, h_p, ar, ai, bre, bim, cre, cim, cfr, cfi, dvec, *, name):
    s, sw = u_p.shape
    nq = sw // QW
    st = nq * 2 * QS
    tb = _tile(s, 256)
    nb, nt, chunk_len = s // tb, tb // SUBLANES, s // SUBLANES

    def body(dy_ref, yraw_ref, u_ref, h_ref, ar_ref, ai_ref, bre_ref, bim_ref, cre_ref, cim_ref, cfr_ref, cfi_ref, d_ref,
             du_out, dbre_out, dbim_out, dcre_out, dcim_out, dcfr_out, dcfi_out, dlbr_out, dlbi_out, dd_out, dbu_out,
             buf, hf, rc, acc, dbbar, dcmat, bbar_s, cmat_s):
        ph, i = pl.program_id(0), pl.program_id(1)

        @pl.when(i == 0)
        def _():
            _ssm_mats(bre_ref, bim_ref, cre_ref, cim_ref, cfr_ref, cfi_ref, bbar_s, cmat_s, nq)

        @pl.when((ph == 0) & (i == 0))
        def _():
            rc[...] = jnp.zeros_like(rc)

        @pl.when((ph == 1) & (i == 0))
        def _():
            for q in range(nq):
                o = q * 2 * QS
                pr, pi_ = _chunk_power(ar_ref[q], ai_ref[q], chunk_len)
                sr, si = _chain_carries(rc[:, o:o + QS], rc[:, o + QS:o + 2 * QS], pr, -pi_, False)
                rc[:, o:o + QS] = sr
                rc[:, o + QS:o + 2 * QS] = si
            acc[...] = jnp.zeros_like(acc)
            dbbar[...] = jnp.zeros_like(dbbar)
            dcmat[...] = jnp.zeros_like(dcmat)
            dd_out[...] = jnp.zeros_like(dd_out)
            dbu_out[...] = jnp.zeros_like(dbu_out)

        dyraw = dy_ref[...] * _gelu_grad(yraw_ref[...])
        for q in range(nq):
            o = q * 2 * QS
            buf[:, o:o + 2 * QS] = lax.dot_general(dyraw[:, q * QW:(q + 1) * QW].astype(bf16), cmat_s[q],
                                                   (((1,), (1,)), ((), ())), preferred_element_type=f32)

        def recur(with_grad):
            for q in range(nq):
                o = q * 2 * QS
                a_r = jnp.broadcast_to(ar_ref[q], (SUBLANES, QS))
                a_i = jnp.broadcast_to(ai_ref[q], (SUBLANES, QS))

                def step(j, carry, o=o, a_r=a_r, a_i=a_i):
                    r0 = pl.multiple_of((nt - 1 - j) * SUBLANES, SUBLANES)
                    if with_grad:
                        rr, ri, gr, gi = carry
                        hr = hf[pl.ds(r0, SUBLANES), o:o + QS]
                        hi = hf[pl.ds(r0, SUBLANES), o + QS:o + 2 * QS]
                        gr = gr + hr * rr + hi * ri
                        gi = gi + hr * ri - hi * rr
                    else:
                        rr, ri = carry
                    nr = buf[pl.ds(r0, SUBLANES), o:o + QS] + a_r * rr + a_i * ri
                    ni = buf[pl.ds(r0, SUBLANES), o + QS:o + 2 * QS] + a_r * ri - a_i * rr
                    buf[pl.ds(r0, SUBLANES), o:o + QS] = nr
                    buf[pl.ds(r0, SUBLANES), o + QS:o + 2 * QS] = ni
                    return (nr, ni, gr, gi) if with_grad else (nr, ni)

                init = (rc[:, o:o + QS], rc[:, o + QS:o + 2 * QS])
                if with_grad:
                    init = init + (acc[:, o:o + QS], acc[:, o + QS:o + 2 * QS])
                res = lax.fori_loop(0, nt, step, init)
                rc[:, o:o + QS] = res[0]
                rc[:, o + QS:o + 2 * QS] = res[1]
                if with_grad:
                    acc[:, o:o + QS] = res[2]
                    acc[:, o + QS:o + 2 * QS] = res[3]

        @pl.when(ph == 0)
        def _():
            recur(False)

        @pl.when(ph == 1)
        def _():
            hf[...] = h_ref[...].astype(f32)
            recur(True)
            uu = u_ref[...]
            dd_out[...] += _colsum(dyraw * uu)
            for q in range(nq):
                o = q * 2 * QS
                cs = slice(q * QW, (q + 1) * QW)
                lam = buf[:, o:o + 2 * QS].astype(bf16)
                duq = lax.dot_general(lam, bbar_s[q], (((1,), (1,)), ((), ())), preferred_element_type=f32) \
                    + d_ref[:, cs] * dyraw[:, cs]
                du_out[:, cs] = duq.astype(bf16)
                dbu_out[:, cs] += _colsum(duq)
                dbbar[q] += lax.dot_general(uu[:, cs].astype(bf16), lam, (((0,), (0,)), ((), ())), preferred_element_type=f32)
                dcmat[q] += lax.dot_general(h_ref[:, o:o + 2 * QS], dyraw[:, cs].astype(bf16), (((0,), (0,)), ((), ())),
                                            preferred_element_type=f32)

        @pl.when((ph == 1) & (i == nb - 1))
        def _():
            for q in range(nq):
                o = q * 2 * QS
                cr, ci, br, bi = cfr_ref[q], cfi_ref[q], bre_ref[q], bim_ref[q]
                gr, gi = dbbar[q, :, 0:QS], dbbar[q, :, QS:2 * QS]
                dbre_out[q] = cr * gr + ci * gi
                dbim_out[q] = cr * gi - ci * gr
                dcfr_out[q] = _colsum(gr * br + gi * bi)
                dcfi_out[q] = _colsum(gi * br - gr * bi)
                dcre_out[q] = dcmat[q, 0:QS, :]
                dcim_out[q] = -dcmat[q, QS:2 * QS, :]
                dlbr_out[q] = _colsum(acc[:, o:o + QS])
                dlbi_out[q] = _colsum(acc[:, o + QS:o + 2 * QS])

    blk = lambda ph, i: (nb - 1 - i, 0)
    oblk = lambda ph, i: ((nb - 1 - i) * ph + (nb - 1) * (1 - ph), 0)
    pshapes = [ar.shape, ai.shape, bre.shape, bim.shape, cre.shape, cim.shape, cfr.shape, cfi.shape, dvec.shape]
    oshapes = [bre.shape, bim.shape, cre.shape, cim.shape, cfr.shape, cfi.shape, ar.shape, ai.shape, dvec.shape, dvec.shape]
    return pl.pallas_call(
        body, name=name, grid=(2, nb),
        in_specs=[pl.BlockSpec((tb, sw), blk), pl.BlockSpec((tb, sw), blk), pl.BlockSpec((tb, sw), blk),
                  pl.BlockSpec((tb, st), blk)] + [_full(p) for p in pshapes],
        out_specs=[pl.BlockSpec((tb, sw), oblk)] + [_full(p) for p in oshapes],
        out_shape=[jax.ShapeDtypeStruct((s, sw), bf16)] + [jax.ShapeDtypeStruct(p, f32) for p in oshapes],
        scratch_shapes=[pltpu.VMEM((tb, st), f32), pltpu.VMEM((tb, st), f32), pltpu.VMEM((SUBLANES, st), f32),
                        pltpu.VMEM((SUBLANES, st), f32), pltpu.VMEM((nq, QW, 2 * QS), f32), pltpu.VMEM((nq, 2 * QS, QW), f32),
                        pltpu.VMEM((nq, QW, 2 * QS), bf16), pltpu.VMEM((nq, 2 * QS, QW), bf16)],
        compiler_params=_params("arbitrary", "arbitrary"),
    )(dy_p, yraw_p, u_p, h_p, ar, ai, bre, bim, cre, cim, cfr, cfi, dvec)


def _lnmod(x, sc, sh, *, name):
    s, d = x.shape
    tb = _tile(s, 512)

    def body(x_ref, sc_ref, sh_ref, o_ref):
        xh, _ = _ln(x_ref[...])
        o_ref[...] = (xh * (1.0 + sc_ref[...]) + sh_ref[...]).astype(bf16)

    blk = pl.BlockSpec((tb, d), lambda i: (i, 0))
    vec = pl.BlockSpec((1, d), _row)
    return pl.pallas_call(body, name=name, grid=(s // tb,), in_specs=[blk, vec, vec], out_specs=blk,
                          out_shape=jax.ShapeDtypeStruct((s, d), bf16), compiler_params=_params("parallel"))(x, sc, sh)


def _conv_taps(ext_ref, w_ref, tb, ntap, off):
    acc = ext_ref[pl.ds(off, tb), :] * w_ref[pl.ds(0, 1), :]
    for k in range(1, ntap):
        acc = acc + ext_ref[pl.ds(off + k, tb), :] * w_ref[pl.ds(k, 1), :]
    return acc


def _conv_halo_specs(tb, cw, halo, s):
    per = tb // halo
    prev = pl.BlockSpec((halo, cw), lambda i: (jnp.maximum(i * per - 1, 0), 0))
    nxt = pl.BlockSpec((halo, cw), lambda i: (jnp.minimum((i + 1) * per, s // halo - 1), 0))
    return prev, nxt


def _conv_v2(a_ref, g_ref, ah_ref, gh_ref, w_ref, b_ref, ext, tb, i):
    gg = g_ref[...]
    ext[pl.ds(CONV_HALO, tb), :] = a_ref[...] * _sig(gg)
    ext[pl.ds(0, CONV_HALO), :] = jnp.where(i > 0, ah_ref[...] * _sig(gh_ref[...]), 0.0)
    return _conv_taps(ext, w_ref, tb, CONV_K, CONV_HALO - CONV_K + 1) + b_ref[...]


def _silu_grad(x):
    sg = _sig(x)
    return sg * (1.0 + x * (1.0 - sg))


def _conv_fwd(cva, cvg, w, b, lng, lnb, *, name):
    s, cw = cva.shape
    tb = _tile(s, 256)
    prev, _ = _conv_halo_specs(tb, cw, CONV_HALO, s)

    def body(a_ref, g_ref, ah_ref, gh_ref, w_ref, b_ref, lng_ref, lnb_ref, o_ref, ext):
        v2 = _conv_v2(a_ref, g_ref, ah_ref, gh_ref, w_ref, b_ref, ext, tb, pl.program_id(0))
        xh, _ = _ln(v2)
        v3 = xh * lng_ref[...] + lnb_ref[...]
        o_ref[...] = (v3 * _sig(v3)).astype(bf16)

    blk = pl.BlockSpec((tb, cw), lambda i: (i, 0))
    vec = pl.BlockSpec((1, cw), _row)
    return pl.pallas_call(
        body, name=name, grid=(s // tb,), in_specs=[blk, blk, prev, prev, _full(w.shape), vec, vec, vec], out_specs=blk,
        out_shape=jax.ShapeDtypeStruct((s, cw), bf16), scratch_shapes=[pltpu.VMEM((tb + CONV_HALO, cw), f32)],
        compiler_params=_params("parallel"))(cva, cvg, cva, cvg, w, b, lng, lnb)


def _conv_bwd_ln(dv4, cva, cvg, w, b, lng, lnb, *, name):
    s, cw = cva.shape
    tb = _tile(s, 256)
    prev, _ = _conv_halo_specs(tb, cw, CONV_HALO, s)

    def body(d_ref, a_ref, g_ref, ah_ref, gh_ref, w_ref, b_ref, lng_ref, lnb_ref, o_ref, dg_ref, db_ref, ext):
        i = pl.program_id(0)

        @pl.when(i == 0)
        def _():
            dg_ref[...] = jnp.zeros_like(dg_ref)
            db_ref[...] = jnp.zeros_like(db_ref)

        v2 = _conv_v2(a_ref, g_ref, ah_ref, gh_ref, w_ref, b_ref, ext, tb, i)
        xh, rstd = _ln(v2)
        v3 = xh * lng_ref[...] + lnb_ref[...]
        dv3 = d_ref[...] * _silu_grad(v3)
        dg_ref[...] += _colsum(dv3 * xh)
        db_ref[...] += _colsum(dv3)
        o_ref[...] = _ln_bwd(dv3 * lng_ref[...], xh, rstd)

    blk = pl.BlockSpec((tb, cw), lambda i: (i, 0))
    vec = pl.BlockSpec((1, cw), _row)
    vshape = jax.ShapeDtypeStruct((1, cw), f32)
    return pl.pallas_call(
        body, name=name, grid=(s // tb,), in_specs=[blk, blk, blk, prev, prev, _full(w.shape), vec, vec, vec],
        out_specs=[blk, vec, vec], out_shape=[jax.ShapeDtypeStruct((s, cw), f32), vshape, vshape],
        scratch_shapes=[pltpu.VMEM((tb + CONV_HALO, cw), f32)],
        compiler_params=_params("arbitrary"))(dv4, cva, cvg, cva, cvg, w, b, lng, lnb)


def _conv_bwd_taps(dv2, cva, cvg, w, *, name):
    s, cw = cva.shape
    tb = _tile(s, 256)
    nb = s // tb
    prev, nxt = _conv_halo_specs(tb, cw, CONV_HALO, s)

    def body(d_ref, dn_ref, a_ref, g_ref, ah_ref, gh_ref, w_ref, da_ref, dg_ref, dw_ref, db_ref, sa_ref, sg_ref, ext, dext):
        i = pl.program_id(0)

        @pl.when(i == 0)
        def _():
            for r in (dw_ref, db_ref, sa_ref, sg_ref):
                r[...] = jnp.zeros_like(r)

        aa, gg = a_ref[...], g_ref[...]
        sg = _sig(gg)
        ext[pl.ds(CONV_HALO, tb), :] = aa * sg
        ext[pl.ds(0, CONV_HALO), :] = jnp.where(i > 0, ah_ref[...] * _sig(gh_ref[...]), 0.0)
        dd = d_ref[...]
        dext[pl.ds(0, tb), :] = dd
        dext[pl.ds(tb, CONV_HALO), :] = jnp.where(i < nb - 1, dn_ref[...], 0.0)
        dv = dext[pl.ds(CONV_K - 1, tb), :] * w_ref[pl.ds(0, 1), :]
        for k in range(1, CONV_K):
            dv = dv + dext[pl.ds(CONV_K - 1 - k, tb), :] * w_ref[pl.ds(k, 1), :]
        for k in range(CONV_K):
            dw_ref[pl.ds(k, 1), :] += _colsum(dd * ext[pl.ds(CONV_HALO - CONV_K + 1 + k, tb), :])
        db_ref[...] += _colsum(dd)
        da = dv * sg
        dgate = dv * aa * sg * (1.0 - sg)
        sa_ref[...] += _colsum(da)
        sg_ref[...] += _colsum(dgate)
        da_ref[...] = da.astype(bf16)
        dg_ref[...] = dgate.astype(bf16)

    blk = pl.BlockSpec((tb, cw), lambda i: (i, 0))
    vec = pl.BlockSpec((1, cw), _row)
    vshape = jax.ShapeDtypeStruct((1, cw), f32)
    act = jax.ShapeDtypeStruct((s, cw), bf16)
    return pl.pallas_call(
        body, name=name, grid=(nb,), in_specs=[blk, nxt, blk, blk, prev, prev, _full(w.shape)],
        out_specs=[blk, blk, _full(w.shape), vec, vec, vec],
        out_shape=[act, act, jax.ShapeDtypeStruct(w.shape, f32), vshape, vshape, vshape],
        scratch_shapes=[pltpu.VMEM((tb + CONV_HALO, cw), f32), pltpu.VMEM((tb + CONV_HALO, cw), f32)],
        compiler_params=_params("arbitrary"))(dv2, dv2, cva, cvg, cva, cvg, w)


def _glu_merge(ya, yb, ycv, gs, gc, *, name):
    s, d = ya.shape
    tb = _tile(s, 512)

    def body(ya_ref, yb_ref, ycv_ref, gs_ref, gc_ref, o_ref):
        z = ya_ref[...] * _sig(yb_ref[...])
        o_ref[...] = (_sig(gs_ref[...]) * z + _sig(gc_ref[...]) * ycv_ref[...]).astype(bf16)

    blk = pl.BlockSpec((tb, d), lambda i: (i, 0))
    return pl.pallas_call(body, name=name, grid=(s // tb,), in_specs=[blk] * 5, out_specs=blk,
                          out_shape=jax.ShapeDtypeStruct((s, d), bf16), compiler_params=_params("parallel"))(ya, yb, ycv, gs, gc)


def _glu_merge_bwd(dm, ya, yb, ycv, gs, gc, *, name):
    s, d = ya.shape
    tb = _tile(s, 512)

    def body(dm_ref, ya_ref, yb_ref, ycv_ref, gs_ref, gc_ref, dya_ref, dyb_ref, dycv_ref, dgs_ref, dgc_ref, sgs_ref, sgc_ref):
        @pl.when(pl.program_id(0) == 0)
        def _():
            sgs_ref[...] = jnp.zeros_like(sgs_ref)
            sgc_ref[...] = jnp.zeros_like(sgc_ref)

        dmv, yav = dm_ref[...], ya_ref[...]
        sb, ss, scv = _sig(yb_ref[...]), _sig(gs_ref[...]), _sig(gc_ref[...])
        z = yav * sb
        dz = dmv * ss
        dgs = dmv * z * ss * (1.0 - ss)
        dgc = dmv * ycv_ref[...] * scv * (1.0 - scv)
        dya_ref[...] = (dz * sb).astype(bf16)
        dyb_ref[...] = (dz * yav * sb * (1.0 - sb)).astype(bf16)
        dycv_ref[...] = (dmv * scv).astype(bf16)
        dgs_ref[...] = dgs.astype(bf16)
        dgc_ref[...] = dgc.astype(bf16)
        sgs_ref[...] += _colsum(dgs)
        sgc_ref[...] += _colsum(dgc)

    blk = pl.BlockSpec((tb, d), lambda i: (i, 0))
    vec = pl.BlockSpec((1, d), _row)
    act = jax.ShapeDtypeStruct((s, d), bf16)
    vshape = jax.ShapeDtypeStruct((1, d), f32)
    return pl.pallas_call(body, name=name, grid=(s // tb,), in_specs=[blk] * 6, out_specs=[blk] * 5 + [vec, vec],
                          out_shape=[act] * 5 + [vshape, vshape], compiler_params=_params("arbitrary"))(dm, ya, yb, ycv, gs, gc)


def _resid_ln_mod(x, o, g, lng, lnb, sc, sh, alpha, *, name):
    s, d = x.shape
    tb = _tile(s, 512)

    def body(x_ref, o_ref, g_ref, lng_ref, lnb_ref, sc_ref, sh_ref, x1_ref, h_ref):
        xh, _ = _ln(alpha * x_ref[...] + g_ref[...] * o_ref[...])
        x1 = xh * lng_ref[...] + lnb_ref[...]
        x1_ref[...] = x1
        xh1, _ = _ln(x1)
        h_ref[...] = (xh1 * (1.0 + sc_ref[...]) + sh_ref[...]).astype(bf16)

    blk = pl.BlockSpec((tb, d), lambda i: (i, 0))
    vec = pl.BlockSpec((1, d), _row)
    return pl.pallas_call(body, name=name, grid=(s // tb,), in_specs=[blk, blk] + [vec] * 5, out_specs=[blk, blk],
                          out_shape=[jax.ShapeDtypeStruct((s, d), f32), jax.ShapeDtypeStruct((s, d), bf16)],
                          compiler_params=_params("parallel"))(x, o, g, lng, lnb, sc, sh)


def _resid_ln_loss(x1, y2, g, lng, lnb, tgt, alpha, *, name):
    s, d = x1.shape
    tb = _tile(s, 512)

    def body(x1_ref, y_ref, g_ref, lng_ref, lnb_ref, t_ref, dr_ref, dy_ref, loss_ref, dlg_ref, dlb_ref, dg_ref):
        @pl.when(pl.program_id(0) == 0)
        def _():
            for r in (loss_ref, dlg_ref, dlb_ref, dg_ref):
                r[...] = jnp.zeros_like(r)

        yv = y_ref[...]
        xh, rstd = _ln(alpha * x1_ref[...] + g_ref[...] * yv)
        err = xh * lng_ref[...] + lnb_ref[...] - t_ref[...]
        loss_ref[...] += 0.5 * jnp.sum(jnp.sum(err * err, axis=-1, keepdims=True) / d, axis=0, keepdims=True)
        dx2 = err / d
        dlg_ref[...] += _colsum(dx2 * xh)
        dlb_ref[...] += _colsum(dx2)
        dr = _ln_bwd(dx2 * lng_ref[...], xh, rstd)
        dg_ref[...] += _colsum(dr * yv)
        dr_ref[...] = dr
        dy_ref[...] = (g_ref[...] * dr).astype(bf16)

    blk = pl.BlockSpec((tb, d), lambda i: (i, 0))
    vec = pl.BlockSpec((1, d), _row)
    vshape = jax.ShapeDtypeStruct((1, d), f32)
    return pl.pallas_call(
        body, name=name, grid=(s // tb,), in_specs=[blk, blk, vec, vec, vec, blk],
        out_specs=[blk, blk, pl.BlockSpec((1, 1), _row), vec, vec, vec],
        out_shape=[jax.ShapeDtypeStruct((s, d), f32), jax.ShapeDtypeStruct((s, d), bf16),
                   jax.ShapeDtypeStruct((1, 1), f32), vshape, vshape, vshape],
        compiler_params=_params("arbitrary"))(x1, y2, g, lng, lnb, tgt)


def _mid_bwd(dh2, x1, dr2, x, o, g, sc, lng, alpha, *, name):
    s, d = x.shape
    tb = _tile(s, 512)

    def body(dh_ref, x1_ref, dr2_ref, x_ref, o_ref, g_ref, sc_ref, lng_ref,
             dr1_ref, do_ref, dsc_ref, dsh_ref, dlg_ref, dlb_ref, dg_ref):
        @pl.when(pl.program_id(0) == 0)
        def _():
            for r in (dsc_ref, dsh_ref, dlg_ref, dlb_ref, dg_ref):
                r[...] = jnp.zeros_like(r)

        dh = dh_ref[...]
        xh1, rstd1 = _ln(x1_ref[...])
        dsc_ref[...] += _colsum(dh * xh1)
        dsh_ref[...] += _colsum(dh)
        dx1 = alpha * dr2_ref[...] + _ln_bwd(dh * (1.0 + sc_ref[...]), xh1, rstd1)
        ov = o_ref[...]
        xhr, rstdr = _ln(alpha * x_ref[...] + g_ref[...] * ov)
        dlg_ref[...] += _colsum(dx1 * xhr)
        dlb_ref[...] += _colsum(dx1)
        dr1 = _ln_bwd(dx1 * lng_ref[...], xhr, rstdr)
        dg_ref[...] += _colsum(dr1 * ov)
        dr1_ref[...] = dr1
        do_ref[...] = (g_ref[...] * dr1).astype(bf16)

    blk = pl.BlockSpec((tb, d), lambda i: (i, 0))
    vec = pl.BlockSpec((1, d), _row)
    vshape = jax.ShapeDtypeStruct((1, d), f32)
    return pl.pallas_call(
        body, name=name, grid=(s // tb,), in_specs=[blk] * 5 + [vec] * 3, out_specs=[blk, blk] + [vec] * 5,
        out_shape=[jax.ShapeDtypeStruct((s, d), f32), jax.ShapeDtypeStruct((s, d), bf16)] + [vshape] * 5,
        compiler_params=_params("arbitrary"))(dh2, x1, dr2, x, o, g, sc, lng)


def _final_bwd(dh1, x, dr1, sc, alpha, *, name):
    s, d = x.shape
    tb = _tile(s, 512)

    def body(dh_ref, x_ref, dr1_ref, sc_ref, dx_ref, dsc_ref, dsh_ref):
        @pl.when(pl.program_id(0) == 0)
        def _():
            dsc_ref[...] = jnp.zeros_like(dsc_ref)
            dsh_ref[...] = jnp.zeros_like(dsh_ref)

        dh = dh_ref[...]
        xh, rstd = _ln(x_ref[...])
        dsc_ref[...] += _colsum(dh * xh)
        dsh_ref[...] += _colsum(dh)
        dx_ref[...] = alpha * dr1_ref[...] + _ln_bwd(dh * (1.0 + sc_ref[...]), xh, rstd)

    blk = pl.BlockSpec((tb, d), lambda i: (i, 0))
    vec = pl.BlockSpec((1, d), _row)
    vshape = jax.ShapeDtypeStruct((1, d), f32)
    return pl.pallas_call(body, name=name, grid=(s // tb,), in_specs=[blk, blk, blk, vec], out_specs=[blk, vec, vec],
                          out_shape=[jax.ShapeDtypeStruct((s, d), f32), vshape, vshape],
                          compiler_params=_params("arbitrary"))(dh1, x, dr1, sc)


def _ffn_specs(s, fh, tb, tc):
    per = tb // FFN_HALO
    blk = pl.BlockSpec((tb, tc), lambda j, i: (i, j))
    prev = pl.BlockSpec((FFN_HALO, tc), lambda j, i: (jnp.maximum(i * per - 1, 0), j))
    nxt = pl.BlockSpec((FFN_HALO, tc), lambda j, i: (jnp.minimum((i + 1) * per, s // FFN_HALO - 1), j))
    taps = pl.BlockSpec((FFN_HALO, tc), lambda j, i: (0, j))
    vec = pl.BlockSpec((1, tc), lambda j, i: (0, j))
    return blk, prev, nxt, taps, vec


def _ffn_mid(upa, upv, wa, wv, ba, bv, *, name):
    s, fh = upa.shape
    tb, tc = _tile(s, 512), _tile(fh, 256)
    blk, prev, _, taps, vec = _ffn_specs(s, fh, tb, tc)
    off = FFN_HALO - FFN_K + 1

    def body(a_ref, v_ref, ah_ref, vh_ref, wa_ref, wv_ref, ba_ref, bv_ref, o_ref, exta, extv):
        first = pl.program_id(1) == 0
        exta[pl.ds(FFN_HALO, tb), :] = a_ref[...]
        extv[pl.ds(FFN_HALO, tb), :] = v_ref[...]
        exta[pl.ds(0, FFN_HALO), :] = jnp.where(first, 0.0, ah_ref[...])
        extv[pl.ds(0, FFN_HALO), :] = jnp.where(first, 0.0, vh_ref[...])
        a2 = _conv_taps(exta, wa_ref, tb, FFN_K, off) + ba_ref[...]
        v2 = _conv_taps(extv, wv_ref, tb, FFN_K, off) + bv_ref[...]
        o_ref[...] = (_gelu(a2) * v2).astype(bf16)

    return pl.pallas_call(
        body, name=name, grid=(fh // tc, s // tb), in_specs=[blk, blk, prev, prev, taps, taps, vec, vec], out_specs=blk,
        out_shape=jax.ShapeDtypeStruct((s, fh), bf16),
        scratch_shapes=[pltpu.VMEM((tb + FFN_HALO, tc), f32)] * 2,
        compiler_params=_params("parallel", "arbitrary"))(upa, upv, upa, upv, wa, wv, ba, bv)


def _ffn_mid_bwd(df, upa, upv, wa, wv, ba, bv, *, name):
    s, fh = upa.shape
    tb, tc = _tile(s, 512), _tile(fh, 256)
    nb = s // tb
    blk, prev, nxt, taps, vec = _ffn_specs(s, fh, tb, tc)
    off = FFN_HALO - FFN_K + 1
    te = tb + FFN_HALO

    def body(df_ref, dfn_ref, a_ref, v_ref, ah_ref, vh_ref, an_ref, vn_ref, wa_ref, wv_ref, ba_ref, bv_ref,
             da_ref, dv_ref, dwa_ref, dwv_ref, dba_ref, dbv_ref, exta, extv, dexta, dextv):
        i = pl.program_id(1)

        @pl.when(i == 0)
        def _():
            for r in (dwa_ref, dwv_ref, dba_ref, dbv_ref):
                r[...] = jnp.zeros_like(r)

        last = i == nb - 1
        for ext, c_ref, h_ref, n_ref in ((exta, a_ref, ah_ref, an_ref), (extv, v_ref, vh_ref, vn_ref)):
            ext[pl.ds(0, FFN_HALO), :] = jnp.where(i == 0, 0.0, h_ref[...])
            ext[pl.ds(FFN_HALO, tb), :] = c_ref[...]
            ext[pl.ds(FFN_HALO + tb, FFN_HALO), :] = jnp.where(last, 0.0, n_ref[...])
        a2 = _conv_taps(exta, wa_ref, te, FFN_K, off) + ba_ref[...]
        v2 = _conv_taps(extv, wv_ref, te, FFN_K, off) + bv_ref[...]
        dexta[pl.ds(0, tb), :] = df_ref[...]
        dexta[pl.ds(tb, FFN_HALO), :] = jnp.where(last, 0.0, dfn_ref[...])
        dfe = dexta[...]
        da2 = dfe * v2 * _gelu_grad(a2)
        dv2 = dfe * _gelu(a2)
        dexta[...] = da2
        dextv[...] = dv2
        for ext, dext, w_ref, dw_ref, db_ref, o_ref in ((exta, dexta, wa_ref, dwa_ref, dba_ref, da_ref),
                                                        (extv, dextv, wv_ref, dwv_ref, dbv_ref, dv_ref)):
            dcur = dext[pl.ds(0, tb), :]
            dup = dext[pl.ds(FFN_K - 1, tb), :] * w_ref[pl.ds(0, 1), :]
            for k in range(1, FFN_K):
                dup = dup + dext[pl.ds(FFN_K - 1 - k, tb), :] * w_ref[pl.ds(k, 1), :]
            o_ref[...] = dup.astype(bf16)
            for k in range(FFN_K):
                dw_ref[pl.ds(k, 1), :] += _colsum(dcur * ext[pl.ds(off + k, tb), :])
            db_ref[...] += _colsum(dcur)

    act = jax.ShapeDtypeStruct((s, fh), bf16)
    wshape = jax.ShapeDtypeStruct((FFN_HALO, fh), f32)
    vshape = jax.ShapeDtypeStruct((1, fh), f32)
    return pl.pallas_call(
        body, name=name, grid=(fh // tc, nb),
        in_specs=[blk, nxt, blk, blk, prev, prev, nxt, nxt, taps, taps, vec, vec],
        out_specs=[blk, blk, taps, taps, vec, vec], out_shape=[act, act, wshape, wshape, vshape, vshape],
        scratch_shapes=[pltpu.VMEM((tb + 2 * FFN_HALO, tc), f32)] * 2 + [pltpu.VMEM((te, tc), f32)] * 2,
        compiler_params=_params("parallel", "arbitrary"))(df, df, upa, upv, upa, upv, upa, upv, wa, wv, ba, bv)


def _interleave(a):
    s, c = a.shape
    return a.reshape(SUBLANES, s // SUBLANES, c).transpose(1, 0, 2).reshape(s, c)


def _deinterleave(a):
    s, c = a.shape
    return a.reshape(s // SUBLANES, SUBLANES, c).transpose(1, 0, 2).reshape(s, c)


def _cols_from_shards(stacked):
    _, k, n = stacked.shape
    return stacked.transpose(1, 0, 2).reshape(k, NDEV * n)


def _cols_to_shards(full):
    k, n8 = full.shape
    return full.reshape(k, NDEV, n8 // NDEV).transpose(1, 0, 2)


def _pad_rows(w, rows):
    return jnp.pad(w, ((0, rows - w.shape[0]), (0, 0)))


def kernel(x, c, w_cond, b_cond, w_in, b_in, ssm_lambda_re, ssm_lambda_im, ssm_log_dt, ssm_b_re, ssm_b_im, ssm_c_re, ssm_c_im, ssm_d, ssm_glu_w_a, ssm_glu_w_b, cv_dw_w, cv_dw_b, cv_ln_g, cv_ln_b, cv_w_pw, w_out, ln1_g, ln1_b, ffn_w_up, ffn_dw_w, ffn_dw_b, ffn_w_down, ln2_g, ln2_b, loss_target, m_w_cond, m_b_cond, m_w_in, m_b_in, m_ssm_lambda_re, m_ssm_lambda_im, m_ssm_log_dt, m_ssm_b_re, m_ssm_b_im, m_ssm_c_re, m_ssm_c_im, m_ssm_d, m_ssm_glu_w_a, m_ssm_glu_w_b, m_cv_dw_w, m_cv_dw_b, m_cv_ln_g, m_cv_ln_b, m_cv_w_pw, m_w_out, m_ln1_g, m_ln1_b, m_ffn_w_up, m_ffn_dw_w, m_ffn_dw_b, m_ffn_w_down, m_ln2_g, m_ln2_b, v_w_cond, v_b_cond, v_w_in, v_b_in, v_ssm_lambda_re, v_ssm_lambda_im, v_ssm_log_dt, v_ssm_b_re, v_ssm_b_im, v_ssm_c_re, v_ssm_c_im, v_ssm_d, v_ssm_glu_w_a, v_ssm_glu_w_b, v_cv_dw_w, v_cv_dw_b, v_cv_ln_g, v_cv_ln_b, v_cv_w_pw, v_w_out, v_ln1_g, v_ln1_b, v_ffn_w_up, v_ffn_dw_w, v_ffn_dw_b, v_ffn_w_down, v_ln2_g, v_ln2_b):
    weights = dict(w_cond=w_cond, b_cond=b_cond, w_in=w_in, b_in=b_in, ssm_lambda_re=ssm_lambda_re, ssm_lambda_im=ssm_lambda_im, ssm_log_dt=ssm_log_dt, ssm_b_re=ssm_b_re, ssm_b_im=ssm_b_im, ssm_c_re=ssm_c_re, ssm_c_im=ssm_c_im, ssm_d=ssm_d, ssm_glu_w_a=ssm_glu_w_a, ssm_glu_w_b=ssm_glu_w_b, cv_dw_w=cv_dw_w, cv_dw_b=cv_dw_b, cv_ln_g=cv_ln_g, cv_ln_b=cv_ln_b, cv_w_pw=cv_w_pw, w_out=w_out, ln1_g=ln1_g, ln1_b=ln1_b, ffn_w_up=ffn_w_up, ffn_dw_w=ffn_dw_w, ffn_dw_b=ffn_dw_b, ffn_w_down=ffn_w_down, ln2_g=ln2_g, ln2_b=ln2_b)
    mom_m = dict(w_cond=m_w_cond, b_cond=m_b_cond, w_in=m_w_in, b_in=m_b_in, ssm_lambda_re=m_ssm_lambda_re, ssm_lambda_im=m_ssm_lambda_im, ssm_log_dt=m_ssm_log_dt, ssm_b_re=m_ssm_b_re, ssm_b_im=m_ssm_b_im, ssm_c_re=m_ssm_c_re, ssm_c_im=m_ssm_c_im, ssm_d=m_ssm_d, ssm_glu_w_a=m_ssm_glu_w_a, ssm_glu_w_b=m_ssm_glu_w_b, cv_dw_w=m_cv_dw_w, cv_dw_b=m_cv_dw_b, cv_ln_g=m_cv_ln_g, cv_ln_b=m_cv_ln_b, cv_w_pw=m_cv_w_pw, w_out=m_w_out, ln1_g=m_ln1_g, ln1_b=m_ln1_b, ffn_w_up=m_ffn_w_up, ffn_dw_w=m_ffn_dw_w, ffn_dw_b=m_ffn_dw_b, ffn_w_down=m_ffn_w_down, ln2_g=m_ln2_g, ln2_b=m_ln2_b)
    mom_v = dict(w_cond=v_w_cond, b_cond=v_b_cond, w_in=v_w_in, b_in=v_b_in, ssm_lambda_re=v_ssm_lambda_re, ssm_lambda_im=v_ssm_lambda_im, ssm_log_dt=v_ssm_log_dt, ssm_b_re=v_ssm_b_re, ssm_b_im=v_ssm_b_im, ssm_c_re=v_ssm_c_re, ssm_c_im=v_ssm_c_im, ssm_d=v_ssm_d, ssm_glu_w_a=v_ssm_glu_w_a, ssm_glu_w_b=v_ssm_glu_w_b, cv_dw_w=v_cv_dw_w, cv_dw_b=v_cv_dw_b, cv_ln_g=v_cv_ln_g, cv_ln_b=v_cv_ln_b, cv_w_pw=v_cv_w_pw, w_out=v_w_out, ln1_g=v_ln1_g, ln1_b=v_ln1_b, ffn_w_up=v_ffn_w_up, ffn_dw_w=v_ffn_dw_w, ffn_dw_b=v_ffn_dw_b, ffn_w_down=v_ffn_w_down, ln2_g=v_ln2_g, ln2_b=v_ln2_b)
    names = list(weights)

    s, d = x.shape[1], x.shape[2]
    sw = cw = d // 2
    fh = ffn_w_down.shape[1] * NDEV
    ng, nq = sw // SSM_GROUP, sw // QW
    gq = ng // nq
    alpha = 2.0 ** 0.25
    me = 4 * lax.axis_index("x") + 2 * lax.axis_index("y") + lax.axis_index("c")
    xs, tgt = x[0], loss_target[0]

    col_names = ["w_in", "ssm_glu_w_a", "ssm_glu_w_b", "cv_w_pw", "ffn_w_up"]
    row_names = ["w_out", "ffn_w_down"]
    big = col_names + row_names
    word_shapes = [(weights[n][0].size // 2,) for n in big]
    tap_shapes = [cv_dw_w[0, :, 0].shape, ffn_dw_w[0, :, 0].shape]
    payload = [_bf16_words(weights[n][0]) for n in big] + [c[0], cv_dw_w[0, :, 0], ffn_dw_w[0, :, 0]]
    got = _exchange(_pack(payload), scatter=False, name="gather_weights").reshape(NDEV, -1)
    parts = _unpack(got, word_shapes + [(d,)] + tap_shapes)
    wfull = {}
    for n, words in zip(big, parts):
        sh = weights[n][0].shape
        stacked = _words_bf16(words, (NDEV,) + sh)
        wfull[n] = _cols_from_shards(stacked) if n in col_names else stacked.reshape(NDEV * sh[0], sh[1])
    c_all = parts[len(big)]
    cv_taps = _cols_from_shards(parts[len(big) + 1])
    ffn_taps = _cols_from_shards(parts[len(big) + 2])
    o1, o2, o3, o4 = sw, sw + cw, sw + 2 * cw, sw + 2 * cw + d
    w_u, w_cva, w_cvg, w_gs, w_gc = (wfull["w_in"][:, a:b] for a, b in ((0, o1), (o1, o2), (o2, o3), (o3, o4), (o4, o4 + d)))
    b_u, b_cva, b_cvg, b_gs, b_gc = (b_in[:, a:b] for a, b in ((0, o1), (o1, o2), (o2, o3), (o3, o4), (o4, o4 + d)))
    w_upa, w_upv = wfull["ffn_w_up"][:, :fh], wfull["ffn_w_up"][:, fh:]
    w_a, w_b, w_pw, w_o, w_dn = wfull["ssm_glu_w_a"], wfull["ssm_glu_w_b"], wfull["cv_w_pw"], wfull["w_out"], wfull["ffn_w_down"]
    cv_w32 = _pad_rows(cv_taps, CONV_HALO)
    ffn_wa, ffn_wv = _pad_rows(ffn_taps[:, :fh], FFN_HALO), _pad_rows(ffn_taps[:, fh:], FFN_HALO)
    ffn_ba, ffn_bv = ffn_dw_b[:, :fh], ffn_dw_b[:, fh:]

    ncond = w_cond.shape[2]
    b_cond_mine = lax.dynamic_slice(b_cond, (0, me * ncond), (1, ncond))
    mod_cols = _cond_fwd(c_all, w_cond[0], b_cond_mine, name="cond_fwd")
    mod_all = _exchange(_pack([mod_cols]), scatter=False, name="gather_mod").reshape(NDEV, -1)[:, :NDEV * ncond]
    mod_mine = lax.dynamic_slice(mod_all.reshape(NDEV, NDEV, ncond), (0, me, 0), (NDEV, 1, ncond)).reshape(1, 6 * d)
    sh1, sc1, g1, sh2, sc2, g2 = (mod_mine[:, k * d:(k + 1) * d] for k in range(6))

    lam_re, lam_im, log_dt = ssm_lambda_re[0], ssm_lambda_im[0], ssm_log_dt[0][:, None]
    lbr, lbi, cfr, cfi = _ssm_prep(lam_re, lam_im, log_dt, name="ssm_prep")
    rows_q = lambda a: a.reshape(nq, 1, QS)
    eye = jnp.eye(gq, dtype=f32)

    def b_mat(b):
        bt = b.reshape(nq, gq, SSM_STATE, SSM_GROUP).transpose(0, 1, 3, 2)
        return jnp.einsum("qgpn,gh->qgphn", bt, eye).reshape(nq, QW, QS)

    def c_mat(cc):
        ct = cc.reshape(nq, gq, SSM_GROUP, SSM_STATE)
        return jnp.einsum("qgpn,gh->qhngp", ct, eye).reshape(nq, QS, QW)

    def b_unmat(mt):
        return jnp.einsum("qgpgn->qgnp", mt.reshape(nq, gq, SSM_GROUP, gq, SSM_STATE)).reshape(ng, SSM_STATE, SSM_GROUP)

    def c_unmat(mt):
        return jnp.einsum("qgngp->qgpn", mt.reshape(nq, gq, SSM_STATE, gq, SSM_GROUP)).reshape(ng, SSM_GROUP, SSM_STATE)

    ssm_args = (rows_q(lbr), rows_q(lbi), b_mat(ssm_b_re[0]), b_mat(ssm_b_im[0]), c_mat(ssm_c_re[0]), c_mat(ssm_c_im[0]),
                rows_q(cfr), rows_q(cfi), ssm_d[0].reshape(1, sw))

    h1 = _lnmod(xs, sc1, sh1, name="ln_mod1")
    u = _mm([(h1, w_u)], b_u, name="in_u")
    cva = _mm([(h1, w_cva)], b_cva, name="in_cva")
    cvg = _mm([(h1, w_cvg)], b_cvg, name="in_cvg")
    gs = _mm([(h1, w_gs)], b_gs, name="in_gs")
    gc = _mm([(h1, w_gc)], b_gc, name="in_gc")
    u_p = _interleave(u)
    h_p, yraw_p, y_p = _ssm_fwd(u_p, *ssm_args, name="ssm_fwd")
    y = _deinterleave(y_p)
    ya = _mm([(y, w_a)], name="glu_a")
    yb = _mm([(y, w_b)], name="glu_b")
    v4 = _conv_fwd(cva, cvg, cv_w32, cv_dw_b, cv_ln_g, cv_ln_b, name="conv_fwd")
    ycv = _mm([(v4, w_pw)], name="conv_pw")
    merged = _glu_merge(ya, yb, ycv, gs, gc, name="merge")
    o = _mm([(merged, w_o)], name="out_proj")
    x1, h2 = _resid_ln_mod(xs, o, g1, ln1_g, ln1_b, sc2, sh2, alpha, name="resid_ln1")
    upa = _mm([(h2, w_upa)], name="ffn_up_a")
    upv = _mm([(h2, w_upv)], name="ffn_up_v")
    f = _ffn_mid(upa, upv, ffn_wa, ffn_wv, ffn_ba, ffn_bv, name="ffn_mid")
    y2 = _mm([(f, w_dn)], name="ffn_down")
    dr2, dy2, loss_part, d_ln2_g, d_ln2_b, d_g2 = _resid_ln_loss(x1, y2, g2, ln2_g, ln2_b, tgt, alpha, name="resid_ln2_loss")

    gw = {}
    df = _mm([(dy2, w_dn)], trans_w=True, name="d_ffn_down")
    gw["ffn_w_down"] = _mm_tn(f, dy2, name="g_ffn_down")
    dupa, dupv, d_ffn_wa, d_ffn_wv, d_ffn_ba, d_ffn_bv = _ffn_mid_bwd(df, upa, upv, ffn_wa, ffn_wv, ffn_ba, ffn_bv, name="ffn_mid_bwd")
    dh2 = _mm([(dupa, w_upa), (dupv, w_upv)], trans_w=True, name="d_ffn_up")
    gw["ffn_w_up"] = jnp.concatenate([_mm_tn(h2, dupa, name="g_ffn_up_a"), _mm_tn(h2, dupv, name="g_ffn_up_v")], axis=1)
    dr1, do, d_sc2, d_sh2, d_ln1_g, d_ln1_b, d_g1 = _mid_bwd(dh2, x1, dr2, xs, o, g1, sc2, ln1_g, alpha, name="mid_bwd")
    dmerged = _mm([(do, w_o)], trans_w=True, name="d_out_proj")
    gw["w_out"] = _mm_tn(merged, do, name="g_out_proj")
    dya, dyb, dycv, dgs, dgc, s_gs, s_gc = _glu_merge_bwd(dmerged, ya, yb, ycv, gs, gc, name="merge_bwd")
    dy = _mm([(dya, w_a), (dyb, w_b)], trans_w=True, name="d_glu")
    gw["ssm_glu_w_a"] = _mm_tn(y, dya, name="g_glu_a")
    gw["ssm_glu_w_b"] = _mm_tn(y, dyb, name="g_glu_b")
    dv4 = _mm([(dycv, w_pw)], trans_w=True, name="d_conv_pw")
    gw["cv_w_pw"] = _mm_tn(v4, dycv, name="g_conv_pw")
    dv2, d_cv_ln_g, d_cv_ln_b = _conv_bwd_ln(dv4, cva, cvg, cv_w32, cv_dw_b, cv_ln_g, cv_ln_b, name="conv_bwd_ln")
    dcva, dcvg, d_cv_w32, d_cv_b, s_cva, s_cvg = _conv_bwd_taps(dv2, cva, cvg, cv_w32, name="conv_bwd_taps")
    (du_p, d_bre_m, d_bim_m, d_cre_m, d_cim_m, d_cfr, d_cfi, d_lbr, d_lbi, d_d, s_u) = _ssm_bwd(
        _interleave(dy), yraw_p, u_p, h_p, *ssm_args, name="ssm_bwd")
    du = _deinterleave(du_p)
    gshape = lam_re.shape
    d_lam_re, d_lam_im, d_log_dt = _ssm_prep_bwd(
        lam_re, lam_im, log_dt, [a.reshape(gshape) for a in (d_lbr, d_lbi, d_cfr, d_cfi)], name="ssm_prep_bwd")
    dh1 = _mm([(du, w_u), (dcva, w_cva), (dcvg, w_cvg), (dgs, w_gs), (dgc, w_gc)], trans_w=True, name="d_in")
    gw["w_in"] = jnp.concatenate([_mm_tn(h1, t, name="g_in_" + nm) for nm, t in
                                  (("u", du), ("cva", dcva), ("cvg", dcvg), ("gs", dgs), ("gc", dgc))], axis=1)
    grad_x, d_sc1, d_sh1 = _final_bwd(dh1, xs, dr1, sc1, alpha, name="final_bwd")

    per_dev = [(_cols_to_shards(gw[n]) if n in col_names else gw[n].reshape((NDEV,) + weights[n][0].shape)).reshape(NDEV, -1)
               for n in big]
    flat = jnp.concatenate(per_dev, axis=1)
    pad = (-flat.shape[1]) % (SUM_ROWS * LANES)
    contrib =jnp.pad(flat, ((0, 0), (0, pad))).reshape(NDEV, -1, LANES)
    received = _exchange(contrib, scatter=True, name="scatter_grads")
    gsum = _sum_parts(received, name="sum_grads").reshape(-1)
    grads = dict(zip(big, _unpack(gsum, [weights[n][0].shape for n in big])))

    dmod = jnp.concatenate([d_sh1, d_sc1, d_g1, d_sh2, d_sc2, d_g2], axis=1)
    small = {
        "b_in": jnp.concatenate([s_u, s_cva, s_cvg, s_gs, s_gc], axis=1),
        "ssm_lambda_re": d_lam_re, "ssm_lambda_im": d_lam_im, "ssm_log_dt": d_log_dt,
        "ssm_b_re": b_unmat(d_bre_m), "ssm_b_im": b_unmat(d_bim_m), "ssm_c_re": c_unmat(d_cre_m), "ssm_c_im": c_unmat(d_cim_m),
        "ssm_d": d_d, "cv_dw_w": d_cv_w32[:CONV_K], "cv_dw_b": d_cv_b, "cv_ln_g": d_cv_ln_g, "cv_ln_b": d_cv_ln_b,
        "ln1_g": d_ln1_g, "ln1_b": d_ln1_b,
        "ffn_dw_w": jnp.concatenate([d_ffn_wa[:FFN_K], d_ffn_wv[:FFN_K]], axis=1),
        "ffn_dw_b": jnp.concatenate([d_ffn_ba, d_ffn_bv], axis=1), "ln2_g": d_ln2_g, "ln2_b": d_ln2_b,
        "b_cond": dmod, "loss": loss_part,
    }
    small_names = list(small)
    small_shapes = [small[n].shape for n in small_names]
    small_all = _exchange(_pack([small[n] for n in small_names]), scatter=False, name="gather_small")
    small_sum = dict(zip(small_names, _unpack(_sum_parts(small_all, name="sum_small").reshape(-1), small_shapes)))
    dmod_all = _unpack(small_all.reshape(NDEV, -1), small_shapes)[small_names.index("b_cond")].reshape(NDEV, 6 * d)
    dmod_cols = lax.dynamic_slice(dmod_all.reshape(NDEV, NDEV, ncond), (0, me, 0), (NDEV, 1, ncond)).reshape(NDEV, ncond)
    grads["w_cond"] = _cond_bwd(c_all, dmod_cols, name="cond_bwd")
    loss = small_sum.pop("loss").reshape(())
    for n, g in small_sum.items():
        grads[n] = g
    ntap = cv_dw_w.shape[3]
    grads["cv_dw_w"] = lax.dynamic_slice(grads["cv_dw_w"], (0, me * ntap), (CONV_K, ntap))
    nffn = ffn_dw_w.shape[3]
    grads["ffn_dw_w"] = lax.dynamic_slice(grads["ffn_dw_w"], (0, me * nffn), (FFN_K, nffn))
    grads = {n: grads[n].reshape(weights[n].shape) for n in names}

    delta, new_m, new_v = {}, {}, {}
    mats = ["w_cond"] + big
    for n in mats:
        sh = weights[n].shape
        as2d = lambda a: a.reshape(sh[1], sh[2])
        dl, nm, nv = _adamw(as2d(weights[n]), as2d(grads[n]), as2d(mom_m[n]), as2d(mom_v[n]), name="adamw_" + n)
        delta[n], new_m[n], new_v[n] = dl.reshape(sh), nm.reshape(sh), nv.reshape(sh)
    rest = [n for n in names if n not in mats]
    rest_shapes = [weights[n].shape for n in rest]
    packed = [_pack([t[n] for n in rest]) for t in (weights, grads, mom_m, mom_v)]
    for tgt_dict, res in zip((delta, new_m, new_v), _adamw(*packed, name="adamw_small")):
        for n, a in zip(rest, _unpack(res.reshape(-1), rest_shapes)):
            tgt_dict[n] = a

    return (loss, grad_x[None], *[grads[n] for n in names], *[delta[n] for n in names],
            *[new_m[n] for n in names], *[new_v[n] for n in names])
```

```python
import functools
import math

import jax
import jax.numpy as jnp
from jax import lax
from jax.experimental import pallas as pl
from jax.experimental.pallas import tpu as pltpu

f32 = jnp.float32
bf16 = jnp.bfloat16

NDEV = 8
LANES = 128
SUBLANES = 8
SSM_GROUP = 16
SSM_STATE = 64
QW = 128
QS = 512
CONV_K = 31
CONV_HALO = 32
FFN_K = 3
FFN_HALO = 8
LN_EPS = 1e-5
ADAM_LR, ADAM_B1, ADAM_B2, ADAM_EPS, ADAM_WD, ADAM_STEP = 0.001, 0.9, 0.999, 1e-08, 0.01, 10
VMEM_LIMIT = 56 * 1024 * 1024
W_TILE_BYTES = 6 * 1024 * 1024
SUM_ROWS = 512
EW_BLOCK_BYTES = 2 * 1024 * 1024
INV_SQRT2 = 1.0 / math.sqrt(2.0)
INV_SQRT_2PI = 1.0 / math.sqrt(2.0 * math.pi)
MESH = pl.DeviceIdType.MESH


def _tile(n, want):
    t = min(n, want)
    while n % t:
        t //= 2
    return t


def _col_tile(n, rows, bytes_per):
    best = LANES if n % LANES == 0 else n
    for t in range(LANES, n + 1, LANES):
        if n % t == 0 and rows * t * bytes_per <= W_TILE_BYTES:
            best = t
    return best


def _params(*sem):
    return pltpu.CompilerParams(dimension_semantics=sem, vmem_limit_bytes=VMEM_LIMIT)


def _row(i):
    return (0, 0)


def _full(shape):
    nd = len(shape)
    return pl.BlockSpec(shape, lambda *a: (0,) * nd)


def _ln(x):
    mu = jnp.mean(x, axis=-1, keepdims=True)
    xc = x - mu
    var = jnp.mean(xc * xc, axis=-1, keepdims=True)
    rstd = lax.rsqrt(var + LN_EPS)
    return xc * rstd, rstd


def _ln_bwd(dxhat, xhat, rstd):
    return rstd * (dxhat - jnp.mean(dxhat, axis=-1, keepdims=True) - xhat * jnp.mean(dxhat * xhat, axis=-1, keepdims=True))


def _sig(x):
    return 1.0 / (1.0 + jnp.exp(-x))


def _gelu(x):
    return 0.5 * x * (1.0 + lax.erf(x * INV_SQRT2))


def _gelu_grad(x):
    return 0.5 * (1.0 + lax.erf(x * INV_SQRT2)) + x * jnp.exp(-0.5 * x * x) * INV_SQRT_2PI


def _colsum(x):
    return jnp.sum(x, axis=0, keepdims=True)


def _mm(pairs, bias=None, *, trans_w=False, out_dtype=f32, name):
    n_p = len(pairs)
    m = pairs[0][0].shape[0]
    n = pairs[0][1].shape[0 if trans_w else 1]
    ktot = sum(x.shape[1] for x, _ in pairs)
    tm = _tile(m, 512)
    tn = _col_tile(n, ktot, 2)
    dn = (((1,), (1,)), ((), ())) if trans_w else (((1,), (0,)), ((), ()))

    def body(*refs):
        o_ref = refs[-1]
        acc = None
        for xr, wr in zip(refs[:n_p], refs[n_p:2 * n_p]):
            r = lax.dot_general(xr[...].astype(bf16), wr[...].astype(bf16), dn, preferred_element_type=f32)
            acc = r if acc is None else acc + r
        if bias is not None:
            acc = acc + refs[2 * n_p][...]
        o_ref[...] = acc.astype(out_dtype)

    in_specs = [pl.BlockSpec((tm, x.shape[1]), lambda j, i: (i, 0)) for x, _ in pairs]
    if trans_w:
        in_specs += [pl.BlockSpec((tn, w.shape[1]), lambda j, i: (j, 0)) for _, w in pairs]
    else:
        in_specs += [pl.BlockSpec((w.shape[0], tn), lambda j, i: (0, j)) for _, w in pairs]
    args = [x for x, _ in pairs] + [w for _, w in pairs]
    if bias is not None:
        in_specs.append(pl.BlockSpec((1, tn), lambda j, i: (0, j)))
        args.append(bias)
    return pl.pallas_call(
        body, name=name, grid=(n // tn, m // tm), in_specs=in_specs,
        out_specs=pl.BlockSpec((tm, tn), lambda j, i: (i, j)),
        out_shape=jax.ShapeDtypeStruct((m, n), out_dtype),
        compiler_params=_params("parallel", "arbitrary"),
    )(*args)


def _mm_tn(x, dy, *, name):
    m, k = x.shape
    n = dy.shape[1]
    tm = _tile(m, 512)
    tn = _col_tile(n, k, 4)

    def body(x_ref, dy_ref, o_ref):
        @pl.when(pl.program_id(1) == 0)
        def _():
            o_ref[...] = jnp.zeros_like(o_ref)

        o_ref[...] += lax.dot_general(x_ref[...].astype(bf16), dy_ref[...].astype(bf16), (((0,), (0,)), ((), ())),
                                      preferred_element_type=f32)

    return pl.pallas_call(
        body, name=name, grid=(n // tn, m // tm),
        in_specs=[pl.BlockSpec((tm, k), lambda j, i: (i, 0)), pl.BlockSpec((tm, tn), lambda j, i: (i, j))],
        out_specs=pl.BlockSpec((k, tn), lambda j, i: (0, j)),
        out_shape=jax.ShapeDtypeStruct((k, n), f32),
        compiler_params=_params("parallel", "arbitrary"),
    )(x, dy)


def _exchange(arrs, *, scatter, name):
    n = len(arrs)

    def body(*refs):
        x_refs, o_refs = refs[:n], refs[n:2 * n]
        send_sems, recv_sems, local_sems = refs[2 * n:]
        ix, iy, ic = lax.axis_index("x"), lax.axis_index("y"), lax.axis_index("c")
        me = 4 * ix + 2 * iy + ic

        def peer(k):
            return (1 - ix if k & 4 else ix, 1 - iy if k & 2 else iy, 1 - ic if k & 1 else ic)

        local = [pltpu.make_async_copy(x.at[me] if scatter else x, o.at[me], local_sems.at[a])
                 for a, (x, o) in enumerate(zip(x_refs, o_refs))]
        for cp in local:
            cp.start()
        sends = []
        for k in range(1, NDEV):
            px, py, pc = peer(k)
            for a, (x, o) in enumerate(zip(x_refs, o_refs)):
                sem = (k - 1) * n + a
                cp = pltpu.make_async_remote_copy(
                    src_ref=x.at[4 * px + 2 * py + pc] if scatter else x, dst_ref=o.at[me], send_sem=send_sems.at[sem],
                    recv_sem=recv_sems.at[sem], device_id=(px, py, pc), device_id_type=MESH)
                cp.start()
                sends.append(cp)
        for k in range(1, NDEV):
            px, py, pc = peer(k)
            for a, (x, o) in enumerate(zip(x_refs, o_refs)):
                sem = (k - 1) * n + a
                pltpu.make_async_remote_copy(
                    src_ref=x.at[me] if scatter else x, dst_ref=o.at[4 * px + 2 * py + pc], send_sem=send_sems.at[sem],
                    recv_sem=recv_sems.at[sem], device_id=(px, py, pc), device_id_type=MESH).wait_recv()
        for cp in sends:
            cp.wait_send()
        for cp in local:
            cp.wait()

    hbm = pl.BlockSpec(memory_space=pltpu.HBM)
    return pl.pallas_call(
        body, name=name, in_specs=[hbm] * n, out_specs=[hbm] * n,
        out_shape=[jax.ShapeDtypeStruct(a.shape if scatter else (NDEV,) + a.shape, a.dtype) for a in arrs],
        scratch_shapes=[pltpu.SemaphoreType.DMA(((NDEV - 1) * n,)), pltpu.SemaphoreType.DMA(((NDEV - 1) * n,)),
                        pltpu.SemaphoreType.DMA((n,))],
    )(*arrs)


def _sum_parts(parts, *, name):
    r = parts.shape[1]

    def body(p_ref, o_ref):
        acc = p_ref[0]
        for j in range(1, NDEV):
            acc = acc + p_ref[j]
        o_ref[...] = acc

    return pl.pallas_call(body, name=name, out_shape=jax.ShapeDtypeStruct((r, LANES), f32), compiler_params=_params())(parts)


def _col_pieces(n, bounds):
    out = []
    for p, (a, b) in enumerate(bounds):
        for j in range(NDEV):
            lo, hi = max(a, n * j), min(b, n * (j + 1))
            if lo < hi:
                out.append((p, j, lo - a, lo - n * j, hi - lo))
    return out


def _unshard_cols(stacked, bounds, *, name):
    _, k, n = stacked.shape
    tk = _tile(k, 256)
    plan = _col_pieces(n, bounds)

    def body(x_ref, *o_refs):
        for p, j, po, so, w in plan:
            o_refs[p][:, po:po + w] = x_ref[j, :, so:so + w]

    return pl.pallas_call(
        body, name=name, grid=(k // tk,), in_specs=[pl.BlockSpec((NDEV, tk, n), lambda i: (0, i, 0))],
        out_specs=[pl.BlockSpec((tk, b - a), lambda i: (i, 0)) for a, b in bounds],
        out_shape=[jax.ShapeDtypeStruct((k, b - a), stacked.dtype) for a, b in bounds],
        compiler_params=_params("parallel"))(stacked)


def _shard_cols(pieces, *, name):
    k = pieces[0].shape[0]
    bounds, off = [], 0
    for p in pieces:
        bounds.append((off, off + p.shape[1]))
        off += p.shape[1]
    n = off // NDEV
    tk = _tile(k, 256)
    plan = _col_pieces(n, bounds)

    def body(*refs):
        o_ref = refs[-1]
        for p, j, po, so, w in plan:
            o_ref[j, :, so:so + w] = refs[p][:, po:po + w]

    return pl.pallas_call(
        body, name=name, grid=(k // tk,), in_specs=[pl.BlockSpec((tk, b - a), lambda i: (i, 0)) for a, b in bounds],
        out_specs=pl.BlockSpec((NDEV, tk, n), lambda i: (0, i, 0)),
        out_shape=jax.ShapeDtypeStruct((NDEV, k, n), pieces[0].dtype),
        compiler_params=_params("parallel"))(*pieces)


def _pack(arrs):
    flat = jnp.concatenate([a.reshape(-1) for a in arrs])
    pad = (-flat.shape[0]) % (SUBLANES * LANES)
    return jnp.pad(flat, (0, pad)).reshape(-1, LANES)


def _unpack(flat, shapes):
    out, off = [], 0
    for s in shapes:
        n = math.prod(s)
        out.append(flat[..., off:off + n].reshape(flat.shape[:-1] + tuple(s)))
        off += n
    return out


def _adamw_math(w, gg, m, v):
    nm = ADAM_B1 * m + (1.0 - ADAM_B1) * gg
    nv = ADAM_B2 * v + (1.0 - ADAM_B2) * (gg * gg)
    m_hat = nm / (1.0 - ADAM_B1 ** ADAM_STEP)
    v_hat = nv / (1.0 - ADAM_B2 ** ADAM_STEP)
    return -ADAM_LR * (m_hat / (jnp.sqrt(v_hat) + ADAM_EPS) + ADAM_WD * w), nm, nv


def _row_block(r, c, copies):
    tr = r
    while copies * tr * c * 4 > EW_BLOCK_BYTES and tr % (2 * SUBLANES) == 0:
        tr //= 2
    return tr


def _adamw(w, g, m, v, *, name):
    r, c = w.shape
    tr = _row_block(r, c, 1)

    def body(w_ref, g_ref, m_ref, v_ref, d_ref, nm_ref, nv_ref):
        d_ref[...], nm_ref[...], nv_ref[...] = _adamw_math(w_ref[...], g_ref[...], m_ref[...], v_ref[...])

    spec = pl.BlockSpec((tr, c), lambda i: (i, 0))
    shp = jax.ShapeDtypeStruct((r, c), f32)
    return pl.pallas_call(
        body, name=name, grid=(r // tr,), in_specs=[spec] * 4, out_specs=[spec] * 3, out_shape=[shp] * 3,
        compiler_params=_params("parallel"),
    )(w, g, m, v)


def _sum_adamw(parts, w, m, v, *, name):
    r, c = w.shape
    tr = _row_block(r, c, NDEV)

    def body(p_ref, w_ref, m_ref, v_ref, g_ref, d_ref, nm_ref, nv_ref):
        gg = p_ref[0]
        for j in range(1, NDEV):
            gg = gg + p_ref[j]
        g_ref[...] = gg
        d_ref[...], nm_ref[...], nv_ref[...] = _adamw_math(w_ref[...], gg, m_ref[...], v_ref[...])

    spec = pl.BlockSpec((tr, c), lambda i: (i, 0))
    shp = jax.ShapeDtypeStruct((r, c), f32)
    return pl.pallas_call(
        body, name=name, grid=(r // tr,), in_specs=[pl.BlockSpec((NDEV, tr, c), lambda i: (0, i, 0))] + [spec] * 3,
        out_specs=[spec] * 4, out_shape=[shp] * 4, compiler_params=_params("parallel"),
    )(parts, w, m, v)


def _cond_fwd(c_all, w, b, *, name):
    nb, n = c_all.shape[0], w.shape[1]

    def body(c_ref, w_ref, b_ref, o_ref):
        cc = c_ref[...]
        o_ref[...] = jnp.dot(cc * _sig(cc), w_ref[...], preferred_element_type=f32,
                             precision=lax.Precision.HIGHEST) + b_ref[...]

    return pl.pallas_call(body, name=name, out_shape=jax.ShapeDtypeStruct((nb, n), f32),
                          compiler_params=_params())(c_all, w, b)


def _cond_bwd(c_all, dmod, *, name):
    d, n = c_all.shape[1], dmod.shape[1]

    def body(c_ref, g_ref, o_ref):
        cc = c_ref[...]
        o_ref[...] = lax.dot_general(cc * _sig(cc), g_ref[...], (((0,), (0,)), ((), ())), preferred_element_type=f32,
                                     precision=lax.Precision.HIGHEST)

    return pl.pallas_call(body, name=name, out_shape=jax.ShapeDtypeStruct((d, n), f32),
                          compiler_params=_params())(c_all, dmod)


def _ssm_disc(lam_re, lam_im, log_dt):
    lr = jnp.minimum(lam_re, -1e-4)
    li = lam_im
    dt = jnp.exp(log_dt)
    mag = jnp.exp(lr * dt)
    ang = li * dt
    lbr, lbi = mag * jnp.cos(ang), mag * jnp.sin(ang)
    num_r, num_i = lbr - 1.0, lbi
    den = lr * lr + li * li
    return lbr, lbi, (num_r * lr + num_i * li) / den, (num_i * lr - num_r * li) / den


def _ssm_prep(lam_re, lam_im, log_dt, *, name):
    def body(a, b, c, o1, o2, o3, o4):
        o1[...], o2[...], o3[...], o4[...] = _ssm_disc(a[...], b[...], c[...])

    shp = jax.ShapeDtypeStruct(lam_re.shape, f32)
    return pl.pallas_call(body, name=name, out_shape=[shp] * 4, compiler_params=_params())(lam_re, lam_im, log_dt)


def _ssm_prep_bwd(lam_re, lam_im, log_dt, cts, *, name):
    def body(a, b, c, g1, g2, g3, g4, o1, o2, o3):
        _, vjp = jax.vjp(_ssm_disc, a[...], b[...], c[...])
        o1[...], o2[...], o3[...] = vjp((g1[...], g2[...], g3[...], g4[...]))

    shp = jax.ShapeDtypeStruct(lam_re.shape, f32)
    return pl.pallas_call(body, name=name, out_shape=[shp, shp, jax.ShapeDtypeStruct(log_dt.shape, f32)],
                          compiler_params=_params())(lam_re, lam_im, log_dt, *cts)


QT = 2 * QS // LANES
HT = QS // LANES


def _tiles_store(buf, q, val):
    for ct in range(QT):
        buf[q * QT + ct] = val[:, ct * LANES:(ct + 1) * LANES]


def _tiles_load(buf, q):
    return jnp.concatenate([buf[q * QT + ct] for ct in range(QT)], axis=1)


def _lane_tiles(row_ref_value):
    return [jnp.broadcast_to(row_ref_value[:, ct * LANES:(ct + 1) * LANES], (SUBLANES, LANES)) for ct in range(HT)]


def _carry_tiles(ref, o):
    return tuple(ref[:, o + ct * LANES:o + (ct + 1) * LANES] for ct in range(QT))


def _carry_store(ref, o, tiles):
    for ct in range(QT):
        ref[:, o + ct * LANES:o + (ct + 1) * LANES] = tiles[ct]
def _chain_carries(loc_r, loc_i, pr, pi_, forward):
    row = lax.broadcasted_iota(jnp.int32, loc_r.shape, 0)
    shift = 1 if forward else SUBLANES - 1
    order = range(1, SUBLANES) if forward else range(SUBLANES - 2, -1, -1)
    er, ei = loc_r, loc_i
    for k in order:
        sr, si = pltpu.roll(er, shift, 0), pltpu.roll(ei, shift, 0)
        er = jnp.where(row == k, loc_r + pr * sr - pi_ * si, er)
        ei = jnp.where(row == k, loc_i + pr * si + pi_ * sr, ei)
    edge = 0 if forward else SUBLANES - 1
    return (jnp.where(row == edge, 0.0, pltpu.roll(er, shift, 0)), jnp.where(row == edge, 0.0, pltpu.roll(ei, shift, 0)))


def _chunk_power(ar, ai, chunk_len):
    pr, pi_ = ar, ai
    for _ in range(int(math.log2(chunk_len))):
        pr, pi_ = pr * pr - pi_ * pi_, 2.0 * pr * pi_
    return pr, pi_


def _ssm_mats(bre_ref, bim_ref, cre_ref, cim_ref, cfr_ref, cfi_ref, bbar_s, cmat_s, nq):
    for q in range(nq):
        cr, ci, br, bi = cfr_ref[q], cfi_ref[q], bre_ref[q], bim_ref[q]
        bbar_s[q, :, 0:QS] = (cr * br - ci * bi).astype(bf16)
        bbar_s[q, :, QS:2 * QS] = (cr * bi + ci * br).astype(bf16)
        cmat_s[q, 0:QS, :] = cre_ref[q].astype(bf16)
        cmat_s[q, QS:2 * QS, :] = (-cim_ref[q]).astype(bf16)


def _ssm_fwd(u, ar, ai, bre, bim, cre, cim, cfr, cfi, dvec, *, name):
    s, sw = u.shape
    nq = sw // QW
    st = nq * 2 * QS
    tb = _tile(s, 256)
    nb, nt, chunk_len = s // tb, tb // SUBLANES, s // SUBLANES
    assert chunk_len & (chunk_len - 1) == 0 and nt % 16 == 0

    def body(u_ref, ar_ref, ai_ref, bre_ref, bim_ref, cre_ref, cim_ref, cfr_ref, cfi_ref, d_ref,
             h_out, yraw_out, y_out, buf, hc, bbar_s, cmat_s):
        ph, i = pl.program_id(0), pl.program_id(1)

        @pl.when(i == 0)
        def _():
            _ssm_mats(bre_ref, bim_ref, cre_ref, cim_ref, cfr_ref, cfi_ref, bbar_s, cmat_s, nq)

        @pl.when((ph == 0) & (i == 0))
        def _():
            hc[...] = jnp.zeros_like(hc)

        @pl.when((ph == 1) & (i == 0))
        def _():
            for q in range(nq):
                o = q * 2 * QS
                pr, pi_ = _chunk_power(ar_ref[q], ai_ref[q], chunk_len)
                sr, si = _chain_carries(hc[:, o:o + QS], hc[:, o + QS:o + 2 * QS], pr, pi_, True)
                hc[:, o:o + QS] = sr
                hc[:, o + QS:o + 2 * QS] = si

        uu = u_ref[...].reshape(tb, sw)
        for q in range(nq):
            _tiles_store(buf, q, jnp.dot(uu[:, q * QW:(q + 1) * QW].astype(bf16), bbar_s[q], preferred_element_type=f32))

        for q in range(nq):
            o = q * 2 * QS
            a_r, a_i = _lane_tiles(ar_ref[q]), _lane_tiles(ai_ref[q])

            def step(t, carry, q=q, a_r=a_r, a_i=a_i):
                rows = pl.ds(t, SUBLANES, stride=nt)
                nr, ni = [], []
                for ct in range(HT):
                    hr, hi = carry[ct], carry[HT + ct]
                    re = a_r[ct] * hr - a_i[ct] * hi + buf[q * QT + ct, rows, :]
                    im = a_r[ct] * hi + a_i[ct] * hr + buf[q * QT + HT + ct, rows, :]
                    buf[q * QT + ct, rows, :] = re
                    buf[q * QT + HT + ct, rows, :] = im
                    nr.append(re)
                    ni.append(im)
                return tuple(nr + ni)

            _carry_store(hc, o, lax.fori_loop(0, nt, step, _carry_tiles(hc, o)))

        @pl.when(ph == 1)
        def _():
            for q in range(nq):
                o = q * 2 * QS
                cs = slice(q * QW, (q + 1) * QW)
                hq = _tiles_load(buf, q).astype(bf16)
                h_out[:, :, o:o + 2 * QS] = hq.reshape(SUBLANES, nt, 2 * QS)
                yq = jnp.dot(hq, cmat_s[q], preferred_element_type=f32) + d_ref[:, cs] * uu[:, cs]
                yraw_out[:, :, cs] = yq.reshape(SUBLANES, nt, QW)
                y_out[:, :, cs] = _gelu(yq).astype(bf16).reshape(SUBLANES, nt, QW)

    blk = lambda ph, i: (0, i, 0)
    oblk = lambda ph, i: (0, i * ph, 0)
    act = lambda c, dt: jax.ShapeDtypeStruct((SUBLANES, chunk_len, c), dt)
    h3, yraw3, y3 = pl.pallas_call(
        body, name=name, grid=(2, nb),
        in_specs=[pl.BlockSpec((SUBLANES, nt, sw), blk), _full(ar.shape), _full(ai.shape), _full(bre.shape), _full(bim.shape),
                  _full(cre.shape), _full(cim.shape), _full(cfr.shape), _full(cfi.shape), _full(dvec.shape)],
        out_specs=[pl.BlockSpec((SUBLANES, nt, st), oblk), pl.BlockSpec((SUBLANES, nt, sw), oblk),
                   pl.BlockSpec((SUBLANES, nt, sw), oblk)],
        out_shape=[act(st, bf16), act(sw, f32), act(sw, bf16)],
        scratch_shapes=[pltpu.VMEM((st // LANES, tb, LANES), f32), pltpu.VMEM((SUBLANES, st), f32),
                        pltpu.VMEM((nq, QW, 2 * QS), bf16), pltpu.VMEM((nq, 2 * QS, QW), bf16)],
        compiler_params=_params("arbitrary", "arbitrary"),
    )(u.reshape(SUBLANES, chunk_len, sw), ar, ai, bre, bim, cre, cim, cfr, cfi, dvec)
    return h3, yraw3, y3.reshape(s, sw)


def _ssm_bwd(dy, yraw3, u, h3, ar, ai, bre, bim, cre, cim, cfr, cfi, dvec, *, name):
    s, sw = u.shape
    nq = sw // QW
    st = nq * 2 * QS
    tb = _tile(s, 256)
    nb, nt, chunk_len = s // tb, tb // SUBLANES, s // SUBLANES

    def body(dy_ref, yraw_ref, u_ref, h_ref, ar_ref, ai_ref, bre_ref, bim_ref, cre_ref, cim_ref, cfr_ref, cfi_ref, d_ref,
             du_out, dbre_out, dbim_out, dcre_out, dcim_out, dcfr_out, dcfi_out, dlbr_out, dlbi_out, dd_out, dbu_out,
             buf, hf, rc, acc, dbbar, dcmat, bbar_s, cmat_s):
        ph, i = pl.program_id(0), pl.program_id(1)

        @pl.when(i == 0)
        def _():
            _ssm_mats(bre_ref, bim_ref, cre_ref, cim_ref, cfr_ref, cfi_ref, bbar_s, cmat_s, nq)

        @pl.when((ph == 0) & (i == 0))
        def _():
            rc[...] = jnp.zeros_like(rc)

        @pl.when((ph == 1) & (i == 0))
        def _():
            for q in range(nq):
                o = q * 2 * QS
                pr, pi_ = _chunk_power(ar_ref[q], ai_ref[q], chunk_len)
                sr, si = _chain_carries(rc[:, o:o + QS], rc[:, o + QS:o + 2 * QS], pr, -pi_, False)
                rc[:, o:o + QS] = sr
                rc[:, o + QS:o + 2 * QS] = si
            acc[...] = jnp.zeros_like(acc)
            dbbar[...] = jnp.zeros_like(dbbar)
            dcmat[...] = jnp.zeros_like(dcmat)
            dd_out[...] = jnp.zeros_like(dd_out)
            dbu_out[...] = jnp.zeros_like(dbu_out)

        dyraw = (dy_ref[...] * _gelu_grad(yraw_ref[...])).reshape(tb, sw)
        for q in range(nq):
            _tiles_store(buf, q, lax.dot_general(dyraw[:, q * QW:(q + 1) * QW].astype(bf16), cmat_s[q],
                                                 (((1,), (1,)), ((), ())), preferred_element_type=f32))

        def recur(with_grad):
            for q in range(nq):
                o = q * 2 * QS
                a_r, a_i = _lane_tiles(ar_ref[q]), _lane_tiles(ai_ref[q])

                def step(j, carry, q=q, a_r=a_r, a_i=a_i):
                    rows = pl.ds(nt - 1 - j, SUBLANES, stride=nt)
                    nr, ni, ngr, ngi = [], [], [], []
                    for ct in range(HT):
                        rr, ri = carry[ct], carry[HT + ct]
                        if with_grad:
                            hr, hi = hf[q * QT + ct, rows, :], hf[q * QT + HT + ct, rows, :]
                            ngr.append(carry[QT + ct] + hr * rr + hi * ri)
                            ngi.append(carry[QT + HT + ct] + hr * ri - hi * rr)
                        re = buf[q * QT + ct, rows, :] + a_r[ct] * rr + a_i[ct] * ri
                        im = buf[q * QT + HT + ct, rows, :] + a_r[ct] * ri - a_i[ct] * rr
                        buf[q * QT + ct, rows, :] = re
                        buf[q * QT + HT + ct, rows, :] = im
                        nr.append(re)
                        ni.append(im)
                    return tuple(nr + ni + ngr + ngi)

                init = _carry_tiles(rc, o) + (_carry_tiles(acc, o) if with_grad else ())
                res = lax.fori_loop(0, nt, step, init)
                _carry_store(rc, o, res[:QT])
                if with_grad:
                    _carry_store(acc, o, res[QT:])

        @pl.when(ph == 0)
        def _():
            recur(False)

        @pl.when(ph == 1)
        def _():
            hblk = h_ref[...].reshape(tb, st)
            for ct in range(st // LANES):
                hf[ct] = hblk[:, ct * LANES:(ct + 1) * LANES].astype(f32)
            recur(True)
            uu = u_ref[...].reshape(tb, sw)
            dd_out[...] += _colsum(dyraw * uu)
            for q in range(nq):
                o = q * 2 * QS
                cs = slice(q * QW, (q + 1) * QW)
                lam = _tiles_load(buf, q).astype(bf16)
                duq = lax.dot_general(lam, bbar_s[q], (((1,), (1,)), ((), ())), preferred_element_type=f32) \
                    + d_ref[:, cs] * dyraw[:, cs]
                du_out[:, :, cs] = duq.astype(bf16).reshape(SUBLANES, nt, QW)
                dbu_out[:, cs] += _colsum(duq)
                dbbar[q] += lax.dot_general(uu[:, cs].astype(bf16), lam, (((0,), (0,)), ((), ())), preferred_element_type=f32)
                dcmat[q] += lax.dot_general(hblk[:, o:o + 2 * QS], dyraw[:, cs].astype(bf16), (((0,), (0,)), ((), ())),
                                            preferred_element_type=f32)

        @pl.when((ph == 1) & (i == nb - 1))
        def _():
            for q in range(nq):
                o = q * 2 * QS
                cr, ci, br, bi = cfr_ref[q], cfi_ref[q], bre_ref[q], bim_ref[q]
                gr, gi = dbbar[q, :, 0:QS], dbbar[q, :, QS:2 * QS]
                dbre_out[q] = cr * gr + ci * gi
                dbim_out[q] = cr * gi - ci * gr
                dcfr_out[q] = _colsum(gr * br + gi * bi)
                dcfi_out[q] = _colsum(gi * br - gr * bi)
                dcre_out[q] = dcmat[q, 0:QS, :]
                dcim_out[q] = -dcmat[q, QS:2 * QS, :]
                dlbr_out[q] = _colsum(acc[:, o:o + QS])
                dlbi_out[q] = _colsum(acc[:, o + QS:o + 2 * QS])

    blk = lambda ph, i: (0, nb - 1 - i, 0)
    oblk = lambda ph, i: (0, (nb - 1 - i) * ph + (nb - 1) * (1 - ph), 0)
    pshapes = [ar.shape, ai.shape, bre.shape, bim.shape, cre.shape, cim.shape, cfr.shape, cfi.shape, dvec.shape]
    oshapes = [bre.shape, bim.shape, cre.shape, cim.shape, cfr.shape, cfi.shape, ar.shape, ai.shape, dvec.shape, dvec.shape]
    act = pl.BlockSpec((SUBLANES, nt, sw), blk)
    view = lambda a: a.reshape(SUBLANES, chunk_len, sw)
    res = pl.pallas_call(
        body, name=name, grid=(2, nb),
        in_specs=[act, act, act, pl.BlockSpec((SUBLANES, nt, st), blk)] + [_full(p) for p in pshapes],
        out_specs=[pl.BlockSpec((SUBLANES, nt, sw), oblk)] + [_full(p) for p in oshapes],
        out_shape=[jax.ShapeDtypeStruct((SUBLANES, chunk_len, sw), bf16)] + [jax.ShapeDtypeStruct(p, f32) for p in oshapes],
        scratch_shapes=[pltpu.VMEM((st // LANES, tb, LANES), f32), pltpu.VMEM((st // LANES, tb, LANES), f32),
                        pltpu.VMEM((SUBLANES, st), f32), pltpu.VMEM((SUBLANES, st), f32),
                        pltpu.VMEM((nq, QW, 2 * QS), f32), pltpu.VMEM((nq, 2 * QS, QW), f32),
                        pltpu.VMEM((nq, QW, 2 * QS), bf16), pltpu.VMEM((nq, 2 * QS, QW), bf16)],
        compiler_params=_params("arbitrary", "arbitrary"),
    )(view(dy), yraw3, view(u), h3, ar, ai, bre, bim, cre, cim, cfr, cfi, dvec)
    return (res[0].reshape(s, sw),) + tuple(res[1:])


def _lnmod(x, sc, sh, *, name):
    s, d = x.shape
    tb = _tile(s, 512)

    def body(x_ref, sc_ref, sh_ref, o_ref):
        xh, _ = _ln(x_ref[...])
        o_ref[...] = (xh * (1.0 + sc_ref[...]) + sh_ref[...]).astype(bf16)

    blk = pl.BlockSpec((tb, d), lambda i: (i, 0))
    vec = pl.BlockSpec((1, d), _row)
    return pl.pallas_call(body, name=name, grid=(s // tb,), in_specs=[blk, vec, vec], out_specs=blk,
                          out_shape=jax.ShapeDtypeStruct((s, d), bf16), compiler_params=_params("parallel"))(x, sc, sh)


def _conv_taps(ext_ref, w_ref, tb, ntap, off):
    acc = ext_ref[pl.ds(off, tb), :] * w_ref[pl.ds(0, 1), :]
    for k in range(1, ntap):
        acc = acc + ext_ref[pl.ds(off + k, tb), :] * w_ref[pl.ds(k, 1), :]
    return acc


def _conv_halo_specs(tb, cw, halo, s):
    per = tb // halo
    prev = pl.BlockSpec((halo, cw), lambda i: (jnp.maximum(i * per - 1, 0), 0))
    nxt = pl.BlockSpec((halo, cw), lambda i: (jnp.minimum((i + 1) * per, s // halo - 1), 0))
    return prev, nxt


def _conv_v2(a_ref, g_ref, ah_ref, gh_ref, w_ref, b_ref, ext, tb, i):
    gg = g_ref[...]
    ext[pl.ds(CONV_HALO, tb), :] = a_ref[...] * _sig(gg)
    ext[pl.ds(0, CONV_HALO), :] = jnp.where(i > 0, ah_ref[...] * _sig(gh_ref[...]), 0.0)
    return _conv_taps(ext, w_ref, tb, CONV_K, CONV_HALO - CONV_K + 1) + b_ref[...]


def _silu_grad(x):
    sg = _sig(x)
    return sg * (1.0 + x * (1.0 - sg))


def _conv_fwd(cva, cvg, w, b, lng, lnb, *, name):
    s, cw = cva.shape
    tb = _tile(s, 256)
    prev, _ = _conv_halo_specs(tb, cw, CONV_HALO, s)

    def body(a_ref, g_ref, ah_ref, gh_ref, w_ref, b_ref, lng_ref, lnb_ref, o_ref, ext):
        v2 = _conv_v2(a_ref, g_ref, ah_ref, gh_ref, w_ref, b_ref, ext, tb, pl.program_id(0))
        xh, _ = _ln(v2)
        v3 = xh * lng_ref[...] + lnb_ref[...]
        o_ref[...] = (v3 * _sig(v3)).astype(bf16)

    blk = pl.BlockSpec((tb, cw), lambda i: (i, 0))
    vec = pl.BlockSpec((1, cw), _row)
    return pl.pallas_call(
        body, name=name, grid=(s // tb,), in_specs=[blk, blk, prev, prev, _full(w.shape), vec, vec, vec], out_specs=blk,
        out_shape=jax.ShapeDtypeStruct((s, cw), bf16), scratch_shapes=[pltpu.VMEM((tb + CONV_HALO, cw), f32)],
        compiler_params=_params("parallel"))(cva, cvg, cva, cvg, w, b, lng, lnb)


def _conv_bwd_ln(dv4, cva, cvg, w, b, lng, lnb, *, name):
    s, cw = cva.shape
    tb = _tile(s, 256)
    prev, _ = _conv_halo_specs(tb, cw, CONV_HALO, s)

    def body(d_ref, a_ref, g_ref, ah_ref, gh_ref, w_ref, b_ref, lng_ref, lnb_ref, o_ref, dg_ref, db_ref, ext):
        i = pl.program_id(0)

        @pl.when(i == 0)
        def _():
            dg_ref[...] = jnp.zeros_like(dg_ref)
            db_ref[...] = jnp.zeros_like(db_ref)

        v2 = _conv_v2(a_ref, g_ref, ah_ref, gh_ref, w_ref, b_ref, ext, tb, i)
        xh, rstd = _ln(v2)
        v3 = xh * lng_ref[...] + lnb_ref[...]
        dv3 = d_ref[...] * _silu_grad(v3)
        dg_ref[...] += _colsum(dv3 * xh)
        db_ref[...] += _colsum(dv3)
        o_ref[...] = _ln_bwd(dv3 * lng_ref[...], xh, rstd)

    blk = pl.BlockSpec((tb, cw), lambda i: (i, 0))
    vec = pl.BlockSpec((1, cw), _row)
    vshape = jax.ShapeDtypeStruct((1, cw), f32)
    return pl.pallas_call(
        body, name=name, grid=(s // tb,), in_specs=[blk, blk, blk, prev, prev, _full(w.shape), vec, vec, vec],
        out_specs=[blk, vec, vec], out_shape=[jax.ShapeDtypeStruct((s, cw), f32), vshape, vshape],
        scratch_shapes=[pltpu.VMEM((tb + CONV_HALO, cw), f32)],
        compiler_params=_params("arbitrary"))(dv4, cva, cvg, cva, cvg, w, b, lng, lnb)


def _conv_bwd_taps(dv2, cva, cvg, w, *, name):
    s, cw = cva.shape
    tb = _tile(s, 256)
    nb = s // tb
    prev, nxt = _conv_halo_specs(tb, cw, CONV_HALO, s)

    def body(d_ref, dn_ref, a_ref, g_ref, ah_ref, gh_ref, w_ref, da_ref, dg_ref, dw_ref, db_ref, sa_ref, sg_ref, ext, dext):
        i = pl.program_id(0)

        @pl.when(i == 0)
        def _():
            for r in (dw_ref, db_ref, sa_ref, sg_ref):
                r[...] = jnp.zeros_like(r)

        aa, gg = a_ref[...], g_ref[...]
        sg = _sig(gg)
        ext[pl.ds(CONV_HALO, tb), :] = aa * sg
        ext[pl.ds(0, CONV_HALO), :] = jnp.where(i > 0, ah_ref[...] * _sig(gh_ref[...]), 0.0)
        dd = d_ref[...]
        dext[pl.ds(0, tb), :] = dd
        dext[pl.ds(tb, CONV_HALO), :] = jnp.where(i < nb - 1, dn_ref[...], 0.0)
        dv = dext[pl.ds(CONV_K - 1, tb), :] * w_ref[pl.ds(0, 1), :]
        for k in range(1, CONV_K):
            dv = dv + dext[pl.ds(CONV_K - 1 - k, tb), :] * w_ref[pl.ds(k, 1), :]
        for k in range(CONV_K):
            dw_ref[pl.ds(k, 1), :] += _colsum(dd * ext[pl.ds(CONV_HALO - CONV_K + 1 + k, tb), :])
        db_ref[...] += _colsum(dd)
        da = dv * sg
        dgate = dv * aa * sg * (1.0 - sg)
        sa_ref[...] += _colsum(da)
        sg_ref[...] += _colsum(dgate)
        da_ref[...] = da.astype(bf16)
        dg_ref[...] = dgate.astype(bf16)

    blk = pl.BlockSpec((tb, cw), lambda i: (i, 0))
    vec = pl.BlockSpec((1, cw), _row)
    vshape = jax.ShapeDtypeStruct((1, cw), f32)
    act = jax.ShapeDtypeStruct((s, cw), bf16)
    return pl.pallas_call(
        body, name=name, grid=(nb,), in_specs=[blk, nxt, blk, blk, prev, prev, _full(w.shape)],
        out_specs=[blk, blk, _full(w.shape), vec, vec, vec],
        out_shape=[act, act, jax.ShapeDtypeStruct(w.shape, f32), vshape, vshape, vshape],
        scratch_shapes=[pltpu.VMEM((tb + CONV_HALO, cw), f32), pltpu.VMEM((tb + CONV_HALO, cw), f32)],
        compiler_params=_params("arbitrary"))(dv2, dv2, cva, cvg, cva, cvg, w)


def _glu_merge(ya, yb, ycv, gs, gc, *, name):
    s, d = ya.shape
    tb = _tile(s, 512)

    def body(ya_ref, yb_ref, ycv_ref, gs_ref, gc_ref, o_ref):
        z = ya_ref[...] * _sig(yb_ref[...])
        o_ref[...] = (_sig(gs_ref[...]) * z + _sig(gc_ref[...]) * ycv_ref[...]).astype(bf16)

    blk = pl.BlockSpec((tb, d), lambda i: (i, 0))
    return pl.pallas_call(body, name=name, grid=(s // tb,), in_specs=[blk] * 5, out_specs=blk,
                          out_shape=jax.ShapeDtypeStruct((s, d), bf16), compiler_params=_params("parallel"))(ya, yb, ycv, gs, gc)


def _glu_merge_bwd(dm, ya, yb, ycv, gs, gc, *, name):
    s, d = ya.shape
    tb = _tile(s, 512)

    def body(dm_ref, ya_ref, yb_ref, ycv_ref, gs_ref, gc_ref, dya_ref, dyb_ref, dycv_ref, dgs_ref, dgc_ref, sgs_ref, sgc_ref):
        @pl.when(pl.program_id(0) == 0)
        def _():
            sgs_ref[...] = jnp.zeros_like(sgs_ref)
            sgc_ref[...] = jnp.zeros_like(sgc_ref)

        dmv, yav = dm_ref[...], ya_ref[...]
        sb, ss, scv = _sig(yb_ref[...]), _sig(gs_ref[...]), _sig(gc_ref[...])
        z = yav * sb
        dz = dmv * ss
        dgs = dmv * z * ss * (1.0 - ss)
        dgc = dmv * ycv_ref[...] * scv * (1.0 - scv)
        dya_ref[...] = (dz * sb).astype(bf16)
        dyb_ref[...] = (dz * yav * sb * (1.0 - sb)).astype(bf16)
        dycv_ref[...] = (dmv * scv).astype(bf16)
        dgs_ref[...] = dgs.astype(bf16)
        dgc_ref[...] = dgc.astype(bf16)
        sgs_ref[...] += _colsum(dgs)
        sgc_ref[...] += _colsum(dgc)

    blk = pl.BlockSpec((tb, d), lambda i: (i, 0))
    vec = pl.BlockSpec((1, d), _row)
    act = jax.ShapeDtypeStruct((s, d), bf16)
    vshape = jax.ShapeDtypeStruct((1, d), f32)
    return pl.pallas_call(body, name=name, grid=(s // tb,), in_specs=[blk] * 6, out_specs=[blk] * 5 + [vec, vec],
                          out_shape=[act] * 5 + [vshape, vshape], compiler_params=_params("arbitrary"))(dm, ya, yb, ycv, gs, gc)


def _resid_ln_mod(x, o, g, lng, lnb, sc, sh, alpha, *, name):
    s, d = x.shape
    tb = _tile(s, 512)

    def body(x_ref, o_ref, g_ref, lng_ref, lnb_ref, sc_ref, sh_ref, x1_ref, h_ref):
        xh, _ = _ln(alpha * x_ref[...] + g_ref[...] * o_ref[...])
        x1 = xh * lng_ref[...] + lnb_ref[...]
        x1_ref[...] = x1
        xh1, _ = _ln(x1)
        h_ref[...] = (xh1 * (1.0 + sc_ref[...]) + sh_ref[...]).astype(bf16)

    blk = pl.BlockSpec((tb, d), lambda i: (i, 0))
    vec = pl.BlockSpec((1, d), _row)
    return pl.pallas_call(body, name=name, grid=(s // tb,), in_specs=[blk, blk] + [vec] * 5, out_specs=[blk, blk],
                          out_shape=[jax.ShapeDtypeStruct((s, d), f32), jax.ShapeDtypeStruct((s, d), bf16)],
                          compiler_params=_params("parallel"))(x, o, g, lng, lnb, sc, sh)


def _resid_ln_loss(x1, y2, g, lng, lnb, tgt, alpha, *, name):
    s, d = x1.shape
    tb = _tile(s, 512)

    def body(x1_ref, y_ref, g_ref, lng_ref, lnb_ref, t_ref, dr_ref, dy_ref, loss_ref, dlg_ref, dlb_ref, dg_ref):
        @pl.when(pl.program_id(0) == 0)
        def _():
            for r in (loss_ref, dlg_ref, dlb_ref, dg_ref):
                r[...] = jnp.zeros_like(r)

        yv = y_ref[...]
        xh, rstd = _ln(alpha * x1_ref[...] + g_ref[...] * yv)
        err = xh * lng_ref[...] + lnb_ref[...] - t_ref[...]
        loss_ref[...] += 0.5 * jnp.sum(jnp.sum(err * err, axis=-1, keepdims=True) / d, axis=0, keepdims=True)
        dx2 = err / d
        dlg_ref[...] += _colsum(dx2 * xh)
        dlb_ref[...] += _colsum(dx2)
        dr = _ln_bwd(dx2 * lng_ref[...], xh, rstd)
        dg_ref[...] += _colsum(dr * yv)
        dr_ref[...] = dr
        dy_ref[...] = (g_ref[...] * dr).astype(bf16)

    blk = pl.BlockSpec((tb, d), lambda i: (i, 0))
    vec = pl.BlockSpec((1, d), _row)
    vshape = jax.ShapeDtypeStruct((1, d), f32)
    return pl.pallas_call(
        body, name=name, grid=(s // tb,), in_specs=[blk, blk, vec, vec, vec, blk],
        out_specs=[blk, blk, pl.BlockSpec((1, 1), _row), vec, vec, vec],
        out_shape=[jax.ShapeDtypeStruct((s, d), f32), jax.ShapeDtypeStruct((s, d), bf16),
                   jax.ShapeDtypeStruct((1, 1), f32), vshape, vshape, vshape],
        compiler_params=_params("arbitrary"))(x1, y2, g, lng, lnb, tgt)


def _mid_bwd(dh2, x1, dr2, x, o, g, sc, lng, alpha, *, name):
    s, d = x.shape
    tb = _tile(s, 512)

    def body(dh_ref, x1_ref, dr2_ref, x_ref, o_ref, g_ref, sc_ref, lng_ref,
             dr1_ref, do_ref, dsc_ref, dsh_ref, dlg_ref, dlb_ref, dg_ref):
        @pl.when(pl.program_id(0) == 0)
        def _():
            for r in (dsc_ref, dsh_ref, dlg_ref, dlb_ref, dg_ref):
                r[...] = jnp.zeros_like(r)

        dh = dh_ref[...]
        xh1, rstd1 = _ln(x1_ref[...])
        dsc_ref[...] += _colsum(dh * xh1)
        dsh_ref[...] += _colsum(dh)
        dx1 = alpha * dr2_ref[...] + _ln_bwd(dh * (1.0 + sc_ref[...]), xh1, rstd1)
        ov = o_ref[...]
        xhr, rstdr = _ln(alpha * x_ref[...] + g_ref[...] * ov)
        dlg_ref[...] += _colsum(dx1 * xhr)
        dlb_ref[...] += _colsum(dx1)
        dr1 = _ln_bwd(dx1 * lng_ref[...], xhr, rstdr)
        dg_ref[...] += _colsum(dr1 * ov)
        dr1_ref[...] = dr1
        do_ref[...] = (g_ref[...] * dr1).astype(bf16)

    blk = pl.BlockSpec((tb, d), lambda i: (i, 0))
    vec = pl.BlockSpec((1, d), _row)
    vshape = jax.ShapeDtypeStruct((1, d), f32)
    return pl.pallas_call(
        body, name=name, grid=(s // tb,), in_specs=[blk] * 5 + [vec] * 3, out_specs=[blk, blk] + [vec] * 5,
        out_shape=[jax.ShapeDtypeStruct((s, d), f32), jax.ShapeDtypeStruct((s, d), bf16)] + [vshape] * 5,
        compiler_params=_params("arbitrary"))(dh2, x1, dr2, x, o, g, sc, lng)


def _final_bwd(dh1, x, dr1, sc, alpha, *, name):
    s, d = x.shape
    tb = _tile(s, 512)

    def body(dh_ref, x_ref, dr1_ref, sc_ref, dx_ref, dsc_ref, dsh_ref):
        @pl.when(pl.program_id(0) == 0)
        def _():
            dsc_ref[...] = jnp.zeros_like(dsc_ref)
            dsh_ref[...] = jnp.zeros_like(dsh_ref)

        dh = dh_ref[...]
        xh, rstd = _ln(x_ref[...])
        dsc_ref[...] += _colsum(dh * xh)
        dsh_ref[...] += _colsum(dh)
        dx_ref[...] = alpha * dr1_ref[...] + _ln_bwd(dh * (1.0 + sc_ref[...]), xh, rstd)

    blk = pl.BlockSpec((tb, d), lambda i: (i, 0))
    vec = pl.BlockSpec((1, d), _row)
    vshape = jax.ShapeDtypeStruct((1, d), f32)
    return pl.pallas_call(body, name=name, grid=(s // tb,), in_specs=[blk, blk, blk, vec], out_specs=[blk, vec, vec],
                          out_shape=[jax.ShapeDtypeStruct((s, d), f32), vshape, vshape],
                          compiler_params=_params("arbitrary"))(dh1, x, dr1, sc)


def _ffn_specs(s, fh, tb, tc):
    per = tb // FFN_HALO
    blk = pl.BlockSpec((tb, tc), lambda j, i: (i, j))
    prev = pl.BlockSpec((FFN_HALO, tc), lambda j, i: (jnp.maximum(i * per - 1, 0), j))
    nxt = pl.BlockSpec((FFN_HALO, tc), lambda j, i: (jnp.minimum((i + 1) * per, s // FFN_HALO - 1), j))
    taps = pl.BlockSpec((FFN_HALO, tc), lambda j, i: (0, j))
    vec = pl.BlockSpec((1, tc), lambda j, i: (0, j))
    return blk, prev, nxt, taps, vec


def _ffn_mid(upa, upv, wa, wv, ba, bv, *, name):
    s, fh = upa.shape
    tb, tc = _tile(s, 512), _tile(fh, 256)
    blk, prev, _, taps, vec = _ffn_specs(s, fh, tb, tc)
    off = FFN_HALO - FFN_K + 1

    def body(a_ref, v_ref, ah_ref, vh_ref, wa_ref, wv_ref, ba_ref, bv_ref, o_ref, exta, extv):
        first = pl.program_id(1) == 0
        exta[pl.ds(FFN_HALO, tb), :] = a_ref[...]
        extv[pl.ds(FFN_HALO, tb), :] = v_ref[...]
        exta[pl.ds(0, FFN_HALO), :] = jnp.where(first, 0.0, ah_ref[...])
        extv[pl.ds(0, FFN_HALO), :] = jnp.where(first, 0.0, vh_ref[...])
        a2 = _conv_taps(exta, wa_ref, tb, FFN_K, off) + ba_ref[...]
        v2 = _conv_taps(extv, wv_ref, tb, FFN_K, off) + bv_ref[...]
        o_ref[...] = (_gelu(a2) * v2).astype(bf16)

    return pl.pallas_call(
        body, name=name, grid=(fh // tc, s // tb), in_specs=[blk, blk, prev, prev, taps, taps, vec, vec], out_specs=blk,
        out_shape=jax.ShapeDtypeStruct((s, fh), bf16),
        scratch_shapes=[pltpu.VMEM((tb + FFN_HALO, tc), f32)] * 2,
        compiler_params=_params("parallel", "arbitrary"))(upa, upv, upa, upv, wa, wv, ba, bv)


def _ffn_mid_bwd(df, upa, upv, wa, wv, ba, bv, *, name):
    s, fh = upa.shape
    tb, tc = _tile(s, 512), _tile(fh, 256)
    nb = s // tb
    blk, prev, nxt, taps, vec = _ffn_specs(s, fh, tb, tc)
    off = FFN_HALO - FFN_K + 1
    te = tb + FFN_HALO

    def body(df_ref, dfn_ref, a_ref, v_ref, ah_ref, vh_ref, an_ref, vn_ref, wa_ref, wv_ref, ba_ref, bv_ref,
             da_ref, dv_ref, dwa_ref, dwv_ref, dba_ref, dbv_ref, exta, extv, dexta, dextv):
        i = pl.program_id(1)

        @pl.when(i == 0)
        def _():
            for r in (dwa_ref, dwv_ref, dba_ref, dbv_ref):
                r[...] = jnp.zeros_like(r)

        last = i == nb - 1
        for ext, c_ref, h_ref, n_ref in ((exta, a_ref, ah_ref, an_ref), (extv, v_ref, vh_ref, vn_ref)):
            ext[pl.ds(0, FFN_HALO), :] = jnp.where(i == 0, 0.0, h_ref[...])
            ext[pl.ds(FFN_HALO, tb), :] = c_ref[...]
            ext[pl.ds(FFN_HALO + tb, FFN_HALO), :] = jnp.where(last, 0.0, n_ref[...])
        a2 = _conv_taps(exta, wa_ref, te, FFN_K, off) + ba_ref[...]
        v2 = _conv_taps(extv, wv_ref, te, FFN_K, off) + bv_ref[...]
        dexta[pl.ds(0, tb), :] = df_ref[...]
        dexta[pl.ds(tb, FFN_HALO), :] = jnp.where(last, 0.0, dfn_ref[...])
        dfe = dexta[...]
        da2 = dfe * v2 * _gelu_grad(a2)
        dv2 = dfe * _gelu(a2)
        dexta[...] = da2
        dextv[...] = dv2
        for ext, dext, w_ref, dw_ref, db_ref, o_ref in ((exta, dexta, wa_ref, dwa_ref, dba_ref, da_ref),
                                                        (extv, dextv, wv_ref, dwv_ref, dbv_ref, dv_ref)):
            dcur = dext[pl.ds(0, tb), :]
            dup = dext[pl.ds(FFN_K - 1, tb), :] * w_ref[pl.ds(0, 1), :]
            for k in range(1, FFN_K):
                dup = dup + dext[pl.ds(FFN_K - 1 - k, tb), :] * w_ref[pl.ds(k, 1), :]
            o_ref[...] = dup.astype(bf16)
            for k in range(FFN_K):
                dw_ref[pl.ds(k, 1), :] += _colsum(dcur * ext[pl.ds(off + k, tb), :])
            db_ref[...] += _colsum(dcur)

    act = jax.ShapeDtypeStruct((s, fh), bf16)
    wshape = jax.ShapeDtypeStruct((FFN_HALO, fh), f32)
    vshape = jax.ShapeDtypeStruct((1, fh), f32)
    return pl.pallas_call(
        body, name=name, grid=(fh // tc, nb),
        in_specs=[blk, nxt, blk, blk, prev, prev, nxt, nxt, taps, taps, vec, vec],
        out_specs=[blk, blk, taps, taps, vec, vec], out_shape=[act, act, wshape, wshape, vshape, vshape],
        scratch_shapes=[pltpu.VMEM((tb + 2 * FFN_HALO, tc), f32)] * 2 + [pltpu.VMEM((te, tc), f32)] * 2,
        compiler_params=_params("parallel", "arbitrary"))(df, df, upa, upv, upa, upv, upa, upv, wa, wv, ba, bv)


def _cols_from_shards(stacked):
    _, k, n = stacked.shape
    return stacked.transpose(1, 0, 2).reshape(k, NDEV * n)


def _pad_rows(w, rows):
    return jnp.pad(w, ((0, rows - w.shape[0]), (0, 0)))


def kernel(x, c, w_cond, b_cond, w_in, b_in, ssm_lambda_re, ssm_lambda_im, ssm_log_dt, ssm_b_re, ssm_b_im, ssm_c_re, ssm_c_im, ssm_d, ssm_glu_w_a, ssm_glu_w_b, cv_dw_w, cv_dw_b, cv_ln_g, cv_ln_b, cv_w_pw, w_out, ln1_g, ln1_b, ffn_w_up, ffn_dw_w, ffn_dw_b, ffn_w_down, ln2_g, ln2_b, loss_target, m_w_cond, m_b_cond, m_w_in, m_b_in, m_ssm_lambda_re, m_ssm_lambda_im, m_ssm_log_dt, m_ssm_b_re, m_ssm_b_im, m_ssm_c_re, m_ssm_c_im, m_ssm_d, m_ssm_glu_w_a, m_ssm_glu_w_b, m_cv_dw_w, m_cv_dw_b, m_cv_ln_g, m_cv_ln_b, m_cv_w_pw, m_w_out, m_ln1_g, m_ln1_b, m_ffn_w_up, m_ffn_dw_w, m_ffn_dw_b, m_ffn_w_down, m_ln2_g, m_ln2_b, v_w_cond, v_b_cond, v_w_in, v_b_in, v_ssm_lambda_re, v_ssm_lambda_im, v_ssm_log_dt, v_ssm_b_re, v_ssm_b_im, v_ssm_c_re, v_ssm_c_im, v_ssm_d, v_ssm_glu_w_a, v_ssm_glu_w_b, v_cv_dw_w, v_cv_dw_b, v_cv_ln_g, v_cv_ln_b, v_cv_w_pw, v_w_out, v_ln1_g, v_ln1_b, v_ffn_w_up, v_ffn_dw_w, v_ffn_dw_b, v_ffn_w_down, v_ln2_g, v_ln2_b):
    weights = dict(w_cond=w_cond, b_cond=b_cond, w_in=w_in, b_in=b_in, ssm_lambda_re=ssm_lambda_re, ssm_lambda_im=ssm_lambda_im, ssm_log_dt=ssm_log_dt, ssm_b_re=ssm_b_re, ssm_b_im=ssm_b_im, ssm_c_re=ssm_c_re, ssm_c_im=ssm_c_im, ssm_d=ssm_d, ssm_glu_w_a=ssm_glu_w_a, ssm_glu_w_b=ssm_glu_w_b, cv_dw_w=cv_dw_w, cv_dw_b=cv_dw_b, cv_ln_g=cv_ln_g, cv_ln_b=cv_ln_b, cv_w_pw=cv_w_pw, w_out=w_out, ln1_g=ln1_g, ln1_b=ln1_b, ffn_w_up=ffn_w_up, ffn_dw_w=ffn_dw_w, ffn_dw_b=ffn_dw_b, ffn_w_down=ffn_w_down, ln2_g=ln2_g, ln2_b=ln2_b)
    mom_m = dict(w_cond=m_w_cond, b_cond=m_b_cond, w_in=m_w_in, b_in=m_b_in, ssm_lambda_re=m_ssm_lambda_re, ssm_lambda_im=m_ssm_lambda_im, ssm_log_dt=m_ssm_log_dt, ssm_b_re=m_ssm_b_re, ssm_b_im=m_ssm_b_im, ssm_c_re=m_ssm_c_re, ssm_c_im=m_ssm_c_im, ssm_d=m_ssm_d, ssm_glu_w_a=m_ssm_glu_w_a, ssm_glu_w_b=m_ssm_glu_w_b, cv_dw_w=m_cv_dw_w, cv_dw_b=m_cv_dw_b, cv_ln_g=m_cv_ln_g, cv_ln_b=m_cv_ln_b, cv_w_pw=m_cv_w_pw, w_out=m_w_out, ln1_g=m_ln1_g, ln1_b=m_ln1_b, ffn_w_up=m_ffn_w_up, ffn_dw_w=m_ffn_dw_w, ffn_dw_b=m_ffn_dw_b, ffn_w_down=m_ffn_w_down, ln2_g=m_ln2_g, ln2_b=m_ln2_b)
    mom_v = dict(w_cond=v_w_cond, b_cond=v_b_cond, w_in=v_w_in, b_in=v_b_in, ssm_lambda_re=v_ssm_lambda_re, ssm_lambda_im=v_ssm_lambda_im, ssm_log_dt=v_ssm_log_dt, ssm_b_re=v_ssm_b_re, ssm_b_im=v_ssm_b_im, ssm_c_re=v_ssm_c_re, ssm_c_im=v_ssm_c_im, ssm_d=v_ssm_d, ssm_glu_w_a=v_ssm_glu_w_a, ssm_glu_w_b=v_ssm_glu_w_b, cv_dw_w=v_cv_dw_w, cv_dw_b=v_cv_dw_b, cv_ln_g=v_cv_ln_g, cv_ln_b=v_cv_ln_b, cv_w_pw=v_cv_w_pw, w_out=v_w_out, ln1_g=v_ln1_g, ln1_b=v_ln1_b, ffn_w_up=v_ffn_w_up, ffn_dw_w=v_ffn_dw_w, ffn_dw_b=v_ffn_dw_b, ffn_w_down=v_ffn_w_down, ln2_g=v_ln2_g, ln2_b=v_ln2_b)
    names = list(weights)

    s, d = x.shape[1], x.shape[2]
    sw = cw = d // 2
    fh = ffn_w_down.shape[1] * NDEV
    ng, nq = sw // SSM_GROUP, sw // QW
    gq = ng // nq
    alpha = 2.0 ** 0.25
    me = 4 * lax.axis_index("x") + 2 * lax.axis_index("y") + lax.axis_index("c")
    xs, tgt = x[0], loss_target[0]

    col_names = ["w_in", "ssm_glu_w_a", "ssm_glu_w_b", "cv_w_pw", "ffn_w_up"]
    row_names = ["w_out", "ffn_w_down"]
    big = col_names + row_names
    payload = [weights[n][0].astype(bf16) for n in big] + [c, cv_dw_w[0, :, 0], ffn_dw_w[0, :, 0]]
    got = dict(zip(big + ["c", "cv_taps", "ffn_taps"], _exchange(payload, scatter=False, name="gather_weights")))
    o1, o2, o3, o4 = sw, sw + cw, sw + 2 * cw, sw + 2 * cw + d
    in_bounds = ((0, o1), (o1, o2), (o2, o3), (o3, o4), (o4, o4 + d))
    w_u, w_cva, w_cvg, w_gs, w_gc = _unshard_cols(got["w_in"], in_bounds, name="unshard_w_in")
    b_u, b_cva, b_cvg, b_gs, b_gc = (b_in[:, a:b] for a, b in in_bounds)
    w_upa, w_upv = _unshard_cols(got["ffn_w_up"], ((0, fh), (fh, 2 * fh)), name="unshard_ffn_up")
    w_a, = _unshard_cols(got["ssm_glu_w_a"], ((0, d),), name="unshard_glu_a")
    w_b, = _unshard_cols(got["ssm_glu_w_b"], ((0, d),), name="unshard_glu_b")
    w_pw, = _unshard_cols(got["cv_w_pw"], ((0, d),), name="unshard_conv_pw")
    w_o, w_dn = got["w_out"].reshape(d, d), got["ffn_w_down"].reshape(fh, d)
    c_all = got["c"].reshape(NDEV, d)
    cv_taps = _cols_from_shards(got["cv_taps"])
    ffn_taps = _cols_from_shards(got["ffn_taps"])
    cv_w32 = _pad_rows(cv_taps, CONV_HALO)
    ffn_wa, ffn_wv = _pad_rows(ffn_taps[:, :fh], FFN_HALO), _pad_rows(ffn_taps[:, fh:], FFN_HALO)
    ffn_ba, ffn_bv = ffn_dw_b[:, :fh], ffn_dw_b[:, fh:]

    ncond = w_cond.shape[2]
    b_cond_mine = lax.dynamic_slice(b_cond, (0, me * ncond), (1, ncond))
    mod_cols = _cond_fwd(c_all, w_cond[0], b_cond_mine, name="cond_fwd")
    mod_all, = _exchange([mod_cols], scatter=False, name="gather_mod")
    mod_mine = lax.dynamic_slice(mod_all, (0, me, 0), (NDEV, 1, ncond)).reshape(1, 6 * d)
    sh1, sc1, g1, sh2, sc2, g2 = (mod_mine[:, k * d:(k + 1) * d] for k in range(6))

    lam_re, lam_im, log_dt = ssm_lambda_re[0], ssm_lambda_im[0], ssm_log_dt[0][:, None]
    lbr, lbi, cfr, cfi = _ssm_prep(lam_re, lam_im, log_dt, name="ssm_prep")
    rows_q = lambda a: a.reshape(nq, 1, QS)
    eye = jnp.eye(gq, dtype=f32)

    def b_mat(b):
        bt = b.reshape(nq, gq, SSM_STATE, SSM_GROUP).transpose(0, 1, 3, 2)
        return jnp.einsum("qgpn,gh->qgphn", bt, eye).reshape(nq, QW, QS)

    def c_mat(cc):
        ct = cc.reshape(nq, gq, SSM_GROUP, SSM_STATE)
        return jnp.einsum("qgpn,gh->qhngp", ct, eye).reshape(nq, QS, QW)

    def b_unmat(mt):
        return jnp.einsum("qgpgn->qgnp", mt.reshape(nq, gq, SSM_GROUP, gq, SSM_STATE)).reshape(ng, SSM_STATE, SSM_GROUP)

    def c_unmat(mt):
        return jnp.einsum("qgngp->qgpn", mt.reshape(nq, gq, SSM_STATE, gq, SSM_GROUP)).reshape(ng, SSM_GROUP, SSM_STATE)

    ssm_args = (rows_q(lbr), rows_q(lbi), b_mat(ssm_b_re[0]), b_mat(ssm_b_im[0]), c_mat(ssm_c_re[0]), c_mat(ssm_c_im[0]),
                rows_q(cfr), rows_q(cfi), ssm_d[0].reshape(1, sw))

    h1 = _lnmod(xs, sc1, sh1, name="ln_mod1")
    u = _mm([(h1, w_u)], b_u, name="in_u")
    cva = _mm([(h1, w_cva)], b_cva, name="in_cva")
    cvg = _mm([(h1, w_cvg)], b_cvg, name="in_cvg")
    gs = _mm([(h1, w_gs)], b_gs, name="in_gs")
    gc = _mm([(h1, w_gc)], b_gc, name="in_gc")
    h3, yraw3, y = _ssm_fwd(u, *ssm_args, name="ssm_fwd")
    ya = _mm([(y, w_a)], name="glu_a")
    yb = _mm([(y, w_b)], name="glu_b")
    v4 = _conv_fwd(cva, cvg, cv_w32, cv_dw_b, cv_ln_g, cv_ln_b, name="conv_fwd")
    ycv = _mm([(v4, w_pw)], name="conv_pw")
    merged = _glu_merge(ya, yb, ycv, gs, gc, name="merge")
    o = _mm([(merged, w_o)], name="out_proj")
    x1, h2 = _resid_ln_mod(xs, o, g1, ln1_g, ln1_b, sc2, sh2, alpha, name="resid_ln1")
    upa = _mm([(h2, w_upa)], name="ffn_up_a")
    upv = _mm([(h2, w_upv)], name="ffn_up_v")
    f = _ffn_mid(upa, upv, ffn_wa, ffn_wv, ffn_ba, ffn_bv, name="ffn_mid")
    y2 = _mm([(f, w_dn)], name="ffn_down")
    dr2, dy2, loss_part, d_ln2_g, d_ln2_b, d_g2 = _resid_ln_loss(x1, y2, g2, ln2_g, ln2_b, tgt, alpha, name="resid_ln2_loss")

    gw = {}
    df = _mm([(dy2, w_dn)], trans_w=True, name="d_ffn_down")
    gw["ffn_w_down"] = _mm_tn(f, dy2, name="g_ffn_down").reshape((NDEV,) + ffn_w_down[0].shape)
    dupa, dupv, d_ffn_wa, d_ffn_wv, d_ffn_ba, d_ffn_bv = _ffn_mid_bwd(df, upa, upv, ffn_wa, ffn_wv, ffn_ba, ffn_bv, name="ffn_mid_bwd")
    dh2 = _mm([(dupa, w_upa), (dupv, w_upv)], trans_w=True, name="d_ffn_up")
    gw["ffn_w_up"] = _shard_cols([_mm_tn(h2, dupa, name="g_ffn_up_a"), _mm_tn(h2, dupv, name="g_ffn_up_v")], name="shard_ffn_up")
    dr1, do, d_sc2, d_sh2, d_ln1_g, d_ln1_b, d_g1 = _mid_bwd(dh2, x1, dr2, xs, o, g1, sc2, ln1_g, alpha, name="mid_bwd")
    dmerged = _mm([(do, w_o)], trans_w=True, name="d_out_proj")
    gw["w_out"] = _mm_tn(merged, do, name="g_out_proj").reshape((NDEV,) + w_out[0].shape)
    dya, dyb, dycv, dgs, dgc, s_gs, s_gc = _glu_merge_bwd(dmerged, ya, yb, ycv, gs, gc, name="merge_bwd")
    dy = _mm([(dya, w_a), (dyb, w_b)], trans_w=True, name="d_glu")
    gw["ssm_glu_w_a"] = _shard_cols([_mm_tn(y, dya, name="g_glu_a")], name="shard_glu_a")
    gw["ssm_glu_w_b"] = _shard_cols([_mm_tn(y, dyb, name="g_glu_b")], name="shard_glu_b")
    dv4 = _mm([(dycv, w_pw)], trans_w=True, name="d_conv_pw")
    gw["cv_w_pw"] = _shard_cols([_mm_tn(v4, dycv, name="g_conv_pw")], name="shard_conv_pw")
    dv2, d_cv_ln_g, d_cv_ln_b = _conv_bwd_ln(dv4, cva, cvg, cv_w32, cv_dw_b, cv_ln_g, cv_ln_b, name="conv_bwd_ln")
    dcva, dcvg, d_cv_w32, d_cv_b, s_cva, s_cvg = _conv_bwd_taps(dv2, cva, cvg, cv_w32, name="conv_bwd_taps")
    (du, d_bre_m, d_bim_m, d_cre_m, d_cim_m, d_cfr, d_cfi, d_lbr, d_lbi, d_d, s_u) = _ssm_bwd(
        dy, yraw3, u, h3, *ssm_args, name="ssm_bwd")
    gshape = lam_re.shape
    d_lam_re, d_lam_im, d_log_dt = _ssm_prep_bwd(
        lam_re, lam_im, log_dt, [a.reshape(gshape) for a in (d_lbr, d_lbi, d_cfr, d_cfi)], name="ssm_prep_bwd")
    dh1 = _mm([(du, w_u), (dcva, w_cva), (dcvg, w_cvg), (dgs, w_gs), (dgc, w_gc)], trans_w=True, name="d_in")
    gw["w_in"] = _shard_cols([_mm_tn(h1, t, name="g_in_" + nm) for nm, t in
                              (("u", du), ("cva", dcva), ("cvg", dcvg), ("gs", dgs), ("gc", dgc))], name="shard_w_in")
    grad_x, d_sc1, d_sh1 = _final_bwd(dh1, xs, dr1, sc1, alpha, name="final_bwd")

    received = dict(zip(big, _exchange([gw[n] for n in big], scatter=True, name="scatter_grads")))
    grads, delta, new_m, new_v = {}, {}, {}, {}
    for n in big:
        grads[n], delta[n], new_m[n], new_v[n] = _sum_adamw(received[n], weights[n][0], mom_m[n][0], mom_v[n][0], name="adamw_" + n)

    dmod = jnp.concatenate([d_sh1, d_sc1, d_g1, d_sh2, d_sc2, d_g2], axis=1)
    small = {
        "b_in": jnp.concatenate([s_u, s_cva, s_cvg, s_gs, s_gc], axis=1),
        "ssm_lambda_re": d_lam_re, "ssm_lambda_im": d_lam_im, "ssm_log_dt": d_log_dt,
        "ssm_b_re": b_unmat(d_bre_m), "ssm_b_im": b_unmat(d_bim_m), "ssm_c_re": c_unmat(d_cre_m), "ssm_c_im": c_unmat(d_cim_m),
        "ssm_d": d_d, "cv_dw_w": d_cv_w32[:CONV_K], "cv_dw_b": d_cv_b, "cv_ln_g": d_cv_ln_g, "cv_ln_b": d_cv_ln_b,
        "ln1_g": d_ln1_g, "ln1_b": d_ln1_b,
        "ffn_dw_w": jnp.concatenate([d_ffn_wa[:FFN_K], d_ffn_wv[:FFN_K]], axis=1),
        "ffn_dw_b": jnp.concatenate([d_ffn_ba, d_ffn_bv], axis=1), "ln2_g": d_ln2_g, "ln2_b": d_ln2_b,
        "b_cond": dmod, "loss": loss_part,
    }
    small_names = list(small)
    small_shapes = [small[n].shape for n in small_names]
    small_all, = _exchange([_pack([small[n] for n in small_names])], scatter=False, name="gather_small")
    small_sum = dict(zip(small_names, _unpack(_sum_parts(small_all, name="sum_small").reshape(-1), small_shapes)))
    dmod_all = _unpack(small_all.reshape(NDEV, -1), small_shapes)[small_names.index("b_cond")].reshape(NDEV, 6 * d)
    dmod_cols = lax.dynamic_slice(dmod_all.reshape(NDEV, NDEV, ncond), (0, me, 0), (NDEV, 1, ncond)).reshape(NDEV, ncond)
    grads["w_cond"] = _cond_bwd(c_all, dmod_cols, name="cond_bwd")
    loss = small_sum.pop("loss").reshape(())
    for n, g in small_sum.items():
        grads[n] = g
    ntap = cv_dw_w.shape[3]
    grads["cv_dw_w"] = lax.dynamic_slice(grads["cv_dw_w"], (0, me * ntap), (CONV_K, ntap))
    nffn = ffn_dw_w.shape[3]
    grads["ffn_dw_w"] = lax.dynamic_slice(grads["ffn_dw_w"], (0, me * nffn), (FFN_K, nffn))
    grads = {n: grads[n].reshape(weights[n].shape) for n in names}

    delta["w_cond"], new_m["w_cond"], new_v["w_cond"] = _adamw(w_cond[0], grads["w_cond"][0], m_w_cond[0], v_w_cond[0],
                                                               name="adamw_w_cond")
    rest = [n for n in names if n not in ["w_cond"] + big]
    rest_shapes = [weights[n].shape for n in rest]
    packed = [_pack([t[n] for n in rest]) for t in (weights, grads, mom_m, mom_v)]
    for tgt_dict, res in zip((delta, new_m, new_v), _adamw(*packed, name="adamw_small")):
        for n, a in zip(rest, _unpack(res.reshape(-1), rest_shapes)):
            tgt_dict[n] = a
    shaped = lambda t: [t[n].reshape(weights[n].shape) for n in names]

    return (loss, grad_x[None], *shaped(grads), *shaped(delta), *shaped(new_m), *shaped(new_v))
```

```python
import functools
import math

import jax
import jax.numpy as jnp
from jax import lax
from jax.experimental import pallas as pl
from jax.experimental.pallas import tpu as pltpu

f32 = jnp.float32
bf16 = jnp.bfloat16

NDEV = 8
LANES = 128
SUBLANES = 8
SSM_GROUP = 16
SSM_STATE = 64
QW = 128
QS = 512
CONV_K = 31
CONV_HALO = 32
FFN_K = 3
FFN_HALO = 8
LN_EPS = 1e-5
ADAM_LR, ADAM_B1, ADAM_B2, ADAM_EPS, ADAM_WD, ADAM_STEP = 0.001, 0.9, 0.999, 1e-08, 0.01, 10
VMEM_LIMIT = 56 * 1024 * 1024
W_TILE_BYTES = 6 * 1024 * 1024
SUM_ROWS = 512
EW_BLOCK_BYTES = 2 * 1024 * 1024
INV_SQRT2 = 1.0 / math.sqrt(2.0)
INV_SQRT_2PI = 1.0 / math.sqrt(2.0 * math.pi)
MESH = pl.DeviceIdType.MESH


def _tile(n, want):
    t = min(n, want)
    while n % t:
        t //= 2
    return t


def _col_tile(n, rows, bytes_per):
    best = LANES if n % LANES == 0 else n
    for t in range(LANES, n + 1, LANES):
        if n % t == 0 and rows * t * bytes_per <= W_TILE_BYTES:
            best = t
    return best


def _params(*sem):
    return pltpu.CompilerParams(dimension_semantics=sem, vmem_limit_bytes=VMEM_LIMIT)


def _row(i):
    return (0, 0)


def _full(shape):
    nd = len(shape)
    return pl.BlockSpec(shape, lambda *a: (0,) * nd)


def _ln(x):
    mu = jnp.mean(x, axis=-1, keepdims=True)
    xc = x - mu
    var = jnp.mean(xc * xc, axis=-1, keepdims=True)
    rstd = lax.rsqrt(var + LN_EPS)
    return xc * rstd, rstd


def _ln_bwd(dxhat, xhat, rstd):
    return rstd * (dxhat - jnp.mean(dxhat, axis=-1, keepdims=True) - xhat * jnp.mean(dxhat * xhat, axis=-1, keepdims=True))


def _sig(x):
    return 1.0 / (1.0 + jnp.exp(-x))


def _gelu(x):
    return 0.5 * x * (1.0 + lax.erf(x * INV_SQRT2))


def _gelu_grad(x):
    return 0.5 * (1.0 + lax.erf(x * INV_SQRT2)) + x * jnp.exp(-0.5 * x * x) * INV_SQRT_2PI


def _colsum(x):
    return jnp.sum(x, axis=0, keepdims=True)


def _mm(pairs, bias=None, *, trans_w=False, out_dtype=f32, name):
    n_p = len(pairs)
    m = pairs[0][0].shape[0]
    n = pairs[0][1].shape[0 if trans_w else 1]
    ktot = sum(x.shape[1] for x, _ in pairs)
    tm = _tile(m, 512)
    tn = _col_tile(n, ktot, 2)
    dn = (((1,), (1,)), ((), ())) if trans_w else (((1,), (0,)), ((), ()))

    def body(*refs):
        o_ref = refs[-1]
        acc = None
        for xr, wr in zip(refs[:n_p], refs[n_p:2 * n_p]):
            r = lax.dot_general(xr[...].astype(bf16), wr[...].astype(bf16), dn, preferred_element_type=f32)
            acc = r if acc is None else acc + r
        if bias is not None:
            acc = acc + refs[2 * n_p][...]
        o_ref[...] = acc.astype(out_dtype)

    in_specs = [pl.BlockSpec((tm, x.shape[1]), lambda j, i: (i, 0)) for x, _ in pairs]
    if trans_w:
        in_specs += [pl.BlockSpec((tn, w.shape[1]), lambda j, i: (j, 0)) for _, w in pairs]
    else:
        in_specs += [pl.BlockSpec((w.shape[0], tn), lambda j, i: (0, j)) for _, w in pairs]
    args = [x for x, _ in pairs] + [w for _, w in pairs]
    if bias is not None:
        in_specs.append(pl.BlockSpec((1, tn), lambda j, i: (0, j)))
        args.append(bias)
    return pl.pallas_call(
        body, name=name, grid=(n // tn, m // tm), in_specs=in_specs,
        out_specs=pl.BlockSpec((tm, tn), lambda j, i: (i, j)),
        out_shape=jax.ShapeDtypeStruct((m, n), out_dtype),
        compiler_params=_params("parallel", "arbitrary"),
    )(*args)


def _mm_tn(x, dy, *, name):
    m, k = x.shape
    n = dy.shape[1]
    tm = _tile(m, 512)
    tn = _col_tile(n, k, 4)

    def body(x_ref, dy_ref, o_ref):
        @pl.when(pl.program_id(1) == 0)
        def _():
            o_ref[...] = jnp.zeros_like(o_ref)

        o_ref[...] += lax.dot_general(x_ref[...].astype(bf16), dy_ref[...].astype(bf16), (((0,), (0,)), ((), ())),
                                      preferred_element_type=f32)

    return pl.pallas_call(
        body, name=name, grid=(n // tn, m // tm),
        in_specs=[pl.BlockSpec((tm, k), lambda j, i: (i, 0)), pl.BlockSpec((tm, tn), lambda j, i: (i, j))],
        out_specs=pl.BlockSpec((k, tn), lambda j, i: (0, j)),
        out_shape=jax.ShapeDtypeStruct((k, n), f32),
        compiler_params=_params("parallel", "arbitrary"),
    )(x, dy)


def _exchange(arrs, *, scatter, name):
    n = len(arrs)

    def body(*refs):
        _exchange_copies(refs[:n], refs[n:2 * n], refs[2 * n:], scatter, True, True)

    return pl.pallas_call(
        body, name=name, in_specs=[HBM_SPEC] * n, out_specs=[HBM_SPEC] * n, out_shape=_exchange_out_shape(arrs, scatter),
        scratch_shapes=_exchange_sems(n),
    )(*arrs)


HBM_SPEC = pl.BlockSpec(memory_space=pltpu.HBM)


def _exchange_out_shape(arrs, scatter):
    return [jax.ShapeDtypeStruct(a.shape if scatter else (NDEV,) + a.shape, a.dtype) for a in arrs]


def _exchange_sems(n):
    return [pltpu.SemaphoreType.DMA(((NDEV - 1) * n,)), pltpu.SemaphoreType.DMA(((NDEV - 1) * n,)), pltpu.SemaphoreType.DMA((n,))]


def _exchange_copies(x_refs, o_refs, sems, scatter, start, wait):
    n = len(x_refs)
    send_sems, recv_sems, local_sems = sems
    ix, iy, ic = lax.axis_index("x"), lax.axis_index("y"), lax.axis_index("c")
    me = 4 * ix + 2 * iy + ic
    local = [pltpu.make_async_copy(x.at[me] if scatter else x, o.at[me], local_sems.at[a])
             for a, (x, o) in enumerate(zip(x_refs, o_refs))]
    sends, recvs = [], []
    for k in range(1, NDEV):
        px, py, pc = (1 - ix if k & 4 else ix, 1 - iy if k & 2 else iy, 1 - ic if k & 1 else ic)
        them = 4 * px + 2 * py + pc
        for a, (x, o) in enumerate(zip(x_refs, o_refs)):
            sem = (k - 1) * n + a
            sends.append(pltpu.make_async_remote_copy(
                src_ref=x.at[them] if scatter else x, dst_ref=o.at[me], send_sem=send_sems.at[sem], recv_sem=recv_sems.at[sem],
                device_id=(px, py, pc), device_id_type=MESH))
            recvs.append(pltpu.make_async_remote_copy(
                src_ref=x.at[me] if scatter else x, dst_ref=o.at[them], send_sem=send_sems.at[sem], recv_sem=recv_sems.at[sem],
                device_id=(px, py, pc), device_id_type=MESH))
    if start:
        for cp in local + sends:
            cp.start()
    if wait:
        for cp in recvs:
            cp.wait_recv()
        for cp in sends:
            cp.wait_send()
        for cp in local:
            cp.wait()


def _call(body, args, *, name, grid, in_specs, out_specs, out_shape, scratch_shapes=(), sem, xchg=None):
    if xchg is None:
        return pl.pallas_call(body, name=name, grid=grid, in_specs=in_specs, out_specs=out_specs, out_shape=out_shape,
                              scratch_shapes=list(scratch_shapes), compiler_params=_params(*sem))(*args), None
    arrs, scatter = xchg
    n, ni, no, ns = len(arrs), len(in_specs), len(out_specs), len(scratch_shapes)

    def wrapped(*refs):
        ins, x_refs = refs[:ni], refs[ni:ni + n]
        outs, o_refs = refs[ni + n:ni + n + no], refs[ni + n + no:ni + 2 * n + no]
        scratch, sems = refs[ni + 2 * n + no:ni + 2 * n + no + ns], refs[ni + 2 * n + no + ns:]
        ids = [pl.program_id(a) for a in range(len(grid))]
        first = functools.reduce(jnp.logical_and, [p == 0 for p in ids])
        last = functools.reduce(jnp.logical_and, [p == g - 1 for p, g in zip(ids, grid)])

        @pl.when(first)
        def _():
            _exchange_copies(x_refs, o_refs, sems, scatter, True, False)

        body(*ins, *outs, *scratch)

        @pl.when(last)
        def _():
            _exchange_copies(x_refs, o_refs, sems, scatter, False, True)

    res = pl.pallas_call(
        wrapped, name=name, grid=grid, in_specs=list(in_specs) + [HBM_SPEC] * n, out_specs=list(out_specs) + [HBM_SPEC] * n,
        out_shape=list(out_shape) + _exchange_out_shape(arrs, scatter),
        scratch_shapes=list(scratch_shapes) + _exchange_sems(n),
        compiler_params=_params(*("arbitrary",) * len(grid)))(*args, *arrs)
    return res[:no], res[no:]


def _sum_parts(parts, *, name):
    r = parts.shape[1]

    def body(p_ref, o_ref):
        acc = p_ref[0]
        for j in range(1, NDEV):
            acc = acc + p_ref[j]
        o_ref[...] = acc

    return pl.pallas_call(body, name=name, out_shape=jax.ShapeDtypeStruct((r, LANES), f32), compiler_params=_params())(parts)


def _col_pieces(n, bounds):
    out = []
    for p, (a, b) in enumerate(bounds):
        for j in range(NDEV):
            lo, hi = max(a, n * j), min(b, n * (j + 1))
            if lo < hi:
                out.append((p, j, lo - a, lo - n * j, hi - lo))
    return out


def _unshard_cols(stacked, bounds, *, name):
    _, k, n = stacked.shape
    tk = _tile(k, 256)
    plan = _col_pieces(n, bounds)

    def body(x_ref, *o_refs):
        for p, j, po, so, w in plan:
            o_refs[p][:, po:po + w] = x_ref[j, :, so:so + w]

    return pl.pallas_call(
        body, name=name, grid=(k // tk,), in_specs=[pl.BlockSpec((NDEV, tk, n), lambda i: (0, i, 0))],
        out_specs=[pl.BlockSpec((tk, b - a), lambda i: (i, 0)) for a, b in bounds],
        out_shape=[jax.ShapeDtypeStruct((k, b - a), stacked.dtype) for a, b in bounds],
        compiler_params=_params("parallel"))(stacked)


def _shard_cols(pieces, *, name):
    k = pieces[0].shape[0]
    bounds, off = [], 0
    for p in pieces:
        bounds.append((off, off + p.shape[1]))
        off += p.shape[1]
    n = off // NDEV
    tk = _tile(k, 256)
    plan = _col_pieces(n, bounds)

    def body(*refs):
        o_ref = refs[-1]
        for p, j, po, so, w in plan:
            o_ref[j, :, so:so + w] = refs[p][:, po:po + w]

    return pl.pallas_call(
        body, name=name, grid=(k // tk,), in_specs=[pl.BlockSpec((tk, b - a), lambda i: (i, 0)) for a, b in bounds],
        out_specs=pl.BlockSpec((NDEV, tk, n), lambda i: (0, i, 0)),
        out_shape=jax.ShapeDtypeStruct((NDEV, k, n), pieces[0].dtype),
        compiler_params=_params("parallel"))(*pieces)


def _pack(arrs):
    flat = jnp.concatenate([a.reshape(-1) for a in arrs])
    pad = (-flat.shape[0]) % (SUBLANES * LANES)
    return jnp.pad(flat, (0, pad)).reshape(-1, LANES)


def _unpack(flat, shapes):
    out, off = [], 0
    for s in shapes:
        n = math.prod(s)
        out.append(flat[..., off:off + n].reshape(flat.shape[:-1] + tuple(s)))
        off += n
    return out


def _adamw_math(w, gg, m, v):
    nm = ADAM_B1 * m + (1.0 - ADAM_B1) * gg
    nv = ADAM_B2 * v + (1.0 - ADAM_B2) * (gg * gg)
    m_hat = nm / (1.0 - ADAM_B1 ** ADAM_STEP)
    v_hat = nv / (1.0 - ADAM_B2 ** ADAM_STEP)
    return -ADAM_LR * (m_hat / (jnp.sqrt(v_hat) + ADAM_EPS) + ADAM_WD * w), nm, nv


def _row_block(r, c, copies):
    tr = r
    while copies * tr * c * 4 > EW_BLOCK_BYTES and tr % (2 * SUBLANES) == 0:
        tr //= 2
    return tr


def _adamw(w, g, m, v, *, name):
    r, c = w.shape
    tr = _row_block(r, c, 1)

    def body(w_ref, g_ref, m_ref, v_ref, d_ref, nm_ref, nv_ref):
        d_ref[...], nm_ref[...], nv_ref[...] = _adamw_math(w_ref[...], g_ref[...], m_ref[...], v_ref[...])

    spec = pl.BlockSpec((tr, c), lambda i: (i, 0))
    shp = jax.ShapeDtypeStruct((r, c), f32)
    return pl.pallas_call(
        body, name=name, grid=(r // tr,), in_specs=[spec] * 4, out_specs=[spec] * 3, out_shape=[shp] * 3,
        compiler_params=_params("parallel"),
    )(w, g, m, v)


def _sum_adamw(parts, w, m, v, *, name):
    r, c = w.shape
    tr = _row_block(r, c, NDEV)

    def body(p_ref, w_ref, m_ref, v_ref, g_ref, d_ref, nm_ref, nv_ref):
        gg = p_ref[0]
        for j in range(1, NDEV):
            gg = gg + p_ref[j]
        g_ref[...] = gg
        d_ref[...], nm_ref[...], nv_ref[...] = _adamw_math(w_ref[...], gg, m_ref[...], v_ref[...])

    spec = pl.BlockSpec((tr, c), lambda i: (i, 0))
    shp = jax.ShapeDtypeStruct((r, c), f32)
    return pl.pallas_call(
        body, name=name, grid=(r // tr,), in_specs=[pl.BlockSpec((NDEV, tr, c), lambda i: (0, i, 0))] + [spec] * 3,
        out_specs=[spec] * 4, out_shape=[shp] * 4, compiler_params=_params("parallel"),
    )(parts, w, m, v)


def _cond_fwd(c_all, w, b, *, name):
    nb, n = c_all.shape[0], w.shape[1]

    def body(c_ref, w_ref, b_ref, o_ref):
        cc = c_ref[...]
        o_ref[...] = jnp.dot(cc * _sig(cc), w_ref[...], preferred_element_type=f32,
                             precision=lax.Precision.HIGHEST) + b_ref[...]

    return pl.pallas_call(body, name=name, out_shape=jax.ShapeDtypeStruct((nb, n), f32),
                          compiler_params=_params())(c_all, w, b)


def _cond_bwd(c_all, dmod, *, name):
    d, n = c_all.shape[1], dmod.shape[1]

    def body(c_ref, g_ref, o_ref):
        cc = c_ref[...]
        o_ref[...] = lax.dot_general(cc * _sig(cc), g_ref[...], (((0,), (0,)), ((), ())), preferred_element_type=f32,
                                     precision=lax.Precision.HIGHEST)

    return pl.pallas_call(body, name=name, out_shape=jax.ShapeDtypeStruct((d, n), f32),
                          compiler_params=_params())(c_all, dmod)


def _ssm_disc(lam_re, lam_im, log_dt):
    lr = jnp.minimum(lam_re, -1e-4)
    li = lam_im
    dt = jnp.exp(log_dt)
    mag = jnp.exp(lr * dt)
    ang = li * dt
    lbr, lbi = mag * jnp.cos(ang), mag * jnp.sin(ang)
    num_r, num_i = lbr - 1.0, lbi
    den = lr * lr + li * li
    return lbr, lbi, (num_r * lr + num_i * li) / den, (num_i * lr - num_r * li) / den


def _ssm_prep(lam_re, lam_im, log_dt, *, name):
    def body(a, b, c, o1, o2, o3, o4):
        o1[...], o2[...], o3[...], o4[...] = _ssm_disc(a[...], b[...], c[...])

    shp = jax.ShapeDtypeStruct(lam_re.shape, f32)
    return pl.pallas_call(body, name=name, out_shape=[shp] * 4, compiler_params=_params())(lam_re, lam_im, log_dt)


def _ssm_prep_bwd(lam_re, lam_im, log_dt, cts, *, name):
    def body(a, b, c, g1, g2, g3, g4, o1, o2, o3):
        _, vjp = jax.vjp(_ssm_disc, a[...], b[...], c[...])
        o1[...], o2[...], o3[...] = vjp((g1[...], g2[...], g3[...], g4[...]))

    shp = jax.ShapeDtypeStruct(lam_re.shape, f32)
    return pl.pallas_call(body, name=name, out_shape=[shp, shp, jax.ShapeDtypeStruct(log_dt.shape, f32)],
                          compiler_params=_params())(lam_re, lam_im, log_dt, *cts)


def _step_major(tb, nt, dtype):
    r = lax.broadcasted_iota(jnp.int32, (tb, tb), 0)
    k = lax.broadcasted_iota(jnp.int32, (tb, tb), 1)
    return (k == (r % SUBLANES) * nt + r // SUBLANES).astype(dtype)


def _chunk_major(tb, nt, dtype):
    k = lax.broadcasted_iota(jnp.int32, (tb, tb), 0)
    r = lax.broadcasted_iota(jnp.int32, (tb, tb), 1)
    return (k == (r % SUBLANES) * nt + r // SUBLANES).astype(dtype)


def _permute_f32(pmat, x):
    return jnp.dot(pmat, x, preferred_element_type=f32, precision=lax.Precision.HIGHEST)


def _permute_bf16(pmat, x):
    return jnp.dot(pmat, x, preferred_element_type=f32).astype(bf16)


def _chain_carries(loc_r, loc_i, pr, pi_, forward):
    row = lax.broadcasted_iota(jnp.int32, loc_r.shape, 0)
    shift = 1 if forward else SUBLANES - 1
    order = range(1, SUBLANES) if forward else range(SUBLANES - 2, -1, -1)
    er, ei = loc_r, loc_i
    for k in order:
        sr, si = pltpu.roll(er, shift, 0), pltpu.roll(ei, shift, 0)
        er = jnp.where(row == k, loc_r + pr * sr - pi_ * si, er)
        ei = jnp.where(row == k, loc_i + pr * si + pi_ * sr, ei)
    edge = 0 if forward else SUBLANES - 1
    return (jnp.where(row == edge, 0.0, pltpu.roll(er, shift, 0)), jnp.where(row == edge, 0.0, pltpu.roll(ei, shift, 0)))


def _chunk_power(ar, ai, chunk_len):
    pr, pi_ = ar, ai
    for _ in range(int(math.log2(chunk_len))):
        pr, pi_ = pr * pr - pi_ * pi_, 2.0 * pr * pi_
    return pr, pi_


def _ssm_mats(bre_ref, bim_ref, cre_ref, cim_ref, cfr_ref, cfi_ref, bbar_s, cmat_s, nq):
    for q in range(nq):
        cr, ci, br, bi = cfr_ref[q], cfi_ref[q], bre_ref[q], bim_ref[q]
        bbar_s[q, :, 0:QS] = (cr * br - ci * bi).astype(bf16)
        bbar_s[q, :, QS:2 * QS] = (cr * bi + ci * br).astype(bf16)
        cmat_s[q, 0:QS, :] = cre_ref[q].astype(bf16)
        cmat_s[q, QS:2 * QS, :] = (-cim_ref[q]).astype(bf16)


def _ssm_fwd(u, ar, ai, bre, bim, cre, cim, cfr, cfi, dvec, *, name, xchg=None):
    s, sw = u.shape
    nq = sw // QW
    st = nq * 2 * QS
    tb = _tile(s, 256)
    nb, nt, chunk_len = s // tb, tb // SUBLANES, s // SUBLANES
    assert chunk_len & (chunk_len - 1) == 0 and nt % 16 == 0

    def body(u_ref, ar_ref, ai_ref, bre_ref, bim_ref, cre_ref, cim_ref, cfr_ref, cfi_ref, d_ref,
             h_out, yraw_out, y_out, buf, hc, bbar_s, cmat_s):
        ph, i = pl.program_id(0), pl.program_id(1)

        @pl.when(i == 0)
        def _():
            _ssm_mats(bre_ref, bim_ref, cre_ref, cim_ref, cfr_ref, cfi_ref, bbar_s, cmat_s, nq)

        @pl.when((ph == 0) & (i == 0))
        def _():
            hc[...] = jnp.zeros_like(hc)

        @pl.when((ph == 1) & (i == 0))
        def _():
            for q in range(nq):
                o = q * 2 * QS
                pr, pi_ = _chunk_power(ar_ref[q], ai_ref[q], chunk_len)
                sr, si = _chain_carries(hc[:, o:o + QS], hc[:, o + QS:o + 2 * QS], pr, pi_, True)
                hc[:, o:o + QS] = sr
                hc[:, o + QS:o + 2 * QS] = si

        uu = u_ref[...].reshape(tb, sw)
        up = _permute_bf16(_step_major(tb, nt, bf16), uu.astype(bf16))
        for q in range(nq):
            o = q * 2 * QS
            buf[:, o:o + 2 * QS] = jnp.dot(up[:, q * QW:(q + 1) * QW], bbar_s[q], preferred_element_type=f32)

        for q in range(nq):
            o = q * 2 * QS
            a_r = jnp.broadcast_to(ar_ref[q], (SUBLANES, QS))
            a_i = jnp.broadcast_to(ai_ref[q], (SUBLANES, QS))

            def step(t, carry, o=o, a_r=a_r, a_i=a_i):
                hr, hi = carry
                r0 = pl.multiple_of(t * SUBLANES, SUBLANES)
                nr = a_r * hr - a_i * hi + buf[pl.ds(r0, SUBLANES), o:o + QS]
                ni = a_r * hi + a_i * hr + buf[pl.ds(r0, SUBLANES), o + QS:o + 2 * QS]
                buf[pl.ds(r0, SUBLANES), o:o + QS] = nr
                buf[pl.ds(r0, SUBLANES), o + QS:o + 2 * QS] = ni
                return nr, ni

            hr, hi = lax.fori_loop(0, nt, step, (hc[:, o:o + QS], hc[:, o + QS:o + 2 * QS]))
            hc[:, o:o + QS] = hr
            hc[:, o + QS:o + 2 * QS] = hi

        @pl.when(ph == 1)
        def _():
            back = _chunk_major(tb, nt, f32)
            for q in range(nq):
                o = q * 2 * QS
                cs = slice(q * QW, (q + 1) * QW)
                hq = buf[:, o:o + 2 * QS].astype(bf16)
                h_out[:, o:o + 2 * QS] = hq
                yq = _permute_f32(back, jnp.dot(hq, cmat_s[q], preferred_element_type=f32)) + d_ref[:, cs] * uu[:, cs]
                yraw_out[:, :, cs] = yq.reshape(SUBLANES, nt, QW)
                y_out[:, :, cs] = _gelu(yq).astype(bf16).reshape(SUBLANES, nt, QW)

    blk = lambda ph, i: (0, i, 0)
    oblk = lambda ph, i: (0, i * ph, 0)
    act = lambda dt: jax.ShapeDtypeStruct((SUBLANES, chunk_len, sw), dt)
    (h_p, yraw3, y3), got = _call(
        body, (u.reshape(SUBLANES, chunk_len, sw), ar, ai, bre, bim, cre, cim, cfr, cfi, dvec), name=name, grid=(2, nb),
        in_specs=[pl.BlockSpec((SUBLANES, nt, sw), blk), _full(ar.shape), _full(ai.shape), _full(bre.shape), _full(bim.shape),
                  _full(cre.shape), _full(cim.shape), _full(cfr.shape), _full(cfi.shape), _full(dvec.shape)],
        out_specs=[pl.BlockSpec((tb, st), lambda ph, i: (i * ph, 0)), pl.BlockSpec((SUBLANES, nt, sw), oblk),
                   pl.BlockSpec((SUBLANES, nt, sw), oblk)],
        out_shape=[jax.ShapeDtypeStruct((s, st), bf16), act(f32), act(bf16)],
        scratch_shapes=[pltpu.VMEM((tb, st), f32), pltpu.VMEM((SUBLANES, st), f32),
                        pltpu.VMEM((nq, QW, 2 * QS), bf16), pltpu.VMEM((nq, 2 * QS, QW), bf16)],
        sem=("arbitrary", "arbitrary"), xchg=xchg)
    return h_p, yraw3, y3.reshape(s, sw), got


def _ssm_bwd(dy, yraw3, u, h_p, ar, ai, bre, bim, cre, cim, cfr, cfi, dvec, *, name, xchg=None):
    s, sw = u.shape
    nq = sw // QW
    st = nq * 2 * QS
    tb = _tile(s, 256)
    nb, nt, chunk_len = s // tb, tb // SUBLANES, s // SUBLANES

    def body(dy_ref, yraw_ref, u_ref, h_ref, ar_ref, ai_ref, bre_ref, bim_ref, cre_ref, cim_ref, cfr_ref, cfi_ref, d_ref,
             du_out, dbre_out, dbim_out, dcre_out, dcim_out, dcfr_out, dcfi_out, dlbr_out, dlbi_out, dd_out, dbu_out,
             buf, hf, rc, acc, dbbar, dcmat, bbar_s, cmat_s):
        ph, i = pl.program_id(0), pl.program_id(1)

        @pl.when(i == 0)
        def _():
            _ssm_mats(bre_ref, bim_ref, cre_ref, cim_ref, cfr_ref, cfi_ref, bbar_s, cmat_s, nq)

        @pl.when((ph == 0) & (i == 0))
        def _():
            rc[...] = jnp.zeros_like(rc)

        @pl.when((ph == 1) & (i == 0))
        def _():
            for q in range(nq):
                o = q * 2 * QS
                pr, pi_ = _chunk_power(ar_ref[q], ai_ref[q], chunk_len)
                sr, si = _chain_carries(rc[:, o:o + QS], rc[:, o + QS:o + 2 * QS], pr, -pi_, False)
                rc[:, o:o + QS] = sr
                rc[:, o + QS:o + 2 * QS] = si
            acc[...] = jnp.zeros_like(acc)
            dbbar[...] = jnp.zeros_like(dbbar)
            dcmat[...] = jnp.zeros_like(dcmat)
            dd_out[...] = jnp.zeros_like(dd_out)
            dbu_out[...] = jnp.zeros_like(dbu_out)

        dyraw = (dy_ref[...] * _gelu_grad(yraw_ref[...])).reshape(tb, sw)
        fwd_perm = _step_major(tb, nt, bf16)
        dyp = _permute_bf16(fwd_perm, dyraw.astype(bf16))
        for q in range(nq):
            o = q * 2 * QS
            buf[:, o:o + 2 * QS] = lax.dot_general(dyp[:, q * QW:(q + 1) * QW], cmat_s[q], (((1,), (1,)), ((), ())),
                                                   preferred_element_type=f32)

        def recur(with_grad):
            for q in range(nq):
                o = q * 2 * QS
                a_r = jnp.broadcast_to(ar_ref[q], (SUBLANES, QS))
                a_i = jnp.broadcast_to(ai_ref[q], (SUBLANES, QS))

                def step(j, carry, o=o, a_r=a_r, a_i=a_i):
                    r0 = pl.multiple_of((nt - 1 - j) * SUBLANES, SUBLANES)
                    if with_grad:
                        rr, ri, gr, gi = carry
                        hr = hf[pl.ds(r0, SUBLANES), o:o + QS]
                        hi = hf[pl.ds(r0, SUBLANES), o + QS:o + 2 * QS]
                        gr = gr + hr * rr + hi * ri
                        gi = gi + hr * ri - hi * rr
                    else:
                        rr, ri = carry
                    nr = buf[pl.ds(r0, SUBLANES), o:o + QS] + a_r * rr + a_i * ri
                    ni = buf[pl.ds(r0, SUBLANES), o + QS:o + 2 * QS] + a_r * ri - a_i * rr
                    buf[pl.ds(r0, SUBLANES), o:o + QS] = nr
                    buf[pl.ds(r0, SUBLANES), o + QS:o + 2 * QS] = ni
                    return (nr, ni, gr, gi) if with_grad else (nr, ni)

                init = (rc[:, o:o + QS], rc[:, o + QS:o + 2 * QS])
                if with_grad:
                    init = init + (acc[:, o:o + QS], acc[:, o + QS:o + 2 * QS])
                res = lax.fori_loop(0, nt, step, init)
                rc[:, o:o + QS] = res[0]
                rc[:, o + QS:o + 2 * QS] = res[1]
                if with_grad:
                    acc[:, o:o + QS] = res[2]
                    acc[:, o + QS:o + 2 * QS] = res[3]

        @pl.when(ph == 0)
        def _():
            recur(False)

        @pl.when(ph == 1)
        def _():
            hf[...] = h_ref[...].astype(f32)
            recur(True)
            uu = u_ref[...].reshape(tb, sw)
            up = _permute_bf16(fwd_perm, uu.astype(bf16))
            back = _chunk_major(tb, nt, f32)
            dd_out[...] += _colsum(dyraw * uu)
            for q in range(nq):
                o = q * 2 * QS
                cs = slice(q * QW, (q + 1) * QW)
                lam = buf[:, o:o + 2 * QS].astype(bf16)
                duq = _permute_f32(back, lax.dot_general(lam, bbar_s[q], (((1,), (1,)), ((), ())), preferred_element_type=f32)) \
                    + d_ref[:, cs] * dyraw[:, cs]
                du_out[:, :, cs] = duq.astype(bf16).reshape(SUBLANES, nt, QW)
                dbu_out[:, cs] += _colsum(duq)
                dbbar[q] += lax.dot_general(up[:, cs], lam, (((0,), (0,)), ((), ())), preferred_element_type=f32)
                dcmat[q] += lax.dot_general(h_ref[:, o:o + 2 * QS], dyp[:, cs], (((0,), (0,)), ((), ())),
                                            preferred_element_type=f32)

        @pl.when((ph == 1) & (i == nb - 1))
        def _():
            for q in range(nq):
                o = q * 2 * QS
                cr, ci, br, bi = cfr_ref[q], cfi_ref[q], bre_ref[q], bim_ref[q]
                gr, gi = dbbar[q, :, 0:QS], dbbar[q, :, QS:2 * QS]
                dbre_out[q] = cr * gr + ci * gi
                dbim_out[q] = cr * gi - ci * gr
                dcfr_out[q] = _colsum(gr * br + gi * bi)
                dcfi_out[q] = _colsum(gi * br - gr * bi)
                dcre_out[q] = dcmat[q, 0:QS, :]
                dcim_out[q] = -dcmat[q, QS:2 * QS, :]
                dlbr_out[q] = _colsum(acc[:, o:o + QS])
                dlbi_out[q] = _colsum(acc[:, o + QS:o + 2 * QS])

    blk = lambda ph, i: (0, nb - 1 - i, 0)
    oblk = lambda ph, i: (0, (nb - 1 - i) * ph + (nb - 1) * (1 - ph), 0)
    pshapes = [ar.shape, ai.shape, bre.shape, bim.shape, cre.shape, cim.shape, cfr.shape, cfi.shape, dvec.shape]
    oshapes = [bre.shape, bim.shape, cre.shape, cim.shape, cfr.shape, cfi.shape, ar.shape, ai.shape, dvec.shape, dvec.shape]
    act = pl.BlockSpec((SUBLANES, nt, sw), blk)
    view = lambda a: a.reshape(SUBLANES, chunk_len, sw)
    res, got = _call(
        body, (view(dy), yraw3, view(u), h_p, ar, ai, bre, bim, cre, cim, cfr, cfi, dvec), name=name, grid=(2, nb),
        in_specs=[act, act, act, pl.BlockSpec((tb, st), lambda ph, i: (nb - 1 - i, 0))] + [_full(p) for p in pshapes],
        out_specs=[pl.BlockSpec((SUBLANES, nt, sw), oblk)] + [_full(p) for p in oshapes],
        out_shape=[jax.ShapeDtypeStruct((SUBLANES, chunk_len, sw), bf16)] + [jax.ShapeDtypeStruct(p, f32) for p in oshapes],
        scratch_shapes=[pltpu.VMEM((tb, st), f32), pltpu.VMEM((tb, st), f32),
                        pltpu.VMEM((SUBLANES, st), f32), pltpu.VMEM((SUBLANES, st), f32),
                        pltpu.VMEM((nq, QW, 2 * QS), f32), pltpu.VMEM((nq, 2 * QS, QW), f32),
                        pltpu.VMEM((nq, QW, 2 * QS), bf16), pltpu.VMEM((nq, 2 * QS, QW), bf16)],
        sem=("arbitrary", "arbitrary"), xchg=xchg)
    return (res[0].reshape(s, sw),) + tuple(res[1:]) + (got,)


def _lnmod(x, sc, sh, *, name):
    s, d = x.shape
    tb = _tile(s, 512)

    def body(x_ref, sc_ref, sh_ref, o_ref):
        xh, _ = _ln(x_ref[...])
        o_ref[...] = (xh * (1.0 + sc_ref[...]) + sh_ref[...]).astype(bf16)

    blk = pl.BlockSpec((tb, d), lambda i: (i, 0))
    vec = pl.BlockSpec((1, d), _row)
    return pl.pallas_call(body, name=name, grid=(s // tb,), in_specs=[blk, vec, vec], out_specs=blk,
                          out_shape=jax.ShapeDtypeStruct((s, d), bf16), compiler_params=_params("parallel"))(x, sc, sh)


def _conv_taps(ext_ref, w_ref, tb, ntap, off):
    acc = ext_ref[pl.ds(off, tb), :] * w_ref[pl.ds(0, 1), :]
    for k in range(1, ntap):
        acc = acc + ext_ref[pl.ds(off + k, tb), :] * w_ref[pl.ds(k, 1), :]
    return acc


def _conv_halo_specs(tb, cw, halo, s):
    per = tb // halo
    prev = pl.BlockSpec((halo, cw), lambda i: (jnp.maximum(i * per - 1, 0), 0))
    nxt = pl.BlockSpec((halo, cw), lambda i: (jnp.minimum((i + 1) * per, s // halo - 1), 0))
    return prev, nxt


def _conv_v2(a_ref, g_ref, ah_ref, gh_ref, w_ref, b_ref, ext, tb, i):
    gg = g_ref[...]
    ext[pl.ds(CONV_HALO, tb), :] = a_ref[...] * _sig(gg)
    ext[pl.ds(0, CONV_HALO), :] = jnp.where(i > 0, ah_ref[...] * _sig(gh_ref[...]), 0.0)
    return _conv_taps(ext, w_ref, tb, CONV_K, CONV_HALO - CONV_K + 1) + b_ref[...]


def _silu_grad(x):
    sg = _sig(x)
    return sg * (1.0 + x * (1.0 - sg))


def _conv_fwd(cva, cvg, w, b, lng, lnb, *, name, xchg=None):
    s, cw = cva.shape
    tb = _tile(s, 256)
    prev, _ = _conv_halo_specs(tb, cw, CONV_HALO, s)

    def body(a_ref, g_ref, ah_ref, gh_ref, w_ref, b_ref, lng_ref, lnb_ref, o_ref, ext):
        v2 = _conv_v2(a_ref, g_ref, ah_ref, gh_ref, w_ref, b_ref, ext, tb, pl.program_id(0))
        xh, _ = _ln(v2)
        v3 = xh * lng_ref[...] + lnb_ref[...]
        o_ref[...] = (v3 * _sig(v3)).astype(bf16)

    blk = pl.BlockSpec((tb, cw), lambda i: (i, 0))
    vec = pl.BlockSpec((1, cw), _row)
    (v4,), got = _call(
        body, (cva, cvg, cva, cvg, w, b, lng, lnb), name=name, grid=(s // tb,),
        in_specs=[blk, blk, prev, prev, _full(w.shape), vec, vec, vec], out_specs=[blk],
        out_shape=[jax.ShapeDtypeStruct((s, cw), bf16)], scratch_shapes=[pltpu.VMEM((tb + CONV_HALO, cw), f32)],
        sem=("parallel",), xchg=xchg)
    return v4, got


def _conv_bwd_ln(dv4, cva, cvg, w, b, lng, lnb, *, name):
    s, cw = cva.shape
    tb = _tile(s, 256)
    prev, _ = _conv_halo_specs(tb, cw, CONV_HALO, s)

    def body(d_ref, a_ref, g_ref, ah_ref, gh_ref, w_ref, b_ref, lng_ref, lnb_ref, o_ref, dg_ref, db_ref, ext):
        i = pl.program_id(0)

        @pl.when(i == 0)
        def _():
            dg_ref[...] = jnp.zeros_like(dg_ref)
            db_ref[...] = jnp.zeros_like(db_ref)

        v2 = _conv_v2(a_ref, g_ref, ah_ref, gh_ref, w_ref, b_ref, ext, tb, i)
        xh, rstd = _ln(v2)
        v3 = xh * lng_ref[...] + lnb_ref[...]
        dv3 = d_ref[...] * _silu_grad(v3)
        dg_ref[...] += _colsum(dv3 * xh)
        db_ref[...] += _colsum(dv3)
        o_ref[...] = _ln_bwd(dv3 * lng_ref[...], xh, rstd)

    blk = pl.BlockSpec((tb, cw), lambda i: (i, 0))
    vec = pl.BlockSpec((1, cw), _row)
    vshape = jax.ShapeDtypeStruct((1, cw), f32)
    return pl.pallas_call(
        body, name=name, grid=(s // tb,), in_specs=[blk, blk, blk, prev, prev, _full(w.shape), vec, vec, vec],
        out_specs=[blk, vec, vec], out_shape=[jax.ShapeDtypeStruct((s, cw), f32), vshape, vshape],
        scratch_shapes=[pltpu.VMEM((tb + CONV_HALO, cw), f32)],
        compiler_params=_params("arbitrary"))(dv4, cva, cvg, cva, cvg, w, b, lng, lnb)


def _conv_bwd_taps(dv2, cva, cvg, w, *, name, xchg=None):
    s, cw = cva.shape
    tb = _tile(s, 256)
    nb = s // tb
    prev, nxt = _conv_halo_specs(tb, cw, CONV_HALO, s)

    def body(d_ref, dn_ref, a_ref, g_ref, ah_ref, gh_ref, w_ref, da_ref, dg_ref, dw_ref, db_ref, sa_ref, sg_ref, ext, dext):
        i = pl.program_id(0)

        @pl.when(i == 0)
        def _():
            for r in (dw_ref, db_ref, sa_ref, sg_ref):
                r[...] = jnp.zeros_like(r)

        aa, gg = a_ref[...], g_ref[...]
        sg = _sig(gg)
        ext[pl.ds(CONV_HALO, tb), :] = aa * sg
        ext[pl.ds(0, CONV_HALO), :] = jnp.where(i > 0, ah_ref[...] * _sig(gh_ref[...]), 0.0)
        dd = d_ref[...]
        dext[pl.ds(0, tb), :] = dd
        dext[pl.ds(tb, CONV_HALO), :] = jnp.where(i < nb - 1, dn_ref[...], 0.0)
        dv = dext[pl.ds(CONV_K - 1, tb), :] * w_ref[pl.ds(0, 1), :]
        for k in range(1, CONV_K):
            dv = dv + dext[pl.ds(CONV_K - 1 - k, tb), :] * w_ref[pl.ds(k, 1), :]
        for k in range(CONV_K):
            dw_ref[pl.ds(k, 1), :] += _colsum(dd * ext[pl.ds(CONV_HALO - CONV_K + 1 + k, tb), :])
        db_ref[...] += _colsum(dd)
        da = dv * sg
        dgate = dv * aa * sg * (1.0 - sg)
        sa_ref[...] += _colsum(da)
        sg_ref[...] += _colsum(dgate)
        da_ref[...] = da.astype(bf16)
        dg_ref[...] = dgate.astype(bf16)

    blk = pl.BlockSpec((tb, cw), lambda i: (i, 0))
    vec = pl.BlockSpec((1, cw), _row)
    vshape = jax.ShapeDtypeStruct((1, cw), f32)
    act = jax.ShapeDtypeStruct((s, cw), bf16)
    res, got = _call(
        body, (dv2, dv2, cva, cvg, cva, cvg, w), name=name, grid=(nb,), in_specs=[blk, nxt, blk, blk, prev, prev, _full(w.shape)],
        out_specs=[blk, blk, _full(w.shape), vec, vec, vec],
        out_shape=[act, act, jax.ShapeDtypeStruct(w.shape, f32), vshape, vshape, vshape],
        scratch_shapes=[pltpu.VMEM((tb + CONV_HALO, cw), f32), pltpu.VMEM((tb + CONV_HALO, cw), f32)],
        sem=("arbitrary",), xchg=xchg)
    return tuple(res) + (got,)


def _glu_merge(ya, yb, ycv, gs, gc, *, name):
    s, d = ya.shape
    tb = _tile(s, 512)

    def body(ya_ref, yb_ref, ycv_ref, gs_ref, gc_ref, o_ref):
        z = ya_ref[...] * _sig(yb_ref[...])
        o_ref[...] = (_sig(gs_ref[...]) * z + _sig(gc_ref[...]) * ycv_ref[...]).astype(bf16)

    blk = pl.BlockSpec((tb, d), lambda i: (i, 0))
    return pl.pallas_call(body, name=name, grid=(s // tb,), in_specs=[blk] * 5, out_specs=blk,
                          out_shape=jax.ShapeDtypeStruct((s, d), bf16), compiler_params=_params("parallel"))(ya, yb, ycv, gs, gc)


def _glu_merge_bwd(dm, ya, yb, ycv, gs, gc, *, name):
    s, d = ya.shape
    tb = _tile(s, 512)

    def body(dm_ref, ya_ref, yb_ref, ycv_ref, gs_ref, gc_ref, dya_ref, dyb_ref, dycv_ref, dgs_ref, dgc_ref, sgs_ref, sgc_ref):
        @pl.when(pl.program_id(0) == 0)
        def _():
            sgs_ref[...] = jnp.zeros_like(sgs_ref)
            sgc_ref[...] = jnp.zeros_like(sgc_ref)

        dmv, yav = dm_ref[...], ya_ref[...]
        sb, ss, scv = _sig(yb_ref[...]), _sig(gs_ref[...]), _sig(gc_ref[...])
        z = yav * sb
        dz = dmv * ss
        dgs = dmv * z * ss * (1.0 - ss)
        dgc = dmv * ycv_ref[...] * scv * (1.0 - scv)
        dya_ref[...] = (dz * sb).astype(bf16)
        dyb_ref[...] = (dz * yav * sb * (1.0 - sb)).astype(bf16)
        dycv_ref[...] = (dmv * scv).astype(bf16)
        dgs_ref[...] = dgs.astype(bf16)
        dgc_ref[...] = dgc.astype(bf16)
        sgs_ref[...] += _colsum(dgs)
        sgc_ref[...] += _colsum(dgc)

    blk = pl.BlockSpec((tb, d), lambda i: (i, 0))
    vec = pl.BlockSpec((1, d), _row)
    act = jax.ShapeDtypeStruct((s, d), bf16)
    vshape = jax.ShapeDtypeStruct((1, d), f32)
    return pl.pallas_call(body, name=name, grid=(s // tb,), in_specs=[blk] * 6, out_specs=[blk] * 5 + [vec, vec],
                          out_shape=[act] * 5 + [vshape, vshape], compiler_params=_params("arbitrary"))(dm, ya, yb, ycv, gs, gc)


def _resid_ln_mod(x, o, g, lng, lnb, sc, sh, alpha, *, name):
    s, d = x.shape
    tb = _tile(s, 512)

    def body(x_ref, o_ref, g_ref, lng_ref, lnb_ref, sc_ref, sh_ref, x1_ref, h_ref):
        xh, _ = _ln(alpha * x_ref[...] + g_ref[...] * o_ref[...])
        x1 = xh * lng_ref[...] + lnb_ref[...]
        x1_ref[...] = x1
        xh1, _ = _ln(x1)
        h_ref[...] = (xh1 * (1.0 + sc_ref[...]) + sh_ref[...]).astype(bf16)

    blk = pl.BlockSpec((tb, d), lambda i: (i, 0))
    vec = pl.BlockSpec((1, d), _row)
    return pl.pallas_call(body, name=name, grid=(s // tb,), in_specs=[blk, blk] + [vec] * 5, out_specs=[blk, blk],
                          out_shape=[jax.ShapeDtypeStruct((s, d), f32), jax.ShapeDtypeStruct((s, d), bf16)],
                          compiler_params=_params("parallel"))(x, o, g, lng, lnb, sc, sh)


def _resid_ln_loss(x1, y2, g, lng, lnb, tgt, alpha, *, name):
    s, d = x1.shape
    tb = _tile(s, 512)

    def body(x1_ref, y_ref, g_ref, lng_ref, lnb_ref, t_ref, dr_ref, dy_ref, loss_ref, dlg_ref, dlb_ref, dg_ref):
        @pl.when(pl.program_id(0) == 0)
        def _():
            for r in (loss_ref, dlg_ref, dlb_ref, dg_ref):
                r[...] = jnp.zeros_like(r)

        yv = y_ref[...]
        xh, rstd = _ln(alpha * x1_ref[...] + g_ref[...] * yv)
        err = xh * lng_ref[...] + lnb_ref[...] - t_ref[...]
        loss_ref[...] += 0.5 * jnp.sum(jnp.sum(err * err, axis=-1, keepdims=True) / d, axis=0, keepdims=True)
        dx2 = err / d
        dlg_ref[...] += _colsum(dx2 * xh)
        dlb_ref[...] += _colsum(dx2)
        dr = _ln_bwd(dx2 * lng_ref[...], xh, rstd)
        dg_ref[...] += _colsum(dr * yv)
        dr_ref[...] = dr
        dy_ref[...] = (g_ref[...] * dr).astype(bf16)

    blk = pl.BlockSpec((tb, d), lambda i: (i, 0))
    vec = pl.BlockSpec((1, d), _row)
    vshape = jax.ShapeDtypeStruct((1, d), f32)
    return pl.pallas_call(
        body, name=name, grid=(s // tb,), in_specs=[blk, blk, vec, vec, vec, blk],
        out_specs=[blk, blk, pl.BlockSpec((1, 1), _row), vec, vec, vec],
        out_shape=[jax.ShapeDtypeStruct((s, d), f32), jax.ShapeDtypeStruct((s, d), bf16),
                   jax.ShapeDtypeStruct((1, 1), f32), vshape, vshape, vshape],
        compiler_params=_params("arbitrary"))(x1, y2, g, lng, lnb, tgt)


def _mid_bwd(dh2, x1, dr2, x, o, g, sc, lng, alpha, *, name):
    s, d = x.shape
    tb = _tile(s, 512)

    def body(dh_ref, x1_ref, dr2_ref, x_ref, o_ref, g_ref, sc_ref, lng_ref,
             dr1_ref, do_ref, dsc_ref, dsh_ref, dlg_ref, dlb_ref, dg_ref):
        @pl.when(pl.program_id(0) == 0)
        def _():
            for r in (dsc_ref, dsh_ref, dlg_ref, dlb_ref, dg_ref):
                r[...] = jnp.zeros_like(r)

        dh = dh_ref[...]
        xh1, rstd1 = _ln(x1_ref[...])
        dsc_ref[...] += _colsum(dh * xh1)
        dsh_ref[...] += _colsum(dh)
        dx1 = alpha * dr2_ref[...] + _ln_bwd(dh * (1.0 + sc_ref[...]), xh1, rstd1)
        ov = o_ref[...]
        xhr, rstdr = _ln(alpha * x_ref[...] + g_ref[...] * ov)
        dlg_ref[...] += _colsum(dx1 * xhr)
        dlb_ref[...] += _colsum(dx1)
        dr1 = _ln_bwd(dx1 * lng_ref[...], xhr, rstdr)
        dg_ref[...] += _colsum(dr1 * ov)
        dr1_ref[...] = dr1
        do_ref[...] = (g_ref[...] * dr1).astype(bf16)

    blk = pl.BlockSpec((tb, d), lambda i: (i, 0))
    vec = pl.BlockSpec((1, d), _row)
    vshape = jax.ShapeDtypeStruct((1, d), f32)
    return pl.pallas_call(
        body, name=name, grid=(s // tb,), in_specs=[blk] * 5 + [vec] * 3, out_specs=[blk, blk] + [vec] * 5,
        out_shape=[jax.ShapeDtypeStruct((s, d), f32), jax.ShapeDtypeStruct((s, d), bf16)] + [vshape] * 5,
        compiler_params=_params("arbitrary"))(dh2, x1, dr2, x, o, g, sc, lng)


def _final_bwd(dh1, x, dr1, sc, alpha, *, name, xchg=None):
    s, d = x.shape
    tb = _tile(s, 512)

    def body(dh_ref, x_ref, dr1_ref, sc_ref, dx_ref, dsc_ref, dsh_ref):
        @pl.when(pl.program_id(0) == 0)
        def _():
            dsc_ref[...] = jnp.zeros_like(dsc_ref)
            dsh_ref[...] = jnp.zeros_like(dsh_ref)

        dh = dh_ref[...]
        xh, rstd = _ln(x_ref[...])
        dsc_ref[...] += _colsum(dh * xh)
        dsh_ref[...] += _colsum(dh)
        dx_ref[...] = alpha * dr1_ref[...] + _ln_bwd(dh * (1.0 + sc_ref[...]), xh, rstd)

    blk = pl.BlockSpec((tb, d), lambda i: (i, 0))
    vec = pl.BlockSpec((1, d), _row)
    vshape = jax.ShapeDtypeStruct((1, d), f32)
    res, got = _call(body, (dh1, x, dr1, sc), name=name, grid=(s // tb,), in_specs=[blk, blk, blk, vec], out_specs=[blk, vec, vec],
                     out_shape=[jax.ShapeDtypeStruct((s, d), f32), vshape, vshape], sem=("arbitrary",), xchg=xchg)
    return tuple(res) + (got,)


def _ffn_specs(s, fh, tb, tc):
    per = tb // FFN_HALO
    blk = pl.BlockSpec((tb, tc), lambda j, i: (i, j))
    prev = pl.BlockSpec((FFN_HALO, tc), lambda j, i: (jnp.maximum(i * per - 1, 0), j))
    nxt = pl.BlockSpec((FFN_HALO, tc), lambda j, i: (jnp.minimum((i + 1) * per, s // FFN_HALO - 1), j))
    taps = pl.BlockSpec((FFN_HALO, tc), lambda j, i: (0, j))
    vec = pl.BlockSpec((1, tc), lambda j, i: (0, j))
    return blk, prev, nxt, taps, vec


def _ffn_mid(upa, upv, wa, wv, ba, bv, *, name):
    s, fh = upa.shape
    tb, tc = _tile(s, 512), _tile(fh, 256)
    blk, prev, _, taps, vec = _ffn_specs(s, fh, tb, tc)
    off = FFN_HALO - FFN_K + 1

    def body(a_ref, v_ref, ah_ref, vh_ref, wa_ref, wv_ref, ba_ref, bv_ref, o_ref, exta, extv):
        first = pl.program_id(1) == 0
        exta[pl.ds(FFN_HALO, tb), :] = a_ref[...]
        extv[pl.ds(FFN_HALO, tb), :] = v_ref[...]
        exta[pl.ds(0, FFN_HALO), :] = jnp.where(first, 0.0, ah_ref[...])
        extv[pl.ds(0, FFN_HALO), :] = jnp.where(first, 0.0, vh_ref[...])
        a2 = _conv_taps(exta, wa_ref, tb, FFN_K, off) + ba_ref[...]
        v2 = _conv_taps(extv, wv_ref, tb, FFN_K, off) + bv_ref[...]
        o_ref[...] = (_gelu(a2) * v2).astype(bf16)

    return pl.pallas_call(
        body, name=name, grid=(fh // tc, s // tb), in_specs=[blk, blk, prev, prev, taps, taps, vec, vec], out_specs=blk,
        out_shape=jax.ShapeDtypeStruct((s, fh), bf16),
        scratch_shapes=[pltpu.VMEM((tb + FFN_HALO, tc), f32)] * 2,
        compiler_params=_params("parallel", "arbitrary"))(upa, upv, upa, upv, wa, wv, ba, bv)


def _ffn_mid_bwd(df, upa, upv, wa, wv, ba, bv, *, name, xchg=None):
    s, fh = upa.shape
    tb, tc = _tile(s, 512), _tile(fh, 256)
    nb = s // tb
    blk, prev, nxt, taps, vec = _ffn_specs(s, fh, tb, tc)
    off = FFN_HALO - FFN_K + 1
    te = tb + FFN_HALO

    def body(df_ref, dfn_ref, a_ref, v_ref, ah_ref, vh_ref, an_ref, vn_ref, wa_ref, wv_ref, ba_ref, bv_ref,
             da_ref, dv_ref, dwa_ref, dwv_ref, dba_ref, dbv_ref, exta, extv, dexta, dextv):
        i = pl.program_id(1)

        @pl.when(i == 0)
        def _():
            for r in (dwa_ref, dwv_ref, dba_ref, dbv_ref):
                r[...] = jnp.zeros_like(r)

        last = i == nb - 1
        for ext, c_ref, h_ref, n_ref in ((exta, a_ref, ah_ref, an_ref), (extv, v_ref, vh_ref, vn_ref)):
            ext[pl.ds(0, FFN_HALO), :] = jnp.where(i == 0, 0.0, h_ref[...])
            ext[pl.ds(FFN_HALO, tb), :] = c_ref[...]
            ext[pl.ds(FFN_HALO + tb, FFN_HALO), :] = jnp.where(last, 0.0, n_ref[...])
        a2 = _conv_taps(exta, wa_ref, te, FFN_K, off) + ba_ref[...]
        v2 = _conv_taps(extv, wv_ref, te, FFN_K, off) + bv_ref[...]
        dexta[pl.ds(0, tb), :] = df_ref[...]
        dexta[pl.ds(tb, FFN_HALO), :] = jnp.where(last, 0.0, dfn_ref[...])
        dfe = dexta[...]
        cdf = 0.5 * (1.0 + lax.erf(a2 * INV_SQRT2))
        dexta[...] = dfe * v2 * (cdf + a2 * jnp.exp(-0.5 * a2 * a2) * INV_SQRT_2PI)
        dextv[...] = dfe * (a2 * cdf)
        for ext, dext, w_ref, dw_ref, db_ref, o_ref in ((exta, dexta, wa_ref, dwa_ref, dba_ref, da_ref),
                                                        (extv, dextv, wv_ref, dwv_ref, dbv_ref, dv_ref)):
            dcur = dext[pl.ds(0, tb), :]
            dup = dext[pl.ds(FFN_K - 1, tb), :] * w_ref[pl.ds(0, 1), :]
            for k in range(1, FFN_K):
                dup = dup + dext[pl.ds(FFN_K - 1 - k, tb), :] * w_ref[pl.ds(k, 1), :]
            o_ref[...] = dup.astype(bf16)
            for k in range(FFN_K):
                dw_ref[pl.ds(k, 1), :] += _colsum(dcur * ext[pl.ds(off + k, tb), :])
            db_ref[...] += _colsum(dcur)

    act = jax.ShapeDtypeStruct((s, fh), bf16)
    wshape = jax.ShapeDtypeStruct((FFN_HALO, fh), f32)
    vshape = jax.ShapeDtypeStruct((1, fh), f32)
    res, got = _call(
        body, (df, df, upa, upv, upa, upv, upa, upv, wa, wv, ba, bv), name=name, grid=(fh // tc, nb),
        in_specs=[blk, nxt, blk, blk, prev, prev, nxt, nxt, taps, taps, vec, vec],
        out_specs=[blk, blk, taps, taps, vec, vec], out_shape=[act, act, wshape, wshape, vshape, vshape],
        scratch_shapes=[pltpu.VMEM((tb + 2 * FFN_HALO, tc), f32)] * 2 + [pltpu.VMEM((te, tc), f32)] * 2,
        sem=("parallel", "arbitrary"), xchg=xchg)
    return tuple(res) + (got,)


def _cols_from_shards(stacked):
    _, k, n = stacked.shape
    return stacked.transpose(1, 0, 2).reshape(k, NDEV * n)


def _pad_rows(w, rows):
    return jnp.pad(w, ((0, rows - w.shape[0]), (0, 0)))


def kernel(x, c, w_cond, b_cond, w_in, b_in, ssm_lambda_re, ssm_lambda_im, ssm_log_dt, ssm_b_re, ssm_b_im, ssm_c_re, ssm_c_im, ssm_d, ssm_glu_w_a, ssm_glu_w_b, cv_dw_w, cv_dw_b, cv_ln_g, cv_ln_b, cv_w_pw, w_out, ln1_g, ln1_b, ffn_w_up, ffn_dw_w, ffn_dw_b, ffn_w_down, ln2_g, ln2_b, loss_target, m_w_cond, m_b_cond, m_w_in, m_b_in, m_ssm_lambda_re, m_ssm_lambda_im, m_ssm_log_dt, m_ssm_b_re, m_ssm_b_im, m_ssm_c_re, m_ssm_c_im, m_ssm_d, m_ssm_glu_w_a, m_ssm_glu_w_b, m_cv_dw_w, m_cv_dw_b, m_cv_ln_g, m_cv_ln_b, m_cv_w_pw, m_w_out, m_ln1_g, m_ln1_b, m_ffn_w_up, m_ffn_dw_w, m_ffn_dw_b, m_ffn_w_down, m_ln2_g, m_ln2_b, v_w_cond, v_b_cond, v_w_in, v_b_in, v_ssm_lambda_re, v_ssm_lambda_im, v_ssm_log_dt, v_ssm_b_re, v_ssm_b_im, v_ssm_c_re, v_ssm_c_im, v_ssm_d, v_ssm_glu_w_a, v_ssm_glu_w_b, v_cv_dw_w, v_cv_dw_b, v_cv_ln_g, v_cv_ln_b, v_cv_w_pw, v_w_out, v_ln1_g, v_ln1_b, v_ffn_w_up, v_ffn_dw_w, v_ffn_dw_b, v_ffn_w_down, v_ln2_g, v_ln2_b):
    weights = dict(w_cond=w_cond, b_cond=b_cond, w_in=w_in, b_in=b_in, ssm_lambda_re=ssm_lambda_re, ssm_lambda_im=ssm_lambda_im, ssm_log_dt=ssm_log_dt, ssm_b_re=ssm_b_re, ssm_b_im=ssm_b_im, ssm_c_re=ssm_c_re, ssm_c_im=ssm_c_im, ssm_d=ssm_d, ssm_glu_w_a=ssm_glu_w_a, ssm_glu_w_b=ssm_glu_w_b, cv_dw_w=cv_dw_w, cv_dw_b=cv_dw_b, cv_ln_g=cv_ln_g, cv_ln_b=cv_ln_b, cv_w_pw=cv_w_pw, w_out=w_out, ln1_g=ln1_g, ln1_b=ln1_b, ffn_w_up=ffn_w_up, ffn_dw_w=ffn_dw_w, ffn_dw_b=ffn_dw_b, ffn_w_down=ffn_w_down, ln2_g=ln2_g, ln2_b=ln2_b)
    mom_m = dict(w_cond=m_w_cond, b_cond=m_b_cond, w_in=m_w_in, b_in=m_b_in, ssm_lambda_re=m_ssm_lambda_re, ssm_lambda_im=m_ssm_lambda_im, ssm_log_dt=m_ssm_log_dt, ssm_b_re=m_ssm_b_re, ssm_b_im=m_ssm_b_im, ssm_c_re=m_ssm_c_re, ssm_c_im=m_ssm_c_im, ssm_d=m_ssm_d, ssm_glu_w_a=m_ssm_glu_w_a, ssm_glu_w_b=m_ssm_glu_w_b, cv_dw_w=m_cv_dw_w, cv_dw_b=m_cv_dw_b, cv_ln_g=m_cv_ln_g, cv_ln_b=m_cv_ln_b, cv_w_pw=m_cv_w_pw, w_out=m_w_out, ln1_g=m_ln1_g, ln1_b=m_ln1_b, ffn_w_up=m_ffn_w_up, ffn_dw_w=m_ffn_dw_w, ffn_dw_b=m_ffn_dw_b, ffn_w_down=m_ffn_w_down, ln2_g=m_ln2_g, ln2_b=m_ln2_b)
    mom_v = dict(w_cond=v_w_cond, b_cond=v_b_cond, w_in=v_w_in, b_in=v_b_in, ssm_lambda_re=v_ssm_lambda_re, ssm_lambda_im=v_ssm_lambda_im, ssm_log_dt=v_ssm_log_dt, ssm_b_re=v_ssm_b_re, ssm_b_im=v_ssm_b_im, ssm_c_re=v_ssm_c_re, ssm_c_im=v_ssm_c_im, ssm_d=v_ssm_d, ssm_glu_w_a=v_ssm_glu_w_a, ssm_glu_w_b=v_ssm_glu_w_b, cv_dw_w=v_cv_dw_w, cv_dw_b=v_cv_dw_b, cv_ln_g=v_cv_ln_g, cv_ln_b=v_cv_ln_b, cv_w_pw=v_cv_w_pw, w_out=v_w_out, ln1_g=v_ln1_g, ln1_b=v_ln1_b, ffn_w_up=v_ffn_w_up, ffn_dw_w=v_ffn_dw_w, ffn_dw_b=v_ffn_dw_b, ffn_w_down=v_ffn_w_down, ln2_g=v_ln2_g, ln2_b=v_ln2_b)
    names = list(weights)

    s, d = x.shape[1], x.shape[2]
    sw = cw = d // 2
    fh = ffn_w_down.shape[1] * NDEV
    ng, nq = sw // SSM_GROUP, sw // QW
    gq = ng // nq
    alpha = 2.0 ** 0.25
    me = 4 * lax.axis_index("x") + 2 * lax.axis_index("y") + lax.axis_index("c")
    xs, tgt = x[0], loss_target[0]

    col_names = ["w_in", "ssm_glu_w_a", "ssm_glu_w_b", "cv_w_pw", "ffn_w_up"]
    row_names = ["w_out", "ffn_w_down"]
    big = col_names + row_names
    sent = lambda ns: [weights[n][0].astype(bf16) for n in ns]
    got_in, got_c, got_cv_taps, got_ffn_taps = _exchange(sent(["w_in"]) + [c, cv_dw_w[0, :, 0], ffn_dw_w[0, :, 0]],
                                                         scatter=False, name="gather_in")
    o1, o2, o3, o4 = sw, sw + cw, sw + 2 * cw, sw + 2 * cw + d
    in_bounds = ((0, o1), (o1, o2), (o2, o3), (o3, o4), (o4, o4 + d))
    w_u, w_cva, w_cvg, w_gs, w_gc = _unshard_cols(got_in, in_bounds, name="unshard_w_in")
    b_u, b_cva, b_cvg, b_gs, b_gc = (b_in[:, a:b] for a, b in in_bounds)
    c_all = got_c.reshape(NDEV, d)
    cv_taps = _cols_from_shards(got_cv_taps)
    ffn_taps = _cols_from_shards(got_ffn_taps)
    cv_w32 = _pad_rows(cv_taps, CONV_HALO)
    ffn_wa, ffn_wv = _pad_rows(ffn_taps[:, :fh], FFN_HALO), _pad_rows(ffn_taps[:, fh:], FFN_HALO)
    ffn_ba, ffn_bv = ffn_dw_b[:, :fh], ffn_dw_b[:, fh:]

    ncond = w_cond.shape[2]
    b_cond_mine = lax.dynamic_slice(b_cond, (0, me * ncond), (1, ncond))
    mod_cols = _cond_fwd(c_all, w_cond[0], b_cond_mine, name="cond_fwd")
    mod_all, = _exchange([mod_cols], scatter=False, name="gather_mod")
    mod_mine = lax.dynamic_slice(mod_all, (0, me, 0), (NDEV, 1, ncond)).reshape(1, 6 * d)
    sh1, sc1, g1, sh2, sc2, g2 = (mod_mine[:, k * d:(k + 1) * d] for k in range(6))

    lam_re, lam_im, log_dt = ssm_lambda_re[0], ssm_lambda_im[0], ssm_log_dt[0][:, None]
    lbr, lbi, cfr, cfi = _ssm_prep(lam_re, lam_im, log_dt, name="ssm_prep")
    rows_q = lambda a: a.reshape(nq, 1, QS)
    eye = jnp.eye(gq, dtype=f32)

    def b_mat(b):
        bt = b.reshape(nq, gq, SSM_STATE, SSM_GROUP).transpose(0, 1, 3, 2)
        return jnp.einsum("qgpn,gh->qgphn", bt, eye).reshape(nq, QW, QS)

    def c_mat(cc):
        ct = cc.reshape(nq, gq, SSM_GROUP, SSM_STATE)
        return jnp.einsum("qgpn,gh->qhngp", ct, eye).reshape(nq, QS, QW)

    def b_unmat(mt):
        return jnp.einsum("qgpgn->qgnp", mt.reshape(nq, gq, SSM_GROUP, gq, SSM_STATE)).reshape(ng, SSM_STATE, SSM_GROUP)

    def c_unmat(mt):
        return jnp.einsum("qgngp->qgpn", mt.reshape(nq, gq, SSM_STATE, gq, SSM_GROUP)).reshape(ng, SSM_GROUP, SSM_STATE)

    ssm_args = (rows_q(lbr), rows_q(lbi), b_mat(ssm_b_re[0]), b_mat(ssm_b_im[0]), c_mat(ssm_c_re[0]), c_mat(ssm_c_im[0]),
                rows_q(cfr), rows_q(cfi), ssm_d[0].reshape(1, sw))

    h1 = _lnmod(xs, sc1, sh1, name="ln_mod1")
    u = _mm([(h1, w_u)], b_u, name="in_u")
    cva = _mm([(h1, w_cva)], b_cva, name="in_cva")
    cvg = _mm([(h1, w_cvg)], b_cvg, name="in_cvg")
    gs = _mm([(h1, w_gs)], b_gs, name="in_gs")
    gc = _mm([(h1, w_gc)], b_gc, name="in_gc")
    v4, (got_a, got_b, got_pw, got_o) = _conv_fwd(
        cva, cvg, cv_w32, cv_dw_b, cv_ln_g, cv_ln_b, name="conv_fwd",
        xchg=(sent(["ssm_glu_w_a", "ssm_glu_w_b", "cv_w_pw", "w_out"]), False))
    h_p, yraw3, y, (got_up, got_dn) = _ssm_fwd(u, *ssm_args, name="ssm_fwd", xchg=(sent(["ffn_w_up", "ffn_w_down"]), False))
    w_a, = _unshard_cols(got_a, ((0, d),), name="unshard_glu_a")
    w_b, = _unshard_cols(got_b, ((0, d),), name="unshard_glu_b")
    w_pw, = _unshard_cols(got_pw, ((0, d),), name="unshard_conv_pw")
    w_upa, w_upv = _unshard_cols(got_up, ((0, fh), (fh, 2 * fh)), name="unshard_ffn_up")
    w_o, w_dn = got_o.reshape(d, d), got_dn.reshape(fh, d)
    ya = _mm([(y, w_a)], name="glu_a")
    yb = _mm([(y, w_b)], name="glu_b")
    ycv = _mm([(v4, w_pw)], name="conv_pw")
    merged = _glu_merge(ya, yb, ycv, gs, gc, name="merge")
    o = _mm([(merged, w_o)], name="out_proj")
    x1, h2 = _resid_ln_mod(xs, o, g1, ln1_g, ln1_b, sc2, sh2, alpha, name="resid_ln1")
    upa = _mm([(h2, w_upa)], name="ffn_up_a")
    upv = _mm([(h2, w_upv)], name="ffn_up_v")
    f = _ffn_mid(upa, upv, ffn_wa, ffn_wv, ffn_ba, ffn_bv, name="ffn_mid")
    y2 = _mm([(f, w_dn)], name="ffn_down")
    dr2, dy2, loss_part, d_ln2_g, d_ln2_b, d_g2 = _resid_ln_loss(x1, y2, g2, ln2_g, ln2_b, tgt, alpha, name="resid_ln2_loss")

    gw = {}
    df = _mm([(dy2, w_dn)], trans_w=True, name="d_ffn_down")
    gw["ffn_w_down"] = _mm_tn(f, dy2, name="g_ffn_down").reshape((NDEV,) + ffn_w_down[0].shape)
    received = {}
    dupa, dupv, d_ffn_wa, d_ffn_wv, d_ffn_ba, d_ffn_bv, (received["ffn_w_down"],) = _ffn_mid_bwd(
        df, upa, upv, ffn_wa, ffn_wv, ffn_ba, ffn_bv, name="ffn_mid_bwd", xchg=([gw["ffn_w_down"]], True))
    dh2 = _mm([(dupa, w_upa), (dupv, w_upv)], trans_w=True, name="d_ffn_up")
    gw["ffn_w_up"] = _shard_cols([_mm_tn(h2, dupa, name="g_ffn_up_a"), _mm_tn(h2, dupv, name="g_ffn_up_v")], name="shard_ffn_up")
    dr1, do, d_sc2, d_sh2, d_ln1_g, d_ln1_b, d_g1 = _mid_bwd(dh2, x1, dr2, xs, o, g1, sc2, ln1_g, alpha, name="mid_bwd")
    dmerged = _mm([(do, w_o)], trans_w=True, name="d_out_proj")
    gw["w_out"] = _mm_tn(merged, do, name="g_out_proj").reshape((NDEV,) + w_out[0].shape)
    dya, dyb, dycv, dgs, dgc, s_gs, s_gc = _glu_merge_bwd(dmerged, ya, yb, ycv, gs, gc, name="merge_bwd")
    dy = _mm([(dya, w_a), (dyb, w_b)], trans_w=True, name="d_glu")
    gw["ssm_glu_w_a"] = _shard_cols([_mm_tn(y, dya, name="g_glu_a")], name="shard_glu_a")
    gw["ssm_glu_w_b"] = _shard_cols([_mm_tn(y, dyb, name="g_glu_b")], name="shard_glu_b")
    dv4 = _mm([(dycv, w_pw)], trans_w=True, name="d_conv_pw")
    gw["cv_w_pw"] = _shard_cols([_mm_tn(v4, dycv, name="g_conv_pw")], name="shard_conv_pw")
    dv2, d_cv_ln_g, d_cv_ln_b = _conv_bwd_ln(dv4, cva, cvg, cv_w32, cv_dw_b, cv_ln_g, cv_ln_b, name="conv_bwd_ln")
    dcva, dcvg, d_cv_w32, d_cv_b, s_cva, s_cvg, (received["ffn_w_up"],) = _conv_bwd_taps(
        dv2, cva, cvg, cv_w32, name="conv_bwd_taps", xchg=([gw["ffn_w_up"]], True))
    late = ["w_out", "ssm_glu_w_a", "ssm_glu_w_b", "cv_w_pw"]
    (du, d_bre_m, d_bim_m, d_cre_m, d_cim_m, d_cfr, d_cfi, d_lbr, d_lbi, d_d, s_u, got_late) = _ssm_bwd(
        dy, yraw3, u, h_p, *ssm_args, name="ssm_bwd", xchg=([gw[n] for n in late], True))
    received.update(zip(late, got_late))
    gshape = lam_re.shape
    d_lam_re, d_lam_im, d_log_dt = _ssm_prep_bwd(
        lam_re, lam_im, log_dt, [a.reshape(gshape) for a in (d_lbr, d_lbi, d_cfr, d_cfi)], name="ssm_prep_bwd")
    dh1 = _mm([(du, w_u), (dcva, w_cva), (dcvg, w_cvg), (dgs, w_gs), (dgc, w_gc)], trans_w=True, name="d_in")
    gw["w_in"] = _shard_cols([_mm_tn(h1, t, name="g_in_" + nm) for nm, t in
                              (("u", du), ("cva", dcva), ("cvg", dcvg), ("gs", dgs), ("gc", dgc))], name="shard_w_in")
    grad_x, d_sc1, d_sh1, (received["w_in"],) = _final_bwd(dh1, xs, dr1, sc1, alpha, name="final_bwd", xchg=([gw["w_in"]], True))

    grads, delta, new_m, new_v = {}, {}, {}, {}
    for n in big:
        grads[n], delta[n], new_m[n], new_v[n] = _sum_adamw(received[n], weights[n][0], mom_m[n][0], mom_v[n][0], name="adamw_" + n)

    dmod = jnp.concatenate([d_sh1, d_sc1, d_g1, d_sh2, d_sc2, d_g2], axis=1)
    small = {
        "b_in": jnp.concatenate([s_u, s_cva, s_cvg, s_gs, s_gc], axis=1),
        "ssm_lambda_re": d_lam_re, "ssm_lambda_im": d_lam_im, "ssm_log_dt": d_log_dt,
        "ssm_b_re": b_unmat(d_bre_m), "ssm_b_im": b_unmat(d_bim_m), "ssm_c_re": c_unmat(d_cre_m), "ssm_c_im": c_unmat(d_cim_m),
        "ssm_d": d_d, "cv_dw_w": d_cv_w32[:CONV_K], "cv_dw_b": d_cv_b, "cv_ln_g": d_cv_ln_g, "cv_ln_b": d_cv_ln_b,
        "ln1_g": d_ln1_g, "ln1_b": d_ln1_b,
        "ffn_dw_w": jnp.concatenate([d_ffn_wa[:FFN_K], d_ffn_wv[:FFN_K]], axis=1),
        "ffn_dw_b": jnp.concatenate([d_ffn_ba, d_ffn_bv], axis=1), "ln2_g": d_ln2_g, "ln2_b": d_ln2_b,
        "b_cond": dmod, "loss": loss_part,
    }
    small_names = list(small)
    small_shapes = [small[n].shape for n in small_names]
    small_all, = _exchange([_pack([small[n] for n in small_names])], scatter=False, name="gather_small")
    small_sum = dict(zip(small_names, _unpack(_sum_parts(small_all, name="sum_small").reshape(-1), small_shapes)))
    dmod_all = _unpack(small_all.reshape(NDEV, -1), small_shapes)[small_names.index("b_cond")].reshape(NDEV, 6 * d)
    dmod_cols = lax.dynamic_slice(dmod_all.reshape(NDEV, NDEV, ncond), (0, me, 0), (NDEV, 1, ncond)).reshape(NDEV, ncond)
    grads["w_cond"] = _cond_bwd(c_all, dmod_cols, name="cond_bwd")
    loss = small_sum.pop("loss").reshape(())
    for n, g in small_sum.items():
        grads[n] = g
    ntap = cv_dw_w.shape[3]
    grads["cv_dw_w"] = lax.dynamic_slice(grads["cv_dw_w"], (0, me * ntap), (CONV_K, ntap))
    nffn = ffn_dw_w.shape[3]
    grads["ffn_dw_w"] = lax.dynamic_slice(grads["ffn_dw_w"], (0, me * nffn), (FFN_K, nffn))
    grads = {n: grads[n].reshape(weights[n].shape) for n in names}

    delta["w_cond"], new_m["w_cond"], new_v["w_cond"] = _adamw(w_cond[0], grads["w_cond"][0], m_w_cond[0], v_w_cond[0],
                                                               name="adamw_w_cond")
    rest = [n for n in names if n not in ["w_cond"] + big]
    rest_shapes = [weights[n].shape for n in rest]
    packed = [_pack([t[n] for n in rest]) for t in (weights, grads, mom_m, mom_v)]
    for tgt_dict, res in zip((delta, new_m, new_v), _adamw(*packed, name="adamw_small")):
        for n, a in zip(rest, _unpack(res.reshape(-1), rest_shapes)):
            tgt_dict[n] = a
    shaped = lambda t: [t[n].reshape(weights[n].shape) for n in names]

    return (loss, grad_x[None], *shaped(grads), *shaped(delta), *shaped(new_m), *shaped(new_v))
```

```python
import functools
import math

import jax
import jax.numpy as jnp
from jax import lax
from jax.experimental import pallas as pl
from jax.experimental.pallas import tpu as pltpu

f32 = jnp.float32
bf16 = jnp.bfloat16

NDEV = 8
LANES = 128
SUBLANES = 8
SSM_GROUP = 16
SSM_STATE = 64
QW = 128
QS = 512
CONV_K = 31
CONV_HALO = 32
FFN_K = 3
FFN_HALO = 8
LN_EPS = 1e-5
ADAM_LR, ADAM_B1, ADAM_B2, ADAM_EPS, ADAM_WD, ADAM_STEP = 0.001, 0.9, 0.999, 1e-08, 0.01, 10
VMEM_LIMIT = 56 * 1024 * 1024
W_TILE_BYTES = 6 * 1024 * 1024
SUM_ROWS = 512
EW_BLOCK_BYTES = 2 * 1024 * 1024
INV_SQRT2 = 1.0 / math.sqrt(2.0)
INV_SQRT_2PI = 1.0 / math.sqrt(2.0 * math.pi)
MESH = pl.DeviceIdType.MESH


def _tile(n, want):
    t = min(n, want)
    while n % t:
        t //= 2
    return t


def _col_tile(n, rows, bytes_per):
    best = LANES if n % LANES == 0 else n
    for t in range(LANES, n + 1, LANES):
        if n % t == 0 and rows * t * bytes_per <= W_TILE_BYTES:
            best = t
    return best


def _params(*sem):
    return pltpu.CompilerParams(dimension_semantics=sem, vmem_limit_bytes=VMEM_LIMIT)


def _row(i):
    return (0, 0)


def _full(shape):
    nd = len(shape)
    return pl.BlockSpec(shape, lambda *a: (0,) * nd)


def _ln(x):
    mu = jnp.mean(x, axis=-1, keepdims=True)
    xc = x - mu
    var = jnp.mean(xc * xc, axis=-1, keepdims=True)
    rstd = lax.rsqrt(var + LN_EPS)
    return xc * rstd, rstd


def _ln_bwd(dxhat, xhat, rstd):
    return rstd * (dxhat - jnp.mean(dxhat, axis=-1, keepdims=True) - xhat * jnp.mean(dxhat * xhat, axis=-1, keepdims=True))


def _sig(x):
    return 1.0 / (1.0 + jnp.exp(-x))


def _gelu(x):
    return 0.5 * x * (1.0 + lax.erf(x * INV_SQRT2))


def _gelu_grad(x):
    return 0.5 * (1.0 + lax.erf(x * INV_SQRT2)) + x * jnp.exp(-0.5 * x * x) * INV_SQRT_2PI


def _colsum(x):
    return jnp.sum(x, axis=0, keepdims=True)


def _mm(pairs, bias=None, *, trans_w=False, out_dtype=f32, name, xchg=None):
    n_p = len(pairs)
    m = pairs[0][0].shape[0]
    n = pairs[0][1].shape[0 if trans_w else 1]
    ktot = sum(x.shape[1] for x, _ in pairs)
    tm = _tile(m, 512)
    tn = _col_tile(n, ktot, 2)
    dn = (((1,), (1,)), ((), ())) if trans_w else (((1,), (0,)), ((), ()))

    def body(*refs):
        o_ref = refs[-1]
        acc = None
        for xr, wr in zip(refs[:n_p], refs[n_p:2 * n_p]):
            r = lax.dot_general(xr[...].astype(bf16), wr[...].astype(bf16), dn, preferred_element_type=f32)
            acc = r if acc is None else acc + r
        if bias is not None:
            acc = acc + refs[2 * n_p][...]
        o_ref[...] = acc.astype(out_dtype)

    in_specs = [pl.BlockSpec((tm, x.shape[1]), lambda j, i: (i, 0)) for x, _ in pairs]
    if trans_w:
        in_specs += [pl.BlockSpec((tn, w.shape[1]), lambda j, i: (j, 0)) for _, w in pairs]
    else:
        in_specs += [pl.BlockSpec((w.shape[0], tn), lambda j, i: (0, j)) for _, w in pairs]
    args = [x for x, _ in pairs] + [w for _, w in pairs]
    if bias is not None:
        in_specs.append(pl.BlockSpec((1, tn), lambda j, i: (0, j)))
        args.append(bias)
    (out,), got = _call(
        body, args, name=name, grid=(n // tn, m // tm), in_specs=in_specs,
        out_specs=[pl.BlockSpec((tm, tn), lambda j, i: (i, j))], out_shape=[jax.ShapeDtypeStruct((m, n), out_dtype)],
        sem=("parallel", "arbitrary"), xchg=xchg)
    return out if xchg is None else (out, got)


def _mm_tn(x, dy, *, out_dtype=f32, name):
    m, k = x.shape
    n = dy.shape[1]
    tm = _tile(m, 512)
    tn = _col_tile(n, k, 4)
    steps = m // tm

    def body(x_ref, dy_ref, o_ref, *scratch):
        acc = scratch[0] if scratch else o_ref

        @pl.when(pl.program_id(1) == 0)
        def _():
            acc[...] = jnp.zeros_like(acc)

        acc[...] += lax.dot_general(x_ref[...].astype(bf16), dy_ref[...].astype(bf16), (((0,), (0,)), ((), ())),
                                    preferred_element_type=f32)
        if scratch:
            @pl.when(pl.program_id(1) == steps - 1)
            def _():
                o_ref[...] = acc[...].astype(out_dtype)

    return pl.pallas_call(
        body, name=name, grid=(n // tn, steps),
        in_specs=[pl.BlockSpec((tm, k), lambda j, i: (i, 0)), pl.BlockSpec((tm, tn), lambda j, i: (i, j))],
        out_specs=pl.BlockSpec((k, tn), lambda j, i: (0, j)),
        out_shape=jax.ShapeDtypeStruct((k, n), out_dtype),
        scratch_shapes=[] if out_dtype == f32 else [pltpu.VMEM((k, tn), f32)],
        compiler_params=_params("parallel", "arbitrary"),
    )(x, dy)


def _exchange(arrs, *, scatter, name):
    n = len(arrs)

    def body(*refs):
        _exchange_copies(refs[:n], refs[n:2 * n], refs[2 * n:], scatter, True, True)

    return pl.pallas_call(
        body, name=name, in_specs=[HBM_SPEC] * n, out_specs=[HBM_SPEC] * n, out_shape=_exchange_out_shape(arrs, scatter),
        scratch_shapes=_exchange_sems(n),
    )(*arrs)


HBM_SPEC = pl.BlockSpec(memory_space=pltpu.HBM)


def _flags(scatter, n):
    return list(scatter) if isinstance(scatter, (list, tuple)) else [scatter] * n


def _exchange_out_shape(arrs, scatter):
    return [jax.ShapeDtypeStruct(a.shape if sc else (NDEV,) + a.shape, a.dtype) for a, sc in zip(arrs, _flags(scatter, len(arrs)))]


def _exchange_sems(n):
    return [pltpu.SemaphoreType.DMA(((NDEV - 1) * n,)), pltpu.SemaphoreType.DMA(((NDEV - 1) * n,)), pltpu.SemaphoreType.DMA((n,))]


def _exchange_copies(x_refs, o_refs, sems, scatter, start, wait):
    n = len(x_refs)
    flags = _flags(scatter, n)
    send_sems, recv_sems, local_sems = sems
    ix, iy, ic = lax.axis_index("x"), lax.axis_index("y"), lax.axis_index("c")
    me = 4 * ix + 2 * iy + ic
    local = [pltpu.make_async_copy(x.at[me] if sc else x, o.at[me], local_sems.at[a])
             for a, (x, o, sc) in enumerate(zip(x_refs, o_refs, flags))]
    sends, recvs = [], []
    for k in range(1, NDEV):
        px, py, pc = (1 - ix if k & 4 else ix, 1 - iy if k & 2 else iy, 1 - ic if k & 1 else ic)
        them = 4 * px + 2 * py + pc
        for a, (x, o, sc) in enumerate(zip(x_refs, o_refs, flags)):
            sem = (k - 1) * n + a
            sends.append(pltpu.make_async_remote_copy(
                src_ref=x.at[them] if sc else x, dst_ref=o.at[me], send_sem=send_sems.at[sem], recv_sem=recv_sems.at[sem],
                device_id=(px, py, pc), device_id_type=MESH))
            recvs.append(pltpu.make_async_remote_copy(
                src_ref=x.at[me] if sc else x, dst_ref=o.at[them], send_sem=send_sems.at[sem], recv_sem=recv_sems.at[sem],
                device_id=(px, py, pc), device_id_type=MESH))
    if start:
        for cp in local + sends:
            cp.start()
    if wait:
        for cp in recvs:
            cp.wait_recv()
        for cp in sends:
            cp.wait_send()
        for cp in local:
            cp.wait()


def _call(body, args, *, name, grid, in_specs, out_specs, out_shape, scratch_shapes=(), sem, xchg=None):
    if xchg is None:
        return pl.pallas_call(body, name=name, grid=grid, in_specs=in_specs, out_specs=out_specs, out_shape=out_shape,
                              scratch_shapes=list(scratch_shapes), compiler_params=_params(*sem))(*args), None
    arrs, scatter = xchg
    n, ni, no, ns = len(arrs), len(in_specs), len(out_specs), len(scratch_shapes)

    def wrapped(*refs):
        ins, x_refs = refs[:ni], refs[ni:ni + n]
        outs, o_refs = refs[ni + n:ni + n + no], refs[ni + n + no:ni + 2 * n + no]
        scratch, sems = refs[ni + 2 * n + no:ni + 2 * n + no + ns], refs[ni + 2 * n + no + ns:]
        ids = [pl.program_id(a) for a in range(len(grid))]
        first = functools.reduce(jnp.logical_and, [p == 0 for p in ids])
        last = functools.reduce(jnp.logical_and, [p == g - 1 for p, g in zip(ids, grid)])

        @pl.when(first)
        def _():
            _exchange_copies(x_refs, o_refs, sems, scatter, True, False)

        body(*ins, *outs, *scratch)

        @pl.when(last)
        def _():
            _exchange_copies(x_refs, o_refs, sems, scatter, False, True)

    res = pl.pallas_call(
        wrapped, name=name, grid=grid, in_specs=list(in_specs) + [HBM_SPEC] * n, out_specs=list(out_specs) + [HBM_SPEC] * n,
        out_shape=list(out_shape) + _exchange_out_shape(arrs, scatter),
        scratch_shapes=list(scratch_shapes) + _exchange_sems(n),
        compiler_params=_params(*("arbitrary",) * len(grid)))(*args, *arrs)
    return res[:no], res[no:]


def _sum_parts(parts, *, name):
    r = parts.shape[1]

    def body(p_ref, o_ref):
        acc = p_ref[0]
        for j in range(1, NDEV):
            acc = acc + p_ref[j]
        o_ref[...] = acc

    return pl.pallas_call(body, name=name, out_shape=jax.ShapeDtypeStruct((r, LANES), f32), compiler_params=_params())(parts)


def _col_pieces(n, bounds):
    out = []
    for p, (a, b) in enumerate(bounds):
        for j in range(NDEV):
            lo, hi = max(a, n * j), min(b, n * (j + 1))
            if lo < hi:
                out.append((p, j, lo - a, lo - n * j, hi - lo))
    return out


def _unshard_cols(stacked, bounds, *, name):
    _, k, n = stacked.shape
    tk = _tile(k, 256)
    plan = _col_pieces(n, bounds)

    def body(x_ref, *o_refs):
        for p, j, po, so, w in plan:
            o_refs[p][:, po:po + w] = x_ref[j, :, so:so + w]

    return pl.pallas_call(
        body, name=name, grid=(k // tk,), in_specs=[pl.BlockSpec((NDEV, tk, n), lambda i: (0, i, 0))],
        out_specs=[pl.BlockSpec((tk, b - a), lambda i: (i, 0)) for a, b in bounds],
        out_shape=[jax.ShapeDtypeStruct((k, b - a), stacked.dtype) for a, b in bounds],
        compiler_params=_params("parallel"))(stacked)


def _shard_cols(pieces, *, out_dtype, name):
    k = pieces[0].shape[0]
    bounds, off = [], 0
    for p in pieces:
        bounds.append((off, off + p.shape[1]))
        off += p.shape[1]
    n = off // NDEV
    tk = _tile(k, 256)
    plan = _col_pieces(n, bounds)

    def body(*refs):
        o_ref = refs[-1]
        for p, j, po, so, w in plan:
            o_ref[j, :, so:so + w] = refs[p][:, po:po + w].astype(out_dtype)

    return pl.pallas_call(
        body, name=name, grid=(k // tk,), in_specs=[pl.BlockSpec((tk, b - a), lambda i: (i, 0)) for a, b in bounds],
        out_specs=pl.BlockSpec((NDEV, tk, n), lambda i: (0, i, 0)),
        out_shape=jax.ShapeDtypeStruct((NDEV, k, n), out_dtype),
        compiler_params=_params("parallel"))(*pieces)


def _pack(arrs):
    flat = jnp.concatenate([a.reshape(-1) for a in arrs])
    pad = (-flat.shape[0]) % (SUBLANES * LANES)
    return jnp.pad(flat, (0, pad)).reshape(-1, LANES)


def _unpack(flat, shapes):
    out, off = [], 0
    for s in shapes:
        n = math.prod(s)
        out.append(flat[..., off:off + n].reshape(flat.shape[:-1] + tuple(s)))
        off += n
    return out


def _adamw_math(w, gg, m, v):
    nm = ADAM_B1 * m + (1.0 - ADAM_B1) * gg
    nv = ADAM_B2 * v + (1.0 - ADAM_B2) * (gg * gg)
    m_hat = nm / (1.0 - ADAM_B1 ** ADAM_STEP)
    v_hat = nv / (1.0 - ADAM_B2 ** ADAM_STEP)
    return -ADAM_LR * (m_hat / (jnp.sqrt(v_hat) + ADAM_EPS) + ADAM_WD * w), nm, nv


def _row_block(r, c, copies):
    tr = r
    while copies * tr * c * 4 > EW_BLOCK_BYTES and tr % (4 * SUBLANES) == 0:
        tr //= 2
    return tr


def _adamw(w, g, m, v, *, name):
    r, c = w.shape
    tr = _row_block(r, c, 1)

    def body(w_ref, g_ref, m_ref, v_ref, d_ref, nm_ref, nv_ref):
        d_ref[...], nm_ref[...], nv_ref[...] = _adamw_math(w_ref[...], g_ref[...], m_ref[...], v_ref[...])

    spec = pl.BlockSpec((tr, c), lambda i: (i, 0))
    shp = jax.ShapeDtypeStruct((r, c), f32)
    return pl.pallas_call(
        body, name=name, grid=(r // tr,), in_specs=[spec] * 4, out_specs=[spec] * 3, out_shape=[shp] * 3,
        compiler_params=_params("parallel"),
    )(w, g, m, v)


def _sum_adamw(parts, w, m, v, *, name, xchg=None):
    r, c = w.shape
    tr = _row_block(r, c, NDEV)

    def body(p_ref, w_ref, m_ref, v_ref, g_ref, d_ref, nm_ref, nv_ref):
        gg = p_ref[0].astype(f32)
        for j in range(1, NDEV):
            gg = gg + p_ref[j].astype(f32)
        g_ref[...] = gg
        d_ref[...], nm_ref[...], nv_ref[...] = _adamw_math(w_ref[...], gg, m_ref[...], v_ref[...])

    spec = pl.BlockSpec((tr, c), lambda i: (i, 0))
    shp = jax.ShapeDtypeStruct((r, c), f32)
    res, got = _call(
        body, (parts, w, m, v), name=name, grid=(r // tr,),
        in_specs=[pl.BlockSpec((NDEV, tr, c), lambda i: (0, i, 0))] + [spec] * 3,
        out_specs=[spec] * 4, out_shape=[shp] * 4, sem=("parallel",), xchg=xchg)
    return tuple(res) if xchg is None else tuple(res) + (got,)


def _cond_fwd(c_all, w, b, *, name):
    nb, n = c_all.shape[0], w.shape[1]

    def body(c_ref, w_ref, b_ref, o_ref):
        cc = c_ref[...]
        o_ref[...] = jnp.dot(cc * _sig(cc), w_ref[...], preferred_element_type=f32,
                             precision=lax.Precision.HIGHEST) + b_ref[...]

    return pl.pallas_call(body, name=name, out_shape=jax.ShapeDtypeStruct((nb, n), f32),
                          compiler_params=_params())(c_all, w, b)


def _cond_bwd(c_all, dmod, *, name):
    d, n = c_all.shape[1], dmod.shape[1]

    def body(c_ref, g_ref, o_ref):
        cc = c_ref[...]
        o_ref[...] = lax.dot_general(cc * _sig(cc), g_ref[...], (((0,), (0,)), ((), ())), preferred_element_type=f32,
                                     precision=lax.Precision.HIGHEST)

    return pl.pallas_call(body, name=name, out_shape=jax.ShapeDtypeStruct((d, n), f32),
                          compiler_params=_params())(c_all, dmod)


def _ssm_disc(lam_re, lam_im, log_dt):
    lr = jnp.minimum(lam_re, -1e-4)
    li = lam_im
    dt = jnp.exp(log_dt)
    mag = jnp.exp(lr * dt)
    ang = li * dt
    lbr, lbi = mag * jnp.cos(ang), mag * jnp.sin(ang)
    num_r, num_i = lbr - 1.0, lbi
    den = lr * lr + li * li
    return lbr, lbi, (num_r * lr + num_i * li) / den, (num_i * lr - num_r * li) / den


def _ssm_prep(lam_re, lam_im, log_dt, *, name):
    def body(a, b, c, o1, o2, o3, o4):
        o1[...], o2[...], o3[...], o4[...] = _ssm_disc(a[...], b[...], c[...])

    shp = jax.ShapeDtypeStruct(lam_re.shape, f32)
    return pl.pallas_call(body, name=name, out_shape=[shp] * 4, compiler_params=_params())(lam_re, lam_im, log_dt)


def _ssm_prep_bwd(lam_re, lam_im, log_dt, cts, *, name):
    def body(a, b, c, g1, g2, g3, g4, o1, o2, o3):
        _, vjp = jax.vjp(_ssm_disc, a[...], b[...], c[...])
        o1[...], o2[...], o3[...] = vjp((g1[...], g2[...], g3[...], g4[...]))

    shp = jax.ShapeDtypeStruct(lam_re.shape, f32)
    return pl.pallas_call(body, name=name, out_shape=[shp, shp, jax.ShapeDtypeStruct(log_dt.shape, f32)],
                          compiler_params=_params())(lam_re, lam_im, log_dt, *cts)


def _step_major(tb, nt, dtype):
    r = lax.broadcasted_iota(jnp.int32, (tb, tb), 0)
    k = lax.broadcasted_iota(jnp.int32, (tb, tb), 1)
    return (k == (r % SUBLANES) * nt + r // SUBLANES).astype(dtype)


def _chunk_major(tb, nt, dtype):
    k = lax.broadcasted_iota(jnp.int32, (tb, tb), 0)
    r = lax.broadcasted_iota(jnp.int32, (tb, tb), 1)
    return (k == (r % SUBLANES) * nt + r // SUBLANES).astype(dtype)


def _permute_f32(pmat, x):
    return jnp.dot(pmat, x, preferred_element_type=f32, precision=lax.Precision.HIGHEST)


def _permute_bf16(pmat, x):
    return jnp.dot(pmat, x, preferred_element_type=f32).astype(bf16)


def _chain_carries(loc_r, loc_i, pr, pi_, forward):
    row = lax.broadcasted_iota(jnp.int32, loc_r.shape, 0)
    shift = 1 if forward else SUBLANES - 1
    order = range(1, SUBLANES) if forward else range(SUBLANES - 2, -1, -1)
    er, ei = loc_r, loc_i
    for k in order:
        sr, si = pltpu.roll(er, shift, 0), pltpu.roll(ei, shift, 0)
        er = jnp.where(row == k, loc_r + pr * sr - pi_ * si, er)
        ei = jnp.where(row == k, loc_i + pr * si + pi_ * sr, ei)
    edge = 0 if forward else SUBLANES - 1
    return (jnp.where(row == edge, 0.0, pltpu.roll(er, shift, 0)), jnp.where(row == edge, 0.0, pltpu.roll(ei, shift, 0)))


def _chunk_power(ar, ai, chunk_len):
    pr, pi_ = ar, ai
    for _ in range(int(math.log2(chunk_len))):
        pr, pi_ = pr * pr - pi_ * pi_, 2.0 * pr * pi_
    return pr, pi_


def _ssm_mats(bre_ref, bim_ref, cre_ref, cim_ref, cfr_ref, cfi_ref, bbar_s, cmat_s, nq):
    for q in range(nq):
        cr, ci, br, bi = cfr_ref[q], cfi_ref[q], bre_ref[q], bim_ref[q]
        bbar_s[q, :, 0:QS] = (cr * br - ci * bi).astype(bf16)
        bbar_s[q, :, QS:2 * QS] = (cr * bi + ci * br).astype(bf16)
        cmat_s[q, 0:QS, :] = cre_ref[q].astype(bf16)
        cmat_s[q, QS:2 * QS, :] = (-cim_ref[q]).astype(bf16)


def _ssm_fwd(u, ar, ai, bre, bim, cre, cim, cfr, cfi, dvec, *, name, xchg=None):
    s, sw = u.shape
    nq = sw // QW
    st = nq * 2 * QS
    tb = _tile(s, 256)
    nb, nt, chunk_len = s // tb, tb // SUBLANES, s // SUBLANES
    assert chunk_len & (chunk_len - 1) == 0 and nt % 16 == 0

    def body(u_ref, ar_ref, ai_ref, bre_ref, bim_ref, cre_ref, cim_ref, cfr_ref, cfi_ref, d_ref,
             h_out, yraw_out, y_out, buf, hc, bbar_s, cmat_s):
        ph, i = pl.program_id(0), pl.program_id(1)

        @pl.when(i == 0)
        def _():
            _ssm_mats(bre_ref, bim_ref, cre_ref, cim_ref, cfr_ref, cfi_ref, bbar_s, cmat_s, nq)

        @pl.when((ph == 0) & (i == 0))
        def _():
            hc[...] = jnp.zeros_like(hc)

        @pl.when((ph == 1) & (i == 0))
        def _():
            for q in range(nq):
                o = q * 2 * QS
                pr, pi_ = _chunk_power(ar_ref[q], ai_ref[q], chunk_len)
                sr, si = _chain_carries(hc[:, o:o + QS], hc[:, o + QS:o + 2 * QS], pr, pi_, True)
                hc[:, o:o + QS] = sr
                hc[:, o + QS:o + 2 * QS] = si

        uu = u_ref[...].reshape(tb, sw)
        up = _permute_bf16(_step_major(tb, nt, bf16), uu.astype(bf16))
        for q in range(nq):
            o = q * 2 * QS
            buf[:, o:o + 2 * QS] = jnp.dot(up[:, q * QW:(q + 1) * QW], bbar_s[q], preferred_element_type=f32)

        for q in range(nq):
            o = q * 2 * QS
            a_r = jnp.broadcast_to(ar_ref[q], (SUBLANES, QS))
            a_i = jnp.broadcast_to(ai_ref[q], (SUBLANES, QS))

            def step(t, carry, o=o, a_r=a_r, a_i=a_i):
                hr, hi = carry
                r0 = pl.multiple_of(t * SUBLANES, SUBLANES)
                nr = a_r * hr - a_i * hi + buf[pl.ds(r0, SUBLANES), o:o + QS]
                ni = a_r * hi + a_i * hr + buf[pl.ds(r0, SUBLANES), o + QS:o + 2 * QS]
                buf[pl.ds(r0, SUBLANES), o:o + QS] = nr
                buf[pl.ds(r0, SUBLANES), o + QS:o + 2 * QS] = ni
                return nr, ni

            hr, hi = lax.fori_loop(0, nt, step, (hc[:, o:o + QS], hc[:, o + QS:o + 2 * QS]))
            hc[:, o:o + QS] = hr
            hc[:, o + QS:o + 2 * QS] = hi

        @pl.when(ph == 1)
        def _():
            back = _chunk_major(tb, nt, f32)
            for q in range(nq):
                o = q * 2 * QS
                cs = slice(q * QW, (q + 1) * QW)
                hq = buf[:, o:o + 2 * QS].astype(bf16)
                h_out[:, o:o + 2 * QS] = hq
                yq = _permute_f32(back, jnp.dot(hq, cmat_s[q], preferred_element_type=f32)) + d_ref[:, cs] * uu[:, cs]
                yraw_out[:, :, cs] = yq.reshape(SUBLANES, nt, QW)
                y_out[:, :, cs] = _gelu(yq).astype(bf16).reshape(SUBLANES, nt, QW)

    blk = lambda ph, i: (0, i, 0)
    oblk = lambda ph, i: (0, i * ph, 0)
    act = lambda dt: jax.ShapeDtypeStruct((SUBLANES, chunk_len, sw), dt)
    (h_p, yraw3, y3), got = _call(
        body, (u.reshape(SUBLANES, chunk_len, sw), ar, ai, bre, bim, cre, cim, cfr, cfi, dvec), name=name, grid=(2, nb),
        in_specs=[pl.BlockSpec((SUBLANES, nt, sw), blk), _full(ar.shape), _full(ai.shape), _full(bre.shape), _full(bim.shape),
                  _full(cre.shape), _full(cim.shape), _full(cfr.shape), _full(cfi.shape), _full(dvec.shape)],
        out_specs=[pl.BlockSpec((tb, st), lambda ph, i: (i * ph, 0)), pl.BlockSpec((SUBLANES, nt, sw), oblk),
                   pl.BlockSpec((SUBLANES, nt, sw), oblk)],
        out_shape=[jax.ShapeDtypeStruct((s, st), bf16), act(f32), act(bf16)],
        scratch_shapes=[pltpu.VMEM((tb, st), f32), pltpu.VMEM((SUBLANES, st), f32),
                        pltpu.VMEM((nq, QW, 2 * QS), bf16), pltpu.VMEM((nq, 2 * QS, QW), bf16)],
        sem=("arbitrary", "arbitrary"), xchg=xchg)
    return h_p, yraw3, y3.reshape(s, sw), got


def _ssm_bwd(dy, yraw3, u, h_p, ar, ai, bre, bim, cre, cim, cfr, cfi, dvec, *, name, xchg=None):
    s, sw = u.shape
    nq = sw // QW
    st = nq * 2 * QS
    tb = _tile(s, 256)
    nb, nt, chunk_len = s // tb, tb // SUBLANES, s // SUBLANES

    def body(dy_ref, yraw_ref, u_ref, h_ref, ar_ref, ai_ref, bre_ref, bim_ref, cre_ref, cim_ref, cfr_ref, cfi_ref, d_ref,
             du_out, dbre_out, dbim_out, dcre_out, dcim_out, dcfr_out, dcfi_out, dlbr_out, dlbi_out, dd_out, dbu_out,
             buf, hf, rc, acc, dbbar, dcmat, bbar_s, cmat_s):
        ph, i = pl.program_id(0), pl.program_id(1)

        @pl.when(i == 0)
        def _():
            _ssm_mats(bre_ref, bim_ref, cre_ref, cim_ref, cfr_ref, cfi_ref, bbar_s, cmat_s, nq)

        @pl.when((ph == 0) & (i == 0))
        def _():
            rc[...] = jnp.zeros_like(rc)

        @pl.when((ph == 1) & (i == 0))
        def _():
            for q in range(nq):
                o = q * 2 * QS
                pr, pi_ = _chunk_power(ar_ref[q], ai_ref[q], chunk_len)
                sr, si = _chain_carries(rc[:, o:o + QS], rc[:, o + QS:o + 2 * QS], pr, -pi_, False)
                rc[:, o:o + QS] = sr
                rc[:, o + QS:o + 2 * QS] = si
            acc[...] = jnp.zeros_like(acc)
            dbbar[...] = jnp.zeros_like(dbbar)
            dcmat[...] = jnp.zeros_like(dcmat)
            dd_out[...] = jnp.zeros_like(dd_out)
            dbu_out[...] = jnp.zeros_like(dbu_out)

        dyraw = (dy_ref[...] * _gelu_grad(yraw_ref[...])).reshape(tb, sw)
        fwd_perm = _step_major(tb, nt, bf16)
        dyp = _permute_bf16(fwd_perm, dyraw.astype(bf16))
        for q in range(nq):
            o = q * 2 * QS
            buf[:, o:o + 2 * QS] = lax.dot_general(dyp[:, q * QW:(q + 1) * QW], cmat_s[q], (((1,), (1,)), ((), ())),
                                                   preferred_element_type=f32)

        def recur(with_grad):
            for q in range(nq):
                o = q * 2 * QS
                a_r = jnp.broadcast_to(ar_ref[q], (SUBLANES, QS))
                a_i = jnp.broadcast_to(ai_ref[q], (SUBLANES, QS))

                def step(j, carry, o=o, a_r=a_r, a_i=a_i):
                    r0 = pl.multiple_of((nt - 1 - j) * SUBLANES, SUBLANES)
                    if with_grad:
                        rr, ri, gr, gi = carry
                        hr = hf[pl.ds(r0, SUBLANES), o:o + QS]
                        hi = hf[pl.ds(r0, SUBLANES), o + QS:o + 2 * QS]
                        gr = gr + hr * rr + hi * ri
                        gi = gi + hr * ri - hi * rr
                    else:
                        rr, ri = carry
                    nr = buf[pl.ds(r0, SUBLANES), o:o + QS] + a_r * rr + a_i * ri
                    ni = buf[pl.ds(r0, SUBLANES), o + QS:o + 2 * QS] + a_r * ri - a_i * rr
                    buf[pl.ds(r0, SUBLANES), o:o + QS] = nr
                    buf[pl.ds(r0, SUBLANES), o + QS:o + 2 * QS] = ni
                    return (nr, ni, gr, gi) if with_grad else (nr, ni)

                init = (rc[:, o:o + QS], rc[:, o + QS:o + 2 * QS])
                if with_grad:
                    init = init + (acc[:, o:o + QS], acc[:, o + QS:o + 2 * QS])
                res = lax.fori_loop(0, nt, step, init)
                rc[:, o:o + QS] = res[0]
                rc[:, o + QS:o + 2 * QS] = res[1]
                if with_grad:
                    acc[:, o:o + QS] = res[2]
                    acc[:, o + QS:o + 2 * QS] = res[3]

        @pl.when(ph == 0)
        def _():
            recur(False)

        @pl.when(ph == 1)
        def _():
            hf[...] = h_ref[...].astype(f32)
            recur(True)
            uu = u_ref[...].reshape(tb, sw)
            up = _permute_bf16(fwd_perm, uu.astype(bf16))
            back = _chunk_major(tb, nt, f32)
            dd_out[...] += _colsum(dyraw * uu)
            for q in range(nq):
                o = q * 2 * QS
                cs = slice(q * QW, (q + 1) * QW)
                lam = buf[:, o:o + 2 * QS].astype(bf16)
                duq = _permute_f32(back, lax.dot_general(lam, bbar_s[q], (((1,), (1,)), ((), ())), preferred_element_type=f32)) \
                    + d_ref[:, cs] * dyraw[:, cs]
                du_out[:, :, cs] = duq.astype(bf16).reshape(SUBLANES, nt, QW)
                dbu_out[:, cs] += _colsum(duq)
                dbbar[q] += lax.dot_general(up[:, cs], lam, (((0,), (0,)), ((), ())), preferred_element_type=f32)
                dcmat[q] += lax.dot_general(h_ref[:, o:o + 2 * QS], dyp[:, cs], (((0,), (0,)), ((), ())),
                                            preferred_element_type=f32)

        @pl.when((ph == 1) & (i == nb - 1))
        def _():
            for q in range(nq):
                o = q * 2 * QS
                cr, ci, br, bi = cfr_ref[q], cfi_ref[q], bre_ref[q], bim_ref[q]
                gr, gi = dbbar[q, :, 0:QS], dbbar[q, :, QS:2 * QS]
                dbre_out[q] = cr * gr + ci * gi
                dbim_out[q] = cr * gi - ci * gr
                dcfr_out[q] = _colsum(gr * br + gi * bi)
                dcfi_out[q] = _colsum(gi * br - gr * bi)
                dcre_out[q] = dcmat[q, 0:QS, :]
                dcim_out[q] = -dcmat[q, QS:2 * QS, :]
                dlbr_out[q] = _colsum(acc[:, o:o + QS])
                dlbi_out[q] = _colsum(acc[:, o + QS:o + 2 * QS])

    blk = lambda ph, i: (0, nb - 1 - i, 0)
    oblk = lambda ph, i: (0, (nb - 1 - i) * ph + (nb - 1) * (1 - ph), 0)
    pshapes = [ar.shape, ai.shape, bre.shape, bim.shape, cre.shape, cim.shape, cfr.shape, cfi.shape, dvec.shape]
    oshapes = [bre.shape, bim.shape, cre.shape, cim.shape, cfr.shape, cfi.shape, ar.shape, ai.shape, dvec.shape, dvec.shape]
    act = pl.BlockSpec((SUBLANES, nt, sw), blk)
    view = lambda a: a.reshape(SUBLANES, chunk_len, sw)
    res, got = _call(
        body, (view(dy), yraw3, view(u), h_p, ar, ai, bre, bim, cre, cim, cfr, cfi, dvec), name=name, grid=(2, nb),
        in_specs=[act, act, act, pl.BlockSpec((tb, st), lambda ph, i: (nb - 1 - i, 0))] + [_full(p) for p in pshapes],
        out_specs=[pl.BlockSpec((SUBLANES, nt, sw), oblk)] + [_full(p) for p in oshapes],
        out_shape=[jax.ShapeDtypeStruct((SUBLANES, chunk_len, sw), bf16)] + [jax.ShapeDtypeStruct(p, f32) for p in oshapes],
        scratch_shapes=[pltpu.VMEM((tb, st), f32), pltpu.VMEM((tb, st), f32),
                        pltpu.VMEM((SUBLANES, st), f32), pltpu.VMEM((SUBLANES, st), f32),
                        pltpu.VMEM((nq, QW, 2 * QS), f32), pltpu.VMEM((nq, 2 * QS, QW), f32),
                        pltpu.VMEM((nq, QW, 2 * QS), bf16), pltpu.VMEM((nq, 2 * QS, QW), bf16)],
        sem=("arbitrary", "arbitrary"), xchg=xchg)
    return (res[0].reshape(s, sw),) + tuple(res[1:]) + (got,)


def _lnmod(x, sc, sh, *, name):
    s, d = x.shape
    tb = _tile(s, 512)

    def body(x_ref, sc_ref, sh_ref, o_ref):
        xh, _ = _ln(x_ref[...])
        o_ref[...] = (xh * (1.0 + sc_ref[...]) + sh_ref[...]).astype(bf16)

    blk = pl.BlockSpec((tb, d), lambda i: (i, 0))
    vec = pl.BlockSpec((1, d), _row)
    return pl.pallas_call(body, name=name, grid=(s // tb,), in_specs=[blk, vec, vec], out_specs=blk,
                          out_shape=jax.ShapeDtypeStruct((s, d), bf16), compiler_params=_params("parallel"))(x, sc, sh)


def _conv_taps(ext_ref, w_ref, tb, ntap, off):
    acc = ext_ref[pl.ds(off, tb), :] * w_ref[pl.ds(0, 1), :]
    for k in range(1, ntap):
        acc = acc + ext_ref[pl.ds(off + k, tb), :] * w_ref[pl.ds(k, 1), :]
    return acc


def _conv_halo_specs(tb, cw, halo, s):
    per = tb // halo
    prev = pl.BlockSpec((halo, cw), lambda i: (jnp.maximum(i * per - 1, 0), 0))
    nxt = pl.BlockSpec((halo, cw), lambda i: (jnp.minimum((i + 1) * per, s // halo - 1), 0))
    return prev, nxt


def _conv_v2(a_ref, g_ref, ah_ref, gh_ref, w_ref, b_ref, ext, tb, i):
    gg = g_ref[...]
    ext[pl.ds(CONV_HALO, tb), :] = a_ref[...] * _sig(gg)
    ext[pl.ds(0, CONV_HALO), :] = jnp.where(i > 0, ah_ref[...] * _sig(gh_ref[...]), 0.0)
    return _conv_taps(ext, w_ref, tb, CONV_K, CONV_HALO - CONV_K + 1) + b_ref[...]


def _silu_grad(x):
    sg = _sig(x)
    return sg * (1.0 + x * (1.0 - sg))


def _conv_fwd(cva, cvg, w, b, lng, lnb, *, name, xchg=None):
    s, cw = cva.shape
    tb = _tile(s, 256)
    prev, _ = _conv_halo_specs(tb, cw, CONV_HALO, s)

    def body(a_ref, g_ref, ah_ref, gh_ref, w_ref, b_ref, lng_ref, lnb_ref, o_ref, ext):
        v2 = _conv_v2(a_ref, g_ref, ah_ref, gh_ref, w_ref, b_ref, ext, tb, pl.program_id(0))
        xh, _ = _ln(v2)
        v3 = xh * lng_ref[...] + lnb_ref[...]
        o_ref[...] = (v3 * _sig(v3)).astype(bf16)

    blk = pl.BlockSpec((tb, cw), lambda i: (i, 0))
    vec = pl.BlockSpec((1, cw), _row)
    (v4,), got = _call(
        body, (cva, cvg, cva, cvg, w, b, lng, lnb), name=name, grid=(s // tb,),
        in_specs=[blk, blk, prev, prev, _full(w.shape), vec, vec, vec], out_specs=[blk],
        out_shape=[jax.ShapeDtypeStruct((s, cw), bf16)], scratch_shapes=[pltpu.VMEM((tb + CONV_HALO, cw), f32)],
        sem=("parallel",), xchg=xchg)
    return v4, got


def _conv_bwd_ln(dv4, cva, cvg, w, b, lng, lnb, *, name):
    s, cw = cva.shape
    tb = _tile(s, 256)
    prev, _ = _conv_halo_specs(tb, cw, CONV_HALO, s)

    def body(d_ref, a_ref, g_ref, ah_ref, gh_ref, w_ref, b_ref, lng_ref, lnb_ref, o_ref, dg_ref, db_ref, ext):
        i = pl.program_id(0)

        @pl.when(i == 0)
        def _():
            dg_ref[...] = jnp.zeros_like(dg_ref)
            db_ref[...] = jnp.zeros_like(db_ref)

        v2 = _conv_v2(a_ref, g_ref, ah_ref, gh_ref, w_ref, b_ref, ext, tb, i)
        xh, rstd = _ln(v2)
        v3 = xh * lng_ref[...] + lnb_ref[...]
        dv3 = d_ref[...] * _silu_grad(v3)
        dg_ref[...] += _colsum(dv3 * xh)
        db_ref[...] += _colsum(dv3)
        o_ref[...] = _ln_bwd(dv3 * lng_ref[...], xh, rstd)

    blk = pl.BlockSpec((tb, cw), lambda i: (i, 0))
    vec = pl.BlockSpec((1, cw), _row)
    vshape = jax.ShapeDtypeStruct((1, cw), f32)
    return pl.pallas_call(
        body, name=name, grid=(s // tb,), in_specs=[blk, blk, blk, prev, prev, _full(w.shape), vec, vec, vec],
        out_specs=[blk, vec, vec], out_shape=[jax.ShapeDtypeStruct((s, cw), f32), vshape, vshape],
        scratch_shapes=[pltpu.VMEM((tb + CONV_HALO, cw), f32)],
        compiler_params=_params("arbitrary"))(dv4, cva, cvg, cva, cvg, w, b, lng, lnb)


def _conv_bwd_taps(dv2, cva, cvg, w, *, name, xchg=None):
    s, cw = cva.shape
    tb = _tile(s, 256)
    nb = s // tb
    prev, nxt = _conv_halo_specs(tb, cw, CONV_HALO, s)

    def body(d_ref, dn_ref, a_ref, g_ref, ah_ref, gh_ref, w_ref, da_ref, dg_ref, dw_ref, db_ref, sa_ref, sg_ref, ext, dext):
        i = pl.program_id(0)

        @pl.when(i == 0)
        def _():
            for r in (dw_ref, db_ref, sa_ref, sg_ref):
                r[...] = jnp.zeros_like(r)

        aa, gg = a_ref[...], g_ref[...]
        sg = _sig(gg)
        ext[pl.ds(CONV_HALO, tb), :] = aa * sg
        ext[pl.ds(0, CONV_HALO), :] = jnp.where(i > 0, ah_ref[...] * _sig(gh_ref[...]), 0.0)
        dd = d_ref[...]
        dext[pl.ds(0, tb), :] = dd
        dext[pl.ds(tb, CONV_HALO), :] = jnp.where(i < nb - 1, dn_ref[...], 0.0)
        dv = dext[pl.ds(CONV_K - 1, tb), :] * w_ref[pl.ds(0, 1), :]
        for k in range(1, CONV_K):
            dv = dv + dext[pl.ds(CONV_K - 1 - k, tb), :] * w_ref[pl.ds(k, 1), :]
        for k in range(CONV_K):
            dw_ref[pl.ds(k, 1), :] += _colsum(dd * ext[pl.ds(CONV_HALO - CONV_K + 1 + k, tb), :])
        db_ref[...] += _colsum(dd)
        da = dv * sg
        dgate = dv * aa * sg * (1.0 - sg)
        sa_ref[...] += _colsum(da)
        sg_ref[...] += _colsum(dgate)
        da_ref[...] = da.astype(bf16)
        dg_ref[...] = dgate.astype(bf16)

    blk = pl.BlockSpec((tb, cw), lambda i: (i, 0))
    vec = pl.BlockSpec((1, cw), _row)
    vshape = jax.ShapeDtypeStruct((1, cw), f32)
    act = jax.ShapeDtypeStruct((s, cw), bf16)
    res, got = _call(
        body, (dv2, dv2, cva, cvg, cva, cvg, w), name=name, grid=(nb,), in_specs=[blk, nxt, blk, blk, prev, prev, _full(w.shape)],
        out_specs=[blk, blk, _full(w.shape), vec, vec, vec],
        out_shape=[act, act, jax.ShapeDtypeStruct(w.shape, f32), vshape, vshape, vshape],
        scratch_shapes=[pltpu.VMEM((tb + CONV_HALO, cw), f32), pltpu.VMEM((tb + CONV_HALO, cw), f32)],
        sem=("arbitrary",), xchg=xchg)
    return tuple(res) + (got,)


def _glu_merge(ya, yb, ycv, gs, gc, *, name):
    s, d = ya.shape
    tb = _tile(s, 512)

    def body(ya_ref, yb_ref, ycv_ref, gs_ref, gc_ref, o_ref):
        z = ya_ref[...] * _sig(yb_ref[...])
        o_ref[...] = (_sig(gs_ref[...]) * z + _sig(gc_ref[...]) * ycv_ref[...]).astype(bf16)

    blk = pl.BlockSpec((tb, d), lambda i: (i, 0))
    return pl.pallas_call(body, name=name, grid=(s // tb,), in_specs=[blk] * 5, out_specs=blk,
                          out_shape=jax.ShapeDtypeStruct((s, d), bf16), compiler_params=_params("parallel"))(ya, yb, ycv, gs, gc)


def _glu_merge_bwd(dm, ya, yb, ycv, gs, gc, *, name):
    s, d = ya.shape
    tb = _tile(s, 512)

    def body(dm_ref, ya_ref, yb_ref, ycv_ref, gs_ref, gc_ref, dya_ref, dyb_ref, dycv_ref, dgs_ref, dgc_ref, sgs_ref, sgc_ref):
        @pl.when(pl.program_id(0) == 0)
        def _():
            sgs_ref[...] = jnp.zeros_like(sgs_ref)
            sgc_ref[...] = jnp.zeros_like(sgc_ref)

        dmv, yav = dm_ref[...], ya_ref[...]
        sb, ss, scv = _sig(yb_ref[...]), _sig(gs_ref[...]), _sig(gc_ref[...])
        z = yav * sb
        dz = dmv * ss
        dgs = dmv * z * ss * (1.0 - ss)
        dgc = dmv * ycv_ref[...] * scv * (1.0 - scv)
        dya_ref[...] = (dz * sb).astype(bf16)
        dyb_ref[...] = (dz * yav * sb * (1.0 - sb)).astype(bf16)
        dycv_ref[...] = (dmv * scv).astype(bf16)
        dgs_ref[...] = dgs.astype(bf16)
        dgc_ref[...] = dgc.astype(bf16)
        sgs_ref[...] += _colsum(dgs)
        sgc_ref[...] += _colsum(dgc)

    blk = pl.BlockSpec((tb, d), lambda i: (i, 0))
    vec = pl.BlockSpec((1, d), _row)
    act = jax.ShapeDtypeStruct((s, d), bf16)
    vshape = jax.ShapeDtypeStruct((1, d), f32)
    return pl.pallas_call(body, name=name, grid=(s // tb,), in_specs=[blk] * 6, out_specs=[blk] * 5 + [vec, vec],
                          out_shape=[act] * 5 + [vshape, vshape], compiler_params=_params("arbitrary"))(dm, ya, yb, ycv, gs, gc)


def _resid_ln_mod(x, o, g, lng, lnb, sc, sh, alpha, *, name):
    s, d = x.shape
    tb = _tile(s, 512)

    def body(x_ref, o_ref, g_ref, lng_ref, lnb_ref, sc_ref, sh_ref, x1_ref, h_ref):
        xh, _ = _ln(alpha * x_ref[...] + g_ref[...] * o_ref[...])
        x1 = xh * lng_ref[...] + lnb_ref[...]
        x1_ref[...] = x1
        xh1, _ = _ln(x1)
        h_ref[...] = (xh1 * (1.0 + sc_ref[...]) + sh_ref[...]).astype(bf16)

    blk = pl.BlockSpec((tb, d), lambda i: (i, 0))
    vec = pl.BlockSpec((1, d), _row)
    return pl.pallas_call(body, name=name, grid=(s // tb,), in_specs=[blk, blk] + [vec] * 5, out_specs=[blk, blk],
                          out_shape=[jax.ShapeDtypeStruct((s, d), f32), jax.ShapeDtypeStruct((s, d), bf16)],
                          compiler_params=_params("parallel"))(x, o, g, lng, lnb, sc, sh)


def _resid_ln_loss(x1, y2, g, lng, lnb, tgt, alpha, *, name):
    s, d = x1.shape
    tb = _tile(s, 512)

    def body(x1_ref, y_ref, g_ref, lng_ref, lnb_ref, t_ref, dr_ref, dy_ref, loss_ref, dlg_ref, dlb_ref, dg_ref):
        @pl.when(pl.program_id(0) == 0)
        def _():
            for r in (loss_ref, dlg_ref, dlb_ref, dg_ref):
                r[...] = jnp.zeros_like(r)

        yv = y_ref[...]
        xh, rstd = _ln(alpha * x1_ref[...] + g_ref[...] * yv)
        err = xh * lng_ref[...] + lnb_ref[...] - t_ref[...]
        loss_ref[...] += 0.5 * jnp.sum(jnp.sum(err * err, axis=-1, keepdims=True) / d, axis=0, keepdims=True)
        dx2 = err / d
        dlg_ref[...] += _colsum(dx2 * xh)
        dlb_ref[...] += _colsum(dx2)
        dr = _ln_bwd(dx2 * lng_ref[...], xh, rstd)
        dg_ref[...] += _colsum(dr * yv)
        dr_ref[...] = dr
        dy_ref[...] = (g_ref[...] * dr).astype(bf16)

    blk = pl.BlockSpec((tb, d), lambda i: (i, 0))
    vec = pl.BlockSpec((1, d), _row)
    vshape = jax.ShapeDtypeStruct((1, d), f32)
    return pl.pallas_call(
        body, name=name, grid=(s // tb,), in_specs=[blk, blk, vec, vec, vec, blk],
        out_specs=[blk, blk, pl.BlockSpec((1, 1), _row), vec, vec, vec],
        out_shape=[jax.ShapeDtypeStruct((s, d), f32), jax.ShapeDtypeStruct((s, d), bf16),
                   jax.ShapeDtypeStruct((1, 1), f32), vshape, vshape, vshape],
        compiler_params=_params("arbitrary"))(x1, y2, g, lng, lnb, tgt)


def _mid_bwd(dh2, x1, dr2, x, o, g, sc, lng, alpha, *, name):
    s, d = x.shape
    tb = _tile(s, 512)

    def body(dh_ref, x1_ref, dr2_ref, x_ref, o_ref, g_ref, sc_ref, lng_ref,
             dr1_ref, do_ref, dsc_ref, dsh_ref, dlg_ref, dlb_ref, dg_ref):
        @pl.when(pl.program_id(0) == 0)
        def _():
            for r in (dsc_ref, dsh_ref, dlg_ref, dlb_ref, dg_ref):
                r[...] = jnp.zeros_like(r)

        dh = dh_ref[...]
        xh1, rstd1 = _ln(x1_ref[...])
        dsc_ref[...] += _colsum(dh * xh1)
        dsh_ref[...] += _colsum(dh)
        dx1 = alpha * dr2_ref[...] + _ln_bwd(dh * (1.0 + sc_ref[...]), xh1, rstd1)
        ov = o_ref[...]
        xhr, rstdr = _ln(alpha * x_ref[...] + g_ref[...] * ov)
        dlg_ref[...] += _colsum(dx1 * xhr)
        dlb_ref[...] += _colsum(dx1)
        dr1 = _ln_bwd(dx1 * lng_ref[...], xhr, rstdr)
        dg_ref[...] += _colsum(dr1 * ov)
        dr1_ref[...] = dr1
        do_ref[...] = (g_ref[...] * dr1).astype(bf16)

    blk = pl.BlockSpec((tb, d), lambda i: (i, 0))
    vec = pl.BlockSpec((1, d), _row)
    vshape = jax.ShapeDtypeStruct((1, d), f32)
    return pl.pallas_call(
        body, name=name, grid=(s // tb,), in_specs=[blk] * 5 + [vec] * 3, out_specs=[blk, blk] + [vec] * 5,
        out_shape=[jax.ShapeDtypeStruct((s, d), f32), jax.ShapeDtypeStruct((s, d), bf16)] + [vshape] * 5,
        compiler_params=_params("arbitrary"))(dh2, x1, dr2, x, o, g, sc, lng)


def _final_bwd(dh1, x, dr1, sc, alpha, *, name, xchg=None):
    s, d = x.shape
    tb = _tile(s, 512)

    def body(dh_ref, x_ref, dr1_ref, sc_ref, dx_ref, dsc_ref, dsh_ref):
        @pl.when(pl.program_id(0) == 0)
        def _():
            dsc_ref[...] = jnp.zeros_like(dsc_ref)
            dsh_ref[...] = jnp.zeros_like(dsh_ref)

        dh = dh_ref[...]
        xh, rstd = _ln(x_ref[...])
        dsc_ref[...] += _colsum(dh * xh)
        dsh_ref[...] += _colsum(dh)
        dx_ref[...] = alpha * dr1_ref[...] + _ln_bwd(dh * (1.0 + sc_ref[...]), xh, rstd)

    blk = pl.BlockSpec((tb, d), lambda i: (i, 0))
    vec = pl.BlockSpec((1, d), _row)
    vshape = jax.ShapeDtypeStruct((1, d), f32)
    res, got = _call(body, (dh1, x, dr1, sc), name=name, grid=(s // tb,), in_specs=[blk, blk, blk, vec], out_specs=[blk, vec, vec],
                     out_shape=[jax.ShapeDtypeStruct((s, d), f32), vshape, vshape], sem=("arbitrary",), xchg=xchg)
    return tuple(res) + (got,)


def _ffn_specs(s, fh, tb, tc):
    per = tb // FFN_HALO
    blk = pl.BlockSpec((tb, tc), lambda j, i: (i, j))
    prev = pl.BlockSpec((FFN_HALO, tc), lambda j, i: (jnp.maximum(i * per - 1, 0), j))
    nxt = pl.BlockSpec((FFN_HALO, tc), lambda j, i: (jnp.minimum((i + 1) * per, s // FFN_HALO - 1), j))
    taps = pl.BlockSpec((FFN_HALO, tc), lambda j, i: (0, j))
    vec = pl.BlockSpec((1, tc), lambda j, i: (0, j))
    return blk, prev, nxt, taps, vec


def _ffn_mid(upa, upv, wa, wv, ba, bv, *, name):
    s, fh = upa.shape
    tb, tc = _tile(s, 512), _tile(fh, 256)
    blk, prev, _, taps, vec = _ffn_specs(s, fh, tb, tc)
    off = FFN_HALO - FFN_K + 1

    def body(a_ref, v_ref, ah_ref, vh_ref, wa_ref, wv_ref, ba_ref, bv_ref, o_ref, exta, extv):
        first = pl.program_id(1) == 0
        exta[pl.ds(FFN_HALO, tb), :] = a_ref[...]
        extv[pl.ds(FFN_HALO, tb), :] = v_ref[...]
        exta[pl.ds(0, FFN_HALO), :] = jnp.where(first, 0.0, ah_ref[...])
        extv[pl.ds(0, FFN_HALO), :] = jnp.where(first, 0.0, vh_ref[...])
        a2 = _conv_taps(exta, wa_ref, tb, FFN_K, off) + ba_ref[...]
        v2 = _conv_taps(extv, wv_ref, tb, FFN_K, off) + bv_ref[...]
        o_ref[...] = (_gelu(a2) * v2).astype(bf16)

    return pl.pallas_call(
        body, name=name, grid=(fh // tc, s // tb), in_specs=[blk, blk, prev, prev, taps, taps, vec, vec], out_specs=blk,
        out_shape=jax.ShapeDtypeStruct((s, fh), bf16),
        scratch_shapes=[pltpu.VMEM((tb + FFN_HALO, tc), f32)] * 2,
        compiler_params=_params("parallel", "arbitrary"))(upa, upv, upa, upv, wa, wv, ba, bv)


def _ffn_mid_bwd(df, upa, upv, wa, wv, ba, bv, *, name, xchg=None):
    s, fh = upa.shape
    tb, tc = _tile(s, 512), _tile(fh, 256)
    nb = s // tb
    blk, prev, nxt, taps, vec = _ffn_specs(s, fh, tb, tc)
    off = FFN_HALO - FFN_K + 1
    te = tb + FFN_HALO

    def body(df_ref, dfn_ref, a_ref, v_ref, ah_ref, vh_ref, an_ref, vn_ref, wa_ref, wv_ref, ba_ref, bv_ref,
             da_ref, dv_ref, dwa_ref, dwv_ref, dba_ref, dbv_ref, exta, extv, dexta, dextv):
        i = pl.program_id(1)

        @pl.when(i == 0)
        def _():
            for r in (dwa_ref, dwv_ref, dba_ref, dbv_ref):
                r[...] = jnp.zeros_like(r)

        last = i == nb - 1
        for ext, c_ref, h_ref, n_ref in ((exta, a_ref, ah_ref, an_ref), (extv, v_ref, vh_ref, vn_ref)):
            ext[pl.ds(0, FFN_HALO), :] = jnp.where(i == 0, 0.0, h_ref[...])
            ext[pl.ds(FFN_HALO, tb), :] = c_ref[...]
            ext[pl.ds(FFN_HALO + tb, FFN_HALO), :] = jnp.where(last, 0.0, n_ref[...])
        a2 = _conv_taps(exta, wa_ref, te, FFN_K, off) + ba_ref[...]
        v2 = _conv_taps(extv, wv_ref, te, FFN_K, off) + bv_ref[...]
        dexta[pl.ds(0, tb), :] = df_ref[...]
        dexta[pl.ds(tb, FFN_HALO), :] = jnp.where(last, 0.0, dfn_ref[...])
        dfe = dexta[...]
        cdf = 0.5 * (1.0 + lax.erf(a2 * INV_SQRT2))
        dexta[...] = dfe * v2 * (cdf + a2 * jnp.exp(-0.5 * a2 * a2) * INV_SQRT_2PI)
        dextv[...] = dfe * (a2 * cdf)
        for ext, dext, w_ref, dw_ref, db_ref, o_ref in ((exta, dexta, wa_ref, dwa_ref, dba_ref, da_ref),
                                                        (extv, dextv, wv_ref, dwv_ref, dbv_ref, dv_ref)):
            dcur = dext[pl.ds(0, tb), :]
            dup = dext[pl.ds(FFN_K - 1, tb), :] * w_ref[pl.ds(0, 1), :]
            for k in range(1, FFN_K):
                dup = dup + dext[pl.ds(FFN_K - 1 - k, tb), :] * w_ref[pl.ds(k, 1), :]
            o_ref[...] = dup.astype(bf16)
            for k in range(FFN_K):
                dw_ref[pl.ds(k, 1), :] += _colsum(dcur * ext[pl.ds(off + k, tb), :])
            db_ref[...] += _colsum(dcur)

    act = jax.ShapeDtypeStruct((s, fh), bf16)
    wshape = jax.ShapeDtypeStruct((FFN_HALO, fh), f32)
    vshape = jax.ShapeDtypeStruct((1, fh), f32)
    res, got = _call(
        body, (df, df, upa, upv, upa, upv, upa, upv, wa, wv, ba, bv), name=name, grid=(fh // tc, nb),
        in_specs=[blk, nxt, blk, blk, prev, prev, nxt, nxt, taps, taps, vec, vec],
        out_specs=[blk, blk, taps, taps, vec, vec], out_shape=[act, act, wshape, wshape, vshape, vshape],
        scratch_shapes=[pltpu.VMEM((tb + 2 * FFN_HALO, tc), f32)] * 2 + [pltpu.VMEM((te, tc), f32)] * 2,
        sem=("parallel", "arbitrary"), xchg=xchg)
    return tuple(res) + (got,)


def _cols_from_shards(stacked):
    _, k, n = stacked.shape
    return stacked.transpose(1, 0, 2).reshape(k, NDEV * n)


def _pad_rows(w, rows):
    return jnp.pad(w, ((0, rows - w.shape[0]), (0, 0)))


def kernel(x, c, w_cond, b_cond, w_in, b_in, ssm_lambda_re, ssm_lambda_im, ssm_log_dt, ssm_b_re, ssm_b_im, ssm_c_re, ssm_c_im, ssm_d, ssm_glu_w_a, ssm_glu_w_b, cv_dw_w, cv_dw_b, cv_ln_g, cv_ln_b, cv_w_pw, w_out, ln1_g, ln1_b, ffn_w_up, ffn_dw_w, ffn_dw_b, ffn_w_down, ln2_g, ln2_b, loss_target, m_w_cond, m_b_cond, m_w_in, m_b_in, m_ssm_lambda_re, m_ssm_lambda_im, m_ssm_log_dt, m_ssm_b_re, m_ssm_b_im, m_ssm_c_re, m_ssm_c_im, m_ssm_d, m_ssm_glu_w_a, m_ssm_glu_w_b, m_cv_dw_w, m_cv_dw_b, m_cv_ln_g, m_cv_ln_b, m_cv_w_pw, m_w_out, m_ln1_g, m_ln1_b, m_ffn_w_up, m_ffn_dw_w, m_ffn_dw_b, m_ffn_w_down, m_ln2_g, m_ln2_b, v_w_cond, v_b_cond, v_w_in, v_b_in, v_ssm_lambda_re, v_ssm_lambda_im, v_ssm_log_dt, v_ssm_b_re, v_ssm_b_im, v_ssm_c_re, v_ssm_c_im, v_ssm_d, v_ssm_glu_w_a, v_ssm_glu_w_b, v_cv_dw_w, v_cv_dw_b, v_cv_ln_g, v_cv_ln_b, v_cv_w_pw, v_w_out, v_ln1_g, v_ln1_b, v_ffn_w_up, v_ffn_dw_w, v_ffn_dw_b, v_ffn_w_down, v_ln2_g, v_ln2_b):
    weights = dict(w_cond=w_cond, b_cond=b_cond, w_in=w_in, b_in=b_in, ssm_lambda_re=ssm_lambda_re, ssm_lambda_im=ssm_lambda_im, ssm_log_dt=ssm_log_dt, ssm_b_re=ssm_b_re, ssm_b_im=ssm_b_im, ssm_c_re=ssm_c_re, ssm_c_im=ssm_c_im, ssm_d=ssm_d, ssm_glu_w_a=ssm_glu_w_a, ssm_glu_w_b=ssm_glu_w_b, cv_dw_w=cv_dw_w, cv_dw_b=cv_dw_b, cv_ln_g=cv_ln_g, cv_ln_b=cv_ln_b, cv_w_pw=cv_w_pw, w_out=w_out, ln1_g=ln1_g, ln1_b=ln1_b, ffn_w_up=ffn_w_up, ffn_dw_w=ffn_dw_w, ffn_dw_b=ffn_dw_b, ffn_w_down=ffn_w_down, ln2_g=ln2_g, ln2_b=ln2_b)
    mom_m = dict(w_cond=m_w_cond, b_cond=m_b_cond, w_in=m_w_in, b_in=m_b_in, ssm_lambda_re=m_ssm_lambda_re, ssm_lambda_im=m_ssm_lambda_im, ssm_log_dt=m_ssm_log_dt, ssm_b_re=m_ssm_b_re, ssm_b_im=m_ssm_b_im, ssm_c_re=m_ssm_c_re, ssm_c_im=m_ssm_c_im, ssm_d=m_ssm_d, ssm_glu_w_a=m_ssm_glu_w_a, ssm_glu_w_b=m_ssm_glu_w_b, cv_dw_w=m_cv_dw_w, cv_dw_b=m_cv_dw_b, cv_ln_g=m_cv_ln_g, cv_ln_b=m_cv_ln_b, cv_w_pw=m_cv_w_pw, w_out=m_w_out, ln1_g=m_ln1_g, ln1_b=m_ln1_b, ffn_w_up=m_ffn_w_up, ffn_dw_w=m_ffn_dw_w, ffn_dw_b=m_ffn_dw_b, ffn_w_down=m_ffn_w_down, ln2_g=m_ln2_g, ln2_b=m_ln2_b)
    mom_v = dict(w_cond=v_w_cond, b_cond=v_b_cond, w_in=v_w_in, b_in=v_b_in, ssm_lambda_re=v_ssm_lambda_re, ssm_lambda_im=v_ssm_lambda_im, ssm_log_dt=v_ssm_log_dt, ssm_b_re=v_ssm_b_re, ssm_b_im=v_ssm_b_im, ssm_c_re=v_ssm_c_re, ssm_c_im=v_ssm_c_im, ssm_d=v_ssm_d, ssm_glu_w_a=v_ssm_glu_w_a, ssm_glu_w_b=v_ssm_glu_w_b, cv_dw_w=v_cv_dw_w, cv_dw_b=v_cv_dw_b, cv_ln_g=v_cv_ln_g, cv_ln_b=v_cv_ln_b, cv_w_pw=v_cv_w_pw, w_out=v_w_out, ln1_g=v_ln1_g, ln1_b=v_ln1_b, ffn_w_up=v_ffn_w_up, ffn_dw_w=v_ffn_dw_w, ffn_dw_b=v_ffn_dw_b, ffn_w_down=v_ffn_w_down, ln2_g=v_ln2_g, ln2_b=v_ln2_b)
    names = list(weights)

    s, d = x.shape[1], x.shape[2]
    sw = cw = d // 2
    fh = ffn_w_down.shape[1] * NDEV
    ng, nq = sw // SSM_GROUP, sw // QW
    gq = ng // nq
    alpha = 2.0 ** 0.25
    me = 4 * lax.axis_index("x") + 2 * lax.axis_index("y") + lax.axis_index("c")
    xs, tgt = x[0], loss_target[0]

    col_names = ["w_in", "ssm_glu_w_a", "ssm_glu_w_b", "cv_w_pw", "ffn_w_up"]
    row_names = ["w_out", "ffn_w_down"]
    big = col_names + row_names
    sent = lambda ns: [weights[n][0].astype(bf16) for n in ns]
    got_in, got_c, got_cv_taps, got_ffn_taps = _exchange(sent(["w_in"]) + [c, cv_dw_w[0, :, 0], ffn_dw_w[0, :, 0]],
                                                         scatter=False, name="gather_in")
    o1, o2, o3, o4 = sw, sw + cw, sw + 2 * cw, sw + 2 * cw + d
    in_bounds = ((0, o1), (o1, o2), (o2, o3), (o3, o4), (o4, o4 + d))
    w_u, w_cva, w_cvg, w_gs, w_gc = _unshard_cols(got_in, in_bounds, name="unshard_w_in")
    b_u, b_cva, b_cvg, b_gs, b_gc = (b_in[:, a:b] for a, b in in_bounds)
    c_all = got_c.reshape(NDEV, d)
    cv_taps = _cols_from_shards(got_cv_taps)
    ffn_taps = _cols_from_shards(got_ffn_taps)
    cv_w32 = _pad_rows(cv_taps, CONV_HALO)
    ffn_wa, ffn_wv = _pad_rows(ffn_taps[:, :fh], FFN_HALO), _pad_rows(ffn_taps[:, fh:], FFN_HALO)
    ffn_ba, ffn_bv = ffn_dw_b[:, :fh], ffn_dw_b[:, fh:]

    ncond = w_cond.shape[2]
    b_cond_mine = lax.dynamic_slice(b_cond, (0, me * ncond), (1, ncond))
    mod_cols = _cond_fwd(c_all, w_cond[0], b_cond_mine, name="cond_fwd")
    mod_all, = _exchange([mod_cols], scatter=False, name="gather_mod")
    mod_mine = lax.dynamic_slice(mod_all, (0, me, 0), (NDEV, 1, ncond)).reshape(1, 6 * d)
    sh1, sc1, g1, sh2, sc2, g2 = (mod_mine[:, k * d:(k + 1) * d] for k in range(6))

    lam_re, lam_im, log_dt = ssm_lambda_re[0], ssm_lambda_im[0], ssm_log_dt[0][:, None]
    lbr, lbi, cfr, cfi = _ssm_prep(lam_re, lam_im, log_dt, name="ssm_prep")
    rows_q = lambda a: a.reshape(nq, 1, QS)
    eye = jnp.eye(gq, dtype=f32)

    def b_mat(b):
        bt = b.reshape(nq, gq, SSM_STATE, SSM_GROUP).transpose(0, 1, 3, 2)
        return jnp.einsum("qgpn,gh->qgphn", bt, eye).reshape(nq, QW, QS)

    def c_mat(cc):
        ct = cc.reshape(nq, gq, SSM_GROUP, SSM_STATE)
        return jnp.einsum("qgpn,gh->qhngp", ct, eye).reshape(nq, QS, QW)

    def b_unmat(mt):
        return jnp.einsum("qgpgn->qgnp", mt.reshape(nq, gq, SSM_GROUP, gq, SSM_STATE)).reshape(ng, SSM_STATE, SSM_GROUP)

    def c_unmat(mt):
        return jnp.einsum("qgngp->qgpn", mt.reshape(nq, gq, SSM_STATE, gq, SSM_GROUP)).reshape(ng, SSM_GROUP, SSM_STATE)

    ssm_args = (rows_q(lbr), rows_q(lbi), b_mat(ssm_b_re[0]), b_mat(ssm_b_im[0]), c_mat(ssm_c_re[0]), c_mat(ssm_c_im[0]),
                rows_q(cfr), rows_q(cfi), ssm_d[0].reshape(1, sw))

    h1 = _lnmod(xs, sc1, sh1, name="ln_mod1")
    u = _mm([(h1, w_u)], b_u, name="in_u")
    cva = _mm([(h1, w_cva)], b_cva, name="in_cva")
    cvg = _mm([(h1, w_cvg)], b_cvg, name="in_cvg")
    gs = _mm([(h1, w_gs)], b_gs, name="in_gs")
    gc = _mm([(h1, w_gc)], b_gc, name="in_gc")
    v4, (got_a, got_b, got_pw, got_o) = _conv_fwd(
        cva, cvg, cv_w32, cv_dw_b, cv_ln_g, cv_ln_b, name="conv_fwd",
        xchg=(sent(["ssm_glu_w_a", "ssm_glu_w_b", "cv_w_pw", "w_out"]), False))
    h_p, yraw3, y, (got_up, got_dn) = _ssm_fwd(u, *ssm_args, name="ssm_fwd", xchg=(sent(["ffn_w_up", "ffn_w_down"]), False))
    w_a, = _unshard_cols(got_a, ((0, d),), name="unshard_glu_a")
    w_b, = _unshard_cols(got_b, ((0, d),), name="unshard_glu_b")
    w_pw, = _unshard_cols(got_pw, ((0, d),), name="unshard_conv_pw")
    w_upa, w_upv = _unshard_cols(got_up, ((0, fh), (fh, 2 * fh)), name="unshard_ffn_up")
    w_o, w_dn = got_o.reshape(d, d), got_dn.reshape(fh, d)
    ya = _mm([(y, w_a)], name="glu_a")
    yb = _mm([(y, w_b)], name="glu_b")
    ycv = _mm([(v4, w_pw)], name="conv_pw")
    merged = _glu_merge(ya, yb, ycv, gs, gc, name="merge")
    o = _mm([(merged, w_o)], name="out_proj")
    x1, h2 = _resid_ln_mod(xs, o, g1, ln1_g, ln1_b, sc2, sh2, alpha, name="resid_ln1")
    upa = _mm([(h2, w_upa)], name="ffn_up_a")
    upv = _mm([(h2, w_upv)], name="ffn_up_v")
    f = _ffn_mid(upa, upv, ffn_wa, ffn_wv, ffn_ba, ffn_bv, name="ffn_mid")
    y2 = _mm([(f, w_dn)], name="ffn_down")
    dr2, dy2, loss_part, d_ln2_g, d_ln2_b, d_g2 = _resid_ln_loss(x1, y2, g2, ln2_g, ln2_b, tgt, alpha, name="resid_ln2_loss")

    gw = {}
    df = _mm([(dy2, w_dn)], trans_w=True, name="d_ffn_down")
    gw["ffn_w_down"] = _mm_tn(f, dy2, out_dtype=bf16, name="g_ffn_down").reshape((NDEV,) + ffn_w_down[0].shape)
    received = {}
    dupa, dupv, d_ffn_wa, d_ffn_wv, d_ffn_ba, d_ffn_bv, (received["ffn_w_down"],) = _ffn_mid_bwd(
        df, upa, upv, ffn_wa, ffn_wv, ffn_ba, ffn_bv, name="ffn_mid_bwd", xchg=([gw["ffn_w_down"]], True))
    dh2 = _mm([(dupa, w_upa), (dupv, w_upv)], trans_w=True, name="d_ffn_up")
    gw["ffn_w_up"] = _shard_cols([_mm_tn(h2, dupa, name="g_ffn_up_a"), _mm_tn(h2, dupv, name="g_ffn_up_v")], out_dtype=bf16,
                                 name="shard_ffn_up")
    dr1, do, d_sc2, d_sh2, d_ln1_g, d_ln1_b, d_g1 = _mid_bwd(dh2, x1, dr2, xs, o, g1, sc2, ln1_g, alpha, name="mid_bwd")
    dmerged = _mm([(do, w_o)], trans_w=True, name="d_out_proj")
    gw["w_out"] = _mm_tn(merged, do, out_dtype=bf16, name="g_out_proj").reshape((NDEV,) + w_out[0].shape)
    dya, dyb, dycv, dgs, dgc, s_gs, s_gc = _glu_merge_bwd(dmerged, ya, yb, ycv, gs, gc, name="merge_bwd")
    dy = _mm([(dya, w_a), (dyb, w_b)], trans_w=True, name="d_glu")
    gw["ssm_glu_w_a"] = _shard_cols([_mm_tn(y, dya, name="g_glu_a")], out_dtype=bf16, name="shard_glu_a")
    gw["ssm_glu_w_b"] = _shard_cols([_mm_tn(y, dyb, name="g_glu_b")], out_dtype=bf16, name="shard_glu_b")
    dv4 = _mm([(dycv, w_pw)], trans_w=True, name="d_conv_pw")
    gw["cv_w_pw"] = _shard_cols([_mm_tn(v4, dycv, name="g_conv_pw")], out_dtype=bf16, name="shard_conv_pw")
    dv2, d_cv_ln_g, d_cv_ln_b = _conv_bwd_ln(dv4, cva, cvg, cv_w32, cv_dw_b, cv_ln_g, cv_ln_b, name="conv_bwd_ln")
    dcva, dcvg, d_cv_w32, d_cv_b, s_cva, s_cvg, (received["ffn_w_up"],) = _conv_bwd_taps(
        dv2, cva, cvg, cv_w32, name="conv_bwd_taps", xchg=([gw["ffn_w_up"]], True))
    late = ["w_out", "ssm_glu_w_a", "ssm_glu_w_b", "cv_w_pw"]
    (du, d_bre_m, d_bim_m, d_cre_m, d_cim_m, d_cfr, d_cfi, d_lbr, d_lbi, d_d, s_u, got_late) = _ssm_bwd(
        dy, yraw3, u, h_p, *ssm_args, name="ssm_bwd", xchg=([gw[n] for n in late], True))
    received.update(zip(late, got_late))
    gshape = lam_re.shape
    d_lam_re, d_lam_im, d_log_dt = _ssm_prep_bwd(
        lam_re, lam_im, log_dt, [a.reshape(gshape) for a in (d_lbr, d_lbi, d_cfr, d_cfi)], name="ssm_prep_bwd")
    gw["w_in"] = _shard_cols([_mm_tn(h1, t, name="g_in_" + nm) for nm, t in
                              (("u", du), ("cva", dcva), ("cvg", dcvg), ("gs", dgs), ("gc", dgc))], out_dtype=bf16, name="shard_w_in")
    small = {
        "b_in": jnp.concatenate([s_u, s_cva, s_cvg, s_gs, s_gc], axis=1),
        "ssm_lambda_re": d_lam_re, "ssm_lambda_im": d_lam_im, "ssm_log_dt": d_log_dt,
        "ssm_b_re": b_unmat(d_bre_m), "ssm_b_im": b_unmat(d_bim_m), "ssm_c_re": c_unmat(d_cre_m), "ssm_c_im": c_unmat(d_cim_m),
        "ssm_d": d_d, "cv_dw_w": d_cv_w32[:CONV_K], "cv_dw_b": d_cv_b, "cv_ln_g": d_cv_ln_g, "cv_ln_b": d_cv_ln_b,
        "ln1_g": d_ln1_g, "ln1_b": d_ln1_b,
        "ffn_dw_w": jnp.concatenate([d_ffn_wa[:FFN_K], d_ffn_wv[:FFN_K]], axis=1),
        "ffn_dw_b": jnp.concatenate([d_ffn_ba, d_ffn_bv], axis=1), "ln2_g": d_ln2_g, "ln2_b": d_ln2_b,
        "mod_g1": d_g1, "mod_sh2": d_sh2, "mod_sc2": d_sc2, "mod_g2": d_g2, "loss": loss_part,
    }
    small_names = list(small)
    small_shapes = [small[n].shape for n in small_names]
    dh1, (received["w_in"], small_all) = _mm(
        [(du, w_u), (dcva, w_cva), (dcvg, w_cvg), (dgs, w_gs), (dgc, w_gc)], trans_w=True, name="d_in",
        xchg=([gw["w_in"], _pack([small[n] for n in small_names])], [True, False]))
    grad_x, d_sc1, d_sh1, _ = _final_bwd(dh1, xs, dr1, sc1, alpha, name="final_bwd")

    grads, delta, new_m, new_v = {}, {}, {}, {}
    for n in big:
        ride = ([_pack([d_sh1, d_sc1])], False) if n == big[0] else None
        res = _sum_adamw(received[n], weights[n][0], mom_m[n][0], mom_v[n][0], name="adamw_" + n, xchg=ride)
        grads[n], delta[n], new_m[n], new_v[n] = res[:4]
        if ride is not None:
            last_all, = res[4]

    small_sum = dict(zip(small_names, _unpack(_sum_parts(small_all, name="sum_small").reshape(-1), small_shapes)))
    last_sum = _unpack(_sum_parts(last_all, name="sum_last").reshape(-1), [(1, d), (1, d)])
    per_dev = dict(zip(small_names, _unpack(small_all.reshape(NDEV, -1), small_shapes)))
    last_dev = _unpack(last_all.reshape(NDEV, -1), [(1, d), (1, d)])
    dmod_all = jnp.concatenate(last_dev + [per_dev[k] for k in ("mod_g1", "mod_sh2", "mod_sc2", "mod_g2")], axis=-1).reshape(NDEV, 6 * d)
    dmod_cols = lax.dynamic_slice(dmod_all.reshape(NDEV, NDEV, ncond), (0, me, 0), (NDEV, 1, ncond)).reshape(NDEV, ncond)
    grads["w_cond"] = _cond_bwd(c_all, dmod_cols, name="cond_bwd")
    loss = small_sum.pop("loss").reshape(())
    grads["b_cond"] = jnp.concatenate(last_sum + [small_sum.pop(k) for k in ("mod_g1", "mod_sh2", "mod_sc2", "mod_g2")], axis=1)
    for n, g in small_sum.items():
        grads[n] = g
    ntap = cv_dw_w.shape[3]
    grads["cv_dw_w"] = lax.dynamic_slice(grads["cv_dw_w"], (0, me * ntap), (CONV_K, ntap))
    nffn = ffn_dw_w.shape[3]
    grads["ffn_dw_w"] = lax.dynamic_slice(grads["ffn_dw_w"], (0, me * nffn), (FFN_K, nffn))
    grads = {n: grads[n].reshape(weights[n].shape) for n in names}

    delta["w_cond"], new_m["w_cond"], new_v["w_cond"] = _adamw(w_cond[0], grads["w_cond"][0], m_w_cond[0], v_w_cond[0],
                                                               name="adamw_w_cond")
    rest = [n for n in names if n not in ["w_cond"] + big]
    rest_shapes = [weights[n].shape for n in rest]
    packed = [_pack([t[n] for n in rest]) for t in (weights, grads, mom_m, mom_v)]
    for tgt_dict, res in zip((delta, new_m, new_v), _adamw(*packed, name="adamw_small")):
        for n, a in zip(rest, _unpack(res.reshape(-1), rest_shapes)):
            tgt_dict[n] = a
    shaped = lambda t: [t[n].reshape(weights[n].shape) for n in names]

    return (loss, grad_x[None], *shaped(grads), *shaped(delta), *shaped(new_m), *shaped(new_v))
```

```python
import functools
import math

import jax
import jax.numpy as jnp
from jax import lax
from jax.experimental import pallas as pl
from jax.experimental.pallas import tpu as pltpu

f32 = jnp.float32
bf16 = jnp.bfloat16

NDEV = 8
LANES = 128
SUBLANES = 8
SSM_GROUP = 16
SSM_STATE = 64
QW = 128
QS = 512
CONV_K = 31
CONV_HALO = 32
FFN_K = 3
FFN_HALO = 8
LN_EPS = 1e-5
ADAM_LR, ADAM_B1, ADAM_B2, ADAM_EPS, ADAM_WD, ADAM_STEP = 0.001, 0.9, 0.999, 1e-08, 0.01, 10
VMEM_LIMIT = 56 * 1024 * 1024
W_TILE_BYTES = 6 * 1024 * 1024
SUM_ROWS = 512
EW_BLOCK_BYTES = 2 * 1024 * 1024
INV_SQRT2 = 1.0 / math.sqrt(2.0)
INV_SQRT_2PI = 1.0 / math.sqrt(2.0 * math.pi)
MESH = pl.DeviceIdType.MESH


def _tile(n, want):
    t = min(n, want)
    while n % t:
        t //= 2
    return t


def _col_tile(n, rows, bytes_per):
    best = LANES if n % LANES == 0 else n
    for t in range(LANES, n + 1, LANES):
        if n % t == 0 and rows * t * bytes_per <= W_TILE_BYTES:
            best = t
    return best


def _params(*sem):
    return pltpu.CompilerParams(dimension_semantics=sem, vmem_limit_bytes=VMEM_LIMIT)


def _row(i):
    return (0, 0)


def _full(shape):
    nd = len(shape)
    return pl.BlockSpec(shape, lambda *a: (0,) * nd)


def _ln(x):
    mu = jnp.mean(x, axis=-1, keepdims=True)
    xc = x - mu
    var = jnp.mean(xc * xc, axis=-1, keepdims=True)
    rstd = lax.rsqrt(var + LN_EPS)
    return xc * rstd, rstd


def _ln_bwd(dxhat, xhat, rstd):
    return rstd * (dxhat - jnp.mean(dxhat, axis=-1, keepdims=True) - xhat * jnp.mean(dxhat * xhat, axis=-1, keepdims=True))


def _sig(x):
    return 1.0 / (1.0 + jnp.exp(-x))


def _gelu(x):
    return 0.5 * x * (1.0 + lax.erf(x * INV_SQRT2))


def _gelu_grad(x):
    return 0.5 * (1.0 + lax.erf(x * INV_SQRT2)) + x * jnp.exp(-0.5 * x * x) * INV_SQRT_2PI


def _colsum(x):
    return jnp.sum(x, axis=0, keepdims=True)


def _mm(pairs, bias=None, *, trans_w=False, out_dtype=f32, name, xchg=None):
    n_p = len(pairs)
    m = pairs[0][0].shape[0]
    n = pairs[0][1].shape[0 if trans_w else 1]
    ktot = sum(x.shape[1] for x, _ in pairs)
    tm = _tile(m, 512)
    tn = _col_tile(n, ktot, 2)
    dn = (((1,), (1,)), ((), ())) if trans_w else (((1,), (0,)), ((), ()))

    def body(*refs):
        o_ref = refs[-1]
        acc = None
        for xr, wr in zip(refs[:n_p], refs[n_p:2 * n_p]):
            r = lax.dot_general(xr[...].astype(bf16), wr[...].astype(bf16), dn, preferred_element_type=f32)
            acc = r if acc is None else acc + r
        if bias is not None:
            acc = acc + refs[2 * n_p][...]
        o_ref[...] = acc.astype(out_dtype)

    in_specs = [pl.BlockSpec((tm, x.shape[1]), lambda j, i: (i, 0)) for x, _ in pairs]
    if trans_w:
        in_specs += [pl.BlockSpec((tn, w.shape[1]), lambda j, i: (j, 0)) for _, w in pairs]
    else:
        in_specs += [pl.BlockSpec((w.shape[0], tn), lambda j, i: (0, j)) for _, w in pairs]
    args = [x for x, _ in pairs] + [w for _, w in pairs]
    if bias is not None:
        in_specs.append(pl.BlockSpec((1, tn), lambda j, i: (0, j)))
        args.append(bias)
    (out,), got = _call(
        body, args, name=name, grid=(n // tn, m // tm), in_specs=in_specs,
        out_specs=[pl.BlockSpec((tm, tn), lambda j, i: (i, j))], out_shape=[jax.ShapeDtypeStruct((m, n), out_dtype)],
        sem=("parallel", "arbitrary"), xchg=xchg)
    return out if xchg is None else (out, got)


def _mm_tn(x, dy, *, out_dtype=f32, name):
    m, k = x.shape
    n = dy.shape[1]
    tm = _tile(m, 512)
    tn = _col_tile(n, k, 4)
    steps = m // tm

    def body(x_ref, dy_ref, o_ref, *scratch):
        acc = scratch[0] if scratch else o_ref

        @pl.when(pl.program_id(1) == 0)
        def _():
            acc[...] = jnp.zeros_like(acc)

        acc[...] += lax.dot_general(x_ref[...].astype(bf16), dy_ref[...].astype(bf16), (((0,), (0,)), ((), ())),
                                    preferred_element_type=f32)
        if scratch:
            @pl.when(pl.program_id(1) == steps - 1)
            def _():
                o_ref[...] = acc[...].astype(out_dtype)

    return pl.pallas_call(
        body, name=name, grid=(n // tn, steps),
        in_specs=[pl.BlockSpec((tm, k), lambda j, i: (i, 0)), pl.BlockSpec((tm, tn), lambda j, i: (i, j))],
        out_specs=pl.BlockSpec((k, tn), lambda j, i: (0, j)),
        out_shape=jax.ShapeDtypeStruct((k, n), out_dtype),
        scratch_shapes=[] if out_dtype == f32 else [pltpu.VMEM((k, tn), f32)],
        compiler_params=_params("parallel", "arbitrary"),
    )(x, dy)


def _mm_tn_sharded(x, dys, *, out_dtype, name, xchg=None):
    m, k = x.shape
    widths = [dy.shape[1] for dy in dys]
    n = sum(widths) // NDEV
    tm = _tile(m, 512)
    steps = m // tm
    n_d = len(dys)

    def body(x_ref, *refs):
        dy_refs, o_ref, acc = refs[:n_d], refs[n_d], refs[n_d + 1]
        i = pl.program_id(0)

        @pl.when(i == 0)
        def _():
            acc[...] = jnp.zeros_like(acc)

        xb = x_ref[...].astype(bf16)
        off = 0
        for dy_ref, w in zip(dy_refs, widths):
            acc[:, off:off + w] += lax.dot_general(xb, dy_ref[...].astype(bf16), (((0,), (0,)), ((), ())), preferred_element_type=f32)
            off += w

        @pl.when(i == steps - 1)
        def _():
            for j in range(NDEV):
                o_ref[j] = acc[:, n * j:n * (j + 1)].astype(out_dtype)

    (out,), got = _call(
        body, (x, *dys), name=name, grid=(steps,),
        in_specs=[pl.BlockSpec((tm, k), lambda i: (i, 0))] + [pl.BlockSpec((tm, w), lambda i: (i, 0)) for w in widths],
        out_specs=[pl.BlockSpec((NDEV, k, n), lambda i: (0, 0, 0))], out_shape=[jax.ShapeDtypeStruct((NDEV, k, n), out_dtype)],
        scratch_shapes=[pltpu.VMEM((k, sum(widths)), f32)], sem=("arbitrary",), xchg=xchg)
    return out if xchg is None else (out, got)


def _exchange(arrs, *, scatter, name):
    n = len(arrs)

    def body(*refs):
        _exchange_copies(refs[:n], refs[n:2 * n], refs[2 * n:], scatter, True, True)

    return pl.pallas_call(
        body, name=name, in_specs=[HBM_SPEC] * n, out_specs=[HBM_SPEC] * n, out_shape=_exchange_out_shape(arrs, scatter),
        scratch_shapes=_exchange_sems(n),
    )(*arrs)


HBM_SPEC = pl.BlockSpec(memory_space=pltpu.HBM)


def _flags(scatter, n):
    return list(scatter) if isinstance(scatter, (list, tuple)) else [scatter] * n


def _exchange_out_shape(arrs, scatter):
    return [jax.ShapeDtypeStruct(a.shape if sc else (NDEV,) + a.shape, a.dtype) for a, sc in zip(arrs, _flags(scatter, len(arrs)))]


def _exchange_sems(n):
    return [pltpu.SemaphoreType.DMA(((NDEV - 1) * n,)), pltpu.SemaphoreType.DMA(((NDEV - 1) * n,)), pltpu.SemaphoreType.DMA((n,))]


def _exchange_copies(x_refs, o_refs, sems, scatter, start, wait):
    n = len(x_refs)
    flags = _flags(scatter, n)
    send_sems, recv_sems, local_sems = sems
    ix, iy, ic = lax.axis_index("x"), lax.axis_index("y"), lax.axis_index("c")
    me = 4 * ix + 2 * iy + ic
    local = [pltpu.make_async_copy(x.at[me] if sc else x, o.at[me], local_sems.at[a])
             for a, (x, o, sc) in enumerate(zip(x_refs, o_refs, flags))]
    sends, recvs = [], []
    for k in range(1, NDEV):
        px, py, pc = (1 - ix if k & 4 else ix, 1 - iy if k & 2 else iy, 1 - ic if k & 1 else ic)
        them = 4 * px + 2 * py + pc
        for a, (x, o, sc) in enumerate(zip(x_refs, o_refs, flags)):
            sem = (k - 1) * n + a
            sends.append(pltpu.make_async_remote_copy(
                src_ref=x.at[them] if sc else x, dst_ref=o.at[me], send_sem=send_sems.at[sem], recv_sem=recv_sems.at[sem],
                device_id=(px, py, pc), device_id_type=MESH))
            recvs.append(pltpu.make_async_remote_copy(
                src_ref=x.at[me] if sc else x, dst_ref=o.at[them], send_sem=send_sems.at[sem], recv_sem=recv_sems.at[sem],
                device_id=(px, py, pc), device_id_type=MESH))
    if start:
        for cp in local + sends:
            cp.start()
    if wait:
        for cp in recvs:
            cp.wait_recv()
        for cp in sends:
            cp.wait_send()
        for cp in local:
            cp.wait()


def _call(body, args, *, name, grid, in_specs, out_specs, out_shape, scratch_shapes=(), sem, xchg=None):
    if xchg is None:
        return pl.pallas_call(body, name=name, grid=grid, in_specs=in_specs, out_specs=out_specs, out_shape=out_shape,
                              scratch_shapes=list(scratch_shapes), compiler_params=_params(*sem))(*args), None
    arrs, scatter = xchg
    n, ni, no, ns = len(arrs), len(in_specs), len(out_specs), len(scratch_shapes)

    def wrapped(*refs):
        ins, x_refs = refs[:ni], refs[ni:ni + n]
        outs, o_refs = refs[ni + n:ni + n + no], refs[ni + n + no:ni + 2 * n + no]
        scratch, sems = refs[ni + 2 * n + no:ni + 2 * n + no + ns], refs[ni + 2 * n + no + ns:]
        ids = [pl.program_id(a) for a in range(len(grid))]
        first = functools.reduce(jnp.logical_and, [p == 0 for p in ids])
        last = functools.reduce(jnp.logical_and, [p == g - 1 for p, g in zip(ids, grid)])

        @pl.when(first)
        def _():
            _exchange_copies(x_refs, o_refs, sems, scatter, True, False)

        body(*ins, *outs, *scratch)

        @pl.when(last)
        def _():
            _exchange_copies(x_refs, o_refs, sems, scatter, False, True)

    res = pl.pallas_call(
        wrapped, name=name, grid=grid, in_specs=list(in_specs) + [HBM_SPEC] * n, out_specs=list(out_specs) + [HBM_SPEC] * n,
        out_shape=list(out_shape) + _exchange_out_shape(arrs, scatter),
        scratch_shapes=list(scratch_shapes) + _exchange_sems(n),
        compiler_params=_params(*("arbitrary",) * len(grid)))(*args, *arrs)
    return res[:no], res[no:]


def _sum_parts(parts, *, name):
    r = parts.shape[1]

    def body(p_ref, o_ref):
        acc = p_ref[0]
        for j in range(1, NDEV):
            acc = acc + p_ref[j]
        o_ref[...] = acc

    return pl.pallas_call(body, name=name, out_shape=jax.ShapeDtypeStruct((r, LANES), f32), compiler_params=_params())(parts)


def _col_pieces(n, bounds):
    out = []
    for p, (a, b) in enumerate(bounds):
        for j in range(NDEV):
            lo, hi = max(a, n * j), min(b, n * (j + 1))
            if lo < hi:
                out.append((p, j, lo - a, lo - n * j, hi - lo))
    return out


def _unshard_cols(stacked, bounds, *, name):
    _, k, n = stacked.shape
    tk = _tile(k, 256)
    plan = _col_pieces(n, bounds)

    def body(x_ref, *o_refs):
        for p, j, po, so, w in plan:
            o_refs[p][:, po:po + w] = x_ref[j, :, so:so + w]

    return pl.pallas_call(
        body, name=name, grid=(k // tk,), in_specs=[pl.BlockSpec((NDEV, tk, n), lambda i: (0, i, 0))],
        out_specs=[pl.BlockSpec((tk, b - a), lambda i: (i, 0)) for a, b in bounds],
        out_shape=[jax.ShapeDtypeStruct((k, b - a), stacked.dtype) for a, b in bounds],
        compiler_params=_params("parallel"))(stacked)


def _shard_cols(pieces, *, out_dtype, name):
    k = pieces[0].shape[0]
    bounds, off = [], 0
    for p in pieces:
        bounds.append((off, off + p.shape[1]))
        off += p.shape[1]
    n = off // NDEV
    tk = _tile(k, 256)
    plan = _col_pieces(n, bounds)

    def body(*refs):
        o_ref = refs[-1]
        for p, j, po, so, w in plan:
            o_ref[j, :, so:so + w] = refs[p][:, po:po + w].astype(out_dtype)

    return pl.pallas_call(
        body, name=name, grid=(k // tk,), in_specs=[pl.BlockSpec((tk, b - a), lambda i: (i, 0)) for a, b in bounds],
        out_specs=pl.BlockSpec((NDEV, tk, n), lambda i: (0, i, 0)),
        out_shape=jax.ShapeDtypeStruct((NDEV, k, n), out_dtype),
        compiler_params=_params("parallel"))(*pieces)


def _pack(arrs):
    flat = jnp.concatenate([a.reshape(-1) for a in arrs])
    pad = (-flat.shape[0]) % (SUBLANES * LANES)
    return jnp.pad(flat, (0, pad)).reshape(-1, LANES)


def _unpack(flat, shapes):
    out, off = [], 0
    for s in shapes:
        n = math.prod(s)
        out.append(flat[..., off:off + n].reshape(flat.shape[:-1] + tuple(s)))
        off += n
    return out


def _adamw_math(w, gg, m, v):
    nm = ADAM_B1 * m + (1.0 - ADAM_B1) * gg
    nv = ADAM_B2 * v + (1.0 - ADAM_B2) * (gg * gg)
    m_hat = nm / (1.0 - ADAM_B1 ** ADAM_STEP)
    v_hat = nv / (1.0 - ADAM_B2 ** ADAM_STEP)
    return -ADAM_LR * (m_hat / (jnp.sqrt(v_hat) + ADAM_EPS) + ADAM_WD * w), nm, nv


def _row_block(r, c, copies):
    tr = r
    while copies * tr * c * 4 > EW_BLOCK_BYTES and tr % (4 * SUBLANES) == 0:
        tr //= 2
    return tr


def _adamw(w, g, m, v, *, name):
    r, c = w.shape
    tr = _row_block(r, c, 1)

    def body(w_ref, g_ref, m_ref, v_ref, d_ref, nm_ref, nv_ref):
        d_ref[...], nm_ref[...], nv_ref[...] = _adamw_math(w_ref[...], g_ref[...], m_ref[...], v_ref[...])

    spec = pl.BlockSpec((tr, c), lambda i: (i, 0))
    shp = jax.ShapeDtypeStruct((r, c), f32)
    return pl.pallas_call(
        body, name=name, grid=(r // tr,), in_specs=[spec] * 4, out_specs=[spec] * 3, out_shape=[shp] * 3,
        compiler_params=_params("parallel"),
    )(w, g, m, v)


def _sum_adamw(parts, w, m, v, *, name, xchg=None):
    r, c = w.shape
    tr = _row_block(r, c, NDEV)

    def body(p_ref, w_ref, m_ref, v_ref, g_ref, d_ref, nm_ref, nv_ref):
        gg = p_ref[0].astype(f32)
        for j in range(1, NDEV):
            gg = gg + p_ref[j].astype(f32)
        g_ref[...] = gg
        d_ref[...], nm_ref[...], nv_ref[...] = _adamw_math(w_ref[...], gg, m_ref[...], v_ref[...])

    spec = pl.BlockSpec((tr, c), lambda i: (i, 0))
    shp = jax.ShapeDtypeStruct((r, c), f32)
    res, got = _call(
        body, (parts, w, m, v), name=name, grid=(r // tr,),
        in_specs=[pl.BlockSpec((NDEV, tr, c), lambda i: (0, i, 0))] + [spec] * 3,
        out_specs=[spec] * 4, out_shape=[shp] * 4, sem=("parallel",), xchg=xchg)
    return tuple(res) if xchg is None else tuple(res) + (got,)


def _cond_fwd(c_all, w, b, *, name):
    nb, n = c_all.shape[0], w.shape[1]

    def body(c_ref, w_ref, b_ref, o_ref):
        cc = c_ref[...]
        o_ref[...] = jnp.dot(cc * _sig(cc), w_ref[...], preferred_element_type=f32,
                             precision=lax.Precision.HIGHEST) + b_ref[...]

    return pl.pallas_call(body, name=name, out_shape=jax.ShapeDtypeStruct((nb, n), f32),
                          compiler_params=_params())(c_all, w, b)


def _cond_bwd(c_all, dmod, *, name):
    d, n = c_all.shape[1], dmod.shape[1]

    def body(c_ref, g_ref, o_ref):
        cc = c_ref[...]
        o_ref[...] = lax.dot_general(cc * _sig(cc), g_ref[...], (((0,), (0,)), ((), ())), preferred_element_type=f32,
                                     precision=lax.Precision.HIGHEST)

    return pl.pallas_call(body, name=name, out_shape=jax.ShapeDtypeStruct((d, n), f32),
                          compiler_params=_params())(c_all, dmod)


def _ssm_disc(lam_re, lam_im, log_dt):
    lr = jnp.minimum(lam_re, -1e-4)
    li = lam_im
    dt = jnp.exp(log_dt)
    mag = jnp.exp(lr * dt)
    ang = li * dt
    lbr, lbi = mag * jnp.cos(ang), mag * jnp.sin(ang)
    num_r, num_i = lbr - 1.0, lbi
    den = lr * lr + li * li
    return lbr, lbi, (num_r * lr + num_i * li) / den, (num_i * lr - num_r * li) / den


def _ssm_prep(lam_re, lam_im, log_dt, *, name):
    def body(a, b, c, o1, o2, o3, o4):
        o1[...], o2[...], o3[...], o4[...] = _ssm_disc(a[...], b[...], c[...])

    shp = jax.ShapeDtypeStruct(lam_re.shape, f32)
    return pl.pallas_call(body, name=name, out_shape=[shp] * 4, compiler_params=_params())(lam_re, lam_im, log_dt)


def _ssm_prep_bwd(lam_re, lam_im, log_dt, cts, *, name):
    def body(a, b, c, g1, g2, g3, g4, o1, o2, o3):
        _, vjp = jax.vjp(_ssm_disc, a[...], b[...], c[...])
        o1[...], o2[...], o3[...] = vjp((g1[...], g2[...], g3[...], g4[...]))

    shp = jax.ShapeDtypeStruct(lam_re.shape, f32)
    return pl.pallas_call(body, name=name, out_shape=[shp, shp, jax.ShapeDtypeStruct(log_dt.shape, f32)],
                          compiler_params=_params())(lam_re, lam_im, log_dt, *cts)


def _step_major(tb, nt, dtype):
    r = lax.broadcasted_iota(jnp.int32, (tb, tb), 0)
    k = lax.broadcasted_iota(jnp.int32, (tb, tb), 1)
    return (k == (r % SUBLANES) * nt + r // SUBLANES).astype(dtype)


def _chunk_major(tb, nt, dtype):
    k = lax.broadcasted_iota(jnp.int32, (tb, tb), 0)
    r = lax.broadcasted_iota(jnp.int32, (tb, tb), 1)
    return (k == (r % SUBLANES) * nt + r // SUBLANES).astype(dtype)


def _permute_f32(pmat, x):
    return jnp.dot(pmat, x, preferred_element_type=f32, precision=lax.Precision.HIGHEST)


def _permute_bf16(pmat, x):
    return jnp.dot(pmat, x, preferred_element_type=f32).astype(bf16)


def _chain_carries(loc_r, loc_i, pr, pi_, forward):
    row = lax.broadcasted_iota(jnp.int32, loc_r.shape, 0)
    shift = 1 if forward else SUBLANES - 1
    order = range(1, SUBLANES) if forward else range(SUBLANES - 2, -1, -1)
    er, ei = loc_r, loc_i
    for k in order:
        sr, si = pltpu.roll(er, shift, 0), pltpu.roll(ei, shift, 0)
        er = jnp.where(row == k, loc_r + pr * sr - pi_ * si, er)
        ei = jnp.where(row == k, loc_i + pr * si + pi_ * sr, ei)
    edge = 0 if forward else SUBLANES - 1
    return (jnp.where(row == edge, 0.0, pltpu.roll(er, shift, 0)), jnp.where(row == edge, 0.0, pltpu.roll(ei, shift, 0)))


def _chunk_power(ar, ai, chunk_len):
    pr, pi_ = ar, ai
    for _ in range(int(math.log2(chunk_len))):
        pr, pi_ = pr * pr - pi_ * pi_, 2.0 * pr * pi_
    return pr, pi_


def _ssm_mats(bre_ref, bim_ref, cre_ref, cim_ref, cfr_ref, cfi_ref, bbar_s, cmat_s, nq):
    for q in range(nq):
        cr, ci, br, bi = cfr_ref[q], cfi_ref[q], bre_ref[q], bim_ref[q]
        bbar_s[q, :, 0:QS] = (cr * br - ci * bi).astype(bf16)
        bbar_s[q, :, QS:2 * QS] = (cr * bi + ci * br).astype(bf16)
        cmat_s[q, 0:QS, :] = cre_ref[q].astype(bf16)
        cmat_s[q, QS:2 * QS, :] = (-cim_ref[q]).astype(bf16)


def _ssm_fwd(u, ar, ai, bre, bim, cre, cim, cfr, cfi, dvec, *, name, xchg=None):
    s, sw = u.shape
    nq = sw // QW
    st = nq * 2 * QS
    tb = _tile(s, 256)
    nb, nt, chunk_len = s // tb, tb // SUBLANES, s // SUBLANES
    assert chunk_len & (chunk_len - 1) == 0 and nt % 16 == 0

    def body(u_ref, ar_ref, ai_ref, bre_ref, bim_ref, cre_ref, cim_ref, cfr_ref, cfi_ref, d_ref,
             h_out, yraw_out, y_out, buf, hc, bbar_s, cmat_s):
        ph, i = pl.program_id(0), pl.program_id(1)

        @pl.when(i == 0)
        def _():
            _ssm_mats(bre_ref, bim_ref, cre_ref, cim_ref, cfr_ref, cfi_ref, bbar_s, cmat_s, nq)

        @pl.when((ph == 0) & (i == 0))
        def _():
            hc[...] = jnp.zeros_like(hc)

        @pl.when((ph == 1) & (i == 0))
        def _():
            for q in range(nq):
                o = q * 2 * QS
                pr, pi_ = _chunk_power(ar_ref[q], ai_ref[q], chunk_len)
                sr, si = _chain_carries(hc[:, o:o + QS], hc[:, o + QS:o + 2 * QS], pr, pi_, True)
                hc[:, o:o + QS] = sr
                hc[:, o + QS:o + 2 * QS] = si

        uu = u_ref[...].reshape(tb, sw)
        up = _permute_bf16(_step_major(tb, nt, bf16), uu.astype(bf16))
        for q in range(nq):
            o = q * 2 * QS
            buf[:, o:o + 2 * QS] = jnp.dot(up[:, q * QW:(q + 1) * QW], bbar_s[q], preferred_element_type=f32)

        for q in range(nq):
            o = q * 2 * QS
            a_r = jnp.broadcast_to(ar_ref[q], (SUBLANES, QS))
            a_i = jnp.broadcast_to(ai_ref[q], (SUBLANES, QS))

            def step(t, carry, o=o, a_r=a_r, a_i=a_i):
                hr, hi = carry
                r0 = pl.multiple_of(t * SUBLANES, SUBLANES)
                nr = a_r * hr - a_i * hi + buf[pl.ds(r0, SUBLANES), o:o + QS]
                ni = a_r * hi + a_i * hr + buf[pl.ds(r0, SUBLANES), o + QS:o + 2 * QS]
                buf[pl.ds(r0, SUBLANES), o:o + QS] = nr
                buf[pl.ds(r0, SUBLANES), o + QS:o + 2 * QS] = ni
                return nr, ni

            hr, hi = lax.fori_loop(0, nt, step, (hc[:, o:o + QS], hc[:, o + QS:o + 2 * QS]))
            hc[:, o:o + QS] = hr
            hc[:, o + QS:o + 2 * QS] = hi

        @pl.when(ph == 1)
        def _():
            back = _chunk_major(tb, nt, f32)
            for q in range(nq):
                o = q * 2 * QS
                cs = slice(q * QW, (q + 1) * QW)
                hq = buf[:, o:o + 2 * QS].astype(bf16)
                h_out[:, o:o + 2 * QS] = hq
                yq = _permute_f32(back, jnp.dot(hq, cmat_s[q], preferred_element_type=f32)) + d_ref[:, cs] * uu[:, cs]
                yraw_out[:, :, cs] = yq.reshape(SUBLANES, nt, QW)
                y_out[:, :, cs] = _gelu(yq).astype(bf16).reshape(SUBLANES, nt, QW)

    blk = lambda ph, i: (0, i, 0)
    oblk = lambda ph, i: (0, i * ph, 0)
    act = lambda dt: jax.ShapeDtypeStruct((SUBLANES, chunk_len, sw), dt)
    (h_p, yraw3, y3), got = _call(
        body, (u.reshape(SUBLANES, chunk_len, sw), ar, ai, bre, bim, cre, cim, cfr, cfi, dvec), name=name, grid=(2, nb),
        in_specs=[pl.BlockSpec((SUBLANES, nt, sw), blk), _full(ar.shape), _full(ai.shape), _full(bre.shape), _full(bim.shape),
                  _full(cre.shape), _full(cim.shape), _full(cfr.shape), _full(cfi.shape), _full(dvec.shape)],
        out_specs=[pl.BlockSpec((tb, st), lambda ph, i: (i * ph, 0)), pl.BlockSpec((SUBLANES, nt, sw), oblk),
                   pl.BlockSpec((SUBLANES, nt, sw), oblk)],
        out_shape=[jax.ShapeDtypeStruct((s, st), bf16), act(f32), act(bf16)],
        scratch_shapes=[pltpu.VMEM((tb, st), f32), pltpu.VMEM((SUBLANES, st), f32),
                        pltpu.VMEM((nq, QW, 2 * QS), bf16), pltpu.VMEM((nq, 2 * QS, QW), bf16)],
        sem=("arbitrary", "arbitrary"), xchg=xchg)
    return h_p, yraw3, y3.reshape(s, sw), got


def _ssm_bwd(dy, yraw3, u, h_p, ar, ai, bre, bim, cre, cim, cfr, cfi, dvec, *, name, xchg=None):
    s, sw = u.shape
    nq = sw // QW
    st = nq * 2 * QS
    tb = _tile(s, 256)
    nb, nt, chunk_len = s // tb, tb // SUBLANES, s // SUBLANES

    def body(dy_ref, yraw_ref, u_ref, h_ref, ar_ref, ai_ref, bre_ref, bim_ref, cre_ref, cim_ref, cfr_ref, cfi_ref, d_ref,
             du_out, dbre_out, dbim_out, dcre_out, dcim_out, dcfr_out, dcfi_out, dlbr_out, dlbi_out, dd_out, dbu_out,
             buf, hf, rc, acc, dbbar, dcmat, bbar_s, cmat_s):
        ph, i = pl.program_id(0), pl.program_id(1)

        @pl.when(i == 0)
        def _():
            _ssm_mats(bre_ref, bim_ref, cre_ref, cim_ref, cfr_ref, cfi_ref, bbar_s, cmat_s, nq)

        @pl.when((ph == 0) & (i == 0))
        def _():
            rc[...] = jnp.zeros_like(rc)

        @pl.when((ph == 1) & (i == 0))
        def _():
            for q in range(nq):
                o = q * 2 * QS
                pr, pi_ = _chunk_power(ar_ref[q], ai_ref[q], chunk_len)
                sr, si = _chain_carries(rc[:, o:o + QS], rc[:, o + QS:o + 2 * QS], pr, -pi_, False)
                rc[:, o:o + QS] = sr
                rc[:, o + QS:o + 2 * QS] = si
            acc[...] = jnp.zeros_like(acc)
            dbbar[...] = jnp.zeros_like(dbbar)
            dcmat[...] = jnp.zeros_like(dcmat)
            dd_out[...] = jnp.zeros_like(dd_out)
            dbu_out[...] = jnp.zeros_like(dbu_out)

        dyraw = (dy_ref[...] * _gelu_grad(yraw_ref[...])).reshape(tb, sw)
        fwd_perm = _step_major(tb, nt, bf16)
        dyp = _permute_bf16(fwd_perm, dyraw.astype(bf16))
        for q in range(nq):
            o = q * 2 * QS
            buf[:, o:o + 2 * QS] = lax.dot_general(dyp[:, q * QW:(q + 1) * QW], cmat_s[q], (((1,), (1,)), ((), ())),
                                                   preferred_element_type=f32)

        def recur(with_grad):
            for q in range(nq):
                o = q * 2 * QS
                a_r = jnp.broadcast_to(ar_ref[q], (SUBLANES, QS))
                a_i = jnp.broadcast_to(ai_ref[q], (SUBLANES, QS))

                def step(j, carry, o=o, a_r=a_r, a_i=a_i):
                    r0 = pl.multiple_of((nt - 1 - j) * SUBLANES, SUBLANES)
                    if with_grad:
                        rr, ri, gr, gi = carry
                        hr = hf[pl.ds(r0, SUBLANES), o:o + QS]
                        hi = hf[pl.ds(r0, SUBLANES), o + QS:o + 2 * QS]
                        gr = gr + hr * rr + hi * ri
                        gi = gi + hr * ri - hi * rr
                    else:
                        rr, ri = carry
                    nr = buf[pl.ds(r0, SUBLANES), o:o + QS] + a_r * rr + a_i * ri
                    ni = buf[pl.ds(r0, SUBLANES), o + QS:o + 2 * QS] + a_r * ri - a_i * rr
                    buf[pl.ds(r0, SUBLANES), o:o + QS] = nr
                    buf[pl.ds(r0, SUBLANES), o + QS:o + 2 * QS] = ni
                    return (nr, ni, gr, gi) if with_grad else (nr, ni)

                init = (rc[:, o:o + QS], rc[:, o + QS:o + 2 * QS])
                if with_grad:
                    init = init + (acc[:, o:o + QS], acc[:, o + QS:o + 2 * QS])
                res = lax.fori_loop(0, nt, step, init)
                rc[:, o:o + QS] = res[0]
                rc[:, o + QS:o + 2 * QS] = res[1]
                if with_grad:
                    acc[:, o:o + QS] = res[2]
                    acc[:, o + QS:o + 2 * QS] = res[3]

        @pl.when(ph == 0)
        def _():
            recur(False)

        @pl.when(ph == 1)
        def _():
            hf[...] = h_ref[...].astype(f32)
            recur(True)
            uu = u_ref[...].reshape(tb, sw)
            up = _permute_bf16(fwd_perm, uu.astype(bf16))
            back = _chunk_major(tb, nt, f32)
            dd_out[...] += _colsum(dyraw * uu)
            for q in range(nq):
                o = q * 2 * QS
                cs = slice(q * QW, (q + 1) * QW)
                lam = buf[:, o:o + 2 * QS].astype(bf16)
                duq = _permute_f32(back, lax.dot_general(lam, bbar_s[q], (((1,), (1,)), ((), ())), preferred_element_type=f32)) \
                    + d_ref[:, cs] * dyraw[:, cs]
                du_out[:, :, cs] = duq.astype(bf16).reshape(SUBLANES, nt, QW)
                dbu_out[:, cs] += _colsum(duq)
                dbbar[q] += lax.dot_general(up[:, cs], lam, (((0,), (0,)), ((), ())), preferred_element_type=f32)
                dcmat[q] += lax.dot_general(h_ref[:, o:o + 2 * QS], dyp[:, cs], (((0,), (0,)), ((), ())),
                                            preferred_element_type=f32)

        @pl.when((ph == 1) & (i == nb - 1))
        def _():
            for q in range(nq):
                o = q * 2 * QS
                cr, ci, br, bi = cfr_ref[q], cfi_ref[q], bre_ref[q], bim_ref[q]
                gr, gi = dbbar[q, :, 0:QS], dbbar[q, :, QS:2 * QS]
                dbre_out[q] = cr * gr + ci * gi
                dbim_out[q] = cr * gi - ci * gr
                dcfr_out[q] = _colsum(gr * br + gi * bi)
                dcfi_out[q] = _colsum(gi * br - gr * bi)
                dcre_out[q] = dcmat[q, 0:QS, :]
                dcim_out[q] = -dcmat[q, QS:2 * QS, :]
                dlbr_out[q] = _colsum(acc[:, o:o + QS])
                dlbi_out[q] = _colsum(acc[:, o + QS:o + 2 * QS])

    blk = lambda ph, i: (0, nb - 1 - i, 0)
    oblk = lambda ph, i: (0, (nb - 1 - i) * ph + (nb - 1) * (1 - ph), 0)
    pshapes = [ar.shape, ai.shape, bre.shape, bim.shape, cre.shape, cim.shape, cfr.shape, cfi.shape, dvec.shape]
    oshapes = [bre.shape, bim.shape, cre.shape, cim.shape, cfr.shape, cfi.shape, ar.shape, ai.shape, dvec.shape, dvec.shape]
    act = pl.BlockSpec((SUBLANES, nt, sw), blk)
    view = lambda a: a.reshape(SUBLANES, chunk_len, sw)
    res, got = _call(
        body, (view(dy), yraw3, view(u), h_p, ar, ai, bre, bim, cre, cim, cfr, cfi, dvec), name=name, grid=(2, nb),
        in_specs=[act, act, act, pl.BlockSpec((tb, st), lambda ph, i: (nb - 1 - i, 0))] + [_full(p) for p in pshapes],
        out_specs=[pl.BlockSpec((SUBLANES, nt, sw), oblk)] + [_full(p) for p in oshapes],
        out_shape=[jax.ShapeDtypeStruct((SUBLANES, chunk_len, sw), bf16)] + [jax.ShapeDtypeStruct(p, f32) for p in oshapes],
        scratch_shapes=[pltpu.VMEM((tb, st), f32), pltpu.VMEM((tb, st), f32),
                        pltpu.VMEM((SUBLANES, st), f32), pltpu.VMEM((SUBLANES, st), f32),
                        pltpu.VMEM((nq, QW, 2 * QS), f32), pltpu.VMEM((nq, 2 * QS, QW), f32),
                        pltpu.VMEM((nq, QW, 2 * QS), bf16), pltpu.VMEM((nq, 2 * QS, QW), bf16)],
        sem=("arbitrary", "arbitrary"), xchg=xchg)
    return (res[0].reshape(s, sw),) + tuple(res[1:]) + (got,)


def _lnmod(x, sc, sh, *, name):
    s, d = x.shape
    tb = _tile(s, 512)

    def body(x_ref, sc_ref, sh_ref, o_ref):
        xh, _ = _ln(x_ref[...])
        o_ref[...] = (xh * (1.0 + sc_ref[...]) + sh_ref[...]).astype(bf16)

    blk = pl.BlockSpec((tb, d), lambda i: (i, 0))
    vec = pl.BlockSpec((1, d), _row)
    return pl.pallas_call(body, name=name, grid=(s // tb,), in_specs=[blk, vec, vec], out_specs=blk,
                          out_shape=jax.ShapeDtypeStruct((s, d), bf16), compiler_params=_params("parallel"))(x, sc, sh)


def _conv_taps(ext_ref, w_ref, tb, ntap, off):
    acc = ext_ref[pl.ds(off, tb), :] * w_ref[pl.ds(0, 1), :]
    for k in range(1, ntap):
        acc = acc + ext_ref[pl.ds(off + k, tb), :] * w_ref[pl.ds(k, 1), :]
    return acc


def _conv_halo_specs(tb, cw, halo, s):
    per = tb // halo
    prev = pl.BlockSpec((halo, cw), lambda i: (jnp.maximum(i * per - 1, 0), 0))
    nxt = pl.BlockSpec((halo, cw), lambda i: (jnp.minimum((i + 1) * per, s // halo - 1), 0))
    return prev, nxt


def _conv_v2(a_ref, g_ref, ah_ref, gh_ref, w_ref, b_ref, ext, tb, i):
    gg = g_ref[...]
    ext[pl.ds(CONV_HALO, tb), :] = a_ref[...] * _sig(gg)
    ext[pl.ds(0, CONV_HALO), :] = jnp.where(i > 0, ah_ref[...] * _sig(gh_ref[...]), 0.0)
    return _conv_taps(ext, w_ref, tb, CONV_K, CONV_HALO - CONV_K + 1) + b_ref[...]


def _silu_grad(x):
    sg = _sig(x)
    return sg * (1.0 + x * (1.0 - sg))


def _conv_fwd(cva, cvg, w, b, lng, lnb, *, name, xchg=None):
    s, cw = cva.shape
    tb = _tile(s, 256)
    prev, _ = _conv_halo_specs(tb, cw, CONV_HALO, s)

    def body(a_ref, g_ref, ah_ref, gh_ref, w_ref, b_ref, lng_ref, lnb_ref, o_ref, ext):
        v2 = _conv_v2(a_ref, g_ref, ah_ref, gh_ref, w_ref, b_ref, ext, tb, pl.program_id(0))
        xh, _ = _ln(v2)
        v3 = xh * lng_ref[...] + lnb_ref[...]
        o_ref[...] = (v3 * _sig(v3)).astype(bf16)

    blk = pl.BlockSpec((tb, cw), lambda i: (i, 0))
    vec = pl.BlockSpec((1, cw), _row)
    (v4,), got = _call(
        body, (cva, cvg, cva, cvg, w, b, lng, lnb), name=name, grid=(s // tb,),
        in_specs=[blk, blk, prev, prev, _full(w.shape), vec, vec, vec], out_specs=[blk],
        out_shape=[jax.ShapeDtypeStruct((s, cw), bf16)], scratch_shapes=[pltpu.VMEM((tb + CONV_HALO, cw), f32)],
        sem=("parallel",), xchg=xchg)
    return v4, got


def _conv_bwd_ln(dv4, cva, cvg, w, b, lng, lnb, *, name):
    s, cw = cva.shape
    tb = _tile(s, 256)
    prev, _ = _conv_halo_specs(tb, cw, CONV_HALO, s)

    def body(d_ref, a_ref, g_ref, ah_ref, gh_ref, w_ref, b_ref, lng_ref, lnb_ref, o_ref, dg_ref, db_ref, ext):
        i = pl.program_id(0)

        @pl.when(i == 0)
        def _():
            dg_ref[...] = jnp.zeros_like(dg_ref)
            db_ref[...] = jnp.zeros_like(db_ref)

        v2 = _conv_v2(a_ref, g_ref, ah_ref, gh_ref, w_ref, b_ref, ext, tb, i)
        xh, rstd = _ln(v2)
        v3 = xh * lng_ref[...] + lnb_ref[...]
        dv3 = d_ref[...] * _silu_grad(v3)
        dg_ref[...] += _colsum(dv3 * xh)
        db_ref[...] += _colsum(dv3)
        o_ref[...] = _ln_bwd(dv3 * lng_ref[...], xh, rstd)

    blk = pl.BlockSpec((tb, cw), lambda i: (i, 0))
    vec = pl.BlockSpec((1, cw), _row)
    vshape = jax.ShapeDtypeStruct((1, cw), f32)
    return pl.pallas_call(
        body, name=name, grid=(s // tb,), in_specs=[blk, blk, blk, prev, prev, _full(w.shape), vec, vec, vec],
        out_specs=[blk, vec, vec], out_shape=[jax.ShapeDtypeStruct((s, cw), f32), vshape, vshape],
        scratch_shapes=[pltpu.VMEM((tb + CONV_HALO, cw), f32)],
        compiler_params=_params("arbitrary"))(dv4, cva, cvg, cva, cvg, w, b, lng, lnb)


def _conv_bwd_taps(dv2, cva, cvg, w, *, name, xchg=None):
    s, cw = cva.shape
    tb = _tile(s, 256)
    nb = s // tb
    prev, nxt = _conv_halo_specs(tb, cw, CONV_HALO, s)

    def body(d_ref, dn_ref, a_ref, g_ref, ah_ref, gh_ref, w_ref, da_ref, dg_ref, dw_ref, db_ref, sa_ref, sg_ref, ext, dext):
        i = pl.program_id(0)

        @pl.when(i == 0)
        def _():
            for r in (dw_ref, db_ref, sa_ref, sg_ref):
                r[...] = jnp.zeros_like(r)

        aa, gg = a_ref[...], g_ref[...]
        sg = _sig(gg)
        ext[pl.ds(CONV_HALO, tb), :] = aa * sg
        ext[pl.ds(0, CONV_HALO), :] = jnp.where(i > 0, ah_ref[...] * _sig(gh_ref[...]), 0.0)
        dd = d_ref[...]
        dext[pl.ds(0, tb), :] = dd
        dext[pl.ds(tb, CONV_HALO), :] = jnp.where(i < nb - 1, dn_ref[...], 0.0)
        dv = dext[pl.ds(CONV_K - 1, tb), :] * w_ref[pl.ds(0, 1), :]
        for k in range(1, CONV_K):
            dv = dv + dext[pl.ds(CONV_K - 1 - k, tb), :] * w_ref[pl.ds(k, 1), :]
        for k in range(CONV_K):
            dw_ref[pl.ds(k, 1), :] += _colsum(dd * ext[pl.ds(CONV_HALO - CONV_K + 1 + k, tb), :])
        db_ref[...] += _colsum(dd)
        da = dv * sg
        dgate = dv * aa * sg * (1.0 - sg)
        sa_ref[...] += _colsum(da)
        sg_ref[...] += _colsum(dgate)
        da_ref[...] = da.astype(bf16)
        dg_ref[...] = dgate.astype(bf16)

    blk = pl.BlockSpec((tb, cw), lambda i: (i, 0))
    vec = pl.BlockSpec((1, cw), _row)
    vshape = jax.ShapeDtypeStruct((1, cw), f32)
    act = jax.ShapeDtypeStruct((s, cw), bf16)
    res, got = _call(
        body, (dv2, dv2, cva, cvg, cva, cvg, w), name=name, grid=(nb,), in_specs=[blk, nxt, blk, blk, prev, prev, _full(w.shape)],
        out_specs=[blk, blk, _full(w.shape), vec, vec, vec],
        out_shape=[act, act, jax.ShapeDtypeStruct(w.shape, f32), vshape, vshape, vshape],
        scratch_shapes=[pltpu.VMEM((tb + CONV_HALO, cw), f32), pltpu.VMEM((tb + CONV_HALO, cw), f32)],
        sem=("arbitrary",), xchg=xchg)
    return tuple(res) + (got,)


def _glu_merge(ya, yb, ycv, gs, gc, *, name):
    s, d = ya.shape
    tb = _tile(s, 512)

    def body(ya_ref, yb_ref, ycv_ref, gs_ref, gc_ref, o_ref):
        ld = lambda r: r[...].astype(f32)
        z = ld(ya_ref) * _sig(ld(yb_ref))
        o_ref[...] = (_sig(ld(gs_ref)) * z + _sig(ld(gc_ref)) * ld(ycv_ref)).astype(bf16)

    blk = pl.BlockSpec((tb, d), lambda i: (i, 0))
    return pl.pallas_call(body, name=name, grid=(s // tb,), in_specs=[blk] * 5, out_specs=blk,
                          out_shape=jax.ShapeDtypeStruct((s, d), bf16), compiler_params=_params("parallel"))(ya, yb, ycv, gs, gc)


def _glu_merge_bwd(dm, ya, yb, ycv, gs, gc, *, name):
    s, d = ya.shape
    tb = _tile(s, 512)

    def body(dm_ref, ya_ref, yb_ref, ycv_ref, gs_ref, gc_ref, dya_ref, dyb_ref, dycv_ref, dgs_ref, dgc_ref, sgs_ref, sgc_ref):
        @pl.when(pl.program_id(0) == 0)
        def _():
            sgs_ref[...] = jnp.zeros_like(sgs_ref)
            sgc_ref[...] = jnp.zeros_like(sgc_ref)

        ld = lambda r: r[...].astype(f32)
        dmv, yav = ld(dm_ref), ld(ya_ref)
        sb, ss, scv = _sig(ld(yb_ref)), _sig(ld(gs_ref)), _sig(ld(gc_ref))
        z = yav * sb
        dz = dmv * ss
        dgs = dmv * z * ss * (1.0 - ss)
        dgc = dmv * ld(ycv_ref) * scv * (1.0 - scv)
        dya_ref[...] = (dz * sb).astype(bf16)
        dyb_ref[...] = (dz * yav * sb * (1.0 - sb)).astype(bf16)
        dycv_ref[...] = (dmv * scv).astype(bf16)
        dgs_ref[...] = dgs.astype(bf16)
        dgc_ref[...] = dgc.astype(bf16)
        sgs_ref[...] += _colsum(dgs)
        sgc_ref[...] += _colsum(dgc)

    blk = pl.BlockSpec((tb, d), lambda i: (i, 0))
    vec = pl.BlockSpec((1, d), _row)
    act = jax.ShapeDtypeStruct((s, d), bf16)
    vshape = jax.ShapeDtypeStruct((1, d), f32)
    return pl.pallas_call(body, name=name, grid=(s // tb,), in_specs=[blk] * 6, out_specs=[blk] * 5 + [vec, vec],
                          out_shape=[act] * 5 + [vshape, vshape], compiler_params=_params("arbitrary"))(dm, ya, yb, ycv, gs, gc)


def _resid_ln_mod(x, o, g, lng, lnb, sc, sh, alpha, *, name):
    s, d = x.shape
    tb = _tile(s, 512)

    def body(x_ref, o_ref, g_ref, lng_ref, lnb_ref, sc_ref, sh_ref, x1_ref, h_ref):
        xh, _ = _ln(alpha * x_ref[...] + g_ref[...] * o_ref[...])
        x1 = xh * lng_ref[...] + lnb_ref[...]
        x1_ref[...] = x1
        xh1, _ = _ln(x1)
        h_ref[...] = (xh1 * (1.0 + sc_ref[...]) + sh_ref[...]).astype(bf16)

    blk = pl.BlockSpec((tb, d), lambda i: (i, 0))
    vec = pl.BlockSpec((1, d), _row)
    return pl.pallas_call(body, name=name, grid=(s // tb,), in_specs=[blk, blk] + [vec] * 5, out_specs=[blk, blk],
                          out_shape=[jax.ShapeDtypeStruct((s, d), f32), jax.ShapeDtypeStruct((s, d), bf16)],
                          compiler_params=_params("parallel"))(x, o, g, lng, lnb, sc, sh)


def _resid_ln_loss(x1, y2, g, lng, lnb, tgt, alpha, *, name):
    s, d = x1.shape
    tb = _tile(s, 512)

    def body(x1_ref, y_ref, g_ref, lng_ref, lnb_ref, t_ref, dr_ref, dy_ref, loss_ref, dlg_ref, dlb_ref, dg_ref):
        @pl.when(pl.program_id(0) == 0)
        def _():
            for r in (loss_ref, dlg_ref, dlb_ref, dg_ref):
                r[...] = jnp.zeros_like(r)

        yv = y_ref[...]
        xh, rstd = _ln(alpha * x1_ref[...] + g_ref[...] * yv)
        err = xh * lng_ref[...] + lnb_ref[...] - t_ref[...]
        loss_ref[...] += 0.5 * jnp.sum(jnp.sum(err * err, axis=-1, keepdims=True) / d, axis=0, keepdims=True)
        dx2 = err / d
        dlg_ref[...] += _colsum(dx2 * xh)
        dlb_ref[...] += _colsum(dx2)
        dr = _ln_bwd(dx2 * lng_ref[...], xh, rstd)
        dg_ref[...] += _colsum(dr * yv)
        dr_ref[...] = dr
        dy_ref[...] = (g_ref[...] * dr).astype(bf16)

    blk = pl.BlockSpec((tb, d), lambda i: (i, 0))
    vec = pl.BlockSpec((1, d), _row)
    vshape = jax.ShapeDtypeStruct((1, d), f32)
    return pl.pallas_call(
        body, name=name, grid=(s // tb,), in_specs=[blk, blk, vec, vec, vec, blk],
        out_specs=[blk, blk, pl.BlockSpec((1, 1), _row), vec, vec, vec],
        out_shape=[jax.ShapeDtypeStruct((s, d), f32), jax.ShapeDtypeStruct((s, d), bf16),
                   jax.ShapeDtypeStruct((1, 1), f32), vshape, vshape, vshape],
        compiler_params=_params("arbitrary"))(x1, y2, g, lng, lnb, tgt)


def _mid_bwd(dh2, x1, dr2, x, o, g, sc, lng, alpha, *, name):
    s, d = x.shape
    tb = _tile(s, 512)

    def body(dh_ref, x1_ref, dr2_ref, x_ref, o_ref, g_ref, sc_ref, lng_ref,
             dr1_ref, do_ref, dsc_ref, dsh_ref, dlg_ref, dlb_ref, dg_ref):
        @pl.when(pl.program_id(0) == 0)
        def _():
            for r in (dsc_ref, dsh_ref, dlg_ref, dlb_ref, dg_ref):
                r[...] = jnp.zeros_like(r)

        dh = dh_ref[...]
        xh1, rstd1 = _ln(x1_ref[...])
        dsc_ref[...] += _colsum(dh * xh1)
        dsh_ref[...] += _colsum(dh)
        dx1 = alpha * dr2_ref[...] + _ln_bwd(dh * (1.0 + sc_ref[...]), xh1, rstd1)
        ov = o_ref[...]
        xhr, rstdr = _ln(alpha * x_ref[...] + g_ref[...] * ov)
        dlg_ref[...] += _colsum(dx1 * xhr)
        dlb_ref[...] += _colsum(dx1)
        dr1 = _ln_bwd(dx1 * lng_ref[...], xhr, rstdr)
        dg_ref[...] += _colsum(dr1 * ov)
        dr1_ref[...] = dr1
        do_ref[...] = (g_ref[...] * dr1).astype(bf16)

    blk = pl.BlockSpec((tb, d), lambda i: (i, 0))
    vec = pl.BlockSpec((1, d), _row)
    vshape = jax.ShapeDtypeStruct((1, d), f32)
    return pl.pallas_call(
        body, name=name, grid=(s // tb,), in_specs=[blk] * 5 + [vec] * 3, out_specs=[blk, blk] + [vec] * 5,
        out_shape=[jax.ShapeDtypeStruct((s, d), f32), jax.ShapeDtypeStruct((s, d), bf16)] + [vshape] * 5,
        compiler_params=_params("arbitrary"))(dh2, x1, dr2, x, o, g, sc, lng)


def _final_bwd(dh1, x, dr1, sc, alpha, *, name, xchg=None):
    s, d = x.shape
    tb = _tile(s, 512)

    def body(dh_ref, x_ref, dr1_ref, sc_ref, dx_ref, dsc_ref, dsh_ref):
        @pl.when(pl.program_id(0) == 0)
        def _():
            dsc_ref[...] = jnp.zeros_like(dsc_ref)
            dsh_ref[...] = jnp.zeros_like(dsh_ref)

        dh = dh_ref[...]
        xh, rstd = _ln(x_ref[...])
        dsc_ref[...] += _colsum(dh * xh)
        dsh_ref[...] += _colsum(dh)
        dx_ref[...] = alpha * dr1_ref[...] + _ln_bwd(dh * (1.0 + sc_ref[...]), xh, rstd)

    blk = pl.BlockSpec((tb, d), lambda i: (i, 0))
    vec = pl.BlockSpec((1, d), _row)
    vshape = jax.ShapeDtypeStruct((1, d), f32)
    res, got = _call(body, (dh1, x, dr1, sc), name=name, grid=(s // tb,), in_specs=[blk, blk, blk, vec], out_specs=[blk, vec, vec],
                     out_shape=[jax.ShapeDtypeStruct((s, d), f32), vshape, vshape], sem=("arbitrary",), xchg=xchg)
    return tuple(res) + (got,)


def _ffn_specs(s, fh, tb, tc):
    per = tb // FFN_HALO
    blk = pl.BlockSpec((tb, tc), lambda j, i: (i, j))
    prev = pl.BlockSpec((FFN_HALO, tc), lambda j, i: (jnp.maximum(i * per - 1, 0), j))
    nxt = pl.BlockSpec((FFN_HALO, tc), lambda j, i: (jnp.minimum((i + 1) * per, s // FFN_HALO - 1), j))
    taps = pl.BlockSpec((FFN_HALO, tc), lambda j, i: (0, j))
    vec = pl.BlockSpec((1, tc), lambda j, i: (0, j))
    return blk, prev, nxt, taps, vec


def _ffn_mid(upa, upv, wa, wv, ba, bv, *, name, xchg=None):
    s, fh = upa.shape
    tb, tc = _tile(s, 512), _tile(fh, 256)
    blk, prev, _, taps, vec = _ffn_specs(s, fh, tb, tc)
    off = FFN_HALO - FFN_K + 1

    def body(a_ref, v_ref, ah_ref, vh_ref, wa_ref, wv_ref, ba_ref, bv_ref, o_ref, exta, extv):
        first = pl.program_id(1) == 0
        exta[pl.ds(FFN_HALO, tb), :] = a_ref[...]
        extv[pl.ds(FFN_HALO, tb), :] = v_ref[...]
        exta[pl.ds(0, FFN_HALO), :] = jnp.where(first, 0.0, ah_ref[...])
        extv[pl.ds(0, FFN_HALO), :] = jnp.where(first, 0.0, vh_ref[...])
        a2 = _conv_taps(exta, wa_ref, tb, FFN_K, off) + ba_ref[...]
        v2 = _conv_taps(extv, wv_ref, tb, FFN_K, off) + bv_ref[...]
        o_ref[...] = (_gelu(a2) * v2).astype(bf16)

    (f,), got = _call(
        body, (upa, upv, upa, upv, wa, wv, ba, bv), name=name, grid=(fh // tc, s // tb),
        in_specs=[blk, blk, prev, prev, taps, taps, vec, vec], out_specs=[blk], out_shape=[jax.ShapeDtypeStruct((s, fh), bf16)],
        scratch_shapes=[pltpu.VMEM((tb + FFN_HALO, tc), f32)] * 2, sem=("parallel", "arbitrary"), xchg=xchg)
    return f, got


def _ffn_mid_bwd(df, upa, upv, wa, wv, ba, bv, *, name, xchg=None):
    s, fh = upa.shape
    tb, tc = _tile(s, 512), _tile(fh, 256)
    nb = s // tb
    blk, prev, nxt, taps, vec = _ffn_specs(s, fh, tb, tc)
    off = FFN_HALO - FFN_K + 1
    te = tb + FFN_HALO

    def body(df_ref, dfn_ref, a_ref, v_ref, ah_ref, vh_ref, an_ref, vn_ref, wa_ref, wv_ref, ba_ref, bv_ref,
             da_ref, dv_ref, dwa_ref, dwv_ref, dba_ref, dbv_ref, exta, extv, dexta, dextv):
        i = pl.program_id(1)

        @pl.when(i == 0)
        def _():
            for r in (dwa_ref, dwv_ref, dba_ref, dbv_ref):
                r[...] = jnp.zeros_like(r)

        last = i == nb - 1
        for ext, c_ref, h_ref, n_ref in ((exta, a_ref, ah_ref, an_ref), (extv, v_ref, vh_ref, vn_ref)):
            ext[pl.ds(0, FFN_HALO), :] = jnp.where(i == 0, 0.0, h_ref[...])
            ext[pl.ds(FFN_HALO, tb), :] = c_ref[...]
            ext[pl.ds(FFN_HALO + tb, FFN_HALO), :] = jnp.where(last, 0.0, n_ref[...])
        a2 = _conv_taps(exta, wa_ref, te, FFN_K, off) + ba_ref[...]
        v2 = _conv_taps(extv, wv_ref, te, FFN_K, off) + bv_ref[...]
        dexta[pl.ds(0, tb), :] = df_ref[...]
        dexta[pl.ds(tb, FFN_HALO), :] = jnp.where(last, 0.0, dfn_ref[...])
        dfe = dexta[...]
        cdf = 0.5 * (1.0 + lax.erf(a2 * INV_SQRT2))
        dexta[...] = dfe * v2 * (cdf + a2 * jnp.exp(-0.5 * a2 * a2) * INV_SQRT_2PI)
        dextv[...] = dfe * (a2 * cdf)
        for ext, dext, w_ref, dw_ref, db_ref, o_ref in ((exta, dexta, wa_ref, dwa_ref, dba_ref, da_ref),
                                                        (extv, dextv, wv_ref, dwv_ref, dbv_ref, dv_ref)):
            dcur = dext[pl.ds(0, tb), :]
            dup = dext[pl.ds(FFN_K - 1, tb), :] * w_ref[pl.ds(0, 1), :]
            for k in range(1, FFN_K):
                dup = dup + dext[pl.ds(FFN_K - 1 - k, tb), :] * w_ref[pl.ds(k, 1), :]
            o_ref[...] = dup.astype(bf16)
            for k in range(FFN_K):
                dw_ref[pl.ds(k, 1), :] += _colsum(dcur * ext[pl.ds(off + k, tb), :])
            db_ref[...] += _colsum(dcur)

    act = jax.ShapeDtypeStruct((s, fh), bf16)
    wshape = jax.ShapeDtypeStruct((FFN_HALO, fh), f32)
    vshape = jax.ShapeDtypeStruct((1, fh), f32)
    res, got = _call(
        body, (df, df, upa, upv, upa, upv, upa, upv, wa, wv, ba, bv), name=name, grid=(fh // tc, nb),
        in_specs=[blk, nxt, blk, blk, prev, prev, nxt, nxt, taps, taps, vec, vec],
        out_specs=[blk, blk, taps, taps, vec, vec], out_shape=[act, act, wshape, wshape, vshape, vshape],
        scratch_shapes=[pltpu.VMEM((tb + 2 * FFN_HALO, tc), f32)] * 2 + [pltpu.VMEM((te, tc), f32)] * 2,
        sem=("parallel", "arbitrary"), xchg=xchg)
    return tuple(res) + (got,)


def _cols_from_shards(stacked):
    _, k, n = stacked.shape
    return stacked.transpose(1, 0, 2).reshape(k, NDEV * n)


def _pad_rows(w, rows):
    return jnp.pad(w, ((0, rows - w.shape[0]), (0, 0)))


def kernel(x, c, w_cond, b_cond, w_in, b_in, ssm_lambda_re, ssm_lambda_im, ssm_log_dt, ssm_b_re, ssm_b_im, ssm_c_re, ssm_c_im, ssm_d, ssm_glu_w_a, ssm_glu_w_b, cv_dw_w, cv_dw_b, cv_ln_g, cv_ln_b, cv_w_pw, w_out, ln1_g, ln1_b, ffn_w_up, ffn_dw_w, ffn_dw_b, ffn_w_down, ln2_g, ln2_b, loss_target, m_w_cond, m_b_cond, m_w_in, m_b_in, m_ssm_lambda_re, m_ssm_lambda_im, m_ssm_log_dt, m_ssm_b_re, m_ssm_b_im, m_ssm_c_re, m_ssm_c_im, m_ssm_d, m_ssm_glu_w_a, m_ssm_glu_w_b, m_cv_dw_w, m_cv_dw_b, m_cv_ln_g, m_cv_ln_b, m_cv_w_pw, m_w_out, m_ln1_g, m_ln1_b, m_ffn_w_up, m_ffn_dw_w, m_ffn_dw_b, m_ffn_w_down, m_ln2_g, m_ln2_b, v_w_cond, v_b_cond, v_w_in, v_b_in, v_ssm_lambda_re, v_ssm_lambda_im, v_ssm_log_dt, v_ssm_b_re, v_ssm_b_im, v_ssm_c_re, v_ssm_c_im, v_ssm_d, v_ssm_glu_w_a, v_ssm_glu_w_b, v_cv_dw_w, v_cv_dw_b, v_cv_ln_g, v_cv_ln_b, v_cv_w_pw, v_w_out, v_ln1_g, v_ln1_b, v_ffn_w_up, v_ffn_dw_w, v_ffn_dw_b, v_ffn_w_down, v_ln2_g, v_ln2_b):
    weights = dict(w_cond=w_cond, b_cond=b_cond, w_in=w_in, b_in=b_in, ssm_lambda_re=ssm_lambda_re, ssm_lambda_im=ssm_lambda_im, ssm_log_dt=ssm_log_dt, ssm_b_re=ssm_b_re, ssm_b_im=ssm_b_im, ssm_c_re=ssm_c_re, ssm_c_im=ssm_c_im, ssm_d=ssm_d, ssm_glu_w_a=ssm_glu_w_a, ssm_glu_w_b=ssm_glu_w_b, cv_dw_w=cv_dw_w, cv_dw_b=cv_dw_b, cv_ln_g=cv_ln_g, cv_ln_b=cv_ln_b, cv_w_pw=cv_w_pw, w_out=w_out, ln1_g=ln1_g, ln1_b=ln1_b, ffn_w_up=ffn_w_up, ffn_dw_w=ffn_dw_w, ffn_dw_b=ffn_dw_b, ffn_w_down=ffn_w_down, ln2_g=ln2_g, ln2_b=ln2_b)
    mom_m = dict(w_cond=m_w_cond, b_cond=m_b_cond, w_in=m_w_in, b_in=m_b_in, ssm_lambda_re=m_ssm_lambda_re, ssm_lambda_im=m_ssm_lambda_im, ssm_log_dt=m_ssm_log_dt, ssm_b_re=m_ssm_b_re, ssm_b_im=m_ssm_b_im, ssm_c_re=m_ssm_c_re, ssm_c_im=m_ssm_c_im, ssm_d=m_ssm_d, ssm_glu_w_a=m_ssm_glu_w_a, ssm_glu_w_b=m_ssm_glu_w_b, cv_dw_w=m_cv_dw_w, cv_dw_b=m_cv_dw_b, cv_ln_g=m_cv_ln_g, cv_ln_b=m_cv_ln_b, cv_w_pw=m_cv_w_pw, w_out=m_w_out, ln1_g=m_ln1_g, ln1_b=m_ln1_b, ffn_w_up=m_ffn_w_up, ffn_dw_w=m_ffn_dw_w, ffn_dw_b=m_ffn_dw_b, ffn_w_down=m_ffn_w_down, ln2_g=m_ln2_g, ln2_b=m_ln2_b)
    mom_v = dict(w_cond=v_w_cond, b_cond=v_b_cond, w_in=v_w_in, b_in=v_b_in, ssm_lambda_re=v_ssm_lambda_re, ssm_lambda_im=v_ssm_lambda_im, ssm_log_dt=v_ssm_log_dt, ssm_b_re=v_ssm_b_re, ssm_b_im=v_ssm_b_im, ssm_c_re=v_ssm_c_re, ssm_c_im=v_ssm_c_im, ssm_d=v_ssm_d, ssm_glu_w_a=v_ssm_glu_w_a, ssm_glu_w_b=v_ssm_glu_w_b, cv_dw_w=v_cv_dw_w, cv_dw_b=v_cv_dw_b, cv_ln_g=v_cv_ln_g, cv_ln_b=v_cv_ln_b, cv_w_pw=v_cv_w_pw, w_out=v_w_out, ln1_g=v_ln1_g, ln1_b=v_ln1_b, ffn_w_up=v_ffn_w_up, ffn_dw_w=v_ffn_dw_w, ffn_dw_b=v_ffn_dw_b, ffn_w_down=v_ffn_w_down, ln2_g=v_ln2_g, ln2_b=v_ln2_b)
    names = list(weights)

    s, d = x.shape[1], x.shape[2]
    sw = cw = d // 2
    fh = ffn_w_down.shape[1] * NDEV
    ng, nq = sw // SSM_GROUP, sw // QW
    gq = ng // nq
    alpha = 2.0 ** 0.25
    me = 4 * lax.axis_index("x") + 2 * lax.axis_index("y") + lax.axis_index("c")
    xs, tgt = x[0], loss_target[0]

    col_names = ["w_in", "ssm_glu_w_a", "ssm_glu_w_b", "cv_w_pw", "ffn_w_up"]
    row_names = ["w_out", "ffn_w_down"]
    big = col_names + row_names
    sent = lambda ns: [weights[n][0].astype(bf16) for n in ns]
    got_in, got_c, got_cv_taps, got_ffn_taps = _exchange(sent(["w_in"]) + [c, cv_dw_w[0, :, 0], ffn_dw_w[0, :, 0]],
                                                         scatter=False, name="gather_in")
    o1, o2, o3, o4 = sw, sw + cw, sw + 2 * cw, sw + 2 * cw + d
    in_bounds = ((0, o1), (o1, o2), (o2, o3), (o3, o4), (o4, o4 + d))
    w_u, w_cva, w_cvg, w_gs, w_gc = _unshard_cols(got_in, in_bounds, name="unshard_w_in")
    b_u, b_cva, b_cvg, b_gs, b_gc = (b_in[:, a:b] for a, b in in_bounds)
    c_all = got_c.reshape(NDEV, d)
    cv_taps = _cols_from_shards(got_cv_taps)
    ffn_taps = _cols_from_shards(got_ffn_taps)
    cv_w32 = _pad_rows(cv_taps, CONV_HALO)
    ffn_wa, ffn_wv = _pad_rows(ffn_taps[:, :fh], FFN_HALO), _pad_rows(ffn_taps[:, fh:], FFN_HALO)
    ffn_ba, ffn_bv = ffn_dw_b[:, :fh], ffn_dw_b[:, fh:]

    ncond = w_cond.shape[2]
    b_cond_mine = lax.dynamic_slice(b_cond, (0, me * ncond), (1, ncond))
    mod_cols = _cond_fwd(c_all, w_cond[0], b_cond_mine, name="cond_fwd")
    mod_all, = _exchange([mod_cols], scatter=False, name="gather_mod")
    mod_mine = lax.dynamic_slice(mod_all, (0, me, 0), (NDEV, 1, ncond)).reshape(1, 6 * d)
    sh1, sc1, g1, sh2, sc2, g2 = (mod_mine[:, k * d:(k + 1) * d] for k in range(6))

    lam_re, lam_im, log_dt = ssm_lambda_re[0], ssm_lambda_im[0], ssm_log_dt[0][:, None]
    lbr, lbi, cfr, cfi = _ssm_prep(lam_re, lam_im, log_dt, name="ssm_prep")
    rows_q = lambda a: a.reshape(nq, 1, QS)
    eye = jnp.eye(gq, dtype=f32)

    def b_mat(b):
        bt = b.reshape(nq, gq, SSM_STATE, SSM_GROUP).transpose(0, 1, 3, 2)
        return jnp.einsum("qgpn,gh->qgphn", bt, eye).reshape(nq, QW, QS)

    def c_mat(cc):
        ct = cc.reshape(nq, gq, SSM_GROUP, SSM_STATE)
        return jnp.einsum("qgpn,gh->qhngp", ct, eye).reshape(nq, QS, QW)

    def b_unmat(mt):
        return jnp.einsum("qgpgn->qgnp", mt.reshape(nq, gq, SSM_GROUP, gq, SSM_STATE)).reshape(ng, SSM_STATE, SSM_GROUP)

    def c_unmat(mt):
        return jnp.einsum("qgngp->qgpn", mt.reshape(nq, gq, SSM_STATE, gq, SSM_GROUP)).reshape(ng, SSM_GROUP, SSM_STATE)

    ssm_args = (rows_q(lbr), rows_q(lbi), b_mat(ssm_b_re[0]), b_mat(ssm_b_im[0]), c_mat(ssm_c_re[0]), c_mat(ssm_c_im[0]),
                rows_q(cfr), rows_q(cfi), ssm_d[0].reshape(1, sw))

    h1 = _lnmod(xs, sc1, sh1, name="ln_mod1")
    u = _mm([(h1, w_u)], b_u, name="in_u")
    cva = _mm([(h1, w_cva)], b_cva, name="in_cva")
    cvg = _mm([(h1, w_cvg)], b_cvg, name="in_cvg")
    gs = _mm([(h1, w_gs)], b_gs, out_dtype=bf16, name="in_gs")
    gc = _mm([(h1, w_gc)], b_gc, out_dtype=bf16, name="in_gc")
    v4, (got_a, got_b, got_pw, got_o) = _conv_fwd(
        cva, cvg, cv_w32, cv_dw_b, cv_ln_g, cv_ln_b, name="conv_fwd",
        xchg=(sent(["ssm_glu_w_a", "ssm_glu_w_b", "cv_w_pw", "w_out"]), False))
    h_p, yraw3, y, (got_up,) = _ssm_fwd(u, *ssm_args, name="ssm_fwd", xchg=(sent(["ffn_w_up"]), False))
    w_a, = _unshard_cols(got_a, ((0, d),), name="unshard_glu_a")
    w_b, = _unshard_cols(got_b, ((0, d),), name="unshard_glu_b")
    w_pw, = _unshard_cols(got_pw, ((0, d),), name="unshard_conv_pw")
    w_upa, w_upv = _unshard_cols(got_up, ((0, fh), (fh, 2 * fh)), name="unshard_ffn_up")
    w_o = got_o.reshape(d, d)
    ya = _mm([(y, w_a)], out_dtype=bf16, name="glu_a")
    yb = _mm([(y, w_b)], out_dtype=bf16, name="glu_b")
    ycv = _mm([(v4, w_pw)], out_dtype=bf16, name="conv_pw")
    merged = _glu_merge(ya, yb, ycv, gs, gc, name="merge")
    o = _mm([(merged, w_o)], name="out_proj")
    x1, h2 = _resid_ln_mod(xs, o, g1, ln1_g, ln1_b, sc2, sh2, alpha, name="resid_ln1")
    upa = _mm([(h2, w_upa)], name="ffn_up_a")
    upv = _mm([(h2, w_upv)], name="ffn_up_v")
    f, (got_dn,) = _ffn_mid(upa, upv, ffn_wa, ffn_wv, ffn_ba, ffn_bv, name="ffn_mid", xchg=(sent(["ffn_w_down"]), False))
    w_dn = got_dn.reshape(fh, d)
    y2 = _mm([(f, w_dn)], name="ffn_down")
    dr2, dy2, loss_part, d_ln2_g, d_ln2_b, d_g2 = _resid_ln_loss(x1, y2, g2, ln2_g, ln2_b, tgt, alpha, name="resid_ln2_loss")

    gw = {}
    df = _mm([(dy2, w_dn)], trans_w=True, name="d_ffn_down")
    gw["ffn_w_down"] = _mm_tn(f, dy2, out_dtype=bf16, name="g_ffn_down").reshape((NDEV,) + ffn_w_down[0].shape)
    received = {}
    dupa, dupv, d_ffn_wa, d_ffn_wv, d_ffn_ba, d_ffn_bv, (received["ffn_w_down"],) = _ffn_mid_bwd(
        df, upa, upv, ffn_wa, ffn_wv, ffn_ba, ffn_bv, name="ffn_mid_bwd", xchg=([gw["ffn_w_down"]], True))
    dh2 = _mm([(dupa, w_upa), (dupv, w_upv)], trans_w=True, name="d_ffn_up")
    gw["ffn_w_up"] = _shard_cols([_mm_tn(h2, dupa, name="g_ffn_up_a"), _mm_tn(h2, dupv, name="g_ffn_up_v")], out_dtype=bf16,
                                 name="shard_ffn_up")
    dr1, do, d_sc2, d_sh2, d_ln1_g, d_ln1_b, d_g1 = _mid_bwd(dh2, x1, dr2, xs, o, g1, sc2, ln1_g, alpha, name="mid_bwd")
    dmerged = _mm([(do, w_o)], trans_w=True, out_dtype=bf16, name="d_out_proj")
    gw["w_out"] = _mm_tn(merged, do, out_dtype=bf16, name="g_out_proj").reshape((NDEV,) + w_out[0].shape)
    dya, dyb, dycv, dgs, dgc, s_gs, s_gc = _glu_merge_bwd(dmerged, ya, yb, ycv, gs, gc, name="merge_bwd")
    dy = _mm([(dya, w_a), (dyb, w_b)], trans_w=True, name="d_glu")
    gw["ssm_glu_w_a"] = _shard_cols([_mm_tn(y, dya, name="g_glu_a")], out_dtype=bf16, name="shard_glu_a")
    gw["ssm_glu_w_b"] = _shard_cols([_mm_tn(y, dyb, name="g_glu_b")], out_dtype=bf16, name="shard_glu_b")
    dv4 = _mm([(dycv, w_pw)], trans_w=True, name="d_conv_pw")
    gw["cv_w_pw"] = _shard_cols([_mm_tn(v4, dycv, name="g_conv_pw")], out_dtype=bf16, name="shard_conv_pw")
    dv2, d_cv_ln_g, d_cv_ln_b = _conv_bwd_ln(dv4, cva, cvg, cv_w32, cv_dw_b, cv_ln_g, cv_ln_b, name="conv_bwd_ln")
    dcva, dcvg, d_cv_w32, d_cv_b, s_cva, s_cvg, (received["ffn_w_up"],) = _conv_bwd_taps(
        dv2, cva, cvg, cv_w32, name="conv_bwd_taps", xchg=([gw["ffn_w_up"]], True))
    late = ["w_out", "ssm_glu_w_a", "ssm_glu_w_b", "cv_w_pw"]
    (du, d_bre_m, d_bim_m, d_cre_m, d_cim_m, d_cfr, d_cfi, d_lbr, d_lbi, d_d, s_u, got_late) = _ssm_bwd(
        dy, yraw3, u, h_p, *ssm_args, name="ssm_bwd", xchg=([gw[n] for n in late], True))
    received.update(zip(late, got_late))
    gshape = lam_re.shape
    d_lam_re, d_lam_im, d_log_dt = _ssm_prep_bwd(
        lam_re, lam_im, log_dt, [a.reshape(gshape) for a in (d_lbr, d_lbi, d_cfr, d_cfi)], name="ssm_prep_bwd")
    small = {
        "b_in": jnp.concatenate([s_u, s_cva, s_cvg, s_gs, s_gc], axis=1),
        "ssm_lambda_re": d_lam_re, "ssm_lambda_im": d_lam_im, "ssm_log_dt": d_log_dt,
        "ssm_b_re": b_unmat(d_bre_m), "ssm_b_im": b_unmat(d_bim_m), "ssm_c_re": c_unmat(d_cre_m), "ssm_c_im": c_unmat(d_cim_m),
        "ssm_d": d_d, "cv_dw_w": d_cv_w32[:CONV_K], "cv_dw_b": d_cv_b, "cv_ln_g": d_cv_ln_g, "cv_ln_b": d_cv_ln_b,
        "ln1_g": d_ln1_g, "ln1_b": d_ln1_b,
        "ffn_dw_w": jnp.concatenate([d_ffn_wa[:FFN_K], d_ffn_wv[:FFN_K]], axis=1),
        "ffn_dw_b": jnp.concatenate([d_ffn_ba, d_ffn_bv], axis=1), "ln2_g": d_ln2_g, "ln2_b": d_ln2_b,
        "mod_g1": d_g1, "mod_sh2": d_sh2, "mod_sc2": d_sc2, "mod_g2": d_g2, "loss": loss_part,
    }
    small_names = list(small)
    small_shapes = [small[n].shape for n in small_names]
    gw["w_in"], (small_all,) = _mm_tn_sharded(h1, [du, dcva, dcvg, dgs, dgc], out_dtype=bf16, name="g_in",
                                              xchg=([_pack([small[n] for n in small_names])], False))
    dh1, (received["w_in"],) = _mm(
        [(du, w_u), (dcva, w_cva), (dcvg, w_cvg), (dgs, w_gs), (dgc, w_gc)], trans_w=True, name="d_in",
        xchg=([gw["w_in"]], True))
    grad_x, d_sc1, d_sh1, _ = _final_bwd(dh1, xs, dr1, sc1, alpha, name="final_bwd")

    grads, delta, new_m, new_v = {}, {}, {}, {}
    for n in big:
        ride = ([_pack([d_sh1, d_sc1])], False) if n == big[0] else None
        res = _sum_adamw(received[n], weights[n][0], mom_m[n][0], mom_v[n][0], name="adamw_" + n, xchg=ride)
        grads[n], delta[n], new_m[n], new_v[n] = res[:4]
        if ride is not None:
            last_all, = res[4]

    small_sum = dict(zip(small_names, _unpack(_sum_parts(small_all, name="sum_small").reshape(-1), small_shapes)))
    last_sum = _unpack(_sum_parts(last_all, name="sum_last").reshape(-1), [(1, d), (1, d)])
    per_dev = dict(zip(small_names, _unpack(small_all.reshape(NDEV, -1), small_shapes)))
    last_dev = _unpack(last_all.reshape(NDEV, -1), [(1, d), (1, d)])
    dmod_all = jnp.concatenate(last_dev + [per_dev[k] for k in ("mod_g1", "mod_sh2", "mod_sc2", "mod_g2")], axis=-1).reshape(NDEV, 6 * d)
    dmod_cols = lax.dynamic_slice(dmod_all.reshape(NDEV, NDEV, ncond), (0, me, 0), (NDEV, 1, ncond)).reshape(NDEV, ncond)
    grads["w_cond"] = _cond_bwd(c_all, dmod_cols, name="cond_bwd")
    loss = small_sum.pop("loss").reshape(())
    grads["b_cond"] = jnp.concatenate(last_sum + [small_sum.pop(k) for k in ("mod_g1", "mod_sh2", "mod_sc2", "mod_g2")], axis=1)
    for n, g in small_sum.items():
        grads[n] = g
    ntap = cv_dw_w.shape[3]
    grads["cv_dw_w"] = lax.dynamic_slice(grads["cv_dw_w"], (0, me * ntap), (CONV_K, ntap))
    nffn = ffn_dw_w.shape[3]
    grads["ffn_dw_w"] = lax.dynamic_slice(grads["ffn_dw_w"], (0, me * nffn), (FFN_K, nffn))
    grads = {n: grads[n].reshape(weights[n].shape) for n in names}

    delta["w_cond"], new_m["w_cond"], new_v["w_cond"] = _adamw(w_cond[0], grads["w_cond"][0], m_w_cond[0], v_w_cond[0],
                                                               name="adamw_w_cond")
    rest = [n for n in names if n not in ["w_cond"] + big]
    rest_shapes = [weights[n].shape for n in rest]
    packed = [_pack([t[n] for n in rest]) for t in (weights, grads, mom_m, mom_v)]
    for tgt_dict, res in zip((delta, new_m, new_v), _adamw(*packed, name="adamw_small")):
        for n, a in zip(rest, _unpack(res.reshape(-1), rest_shapes)):
            tgt_dict[n] = a
    shaped = lambda t: [t[n].reshape(weights[n].shape) for n in names]

    return (loss, grad_x[None], *shaped(grads), *shaped(delta), *shaped(new_m), *shaped(new_v))
```

```python
import functools
import math

import jax
import jax.numpy as jnp
from jax import lax
from jax.experimental import pallas as pl
from jax.experimental.pallas import tpu as pltpu

f32 = jnp.float32
bf16 = jnp.bfloat16

NDEV = 8
LANES = 128
SUBLANES = 8
SSM_GROUP = 16
SSM_STATE = 64
QW = 128
QS = 512
CONV_K = 31
CONV_HALO = 32
FFN_K = 3
FFN_HALO = 8
LN_EPS = 1e-5
ADAM_LR, ADAM_B1, ADAM_B2, ADAM_EPS, ADAM_WD, ADAM_STEP = 0.001, 0.9, 0.999, 1e-08, 0.01, 10
VMEM_LIMIT = 56 * 1024 * 1024
W_TILE_BYTES = 6 * 1024 * 1024
SUM_ROWS = 512
EW_BLOCK_BYTES = 2 * 1024 * 1024
INV_SQRT2 = 1.0 / math.sqrt(2.0)
INV_SQRT_2PI = 1.0 / math.sqrt(2.0 * math.pi)
MESH = pl.DeviceIdType.MESH


def _tile(n, want):
    t = min(n, want)
    while n % t:
        t //= 2
    return t


def _col_tile(n, rows, bytes_per):
    best = LANES if n % LANES == 0 else n
    for t in range(LANES, n + 1, LANES):
        if n % t == 0 and rows * t * bytes_per <= W_TILE_BYTES:
            best = t
    return best


def _params(*sem):
    return pltpu.CompilerParams(dimension_semantics=sem, vmem_limit_bytes=VMEM_LIMIT)


def _row(i):
    return (0, 0)


def _full(shape):
    nd = len(shape)
    return pl.BlockSpec(shape, lambda *a: (0,) * nd)


def _ln(x):
    mu = jnp.mean(x, axis=-1, keepdims=True)
    xc = x - mu
    var = jnp.mean(xc * xc, axis=-1, keepdims=True)
    rstd = lax.rsqrt(var + LN_EPS)
    return xc * rstd, rstd


def _ln_bwd(dxhat, xhat, rstd):
    return rstd * (dxhat - jnp.mean(dxhat, axis=-1, keepdims=True) - xhat * jnp.mean(dxhat * xhat, axis=-1, keepdims=True))


def _sig(x):
    return 1.0 / (1.0 + jnp.exp(-x))


def _gelu(x):
    return 0.5 * x * (1.0 + lax.erf(x * INV_SQRT2))


def _gelu_grad(x):
    return 0.5 * (1.0 + lax.erf(x * INV_SQRT2)) + x * jnp.exp(-0.5 * x * x) * INV_SQRT_2PI


def _colsum(x):
    return jnp.sum(x, axis=0, keepdims=True)


def _mm(pairs, bias=None, *, trans_w=False, out_dtype=f32, name, xchg=None):
    n_p = len(pairs)
    m = pairs[0][0].shape[0]
    n = pairs[0][1].shape[0 if trans_w else 1]
    ktot = sum(x.shape[1] for x, _ in pairs)
    tm = _tile(m, 512)
    tn = _col_tile(n, ktot, 2)
    dn = (((1,), (1,)), ((), ())) if trans_w else (((1,), (0,)), ((), ()))

    def body(*refs):
        o_ref = refs[-1]
        acc = None
        for xr, wr in zip(refs[:n_p], refs[n_p:2 * n_p]):
            r = lax.dot_general(xr[...].astype(bf16), wr[...].astype(bf16), dn, preferred_element_type=f32)
            acc = r if acc is None else acc + r
        if bias is not None:
            acc = acc + refs[2 * n_p][...]
        o_ref[...] = acc.astype(out_dtype)

    in_specs = [pl.BlockSpec((tm, x.shape[1]), lambda j, i: (i, 0)) for x, _ in pairs]
    if trans_w:
        in_specs += [pl.BlockSpec((tn, w.shape[1]), lambda j, i: (j, 0)) for _, w in pairs]
    else:
        in_specs += [pl.BlockSpec((w.shape[0], tn), lambda j, i: (0, j)) for _, w in pairs]
    args = [x for x, _ in pairs] + [w for _, w in pairs]
    if bias is not None:
        in_specs.append(pl.BlockSpec((1, tn), lambda j, i: (0, j)))
        args.append(bias)
    (out,), got = _call(
        body, args, name=name, grid=(n // tn, m // tm), in_specs=in_specs,
        out_specs=[pl.BlockSpec((tm, tn), lambda j, i: (i, j))], out_shape=[jax.ShapeDtypeStruct((m, n), out_dtype)],
        sem=("parallel", "arbitrary"), xchg=xchg)
    return out if xchg is None else (out, got)


def _mm_tn(x, dy, *, out_dtype=f32, name):
    m, k = x.shape
    n = dy.shape[1]
    tm = _tile(m, 512)
    tn = _col_tile(n, k, 4)
    steps = m // tm

    def body(x_ref, dy_ref, o_ref, *scratch):
        acc = scratch[0] if scratch else o_ref

        @pl.when(pl.program_id(1) == 0)
        def _():
            acc[...] = jnp.zeros_like(acc)

        acc[...] += lax.dot_general(x_ref[...].astype(bf16), dy_ref[...].astype(bf16), (((0,), (0,)), ((), ())),
                                    preferred_element_type=f32)
        if scratch:
            @pl.when(pl.program_id(1) == steps - 1)
            def _():
                o_ref[...] = acc[...].astype(out_dtype)

    return pl.pallas_call(
        body, name=name, grid=(n // tn, steps),
        in_specs=[pl.BlockSpec((tm, k), lambda j, i: (i, 0)), pl.BlockSpec((tm, tn), lambda j, i: (i, j))],
        out_specs=pl.BlockSpec((k, tn), lambda j, i: (0, j)),
        out_shape=jax.ShapeDtypeStruct((k, n), out_dtype),
        scratch_shapes=[] if out_dtype == f32 else [pltpu.VMEM((k, tn), f32)],
        compiler_params=_params("parallel", "arbitrary"),
    )(x, dy)


def _mm_tn_sharded(x, dys, *, out_dtype, name, xchg=None):
    m, k = x.shape
    widths = [dy.shape[1] for dy in dys]
    n = sum(widths) // NDEV
    tm = _tile(m, 512)
    steps = m // tm
    n_d = len(dys)

    def body(x_ref, *refs):
        dy_refs, o_ref, acc = refs[:n_d], refs[n_d], refs[n_d + 1]
        i = pl.program_id(0)

        @pl.when(i == 0)
        def _():
            acc[...] = jnp.zeros_like(acc)

        xb = x_ref[...].astype(bf16)
        off = 0
        for dy_ref, w in zip(dy_refs, widths):
            acc[:, off:off + w] += lax.dot_general(xb, dy_ref[...].astype(bf16), (((0,), (0,)), ((), ())), preferred_element_type=f32)
            off += w

        @pl.when(i == steps - 1)
        def _():
            for j in range(NDEV):
                o_ref[j] = acc[:, n * j:n * (j + 1)].astype(out_dtype)

    (out,), got = _call(
        body, (x, *dys), name=name, grid=(steps,),
        in_specs=[pl.BlockSpec((tm, k), lambda i: (i, 0))] + [pl.BlockSpec((tm, w), lambda i: (i, 0)) for w in widths],
        out_specs=[pl.BlockSpec((NDEV, k, n), lambda i: (0, 0, 0))], out_shape=[jax.ShapeDtypeStruct((NDEV, k, n), out_dtype)],
        scratch_shapes=[pltpu.VMEM((k, sum(widths)), f32)], sem=("arbitrary",), xchg=xchg)
    return out if xchg is None else (out, got)


def _exchange(arrs, *, scatter, name):
    n = len(arrs)

    def body(*refs):
        _exchange_copies(refs[:n], refs[n:2 * n], refs[2 * n:], scatter, True, True)

    return pl.pallas_call(
        body, name=name, in_specs=[HBM_SPEC] * n, out_specs=[HBM_SPEC] * n, out_shape=_exchange_out_shape(arrs, scatter),
        scratch_shapes=_exchange_sems(n),
    )(*arrs)


HBM_SPEC = pl.BlockSpec(memory_space=pltpu.HBM)


def _flags(scatter, n):
    return list(scatter) if isinstance(scatter, (list, tuple)) else [scatter] * n


def _exchange_out_shape(arrs, scatter):
    return [jax.ShapeDtypeStruct(a.shape if sc else (NDEV,) + a.shape, a.dtype) for a, sc in zip(arrs, _flags(scatter, len(arrs)))]


def _exchange_sems(n):
    return [pltpu.SemaphoreType.DMA(((NDEV - 1) * n,)), pltpu.SemaphoreType.DMA(((NDEV - 1) * n,)), pltpu.SemaphoreType.DMA((n,))]


def _exchange_copies(x_refs, o_refs, sems, scatter, start, wait):
    n = len(x_refs)
    flags = _flags(scatter, n)
    send_sems, recv_sems, local_sems = sems
    ix, iy, ic = lax.axis_index("x"), lax.axis_index("y"), lax.axis_index("c")
    me = 4 * ix + 2 * iy + ic
    local = [pltpu.make_async_copy(x.at[me] if sc else x, o.at[me], local_sems.at[a])
             for a, (x, o, sc) in enumerate(zip(x_refs, o_refs, flags))]
    sends, recvs = [], []
    for k in range(1, NDEV):
        px, py, pc = (1 - ix if k & 4 else ix, 1 - iy if k & 2 else iy, 1 - ic if k & 1 else ic)
        them = 4 * px + 2 * py + pc
        for a, (x, o, sc) in enumerate(zip(x_refs, o_refs, flags)):
            sem = (k - 1) * n + a
            sends.append(pltpu.make_async_remote_copy(
                src_ref=x.at[them] if sc else x, dst_ref=o.at[me], send_sem=send_sems.at[sem], recv_sem=recv_sems.at[sem],
                device_id=(px, py, pc), device_id_type=MESH))
            recvs.append(pltpu.make_async_remote_copy(
                src_ref=x.at[me] if sc else x, dst_ref=o.at[them], send_sem=send_sems.at[sem], recv_sem=recv_sems.at[sem],
                device_id=(px, py, pc), device_id_type=MESH))
    if start:
        for cp in local + sends:
            cp.start()
    if wait:
        for cp in recvs:
            cp.wait_recv()
        for cp in sends:
            cp.wait_send()
        for cp in local:
            cp.wait()


def _call(body, args, *, name, grid, in_specs, out_specs, out_shape, scratch_shapes=(), sem, xchg=None):
    if xchg is None:
        return pl.pallas_call(body, name=name, grid=grid, in_specs=in_specs, out_specs=out_specs, out_shape=out_shape,
                              scratch_shapes=list(scratch_shapes), compiler_params=_params(*sem))(*args), None
    arrs, scatter = xchg
    n, ni, no, ns = len(arrs), len(in_specs), len(out_specs), len(scratch_shapes)

    def wrapped(*refs):
        ins, x_refs = refs[:ni], refs[ni:ni + n]
        outs, o_refs = refs[ni + n:ni + n + no], refs[ni + n + no:ni + 2 * n + no]
        scratch, sems = refs[ni + 2 * n + no:ni + 2 * n + no + ns], refs[ni + 2 * n + no + ns:]
        ids = [pl.program_id(a) for a in range(len(grid))]
        first = functools.reduce(jnp.logical_and, [p == 0 for p in ids])
        last = functools.reduce(jnp.logical_and, [p == g - 1 for p, g in zip(ids, grid)])

        @pl.when(first)
        def _():
            _exchange_copies(x_refs, o_refs, sems, scatter, True, False)

        body(*ins, *outs, *scratch)

        @pl.when(last)
        def _():
            _exchange_copies(x_refs, o_refs, sems, scatter, False, True)

    res = pl.pallas_call(
        wrapped, name=name, grid=grid, in_specs=list(in_specs) + [HBM_SPEC] * n, out_specs=list(out_specs) + [HBM_SPEC] * n,
        out_shape=list(out_shape) + _exchange_out_shape(arrs, scatter),
        scratch_shapes=list(scratch_shapes) + _exchange_sems(n),
        compiler_params=_params(*("arbitrary",) * len(grid)))(*args, *arrs)
    return res[:no], res[no:]


def _sum_parts(parts, *, name):
    r = parts.shape[1]

    def body(p_ref, o_ref):
        acc = p_ref[0]
        for j in range(1, NDEV):
            acc = acc + p_ref[j]
        o_ref[...] = acc

    return pl.pallas_call(body, name=name, out_shape=jax.ShapeDtypeStruct((r, LANES), f32), compiler_params=_params())(parts)


def _col_pieces(n, bounds):
    out = []
    for p, (a, b) in enumerate(bounds):
        for j in range(NDEV):
            lo, hi = max(a, n * j), min(b, n * (j + 1))
            if lo < hi:
                out.append((p, j, lo - a, lo - n * j, hi - lo))
    return out


def _unshard_cols(stacked, bounds, *, name):
    _, k, n = stacked.shape
    tk = _tile(k, 256)
    plan = _col_pieces(n, bounds)

    def body(x_ref, *o_refs):
        for p, j, po, so, w in plan:
            o_refs[p][:, po:po + w] = x_ref[j, :, so:so + w]

    return pl.pallas_call(
        body, name=name, grid=(k // tk,), in_specs=[pl.BlockSpec((NDEV, tk, n), lambda i: (0, i, 0))],
        out_specs=[pl.BlockSpec((tk, b - a), lambda i: (i, 0)) for a, b in bounds],
        out_shape=[jax.ShapeDtypeStruct((k, b - a), stacked.dtype) for a, b in bounds],
        compiler_params=_params("parallel"))(stacked)


def _shard_cols(pieces, *, out_dtype, name):
    k = pieces[0].shape[0]
    bounds, off = [], 0
    for p in pieces:
        bounds.append((off, off + p.shape[1]))
        off += p.shape[1]
    n = off // NDEV
    tk = _tile(k, 256)
    plan = _col_pieces(n, bounds)

    def body(*refs):
        o_ref = refs[-1]
        for p, j, po, so, w in plan:
            o_ref[j, :, so:so + w] = refs[p][:, po:po + w].astype(out_dtype)

    return pl.pallas_call(
        body, name=name, grid=(k // tk,), in_specs=[pl.BlockSpec((tk, b - a), lambda i: (i, 0)) for a, b in bounds],
        out_specs=pl.BlockSpec((NDEV, tk, n), lambda i: (0, i, 0)),
        out_shape=jax.ShapeDtypeStruct((NDEV, k, n), out_dtype),
        compiler_params=_params("parallel"))(*pieces)


def _pack(arrs):
    flat = jnp.concatenate([a.reshape(-1) for a in arrs])
    pad = (-flat.shape[0]) % (SUBLANES * LANES)
    return jnp.pad(flat, (0, pad)).reshape(-1, LANES)


def _unpack(flat, shapes):
    out, off = [], 0
    for s in shapes:
        n = math.prod(s)
        out.append(flat[..., off:off + n].reshape(flat.shape[:-1] + tuple(s)))
        off += n
    return out


def _adamw_math(w, gg, m, v):
    nm = ADAM_B1 * m + (1.0 - ADAM_B1) * gg
    nv = ADAM_B2 * v + (1.0 - ADAM_B2) * (gg * gg)
    m_hat = nm / (1.0 - ADAM_B1 ** ADAM_STEP)
    v_hat = nv / (1.0 - ADAM_B2 ** ADAM_STEP)
    return -ADAM_LR * (m_hat / (jnp.sqrt(v_hat) + ADAM_EPS) + ADAM_WD * w), nm, nv


def _row_block(r, c, copies):
    tr = r
    while copies * tr * c * 4 > EW_BLOCK_BYTES and tr % (4 * SUBLANES) == 0:
        tr //= 2
    return tr


def _adamw(w, g, m, v, *, name):
    r, c = w.shape
    tr = _row_block(r, c, 1)

    def body(w_ref, g_ref, m_ref, v_ref, d_ref, nm_ref, nv_ref):
        d_ref[...], nm_ref[...], nv_ref[...] = _adamw_math(w_ref[...], g_ref[...], m_ref[...], v_ref[...])

    spec = pl.BlockSpec((tr, c), lambda i: (i, 0))
    shp = jax.ShapeDtypeStruct((r, c), f32)
    return pl.pallas_call(
        body, name=name, grid=(r // tr,), in_specs=[spec] * 4, out_specs=[spec] * 3, out_shape=[shp] * 3,
        compiler_params=_params("parallel"),
    )(w, g, m, v)


def _sum_adamw(parts, w, m, v, *, name, xchg=None):
    r, c = w.shape
    tr = _row_block(r, c, NDEV)

    def body(p_ref, w_ref, m_ref, v_ref, g_ref, d_ref, nm_ref, nv_ref):
        gg = p_ref[0].astype(f32)
        for j in range(1, NDEV):
            gg = gg + p_ref[j].astype(f32)
        g_ref[...] = gg
        d_ref[...], nm_ref[...], nv_ref[...] = _adamw_math(w_ref[...], gg, m_ref[...], v_ref[...])

    spec = pl.BlockSpec((tr, c), lambda i: (i, 0))
    shp = jax.ShapeDtypeStruct((r, c), f32)
    res, got = _call(
        body, (parts, w, m, v), name=name, grid=(r // tr,),
        in_specs=[pl.BlockSpec((NDEV, tr, c), lambda i: (0, i, 0))] + [spec] * 3,
        out_specs=[spec] * 4, out_shape=[shp] * 4, sem=("parallel",), xchg=xchg)
    return tuple(res) if xchg is None else tuple(res) + (got,)


def _cond_fwd(c_all, w, b, *, name):
    nb, n = c_all.shape[0], w.shape[1]

    def body(c_ref, w_ref, b_ref, o_ref):
        cc = c_ref[...]
        o_ref[...] = jnp.dot(cc * _sig(cc), w_ref[...], preferred_element_type=f32,
                             precision=lax.Precision.HIGHEST) + b_ref[...]

    return pl.pallas_call(body, name=name, out_shape=jax.ShapeDtypeStruct((nb, n), f32),
                          compiler_params=_params())(c_all, w, b)


def _cond_bwd(c_all, dmod, *, name):
    d, n = c_all.shape[1], dmod.shape[1]

    def body(c_ref, g_ref, o_ref):
        cc = c_ref[...]
        o_ref[...] = lax.dot_general(cc * _sig(cc), g_ref[...], (((0,), (0,)), ((), ())), preferred_element_type=f32,
                                     precision=lax.Precision.HIGHEST)

    return pl.pallas_call(body, name=name, out_shape=jax.ShapeDtypeStruct((d, n), f32),
                          compiler_params=_params())(c_all, dmod)


def _ssm_disc(lam_re, lam_im, log_dt):
    lr = jnp.minimum(lam_re, -1e-4)
    li = lam_im
    dt = jnp.exp(log_dt)
    mag = jnp.exp(lr * dt)
    ang = li * dt
    lbr, lbi = mag * jnp.cos(ang), mag * jnp.sin(ang)
    num_r, num_i = lbr - 1.0, lbi
    den = lr * lr + li * li
    return lbr, lbi, (num_r * lr + num_i * li) / den, (num_i * lr - num_r * li) / den


def _ssm_prep(lam_re, lam_im, log_dt, *, name):
    def body(a, b, c, o1, o2, o3, o4):
        o1[...], o2[...], o3[...], o4[...] = _ssm_disc(a[...], b[...], c[...])

    shp = jax.ShapeDtypeStruct(lam_re.shape, f32)
    return pl.pallas_call(body, name=name, out_shape=[shp] * 4, compiler_params=_params())(lam_re, lam_im, log_dt)


def _ssm_prep_bwd(lam_re, lam_im, log_dt, cts, *, name):
    def body(a, b, c, g1, g2, g3, g4, o1, o2, o3):
        _, vjp = jax.vjp(_ssm_disc, a[...], b[...], c[...])
        o1[...], o2[...], o3[...] = vjp((g1[...], g2[...], g3[...], g4[...]))

    shp = jax.ShapeDtypeStruct(lam_re.shape, f32)
    return pl.pallas_call(body, name=name, out_shape=[shp, shp, jax.ShapeDtypeStruct(log_dt.shape, f32)],
                          compiler_params=_params())(lam_re, lam_im, log_dt, *cts)


def _step_major(tb, nt, dtype):
    r = lax.broadcasted_iota(jnp.int32, (tb, tb), 0)
    k = lax.broadcasted_iota(jnp.int32, (tb, tb), 1)
    return (k == (r % SUBLANES) * nt + r // SUBLANES).astype(dtype)


def _chunk_major(tb, nt, dtype):
    k = lax.broadcasted_iota(jnp.int32, (tb, tb), 0)
    r = lax.broadcasted_iota(jnp.int32, (tb, tb), 1)
    return (k == (r % SUBLANES) * nt + r // SUBLANES).astype(dtype)


def _permute_f32(pmat, x):
    return jnp.dot(pmat, x, preferred_element_type=f32, precision=lax.Precision.HIGHEST)


def _permute_bf16(pmat, x):
    return jnp.dot(pmat, x, preferred_element_type=f32).astype(bf16)


def _chain_carries(loc_r, loc_i, pr, pi_, forward):
    row = lax.broadcasted_iota(jnp.int32, loc_r.shape, 0)
    shift = 1 if forward else SUBLANES - 1
    order = range(1, SUBLANES) if forward else range(SUBLANES - 2, -1, -1)
    er, ei = loc_r, loc_i
    for k in order:
        sr, si = pltpu.roll(er, shift, 0), pltpu.roll(ei, shift, 0)
        er = jnp.where(row == k, loc_r + pr * sr - pi_ * si, er)
        ei = jnp.where(row == k, loc_i + pr * si + pi_ * sr, ei)
    edge = 0 if forward else SUBLANES - 1
    return (jnp.where(row == edge, 0.0, pltpu.roll(er, shift, 0)), jnp.where(row == edge, 0.0, pltpu.roll(ei, shift, 0)))


def _chunk_power(ar, ai, chunk_len):
    pr, pi_ = ar, ai
    for _ in range(int(math.log2(chunk_len))):
        pr, pi_ = pr * pr - pi_ * pi_, 2.0 * pr * pi_
    return pr, pi_


def _ssm_mats(bre_ref, bim_ref, cre_ref, cim_ref, cfr_ref, cfi_ref, bbar_s, cmat_s, nq):
    for q in range(nq):
        cr, ci, br, bi = cfr_ref[q], cfi_ref[q], bre_ref[q], bim_ref[q]
        bbar_s[q, :, 0:QS] = (cr * br - ci * bi).astype(bf16)
        bbar_s[q, :, QS:2 * QS] = (cr * bi + ci * br).astype(bf16)
        cmat_s[q, 0:QS, :] = cre_ref[q].astype(bf16)
        cmat_s[q, QS:2 * QS, :] = (-cim_ref[q]).astype(bf16)


def _ssm_fwd(u, ar, ai, bre, bim, cre, cim, cfr, cfi, dvec, *, name, xchg=None):
    s, sw = u.shape
    nq = sw // QW
    st = nq * 2 * QS
    tb = _tile(s, 256)
    nb, nt, chunk_len = s // tb, tb // SUBLANES, s // SUBLANES
    assert chunk_len & (chunk_len - 1) == 0 and nt % 16 == 0

    def body(u_ref, ar_ref, ai_ref, bre_ref, bim_ref, cre_ref, cim_ref, cfr_ref, cfi_ref, d_ref,
             h_out, yraw_out, y_out, buf, hc, bbar_s, cmat_s):
        ph, i = pl.program_id(0), pl.program_id(1)

        @pl.when(i == 0)
        def _():
            _ssm_mats(bre_ref, bim_ref, cre_ref, cim_ref, cfr_ref, cfi_ref, bbar_s, cmat_s, nq)

        @pl.when((ph == 0) & (i == 0))
        def _():
            hc[...] = jnp.zeros_like(hc)

        @pl.when((ph == 1) & (i == 0))
        def _():
            for q in range(nq):
                o = q * 2 * QS
                pr, pi_ = _chunk_power(ar_ref[q], ai_ref[q], chunk_len)
                sr, si = _chain_carries(hc[:, o:o + QS], hc[:, o + QS:o + 2 * QS], pr, pi_, True)
                hc[:, o:o + QS] = sr
                hc[:, o + QS:o + 2 * QS] = si

        uu = u_ref[...].reshape(tb, sw)
        up = _permute_bf16(_step_major(tb, nt, bf16), uu.astype(bf16))
        for q in range(nq):
            o = q * 2 * QS
            buf[:, o:o + 2 * QS] = jnp.dot(up[:, q * QW:(q + 1) * QW], bbar_s[q], preferred_element_type=f32)

        for q in range(nq):
            o = q * 2 * QS
            a_r = jnp.broadcast_to(ar_ref[q], (SUBLANES, QS))
            a_i = jnp.broadcast_to(ai_ref[q], (SUBLANES, QS))

            def step(t, carry, o=o, a_r=a_r, a_i=a_i):
                hr, hi = carry
                r0 = pl.multiple_of(t * SUBLANES, SUBLANES)
                nr = a_r * hr - a_i * hi + buf[pl.ds(r0, SUBLANES), o:o + QS]
                ni = a_r * hi + a_i * hr + buf[pl.ds(r0, SUBLANES), o + QS:o + 2 * QS]
                buf[pl.ds(r0, SUBLANES), o:o + QS] = nr
                buf[pl.ds(r0, SUBLANES), o + QS:o + 2 * QS] = ni
                return nr, ni

            hr, hi = lax.fori_loop(0, nt, step, (hc[:, o:o + QS], hc[:, o + QS:o + 2 * QS]))
            hc[:, o:o + QS] = hr
            hc[:, o + QS:o + 2 * QS] = hi

        @pl.when(ph == 1)
        def _():
            back = _chunk_major(tb, nt, f32)
            for q in range(nq):
                o = q * 2 * QS
                cs = slice(q * QW, (q + 1) * QW)
                hq = buf[:, o:o + 2 * QS].astype(bf16)
                h_out[:, o:o + 2 * QS] = hq
                yq = _permute_f32(back, jnp.dot(hq, cmat_s[q], preferred_element_type=f32)) + d_ref[:, cs] * uu[:, cs]
                yraw_out[:, :, cs] = yq.reshape(SUBLANES, nt, QW)
                y_out[:, :, cs] = _gelu(yq).astype(bf16).reshape(SUBLANES, nt, QW)

    blk = lambda ph, i: (0, i, 0)
    oblk = lambda ph, i: (0, i * ph, 0)
    act = lambda dt: jax.ShapeDtypeStruct((SUBLANES, chunk_len, sw), dt)
    (h_p, yraw3, y3), got = _call(
        body, (u.reshape(SUBLANES, chunk_len, sw), ar, ai, bre, bim, cre, cim, cfr, cfi, dvec), name=name, grid=(2, nb),
        in_specs=[pl.BlockSpec((SUBLANES, nt, sw), blk), _full(ar.shape), _full(ai.shape), _full(bre.shape), _full(bim.shape),
                  _full(cre.shape), _full(cim.shape), _full(cfr.shape), _full(cfi.shape), _full(dvec.shape)],
        out_specs=[pl.BlockSpec((tb, st), lambda ph, i: (i * ph, 0)), pl.BlockSpec((SUBLANES, nt, sw), oblk),
                   pl.BlockSpec((SUBLANES, nt, sw), oblk)],
        out_shape=[jax.ShapeDtypeStruct((s, st), bf16), act(f32), act(bf16)],
        scratch_shapes=[pltpu.VMEM((tb, st), f32), pltpu.VMEM((SUBLANES, st), f32),
                        pltpu.VMEM((nq, QW, 2 * QS), bf16), pltpu.VMEM((nq, 2 * QS, QW), bf16)],
        sem=("arbitrary", "arbitrary"), xchg=xchg)
    return h_p, yraw3, y3.reshape(s, sw), got


def _ssm_bwd(dy, yraw3, u, h_p, ar, ai, bre, bim, cre, cim, cfr, cfi, dvec, *, name, xchg=None):
    s, sw = u.shape
    nq = sw // QW
    st = nq * 2 * QS
    tb = _tile(s, 256)
    nb, nt, chunk_len = s // tb, tb // SUBLANES, s // SUBLANES

    def body(dy_ref, yraw_ref, u_ref, h_ref, ar_ref, ai_ref, bre_ref, bim_ref, cre_ref, cim_ref, cfr_ref, cfi_ref, d_ref,
             du_out, dbre_out, dbim_out, dcre_out, dcim_out, dcfr_out, dcfi_out, dlbr_out, dlbi_out, dd_out, dbu_out,
             buf, hf, rc, acc, dbbar, dcmat, bbar_s, cmat_s):
        ph, i = pl.program_id(0), pl.program_id(1)

        @pl.when(i == 0)
        def _():
            _ssm_mats(bre_ref, bim_ref, cre_ref, cim_ref, cfr_ref, cfi_ref, bbar_s, cmat_s, nq)

        @pl.when((ph == 0) & (i == 0))
        def _():
            rc[...] = jnp.zeros_like(rc)

        @pl.when((ph == 1) & (i == 0))
        def _():
            for q in range(nq):
                o = q * 2 * QS
                pr, pi_ = _chunk_power(ar_ref[q], ai_ref[q], chunk_len)
                sr, si = _chain_carries(rc[:, o:o + QS], rc[:, o + QS:o + 2 * QS], pr, -pi_, False)
                rc[:, o:o + QS] = sr
                rc[:, o + QS:o + 2 * QS] = si
            acc[...] = jnp.zeros_like(acc)
            dbbar[...] = jnp.zeros_like(dbbar)
            dcmat[...] = jnp.zeros_like(dcmat)
            dd_out[...] = jnp.zeros_like(dd_out)
            dbu_out[...] = jnp.zeros_like(dbu_out)

        dyraw = (dy_ref[...] * _gelu_grad(yraw_ref[...])).reshape(tb, sw)
        fwd_perm = _step_major(tb, nt, bf16)
        dyp = _permute_bf16(fwd_perm, dyraw.astype(bf16))
        for q in range(nq):
            o = q * 2 * QS
            buf[:, o:o + 2 * QS] = lax.dot_general(dyp[:, q * QW:(q + 1) * QW], cmat_s[q], (((1,), (1,)), ((), ())),
                                                   preferred_element_type=f32)

        def recur(with_grad):
            for q in range(nq):
                o = q * 2 * QS
                a_r = jnp.broadcast_to(ar_ref[q], (SUBLANES, QS))
                a_i = jnp.broadcast_to(ai_ref[q], (SUBLANES, QS))

                def step(j, carry, o=o, a_r=a_r, a_i=a_i):
                    r0 = pl.multiple_of((nt - 1 - j) * SUBLANES, SUBLANES)
                    if with_grad:
                        rr, ri, gr, gi = carry
                        hr = hf[pl.ds(r0, SUBLANES), o:o + QS]
                        hi = hf[pl.ds(r0, SUBLANES), o + QS:o + 2 * QS]
                        gr = gr + hr * rr + hi * ri
                        gi = gi + hr * ri - hi * rr
                    else:
                        rr, ri = carry
                    nr = buf[pl.ds(r0, SUBLANES), o:o + QS] + a_r * rr + a_i * ri
                    ni = buf[pl.ds(r0, SUBLANES), o + QS:o + 2 * QS] + a_r * ri - a_i * rr
                    buf[pl.ds(r0, SUBLANES), o:o + QS] = nr
                    buf[pl.ds(r0, SUBLANES), o + QS:o + 2 * QS] = ni
                    return (nr, ni, gr, gi) if with_grad else (nr, ni)

                init = (rc[:, o:o + QS], rc[:, o + QS:o + 2 * QS])
                if with_grad:
                    init = init + (acc[:, o:o + QS], acc[:, o + QS:o + 2 * QS])
                res = lax.fori_loop(0, nt, step, init)
                rc[:, o:o + QS] = res[0]
                rc[:, o + QS:o + 2 * QS] = res[1]
                if with_grad:
                    acc[:, o:o + QS] = res[2]
                    acc[:, o + QS:o + 2 * QS] = res[3]

        @pl.when(ph == 0)
        def _():
            recur(False)

        @pl.when(ph == 1)
        def _():
            hf[...] = h_ref[...].astype(f32)
            recur(True)
            uu = u_ref[...].reshape(tb, sw)
            up = _permute_bf16(fwd_perm, uu.astype(bf16))
            back = _chunk_major(tb, nt, f32)
            dd_out[...] += _colsum(dyraw * uu)
            for q in range(nq):
                o = q * 2 * QS
                cs = slice(q * QW, (q + 1) * QW)
                lam = buf[:, o:o + 2 * QS].astype(bf16)
                duq = _permute_f32(back, lax.dot_general(lam, bbar_s[q], (((1,), (1,)), ((), ())), preferred_element_type=f32)) \
                    + d_ref[:, cs] * dyraw[:, cs]
                du_out[:, :, cs] = duq.astype(bf16).reshape(SUBLANES, nt, QW)
                dbu_out[:, cs] += _colsum(duq)
                dbbar[q] += lax.dot_general(up[:, cs], lam, (((0,), (0,)), ((), ())), preferred_element_type=f32)
                dcmat[q] += lax.dot_general(h_ref[:, o:o + 2 * QS], dyp[:, cs], (((0,), (0,)), ((), ())),
                                            preferred_element_type=f32)

        @pl.when((ph == 1) & (i == nb - 1))
        def _():
            for q in range(nq):
                o = q * 2 * QS
                cr, ci, br, bi = cfr_ref[q], cfi_ref[q], bre_ref[q], bim_ref[q]
                gr, gi = dbbar[q, :, 0:QS], dbbar[q, :, QS:2 * QS]
                dbre_out[q] = cr * gr + ci * gi
                dbim_out[q] = cr * gi - ci * gr
                dcfr_out[q] = _colsum(gr * br + gi * bi)
                dcfi_out[q] = _colsum(gi * br - gr * bi)
                dcre_out[q] = dcmat[q, 0:QS, :]
                dcim_out[q] = -dcmat[q, QS:2 * QS, :]
                dlbr_out[q] = _colsum(acc[:, o:o + QS])
                dlbi_out[q] = _colsum(acc[:, o + QS:o + 2 * QS])

    blk = lambda ph, i: (0, nb - 1 - i, 0)
    oblk = lambda ph, i: (0, (nb - 1 - i) * ph + (nb - 1) * (1 - ph), 0)
    pshapes = [ar.shape, ai.shape, bre.shape, bim.shape, cre.shape, cim.shape, cfr.shape, cfi.shape, dvec.shape]
    oshapes = [bre.shape, bim.shape, cre.shape, cim.shape, cfr.shape, cfi.shape, ar.shape, ai.shape, dvec.shape, dvec.shape]
    act = pl.BlockSpec((SUBLANES, nt, sw), blk)
    view = lambda a: a.reshape(SUBLANES, chunk_len, sw)
    res, got = _call(
        body, (view(dy), yraw3, view(u), h_p, ar, ai, bre, bim, cre, cim, cfr, cfi, dvec), name=name, grid=(2, nb),
        in_specs=[act, act, act, pl.BlockSpec((tb, st), lambda ph, i: (nb - 1 - i, 0))] + [_full(p) for p in pshapes],
        out_specs=[pl.BlockSpec((SUBLANES, nt, sw), oblk)] + [_full(p) for p in oshapes],
        out_shape=[jax.ShapeDtypeStruct((SUBLANES, chunk_len, sw), bf16)] + [jax.ShapeDtypeStruct(p, f32) for p in oshapes],
        scratch_shapes=[pltpu.VMEM((tb, st), f32), pltpu.VMEM((tb, st), f32),
                        pltpu.VMEM((SUBLANES, st), f32), pltpu.VMEM((SUBLANES, st), f32),
                        pltpu.VMEM((nq, QW, 2 * QS), f32), pltpu.VMEM((nq, 2 * QS, QW), f32),
                        pltpu.VMEM((nq, QW, 2 * QS), bf16), pltpu.VMEM((nq, 2 * QS, QW), bf16)],
        sem=("arbitrary", "arbitrary"), xchg=xchg)
    return (res[0].reshape(s, sw),) + tuple(res[1:]) + (got,)


def _lnmod(x, sc, sh, *, name):
    s, d = x.shape
    tb = _tile(s, 512)

    def body(x_ref, sc_ref, sh_ref, o_ref):
        xh, _ = _ln(x_ref[...])
        o_ref[...] = (xh * (1.0 + sc_ref[...]) + sh_ref[...]).astype(bf16)

    blk = pl.BlockSpec((tb, d), lambda i: (i, 0))
    vec = pl.BlockSpec((1, d), _row)
    return pl.pallas_call(body, name=name, grid=(s // tb,), in_specs=[blk, vec, vec], out_specs=blk,
                          out_shape=jax.ShapeDtypeStruct((s, d), bf16), compiler_params=_params("parallel"))(x, sc, sh)


ROWS = 32


def _row_chunks(n_rows, rows, fn, init, start=0):
    return lax.fori_loop(start, n_rows // rows, lambda c, carry: fn(pl.multiple_of(c * rows, rows), carry), init)


def _rows_from(win, o, rows):
    if o % SUBLANES == 0:
        return win[o:o + rows]
    n = win.shape[0]
    return pltpu.roll(win, (n - o) % n, 0)[0:rows]


def _window_before(ref, halo, r0, rows, first):
    if first:
        return jnp.concatenate([halo, ref[pl.ds(0, rows), :]], axis=0)
    return ref[pl.ds(pl.multiple_of(r0 - SUBLANES, SUBLANES), rows + SUBLANES), :]


def _taps3(win, w, off, rows):
    return _rows_from(win, off, rows) * w[0] + _rows_from(win, off + 1, rows) * w[1] + _rows_from(win, off + 2, rows) * w[2]


def _fold8(x):
    acc = x[0:SUBLANES]
    for r in range(1, x.shape[0] // SUBLANES):
        acc = acc + x[r * SUBLANES:(r + 1) * SUBLANES]
    return acc


def _conv_halo_specs(tb, cw, halo, s):
    per = tb // halo
    prev = pl.BlockSpec((halo, cw), lambda i: (jnp.maximum(i * per - 1, 0), 0))
    nxt = pl.BlockSpec((halo, cw), lambda i: (jnp.minimum((i + 1) * per, s // halo - 1), 0))
    return prev, nxt


WIDE_ROWS = 16


def _shift_groups(lo, hi):
    return [(b, [o for o in range(lo, hi + 1) if o % SUBLANES == b]) for b in range(SUBLANES)]


def _shifted(win, b):
    return win if b == 0 else _rows_from(win, b, win.shape[0] - SUBLANES)


def _conv31(win, w_ref, cols, rows, lo, hi, tap_of):
    acc = None
    for b, offs in _shift_groups(lo, hi):
        if offs:
            wb = _shifted(win, b)
            for o in offs:
                term = wb[o - b:o - b + rows] * w_ref[pl.ds(tap_of(o), 1), cols]
                acc = term if acc is None else acc + term
    return acc


def _gate_into(ext, a_ref, g_ref, ah_ref, gh_ref, tb, i):
    ext[pl.ds(0, CONV_HALO), :] = jnp.where(i > 0, ah_ref[...] * _sig(gh_ref[...]), 0.0)

    def chunk(r0, carry):
        ext[pl.ds(pl.multiple_of(r0 + CONV_HALO, SUBLANES), WIDE_ROWS), :] = \
            a_ref[pl.ds(r0, WIDE_ROWS), :] * _sig(g_ref[pl.ds(r0, WIDE_ROWS), :])
        return carry

    _row_chunks(tb, WIDE_ROWS, chunk, 0)


def _causal_conv_into(v2buf, ext, w_ref, b_ref, tb, cw):
    for ct in range(cw // LANES):
        cols = slice(ct * LANES, (ct + 1) * LANES)

        def chunk(r0, carry, cols=cols):
            win = ext[pl.ds(r0, ROWS + CONV_HALO), cols]
            v2buf[pl.ds(r0, ROWS), cols] = _conv31(win, w_ref, cols, ROWS, 2, CONV_K + 1, lambda o: o - 2) + b_ref[:, cols]
            return carry

        _row_chunks(tb, ROWS, chunk, 0)


def _silu_grad(x):
    sg = _sig(x)
    return sg * (1.0 + x * (1.0 - sg))


def _conv_fwd(cva, cvg, w, b, lng, lnb, *, name, xchg=None):
    s, cw = cva.shape
    tb = _tile(s, 256)
    prev, _ = _conv_halo_specs(tb, cw, CONV_HALO, s)

    def body(a_ref, g_ref, ah_ref, gh_ref, w_ref, b_ref, lng_ref, lnb_ref, o_ref, ext, v2buf):
        _gate_into(ext, a_ref, g_ref, ah_ref, gh_ref, tb, pl.program_id(0))
        _causal_conv_into(v2buf, ext, w_ref, b_ref, tb, cw)
        xh, _ = _ln(v2buf[...])
        v3 = xh * lng_ref[...] + lnb_ref[...]
        o_ref[...] = (v3 * _sig(v3)).astype(bf16)

    blk = pl.BlockSpec((tb, cw), lambda i: (i, 0))
    vec = pl.BlockSpec((1, cw), _row)
    (v4,), got = _call(
        body, (cva, cvg, cva, cvg, w, b, lng, lnb), name=name, grid=(s // tb,),
        in_specs=[blk, blk, prev, prev, _full(w.shape), vec, vec, vec], out_specs=[blk],
        out_shape=[jax.ShapeDtypeStruct((s, cw), bf16)],
        scratch_shapes=[pltpu.VMEM((tb + CONV_HALO, cw), f32), pltpu.VMEM((tb, cw), f32)], sem=("parallel",), xchg=xchg)
    return v4, got


def _conv_bwd_ln(dv4, cva, cvg, w, b, lng, lnb, *, name):
    s, cw = cva.shape
    tb = _tile(s, 256)
    prev, _ = _conv_halo_specs(tb, cw, CONV_HALO, s)

    def body(d_ref, a_ref, g_ref, ah_ref, gh_ref, w_ref, b_ref, lng_ref, lnb_ref, o_ref, dg_ref, db_ref, ext, v2buf):
        i = pl.program_id(0)

        @pl.when(i == 0)
        def _():
            dg_ref[...] = jnp.zeros_like(dg_ref)
            db_ref[...] = jnp.zeros_like(db_ref)

        _gate_into(ext, a_ref, g_ref, ah_ref, gh_ref, tb, i)
        _causal_conv_into(v2buf, ext, w_ref, b_ref, tb, cw)
        xh, rstd = _ln(v2buf[...])
        v3 = xh * lng_ref[...] + lnb_ref[...]
        dv3 = d_ref[...] * _silu_grad(v3)
        dg_ref[...] += _colsum(dv3 * xh)
        db_ref[...] += _colsum(dv3)
        o_ref[...] = _ln_bwd(dv3 * lng_ref[...], xh, rstd)

    blk = pl.BlockSpec((tb, cw), lambda i: (i, 0))
    vec = pl.BlockSpec((1, cw), _row)
    vshape = jax.ShapeDtypeStruct((1, cw), f32)
    return pl.pallas_call(
        body, name=name, grid=(s // tb,), in_specs=[blk, blk, blk, prev, prev, _full(w.shape), vec, vec, vec],
        out_specs=[blk, vec, vec], out_shape=[jax.ShapeDtypeStruct((s, cw), f32), vshape, vshape],
        scratch_shapes=[pltpu.VMEM((tb + CONV_HALO, cw), f32), pltpu.VMEM((tb, cw), f32)],
        compiler_params=_params("arbitrary"))(dv4, cva, cvg, cva, cvg, w, b, lng, lnb)


def _conv_bwd_taps(dv2, cva, cvg, w, *, name, xchg=None):
    s, cw = cva.shape
    tb = _tile(s, 256)
    nb = s // tb
    prev, nxt = _conv_halo_specs(tb, cw, CONV_HALO, s)

    def body(d_ref, dn_ref, a_ref, g_ref, ah_ref, gh_ref, w_ref, da_ref, dg_ref, dw_ref, db_ref, sa_ref, sg_ref, ext, dext, dvbuf):
        i = pl.program_id(0)

        @pl.when(i == 0)
        def _():
            for r in (dw_ref, db_ref, sa_ref, sg_ref):
                r[...] = jnp.zeros_like(r)

        _gate_into(ext, a_ref, g_ref, ah_ref, gh_ref, tb, i)
        dext[pl.ds(tb, CONV_HALO), :] = jnp.where(i < nb - 1, dn_ref[...], 0.0)

        def copy(r0, carry):
            dext[pl.ds(r0, WIDE_ROWS), :] = d_ref[pl.ds(r0, WIDE_ROWS), :]
            return carry

        _row_chunks(tb, WIDE_ROWS, copy, 0)

        for ct in range(cw // LANES):
            cols = slice(ct * LANES, (ct + 1) * LANES)

            def back(r0, carry, cols=cols):
                win = dext[pl.ds(r0, ROWS + CONV_HALO), cols]
                dvbuf[pl.ds(r0, ROWS), cols] = _conv31(win, w_ref, cols, ROWS, 0, CONV_K - 1, lambda o: CONV_K - 1 - o)
                return carry

            _row_chunks(tb, ROWS, back, 0)

            for b, offs in _shift_groups(2, CONV_K + 1):
                def taps(r0, sums, cols=cols, b=b, offs=offs):
                    wb = _shifted(ext[pl.ds(r0, ROWS + CONV_HALO), cols], b)
                    dd = d_ref[pl.ds(r0, ROWS), cols]
                    return tuple(sm + _fold8(dd * wb[o - b:o - b + ROWS]) for sm, o in zip(sums, offs))

                sums = _row_chunks(tb, ROWS, taps, tuple(jnp.zeros((SUBLANES, LANES), f32) for _ in offs))
                for sm, o in zip(sums, offs):
                    dw_ref[pl.ds(o - 2, 1), cols] += _colsum(sm)

        def gate_back(r0, sums):
            rows = pl.ds(r0, WIDE_ROWS)
            aa, sg, dv = a_ref[rows, :], _sig(g_ref[rows, :]), dvbuf[rows, :]
            da = dv * sg
            dgate = dv * aa * sg * (1.0 - sg)
            da_ref[rows, :] = da.astype(bf16)
            dg_ref[rows, :] = dgate.astype(bf16)
            return sums[0] + _fold8(da), sums[1] + _fold8(dgate), sums[2] + _fold8(d_ref[rows, :])

        zero = jnp.zeros((SUBLANES, cw), f32)
        sums = _row_chunks(tb, WIDE_ROWS, gate_back, (zero, zero, zero))
        sa_ref[...] += _colsum(sums[0])
        sg_ref[...] += _colsum(sums[1])
        db_ref[...] += _colsum(sums[2])

    blk = pl.BlockSpec((tb, cw), lambda i: (i, 0))
    vec = pl.BlockSpec((1, cw), _row)
    vshape = jax.ShapeDtypeStruct((1, cw), f32)
    act = jax.ShapeDtypeStruct((s, cw), bf16)
    res, got = _call(
        body, (dv2, dv2, cva, cvg, cva, cvg, w), name=name, grid=(nb,), in_specs=[blk, nxt, blk, blk, prev, prev, _full(w.shape)],
        out_specs=[blk, blk, _full(w.shape), vec, vec, vec],
        out_shape=[act, act, jax.ShapeDtypeStruct(w.shape, f32), vshape, vshape, vshape],
        scratch_shapes=[pltpu.VMEM((tb + CONV_HALO, cw), f32), pltpu.VMEM((tb + CONV_HALO, cw), f32), pltpu.VMEM((tb, cw), f32)],
        sem=("arbitrary",), xchg=xchg)
    return tuple(res) + (got,)


def _glu_merge(ya, yb, ycv, gs, gc, *, name):
    s, d = ya.shape
    tb = _tile(s, 512)

    def body(ya_ref, yb_ref, ycv_ref, gs_ref, gc_ref, o_ref):
        ld = lambda r: r[...].astype(f32)
        z = ld(ya_ref) * _sig(ld(yb_ref))
        o_ref[...] = (_sig(ld(gs_ref)) * z + _sig(ld(gc_ref)) * ld(ycv_ref)).astype(bf16)

    blk = pl.BlockSpec((tb, d), lambda i: (i, 0))
    return pl.pallas_call(body, name=name, grid=(s // tb,), in_specs=[blk] * 5, out_specs=blk,
                          out_shape=jax.ShapeDtypeStruct((s, d), bf16), compiler_params=_params("parallel"))(ya, yb, ycv, gs, gc)


def _glu_merge_bwd(dm, ya, yb, ycv, gs, gc, *, name):
    s, d = ya.shape
    tb = _tile(s, 512)

    def body(dm_ref, ya_ref, yb_ref, ycv_ref, gs_ref, gc_ref, dya_ref, dyb_ref, dycv_ref, dgs_ref, dgc_ref, sgs_ref, sgc_ref):
        @pl.when(pl.program_id(0) == 0)
        def _():
            sgs_ref[...] = jnp.zeros_like(sgs_ref)
            sgc_ref[...] = jnp.zeros_like(sgc_ref)

        ld = lambda r: r[...].astype(f32)
        dmv, yav = ld(dm_ref), ld(ya_ref)
        sb, ss, scv = _sig(ld(yb_ref)), _sig(ld(gs_ref)), _sig(ld(gc_ref))
        z = yav * sb
        dz = dmv * ss
        dgs = dmv * z * ss * (1.0 - ss)
        dgc = dmv * ld(ycv_ref) * scv * (1.0 - scv)
        dya_ref[...] = (dz * sb).astype(bf16)
        dyb_ref[...] = (dz * yav * sb * (1.0 - sb)).astype(bf16)
        dycv_ref[...] = (dmv * scv).astype(bf16)
        dgs_ref[...] = dgs.astype(bf16)
        dgc_ref[...] = dgc.astype(bf16)
        sgs_ref[...] += _colsum(dgs)
        sgc_ref[...] += _colsum(dgc)

    blk = pl.BlockSpec((tb, d), lambda i: (i, 0))
    vec = pl.BlockSpec((1, d), _row)
    act = jax.ShapeDtypeStruct((s, d), bf16)
    vshape = jax.ShapeDtypeStruct((1, d), f32)
    return pl.pallas_call(body, name=name, grid=(s // tb,), in_specs=[blk] * 6, out_specs=[blk] * 5 + [vec, vec],
                          out_shape=[act] * 5 + [vshape, vshape], compiler_params=_params("arbitrary"))(dm, ya, yb, ycv, gs, gc)


def _resid_ln_mod(x, o, g, lng, lnb, sc, sh, alpha, *, name):
    s, d = x.shape
    tb = _tile(s, 512)

    def body(x_ref, o_ref, g_ref, lng_ref, lnb_ref, sc_ref, sh_ref, x1_ref, h_ref):
        xh, _ = _ln(alpha * x_ref[...] + g_ref[...] * o_ref[...])
        x1 = xh * lng_ref[...] + lnb_ref[...]
        x1_ref[...] = x1
        xh1, _ = _ln(x1)
        h_ref[...] = (xh1 * (1.0 + sc_ref[...]) + sh_ref[...]).astype(bf16)

    blk = pl.BlockSpec((tb, d), lambda i: (i, 0))
    vec = pl.BlockSpec((1, d), _row)
    return pl.pallas_call(body, name=name, grid=(s // tb,), in_specs=[blk, blk] + [vec] * 5, out_specs=[blk, blk],
                          out_shape=[jax.ShapeDtypeStruct((s, d), f32), jax.ShapeDtypeStruct((s, d), bf16)],
                          compiler_params=_params("parallel"))(x, o, g, lng, lnb, sc, sh)


def _resid_ln_loss(x1, y2, g, lng, lnb, tgt, alpha, *, name):
    s, d = x1.shape
    tb = _tile(s, 512)

    def body(x1_ref, y_ref, g_ref, lng_ref, lnb_ref, t_ref, dr_ref, dy_ref, loss_ref, dlg_ref, dlb_ref, dg_ref):
        @pl.when(pl.program_id(0) == 0)
        def _():
            for r in (loss_ref, dlg_ref, dlb_ref, dg_ref):
                r[...] = jnp.zeros_like(r)

        yv = y_ref[...]
        xh, rstd = _ln(alpha * x1_ref[...] + g_ref[...] * yv)
        err = xh * lng_ref[...] + lnb_ref[...] - t_ref[...]
        loss_ref[...] += 0.5 * jnp.sum(jnp.sum(err * err, axis=-1, keepdims=True) / d, axis=0, keepdims=True)
        dx2 = err / d
        dlg_ref[...] += _colsum(dx2 * xh)
        dlb_ref[...] += _colsum(dx2)
        dr = _ln_bwd(dx2 * lng_ref[...], xh, rstd)
        dg_ref[...] += _colsum(dr * yv)
        dr_ref[...] = dr
        dy_ref[...] = (g_ref[...] * dr).astype(bf16)

    blk = pl.BlockSpec((tb, d), lambda i: (i, 0))
    vec = pl.BlockSpec((1, d), _row)
    vshape = jax.ShapeDtypeStruct((1, d), f32)
    return pl.pallas_call(
        body, name=name, grid=(s // tb,), in_specs=[blk, blk, vec, vec, vec, blk],
        out_specs=[blk, blk, pl.BlockSpec((1, 1), _row), vec, vec, vec],
        out_shape=[jax.ShapeDtypeStruct((s, d), f32), jax.ShapeDtypeStruct((s, d), bf16),
                   jax.ShapeDtypeStruct((1, 1), f32), vshape, vshape, vshape],
        compiler_params=_params("arbitrary"))(x1, y2, g, lng, lnb, tgt)


def _mid_bwd(dh2, x1, dr2, x, o, g, sc, lng, alpha, *, name):
    s, d = x.shape
    tb = _tile(s, 512)

    def body(dh_ref, x1_ref, dr2_ref, x_ref, o_ref, g_ref, sc_ref, lng_ref,
             dr1_ref, do_ref, dsc_ref, dsh_ref, dlg_ref, dlb_ref, dg_ref):
        @pl.when(pl.program_id(0) == 0)
        def _():
            for r in (dsc_ref, dsh_ref, dlg_ref, dlb_ref, dg_ref):
                r[...] = jnp.zeros_like(r)

        dh = dh_ref[...]
        xh1, rstd1 = _ln(x1_ref[...])
        dsc_ref[...] += _colsum(dh * xh1)
        dsh_ref[...] += _colsum(dh)
        dx1 = alpha * dr2_ref[...] + _ln_bwd(dh * (1.0 + sc_ref[...]), xh1, rstd1)
        ov = o_ref[...]
        xhr, rstdr = _ln(alpha * x_ref[...] + g_ref[...] * ov)
        dlg_ref[...] += _colsum(dx1 * xhr)
        dlb_ref[...] += _colsum(dx1)
        dr1 = _ln_bwd(dx1 * lng_ref[...], xhr, rstdr)
        dg_ref[...] += _colsum(dr1 * ov)
        dr1_ref[...] = dr1
        do_ref[...] = (g_ref[...] * dr1).astype(bf16)

    blk = pl.BlockSpec((tb, d), lambda i: (i, 0))
    vec = pl.BlockSpec((1, d), _row)
    vshape = jax.ShapeDtypeStruct((1, d), f32)
    return pl.pallas_call(
        body, name=name, grid=(s // tb,), in_specs=[blk] * 5 + [vec] * 3, out_specs=[blk, blk] + [vec] * 5,
        out_shape=[jax.ShapeDtypeStruct((s, d), f32), jax.ShapeDtypeStruct((s, d), bf16)] + [vshape] * 5,
        compiler_params=_params("arbitrary"))(dh2, x1, dr2, x, o, g, sc, lng)


def _final_bwd(dh1, x, dr1, sc, alpha, *, name, xchg=None):
    s, d = x.shape
    tb = _tile(s, 512)

    def body(dh_ref, x_ref, dr1_ref, sc_ref, dx_ref, dsc_ref, dsh_ref):
        @pl.when(pl.program_id(0) == 0)
        def _():
            dsc_ref[...] = jnp.zeros_like(dsc_ref)
            dsh_ref[...] = jnp.zeros_like(dsh_ref)

        dh = dh_ref[...]
        xh, rstd = _ln(x_ref[...])
        dsc_ref[...] += _colsum(dh * xh)
        dsh_ref[...] += _colsum(dh)
        dx_ref[...] = alpha * dr1_ref[...] + _ln_bwd(dh * (1.0 + sc_ref[...]), xh, rstd)

    blk = pl.BlockSpec((tb, d), lambda i: (i, 0))
    vec = pl.BlockSpec((1, d), _row)
    vshape = jax.ShapeDtypeStruct((1, d), f32)
    res, got = _call(body, (dh1, x, dr1, sc), name=name, grid=(s // tb,), in_specs=[blk, blk, blk, vec], out_specs=[blk, vec, vec],
                     out_shape=[jax.ShapeDtypeStruct((s, d), f32), vshape, vshape], sem=("arbitrary",), xchg=xchg)
    return tuple(res) + (got,)


def _ffn_specs(s, fh, tb, tc):
    per = tb // FFN_HALO
    blk = pl.BlockSpec((tb, tc), lambda j, i: (i, j))
    prev = pl.BlockSpec((FFN_HALO, tc), lambda j, i: (jnp.maximum(i * per - 1, 0), j))
    nxt = pl.BlockSpec((FFN_HALO, tc), lambda j, i: (jnp.minimum((i + 1) * per, s // FFN_HALO - 1), j))
    taps = pl.BlockSpec((FFN_HALO, tc), lambda j, i: (0, j))
    vec = pl.BlockSpec((1, tc), lambda j, i: (0, j))
    return blk, prev, nxt, taps, vec


def _ffn_mid(upa, upv, wa, wv, ba, bv, *, name, xchg=None):
    s, fh = upa.shape
    tb, tc = _tile(s, 512), _tile(fh, 256)
    blk, prev, _, taps, vec = _ffn_specs(s, fh, tb, tc)
    off = FFN_HALO - FFN_K + 1

    def body(a_ref, v_ref, ah_ref, vh_ref, wa_ref, wv_ref, ba_ref, bv_ref, o_ref):
        first = pl.program_id(1) == 0
        halo_a, halo_v = jnp.where(first, 0.0, ah_ref[...]), jnp.where(first, 0.0, vh_ref[...])
        wa = [wa_ref[pl.ds(k, 1), :] for k in range(FFN_K)]
        wv = [wv_ref[pl.ds(k, 1), :] for k in range(FFN_K)]
        ba, bv = ba_ref[...], bv_ref[...]

        def chunk(r0, carry, head=False):
            a2 = _taps3(_window_before(a_ref, halo_a, r0, ROWS, head), wa, off, ROWS) + ba
            v2 = _taps3(_window_before(v_ref, halo_v, r0, ROWS, head), wv, off, ROWS) + bv
            o_ref[pl.ds(r0, ROWS), :] = (_gelu(a2) * v2).astype(bf16)
            return carry

        chunk(0, 0, head=True)
        _row_chunks(tb, ROWS, chunk, 0, start=1)

    (f,), got = _call(
        body, (upa, upv, upa, upv, wa, wv, ba, bv), name=name, grid=(fh // tc, s // tb),
        in_specs=[blk, blk, prev, prev, taps, taps, vec, vec], out_specs=[blk], out_shape=[jax.ShapeDtypeStruct((s, fh), bf16)],
        sem=("parallel", "arbitrary"), xchg=xchg)
    return f, got


def _ffn_mid_bwd(df, upa, upv, wa, wv, ba, bv, *, name, xchg=None):
    s, fh = upa.shape
    tb, tc = _tile(s, 512), _tile(fh, 256)
    nb = s // tb
    blk, prev, nxt, taps, vec = _ffn_specs(s, fh, tb, tc)
    off = FFN_HALO - FFN_K + 1
    te = tb + FFN_HALO

    def body(df_ref, dfn_ref, a_ref, v_ref, ah_ref, vh_ref, an_ref, vn_ref, wa_ref, wv_ref, ba_ref, bv_ref,
             da_ref, dv_ref, dwa_ref, dwv_ref, dba_ref, dbv_ref, dexta, dextv):
        i = pl.program_id(1)

        @pl.when(i == 0)
        def _():
            for r in (dwa_ref, dwv_ref, dba_ref, dbv_ref):
                r[...] = jnp.zeros_like(r)

        last = i == nb - 1
        halo_a, halo_v = jnp.where(i == 0, 0.0, ah_ref[...]), jnp.where(i == 0, 0.0, vh_ref[...])
        wa = [wa_ref[pl.ds(k, 1), :] for k in range(FFN_K)]
        wv = [wv_ref[pl.ds(k, 1), :] for k in range(FFN_K)]
        ba, bv = ba_ref[...], bv_ref[...]

        def conv_cotangents(r0, rows, xa, xv, dfe):
            sa = [_rows_from(xa, off + k, rows) for k in range(FFN_K)]
            sv = [_rows_from(xv, off + k, rows) for k in range(FFN_K)]
            a2 = sa[0] * wa[0] + sa[1] * wa[1] + sa[2] * wa[2] + ba
            v2 = sv[0] * wv[0] + sv[1] * wv[1] + sv[2] * wv[2] + bv
            cdf = 0.5 * (1.0 + lax.erf(a2 * INV_SQRT2))
            da2 = dfe * v2 * (cdf + a2 * jnp.exp(-0.5 * a2 * a2) * INV_SQRT_2PI)
            dv2 = dfe * (a2 * cdf)
            dexta[pl.ds(r0, rows), :] = da2
            dextv[pl.ds(r0, rows), :] = dv2
            return da2, dv2, sa, sv

        def chunk(r0, sums, head=False):
            da2, dv2, sa, sv = conv_cotangents(r0, ROWS, _window_before(a_ref, halo_a, r0, ROWS, head),
                                               _window_before(v_ref, halo_v, r0, ROWS, head), df_ref[pl.ds(r0, ROWS), :])
            new = [sums[k] + _fold8(da2 * sa[k]) for k in range(FFN_K)] + [sums[FFN_K] + _fold8(da2)]
            new += [sums[FFN_K + 1 + k] + _fold8(dv2 * sv[k]) for k in range(FFN_K)] + [sums[2 * FFN_K + 1] + _fold8(dv2)]
            return tuple(new)

        sums = chunk(0, tuple(jnp.zeros((SUBLANES, tc), f32) for _ in range(2 * FFN_K + 2)), head=True)
        sums = _row_chunks(tb, ROWS, chunk, sums, start=1)
        conv_cotangents(tb, FFN_HALO,
                        jnp.concatenate([a_ref[pl.ds(tb - FFN_HALO, FFN_HALO), :], jnp.where(last, 0.0, an_ref[...])], axis=0),
                        jnp.concatenate([v_ref[pl.ds(tb - FFN_HALO, FFN_HALO), :], jnp.where(last, 0.0, vn_ref[...])], axis=0),
                        jnp.where(last, 0.0, dfn_ref[...]))
        for k in range(FFN_K):
            dwa_ref[pl.ds(k, 1), :] += _colsum(sums[k])
            dwv_ref[pl.ds(k, 1), :] += _colsum(sums[FFN_K + 1 + k])
        dba_ref[...] += _colsum(sums[FFN_K])
        dbv_ref[...] += _colsum(sums[2 * FFN_K + 1])

        def back(r0, carry):
            for dext, w, o_ref in ((dexta, wa, da_ref), (dextv, wv, dv_ref)):
                dd = dext[pl.ds(r0, ROWS + FFN_HALO), :]
                o_ref[pl.ds(r0, ROWS), :] = (_rows_from(dd, 2, ROWS) * w[0] + _rows_from(dd, 1, ROWS) * w[1]
                                             + dd[0:ROWS] * w[2]).astype(bf16)
            return carry

        _row_chunks(tb, ROWS, back, 0)

    act = jax.ShapeDtypeStruct((s, fh), bf16)
    wshape = jax.ShapeDtypeStruct((FFN_HALO, fh), f32)
    vshape = jax.ShapeDtypeStruct((1, fh), f32)
    res, got = _call(
        body, (df, df, upa, upv, upa, upv, upa, upv, wa, wv, ba, bv), name=name, grid=(fh // tc, nb),
        in_specs=[blk, nxt, blk, blk, prev, prev, nxt, nxt, taps, taps, vec, vec],
        out_specs=[blk, blk, taps, taps, vec, vec], out_shape=[act, act, wshape, wshape, vshape, vshape],
        scratch_shapes=[pltpu.VMEM((te, tc), f32)] * 2, sem=("parallel", "arbitrary"), xchg=xchg)
    return tuple(res) + (got,)


def _cols_from_shards(stacked):
    _, k, n = stacked.shape
    return stacked.transpose(1, 0, 2).reshape(k, NDEV * n)


def _pad_rows(w, rows):
    return jnp.pad(w, ((0, rows - w.shape[0]), (0, 0)))


def kernel(x, c, w_cond, b_cond, w_in, b_in, ssm_lambda_re, ssm_lambda_im, ssm_log_dt, ssm_b_re, ssm_b_im, ssm_c_re, ssm_c_im, ssm_d, ssm_glu_w_a, ssm_glu_w_b, cv_dw_w, cv_dw_b, cv_ln_g, cv_ln_b, cv_w_pw, w_out, ln1_g, ln1_b, ffn_w_up, ffn_dw_w, ffn_dw_b, ffn_w_down, ln2_g, ln2_b, loss_target, m_w_cond, m_b_cond, m_w_in, m_b_in, m_ssm_lambda_re, m_ssm_lambda_im, m_ssm_log_dt, m_ssm_b_re, m_ssm_b_im, m_ssm_c_re, m_ssm_c_im, m_ssm_d, m_ssm_glu_w_a, m_ssm_glu_w_b, m_cv_dw_w, m_cv_dw_b, m_cv_ln_g, m_cv_ln_b, m_cv_w_pw, m_w_out, m_ln1_g, m_ln1_b, m_ffn_w_up, m_ffn_dw_w, m_ffn_dw_b, m_ffn_w_down, m_ln2_g, m_ln2_b, v_w_cond, v_b_cond, v_w_in, v_b_in, v_ssm_lambda_re, v_ssm_lambda_im, v_ssm_log_dt, v_ssm_b_re, v_ssm_b_im, v_ssm_c_re, v_ssm_c_im, v_ssm_d, v_ssm_glu_w_a, v_ssm_glu_w_b, v_cv_dw_w, v_cv_dw_b, v_cv_ln_g, v_cv_ln_b, v_cv_w_pw, v_w_out, v_ln1_g, v_ln1_b, v_ffn_w_up, v_ffn_dw_w, v_ffn_dw_b, v_ffn_w_down, v_ln2_g, v_ln2_b):
    weights = dict(w_cond=w_cond, b_cond=b_cond, w_in=w_in, b_in=b_in, ssm_lambda_re=ssm_lambda_re, ssm_lambda_im=ssm_lambda_im, ssm_log_dt=ssm_log_dt, ssm_b_re=ssm_b_re, ssm_b_im=ssm_b_im, ssm_c_re=ssm_c_re, ssm_c_im=ssm_c_im, ssm_d=ssm_d, ssm_glu_w_a=ssm_glu_w_a, ssm_glu_w_b=ssm_glu_w_b, cv_dw_w=cv_dw_w, cv_dw_b=cv_dw_b, cv_ln_g=cv_ln_g, cv_ln_b=cv_ln_b, cv_w_pw=cv_w_pw, w_out=w_out, ln1_g=ln1_g, ln1_b=ln1_b, ffn_w_up=ffn_w_up, ffn_dw_w=ffn_dw_w, ffn_dw_b=ffn_dw_b, ffn_w_down=ffn_w_down, ln2_g=ln2_g, ln2_b=ln2_b)
    mom_m = dict(w_cond=m_w_cond, b_cond=m_b_cond, w_in=m_w_in, b_in=m_b_in, ssm_lambda_re=m_ssm_lambda_re, ssm_lambda_im=m_ssm_lambda_im, ssm_log_dt=m_ssm_log_dt, ssm_b_re=m_ssm_b_re, ssm_b_im=m_ssm_b_im, ssm_c_re=m_ssm_c_re, ssm_c_im=m_ssm_c_im, ssm_d=m_ssm_d, ssm_glu_w_a=m_ssm_glu_w_a, ssm_glu_w_b=m_ssm_glu_w_b, cv_dw_w=m_cv_dw_w, cv_dw_b=m_cv_dw_b, cv_ln_g=m_cv_ln_g, cv_ln_b=m_cv_ln_b, cv_w_pw=m_cv_w_pw, w_out=m_w_out, ln1_g=m_ln1_g, ln1_b=m_ln1_b, ffn_w_up=m_ffn_w_up, ffn_dw_w=m_ffn_dw_w, ffn_dw_b=m_ffn_dw_b, ffn_w_down=m_ffn_w_down, ln2_g=m_ln2_g, ln2_b=m_ln2_b)
    mom_v = dict(w_cond=v_w_cond, b_cond=v_b_cond, w_in=v_w_in, b_in=v_b_in, ssm_lambda_re=v_ssm_lambda_re, ssm_lambda_im=v_ssm_lambda_im, ssm_log_dt=v_ssm_log_dt, ssm_b_re=v_ssm_b_re, ssm_b_im=v_ssm_b_im, ssm_c_re=v_ssm_c_re, ssm_c_im=v_ssm_c_im, ssm_d=v_ssm_d, ssm_glu_w_a=v_ssm_glu_w_a, ssm_glu_w_b=v_ssm_glu_w_b, cv_dw_w=v_cv_dw_w, cv_dw_b=v_cv_dw_b, cv_ln_g=v_cv_ln_g, cv_ln_b=v_cv_ln_b, cv_w_pw=v_cv_w_pw, w_out=v_w_out, ln1_g=v_ln1_g, ln1_b=v_ln1_b, ffn_w_up=v_ffn_w_up, ffn_dw_w=v_ffn_dw_w, ffn_dw_b=v_ffn_dw_b, ffn_w_down=v_ffn_w_down, ln2_g=v_ln2_g, ln2_b=v_ln2_b)
    names = list(weights)

    s, d = x.shape[1], x.shape[2]
    sw = cw = d // 2
    fh = ffn_w_down.shape[1] * NDEV
    ng, nq = sw // SSM_GROUP, sw // QW
    gq = ng // nq
    alpha = 2.0 ** 0.25
    me = 4 * lax.axis_index("x") + 2 * lax.axis_index("y") + lax.axis_index("c")
    xs, tgt = x[0], loss_target[0]

    col_names = ["w_in", "ssm_glu_w_a", "ssm_glu_w_b", "cv_w_pw", "ffn_w_up"]
    row_names = ["w_out", "ffn_w_down"]
    big = col_names + row_names
    sent = lambda ns: [weights[n][0].astype(bf16) for n in ns]
    got_in, got_c, got_cv_taps, got_ffn_taps = _exchange(sent(["w_in"]) + [c, cv_dw_w[0, :, 0], ffn_dw_w[0, :, 0]],
                                                         scatter=False, name="gather_in")
    o1, o2, o3, o4 = sw, sw + cw, sw + 2 * cw, sw + 2 * cw + d
    in_bounds = ((0, o1), (o1, o2), (o2, o3), (o3, o4), (o4, o4 + d))
    w_u, w_cva, w_cvg, w_gs, w_gc = _unshard_cols(got_in, in_bounds, name="unshard_w_in")
    b_u, b_cva, b_cvg, b_gs, b_gc = (b_in[:, a:b] for a, b in in_bounds)
    c_all = got_c.reshape(NDEV, d)
    cv_taps = _cols_from_shards(got_cv_taps)
    ffn_taps = _cols_from_shards(got_ffn_taps)
    cv_w32 = _pad_rows(cv_taps, CONV_HALO)
    ffn_wa, ffn_wv = _pad_rows(ffn_taps[:, :fh], FFN_HALO), _pad_rows(ffn_taps[:, fh:], FFN_HALO)
    ffn_ba, ffn_bv = ffn_dw_b[:, :fh], ffn_dw_b[:, fh:]

    ncond = w_cond.shape[2]
    b_cond_mine = lax.dynamic_slice(b_cond, (0, me * ncond), (1, ncond))
    mod_cols = _cond_fwd(c_all, w_cond[0], b_cond_mine, name="cond_fwd")
    mod_all, = _exchange([mod_cols], scatter=False, name="gather_mod")
    mod_mine = lax.dynamic_slice(mod_all, (0, me, 0), (NDEV, 1, ncond)).reshape(1, 6 * d)
    sh1, sc1, g1, sh2, sc2, g2 = (mod_mine[:, k * d:(k + 1) * d] for k in range(6))

    lam_re, lam_im, log_dt = ssm_lambda_re[0], ssm_lambda_im[0], ssm_log_dt[0][:, None]
    lbr, lbi, cfr, cfi = _ssm_prep(lam_re, lam_im, log_dt, name="ssm_prep")
    rows_q = lambda a: a.reshape(nq, 1, QS)
    eye = jnp.eye(gq, dtype=f32)

    def b_mat(b):
        bt = b.reshape(nq, gq, SSM_STATE, SSM_GROUP).transpose(0, 1, 3, 2)
        return jnp.einsum("qgpn,gh->qgphn", bt, eye).reshape(nq, QW, QS)

    def c_mat(cc):
        ct = cc.reshape(nq, gq, SSM_GROUP, SSM_STATE)
        return jnp.einsum("qgpn,gh->qhngp", ct, eye).reshape(nq, QS, QW)

    def b_unmat(mt):
        return jnp.einsum("qgpgn->qgnp", mt.reshape(nq, gq, SSM_GROUP, gq, SSM_STATE)).reshape(ng, SSM_STATE, SSM_GROUP)

    def c_unmat(mt):
        return jnp.einsum("qgngp->qgpn", mt.reshape(nq, gq, SSM_STATE, gq, SSM_GROUP)).reshape(ng, SSM_GROUP, SSM_STATE)

    ssm_args = (rows_q(lbr), rows_q(lbi), b_mat(ssm_b_re[0]), b_mat(ssm_b_im[0]), c_mat(ssm_c_re[0]), c_mat(ssm_c_im[0]),
                rows_q(cfr), rows_q(cfi), ssm_d[0].reshape(1, sw))

    h1 = _lnmod(xs, sc1, sh1, name="ln_mod1")
    u = _mm([(h1, w_u)], b_u, name="in_u")
    cva = _mm([(h1, w_cva)], b_cva, name="in_cva")
    cvg = _mm([(h1, w_cvg)], b_cvg, name="in_cvg")
    gs = _mm([(h1, w_gs)], b_gs, out_dtype=bf16, name="in_gs")
    gc = _mm([(h1, w_gc)], b_gc, out_dtype=bf16, name="in_gc")
    v4, (got_a, got_b, got_pw, got_o) = _conv_fwd(
        cva, cvg, cv_w32, cv_dw_b, cv_ln_g, cv_ln_b, name="conv_fwd",
        xchg=(sent(["ssm_glu_w_a", "ssm_glu_w_b", "cv_w_pw", "w_out"]), False))
    h_p, yraw3, y, (got_up,) = _ssm_fwd(u, *ssm_args, name="ssm_fwd", xchg=(sent(["ffn_w_up"]), False))
    w_a, = _unshard_cols(got_a, ((0, d),), name="unshard_glu_a")
    w_b, = _unshard_cols(got_b, ((0, d),), name="unshard_glu_b")
    w_pw, = _unshard_cols(got_pw, ((0, d),), name="unshard_conv_pw")
    w_upa, w_upv = _unshard_cols(got_up, ((0, fh), (fh, 2 * fh)), name="unshard_ffn_up")
    w_o = got_o.reshape(d, d)
    ya = _mm([(y, w_a)], out_dtype=bf16, name="glu_a")
    yb = _mm([(y, w_b)], out_dtype=bf16, name="glu_b")
    ycv = _mm([(v4, w_pw)], out_dtype=bf16, name="conv_pw")
    merged = _glu_merge(ya, yb, ycv, gs, gc, name="merge")
    o = _mm([(merged, w_o)], name="out_proj")
    x1, h2 = _resid_ln_mod(xs, o, g1, ln1_g, ln1_b, sc2, sh2, alpha, name="resid_ln1")
    upa = _mm([(h2, w_upa)], name="ffn_up_a")
    upv = _mm([(h2, w_upv)], name="ffn_up_v")
    f, (got_dn,) = _ffn_mid(upa, upv, ffn_wa, ffn_wv, ffn_ba, ffn_bv, name="ffn_mid", xchg=(sent(["ffn_w_down"]), False))
    w_dn = got_dn.reshape(fh, d)
    y2 = _mm([(f, w_dn)], name="ffn_down")
    dr2, dy2, loss_part, d_ln2_g, d_ln2_b, d_g2 = _resid_ln_loss(x1, y2, g2, ln2_g, ln2_b, tgt, alpha, name="resid_ln2_loss")

    gw = {}
    df = _mm([(dy2, w_dn)], trans_w=True, name="d_ffn_down")
    gw["ffn_w_down"] = _mm_tn(f, dy2, out_dtype=bf16, name="g_ffn_down").reshape((NDEV,) + ffn_w_down[0].shape)
    received = {}
    dupa, dupv, d_ffn_wa, d_ffn_wv, d_ffn_ba, d_ffn_bv, (received["ffn_w_down"],) = _ffn_mid_bwd(
        df, upa, upv, ffn_wa, ffn_wv, ffn_ba, ffn_bv, name="ffn_mid_bwd", xchg=([gw["ffn_w_down"]], True))
    dh2 = _mm([(dupa, w_upa), (dupv, w_upv)], trans_w=True, name="d_ffn_up")
    gw["ffn_w_up"] = _shard_cols([_mm_tn(h2, dupa, name="g_ffn_up_a"), _mm_tn(h2, dupv, name="g_ffn_up_v")], out_dtype=bf16,
                                 name="shard_ffn_up")
    dr1, do, d_sc2, d_sh2, d_ln1_g, d_ln1_b, d_g1 = _mid_bwd(dh2, x1, dr2, xs, o, g1, sc2, ln1_g, alpha, name="mid_bwd")
    dmerged = _mm([(do, w_o)], trans_w=True, out_dtype=bf16, name="d_out_proj")
    gw["w_out"] = _mm_tn(merged, do, out_dtype=bf16, name="g_out_proj").reshape((NDEV,) + w_out[0].shape)
    dya, dyb, dycv, dgs, dgc, s_gs, s_gc = _glu_merge_bwd(dmerged, ya, yb, ycv, gs, gc, name="merge_bwd")
    dy = _mm([(dya, w_a), (dyb, w_b)], trans_w=True, name="d_glu")
    gw["ssm_glu_w_a"] = _shard_cols([_mm_tn(y, dya, name="g_glu_a")], out_dtype=bf16, name="shard_glu_a")
    gw["ssm_glu_w_b"] = _shard_cols([_mm_tn(y, dyb, name="g_glu_b")], out_dtype=bf16, name="shard_glu_b")
    dv4 = _mm([(dycv, w_pw)], trans_w=True, name="d_conv_pw")
    gw["cv_w_pw"] = _shard_cols([_mm_tn(v4, dycv, name="g_conv_pw")], out_dtype=bf16, name="shard_conv_pw")
    dv2, d_cv_ln_g, d_cv_ln_b = _conv_bwd_ln(dv4, cva, cvg, cv_w32, cv_dw_b, cv_ln_g, cv_ln_b, name="conv_bwd_ln")
    dcva, dcvg, d_cv_w32, d_cv_b, s_cva, s_cvg, (received["ffn_w_up"],) = _conv_bwd_taps(
        dv2, cva, cvg, cv_w32, name="conv_bwd_taps", xchg=([gw["ffn_w_up"]], True))
    late = ["w_out", "ssm_glu_w_a", "ssm_glu_w_b", "cv_w_pw"]
    (du, d_bre_m, d_bim_m, d_cre_m, d_cim_m, d_cfr, d_cfi, d_lbr, d_lbi, d_d, s_u, got_late) = _ssm_bwd(
        dy, yraw3, u, h_p, *ssm_args, name="ssm_bwd", xchg=([gw[n] for n in late], True))
    received.update(zip(late, got_late))
    gshape = lam_re.shape
    d_lam_re, d_lam_im, d_log_dt = _ssm_prep_bwd(
        lam_re, lam_im, log_dt, [a.reshape(gshape) for a in (d_lbr, d_lbi, d_cfr, d_cfi)], name="ssm_prep_bwd")
    small = {
        "b_in": jnp.concatenate([s_u, s_cva, s_cvg, s_gs, s_gc], axis=1),
        "ssm_lambda_re": d_lam_re, "ssm_lambda_im": d_lam_im, "ssm_log_dt": d_log_dt,
        "ssm_b_re": b_unmat(d_bre_m), "ssm_b_im": b_unmat(d_bim_m), "ssm_c_re": c_unmat(d_cre_m), "ssm_c_im": c_unmat(d_cim_m),
        "ssm_d": d_d, "cv_dw_w": d_cv_w32[:CONV_K], "cv_dw_b": d_cv_b, "cv_ln_g": d_cv_ln_g, "cv_ln_b": d_cv_ln_b,
        "ln1_g": d_ln1_g, "ln1_b": d_ln1_b,
        "ffn_dw_w": jnp.concatenate([d_ffn_wa[:FFN_K], d_ffn_wv[:FFN_K]], axis=1),
        "ffn_dw_b": jnp.concatenate([d_ffn_ba, d_ffn_bv], axis=1), "ln2_g": d_ln2_g, "ln2_b": d_ln2_b,
        "mod_g1": d_g1, "mod_sh2": d_sh2, "mod_sc2": d_sc2, "mod_g2": d_g2, "loss": loss_part,
    }
    small_names = list(small)
    small_shapes = [small[n].shape for n in small_names]
    gw["w_in"], (small_all,) = _mm_tn_sharded(h1, [du, dcva, dcvg, dgs, dgc], out_dtype=bf16, name="g_in",
                                              xchg=([_pack([small[n] for n in small_names])], False))
    dh1, (received["w_in"],) = _mm(
        [(du, w_u), (dcva, w_cva), (dcvg, w_cvg), (dgs, w_gs), (dgc, w_gc)], trans_w=True, name="d_in",
        xchg=([gw["w_in"]], True))
    grad_x, d_sc1, d_sh1, _ = _final_bwd(dh1, xs, dr1, sc1, alpha, name="final_bwd")

    grads, delta, new_m, new_v = {}, {}, {}, {}
    for n in big:
        ride = ([_pack([d_sh1, d_sc1])], False) if n == big[0] else None
        res = _sum_adamw(received[n], weights[n][0], mom_m[n][0], mom_v[n][0], name="adamw_" + n, xchg=ride)
        grads[n], delta[n], new_m[n], new_v[n] = res[:4]
        if ride is not None:
            last_all, = res[4]

    small_sum = dict(zip(small_names, _unpack(_sum_parts(small_all, name="sum_small").reshape(-1), small_shapes)))
    last_sum = _unpack(_sum_parts(last_all, name="sum_last").reshape(-1), [(1, d), (1, d)])
    per_dev = dict(zip(small_names, _unpack(small_all.reshape(NDEV, -1), small_shapes)))
    last_dev = _unpack(last_all.reshape(NDEV, -1), [(1, d), (1, d)])
    dmod_all = jnp.concatenate(last_dev + [per_dev[k] for k in ("mod_g1", "mod_sh2", "mod_sc2", "mod_g2")], axis=-1).reshape(NDEV, 6 * d)
    dmod_cols = lax.dynamic_slice(dmod_all.reshape(NDEV, NDEV, ncond), (0, me, 0), (NDEV, 1, ncond)).reshape(NDEV, ncond)
    grads["w_cond"] = _cond_bwd(c_all, dmod_cols, name="cond_bwd")
    loss = small_sum.pop("loss").reshape(())
    grads["b_cond"] = jnp.concatenate(last_sum + [small_sum.pop(k) for k in ("mod_g1", "mod_sh2", "mod_sc2", "mod_g2")], axis=1)
    for n, g in small_sum.items():
        grads[n] = g
    ntap = cv_dw_w.shape[3]
    grads["cv_dw_w"] = lax.dynamic_slice(grads["cv_dw_w"], (0, me * ntap), (CONV_K, ntap))
    nffn = ffn_dw_w.shape[3]
    grads["ffn_dw_w"] = lax.dynamic_slice(grads["ffn_dw_w"], (0, me * nffn), (FFN_K, nffn))
    grads = {n: grads[n].reshape(weights[n].shape) for n in names}

    delta["w_cond"], new_m["w_cond"], new_v["w_cond"] = _adamw(w_cond[0], grads["w_cond"][0], m_w_cond[0], v_w_cond[0],
                                                               name="adamw_w_cond")
    rest = [n for n in names if n not in ["w_cond"] + big]
    rest_shapes = [weights[n].shape for n in rest]
    packed = [_pack([t[n] for n in rest]) for t in (weights, grads, mom_m, mom_v)]
    for tgt_dict, res in zip((delta, new_m, new_v), _adamw(*packed, name="adamw_small")):
        for n, a in zip(rest, _unpack(res.reshape(-1), rest_shapes)):
            tgt_dict[n] = a
    shaped = lambda t: [t[n].reshape(weights[n].shape) for n in names]

    return (loss, grad_x[None], *shaped(grads), *shaped(delta), *shaped(new_m), *shaped(new_v))
```

```python
import functools
import math

import jax
import jax.numpy as jnp
from jax import lax
from jax.experimental import pallas as pl
from jax.experimental.pallas import tpu as pltpu

f32 = jnp.float32
bf16 = jnp.bfloat16

NDEV = 8
LANES = 128
SUBLANES = 8
SSM_GROUP = 16
SSM_STATE = 64
QW = 128
QS = 512
CONV_K = 31
CONV_HALO = 32
FFN_K = 3
FFN_HALO = 8
LN_EPS = 1e-5
ADAM_LR, ADAM_B1, ADAM_B2, ADAM_EPS, ADAM_WD, ADAM_STEP = 0.001, 0.9, 0.999, 1e-08, 0.01, 10
VMEM_LIMIT = 56 * 1024 * 1024
W_TILE_BYTES = 6 * 1024 * 1024
SUM_ROWS = 512
EW_BLOCK_BYTES = 2 * 1024 * 1024
INV_SQRT2 = 1.0 / math.sqrt(2.0)
INV_SQRT_2PI = 1.0 / math.sqrt(2.0 * math.pi)
MESH = pl.DeviceIdType.MESH


def _tile(n, want):
    t = min(n, want)
    while n % t:
        t //= 2
    return t


def _col_tile(n, rows, bytes_per):
    best = LANES if n % LANES == 0 else n
    for t in range(LANES, n + 1, LANES):
        if n % t == 0 and rows * t * bytes_per <= W_TILE_BYTES:
            best = t
    return best


def _params(*sem):
    return pltpu.CompilerParams(dimension_semantics=sem, vmem_limit_bytes=VMEM_LIMIT)


def _row(i):
    return (0, 0)


def _full(shape):
    nd = len(shape)
    return pl.BlockSpec(shape, lambda *a: (0,) * nd)


def _ln(x):
    mu = jnp.mean(x, axis=-1, keepdims=True)
    xc = x - mu
    var = jnp.mean(xc * xc, axis=-1, keepdims=True)
    rstd = lax.rsqrt(var + LN_EPS)
    return xc * rstd, rstd


def _ln_bwd(dxhat, xhat, rstd):
    return rstd * (dxhat - jnp.mean(dxhat, axis=-1, keepdims=True) - xhat * jnp.mean(dxhat * xhat, axis=-1, keepdims=True))


def _sig(x):
    return 1.0 / (1.0 + jnp.exp(-x))


def _gelu(x):
    return 0.5 * x * (1.0 + lax.erf(x * INV_SQRT2))


def _gelu_grad(x):
    return 0.5 * (1.0 + lax.erf(x * INV_SQRT2)) + x * jnp.exp(-0.5 * x * x) * INV_SQRT_2PI


def _colsum(x):
    return jnp.sum(x, axis=0, keepdims=True)


def _mm(pairs, bias=None, *, trans_w=False, out_dtype=f32, name, xchg=None):
    n_p = len(pairs)
    m = pairs[0][0].shape[0]
    n = pairs[0][1].shape[0 if trans_w else 1]
    ktot = sum(x.shape[1] for x, _ in pairs)
    tm = _tile(m, 512)
    tn = _col_tile(n, ktot, 2)
    dn = (((1,), (1,)), ((), ())) if trans_w else (((1,), (0,)), ((), ()))

    def body(*refs):
        o_ref = refs[-1]
        acc = None
        for xr, wr in zip(refs[:n_p], refs[n_p:2 * n_p]):
            r = lax.dot_general(xr[...].astype(bf16), wr[...].astype(bf16), dn, preferred_element_type=f32)
            acc = r if acc is None else acc + r
        if bias is not None:
            acc = acc + refs[2 * n_p][...]
        o_ref[...] = acc.astype(out_dtype)

    in_specs = [pl.BlockSpec((tm, x.shape[1]), lambda j, i: (i, 0)) for x, _ in pairs]
    if trans_w:
        in_specs += [pl.BlockSpec((tn, w.shape[1]), lambda j, i: (j, 0)) for _, w in pairs]
    else:
        in_specs += [pl.BlockSpec((w.shape[0], tn), lambda j, i: (0, j)) for _, w in pairs]
    args = [x for x, _ in pairs] + [w for _, w in pairs]
    if bias is not None:
        in_specs.append(pl.BlockSpec((1, tn), lambda j, i: (0, j)))
        args.append(bias)
    (out,), got = _call(
        body, args, name=name, grid=(n // tn, m // tm), in_specs=in_specs,
        out_specs=[pl.BlockSpec((tm, tn), lambda j, i: (i, j))], out_shape=[jax.ShapeDtypeStruct((m, n), out_dtype)],
        sem=("parallel", "arbitrary"), xchg=xchg)
    return out if xchg is None else (out, got)


def _mm_tn(x, dy, *, out_dtype=f32, name):
    m, k = x.shape
    n = dy.shape[1]
    tm = _tile(m, 512)
    tn = _col_tile(n, k, 4)
    steps = m // tm

    def body(x_ref, dy_ref, o_ref, *scratch):
        acc = scratch[0] if scratch else o_ref

        @pl.when(pl.program_id(1) == 0)
        def _():
            acc[...] = jnp.zeros_like(acc)

        acc[...] += lax.dot_general(x_ref[...].astype(bf16), dy_ref[...].astype(bf16), (((0,), (0,)), ((), ())),
                                    preferred_element_type=f32)
        if scratch:
            @pl.when(pl.program_id(1) == steps - 1)
            def _():
                o_ref[...] = acc[...].astype(out_dtype)

    return pl.pallas_call(
        body, name=name, grid=(n // tn, steps),
        in_specs=[pl.BlockSpec((tm, k), lambda j, i: (i, 0)), pl.BlockSpec((tm, tn), lambda j, i: (i, j))],
        out_specs=pl.BlockSpec((k, tn), lambda j, i: (0, j)),
        out_shape=jax.ShapeDtypeStruct((k, n), out_dtype),
        scratch_shapes=[] if out_dtype == f32 else [pltpu.VMEM((k, tn), f32)],
        compiler_params=_params("parallel", "arbitrary"),
    )(x, dy)


def _mm_tn_sharded(x, dys, *, out_dtype, name, xchg=None):
    m, k = x.shape
    widths = [dy.shape[1] for dy in dys]
    n = sum(widths) // NDEV
    tm = _tile(m, 512)
    steps = m // tm
    n_d = len(dys)

    def body(x_ref, *refs):
        dy_refs, o_ref, acc = refs[:n_d], refs[n_d], refs[n_d + 1]
        i = pl.program_id(0)

        @pl.when(i == 0)
        def _():
            acc[...] = jnp.zeros_like(acc)

        xb = x_ref[...].astype(bf16)
        off = 0
        for dy_ref, w in zip(dy_refs, widths):
            acc[:, off:off + w] += lax.dot_general(xb, dy_ref[...].astype(bf16), (((0,), (0,)), ((), ())), preferred_element_type=f32)
            off += w

        @pl.when(i == steps - 1)
        def _():
            for j in range(NDEV):
                o_ref[j] = acc[:, n * j:n * (j + 1)].astype(out_dtype)

    (out,), got = _call(
        body, (x, *dys), name=name, grid=(steps,),
        in_specs=[pl.BlockSpec((tm, k), lambda i: (i, 0))] + [pl.BlockSpec((tm, w), lambda i: (i, 0)) for w in widths],
        out_specs=[pl.BlockSpec((NDEV, k, n), lambda i: (0, 0, 0))], out_shape=[jax.ShapeDtypeStruct((NDEV, k, n), out_dtype)],
        scratch_shapes=[pltpu.VMEM((k, sum(widths)), f32)], sem=("arbitrary",), xchg=xchg)
    return out if xchg is None else (out, got)


def _exchange(arrs, *, scatter, name):
    n = len(arrs)

    def body(*refs):
        _exchange_copies(refs[:n], refs[n:2 * n], refs[2 * n:], scatter, True, True)

    return pl.pallas_call(
        body, name=name, in_specs=[HBM_SPEC] * n, out_specs=[HBM_SPEC] * n, out_shape=_exchange_out_shape(arrs, scatter),
        scratch_shapes=_exchange_sems(n),
    )(*arrs)


HBM_SPEC = pl.BlockSpec(memory_space=pltpu.HBM)


def _flags(scatter, n):
    return list(scatter) if isinstance(scatter, (list, tuple)) else [scatter] * n


def _exchange_out_shape(arrs, scatter):
    return [jax.ShapeDtypeStruct(a.shape if sc else (NDEV,) + a.shape, a.dtype) for a, sc in zip(arrs, _flags(scatter, len(arrs)))]


def _exchange_sems(n):
    return [pltpu.SemaphoreType.DMA(((NDEV - 1) * n,)), pltpu.SemaphoreType.DMA(((NDEV - 1) * n,)), pltpu.SemaphoreType.DMA((n,))]


def _exchange_copies(x_refs, o_refs, sems, scatter, start, wait):
    n = len(x_refs)
    flags = _flags(scatter, n)
    send_sems, recv_sems, local_sems = sems
    ix, iy, ic = lax.axis_index("x"), lax.axis_index("y"), lax.axis_index("c")
    me = 4 * ix + 2 * iy + ic
    local = [pltpu.make_async_copy(x.at[me] if sc else x, o.at[me], local_sems.at[a])
             for a, (x, o, sc) in enumerate(zip(x_refs, o_refs, flags))]

    def peer(k):
        return (1 - ix if k & 4 else ix, 1 - iy if k & 2 else iy, 1 - ic if k & 1 else ic)

    def index(p):
        return 4 * p[0] + 2 * p[1] + p[2]

    def copy(a, k, src, dst, to):
        sem = (k - 1) * n + a
        return pltpu.make_async_remote_copy(src_ref=src, dst_ref=dst, send_sem=send_sems.at[sem], recv_sem=recv_sems.at[sem],
                                            device_id=to, device_id_type=MESH)

    sends, arrivals, passed_on = [], [], []
    for a, (x, o, sc) in enumerate(zip(x_refs, o_refs, flags)):
        if sc:
            for k in range(1, NDEV):
                p = peer(k)
                sends.append(copy(a, k, x.at[index(p)], o.at[me], p))
                arrivals.append(copy(a, k, x.at[me], o.at[index(p)], p))
        else:
            sib = peer(1)
            sends.append(copy(a, 1, x, o.at[me], sib))
            arrivals.append(copy(a, 1, x, o.at[index(sib)], sib))
            for k in (2, 4, 6):
                p, q = peer(k), peer(k + 1)
                sends.append(copy(a, k, x, o.at[me], p))
                passed_on.append((copy(a, k, x, o.at[index(p)], p), copy(a, k + 1, o.at[index(p)], o.at[index(p)], sib)))
                arrivals.append(copy(a, k + 1, o.at[index(q)], o.at[index(q)], sib))
    if start:
        for cp in local + sends:
            cp.start()
    if wait:
        for landed, hand_over in passed_on:
            landed.wait_recv()
            hand_over.start()
        for cp in arrivals:
            cp.wait_recv()
        for cp in sends + [hand_over for _, hand_over in passed_on]:
            cp.wait_send()
        for cp in local:
            cp.wait()


def _call(body, args, *, name, grid, in_specs, out_specs, out_shape, scratch_shapes=(), sem, xchg=None):
    if xchg is None:
        return pl.pallas_call(body, name=name, grid=grid, in_specs=in_specs, out_specs=out_specs, out_shape=out_shape,
                              scratch_shapes=list(scratch_shapes), compiler_params=_params(*sem))(*args), None
    arrs, scatter = xchg
    n, ni, no, ns = len(arrs), len(in_specs), len(out_specs), len(scratch_shapes)

    def wrapped(*refs):
        ins, x_refs = refs[:ni], refs[ni:ni + n]
        outs, o_refs = refs[ni + n:ni + n + no], refs[ni + n + no:ni + 2 * n + no]
        scratch, sems = refs[ni + 2 * n + no:ni + 2 * n + no + ns], refs[ni + 2 * n + no + ns:]
        ids = [pl.program_id(a) for a in range(len(grid))]
        first = functools.reduce(jnp.logical_and, [p == 0 for p in ids])
        last = functools.reduce(jnp.logical_and, [p == g - 1 for p, g in zip(ids, grid)])

        @pl.when(first)
        def _():
            _exchange_copies(x_refs, o_refs, sems, scatter, True, False)

        body(*ins, *outs, *scratch)

        @pl.when(last)
        def _():
            _exchange_copies(x_refs, o_refs, sems, scatter, False, True)

    res = pl.pallas_call(
        wrapped, name=name, grid=grid, in_specs=list(in_specs) + [HBM_SPEC] * n, out_specs=list(out_specs) + [HBM_SPEC] * n,
        out_shape=list(out_shape) + _exchange_out_shape(arrs, scatter),
        scratch_shapes=list(scratch_shapes) + _exchange_sems(n),
        compiler_params=_params(*("arbitrary",) * len(grid)))(*args, *arrs)
    return res[:no], res[no:]


def _sum_parts(parts, *, name):
    r = parts.shape[1]

    def body(p_ref, o_ref):
        acc = p_ref[0]
        for j in range(1, NDEV):
            acc = acc + p_ref[j]
        o_ref[...] = acc

    return pl.pallas_call(body, name=name, out_shape=jax.ShapeDtypeStruct((r, LANES), f32), compiler_params=_params())(parts)


def _col_pieces(n, bounds):
    out = []
    for p, (a, b) in enumerate(bounds):
        for j in range(NDEV):
            lo, hi = max(a, n * j), min(b, n * (j + 1))
            if lo < hi:
                out.append((p, j, lo - a, lo - n * j, hi - lo))
    return out


def _unshard_cols(stacked, bounds, *, name):
    _, k, n = stacked.shape
    tk = _tile(k, 256)
    plan = _col_pieces(n, bounds)

    def body(x_ref, *o_refs):
        for p, j, po, so, w in plan:
            o_refs[p][:, po:po + w] = x_ref[j, :, so:so + w]

    return pl.pallas_call(
        body, name=name, grid=(k // tk,), in_specs=[pl.BlockSpec((NDEV, tk, n), lambda i: (0, i, 0))],
        out_specs=[pl.BlockSpec((tk, b - a), lambda i: (i, 0)) for a, b in bounds],
        out_shape=[jax.ShapeDtypeStruct((k, b - a), stacked.dtype) for a, b in bounds],
        compiler_params=_params("parallel"))(stacked)


def _shard_cols(pieces, *, out_dtype, name):
    k = pieces[0].shape[0]
    bounds, off = [], 0
    for p in pieces:
        bounds.append((off, off + p.shape[1]))
        off += p.shape[1]
    n = off // NDEV
    tk = _tile(k, 256)
    plan = _col_pieces(n, bounds)

    def body(*refs):
        o_ref = refs[-1]
        for p, j, po, so, w in plan:
            o_ref[j, :, so:so + w] = refs[p][:, po:po + w].astype(out_dtype)

    return pl.pallas_call(
        body, name=name, grid=(k // tk,), in_specs=[pl.BlockSpec((tk, b - a), lambda i: (i, 0)) for a, b in bounds],
        out_specs=pl.BlockSpec((NDEV, tk, n), lambda i: (0, i, 0)),
        out_shape=jax.ShapeDtypeStruct((NDEV, k, n), out_dtype),
        compiler_params=_params("parallel"))(*pieces)


def _pack(arrs):
    flat = jnp.concatenate([a.reshape(-1) for a in arrs])
    pad = (-flat.shape[0]) % (SUBLANES * LANES)
    return jnp.pad(flat, (0, pad)).reshape(-1, LANES)


def _unpack(flat, shapes):
    out, off = [], 0
    for s in shapes:
        n = math.prod(s)
        out.append(flat[..., off:off + n].reshape(flat.shape[:-1] + tuple(s)))
        off += n
    return out


def _adamw_math(w, gg, m, v):
    nm = ADAM_B1 * m + (1.0 - ADAM_B1) * gg
    nv = ADAM_B2 * v + (1.0 - ADAM_B2) * (gg * gg)
    m_hat = nm / (1.0 - ADAM_B1 ** ADAM_STEP)
    v_hat = nv / (1.0 - ADAM_B2 ** ADAM_STEP)
    return -ADAM_LR * (m_hat / (jnp.sqrt(v_hat) + ADAM_EPS) + ADAM_WD * w), nm, nv


def _row_block(r, c, copies):
    tr = r
    while copies * tr * c * 4 > EW_BLOCK_BYTES and tr % (4 * SUBLANES) == 0:
        tr //= 2
    return tr


def _adamw(w, g, m, v, *, name):
    r, c = w.shape
    tr = _row_block(r, c, 1)

    def body(w_ref, g_ref, m_ref, v_ref, d_ref, nm_ref, nv_ref):
        d_ref[...], nm_ref[...], nv_ref[...] = _adamw_math(w_ref[...], g_ref[...], m_ref[...], v_ref[...])

    spec = pl.BlockSpec((tr, c), lambda i: (i, 0))
    shp = jax.ShapeDtypeStruct((r, c), f32)
    return pl.pallas_call(
        body, name=name, grid=(r // tr,), in_specs=[spec] * 4, out_specs=[spec] * 3, out_shape=[shp] * 3,
        compiler_params=_params("parallel"),
    )(w, g, m, v)


def _sum_adamw(parts, w, m, v, *, name, xchg=None):
    r, c = w.shape
    tr = _row_block(r, c, NDEV)

    def body(p_ref, w_ref, m_ref, v_ref, g_ref, d_ref, nm_ref, nv_ref):
        gg = p_ref[0].astype(f32)
        for j in range(1, NDEV):
            gg = gg + p_ref[j].astype(f32)
        g_ref[...] = gg
        d_ref[...], nm_ref[...], nv_ref[...] = _adamw_math(w_ref[...], gg, m_ref[...], v_ref[...])

    spec = pl.BlockSpec((tr, c), lambda i: (i, 0))
    shp = jax.ShapeDtypeStruct((r, c), f32)
    res, got = _call(
        body, (parts, w, m, v), name=name, grid=(r // tr,),
        in_specs=[pl.BlockSpec((NDEV, tr, c), lambda i: (0, i, 0))] + [spec] * 3,
        out_specs=[spec] * 4, out_shape=[shp] * 4, sem=("parallel",), xchg=xchg)
    return tuple(res) if xchg is None else tuple(res) + (got,)


def _cond_fwd(c_all, w, b, *, name):
    nb, n = c_all.shape[0], w.shape[1]

    def body(c_ref, w_ref, b_ref, o_ref):
        cc = c_ref[...]
        o_ref[...] = jnp.dot(cc * _sig(cc), w_ref[...], preferred_element_type=f32,
                             precision=lax.Precision.HIGHEST) + b_ref[...]

    return pl.pallas_call(body, name=name, out_shape=jax.ShapeDtypeStruct((nb, n), f32),
                          compiler_params=_params())(c_all, w, b)


def _cond_bwd(c_all, dmod, *, name):
    d, n = c_all.shape[1], dmod.shape[1]

    def body(c_ref, g_ref, o_ref):
        cc = c_ref[...]
        o_ref[...] = lax.dot_general(cc * _sig(cc), g_ref[...], (((0,), (0,)), ((), ())), preferred_element_type=f32,
                                     precision=lax.Precision.HIGHEST)

    return pl.pallas_call(body, name=name, out_shape=jax.ShapeDtypeStruct((d, n), f32),
                          compiler_params=_params())(c_all, dmod)


def _ssm_disc(lam_re, lam_im, log_dt):
    lr = jnp.minimum(lam_re, -1e-4)
    li = lam_im
    dt = jnp.exp(log_dt)
    mag = jnp.exp(lr * dt)
    ang = li * dt
    lbr, lbi = mag * jnp.cos(ang), mag * jnp.sin(ang)
    num_r, num_i = lbr - 1.0, lbi
    den = lr * lr + li * li
    return lbr, lbi, (num_r * lr + num_i * li) / den, (num_i * lr - num_r * li) / den


def _ssm_prep(lam_re, lam_im, log_dt, *, name):
    def body(a, b, c, o1, o2, o3, o4):
        o1[...], o2[...], o3[...], o4[...] = _ssm_disc(a[...], b[...], c[...])

    shp = jax.ShapeDtypeStruct(lam_re.shape, f32)
    return pl.pallas_call(body, name=name, out_shape=[shp] * 4, compiler_params=_params())(lam_re, lam_im, log_dt)


def _ssm_prep_bwd(lam_re, lam_im, log_dt, cts, *, name):
    def body(a, b, c, g1, g2, g3, g4, o1, o2, o3):
        _, vjp = jax.vjp(_ssm_disc, a[...], b[...], c[...])
        o1[...], o2[...], o3[...] = vjp((g1[...], g2[...], g3[...], g4[...]))

    shp = jax.ShapeDtypeStruct(lam_re.shape, f32)
    return pl.pallas_call(body, name=name, out_shape=[shp, shp, jax.ShapeDtypeStruct(log_dt.shape, f32)],
                          compiler_params=_params())(lam_re, lam_im, log_dt, *cts)


def _step_major(tb, nt, dtype):
    r = lax.broadcasted_iota(jnp.int32, (tb, tb), 0)
    k = lax.broadcasted_iota(jnp.int32, (tb, tb), 1)
    return (k == (r % SUBLANES) * nt + r // SUBLANES).astype(dtype)


def _chunk_major(tb, nt, dtype):
    k = lax.broadcasted_iota(jnp.int32, (tb, tb), 0)
    r = lax.broadcasted_iota(jnp.int32, (tb, tb), 1)
    return (k == (r % SUBLANES) * nt + r // SUBLANES).astype(dtype)


def _permute_f32(pmat, x):
    hi = x.astype(bf16)
    rest = x - hi.astype(f32)
    mid = rest.astype(bf16)
    lo = (rest - mid.astype(f32)).astype(bf16)
    move = lambda part: jnp.dot(pmat, part, preferred_element_type=f32)
    return (move(hi) + move(mid)) + move(lo)


def _permute_bf16(pmat, x):
    return jnp.dot(pmat, x, preferred_element_type=f32).astype(bf16)


def _chain_carries(loc_r, loc_i, pr, pi_, forward):
    row = lax.broadcasted_iota(jnp.int32, loc_r.shape, 0)
    shift = 1 if forward else SUBLANES - 1
    order = range(1, SUBLANES) if forward else range(SUBLANES - 2, -1, -1)
    er, ei = loc_r, loc_i
    for k in order:
        sr, si = pltpu.roll(er, shift, 0), pltpu.roll(ei, shift, 0)
        er = jnp.where(row == k, loc_r + pr * sr - pi_ * si, er)
        ei = jnp.where(row == k, loc_i + pr * si + pi_ * sr, ei)
    edge = 0 if forward else SUBLANES - 1
    return (jnp.where(row == edge, 0.0, pltpu.roll(er, shift, 0)), jnp.where(row == edge, 0.0, pltpu.roll(ei, shift, 0)))


def _chunk_power(ar, ai, chunk_len):
    pr, pi_ = ar, ai
    for _ in range(int(math.log2(chunk_len))):
        pr, pi_ = pr * pr - pi_ * pi_, 2.0 * pr * pi_
    return pr, pi_


def _ssm_mats(bre_ref, bim_ref, cre_ref, cim_ref, cfr_ref, cfi_ref, bbar_s, cmat_s, nq):
    for q in range(nq):
        cr, ci, br, bi = cfr_ref[q], cfi_ref[q], bre_ref[q], bim_ref[q]
        bbar_s[q, :, 0:QS] = (cr * br - ci * bi).astype(bf16)
        bbar_s[q, :, QS:2 * QS] = (cr * bi + ci * br).astype(bf16)
        cmat_s[q, 0:QS, :] = cre_ref[q].astype(bf16)
        cmat_s[q, QS:2 * QS, :] = (-cim_ref[q]).astype(bf16)


def _ssm_fwd(u, ar, ai, bre, bim, cre, cim, cfr, cfi, dvec, *, name, xchg=None):
    s, sw = u.shape
    nq = sw // QW
    st = nq * 2 * QS
    tb = _tile(s, 256)
    nb, nt, chunk_len = s // tb, tb // SUBLANES, s // SUBLANES
    assert chunk_len & (chunk_len - 1) == 0 and nt % 16 == 0

    def body(u_ref, ar_ref, ai_ref, bre_ref, bim_ref, cre_ref, cim_ref, cfr_ref, cfi_ref, d_ref,
             h_out, yraw_out, y_out, buf, hc, bbar_s, cmat_s):
        ph, i = pl.program_id(0), pl.program_id(1)

        @pl.when(i == 0)
        def _():
            _ssm_mats(bre_ref, bim_ref, cre_ref, cim_ref, cfr_ref, cfi_ref, bbar_s, cmat_s, nq)

        @pl.when((ph == 0) & (i == 0))
        def _():
            hc[...] = jnp.zeros_like(hc)

        @pl.when((ph == 1) & (i == 0))
        def _():
            for q in range(nq):
                o = q * 2 * QS
                pr, pi_ = _chunk_power(ar_ref[q], ai_ref[q], chunk_len)
                sr, si = _chain_carries(hc[:, o:o + QS], hc[:, o + QS:o + 2 * QS], pr, pi_, True)
                hc[:, o:o + QS] = sr
                hc[:, o + QS:o + 2 * QS] = si

        uu = u_ref[...].reshape(tb, sw)
        up = _permute_bf16(_step_major(tb, nt, bf16), uu.astype(bf16))
        for q in range(nq):
            o = q * 2 * QS
            buf[:, o:o + 2 * QS] = jnp.dot(up[:, q * QW:(q + 1) * QW], bbar_s[q], preferred_element_type=f32)

        for q in range(nq):
            o = q * 2 * QS
            a_r = jnp.broadcast_to(ar_ref[q], (SUBLANES, QS))
            a_i = jnp.broadcast_to(ai_ref[q], (SUBLANES, QS))

            def step(t, carry, o=o, a_r=a_r, a_i=a_i):
                hr, hi = carry
                r0 = pl.multiple_of(t * SUBLANES, SUBLANES)
                nr = a_r * hr - a_i * hi + buf[pl.ds(r0, SUBLANES), o:o + QS]
                ni = a_r * hi + a_i * hr + buf[pl.ds(r0, SUBLANES), o + QS:o + 2 * QS]
                buf[pl.ds(r0, SUBLANES), o:o + QS] = nr
                buf[pl.ds(r0, SUBLANES), o + QS:o + 2 * QS] = ni
                return nr, ni

            hr, hi = lax.fori_loop(0, nt, step, (hc[:, o:o + QS], hc[:, o + QS:o + 2 * QS]))
            hc[:, o:o + QS] = hr
            hc[:, o + QS:o + 2 * QS] = hi

        @pl.when(ph == 1)
        def _():
            back = _chunk_major(tb, nt, bf16)
            for q in range(nq):
                o = q * 2 * QS
                cs = slice(q * QW, (q + 1) * QW)
                hq = buf[:, o:o + 2 * QS].astype(bf16)
                h_out[:, o:o + 2 * QS] = hq
                yq = _permute_f32(back, jnp.dot(hq, cmat_s[q], preferred_element_type=f32)) + d_ref[:, cs] * uu[:, cs]
                yraw_out[:, :, cs] = yq.reshape(SUBLANES, nt, QW)
                y_out[:, :, cs] = _gelu(yq).astype(bf16).reshape(SUBLANES, nt, QW)

    blk = lambda ph, i: (0, i, 0)
    oblk = lambda ph, i: (0, i * ph, 0)
    act = lambda dt: jax.ShapeDtypeStruct((SUBLANES, chunk_len, sw), dt)
    (h_p, yraw3, y3), got = _call(
        body, (u.reshape(SUBLANES, chunk_len, sw), ar, ai, bre, bim, cre, cim, cfr, cfi, dvec), name=name, grid=(2, nb),
        in_specs=[pl.BlockSpec((SUBLANES, nt, sw), blk), _full(ar.shape), _full(ai.shape), _full(bre.shape), _full(bim.shape),
                  _full(cre.shape), _full(cim.shape), _full(cfr.shape), _full(cfi.shape), _full(dvec.shape)],
        out_specs=[pl.BlockSpec((tb, st), lambda ph, i: (i * ph, 0)), pl.BlockSpec((SUBLANES, nt, sw), oblk),
                   pl.BlockSpec((SUBLANES, nt, sw), oblk)],
        out_shape=[jax.ShapeDtypeStruct((s, st), bf16), act(f32), act(bf16)],
        scratch_shapes=[pltpu.VMEM((tb, st), f32), pltpu.VMEM((SUBLANES, st), f32),
                        pltpu.VMEM((nq, QW, 2 * QS), bf16), pltpu.VMEM((nq, 2 * QS, QW), bf16)],
        sem=("arbitrary", "arbitrary"), xchg=xchg)
    return h_p, yraw3, y3.reshape(s, sw), got


def _ssm_bwd(dy, yraw3, u, h_p, ar, ai, bre, bim, cre, cim, cfr, cfi, dvec, *, name, xchg=None):
    s, sw = u.shape
    nq = sw // QW
    st = nq * 2 * QS
    tb = _tile(s, 256)
    nb, nt, chunk_len = s // tb, tb // SUBLANES, s // SUBLANES

    def body(dy_ref, yraw_ref, u_ref, h_ref, ar_ref, ai_ref, bre_ref, bim_ref, cre_ref, cim_ref, cfr_ref, cfi_ref, d_ref,
             du_out, dbre_out, dbim_out, dcre_out, dcim_out, dcfr_out, dcfi_out, dlbr_out, dlbi_out, dd_out, dbu_out,
             buf, hf, rc, acc, dbbar, dcmat, bbar_s, cmat_s):
        ph, i = pl.program_id(0), pl.program_id(1)

        @pl.when(i == 0)
        def _():
            _ssm_mats(bre_ref, bim_ref, cre_ref, cim_ref, cfr_ref, cfi_ref, bbar_s, cmat_s, nq)

        @pl.when((ph == 0) & (i == 0))
        def _():
            rc[...] = jnp.zeros_like(rc)

        @pl.when((ph == 1) & (i == 0))
        def _():
            for q in range(nq):
                o = q * 2 * QS
                pr, pi_ = _chunk_power(ar_ref[q], ai_ref[q], chunk_len)
                sr, si = _chain_carries(rc[:, o:o + QS], rc[:, o + QS:o + 2 * QS], pr, -pi_, False)
                rc[:, o:o + QS] = sr
                rc[:, o + QS:o + 2 * QS] = si
            acc[...] = jnp.zeros_like(acc)
            dbbar[...] = jnp.zeros_like(dbbar)
            dcmat[...] = jnp.zeros_like(dcmat)
            dd_out[...] = jnp.zeros_like(dd_out)
            dbu_out[...] = jnp.zeros_like(dbu_out)

        dyraw = (dy_ref[...] * _gelu_grad(yraw_ref[...])).reshape(tb, sw)
        fwd_perm = _step_major(tb, nt, bf16)
        dyp = _permute_bf16(fwd_perm, dyraw.astype(bf16))
        for q in range(nq):
            o = q * 2 * QS
            buf[:, o:o + 2 * QS] = lax.dot_general(dyp[:, q * QW:(q + 1) * QW], cmat_s[q], (((1,), (1,)), ((), ())),
                                                   preferred_element_type=f32)

        def recur(with_grad):
            for q in range(nq):
                o = q * 2 * QS
                a_r = jnp.broadcast_to(ar_ref[q], (SUBLANES, QS))
                a_i = jnp.broadcast_to(ai_ref[q], (SUBLANES, QS))

                def step(j, carry, o=o, a_r=a_r, a_i=a_i):
                    r0 = pl.multiple_of((nt - 1 - j) * SUBLANES, SUBLANES)
                    if with_grad:
                        rr, ri, gr, gi = carry
                        hr = hf[pl.ds(r0, SUBLANES), o:o + QS]
                        hi = hf[pl.ds(r0, SUBLANES), o + QS:o + 2 * QS]
                        gr = gr + hr * rr + hi * ri
                        gi = gi + hr * ri - hi * rr
                    else:
                        rr, ri = carry
                    nr = buf[pl.ds(r0, SUBLANES), o:o + QS] + a_r * rr + a_i * ri
                    ni = buf[pl.ds(r0, SUBLANES), o + QS:o + 2 * QS] + a_r * ri - a_i * rr
                    buf[pl.ds(r0, SUBLANES), o:o + QS] = nr
                    buf[pl.ds(r0, SUBLANES), o + QS:o + 2 * QS] = ni
                    return (nr, ni, gr, gi) if with_grad else (nr, ni)

                init = (rc[:, o:o + QS], rc[:, o + QS:o + 2 * QS])
                if with_grad:
                    init = init + (acc[:, o:o + QS], acc[:, o + QS:o + 2 * QS])
                res = lax.fori_loop(0, nt, step, init)
                rc[:, o:o + QS] = res[0]
                rc[:, o + QS:o + 2 * QS] = res[1]
                if with_grad:
                    acc[:, o:o + QS] = res[2]
                    acc[:, o + QS:o + 2 * QS] = res[3]

        @pl.when(ph == 0)
        def _():
            recur(False)

        @pl.when(ph == 1)
        def _():
            hf[...] = h_ref[...].astype(f32)
            recur(True)
            uu = u_ref[...].reshape(tb, sw)
            up = _permute_bf16(fwd_perm, uu.astype(bf16))
            back = _chunk_major(tb, nt, bf16)
            dd_out[...] += _colsum(dyraw * uu)
            for q in range(nq):
                o = q * 2 * QS
                cs = slice(q * QW, (q + 1) * QW)
                lam = buf[:, o:o + 2 * QS].astype(bf16)
                duq = _permute_f32(back, lax.dot_general(lam, bbar_s[q], (((1,), (1,)), ((), ())), preferred_element_type=f32)) \
                    + d_ref[:, cs] * dyraw[:, cs]
                du_out[:, :, cs] = duq.astype(bf16).reshape(SUBLANES, nt, QW)
                dbu_out[:, cs] += _colsum(duq)
                dbbar[q] += lax.dot_general(up[:, cs], lam, (((0,), (0,)), ((), ())), preferred_element_type=f32)
                dcmat[q] += lax.dot_general(h_ref[:, o:o + 2 * QS], dyp[:, cs], (((0,), (0,)), ((), ())),
                                            preferred_element_type=f32)

        @pl.when((ph == 1) & (i == nb - 1))
        def _():
            for q in range(nq):
                o = q * 2 * QS
                cr, ci, br, bi = cfr_ref[q], cfi_ref[q], bre_ref[q], bim_ref[q]
                gr, gi = dbbar[q, :, 0:QS], dbbar[q, :, QS:2 * QS]
                dbre_out[q] = cr * gr + ci * gi
                dbim_out[q] = cr * gi - ci * gr
                dcfr_out[q] = _colsum(gr * br + gi * bi)
                dcfi_out[q] = _colsum(gi * br - gr * bi)
                dcre_out[q] = dcmat[q, 0:QS, :]
                dcim_out[q] = -dcmat[q, QS:2 * QS, :]
                dlbr_out[q] = _colsum(acc[:, o:o + QS])
                dlbi_out[q] = _colsum(acc[:, o + QS:o + 2 * QS])

    blk = lambda ph, i: (0, nb - 1 - i, 0)
    oblk = lambda ph, i: (0, (nb - 1 - i) * ph + (nb - 1) * (1 - ph), 0)
    pshapes = [ar.shape, ai.shape, bre.shape, bim.shape, cre.shape, cim.shape, cfr.shape, cfi.shape, dvec.shape]
    oshapes = [bre.shape, bim.shape, cre.shape, cim.shape, cfr.shape, cfi.shape, ar.shape, ai.shape, dvec.shape, dvec.shape]
    act = pl.BlockSpec((SUBLANES, nt, sw), blk)
    view = lambda a: a.reshape(SUBLANES, chunk_len, sw)
    res, got = _call(
        body, (view(dy), yraw3, view(u), h_p, ar, ai, bre, bim, cre, cim, cfr, cfi, dvec), name=name, grid=(2, nb),
        in_specs=[act, act, act, pl.BlockSpec((tb, st), lambda ph, i: (nb - 1 - i, 0))] + [_full(p) for p in pshapes],
        out_specs=[pl.BlockSpec((SUBLANES, nt, sw), oblk)] + [_full(p) for p in oshapes],
        out_shape=[jax.ShapeDtypeStruct((SUBLANES, chunk_len, sw), bf16)] + [jax.ShapeDtypeStruct(p, f32) for p in oshapes],
        scratch_shapes=[pltpu.VMEM((tb, st), f32), pltpu.VMEM((tb, st), f32),
                        pltpu.VMEM((SUBLANES, st), f32), pltpu.VMEM((SUBLANES, st), f32),
                        pltpu.VMEM((nq, QW, 2 * QS), f32), pltpu.VMEM((nq, 2 * QS, QW), f32),
                        pltpu.VMEM((nq, QW, 2 * QS), bf16), pltpu.VMEM((nq, 2 * QS, QW), bf16)],
        sem=("arbitrary", "arbitrary"), xchg=xchg)
    return (res[0].reshape(s, sw),) + tuple(res[1:]) + (got,)


def _lnmod(x, sc, sh, *, name):
    s, d = x.shape
    tb = _tile(s, 512)

    def body(x_ref, sc_ref, sh_ref, o_ref):
        xh, _ = _ln(x_ref[...])
        o_ref[...] = (xh * (1.0 + sc_ref[...]) + sh_ref[...]).astype(bf16)

    blk = pl.BlockSpec((tb, d), lambda i: (i, 0))
    vec = pl.BlockSpec((1, d), _row)
    return pl.pallas_call(body, name=name, grid=(s // tb,), in_specs=[blk, vec, vec], out_specs=blk,
                          out_shape=jax.ShapeDtypeStruct((s, d), bf16), compiler_params=_params("parallel"))(x, sc, sh)


ROWS = 32


def _row_chunks(n_rows, rows, fn, init, start=0):
    return lax.fori_loop(start, n_rows // rows, lambda c, carry: fn(pl.multiple_of(c * rows, rows), carry), init)


def _rows_from(win, o, rows):
    if o % SUBLANES == 0:
        return win[o:o + rows]
    n = win.shape[0]
    return pltpu.roll(win, (n - o) % n, 0)[0:rows]


def _window_before(ref, halo, r0, rows, first):
    if first:
        return jnp.concatenate([halo, ref[pl.ds(0, rows), :]], axis=0)
    return ref[pl.ds(pl.multiple_of(r0 - SUBLANES, SUBLANES), rows + SUBLANES), :]


def _taps3(win, w, off, rows):
    return _rows_from(win, off, rows) * w[0] + _rows_from(win, off + 1, rows) * w[1] + _rows_from(win, off + 2, rows) * w[2]


def _fold8(x):
    acc = x[0:SUBLANES]
    for r in range(1, x.shape[0] // SUBLANES):
        acc = acc + x[r * SUBLANES:(r + 1) * SUBLANES]
    return acc


def _conv_halo_specs(tb, cw, halo, s):
    per = tb // halo
    prev = pl.BlockSpec((halo, cw), lambda i: (jnp.maximum(i * per - 1, 0), 0))
    nxt = pl.BlockSpec((halo, cw), lambda i: (jnp.minimum((i + 1) * per, s // halo - 1), 0))
    return prev, nxt


WIDE_ROWS = 16


def _shift_groups(lo, hi):
    return [(b, [o for o in range(lo, hi + 1) if o % SUBLANES == b]) for b in range(SUBLANES)]


def _shifted(win, b):
    return win if b == 0 else _rows_from(win, b, win.shape[0] - SUBLANES)


def _conv31(win, w_ref, cols, rows, lo, hi, tap_of):
    acc = None
    for b, offs in _shift_groups(lo, hi):
        if offs:
            wb = _shifted(win, b)
            for o in offs:
                term = wb[o - b:o - b + rows] * w_ref[pl.ds(tap_of(o), 1), cols]
                acc = term if acc is None else acc + term
    return acc


def _gate_into(ext, a_ref, g_ref, ah_ref, gh_ref, tb, i):
    ext[pl.ds(0, CONV_HALO), :] = jnp.where(i > 0, ah_ref[...] * _sig(gh_ref[...]), 0.0)

    def chunk(r0, carry):
        ext[pl.ds(pl.multiple_of(r0 + CONV_HALO, SUBLANES), WIDE_ROWS), :] = \
            a_ref[pl.ds(r0, WIDE_ROWS), :] * _sig(g_ref[pl.ds(r0, WIDE_ROWS), :])
        return carry

    _row_chunks(tb, WIDE_ROWS, chunk, 0)


def _causal_conv_into(v2buf, ext, w_ref, b_ref, tb, cw):
    for ct in range(cw // LANES):
        cols = slice(ct * LANES, (ct + 1) * LANES)

        def chunk(r0, carry, cols=cols):
            win = ext[pl.ds(r0, ROWS + CONV_HALO), cols]
            v2buf[pl.ds(r0, ROWS), cols] = _conv31(win, w_ref, cols, ROWS, 2, CONV_K + 1, lambda o: o - 2) + b_ref[:, cols]
            return carry

        _row_chunks(tb, ROWS, chunk, 0)


def _silu_grad(x):
    sg = _sig(x)
    return sg * (1.0 + x * (1.0 - sg))


def _conv_fwd(cva, cvg, w, b, lng, lnb, *, name, xchg=None):
    s, cw = cva.shape
    tb = _tile(s, 256)
    prev, _ = _conv_halo_specs(tb, cw, CONV_HALO, s)

    def body(a_ref, g_ref, ah_ref, gh_ref, w_ref, b_ref, lng_ref, lnb_ref, o_ref, ext, v2buf):
        _gate_into(ext, a_ref, g_ref, ah_ref, gh_ref, tb, pl.program_id(0))
        _causal_conv_into(v2buf, ext, w_ref, b_ref, tb, cw)
        xh, _ = _ln(v2buf[...])
        v3 = xh * lng_ref[...] + lnb_ref[...]
        o_ref[...] = (v3 * _sig(v3)).astype(bf16)

    blk = pl.BlockSpec((tb, cw), lambda i: (i, 0))
    vec = pl.BlockSpec((1, cw), _row)
    (v4,), got = _call(
        body, (cva, cvg, cva, cvg, w, b, lng, lnb), name=name, grid=(s // tb,),
        in_specs=[blk, blk, prev, prev, _full(w.shape), vec, vec, vec], out_specs=[blk],
        out_shape=[jax.ShapeDtypeStruct((s, cw), bf16)],
        scratch_shapes=[pltpu.VMEM((tb + CONV_HALO, cw), f32), pltpu.VMEM((tb, cw), f32)], sem=("parallel",), xchg=xchg)
    return v4, got


def _conv_bwd_ln(dv4, cva, cvg, w, b, lng, lnb, *, name):
    s, cw = cva.shape
    tb = _tile(s, 256)
    prev, _ = _conv_halo_specs(tb, cw, CONV_HALO, s)

    def body(d_ref, a_ref, g_ref, ah_ref, gh_ref, w_ref, b_ref, lng_ref, lnb_ref, o_ref, dg_ref, db_ref, ext, v2buf):
        i = pl.program_id(0)

        @pl.when(i == 0)
        def _():
            dg_ref[...] = jnp.zeros_like(dg_ref)
            db_ref[...] = jnp.zeros_like(db_ref)

        _gate_into(ext, a_ref, g_ref, ah_ref, gh_ref, tb, i)
        _causal_conv_into(v2buf, ext, w_ref, b_ref, tb, cw)
        xh, rstd = _ln(v2buf[...])
        v3 = xh * lng_ref[...] + lnb_ref[...]
        dv3 = d_ref[...] * _silu_grad(v3)
        dg_ref[...] += _colsum(dv3 * xh)
        db_ref[...] += _colsum(dv3)
        o_ref[...] = _ln_bwd(dv3 * lng_ref[...], xh, rstd)

    blk = pl.BlockSpec((tb, cw), lambda i: (i, 0))
    vec = pl.BlockSpec((1, cw), _row)
    vshape = jax.ShapeDtypeStruct((1, cw), f32)
    return pl.pallas_call(
        body, name=name, grid=(s // tb,), in_specs=[blk, blk, blk, prev, prev, _full(w.shape), vec, vec, vec],
        out_specs=[blk, vec, vec], out_shape=[jax.ShapeDtypeStruct((s, cw), f32), vshape, vshape],
        scratch_shapes=[pltpu.VMEM((tb + CONV_HALO, cw), f32), pltpu.VMEM((tb, cw), f32)],
        compiler_params=_params("arbitrary"))(dv4, cva, cvg, cva, cvg, w, b, lng, lnb)


def _conv_bwd_taps(dv2, cva, cvg, w, *, name, xchg=None):
    s, cw = cva.shape
    tb = _tile(s, 256)
    nb = s // tb
    prev, nxt = _conv_halo_specs(tb, cw, CONV_HALO, s)

    def body(d_ref, dn_ref, a_ref, g_ref, ah_ref, gh_ref, w_ref, da_ref, dg_ref, dw_ref, db_ref, sa_ref, sg_ref,
             ext, dext, dvbuf, tap_sums):
        i = pl.program_id(0)

        @pl.when(i == 0)
        def _():
            for r in (dw_ref, db_ref, sa_ref, sg_ref):
                r[...] = jnp.zeros_like(r)

        _gate_into(ext, a_ref, g_ref, ah_ref, gh_ref, tb, i)
        dext[pl.ds(tb, CONV_HALO), :] = jnp.where(i < nb - 1, dn_ref[...], 0.0)

        def copy(r0, carry):
            dext[pl.ds(r0, WIDE_ROWS), :] = d_ref[pl.ds(r0, WIDE_ROWS), :]
            return carry

        _row_chunks(tb, WIDE_ROWS, copy, 0)

        for ct in range(cw // LANES):
            cols = slice(ct * LANES, (ct + 1) * LANES)

            tap_sums[...] = jnp.zeros_like(tap_sums)

            def back(r0, carry, cols=cols):
                win = dext[pl.ds(r0, ROWS + CONV_HALO), cols]
                dvbuf[pl.ds(r0, ROWS), cols] = _conv31(win, w_ref, cols, ROWS, 0, CONV_K - 1, lambda o: CONV_K - 1 - o)
                win = ext[pl.ds(r0, ROWS + CONV_HALO), cols]
                dd = d_ref[pl.ds(r0, ROWS), cols]
                for b, offs in _shift_groups(2, CONV_K + 1):
                    wb = _shifted(win, b)
                    for o in offs:
                        tap_sums[o - 2] += _fold8(dd * wb[o - b:o - b + ROWS])
                return carry

            _row_chunks(tb, ROWS, back, 0)
            for k in range(CONV_K):
                dw_ref[pl.ds(k, 1), cols] += _colsum(tap_sums[k])

        def gate_back(r0, sums):
            rows = pl.ds(r0, WIDE_ROWS)
            aa, sg, dv = a_ref[rows, :], _sig(g_ref[rows, :]), dvbuf[rows, :]
            da = dv * sg
            dgate = dv * aa * sg * (1.0 - sg)
            da_ref[rows, :] = da.astype(bf16)
            dg_ref[rows, :] = dgate.astype(bf16)
            return sums[0] + _fold8(da), sums[1] + _fold8(dgate), sums[2] + _fold8(d_ref[rows, :])

        zero = jnp.zeros((SUBLANES, cw), f32)
        sums = _row_chunks(tb, WIDE_ROWS, gate_back, (zero, zero, zero))
        sa_ref[...] += _colsum(sums[0])
        sg_ref[...] += _colsum(sums[1])
        db_ref[...] += _colsum(sums[2])

    blk = pl.BlockSpec((tb, cw), lambda i: (i, 0))
    vec = pl.BlockSpec((1, cw), _row)
    vshape = jax.ShapeDtypeStruct((1, cw), f32)
    act = jax.ShapeDtypeStruct((s, cw), bf16)
    res, got = _call(
        body, (dv2, dv2, cva, cvg, cva, cvg, w), name=name, grid=(nb,), in_specs=[blk, nxt, blk, blk, prev, prev, _full(w.shape)],
        out_specs=[blk, blk, _full(w.shape), vec, vec, vec],
        out_shape=[act, act, jax.ShapeDtypeStruct(w.shape, f32), vshape, vshape, vshape],
        scratch_shapes=[pltpu.VMEM((tb + CONV_HALO, cw), f32), pltpu.VMEM((tb + CONV_HALO, cw), f32), pltpu.VMEM((tb, cw), f32),
                        pltpu.VMEM((CONV_HALO, SUBLANES, LANES), f32)],
        sem=("arbitrary",), xchg=xchg)
    return tuple(res) + (got,)


def _glu_merge(ya, yb, ycv, gs, gc, *, name):
    s, d = ya.shape
    tb = _tile(s, 512)

    def body(ya_ref, yb_ref, ycv_ref, gs_ref, gc_ref, o_ref):
        ld = lambda r: r[...].astype(f32)
        z = ld(ya_ref) * _sig(ld(yb_ref))
        o_ref[...] = (_sig(ld(gs_ref)) * z + _sig(ld(gc_ref)) * ld(ycv_ref)).astype(bf16)

    blk = pl.BlockSpec((tb, d), lambda i: (i, 0))
    return pl.pallas_call(body, name=name, grid=(s // tb,), in_specs=[blk] * 5, out_specs=blk,
                          out_shape=jax.ShapeDtypeStruct((s, d), bf16), compiler_params=_params("parallel"))(ya, yb, ycv, gs, gc)


def _glu_merge_bwd(dm, ya, yb, ycv, gs, gc, *, name):
    s, d = ya.shape
    tb = _tile(s, 512)

    def body(dm_ref, ya_ref, yb_ref, ycv_ref, gs_ref, gc_ref, dya_ref, dyb_ref, dycv_ref, dgs_ref, dgc_ref, sgs_ref, sgc_ref):
        @pl.when(pl.program_id(0) == 0)
        def _():
            sgs_ref[...] = jnp.zeros_like(sgs_ref)
            sgc_ref[...] = jnp.zeros_like(sgc_ref)

        ld = lambda r: r[...].astype(f32)
        dmv, yav = ld(dm_ref), ld(ya_ref)
        sb, ss, scv = _sig(ld(yb_ref)), _sig(ld(gs_ref)), _sig(ld(gc_ref))
        z = yav * sb
        dz = dmv * ss
        dgs = dmv * z * ss * (1.0 - ss)
        dgc = dmv * ld(ycv_ref) * scv * (1.0 - scv)
        dya_ref[...] = (dz * sb).astype(bf16)
        dyb_ref[...] = (dz * yav * sb * (1.0 - sb)).astype(bf16)
        dycv_ref[...] = (dmv * scv).astype(bf16)
        dgs_ref[...] = dgs.astype(bf16)
        dgc_ref[...] = dgc.astype(bf16)
        sgs_ref[...] += _colsum(dgs)
        sgc_ref[...] += _colsum(dgc)

    blk = pl.BlockSpec((tb, d), lambda i: (i, 0))
    vec = pl.BlockSpec((1, d), _row)
    act = jax.ShapeDtypeStruct((s, d), bf16)
    vshape = jax.ShapeDtypeStruct((1, d), f32)
    return pl.pallas_call(body, name=name, grid=(s // tb,), in_specs=[blk] * 6, out_specs=[blk] * 5 + [vec, vec],
                          out_shape=[act] * 5 + [vshape, vshape], compiler_params=_params("arbitrary"))(dm, ya, yb, ycv, gs, gc)


def _resid_ln_mod(x, o, g, lng, lnb, sc, sh, alpha, *, name):
    s, d = x.shape
    tb = _tile(s, 512)

    def body(x_ref, o_ref, g_ref, lng_ref, lnb_ref, sc_ref, sh_ref, x1_ref, h_ref):
        xh, _ = _ln(alpha * x_ref[...] + g_ref[...] * o_ref[...])
        x1 = xh * lng_ref[...] + lnb_ref[...]
        x1_ref[...] = x1
        xh1, _ = _ln(x1)
        h_ref[...] = (xh1 * (1.0 + sc_ref[...]) + sh_ref[...]).astype(bf16)

    blk = pl.BlockSpec((tb, d), lambda i: (i, 0))
    vec = pl.BlockSpec((1, d), _row)
    return pl.pallas_call(body, name=name, grid=(s // tb,), in_specs=[blk, blk] + [vec] * 5, out_specs=[blk, blk],
                          out_shape=[jax.ShapeDtypeStruct((s, d), f32), jax.ShapeDtypeStruct((s, d), bf16)],
                          compiler_params=_params("parallel"))(x, o, g, lng, lnb, sc, sh)


def _resid_ln_loss(x1, y2, g, lng, lnb, tgt, alpha, *, name):
    s, d = x1.shape
    tb = _tile(s, 512)

    def body(x1_ref, y_ref, g_ref, lng_ref, lnb_ref, t_ref, dr_ref, dy_ref, loss_ref, dlg_ref, dlb_ref, dg_ref):
        @pl.when(pl.program_id(0) == 0)
        def _():
            for r in (loss_ref, dlg_ref, dlb_ref, dg_ref):
                r[...] = jnp.zeros_like(r)

        yv = y_ref[...]
        xh, rstd = _ln(alpha * x1_ref[...] + g_ref[...] * yv)
        err = xh * lng_ref[...] + lnb_ref[...] - t_ref[...]
        loss_ref[...] += 0.5 * jnp.sum(jnp.sum(err * err, axis=-1, keepdims=True) / d, axis=0, keepdims=True)
        dx2 = err / d
        dlg_ref[...] += _colsum(dx2 * xh)
        dlb_ref[...] += _colsum(dx2)
        dr = _ln_bwd(dx2 * lng_ref[...], xh, rstd)
        dg_ref[...] += _colsum(dr * yv)
        dr_ref[...] = dr
        dy_ref[...] = (g_ref[...] * dr).astype(bf16)

    blk = pl.BlockSpec((tb, d), lambda i: (i, 0))
    vec = pl.BlockSpec((1, d), _row)
    vshape = jax.ShapeDtypeStruct((1, d), f32)
    return pl.pallas_call(
        body, name=name, grid=(s // tb,), in_specs=[blk, blk, vec, vec, vec, blk],
        out_specs=[blk, blk, pl.BlockSpec((1, 1), _row), vec, vec, vec],
        out_shape=[jax.ShapeDtypeStruct((s, d), f32), jax.ShapeDtypeStruct((s, d), bf16),
                   jax.ShapeDtypeStruct((1, 1), f32), vshape, vshape, vshape],
        compiler_params=_params("arbitrary"))(x1, y2, g, lng, lnb, tgt)


def _mid_bwd(dh2, x1, dr2, x, o, g, sc, lng, alpha, *, name):
    s, d = x.shape
    tb = _tile(s, 512)

    def body(dh_ref, x1_ref, dr2_ref, x_ref, o_ref, g_ref, sc_ref, lng_ref,
             dr1_ref, do_ref, dsc_ref, dsh_ref, dlg_ref, dlb_ref, dg_ref):
        @pl.when(pl.program_id(0) == 0)
        def _():
            for r in (dsc_ref, dsh_ref, dlg_ref, dlb_ref, dg_ref):
                r[...] = jnp.zeros_like(r)

        dh = dh_ref[...]
        xh1, rstd1 = _ln(x1_ref[...])
        dsc_ref[...] += _colsum(dh * xh1)
        dsh_ref[...] += _colsum(dh)
        dx1 = alpha * dr2_ref[...] + _ln_bwd(dh * (1.0 + sc_ref[...]), xh1, rstd1)
        ov = o_ref[...]
        xhr, rstdr = _ln(alpha * x_ref[...] + g_ref[...] * ov)
        dlg_ref[...] += _colsum(dx1 * xhr)
        dlb_ref[...] += _colsum(dx1)
        dr1 = _ln_bwd(dx1 * lng_ref[...], xhr, rstdr)
        dg_ref[...] += _colsum(dr1 * ov)
        dr1_ref[...] = dr1
        do_ref[...] = (g_ref[...] * dr1).astype(bf16)

    blk = pl.BlockSpec((tb, d), lambda i: (i, 0))
    vec = pl.BlockSpec((1, d), _row)
    vshape = jax.ShapeDtypeStruct((1, d), f32)
    return pl.pallas_call(
        body, name=name, grid=(s // tb,), in_specs=[blk] * 5 + [vec] * 3, out_specs=[blk, blk] + [vec] * 5,
        out_shape=[jax.ShapeDtypeStruct((s, d), f32), jax.ShapeDtypeStruct((s, d), bf16)] + [vshape] * 5,
        compiler_params=_params("arbitrary"))(dh2, x1, dr2, x, o, g, sc, lng)


def _final_bwd(dh1, x, dr1, sc, alpha, *, name, xchg=None):
    s, d = x.shape
    tb = _tile(s, 512)

    def body(dh_ref, x_ref, dr1_ref, sc_ref, dx_ref, dsc_ref, dsh_ref):
        @pl.when(pl.program_id(0) == 0)
        def _():
            dsc_ref[...] = jnp.zeros_like(dsc_ref)
            dsh_ref[...] = jnp.zeros_like(dsh_ref)

        dh = dh_ref[...]
        xh, rstd = _ln(x_ref[...])
        dsc_ref[...] += _colsum(dh * xh)
        dsh_ref[...] += _colsum(dh)
        dx_ref[...] = alpha * dr1_ref[...] + _ln_bwd(dh * (1.0 + sc_ref[...]), xh, rstd)

    blk = pl.BlockSpec((tb, d), lambda i: (i, 0))
    vec = pl.BlockSpec((1, d), _row)
    vshape = jax.ShapeDtypeStruct((1, d), f32)
    res, got = _call(body, (dh1, x, dr1, sc), name=name, grid=(s // tb,), in_specs=[blk, blk, blk, vec], out_specs=[blk, vec, vec],
                     out_shape=[jax.ShapeDtypeStruct((s, d), f32), vshape, vshape], sem=("arbitrary",), xchg=xchg)
    return tuple(res) + (got,)


def _ffn_specs(s, fh, tb, tc):
    per = tb // FFN_HALO
    blk = pl.BlockSpec((tb, tc), lambda j, i: (i, j))
    prev = pl.BlockSpec((FFN_HALO, tc), lambda j, i: (jnp.maximum(i * per - 1, 0), j))
    nxt = pl.BlockSpec((FFN_HALO, tc), lambda j, i: (jnp.minimum((i + 1) * per, s // FFN_HALO - 1), j))
    taps = pl.BlockSpec((FFN_HALO, tc), lambda j, i: (0, j))
    vec = pl.BlockSpec((1, tc), lambda j, i: (0, j))
    return blk, prev, nxt, taps, vec


def _ffn_mid(upa, upv, wa, wv, ba, bv, *, name, xchg=None):
    s, fh = upa.shape
    tb, tc = _tile(s, 512), _tile(fh, 256)
    blk, prev, _, taps, vec = _ffn_specs(s, fh, tb, tc)
    off = FFN_HALO - FFN_K + 1

    def body(a_ref, v_ref, ah_ref, vh_ref, wa_ref, wv_ref, ba_ref, bv_ref, o_ref):
        first = pl.program_id(1) == 0
        halo_a, halo_v = jnp.where(first, 0.0, ah_ref[...]), jnp.where(first, 0.0, vh_ref[...])
        wa = [wa_ref[pl.ds(k, 1), :] for k in range(FFN_K)]
        wv = [wv_ref[pl.ds(k, 1), :] for k in range(FFN_K)]
        ba, bv = ba_ref[...], bv_ref[...]

        def chunk(r0, carry, head=False):
            a2 = _taps3(_window_before(a_ref, halo_a, r0, ROWS, head), wa, off, ROWS) + ba
            v2 = _taps3(_window_before(v_ref, halo_v, r0, ROWS, head), wv, off, ROWS) + bv
            o_ref[pl.ds(r0, ROWS), :] = (_gelu(a2) * v2).astype(bf16)
            return carry

        chunk(0, 0, head=True)
        _row_chunks(tb, ROWS, chunk, 0, start=1)

    (f,), got = _call(
        body, (upa, upv, upa, upv, wa, wv, ba, bv), name=name, grid=(fh // tc, s // tb),
        in_specs=[blk, blk, prev, prev, taps, taps, vec, vec], out_specs=[blk], out_shape=[jax.ShapeDtypeStruct((s, fh), bf16)],
        sem=("parallel", "arbitrary"), xchg=xchg)
    return f, got


def _ffn_mid_bwd(df, upa, upv, wa, wv, ba, bv, *, name, xchg=None):
    s, fh = upa.shape
    tb, tc = _tile(s, 512), _tile(fh, 256)
    nb = s // tb
    blk, prev, nxt, taps, vec = _ffn_specs(s, fh, tb, tc)
    off = FFN_HALO - FFN_K + 1
    te = tb + FFN_HALO

    def body(df_ref, dfn_ref, a_ref, v_ref, ah_ref, vh_ref, an_ref, vn_ref, wa_ref, wv_ref, ba_ref, bv_ref,
             da_ref, dv_ref, dwa_ref, dwv_ref, dba_ref, dbv_ref, dexta, dextv):
        i = pl.program_id(1)

        @pl.when(i == 0)
        def _():
            for r in (dwa_ref, dwv_ref, dba_ref, dbv_ref):
                r[...] = jnp.zeros_like(r)

        last = i == nb - 1
        halo_a, halo_v = jnp.where(i == 0, 0.0, ah_ref[...]), jnp.where(i == 0, 0.0, vh_ref[...])
        wa = [wa_ref[pl.ds(k, 1), :] for k in range(FFN_K)]
        wv = [wv_ref[pl.ds(k, 1), :] for k in range(FFN_K)]
        ba, bv = ba_ref[...], bv_ref[...]

        def conv_cotangents(r0, rows, xa, xv, dfe):
            sa = [_rows_from(xa, off + k, rows) for k in range(FFN_K)]
            sv = [_rows_from(xv, off + k, rows) for k in range(FFN_K)]
            a2 = sa[0] * wa[0] + sa[1] * wa[1] + sa[2] * wa[2] + ba
            v2 = sv[0] * wv[0] + sv[1] * wv[1] + sv[2] * wv[2] + bv
            cdf = 0.5 * (1.0 + lax.erf(a2 * INV_SQRT2))
            da2 = dfe * v2 * (cdf + a2 * jnp.exp(-0.5 * a2 * a2) * INV_SQRT_2PI)
            dv2 = dfe * (a2 * cdf)
            dexta[pl.ds(r0, rows), :] = da2
            dextv[pl.ds(r0, rows), :] = dv2
            return da2, dv2, sa, sv

        def chunk(r0, sums, head=False):
            da2, dv2, sa, sv = conv_cotangents(r0, ROWS, _window_before(a_ref, halo_a, r0, ROWS, head),
                                               _window_before(v_ref, halo_v, r0, ROWS, head), df_ref[pl.ds(r0, ROWS), :])
            new = [sums[k] + _fold8(da2 * sa[k]) for k in range(FFN_K)] + [sums[FFN_K] + _fold8(da2)]
            new += [sums[FFN_K + 1 + k] + _fold8(dv2 * sv[k]) for k in range(FFN_K)] + [sums[2 * FFN_K + 1] + _fold8(dv2)]
            return tuple(new)

        sums = chunk(0, tuple(jnp.zeros((SUBLANES, tc), f32) for _ in range(2 * FFN_K + 2)), head=True)
        sums = _row_chunks(tb, ROWS, chunk, sums, start=1)
        conv_cotangents(tb, FFN_HALO,
                        jnp.concatenate([a_ref[pl.ds(tb - FFN_HALO, FFN_HALO), :], jnp.where(last, 0.0, an_ref[...])], axis=0),
                        jnp.concatenate([v_ref[pl.ds(tb - FFN_HALO, FFN_HALO), :], jnp.where(last, 0.0, vn_ref[...])], axis=0),
                        jnp.where(last, 0.0, dfn_ref[...]))
        for k in range(FFN_K):
            dwa_ref[pl.ds(k, 1), :] += _colsum(sums[k])
            dwv_ref[pl.ds(k, 1), :] += _colsum(sums[FFN_K + 1 + k])
        dba_ref[...] += _colsum(sums[FFN_K])
        dbv_ref[...] += _colsum(sums[2 * FFN_K + 1])

        def back(r0, carry):
            for dext, w, o_ref in ((dexta, wa, da_ref), (dextv, wv, dv_ref)):
                dd = dext[pl.ds(r0, ROWS + FFN_HALO), :]
                o_ref[pl.ds(r0, ROWS), :] = (_rows_from(dd, 2, ROWS) * w[0] + _rows_from(dd, 1, ROWS) * w[1]
                                             + dd[0:ROWS] * w[2]).astype(bf16)
            return carry

        _row_chunks(tb, ROWS, back, 0)

    act = jax.ShapeDtypeStruct((s, fh), bf16)
    wshape = jax.ShapeDtypeStruct((FFN_HALO, fh), f32)
    vshape = jax.ShapeDtypeStruct((1, fh), f32)
    res, got = _call(
        body, (df, df, upa, upv, upa, upv, upa, upv, wa, wv, ba, bv), name=name, grid=(fh // tc, nb),
        in_specs=[blk, nxt, blk, blk, prev, prev, nxt, nxt, taps, taps, vec, vec],
        out_specs=[blk, blk, taps, taps, vec, vec], out_shape=[act, act, wshape, wshape, vshape, vshape],
        scratch_shapes=[pltpu.VMEM((te, tc), f32)] * 2, sem=("parallel", "arbitrary"), xchg=xchg)
    return tuple(res) + (got,)


def _cols_from_shards(stacked):
    _, k, n = stacked.shape
    return stacked.transpose(1, 0, 2).reshape(k, NDEV * n)


def _pad_rows(w, rows):
    return jnp.pad(w, ((0, rows - w.shape[0]), (0, 0)))


def kernel(x, c, w_cond, b_cond, w_in, b_in, ssm_lambda_re, ssm_lambda_im, ssm_log_dt, ssm_b_re, ssm_b_im, ssm_c_re, ssm_c_im, ssm_d, ssm_glu_w_a, ssm_glu_w_b, cv_dw_w, cv_dw_b, cv_ln_g, cv_ln_b, cv_w_pw, w_out, ln1_g, ln1_b, ffn_w_up, ffn_dw_w, ffn_dw_b, ffn_w_down, ln2_g, ln2_b, loss_target, m_w_cond, m_b_cond, m_w_in, m_b_in, m_ssm_lambda_re, m_ssm_lambda_im, m_ssm_log_dt, m_ssm_b_re, m_ssm_b_im, m_ssm_c_re, m_ssm_c_im, m_ssm_d, m_ssm_glu_w_a, m_ssm_glu_w_b, m_cv_dw_w, m_cv_dw_b, m_cv_ln_g, m_cv_ln_b, m_cv_w_pw, m_w_out, m_ln1_g, m_ln1_b, m_ffn_w_up, m_ffn_dw_w, m_ffn_dw_b, m_ffn_w_down, m_ln2_g, m_ln2_b, v_w_cond, v_b_cond, v_w_in, v_b_in, v_ssm_lambda_re, v_ssm_lambda_im, v_ssm_log_dt, v_ssm_b_re, v_ssm_b_im, v_ssm_c_re, v_ssm_c_im, v_ssm_d, v_ssm_glu_w_a, v_ssm_glu_w_b, v_cv_dw_w, v_cv_dw_b, v_cv_ln_g, v_cv_ln_b, v_cv_w_pw, v_w_out, v_ln1_g, v_ln1_b, v_ffn_w_up, v_ffn_dw_w, v_ffn_dw_b, v_ffn_w_down, v_ln2_g, v_ln2_b):
    weights = dict(w_cond=w_cond, b_cond=b_cond, w_in=w_in, b_in=b_in, ssm_lambda_re=ssm_lambda_re, ssm_lambda_im=ssm_lambda_im, ssm_log_dt=ssm_log_dt, ssm_b_re=ssm_b_re, ssm_b_im=ssm_b_im, ssm_c_re=ssm_c_re, ssm_c_im=ssm_c_im, ssm_d=ssm_d, ssm_glu_w_a=ssm_glu_w_a, ssm_glu_w_b=ssm_glu_w_b, cv_dw_w=cv_dw_w, cv_dw_b=cv_dw_b, cv_ln_g=cv_ln_g, cv_ln_b=cv_ln_b, cv_w_pw=cv_w_pw, w_out=w_out, ln1_g=ln1_g, ln1_b=ln1_b, ffn_w_up=ffn_w_up, ffn_dw_w=ffn_dw_w, ffn_dw_b=ffn_dw_b, ffn_w_down=ffn_w_down, ln2_g=ln2_g, ln2_b=ln2_b)
    mom_m = dict(w_cond=m_w_cond, b_cond=m_b_cond, w_in=m_w_in, b_in=m_b_in, ssm_lambda_re=m_ssm_lambda_re, ssm_lambda_im=m_ssm_lambda_im, ssm_log_dt=m_ssm_log_dt, ssm_b_re=m_ssm_b_re, ssm_b_im=m_ssm_b_im, ssm_c_re=m_ssm_c_re, ssm_c_im=m_ssm_c_im, ssm_d=m_ssm_d, ssm_glu_w_a=m_ssm_glu_w_a, ssm_glu_w_b=m_ssm_glu_w_b, cv_dw_w=m_cv_dw_w, cv_dw_b=m_cv_dw_b, cv_ln_g=m_cv_ln_g, cv_ln_b=m_cv_ln_b, cv_w_pw=m_cv_w_pw, w_out=m_w_out, ln1_g=m_ln1_g, ln1_b=m_ln1_b, ffn_w_up=m_ffn_w_up, ffn_dw_w=m_ffn_dw_w, ffn_dw_b=m_ffn_dw_b, ffn_w_down=m_ffn_w_down, ln2_g=m_ln2_g, ln2_b=m_ln2_b)
    mom_v = dict(w_cond=v_w_cond, b_cond=v_b_cond, w_in=v_w_in, b_in=v_b_in, ssm_lambda_re=v_ssm_lambda_re, ssm_lambda_im=v_ssm_lambda_im, ssm_log_dt=v_ssm_log_dt, ssm_b_re=v_ssm_b_re, ssm_b_im=v_ssm_b_im, ssm_c_re=v_ssm_c_re, ssm_c_im=v_ssm_c_im, ssm_d=v_ssm_d, ssm_glu_w_a=v_ssm_glu_w_a, ssm_glu_w_b=v_ssm_glu_w_b, cv_dw_w=v_cv_dw_w, cv_dw_b=v_cv_dw_b, cv_ln_g=v_cv_ln_g, cv_ln_b=v_cv_ln_b, cv_w_pw=v_cv_w_pw, w_out=v_w_out, ln1_g=v_ln1_g, ln1_b=v_ln1_b, ffn_w_up=v_ffn_w_up, ffn_dw_w=v_ffn_dw_w, ffn_dw_b=v_ffn_dw_b, ffn_w_down=v_ffn_w_down, ln2_g=v_ln2_g, ln2_b=v_ln2_b)
    names = list(weights)

    s, d = x.shape[1], x.shape[2]
    sw = cw = d // 2
    fh = ffn_w_down.shape[1] * NDEV
    ng, nq = sw // SSM_GROUP, sw // QW
    gq = ng // nq
    alpha = 2.0 ** 0.25
    me = 4 * lax.axis_index("x") + 2 * lax.axis_index("y") + lax.axis_index("c")
    xs, tgt = x[0], loss_target[0]

    col_names = ["w_in", "ssm_glu_w_a", "ssm_glu_w_b", "cv_w_pw", "ffn_w_up"]
    row_names = ["w_out", "ffn_w_down"]
    big = col_names + row_names
    sent = lambda ns: [weights[n][0].astype(bf16) for n in ns]
    got_in, got_c, got_cv_taps, got_ffn_taps = _exchange(sent(["w_in"]) + [c, cv_dw_w[0, :, 0], ffn_dw_w[0, :, 0]],
                                                         scatter=False, name="gather_in")
    o1, o2, o3, o4 = sw, sw + cw, sw + 2 * cw, sw + 2 * cw + d
    in_bounds = ((0, o1), (o1, o2), (o2, o3), (o3, o4), (o4, o4 + d))
    w_u, w_cva, w_cvg, w_gs, w_gc = _unshard_cols(got_in, in_bounds, name="unshard_w_in")
    b_u, b_cva, b_cvg, b_gs, b_gc = (b_in[:, a:b] for a, b in in_bounds)
    c_all = got_c.reshape(NDEV, d)
    cv_taps = _cols_from_shards(got_cv_taps)
    ffn_taps = _cols_from_shards(got_ffn_taps)
    cv_w32 = _pad_rows(cv_taps, CONV_HALO)
    ffn_wa, ffn_wv = _pad_rows(ffn_taps[:, :fh], FFN_HALO), _pad_rows(ffn_taps[:, fh:], FFN_HALO)
    ffn_ba, ffn_bv = ffn_dw_b[:, :fh], ffn_dw_b[:, fh:]

    ncond = w_cond.shape[2]
    b_cond_mine = lax.dynamic_slice(b_cond, (0, me * ncond), (1, ncond))
    mod_cols = _cond_fwd(c_all, w_cond[0], b_cond_mine, name="cond_fwd")
    mod_all, = _exchange([mod_cols], scatter=False, name="gather_mod")
    mod_mine = lax.dynamic_slice(mod_all, (0, me, 0), (NDEV, 1, ncond)).reshape(1, 6 * d)
    sh1, sc1, g1, sh2, sc2, g2 = (mod_mine[:, k * d:(k + 1) * d] for k in range(6))

    lam_re, lam_im, log_dt = ssm_lambda_re[0], ssm_lambda_im[0], ssm_log_dt[0][:, None]
    lbr, lbi, cfr, cfi = _ssm_prep(lam_re, lam_im, log_dt, name="ssm_prep")
    rows_q = lambda a: a.reshape(nq, 1, QS)
    eye = jnp.eye(gq, dtype=f32)

    def b_mat(b):
        bt = b.reshape(nq, gq, SSM_STATE, SSM_GROUP).transpose(0, 1, 3, 2)
        return jnp.einsum("qgpn,gh->qgphn", bt, eye).reshape(nq, QW, QS)

    def c_mat(cc):
        ct = cc.reshape(nq, gq, SSM_GROUP, SSM_STATE)
        return jnp.einsum("qgpn,gh->qhngp", ct, eye).reshape(nq, QS, QW)

    def b_unmat(mt):
        return jnp.einsum("qgpgn->qgnp", mt.reshape(nq, gq, SSM_GROUP, gq, SSM_STATE)).reshape(ng, SSM_STATE, SSM_GROUP)

    def c_unmat(mt):
        return jnp.einsum("qgngp->qgpn", mt.reshape(nq, gq, SSM_STATE, gq, SSM_GROUP)).reshape(ng, SSM_GROUP, SSM_STATE)

    ssm_args = (rows_q(lbr), rows_q(lbi), b_mat(ssm_b_re[0]), b_mat(ssm_b_im[0]), c_mat(ssm_c_re[0]), c_mat(ssm_c_im[0]),
                rows_q(cfr), rows_q(cfi), ssm_d[0].reshape(1, sw))

    h1 = _lnmod(xs, sc1, sh1, name="ln_mod1")
    u = _mm([(h1, w_u)], b_u, name="in_u")
    cva = _mm([(h1, w_cva)], b_cva, name="in_cva")
    cvg = _mm([(h1, w_cvg)], b_cvg, name="in_cvg")
    gs = _mm([(h1, w_gs)], b_gs, out_dtype=bf16, name="in_gs")
    gc = _mm([(h1, w_gc)], b_gc, out_dtype=bf16, name="in_gc")
    v4, (got_a, got_b, got_pw, got_o) = _conv_fwd(
        cva, cvg, cv_w32, cv_dw_b, cv_ln_g, cv_ln_b, name="conv_fwd",
        xchg=(sent(["ssm_glu_w_a", "ssm_glu_w_b", "cv_w_pw", "w_out"]), False))
    h_p, yraw3, y, (got_up,) = _ssm_fwd(u, *ssm_args, name="ssm_fwd", xchg=(sent(["ffn_w_up"]), False))
    w_a, = _unshard_cols(got_a, ((0, d),), name="unshard_glu_a")
    w_b, = _unshard_cols(got_b, ((0, d),), name="unshard_glu_b")
    w_pw, = _unshard_cols(got_pw, ((0, d),), name="unshard_conv_pw")
    w_upa, w_upv = _unshard_cols(got_up, ((0, fh), (fh, 2 * fh)), name="unshard_ffn_up")
    w_o = got_o.reshape(d, d)
    ya = _mm([(y, w_a)], out_dtype=bf16, name="glu_a")
    yb = _mm([(y, w_b)], out_dtype=bf16, name="glu_b")
    ycv = _mm([(v4, w_pw)], out_dtype=bf16, name="conv_pw")
    merged = _glu_merge(ya, yb, ycv, gs, gc, name="merge")
    o = _mm([(merged, w_o)], name="out_proj")
    x1, h2 = _resid_ln_mod(xs, o, g1, ln1_g, ln1_b, sc2, sh2, alpha, name="resid_ln1")
    upa = _mm([(h2, w_upa)], name="ffn_up_a")
    upv = _mm([(h2, w_upv)], name="ffn_up_v")
    f, (got_dn,) = _ffn_mid(upa, upv, ffn_wa, ffn_wv, ffn_ba, ffn_bv, name="ffn_mid", xchg=(sent(["ffn_w_down"]), False))
    w_dn = got_dn.reshape(fh, d)
    y2 = _mm([(f, w_dn)], name="ffn_down")
    dr2, dy2, loss_part, d_ln2_g, d_ln2_b, d_g2 = _resid_ln_loss(x1, y2, g2, ln2_g, ln2_b, tgt, alpha, name="resid_ln2_loss")

    gw = {}
    df = _mm([(dy2, w_dn)], trans_w=True, name="d_ffn_down")
    gw["ffn_w_down"] = _mm_tn(f, dy2, out_dtype=bf16, name="g_ffn_down").reshape((NDEV,) + ffn_w_down[0].shape)
    received = {}
    dupa, dupv, d_ffn_wa, d_ffn_wv, d_ffn_ba, d_ffn_bv, (received["ffn_w_down"],) = _ffn_mid_bwd(
        df, upa, upv, ffn_wa, ffn_wv, ffn_ba, ffn_bv, name="ffn_mid_bwd", xchg=([gw["ffn_w_down"]], True))
    dh2 = _mm([(dupa, w_upa), (dupv, w_upv)], trans_w=True, name="d_ffn_up")
    gw["ffn_w_up"] = _shard_cols([_mm_tn(h2, dupa, name="g_ffn_up_a"), _mm_tn(h2, dupv, name="g_ffn_up_v")], out_dtype=bf16,
                                 name="shard_ffn_up")
    dr1, do, d_sc2, d_sh2, d_ln1_g, d_ln1_b, d_g1 = _mid_bwd(dh2, x1, dr2, xs, o, g1, sc2, ln1_g, alpha, name="mid_bwd")
    dmerged = _mm([(do, w_o)], trans_w=True, out_dtype=bf16, name="d_out_proj")
    gw["w_out"] = _mm_tn(merged, do, out_dtype=bf16, name="g_out_proj").reshape((NDEV,) + w_out[0].shape)
    dya, dyb, dycv, dgs, dgc, s_gs, s_gc = _glu_merge_bwd(dmerged, ya, yb, ycv, gs, gc, name="merge_bwd")
    dy = _mm([(dya, w_a), (dyb, w_b)], trans_w=True, name="d_glu")
    gw["ssm_glu_w_a"] = _shard_cols([_mm_tn(y, dya, name="g_glu_a")], out_dtype=bf16, name="shard_glu_a")
    gw["ssm_glu_w_b"] = _shard_cols([_mm_tn(y, dyb, name="g_glu_b")], out_dtype=bf16, name="shard_glu_b")
    dv4 = _mm([(dycv, w_pw)], trans_w=True, name="d_conv_pw")
    gw["cv_w_pw"] = _shard_cols([_mm_tn(v4, dycv, name="g_conv_pw")], out_dtype=bf16, name="shard_conv_pw")
    dv2, d_cv_ln_g, d_cv_ln_b = _conv_bwd_ln(dv4, cva, cvg, cv_w32, cv_dw_b, cv_ln_g, cv_ln_b, name="conv_bwd_ln")
    dcva, dcvg, d_cv_w32, d_cv_b, s_cva, s_cvg, (received["ffn_w_up"],) = _conv_bwd_taps(
        dv2, cva, cvg, cv_w32, name="conv_bwd_taps", xchg=([gw["ffn_w_up"]], True))
    late = ["w_out", "ssm_glu_w_a", "ssm_glu_w_b", "cv_w_pw"]
    (du, d_bre_m, d_bim_m, d_cre_m, d_cim_m, d_cfr, d_cfi, d_lbr, d_lbi, d_d, s_u, got_late) = _ssm_bwd(
        dy, yraw3, u, h_p, *ssm_args, name="ssm_bwd", xchg=([gw[n] for n in late], True))
    received.update(zip(late, got_late))
    gshape = lam_re.shape
    d_lam_re, d_lam_im, d_log_dt = _ssm_prep_bwd(
        lam_re, lam_im, log_dt, [a.reshape(gshape) for a in (d_lbr, d_lbi, d_cfr, d_cfi)], name="ssm_prep_bwd")
    small = {
        "b_in": jnp.concatenate([s_u, s_cva, s_cvg, s_gs, s_gc], axis=1),
        "ssm_lambda_re": d_lam_re, "ssm_lambda_im": d_lam_im, "ssm_log_dt": d_log_dt,
        "ssm_b_re": b_unmat(d_bre_m), "ssm_b_im": b_unmat(d_bim_m), "ssm_c_re": c_unmat(d_cre_m), "ssm_c_im": c_unmat(d_cim_m),
        "ssm_d": d_d, "cv_dw_w": d_cv_w32[:CONV_K], "cv_dw_b": d_cv_b, "cv_ln_g": d_cv_ln_g, "cv_ln_b": d_cv_ln_b,
        "ln1_g": d_ln1_g, "ln1_b": d_ln1_b,
        "ffn_dw_w": jnp.concatenate([d_ffn_wa[:FFN_K], d_ffn_wv[:FFN_K]], axis=1),
        "ffn_dw_b": jnp.concatenate([d_ffn_ba, d_ffn_bv], axis=1), "ln2_g": d_ln2_g, "ln2_b": d_ln2_b,
        "mod_g1": d_g1, "mod_sh2": d_sh2, "mod_sc2": d_sc2, "mod_g2": d_g2, "loss": loss_part,
    }
    small_names = list(small)
    small_shapes = [small[n].shape for n in small_names]
    gw["w_in"], (small_all,) = _mm_tn_sharded(h1, [du, dcva, dcvg, dgs, dgc], out_dtype=bf16, name="g_in",
                                              xchg=([_pack([small[n] for n in small_names])], False))
    dh1, (received["w_in"],) = _mm(
        [(du, w_u), (dcva, w_cva), (dcvg, w_cvg), (dgs, w_gs), (dgc, w_gc)], trans_w=True, name="d_in",
        xchg=([gw["w_in"]], True))
    grad_x, d_sc1, d_sh1, _ = _final_bwd(dh1, xs, dr1, sc1, alpha, name="final_bwd")

    grads, delta, new_m, new_v = {}, {}, {}, {}
    for n in big:
        ride = ([_pack([d_sh1, d_sc1])], False) if n == big[0] else None
        res = _sum_adamw(received[n], weights[n][0], mom_m[n][0], mom_v[n][0], name="adamw_" + n, xchg=ride)
        grads[n], delta[n], new_m[n], new_v[n] = res[:4]
        if ride is not None:
            last_all, = res[4]

    small_sum = dict(zip(small_names, _unpack(_sum_parts(small_all, name="sum_small").reshape(-1), small_shapes)))
    last_sum = _unpack(_sum_parts(last_all, name="sum_last").reshape(-1), [(1, d), (1, d)])
    per_dev = dict(zip(small_names, _unpack(small_all.reshape(NDEV, -1), small_shapes)))
    last_dev = _unpack(last_all.reshape(NDEV, -1), [(1, d), (1, d)])
    dmod_all = jnp.concatenate(last_dev + [per_dev[k] for k in ("mod_g1", "mod_sh2", "mod_sc2", "mod_g2")], axis=-1).reshape(NDEV, 6 * d)
    dmod_cols = lax.dynamic_slice(dmod_all.reshape(NDEV, NDEV, ncond), (0, me, 0), (NDEV, 1, ncond)).reshape(NDEV, ncond)
    grads["w_cond"] = _cond_bwd(c_all, dmod_cols, name="cond_bwd")
    loss = small_sum.pop("loss").reshape(())
    grads["b_cond"] = jnp.concatenate(last_sum + [small_sum.pop(k) for k in ("mod_g1", "mod_sh2", "mod_sc2", "mod_g2")], axis=1)
    for n, g in small_sum.items():
        grads[n] = g
    ntap = cv_dw_w.shape[3]
    grads["cv_dw_w"] = lax.dynamic_slice(grads["cv_dw_w"], (0, me * ntap), (CONV_K, ntap))
    nffn = ffn_dw_w.shape[3]
    grads["ffn_dw_w"] = lax.dynamic_slice(grads["ffn_dw_w"], (0, me * nffn), (FFN_K, nffn))
    grads = {n: grads[n].reshape(weights[n].shape) for n in names}

    delta["w_cond"], new_m["w_cond"], new_v["w_cond"] = _adamw(w_cond[0], grads["w_cond"][0], m_w_cond[0], v_w_cond[0],
                                                               name="adamw_w_cond")
    rest = [n for n in names if n not in ["w_cond"] + big]
    rest_shapes = [weights[n].shape for n in rest]
    packed = [_pack([t[n] for n in rest]) for t in (weights, grads, mom_m, mom_v)]
    for tgt_dict, res in zip((delta, new_m, new_v), _adamw(*packed, name="adamw_small")):
        for n, a in zip(rest, _unpack(res.reshape(-1), rest_shapes)):
            tgt_dict[n] = a
    shaped = lambda t: [t[n].reshape(weights[n].shape) for n in names]

    return (loss, grad_x[None], *shaped(grads), *shaped(delta), *shaped(new_m), *shaped(new_v))
```

```python
import functools
import math

import jax
import jax.numpy as jnp
from jax import lax
from jax.experimental import pallas as pl
from jax.experimental.pallas import tpu as pltpu

f32 = jnp.float32
bf16 = jnp.bfloat16

NDEV = 8
LANES = 128
SUBLANES = 8
SSM_GROUP = 16
SSM_STATE = 64
QW = 128
QS = 512
CONV_K = 31
CONV_HALO = 32
FFN_K = 3
FFN_HALO = 8
LN_EPS = 1e-5
ADAM_LR, ADAM_B1, ADAM_B2, ADAM_EPS, ADAM_WD, ADAM_STEP = 0.001, 0.9, 0.999, 1e-08, 0.01, 10
VMEM_LIMIT = 56 * 1024 * 1024
W_TILE_BYTES = 6 * 1024 * 1024
SUM_ROWS = 512
EW_BLOCK_BYTES = 2 * 1024 * 1024
INV_SQRT2 = 1.0 / math.sqrt(2.0)
INV_SQRT_2PI = 1.0 / math.sqrt(2.0 * math.pi)
MESH = pl.DeviceIdType.MESH


def _tile(n, want):
    t = min(n, want)
    while n % t:
        t //= 2
    return t


def _col_tile(n, rows, bytes_per):
    best = LANES if n % LANES == 0 else n
    for t in range(LANES, n + 1, LANES):
        if n % t == 0 and rows * t * bytes_per <= W_TILE_BYTES:
            best = t
    return best


def _params(*sem):
    return pltpu.CompilerParams(dimension_semantics=sem, vmem_limit_bytes=VMEM_LIMIT)


def _row(i):
    return (0, 0)


def _full(shape):
    nd = len(shape)
    return pl.BlockSpec(shape, lambda *a: (0,) * nd)


def _ln(x):
    mu = jnp.mean(x, axis=-1, keepdims=True)
    xc = x - mu
    var = jnp.mean(xc * xc, axis=-1, keepdims=True)
    rstd = lax.rsqrt(var + LN_EPS)
    return xc * rstd, rstd


def _ln_bwd(dxhat, xhat, rstd):
    return rstd * (dxhat - jnp.mean(dxhat, axis=-1, keepdims=True) - xhat * jnp.mean(dxhat * xhat, axis=-1, keepdims=True))


def _sig(x):
    return 1.0 / (1.0 + jnp.exp(-x))


def _gelu(x):
    return 0.5 * x * (1.0 + lax.erf(x * INV_SQRT2))


def _gelu_grad(x):
    return 0.5 * (1.0 + lax.erf(x * INV_SQRT2)) + x * jnp.exp(-0.5 * x * x) * INV_SQRT_2PI


def _colsum(x):
    return jnp.sum(x, axis=0, keepdims=True)


def _mm(pairs, bias=None, *, trans_w=False, out_dtype=f32, name, xchg=None):
    n_p = len(pairs)
    m = pairs[0][0].shape[0]
    n = pairs[0][1].shape[0 if trans_w else 1]
    ktot = sum(x.shape[1] for x, _ in pairs)
    tm = _tile(m, 512)
    tn = _col_tile(n, ktot, 2)
    dn = (((1,), (1,)), ((), ())) if trans_w else (((1,), (0,)), ((), ()))

    def body(*refs):
        o_ref = refs[-1]
        acc = None
        for xr, wr in zip(refs[:n_p], refs[n_p:2 * n_p]):
            r = lax.dot_general(xr[...].astype(bf16), wr[...].astype(bf16), dn, preferred_element_type=f32)
            acc = r if acc is None else acc + r
        if bias is not None:
            acc = acc + refs[2 * n_p][...]
        o_ref[...] = acc.astype(out_dtype)

    in_specs = [pl.BlockSpec((tm, x.shape[1]), lambda j, i: (i, 0)) for x, _ in pairs]
    if trans_w:
        in_specs += [pl.BlockSpec((tn, w.shape[1]), lambda j, i: (j, 0)) for _, w in pairs]
    else:
        in_specs += [pl.BlockSpec((w.shape[0], tn), lambda j, i: (0, j)) for _, w in pairs]
    args = [x for x, _ in pairs] + [w for _, w in pairs]
    if bias is not None:
        in_specs.append(pl.BlockSpec((1, tn), lambda j, i: (0, j)))
        args.append(bias)
    (out,), got = _call(
        body, args, name=name, grid=(n // tn, m // tm), in_specs=in_specs,
        out_specs=[pl.BlockSpec((tm, tn), lambda j, i: (i, j))], out_shape=[jax.ShapeDtypeStruct((m, n), out_dtype)],
        sem=("parallel", "arbitrary"), xchg=xchg)
    return out if xchg is None else (out, got)


def _mm_fanout(x, ws, biases, out_dtypes, *, name):
    m, k = x.shape
    tm = _tile(m, 512)
    n_w = len(ws)

    def body(x_ref, *refs):
        xb = x_ref[...].astype(bf16)
        for w_ref, b_ref, o_ref, dt in zip(refs[:n_w], refs[n_w:2 * n_w], refs[2 * n_w:], out_dtypes):
            o_ref[...] = (jnp.dot(xb, w_ref[...], preferred_element_type=f32) + b_ref[...]).astype(dt)

    return pl.pallas_call(
        body, name=name, grid=(m // tm,),
        in_specs=[pl.BlockSpec((tm, k), lambda i: (i, 0))] + [_full(w.shape) for w in ws] + [_full(b.shape) for b in biases],
        out_specs=[pl.BlockSpec((tm, w.shape[1]), lambda i: (i, 0)) for w in ws],
        out_shape=[jax.ShapeDtypeStruct((m, w.shape[1]), dt) for w, dt in zip(ws, out_dtypes)],
        compiler_params=_params("parallel"))(x, *ws, *biases)


def _mm_tn(x, dy, *, out_dtype=f32, name):
    m, k = x.shape
    n = dy.shape[1]
    tm = _tile(m, 512)
    tn = _col_tile(n, k, 4)
    steps = m // tm

    def body(x_ref, dy_ref, o_ref, *scratch):
        acc = scratch[0] if scratch else o_ref

        @pl.when(pl.program_id(1) == 0)
        def _():
            acc[...] = jnp.zeros_like(acc)

        acc[...] += lax.dot_general(x_ref[...].astype(bf16), dy_ref[...].astype(bf16), (((0,), (0,)), ((), ())),
                                    preferred_element_type=f32)
        if scratch:
            @pl.when(pl.program_id(1) == steps - 1)
            def _():
                o_ref[...] = acc[...].astype(out_dtype)

    return pl.pallas_call(
        body, name=name, grid=(n // tn, steps),
        in_specs=[pl.BlockSpec((tm, k), lambda j, i: (i, 0)), pl.BlockSpec((tm, tn), lambda j, i: (i, j))],
        out_specs=pl.BlockSpec((k, tn), lambda j, i: (0, j)),
        out_shape=jax.ShapeDtypeStruct((k, n), out_dtype),
        scratch_shapes=[] if out_dtype == f32 else [pltpu.VMEM((k, tn), f32)],
        compiler_params=_params("parallel", "arbitrary"),
    )(x, dy)


def _mm_tn_sharded(x, dys, *, out_dtype, name, xchg=None):
    m, k = x.shape
    widths = [dy.shape[1] for dy in dys]
    n = sum(widths) // NDEV
    tm = _tile(m, 512)
    steps = m // tm
    n_d = len(dys)

    def body(x_ref, *refs):
        dy_refs, o_ref, acc = refs[:n_d], refs[n_d], refs[n_d + 1]
        i = pl.program_id(0)

        @pl.when(i == 0)
        def _():
            acc[...] = jnp.zeros_like(acc)

        xb = x_ref[...].astype(bf16)
        off = 0
        for dy_ref, w in zip(dy_refs, widths):
            acc[:, off:off + w] += lax.dot_general(xb, dy_ref[...].astype(bf16), (((0,), (0,)), ((), ())), preferred_element_type=f32)
            off += w

        @pl.when(i == steps - 1)
        def _():
            for j in range(NDEV):
                o_ref[j] = acc[:, n * j:n * (j + 1)].astype(out_dtype)

    (out,), got = _call(
        body, (x, *dys), name=name, grid=(steps,),
        in_specs=[pl.BlockSpec((tm, k), lambda i: (i, 0))] + [pl.BlockSpec((tm, w), lambda i: (i, 0)) for w in widths],
        out_specs=[pl.BlockSpec((NDEV, k, n), lambda i: (0, 0, 0))], out_shape=[jax.ShapeDtypeStruct((NDEV, k, n), out_dtype)],
        scratch_shapes=[pltpu.VMEM((k, sum(widths)), f32)], sem=("arbitrary",), xchg=xchg)
    return out if xchg is None else (out, got)


def _exchange(arrs, *, scatter, name):
    n = len(arrs)

    def body(*refs):
        _exchange_copies(refs[:n], refs[n:2 * n], refs[2 * n:], scatter, True, True)

    return pl.pallas_call(
        body, name=name, in_specs=[HBM_SPEC] * n, out_specs=[HBM_SPEC] * n, out_shape=_exchange_out_shape(arrs, scatter),
        scratch_shapes=_exchange_sems(n),
    )(*arrs)


HBM_SPEC = pl.BlockSpec(memory_space=pltpu.HBM)


def _flags(scatter, n):
    return list(scatter) if isinstance(scatter, (list, tuple)) else [scatter] * n


def _exchange_out_shape(arrs, scatter):
    return [jax.ShapeDtypeStruct(a.shape if sc else (NDEV,) + a.shape, a.dtype) for a, sc in zip(arrs, _flags(scatter, len(arrs)))]


def _exchange_sems(n):
    return [pltpu.SemaphoreType.DMA(((NDEV - 1) * n,)), pltpu.SemaphoreType.DMA(((NDEV - 1) * n,)), pltpu.SemaphoreType.DMA((n,))]


def _exchange_copies(x_refs, o_refs, sems, scatter, start, wait):
    n = len(x_refs)
    flags = _flags(scatter, n)
    send_sems, recv_sems, local_sems = sems
    ix, iy, ic = lax.axis_index("x"), lax.axis_index("y"), lax.axis_index("c")
    me = 4 * ix + 2 * iy + ic
    local = [pltpu.make_async_copy(x.at[me] if sc else x, o.at[me], local_sems.at[a])
             for a, (x, o, sc) in enumerate(zip(x_refs, o_refs, flags))]

    def peer(k):
        return (1 - ix if k & 4 else ix, 1 - iy if k & 2 else iy, 1 - ic if k & 1 else ic)

    def index(p):
        return 4 * p[0] + 2 * p[1] + p[2]

    def copy(a, k, src, dst, to):
        sem = (k - 1) * n + a
        return pltpu.make_async_remote_copy(src_ref=src, dst_ref=dst, send_sem=send_sems.at[sem], recv_sem=recv_sems.at[sem],
                                            device_id=to, device_id_type=MESH)

    sends, arrivals, passed_on = [], [], []
    for a, (x, o, sc) in enumerate(zip(x_refs, o_refs, flags)):
        if sc:
            for k in range(1, NDEV):
                p = peer(k)
                sends.append(copy(a, k, x.at[index(p)], o.at[me], p))
                arrivals.append(copy(a, k, x.at[me], o.at[index(p)], p))
        else:
            sib = peer(1)
            sends.append(copy(a, 1, x, o.at[me], sib))
            arrivals.append(copy(a, 1, x, o.at[index(sib)], sib))
            for k in (2, 4, 6):
                p, q = peer(k), peer(k + 1)
                sends.append(copy(a, k, x, o.at[me], p))
                passed_on.append((copy(a, k, x, o.at[index(p)], p), copy(a, k + 1, o.at[index(p)], o.at[index(p)], sib)))
                arrivals.append(copy(a, k + 1, o.at[index(q)], o.at[index(q)], sib))
    if start:
        for cp in local + sends:
            cp.start()
    if wait:
        for landed, hand_over in passed_on:
            landed.wait_recv()
            hand_over.start()
        for cp in arrivals:
            cp.wait_recv()
        for cp in sends + [hand_over for _, hand_over in passed_on]:
            cp.wait_send()
        for cp in local:
            cp.wait()


def _call(body, args, *, name, grid, in_specs, out_specs, out_shape, scratch_shapes=(), sem, xchg=None):
    if xchg is None:
        return pl.pallas_call(body, name=name, grid=grid, in_specs=in_specs, out_specs=out_specs, out_shape=out_shape,
                              scratch_shapes=list(scratch_shapes), compiler_params=_params(*sem))(*args), None
    arrs, scatter = xchg
    n, ni, no, ns = len(arrs), len(in_specs), len(out_specs), len(scratch_shapes)

    def wrapped(*refs):
        ins, x_refs = refs[:ni], refs[ni:ni + n]
        outs, o_refs = refs[ni + n:ni + n + no], refs[ni + n + no:ni + 2 * n + no]
        scratch, sems = refs[ni + 2 * n + no:ni + 2 * n + no + ns], refs[ni + 2 * n + no + ns:]
        ids = [pl.program_id(a) for a in range(len(grid))]
        first = functools.reduce(jnp.logical_and, [p == 0 for p in ids])
        last = functools.reduce(jnp.logical_and, [p == g - 1 for p, g in zip(ids, grid)])

        @pl.when(first)
        def _():
            _exchange_copies(x_refs, o_refs, sems, scatter, True, False)

        body(*ins, *outs, *scratch)

        @pl.when(last)
        def _():
            _exchange_copies(x_refs, o_refs, sems, scatter, False, True)

    res = pl.pallas_call(
        wrapped, name=name, grid=grid, in_specs=list(in_specs) + [HBM_SPEC] * n, out_specs=list(out_specs) + [HBM_SPEC] * n,
        out_shape=list(out_shape) + _exchange_out_shape(arrs, scatter),
        scratch_shapes=list(scratch_shapes) + _exchange_sems(n),
        compiler_params=_params(*("arbitrary",) * len(grid)))(*args, *arrs)
    return res[:no], res[no:]


def _sum_parts(parts, *, name):
    r = parts.shape[1]

    def body(p_ref, o_ref):
        acc = p_ref[0]
        for j in range(1, NDEV):
            acc = acc + p_ref[j]
        o_ref[...] = acc

    return pl.pallas_call(body, name=name, out_shape=jax.ShapeDtypeStruct((r, LANES), f32), compiler_params=_params())(parts)


def _col_pieces(n, bounds):
    out = []
    for p, (a, b) in enumerate(bounds):
        for j in range(NDEV):
            lo, hi = max(a, n * j), min(b, n * (j + 1))
            if lo < hi:
                out.append((p, j, lo - a, lo - n * j, hi - lo))
    return out


def _unshard_cols(stacked, bounds, *, name):
    _, k, n = stacked.shape
    tk = _tile(k, 256)
    plan = _col_pieces(n, bounds)

    def body(x_ref, *o_refs):
        for p, j, po, so, w in plan:
            o_refs[p][:, po:po + w] = x_ref[j, :, so:so + w]

    return pl.pallas_call(
        body, name=name, grid=(k // tk,), in_specs=[pl.BlockSpec((NDEV, tk, n), lambda i: (0, i, 0))],
        out_specs=[pl.BlockSpec((tk, b - a), lambda i: (i, 0)) for a, b in bounds],
        out_shape=[jax.ShapeDtypeStruct((k, b - a), stacked.dtype) for a, b in bounds],
        compiler_params=_params("parallel"))(stacked)


def _shard_cols(pieces, *, out_dtype, name):
    k = pieces[0].shape[0]
    bounds, off = [], 0
    for p in pieces:
        bounds.append((off, off + p.shape[1]))
        off += p.shape[1]
    n = off // NDEV
    tk = _tile(k, 256)
    plan = _col_pieces(n, bounds)

    def body(*refs):
        o_ref = refs[-1]
        for p, j, po, so, w in plan:
            o_ref[j, :, so:so + w] = refs[p][:, po:po + w].astype(out_dtype)

    return pl.pallas_call(
        body, name=name, grid=(k // tk,), in_specs=[pl.BlockSpec((tk, b - a), lambda i: (i, 0)) for a, b in bounds],
        out_specs=pl.BlockSpec((NDEV, tk, n), lambda i: (0, i, 0)),
        out_shape=jax.ShapeDtypeStruct((NDEV, k, n), out_dtype),
        compiler_params=_params("parallel"))(*pieces)


def _pack(arrs):
    flat = jnp.concatenate([a.reshape(-1) for a in arrs])
    pad = (-flat.shape[0]) % (SUBLANES * LANES)
    return jnp.pad(flat, (0, pad)).reshape(-1, LANES)


def _unpack(flat, shapes):
    out, off = [], 0
    for s in shapes:
        n = math.prod(s)
        out.append(flat[..., off:off + n].reshape(flat.shape[:-1] + tuple(s)))
        off += n
    return out


def _adamw_math(w, gg, m, v):
    nm = ADAM_B1 * m + (1.0 - ADAM_B1) * gg
    nv = ADAM_B2 * v + (1.0 - ADAM_B2) * (gg * gg)
    m_hat = nm / (1.0 - ADAM_B1 ** ADAM_STEP)
    v_hat = nv / (1.0 - ADAM_B2 ** ADAM_STEP)
    return -ADAM_LR * (m_hat / (jnp.sqrt(v_hat) + ADAM_EPS) + ADAM_WD * w), nm, nv


def _row_block(r, c, copies):
    tr = r
    while copies * tr * c * 4 > EW_BLOCK_BYTES and tr % (4 * SUBLANES) == 0:
        tr //= 2
    return tr


def _adamw(w, g, m, v, *, name):
    r, c = w.shape
    tr = _row_block(r, c, 1)

    def body(w_ref, g_ref, m_ref, v_ref, d_ref, nm_ref, nv_ref):
        d_ref[...], nm_ref[...], nv_ref[...] = _adamw_math(w_ref[...], g_ref[...], m_ref[...], v_ref[...])

    spec = pl.BlockSpec((tr, c), lambda i: (i, 0))
    shp = jax.ShapeDtypeStruct((r, c), f32)
    return pl.pallas_call(
        body, name=name, grid=(r // tr,), in_specs=[spec] * 4, out_specs=[spec] * 3, out_shape=[shp] * 3,
        compiler_params=_params("parallel"),
    )(w, g, m, v)


def _sum_adamw(parts, w, m, v, *, name, xchg=None):
    r, c = w.shape
    tr = _row_block(r, c, NDEV)

    def body(p_ref, w_ref, m_ref, v_ref, g_ref, d_ref, nm_ref, nv_ref):
        gg = p_ref[0].astype(f32)
        for j in range(1, NDEV):
            gg = gg + p_ref[j].astype(f32)
        g_ref[...] = gg
        d_ref[...], nm_ref[...], nv_ref[...] = _adamw_math(w_ref[...], gg, m_ref[...], v_ref[...])

    spec = pl.BlockSpec((tr, c), lambda i: (i, 0))
    shp = jax.ShapeDtypeStruct((r, c), f32)
    res, got = _call(
        body, (parts, w, m, v), name=name, grid=(r // tr,),
        in_specs=[pl.BlockSpec((NDEV, tr, c), lambda i: (0, i, 0))] + [spec] * 3,
        out_specs=[spec] * 4, out_shape=[shp] * 4, sem=("parallel",), xchg=xchg)
    return tuple(res) if xchg is None else tuple(res) + (got,)


def _cond_fwd(c_all, w, b, *, name):
    nb, n = c_all.shape[0], w.shape[1]

    def body(c_ref, w_ref, b_ref, o_ref):
        cc = c_ref[...]
        o_ref[...] = jnp.dot(cc * _sig(cc), w_ref[...], preferred_element_type=f32,
                             precision=lax.Precision.HIGHEST) + b_ref[...]

    return pl.pallas_call(body, name=name, out_shape=jax.ShapeDtypeStruct((nb, n), f32),
                          compiler_params=_params())(c_all, w, b)


def _cond_bwd(c_all, dmod, *, name):
    d, n = c_all.shape[1], dmod.shape[1]

    def body(c_ref, g_ref, o_ref):
        cc = c_ref[...]
        o_ref[...] = lax.dot_general(cc * _sig(cc), g_ref[...], (((0,), (0,)), ((), ())), preferred_element_type=f32,
                                     precision=lax.Precision.HIGHEST)

    return pl.pallas_call(body, name=name, out_shape=jax.ShapeDtypeStruct((d, n), f32),
                          compiler_params=_params())(c_all, dmod)


def _ssm_disc(lam_re, lam_im, log_dt):
    lr = jnp.minimum(lam_re, -1e-4)
    li = lam_im
    dt = jnp.exp(log_dt)
    mag = jnp.exp(lr * dt)
    ang = li * dt
    lbr, lbi = mag * jnp.cos(ang), mag * jnp.sin(ang)
    num_r, num_i = lbr - 1.0, lbi
    den = lr * lr + li * li
    return lbr, lbi, (num_r * lr + num_i * li) / den, (num_i * lr - num_r * li) / den


def _ssm_prep(lam_re, lam_im, log_dt, *, name):
    def body(a, b, c, o1, o2, o3, o4):
        o1[...], o2[...], o3[...], o4[...] = _ssm_disc(a[...], b[...], c[...])

    shp = jax.ShapeDtypeStruct(lam_re.shape, f32)
    return pl.pallas_call(body, name=name, out_shape=[shp] * 4, compiler_params=_params())(lam_re, lam_im, log_dt)


def _ssm_prep_bwd(lam_re, lam_im, log_dt, cts, *, name):
    def body(a, b, c, g1, g2, g3, g4, o1, o2, o3):
        _, vjp = jax.vjp(_ssm_disc, a[...], b[...], c[...])
        o1[...], o2[...], o3[...] = vjp((g1[...], g2[...], g3[...], g4[...]))

    shp = jax.ShapeDtypeStruct(lam_re.shape, f32)
    return pl.pallas_call(body, name=name, out_shape=[shp, shp, jax.ShapeDtypeStruct(log_dt.shape, f32)],
                          compiler_params=_params())(lam_re, lam_im, log_dt, *cts)


def _step_major(tb, nt, dtype):
    r = lax.broadcasted_iota(jnp.int32, (tb, tb), 0)
    k = lax.broadcasted_iota(jnp.int32, (tb, tb), 1)
    return (k == (r % SUBLANES) * nt + r // SUBLANES).astype(dtype)


def _chunk_major(tb, nt, dtype):
    k = lax.broadcasted_iota(jnp.int32, (tb, tb), 0)
    r = lax.broadcasted_iota(jnp.int32, (tb, tb), 1)
    return (k == (r % SUBLANES) * nt + r // SUBLANES).astype(dtype)


def _permute_f32(pmat, x):
    hi = x.astype(bf16)
    rest = x - hi.astype(f32)
    mid = rest.astype(bf16)
    lo = (rest - mid.astype(f32)).astype(bf16)
    move = lambda part: jnp.dot(pmat, part, preferred_element_type=f32)
    return (move(hi) + move(mid)) + move(lo)


def _permute_bf16(pmat, x):
    return jnp.dot(pmat, x, preferred_element_type=f32).astype(bf16)


def _chain_carries(loc_r, loc_i, pr, pi_, forward):
    row = lax.broadcasted_iota(jnp.int32, loc_r.shape, 0)
    shift = 1 if forward else SUBLANES - 1
    order = range(1, SUBLANES) if forward else range(SUBLANES - 2, -1, -1)
    er, ei = loc_r, loc_i
    for k in order:
        sr, si = pltpu.roll(er, shift, 0), pltpu.roll(ei, shift, 0)
        er = jnp.where(row == k, loc_r + pr * sr - pi_ * si, er)
        ei = jnp.where(row == k, loc_i + pr * si + pi_ * sr, ei)
    edge = 0 if forward else SUBLANES - 1
    return (jnp.where(row == edge, 0.0, pltpu.roll(er, shift, 0)), jnp.where(row == edge, 0.0, pltpu.roll(ei, shift, 0)))


def _chunk_power(ar, ai, chunk_len):
    pr, pi_ = ar, ai
    for _ in range(int(math.log2(chunk_len))):
        pr, pi_ = pr * pr - pi_ * pi_, 2.0 * pr * pi_
    return pr, pi_


def _ssm_mats(bre_ref, bim_ref, cre_ref, cim_ref, cfr_ref, cfi_ref, bbar_s, cmat_s, nq):
    for q in range(nq):
        cr, ci, br, bi = cfr_ref[q], cfi_ref[q], bre_ref[q], bim_ref[q]
        bbar_s[q, :, 0:QS] = (cr * br - ci * bi).astype(bf16)
        bbar_s[q, :, QS:2 * QS] = (cr * bi + ci * br).astype(bf16)
        cmat_s[q, 0:QS, :] = cre_ref[q].astype(bf16)
        cmat_s[q, QS:2 * QS, :] = (-cim_ref[q]).astype(bf16)


def _ssm_fwd(u, ar, ai, bre, bim, cre, cim, cfr, cfi, dvec, *, name, xchg=None):
    s, sw = u.shape
    nq = sw // QW
    st = nq * 2 * QS
    tb = _tile(s, 256)
    nb, nt, chunk_len = s // tb, tb // SUBLANES, s // SUBLANES
    assert chunk_len & (chunk_len - 1) == 0 and nt % 16 == 0

    def body(u_ref, ar_ref, ai_ref, bre_ref, bim_ref, cre_ref, cim_ref, cfr_ref, cfi_ref, d_ref,
             h_out, yraw_out, y_out, buf, hc, bbar_s, cmat_s):
        ph, i = pl.program_id(0), pl.program_id(1)

        @pl.when(i == 0)
        def _():
            _ssm_mats(bre_ref, bim_ref, cre_ref, cim_ref, cfr_ref, cfi_ref, bbar_s, cmat_s, nq)

        @pl.when((ph == 0) & (i == 0))
        def _():
            hc[...] = jnp.zeros_like(hc)

        @pl.when((ph == 1) & (i == 0))
        def _():
            for q in range(nq):
                o = q * 2 * QS
                pr, pi_ = _chunk_power(ar_ref[q], ai_ref[q], chunk_len)
                sr, si = _chain_carries(hc[:, o:o + QS], hc[:, o + QS:o + 2 * QS], pr, pi_, True)
                hc[:, o:o + QS] = sr
                hc[:, o + QS:o + 2 * QS] = si

        uu = u_ref[...].reshape(tb, sw)
        up = _permute_bf16(_step_major(tb, nt, bf16), uu.astype(bf16))
        for q in range(nq):
            o = q * 2 * QS
            buf[:, o:o + 2 * QS] = jnp.dot(up[:, q * QW:(q + 1) * QW], bbar_s[q], preferred_element_type=f32)

        for q in range(nq):
            o = q * 2 * QS
            a_r = jnp.broadcast_to(ar_ref[q], (SUBLANES, QS))
            a_i = jnp.broadcast_to(ai_ref[q], (SUBLANES, QS))

            def step(t, carry, o=o, a_r=a_r, a_i=a_i):
                hr, hi = carry
                r0 = pl.multiple_of(t * SUBLANES, SUBLANES)
                nr = a_r * hr - a_i * hi + buf[pl.ds(r0, SUBLANES), o:o + QS]
                ni = a_r * hi + a_i * hr + buf[pl.ds(r0, SUBLANES), o + QS:o + 2 * QS]
                buf[pl.ds(r0, SUBLANES), o:o + QS] = nr
                buf[pl.ds(r0, SUBLANES), o + QS:o + 2 * QS] = ni
                return nr, ni

            hr, hi = lax.fori_loop(0, nt, step, (hc[:, o:o + QS], hc[:, o + QS:o + 2 * QS]))
            hc[:, o:o + QS] = hr
            hc[:, o + QS:o + 2 * QS] = hi

        @pl.when(ph == 1)
        def _():
            back = _chunk_major(tb, nt, bf16)
            for q in range(nq):
                o = q * 2 * QS
                cs = slice(q * QW, (q + 1) * QW)
                hq = buf[:, o:o + 2 * QS].astype(bf16)
                h_out[:, o:o + 2 * QS] = hq
                yq = _permute_f32(back, jnp.dot(hq, cmat_s[q], preferred_element_type=f32)) + d_ref[:, cs] * uu[:, cs]
                yraw_out[:, :, cs] = yq.reshape(SUBLANES, nt, QW)
                y_out[:, :, cs] = _gelu(yq).astype(bf16).reshape(SUBLANES, nt, QW)

    blk = lambda ph, i: (0, i, 0)
    oblk = lambda ph, i: (0, i * ph, 0)
    act = lambda dt: jax.ShapeDtypeStruct((SUBLANES, chunk_len, sw), dt)
    (h_p, yraw3, y3), got = _call(
        body, (u.reshape(SUBLANES, chunk_len, sw), ar, ai, bre, bim, cre, cim, cfr, cfi, dvec), name=name, grid=(2, nb),
        in_specs=[pl.BlockSpec((SUBLANES, nt, sw), blk), _full(ar.shape), _full(ai.shape), _full(bre.shape), _full(bim.shape),
                  _full(cre.shape), _full(cim.shape), _full(cfr.shape), _full(cfi.shape), _full(dvec.shape)],
        out_specs=[pl.BlockSpec((tb, st), lambda ph, i: (i * ph, 0)), pl.BlockSpec((SUBLANES, nt, sw), oblk),
                   pl.BlockSpec((SUBLANES, nt, sw), oblk)],
        out_shape=[jax.ShapeDtypeStruct((s, st), bf16), act(f32), act(bf16)],
        scratch_shapes=[pltpu.VMEM((tb, st), f32), pltpu.VMEM((SUBLANES, st), f32),
                        pltpu.VMEM((nq, QW, 2 * QS), bf16), pltpu.VMEM((nq, 2 * QS, QW), bf16)],
        sem=("arbitrary", "arbitrary"), xchg=xchg)
    return h_p, yraw3, y3.reshape(s, sw), got


def _ssm_bwd(dy, yraw3, u, h_p, ar, ai, bre, bim, cre, cim, cfr, cfi, dvec, *, name, xchg=None):
    s, sw = u.shape
    nq = sw // QW
    st = nq * 2 * QS
    tb = _tile(s, 256)
    nb, nt, chunk_len = s // tb, tb // SUBLANES, s // SUBLANES

    def body(dy_ref, yraw_ref, u_ref, h_ref, ar_ref, ai_ref, bre_ref, bim_ref, cre_ref, cim_ref, cfr_ref, cfi_ref, d_ref,
             du_out, dbre_out, dbim_out, dcre_out, dcim_out, dcfr_out, dcfi_out, dlbr_out, dlbi_out, dd_out, dbu_out,
             buf, hf, rc, acc, dbbar, dcmat, bbar_s, cmat_s):
        ph, i = pl.program_id(0), pl.program_id(1)

        @pl.when(i == 0)
        def _():
            _ssm_mats(bre_ref, bim_ref, cre_ref, cim_ref, cfr_ref, cfi_ref, bbar_s, cmat_s, nq)

        @pl.when((ph == 0) & (i == 0))
        def _():
            rc[...] = jnp.zeros_like(rc)

        @pl.when((ph == 1) & (i == 0))
        def _():
            for q in range(nq):
                o = q * 2 * QS
                pr, pi_ = _chunk_power(ar_ref[q], ai_ref[q], chunk_len)
                sr, si = _chain_carries(rc[:, o:o + QS], rc[:, o + QS:o + 2 * QS], pr, -pi_, False)
                rc[:, o:o + QS] = sr
                rc[:, o + QS:o + 2 * QS] = si
            acc[...] = jnp.zeros_like(acc)
            dbbar[...] = jnp.zeros_like(dbbar)
            dcmat[...] = jnp.zeros_like(dcmat)
            dd_out[...] = jnp.zeros_like(dd_out)
            dbu_out[...] = jnp.zeros_like(dbu_out)

        dyraw = (dy_ref[...].astype(f32) * _gelu_grad(yraw_ref[...])).reshape(tb, sw)
        fwd_perm = _step_major(tb, nt, bf16)
        dyp = _permute_bf16(fwd_perm, dyraw.astype(bf16))
        for q in range(nq):
            o = q * 2 * QS
            buf[:, o:o + 2 * QS] = lax.dot_general(dyp[:, q * QW:(q + 1) * QW], cmat_s[q], (((1,), (1,)), ((), ())),
                                                   preferred_element_type=f32)

        def recur(with_grad):
            for q in range(nq):
                o = q * 2 * QS
                a_r = jnp.broadcast_to(ar_ref[q], (SUBLANES, QS))
                a_i = jnp.broadcast_to(ai_ref[q], (SUBLANES, QS))

                def step(j, carry, o=o, a_r=a_r, a_i=a_i):
                    r0 = pl.multiple_of((nt - 1 - j) * SUBLANES, SUBLANES)
                    if with_grad:
                        rr, ri, gr, gi = carry
                        hr = hf[pl.ds(r0, SUBLANES), o:o + QS]
                        hi = hf[pl.ds(r0, SUBLANES), o + QS:o + 2 * QS]
                        gr = gr + hr * rr + hi * ri
                        gi = gi + hr * ri - hi * rr
                    else:
                        rr, ri = carry
                    nr = buf[pl.ds(r0, SUBLANES), o:o + QS] + a_r * rr + a_i * ri
                    ni = buf[pl.ds(r0, SUBLANES), o + QS:o + 2 * QS] + a_r * ri - a_i * rr
                    buf[pl.ds(r0, SUBLANES), o:o + QS] = nr
                    buf[pl.ds(r0, SUBLANES), o + QS:o + 2 * QS] = ni
                    return (nr, ni, gr, gi) if with_grad else (nr, ni)

                init = (rc[:, o:o + QS], rc[:, o + QS:o + 2 * QS])
                if with_grad:
                    init = init + (acc[:, o:o + QS], acc[:, o + QS:o + 2 * QS])
                res = lax.fori_loop(0, nt, step, init)
                rc[:, o:o + QS] = res[0]
                rc[:, o + QS:o + 2 * QS] = res[1]
                if with_grad:
                    acc[:, o:o + QS] = res[2]
                    acc[:, o + QS:o + 2 * QS] = res[3]

        @pl.when(ph == 0)
        def _():
            recur(False)

        @pl.when(ph == 1)
        def _():
            hf[...] = h_ref[...].astype(f32)
            recur(True)
            uu = u_ref[...].reshape(tb, sw)
            up = _permute_bf16(fwd_perm, uu.astype(bf16))
            back = _chunk_major(tb, nt, bf16)
            dd_out[...] += _colsum(dyraw * uu)
            for q in range(nq):
                o = q * 2 * QS
                cs = slice(q * QW, (q + 1) * QW)
                lam = buf[:, o:o + 2 * QS].astype(bf16)
                duq = _permute_f32(back, lax.dot_general(lam, bbar_s[q], (((1,), (1,)), ((), ())), preferred_element_type=f32)) \
                    + d_ref[:, cs] * dyraw[:, cs]
                du_out[:, :, cs] = duq.astype(bf16).reshape(SUBLANES, nt, QW)
                dbu_out[:, cs] += _colsum(duq)
                dbbar[q] += lax.dot_general(up[:, cs], lam, (((0,), (0,)), ((), ())), preferred_element_type=f32)
                dcmat[q] += lax.dot_general(h_ref[:, o:o + 2 * QS], dyp[:, cs], (((0,), (0,)), ((), ())),
                                            preferred_element_type=f32)

        @pl.when((ph == 1) & (i == nb - 1))
        def _():
            for q in range(nq):
                o = q * 2 * QS
                cr, ci, br, bi = cfr_ref[q], cfi_ref[q], bre_ref[q], bim_ref[q]
                gr, gi = dbbar[q, :, 0:QS], dbbar[q, :, QS:2 * QS]
                dbre_out[q] = cr * gr + ci * gi
                dbim_out[q] = cr * gi - ci * gr
                dcfr_out[q] = _colsum(gr * br + gi * bi)
                dcfi_out[q] = _colsum(gi * br - gr * bi)
                dcre_out[q] = dcmat[q, 0:QS, :]
                dcim_out[q] = -dcmat[q, QS:2 * QS, :]
                dlbr_out[q] = _colsum(acc[:, o:o + QS])
                dlbi_out[q] = _colsum(acc[:, o + QS:o + 2 * QS])

    blk = lambda ph, i: (0, nb - 1 - i, 0)
    oblk = lambda ph, i: (0, (nb - 1 - i) * ph + (nb - 1) * (1 - ph), 0)
    pshapes = [ar.shape, ai.shape, bre.shape, bim.shape, cre.shape, cim.shape, cfr.shape, cfi.shape, dvec.shape]
    oshapes = [bre.shape, bim.shape, cre.shape, cim.shape, cfr.shape, cfi.shape, ar.shape, ai.shape, dvec.shape, dvec.shape]
    act = pl.BlockSpec((SUBLANES, nt, sw), blk)
    view = lambda a: a.reshape(SUBLANES, chunk_len, sw)
    res, got = _call(
        body, (view(dy), yraw3, view(u), h_p, ar, ai, bre, bim, cre, cim, cfr, cfi, dvec), name=name, grid=(2, nb),
        in_specs=[act, act, act, pl.BlockSpec((tb, st), lambda ph, i: (nb - 1 - i, 0))] + [_full(p) for p in pshapes],
        out_specs=[pl.BlockSpec((SUBLANES, nt, sw), oblk)] + [_full(p) for p in oshapes],
        out_shape=[jax.ShapeDtypeStruct((SUBLANES, chunk_len, sw), bf16)] + [jax.ShapeDtypeStruct(p, f32) for p in oshapes],
        scratch_shapes=[pltpu.VMEM((tb, st), f32), pltpu.VMEM((tb, st), f32),
                        pltpu.VMEM((SUBLANES, st), f32), pltpu.VMEM((SUBLANES, st), f32),
                        pltpu.VMEM((nq, QW, 2 * QS), f32), pltpu.VMEM((nq, 2 * QS, QW), f32),
                        pltpu.VMEM((nq, QW, 2 * QS), bf16), pltpu.VMEM((nq, 2 * QS, QW), bf16)],
        sem=("arbitrary", "arbitrary"), xchg=xchg)
    return (res[0].reshape(s, sw),) + tuple(res[1:]) + (got,)


def _lnmod(x, sc, sh, *, name):
    s, d = x.shape
    tb = _tile(s, 512)

    def body(x_ref, sc_ref, sh_ref, o_ref):
        xh, _ = _ln(x_ref[...])
        o_ref[...] = (xh * (1.0 + sc_ref[...]) + sh_ref[...]).astype(bf16)

    blk = pl.BlockSpec((tb, d), lambda i: (i, 0))
    vec = pl.BlockSpec((1, d), _row)
    return pl.pallas_call(body, name=name, grid=(s // tb,), in_specs=[blk, vec, vec], out_specs=blk,
                          out_shape=jax.ShapeDtypeStruct((s, d), bf16), compiler_params=_params("parallel"))(x, sc, sh)


ROWS = 32


def _row_chunks(n_rows, rows, fn, init, start=0):
    return lax.fori_loop(start, n_rows // rows, lambda c, carry: fn(pl.multiple_of(c * rows, rows), carry), init)


def _rows_from(win, o, rows):
    if o % SUBLANES == 0:
        return win[o:o + rows]
    n = win.shape[0]
    return pltpu.roll(win, (n - o) % n, 0)[0:rows]


def _window_before(ref, halo, r0, rows, first, cols):
    if first:
        return jnp.concatenate([halo, ref[pl.ds(0, rows), cols]], axis=0)
    return ref[pl.ds(pl.multiple_of(r0 - SUBLANES, SUBLANES), rows + SUBLANES), cols]


def _taps3(win, w, off, rows):
    return _rows_from(win, off, rows) * w[0] + _rows_from(win, off + 1, rows) * w[1] + _rows_from(win, off + 2, rows) * w[2]


def _fold8(x):
    acc = x[0:SUBLANES]
    for r in range(1, x.shape[0] // SUBLANES):
        acc = acc + x[r * SUBLANES:(r + 1) * SUBLANES]
    return acc


def _conv_halo_specs(tb, cw, halo, s):
    per = tb // halo
    prev = pl.BlockSpec((halo, cw), lambda i: (jnp.maximum(i * per - 1, 0), 0))
    nxt = pl.BlockSpec((halo, cw), lambda i: (jnp.minimum((i + 1) * per, s // halo - 1), 0))
    return prev, nxt


WIDE_ROWS = 16


def _shift_groups(lo, hi):
    return [(b, [o for o in range(lo, hi + 1) if o % SUBLANES == b]) for b in range(SUBLANES)]


def _shifted(win, b):
    return win if b == 0 else _rows_from(win, b, win.shape[0] - SUBLANES)


def _conv31(win, w_ref, cols, rows, lo, hi, tap_of):
    acc = None
    for b, offs in _shift_groups(lo, hi):
        if offs:
            wb = _shifted(win, b)
            for o in offs:
                term = wb[o - b:o - b + rows] * w_ref[pl.ds(tap_of(o), 1), cols]
                acc = term if acc is None else acc + term
    return acc


def _gate_into(ext, a_ref, g_ref, ah_ref, gh_ref, tb, i):
    ext[pl.ds(0, CONV_HALO), :] = jnp.where(i > 0, ah_ref[...] * _sig(gh_ref[...]), 0.0)

    def chunk(r0, carry):
        ext[pl.ds(pl.multiple_of(r0 + CONV_HALO, SUBLANES), WIDE_ROWS), :] = \
            a_ref[pl.ds(r0, WIDE_ROWS), :] * _sig(g_ref[pl.ds(r0, WIDE_ROWS), :])
        return carry

    _row_chunks(tb, WIDE_ROWS, chunk, 0)


def _causal_conv_into(v2buf, ext, w_ref, b_ref, tb, cw):
    for ct in range(cw // LANES):
        cols = slice(ct * LANES, (ct + 1) * LANES)

        def chunk(r0, carry, cols=cols):
            win = ext[pl.ds(r0, ROWS + CONV_HALO), cols]
            v2buf[pl.ds(r0, ROWS), cols] = _conv31(win, w_ref, cols, ROWS, 2, CONV_K + 1, lambda o: o - 2) + b_ref[:, cols]
            return carry

        _row_chunks(tb, ROWS, chunk, 0)


def _silu_grad(x):
    sg = _sig(x)
    return sg * (1.0 + x * (1.0 - sg))


def _conv_fwd(cva, cvg, w, b, lng, lnb, *, name, xchg=None):
    s, cw = cva.shape
    tb = _tile(s, 256)
    prev, _ = _conv_halo_specs(tb, cw, CONV_HALO, s)

    def body(a_ref, g_ref, ah_ref, gh_ref, w_ref, b_ref, lng_ref, lnb_ref, o_ref, ext, v2buf):
        _gate_into(ext, a_ref, g_ref, ah_ref, gh_ref, tb, pl.program_id(0))
        _causal_conv_into(v2buf, ext, w_ref, b_ref, tb, cw)
        xh, _ = _ln(v2buf[...])
        v3 = xh * lng_ref[...] + lnb_ref[...]
        o_ref[...] = (v3 * _sig(v3)).astype(bf16)

    blk = pl.BlockSpec((tb, cw), lambda i: (i, 0))
    vec = pl.BlockSpec((1, cw), _row)
    (v4,), got = _call(
        body, (cva, cvg, cva, cvg, w, b, lng, lnb), name=name, grid=(s // tb,),
        in_specs=[blk, blk, prev, prev, _full(w.shape), vec, vec, vec], out_specs=[blk],
        out_shape=[jax.ShapeDtypeStruct((s, cw), bf16)],
        scratch_shapes=[pltpu.VMEM((tb + CONV_HALO, cw), f32), pltpu.VMEM((tb, cw), f32)], sem=("parallel",), xchg=xchg)
    return v4, got


def _conv_bwd_ln(dv4, cva, cvg, w, b, lng, lnb, *, name):
    s, cw = cva.shape
    tb = _tile(s, 256)
    prev, _ = _conv_halo_specs(tb, cw, CONV_HALO, s)

    def body(d_ref, a_ref, g_ref, ah_ref, gh_ref, w_ref, b_ref, lng_ref, lnb_ref, o_ref, dg_ref, db_ref, ext, v2buf):
        i = pl.program_id(0)

        @pl.when(i == 0)
        def _():
            dg_ref[...] = jnp.zeros_like(dg_ref)
            db_ref[...] = jnp.zeros_like(db_ref)

        _gate_into(ext, a_ref, g_ref, ah_ref, gh_ref, tb, i)
        _causal_conv_into(v2buf, ext, w_ref, b_ref, tb, cw)
        xh, rstd = _ln(v2buf[...])
        v3 = xh * lng_ref[...] + lnb_ref[...]
        dv3 = d_ref[...].astype(f32) * _silu_grad(v3)
        dg_ref[...] += _colsum(dv3 * xh)
        db_ref[...] += _colsum(dv3)
        o_ref[...] = _ln_bwd(dv3 * lng_ref[...], xh, rstd)

    blk = pl.BlockSpec((tb, cw), lambda i: (i, 0))
    vec = pl.BlockSpec((1, cw), _row)
    vshape = jax.ShapeDtypeStruct((1, cw), f32)
    return pl.pallas_call(
        body, name=name, grid=(s // tb,), in_specs=[blk, blk, blk, prev, prev, _full(w.shape), vec, vec, vec],
        out_specs=[blk, vec, vec], out_shape=[jax.ShapeDtypeStruct((s, cw), f32), vshape, vshape],
        scratch_shapes=[pltpu.VMEM((tb + CONV_HALO, cw), f32), pltpu.VMEM((tb, cw), f32)],
        compiler_params=_params("arbitrary"))(dv4, cva, cvg, cva, cvg, w, b, lng, lnb)


def _conv_bwd_taps(dv2, cva, cvg, w, *, name, xchg=None):
    s, cw = cva.shape
    tb = _tile(s, 256)
    nb = s // tb
    prev, nxt = _conv_halo_specs(tb, cw, CONV_HALO, s)

    def body(d_ref, dn_ref, a_ref, g_ref, ah_ref, gh_ref, w_ref, da_ref, dg_ref, dw_ref, db_ref, sa_ref, sg_ref,
             ext, dext, dvbuf, tap_sums):
        i = pl.program_id(0)

        @pl.when(i == 0)
        def _():
            for r in (dw_ref, db_ref, sa_ref, sg_ref):
                r[...] = jnp.zeros_like(r)

        _gate_into(ext, a_ref, g_ref, ah_ref, gh_ref, tb, i)
        dext[pl.ds(tb, CONV_HALO), :] = jnp.where(i < nb - 1, dn_ref[...], 0.0)

        def copy(r0, carry):
            dext[pl.ds(r0, WIDE_ROWS), :] = d_ref[pl.ds(r0, WIDE_ROWS), :]
            return carry

        _row_chunks(tb, WIDE_ROWS, copy, 0)

        for ct in range(cw // LANES):
            cols = slice(ct * LANES, (ct + 1) * LANES)

            tap_sums[...] = jnp.zeros_like(tap_sums)

            def back(r0, carry, cols=cols):
                win = dext[pl.ds(r0, ROWS + CONV_HALO), cols]
                dvbuf[pl.ds(r0, ROWS), cols] = _conv31(win, w_ref, cols, ROWS, 0, CONV_K - 1, lambda o: CONV_K - 1 - o)
                win = ext[pl.ds(r0, ROWS + CONV_HALO), cols]
                dd = d_ref[pl.ds(r0, ROWS), cols]
                for b, offs in _shift_groups(2, CONV_K + 1):
                    wb = _shifted(win, b)
                    for o in offs:
                        tap_sums[o - 2] += _fold8(dd * wb[o - b:o - b + ROWS])
                return carry

            _row_chunks(tb, ROWS, back, 0)
            for k in range(CONV_K):
                dw_ref[pl.ds(k, 1), cols] += _colsum(tap_sums[k])

        def gate_back(r0, sums):
            rows = pl.ds(r0, WIDE_ROWS)
            aa, sg, dv = a_ref[rows, :], _sig(g_ref[rows, :]), dvbuf[rows, :]
            da = dv * sg
            dgate = dv * aa * sg * (1.0 - sg)
            da_ref[rows, :] = da.astype(bf16)
            dg_ref[rows, :] = dgate.astype(bf16)
            return sums[0] + _fold8(da), sums[1] + _fold8(dgate), sums[2] + _fold8(d_ref[rows, :])

        zero = jnp.zeros((SUBLANES, cw), f32)
        sums = _row_chunks(tb, WIDE_ROWS, gate_back, (zero, zero, zero))
        sa_ref[...] += _colsum(sums[0])
        sg_ref[...] += _colsum(sums[1])
        db_ref[...] += _colsum(sums[2])

    blk = pl.BlockSpec((tb, cw), lambda i: (i, 0))
    vec = pl.BlockSpec((1, cw), _row)
    vshape = jax.ShapeDtypeStruct((1, cw), f32)
    act = jax.ShapeDtypeStruct((s, cw), bf16)
    res, got = _call(
        body, (dv2, dv2, cva, cvg, cva, cvg, w), name=name, grid=(nb,), in_specs=[blk, nxt, blk, blk, prev, prev, _full(w.shape)],
        out_specs=[blk, blk, _full(w.shape), vec, vec, vec],
        out_shape=[act, act, jax.ShapeDtypeStruct(w.shape, f32), vshape, vshape, vshape],
        scratch_shapes=[pltpu.VMEM((tb + CONV_HALO, cw), f32), pltpu.VMEM((tb + CONV_HALO, cw), f32), pltpu.VMEM((tb, cw), f32),
                        pltpu.VMEM((CONV_HALO, SUBLANES, LANES), f32)],
        sem=("arbitrary",), xchg=xchg)
    return tuple(res) + (got,)


def _glu_merge(ya, yb, ycv, gs, gc, *, name):
    s, d = ya.shape
    tb = _tile(s, 512)

    def body(ya_ref, yb_ref, ycv_ref, gs_ref, gc_ref, o_ref):
        ld = lambda r: r[...].astype(f32)
        z = ld(ya_ref) * _sig(ld(yb_ref))
        o_ref[...] = (_sig(ld(gs_ref)) * z + _sig(ld(gc_ref)) * ld(ycv_ref)).astype(bf16)

    blk = pl.BlockSpec((tb, d), lambda i: (i, 0))
    return pl.pallas_call(body, name=name, grid=(s // tb,), in_specs=[blk] * 5, out_specs=blk,
                          out_shape=jax.ShapeDtypeStruct((s, d), bf16), compiler_params=_params("parallel"))(ya, yb, ycv, gs, gc)


def _glu_merge_bwd(dm, ya, yb, ycv, gs, gc, *, name):
    s, d = ya.shape
    tb = _tile(s, 512)

    def body(dm_ref, ya_ref, yb_ref, ycv_ref, gs_ref, gc_ref, dya_ref, dyb_ref, dycv_ref, dgs_ref, dgc_ref, sgs_ref, sgc_ref):
        @pl.when(pl.program_id(0) == 0)
        def _():
            sgs_ref[...] = jnp.zeros_like(sgs_ref)
            sgc_ref[...] = jnp.zeros_like(sgc_ref)

        ld = lambda r: r[...].astype(f32)
        dmv, yav = ld(dm_ref), ld(ya_ref)
        sb, ss, scv = _sig(ld(yb_ref)), _sig(ld(gs_ref)), _sig(ld(gc_ref))
        z = yav * sb
        dz = dmv * ss
        dgs = dmv * z * ss * (1.0 - ss)
        dgc = dmv * ld(ycv_ref) * scv * (1.0 - scv)
        dya_ref[...] = (dz * sb).astype(bf16)
        dyb_ref[...] = (dz * yav * sb * (1.0 - sb)).astype(bf16)
        dycv_ref[...] = (dmv * scv).astype(bf16)
        dgs_ref[...] = dgs.astype(bf16)
        dgc_ref[...] = dgc.astype(bf16)
        sgs_ref[...] += _colsum(dgs)
        sgc_ref[...] += _colsum(dgc)

    blk = pl.BlockSpec((tb, d), lambda i: (i, 0))
    vec = pl.BlockSpec((1, d), _row)
    act = jax.ShapeDtypeStruct((s, d), bf16)
    vshape = jax.ShapeDtypeStruct((1, d), f32)
    return pl.pallas_call(body, name=name, grid=(s // tb,), in_specs=[blk] * 6, out_specs=[blk] * 5 + [vec, vec],
                          out_shape=[act] * 5 + [vshape, vshape], compiler_params=_params("arbitrary"))(dm, ya, yb, ycv, gs, gc)


def _resid_ln_mod(x, o, g, lng, lnb, sc, sh, alpha, *, name):
    s, d = x.shape
    tb = _tile(s, 512)

    def body(x_ref, o_ref, g_ref, lng_ref, lnb_ref, sc_ref, sh_ref, x1_ref, h_ref):
        xh, _ = _ln(alpha * x_ref[...] + g_ref[...] * o_ref[...].astype(f32))
        x1 = xh * lng_ref[...] + lnb_ref[...]
        x1_ref[...] = x1
        xh1, _ = _ln(x1)
        h_ref[...] = (xh1 * (1.0 + sc_ref[...]) + sh_ref[...]).astype(bf16)

    blk = pl.BlockSpec((tb, d), lambda i: (i, 0))
    vec = pl.BlockSpec((1, d), _row)
    return pl.pallas_call(body, name=name, grid=(s // tb,), in_specs=[blk, blk] + [vec] * 5, out_specs=[blk, blk],
                          out_shape=[jax.ShapeDtypeStruct((s, d), f32), jax.ShapeDtypeStruct((s, d), bf16)],
                          compiler_params=_params("parallel"))(x, o, g, lng, lnb, sc, sh)


def _resid_ln_loss(x1, y2, g, lng, lnb, tgt, alpha, *, name):
    s, d = x1.shape
    tb = _tile(s, 512)

    def body(x1_ref, y_ref, g_ref, lng_ref, lnb_ref, t_ref, dr_ref, dy_ref, loss_ref, dlg_ref, dlb_ref, dg_ref):
        @pl.when(pl.program_id(0) == 0)
        def _():
            for r in (loss_ref, dlg_ref, dlb_ref, dg_ref):
                r[...] = jnp.zeros_like(r)

        yv = y_ref[...]
        xh, rstd = _ln(alpha * x1_ref[...] + g_ref[...] * yv)
        err = xh * lng_ref[...] + lnb_ref[...] - t_ref[...]
        loss_ref[...] += 0.5 * jnp.sum(jnp.sum(err * err, axis=-1, keepdims=True) / d, axis=0, keepdims=True)
        dx2 = err / d
        dlg_ref[...] += _colsum(dx2 * xh)
        dlb_ref[...] += _colsum(dx2)
        dr = _ln_bwd(dx2 * lng_ref[...], xh, rstd)
        dg_ref[...] += _colsum(dr * yv)
        dr_ref[...] = dr
        dy_ref[...] = (g_ref[...] * dr).astype(bf16)

    blk = pl.BlockSpec((tb, d), lambda i: (i, 0))
    vec = pl.BlockSpec((1, d), _row)
    vshape = jax.ShapeDtypeStruct((1, d), f32)
    return pl.pallas_call(
        body, name=name, grid=(s // tb,), in_specs=[blk, blk, vec, vec, vec, blk],
        out_specs=[blk, blk, pl.BlockSpec((1, 1), _row), vec, vec, vec],
        out_shape=[jax.ShapeDtypeStruct((s, d), f32), jax.ShapeDtypeStruct((s, d), bf16),
                   jax.ShapeDtypeStruct((1, 1), f32), vshape, vshape, vshape],
        compiler_params=_params("arbitrary"))(x1, y2, g, lng, lnb, tgt)


def _mid_bwd(dh2, x1, dr2, x, o, g, sc, lng, alpha, *, name):
    s, d = x.shape
    tb = _tile(s, 512)

    def body(dh_ref, x1_ref, dr2_ref, x_ref, o_ref, g_ref, sc_ref, lng_ref,
             dr1_ref, do_ref, dsc_ref, dsh_ref, dlg_ref, dlb_ref, dg_ref):
        @pl.when(pl.program_id(0) == 0)
        def _():
            for r in (dsc_ref, dsh_ref, dlg_ref, dlb_ref, dg_ref):
                r[...] = jnp.zeros_like(r)

        dh = dh_ref[...].astype(f32)
        xh1, rstd1 = _ln(x1_ref[...])
        dsc_ref[...] += _colsum(dh * xh1)
        dsh_ref[...] += _colsum(dh)
        dx1 = alpha * dr2_ref[...] + _ln_bwd(dh * (1.0 + sc_ref[...]), xh1, rstd1)
        ov = o_ref[...].astype(f32)
        xhr, rstdr = _ln(alpha * x_ref[...] + g_ref[...] * ov)
        dlg_ref[...] += _colsum(dx1 * xhr)
        dlb_ref[...] += _colsum(dx1)
        dr1 = _ln_bwd(dx1 * lng_ref[...], xhr, rstdr)
        dg_ref[...] += _colsum(dr1 * ov)
        dr1_ref[...] = dr1
        do_ref[...] = (g_ref[...] * dr1).astype(bf16)

    blk = pl.BlockSpec((tb, d), lambda i: (i, 0))
    vec = pl.BlockSpec((1, d), _row)
    vshape = jax.ShapeDtypeStruct((1, d), f32)
    return pl.pallas_call(
        body, name=name, grid=(s // tb,), in_specs=[blk] * 5 + [vec] * 3, out_specs=[blk, blk] + [vec] * 5,
        out_shape=[jax.ShapeDtypeStruct((s, d), f32), jax.ShapeDtypeStruct((s, d), bf16)] + [vshape] * 5,
        compiler_params=_params("arbitrary"))(dh2, x1, dr2, x, o, g, sc, lng)


def _final_bwd(dh1, x, dr1, sc, alpha, *, name, xchg=None):
    s, d = x.shape
    tb = _tile(s, 512)

    def body(dh_ref, x_ref, dr1_ref, sc_ref, dx_ref, dsc_ref, dsh_ref):
        @pl.when(pl.program_id(0) == 0)
        def _():
            dsc_ref[...] = jnp.zeros_like(dsc_ref)
            dsh_ref[...] = jnp.zeros_like(dsh_ref)

        dh = dh_ref[...].astype(f32)
        xh, rstd = _ln(x_ref[...])
        dsc_ref[...] += _colsum(dh * xh)
        dsh_ref[...] += _colsum(dh)
        dx_ref[...] = alpha * dr1_ref[...] + _ln_bwd(dh * (1.0 + sc_ref[...]), xh, rstd)

    blk = pl.BlockSpec((tb, d), lambda i: (i, 0))
    vec = pl.BlockSpec((1, d), _row)
    vshape = jax.ShapeDtypeStruct((1, d), f32)
    res, got = _call(body, (dh1, x, dr1, sc), name=name, grid=(s // tb,), in_specs=[blk, blk, blk, vec], out_specs=[blk, vec, vec],
                     out_shape=[jax.ShapeDtypeStruct((s, d), f32), vshape, vshape], sem=("arbitrary",), xchg=xchg)
    return tuple(res) + (got,)


TALL_ROWS = 64


def _ffn_col_tile(fh):
    return fh // 2 if (fh // 2) % LANES == 0 else fh


def _ffn_specs(s, fh, tb, tc):
    per = tb // FFN_HALO
    blk = pl.BlockSpec((tb, tc), lambda j, i: (i, j))
    prev = pl.BlockSpec((FFN_HALO, tc), lambda j, i: (jnp.maximum(i * per - 1, 0), j))
    nxt = pl.BlockSpec((FFN_HALO, tc), lambda j, i: (jnp.minimum((i + 1) * per, s // FFN_HALO - 1), j))
    taps = pl.BlockSpec((FFN_HALO, tc), lambda j, i: (0, j))
    vec = pl.BlockSpec((1, tc), lambda j, i: (0, j))
    return blk, prev, nxt, taps, vec


def _ffn_mid(upa, upv, wa, wv, ba, bv, *, name, xchg=None):
    s, fh = upa.shape
    tb, tc = _tile(s, 512), _ffn_col_tile(fh)
    blk, prev, _, taps, vec = _ffn_specs(s, fh, tb, tc)
    off = FFN_HALO - FFN_K + 1

    def body(a_ref, v_ref, ah_ref, vh_ref, wa_ref, wv_ref, ba_ref, bv_ref, o_ref):
        first = pl.program_id(1) == 0
        for lt in range(tc // LANES):
            cols = slice(lt * LANES, (lt + 1) * LANES)
            halo_a, halo_v = jnp.where(first, 0.0, ah_ref[:, cols]), jnp.where(first, 0.0, vh_ref[:, cols])
            wa = [wa_ref[pl.ds(k, 1), cols] for k in range(FFN_K)]
            wv = [wv_ref[pl.ds(k, 1), cols] for k in range(FFN_K)]
            ba, bv = ba_ref[:, cols], bv_ref[:, cols]

            def chunk(r0, carry, head=False, cols=cols, halo_a=halo_a, halo_v=halo_v, wa=wa, wv=wv, ba=ba, bv=bv):
                a2 = _taps3(_window_before(a_ref, halo_a, r0, TALL_ROWS, head, cols), wa, off, TALL_ROWS) + ba
                v2 = _taps3(_window_before(v_ref, halo_v, r0, TALL_ROWS, head, cols), wv, off, TALL_ROWS) + bv
                o_ref[pl.ds(r0, TALL_ROWS), cols] = (_gelu(a2) * v2).astype(bf16)
                return carry

            chunk(0, 0, head=True)
            _row_chunks(tb, TALL_ROWS, chunk, 0, start=1)

    (f,), got = _call(
        body, (upa, upv, upa, upv, wa, wv, ba, bv), name=name, grid=(fh // tc, s // tb),
        in_specs=[blk, blk, prev, prev, taps, taps, vec, vec], out_specs=[blk], out_shape=[jax.ShapeDtypeStruct((s, fh), bf16)],
        sem=("parallel", "arbitrary"), xchg=xchg)
    return f, got


def _ffn_mid_bwd_tile(cols, first, last, tb, off, df_ref, dfn_ref, a_ref, v_ref, ah_ref, vh_ref, an_ref, vn_ref, wa_ref, wv_ref,
                      ba_ref, bv_ref, da_ref, dv_ref, dwa_ref, dwv_ref, dba_ref, dbv_ref, dexta, dextv):
    rows_c = TALL_ROWS
    halo_a, halo_v = jnp.where(first, 0.0, ah_ref[:, cols]), jnp.where(first, 0.0, vh_ref[:, cols])
    wa = [wa_ref[pl.ds(k, 1), cols] for k in range(FFN_K)]
    wv = [wv_ref[pl.ds(k, 1), cols] for k in range(FFN_K)]
    ba, bv = ba_ref[:, cols], bv_ref[:, cols]

    def conv_cotangents(r0, rows, xa, xv, dfe):
        sa = [_rows_from(xa, off + k, rows) for k in range(FFN_K)]
        sv = [_rows_from(xv, off + k, rows) for k in range(FFN_K)]
        a2 = sa[0] * wa[0] + sa[1] * wa[1] + sa[2] * wa[2] + ba
        v2 = sv[0] * wv[0] + sv[1] * wv[1] + sv[2] * wv[2] + bv
        cdf = 0.5 * (1.0 + lax.erf(a2 * INV_SQRT2))
        da2 = dfe * v2 * (cdf + a2 * jnp.exp(-0.5 * a2 * a2) * INV_SQRT_2PI)
        dv2 = dfe * (a2 * cdf)
        dexta[pl.ds(r0, rows), cols] = da2
        dextv[pl.ds(r0, rows), cols] = dv2
        return da2, dv2, sa, sv

    def chunk(r0, sums, head=False):
        da2, dv2, sa, sv = conv_cotangents(r0, rows_c, _window_before(a_ref, halo_a, r0, rows_c, head, cols),
                                           _window_before(v_ref, halo_v, r0, rows_c, head, cols), df_ref[pl.ds(r0, rows_c), cols])
        new = [sums[k] + _fold8(da2 * sa[k]) for k in range(FFN_K)] + [sums[FFN_K] + _fold8(da2)]
        new += [sums[FFN_K + 1 + k] + _fold8(dv2 * sv[k]) for k in range(FFN_K)] + [sums[2 * FFN_K + 1] + _fold8(dv2)]
        return tuple(new)

    sums = chunk(0, tuple(jnp.zeros((SUBLANES, LANES), f32) for _ in range(2 * FFN_K + 2)), head=True)
    sums = _row_chunks(tb, rows_c, chunk, sums, start=1)
    conv_cotangents(tb, FFN_HALO,
                    jnp.concatenate([a_ref[pl.ds(tb - FFN_HALO, FFN_HALO), cols], jnp.where(last, 0.0, an_ref[:, cols])], axis=0),
                    jnp.concatenate([v_ref[pl.ds(tb - FFN_HALO, FFN_HALO), cols], jnp.where(last, 0.0, vn_ref[:, cols])], axis=0),
                    jnp.where(last, 0.0, dfn_ref[:, cols]))
    for k in range(FFN_K):
        dwa_ref[pl.ds(k, 1), cols] += _colsum(sums[k])
        dwv_ref[pl.ds(k, 1), cols] += _colsum(sums[FFN_K + 1 + k])
    dba_ref[:, cols] += _colsum(sums[FFN_K])
    dbv_ref[:, cols] += _colsum(sums[2 * FFN_K + 1])

    def back(r0, carry):
        for dext, w, o_ref in ((dexta, wa, da_ref), (dextv, wv, dv_ref)):
            dd = dext[pl.ds(r0, rows_c + FFN_HALO), cols]
            o_ref[pl.ds(r0, rows_c), cols] = (_rows_from(dd, 2, rows_c) * w[0] + _rows_from(dd, 1, rows_c) * w[1]
                                              + dd[0:rows_c] * w[2]).astype(bf16)
        return carry

    _row_chunks(tb, rows_c, back, 0)


def _ffn_mid_bwd(df, upa, upv, wa, wv, ba, bv, *, name, xchg=None):
    s, fh = upa.shape
    tb, tc = _tile(s, 512), _ffn_col_tile(fh)
    nb = s // tb
    blk, prev, nxt, taps, vec = _ffn_specs(s, fh, tb, tc)
    off = FFN_HALO - FFN_K + 1
    te = tb + FFN_HALO

    def body(df_ref, dfn_ref, a_ref, v_ref, ah_ref, vh_ref, an_ref, vn_ref, wa_ref, wv_ref, ba_ref, bv_ref,
             da_ref, dv_ref, dwa_ref, dwv_ref, dba_ref, dbv_ref, dexta, dextv):
        i = pl.program_id(1)

        @pl.when(i == 0)
        def _():
            for r in (dwa_ref, dwv_ref, dba_ref, dbv_ref):
                r[...] = jnp.zeros_like(r)

        last = i == nb - 1
        for lt in range(tc // LANES):
            _ffn_mid_bwd_tile(slice(lt * LANES, (lt + 1) * LANES), i == 0, last, tb, off, df_ref, dfn_ref, a_ref, v_ref,
                              ah_ref, vh_ref, an_ref, vn_ref, wa_ref, wv_ref, ba_ref, bv_ref, da_ref, dv_ref, dwa_ref, dwv_ref,
                              dba_ref, dbv_ref, dexta, dextv)

    act = jax.ShapeDtypeStruct((s, fh), bf16)
    wshape = jax.ShapeDtypeStruct((FFN_HALO, fh), f32)
    vshape = jax.ShapeDtypeStruct((1, fh), f32)
    res, got = _call(
        body, (df, df, upa, upv, upa, upv, upa, upv, wa, wv, ba, bv), name=name, grid=(fh // tc, nb),
        in_specs=[blk, nxt, blk, blk, prev, prev, nxt, nxt, taps, taps, vec, vec],
        out_specs=[blk, blk, taps, taps, vec, vec], out_shape=[act, act, wshape, wshape, vshape, vshape],
        scratch_shapes=[pltpu.VMEM((te, tc), f32)] * 2, sem=("parallel", "arbitrary"), xchg=xchg)
    return tuple(res) + (got,)


def _cols_from_shards(stacked):
    _, k, n = stacked.shape
    return stacked.transpose(1, 0, 2).reshape(k, NDEV * n)


def _pad_rows(w, rows):
    return jnp.pad(w, ((0, rows - w.shape[0]), (0, 0)))


def kernel(x, c, w_cond, b_cond, w_in, b_in, ssm_lambda_re, ssm_lambda_im, ssm_log_dt, ssm_b_re, ssm_b_im, ssm_c_re, ssm_c_im, ssm_d, ssm_glu_w_a, ssm_glu_w_b, cv_dw_w, cv_dw_b, cv_ln_g, cv_ln_b, cv_w_pw, w_out, ln1_g, ln1_b, ffn_w_up, ffn_dw_w, ffn_dw_b, ffn_w_down, ln2_g, ln2_b, loss_target, m_w_cond, m_b_cond, m_w_in, m_b_in, m_ssm_lambda_re, m_ssm_lambda_im, m_ssm_log_dt, m_ssm_b_re, m_ssm_b_im, m_ssm_c_re, m_ssm_c_im, m_ssm_d, m_ssm_glu_w_a, m_ssm_glu_w_b, m_cv_dw_w, m_cv_dw_b, m_cv_ln_g, m_cv_ln_b, m_cv_w_pw, m_w_out, m_ln1_g, m_ln1_b, m_ffn_w_up, m_ffn_dw_w, m_ffn_dw_b, m_ffn_w_down, m_ln2_g, m_ln2_b, v_w_cond, v_b_cond, v_w_in, v_b_in, v_ssm_lambda_re, v_ssm_lambda_im, v_ssm_log_dt, v_ssm_b_re, v_ssm_b_im, v_ssm_c_re, v_ssm_c_im, v_ssm_d, v_ssm_glu_w_a, v_ssm_glu_w_b, v_cv_dw_w, v_cv_dw_b, v_cv_ln_g, v_cv_ln_b, v_cv_w_pw, v_w_out, v_ln1_g, v_ln1_b, v_ffn_w_up, v_ffn_dw_w, v_ffn_dw_b, v_ffn_w_down, v_ln2_g, v_ln2_b):
    weights = dict(w_cond=w_cond, b_cond=b_cond, w_in=w_in, b_in=b_in, ssm_lambda_re=ssm_lambda_re, ssm_lambda_im=ssm_lambda_im, ssm_log_dt=ssm_log_dt, ssm_b_re=ssm_b_re, ssm_b_im=ssm_b_im, ssm_c_re=ssm_c_re, ssm_c_im=ssm_c_im, ssm_d=ssm_d, ssm_glu_w_a=ssm_glu_w_a, ssm_glu_w_b=ssm_glu_w_b, cv_dw_w=cv_dw_w, cv_dw_b=cv_dw_b, cv_ln_g=cv_ln_g, cv_ln_b=cv_ln_b, cv_w_pw=cv_w_pw, w_out=w_out, ln1_g=ln1_g, ln1_b=ln1_b, ffn_w_up=ffn_w_up, ffn_dw_w=ffn_dw_w, ffn_dw_b=ffn_dw_b, ffn_w_down=ffn_w_down, ln2_g=ln2_g, ln2_b=ln2_b)
    mom_m = dict(w_cond=m_w_cond, b_cond=m_b_cond, w_in=m_w_in, b_in=m_b_in, ssm_lambda_re=m_ssm_lambda_re, ssm_lambda_im=m_ssm_lambda_im, ssm_log_dt=m_ssm_log_dt, ssm_b_re=m_ssm_b_re, ssm_b_im=m_ssm_b_im, ssm_c_re=m_ssm_c_re, ssm_c_im=m_ssm_c_im, ssm_d=m_ssm_d, ssm_glu_w_a=m_ssm_glu_w_a, ssm_glu_w_b=m_ssm_glu_w_b, cv_dw_w=m_cv_dw_w, cv_dw_b=m_cv_dw_b, cv_ln_g=m_cv_ln_g, cv_ln_b=m_cv_ln_b, cv_w_pw=m_cv_w_pw, w_out=m_w_out, ln1_g=m_ln1_g, ln1_b=m_ln1_b, ffn_w_up=m_ffn_w_up, ffn_dw_w=m_ffn_dw_w, ffn_dw_b=m_ffn_dw_b, ffn_w_down=m_ffn_w_down, ln2_g=m_ln2_g, ln2_b=m_ln2_b)
    mom_v = dict(w_cond=v_w_cond, b_cond=v_b_cond, w_in=v_w_in, b_in=v_b_in, ssm_lambda_re=v_ssm_lambda_re, ssm_lambda_im=v_ssm_lambda_im, ssm_log_dt=v_ssm_log_dt, ssm_b_re=v_ssm_b_re, ssm_b_im=v_ssm_b_im, ssm_c_re=v_ssm_c_re, ssm_c_im=v_ssm_c_im, ssm_d=v_ssm_d, ssm_glu_w_a=v_ssm_glu_w_a, ssm_glu_w_b=v_ssm_glu_w_b, cv_dw_w=v_cv_dw_w, cv_dw_b=v_cv_dw_b, cv_ln_g=v_cv_ln_g, cv_ln_b=v_cv_ln_b, cv_w_pw=v_cv_w_pw, w_out=v_w_out, ln1_g=v_ln1_g, ln1_b=v_ln1_b, ffn_w_up=v_ffn_w_up, ffn_dw_w=v_ffn_dw_w, ffn_dw_b=v_ffn_dw_b, ffn_w_down=v_ffn_w_down, ln2_g=v_ln2_g, ln2_b=v_ln2_b)
    names = list(weights)

    s, d = x.shape[1], x.shape[2]
    sw = cw = d // 2
    fh = ffn_w_down.shape[1] * NDEV
    ng, nq = sw // SSM_GROUP, sw // QW
    gq = ng // nq
    alpha = 2.0 ** 0.25
    me = 4 * lax.axis_index("x") + 2 * lax.axis_index("y") + lax.axis_index("c")
    xs, tgt = x[0], loss_target[0]

    col_names = ["w_in", "ssm_glu_w_a", "ssm_glu_w_b", "cv_w_pw", "ffn_w_up"]
    row_names = ["w_out", "ffn_w_down"]
    big = col_names + row_names
    sent = lambda ns: [weights[n][0].astype(bf16) for n in ns]
    got_in, got_c, got_cv_taps, got_ffn_taps = _exchange(sent(["w_in"]) + [c, cv_dw_w[0, :, 0], ffn_dw_w[0, :, 0]],
                                                         scatter=False, name="gather_in")
    o1, o2, o3, o4 = sw, sw + cw, sw + 2 * cw, sw + 2 * cw + d
    in_bounds = ((0, o1), (o1, o2), (o2, o3), (o3, o4), (o4, o4 + d))
    w_u, w_cva, w_cvg, w_gs, w_gc = _unshard_cols(got_in, in_bounds, name="unshard_w_in")
    b_u, b_cva, b_cvg, b_gs, b_gc = (b_in[:, a:b] for a, b in in_bounds)
    c_all = got_c.reshape(NDEV, d)
    cv_taps = _cols_from_shards(got_cv_taps)
    ffn_taps = _cols_from_shards(got_ffn_taps)
    cv_w32 = _pad_rows(cv_taps, CONV_HALO)
    ffn_wa, ffn_wv = _pad_rows(ffn_taps[:, :fh], FFN_HALO), _pad_rows(ffn_taps[:, fh:], FFN_HALO)
    ffn_ba, ffn_bv = ffn_dw_b[:, :fh], ffn_dw_b[:, fh:]

    ncond = w_cond.shape[2]
    b_cond_mine = lax.dynamic_slice(b_cond, (0, me * ncond), (1, ncond))
    mod_cols = _cond_fwd(c_all, w_cond[0], b_cond_mine, name="cond_fwd")
    mod_all, = _exchange([mod_cols], scatter=False, name="gather_mod")
    mod_mine = lax.dynamic_slice(mod_all, (0, me, 0), (NDEV, 1, ncond)).reshape(1, 6 * d)
    sh1, sc1, g1, sh2, sc2, g2 = (mod_mine[:, k * d:(k + 1) * d] for k in range(6))

    lam_re, lam_im, log_dt = ssm_lambda_re[0], ssm_lambda_im[0], ssm_log_dt[0][:, None]
    lbr, lbi, cfr, cfi = _ssm_prep(lam_re, lam_im, log_dt, name="ssm_prep")
    rows_q = lambda a: a.reshape(nq, 1, QS)
    eye = jnp.eye(gq, dtype=f32)

    def b_mat(b):
        bt = b.reshape(nq, gq, SSM_STATE, SSM_GROUP).transpose(0, 1, 3, 2)
        return jnp.einsum("qgpn,gh->qgphn", bt, eye).reshape(nq, QW, QS)

    def c_mat(cc):
        ct = cc.reshape(nq, gq, SSM_GROUP, SSM_STATE)
        return jnp.einsum("qgpn,gh->qhngp", ct, eye).reshape(nq, QS, QW)

    def b_unmat(mt):
        return jnp.einsum("qgpgn->qgnp", mt.reshape(nq, gq, SSM_GROUP, gq, SSM_STATE)).reshape(ng, SSM_STATE, SSM_GROUP)

    def c_unmat(mt):
        return jnp.einsum("qgngp->qgpn", mt.reshape(nq, gq, SSM_STATE, gq, SSM_GROUP)).reshape(ng, SSM_GROUP, SSM_STATE)

    ssm_args = (rows_q(lbr), rows_q(lbi), b_mat(ssm_b_re[0]), b_mat(ssm_b_im[0]), c_mat(ssm_c_re[0]), c_mat(ssm_c_im[0]),
                rows_q(cfr), rows_q(cfi), ssm_d[0].reshape(1, sw))

    h1 = _lnmod(xs, sc1, sh1, name="ln_mod1")
    u, cva, cvg, gs, gc = _mm_fanout(h1, [w_u, w_cva, w_cvg, w_gs, w_gc], [b_u, b_cva, b_cvg, b_gs, b_gc],
                                     [f32, f32, f32, bf16, bf16], name="in_proj")
    v4, (got_a, got_b, got_pw, got_o) = _conv_fwd(
        cva, cvg, cv_w32, cv_dw_b, cv_ln_g, cv_ln_b, name="conv_fwd",
        xchg=(sent(["ssm_glu_w_a", "ssm_glu_w_b", "cv_w_pw", "w_out"]), False))
    h_p, yraw3, y, (got_up,) = _ssm_fwd(u, *ssm_args, name="ssm_fwd", xchg=(sent(["ffn_w_up"]), False))
    w_a, = _unshard_cols(got_a, ((0, d),), name="unshard_glu_a")
    w_b, = _unshard_cols(got_b, ((0, d),), name="unshard_glu_b")
    w_pw, = _unshard_cols(got_pw, ((0, d),), name="unshard_conv_pw")
    w_upa, w_upv = _unshard_cols(got_up, ((0, fh), (fh, 2 * fh)), name="unshard_ffn_up")
    w_o = got_o.reshape(d, d)
    ya = _mm([(y, w_a)], out_dtype=bf16, name="glu_a")
    yb = _mm([(y, w_b)], out_dtype=bf16, name="glu_b")
    ycv = _mm([(v4, w_pw)], out_dtype=bf16, name="conv_pw")
    merged = _glu_merge(ya, yb, ycv, gs, gc, name="merge")
    o = _mm([(merged, w_o)], out_dtype=bf16, name="out_proj")
    x1, h2 = _resid_ln_mod(xs, o, g1, ln1_g, ln1_b, sc2, sh2, alpha, name="resid_ln1")
    upa = _mm([(h2, w_upa)], name="ffn_up_a")
    upv = _mm([(h2, w_upv)], name="ffn_up_v")
    f, (got_dn,) = _ffn_mid(upa, upv, ffn_wa, ffn_wv, ffn_ba, ffn_bv, name="ffn_mid", xchg=(sent(["ffn_w_down"]), False))
    w_dn = got_dn.reshape(fh, d)
    y2 = _mm([(f, w_dn)], name="ffn_down")
    dr2, dy2, loss_part, d_ln2_g, d_ln2_b, d_g2 = _resid_ln_loss(x1, y2, g2, ln2_g, ln2_b, tgt, alpha, name="resid_ln2_loss")

    gw = {}
    df = _mm([(dy2, w_dn)], trans_w=True, name="d_ffn_down")
    gw["ffn_w_down"] = _mm_tn(f, dy2, out_dtype=bf16, name="g_ffn_down").reshape((NDEV,) + ffn_w_down[0].shape)
    received = {}
    dupa, dupv, d_ffn_wa, d_ffn_wv, d_ffn_ba, d_ffn_bv, (received["ffn_w_down"],) = _ffn_mid_bwd(
        df, upa, upv, ffn_wa, ffn_wv, ffn_ba, ffn_bv, name="ffn_mid_bwd", xchg=([gw["ffn_w_down"]], True))
    dh2 = _mm([(dupa, w_upa), (dupv, w_upv)], trans_w=True, out_dtype=bf16, name="d_ffn_up")
    gw["ffn_w_up"] = _shard_cols([_mm_tn(h2, dupa, name="g_ffn_up_a"), _mm_tn(h2, dupv, name="g_ffn_up_v")], out_dtype=bf16,
                                 name="shard_ffn_up")
    dr1, do, d_sc2, d_sh2, d_ln1_g, d_ln1_b, d_g1 = _mid_bwd(dh2, x1, dr2, xs, o, g1, sc2, ln1_g, alpha, name="mid_bwd")
    dmerged = _mm([(do, w_o)], trans_w=True, out_dtype=bf16, name="d_out_proj")
    gw["w_out"] = _mm_tn(merged, do, out_dtype=bf16, name="g_out_proj").reshape((NDEV,) + w_out[0].shape)
    dya, dyb, dycv, dgs, dgc, s_gs, s_gc = _glu_merge_bwd(dmerged, ya, yb, ycv, gs, gc, name="merge_bwd")
    dy = _mm([(dya, w_a), (dyb, w_b)], trans_w=True, out_dtype=bf16, name="d_glu")
    gw["ssm_glu_w_a"] = _shard_cols([_mm_tn(y, dya, name="g_glu_a")], out_dtype=bf16, name="shard_glu_a")
    gw["ssm_glu_w_b"] = _shard_cols([_mm_tn(y, dyb, name="g_glu_b")], out_dtype=bf16, name="shard_glu_b")
    dv4 = _mm([(dycv, w_pw)], trans_w=True, out_dtype=bf16, name="d_conv_pw")
    gw["cv_w_pw"] = _shard_cols([_mm_tn(v4, dycv, name="g_conv_pw")], out_dtype=bf16, name="shard_conv_pw")
    dv2, d_cv_ln_g, d_cv_ln_b = _conv_bwd_ln(dv4, cva, cvg, cv_w32, cv_dw_b, cv_ln_g, cv_ln_b, name="conv_bwd_ln")
    dcva, dcvg, d_cv_w32, d_cv_b, s_cva, s_cvg, (received["ffn_w_up"],) = _conv_bwd_taps(
        dv2, cva, cvg, cv_w32, name="conv_bwd_taps", xchg=([gw["ffn_w_up"]], True))
    late = ["w_out", "ssm_glu_w_a", "ssm_glu_w_b", "cv_w_pw"]
    (du, d_bre_m, d_bim_m, d_cre_m, d_cim_m, d_cfr, d_cfi, d_lbr, d_lbi, d_d, s_u, got_late) = _ssm_bwd(
        dy, yraw3, u, h_p, *ssm_args, name="ssm_bwd", xchg=([gw[n] for n in late], True))
    received.update(zip(late, got_late))
    gshape = lam_re.shape
    d_lam_re, d_lam_im, d_log_dt = _ssm_prep_bwd(
        lam_re, lam_im, log_dt, [a.reshape(gshape) for a in (d_lbr, d_lbi, d_cfr, d_cfi)], name="ssm_prep_bwd")
    small = {
        "b_in": jnp.concatenate([s_u, s_cva, s_cvg, s_gs, s_gc], axis=1),
        "ssm_lambda_re": d_lam_re, "ssm_lambda_im": d_lam_im, "ssm_log_dt": d_log_dt,
        "ssm_b_re": b_unmat(d_bre_m), "ssm_b_im": b_unmat(d_bim_m), "ssm_c_re": c_unmat(d_cre_m), "ssm_c_im": c_unmat(d_cim_m),
        "ssm_d": d_d, "cv_dw_w": d_cv_w32[:CONV_K], "cv_dw_b": d_cv_b, "cv_ln_g": d_cv_ln_g, "cv_ln_b": d_cv_ln_b,
        "ln1_g": d_ln1_g, "ln1_b": d_ln1_b,
        "ffn_dw_w": jnp.concatenate([d_ffn_wa[:FFN_K], d_ffn_wv[:FFN_K]], axis=1),
        "ffn_dw_b": jnp.concatenate([d_ffn_ba, d_ffn_bv], axis=1), "ln2_g": d_ln2_g, "ln2_b": d_ln2_b,
        "mod_g1": d_g1, "mod_sh2": d_sh2, "mod_sc2": d_sc2, "mod_g2": d_g2, "loss": loss_part,
    }
    small_names = list(small)
    small_shapes = [small[n].shape for n in small_names]
    gw["w_in"], (small_all,) = _mm_tn_sharded(h1, [du, dcva, dcvg, dgs, dgc], out_dtype=bf16, name="g_in",
                                              xchg=([_pack([small[n] for n in small_names])], False))
    dh1, (received["w_in"],) = _mm(
        [(du, w_u), (dcva, w_cva), (dcvg, w_cvg), (dgs, w_gs), (dgc, w_gc)], trans_w=True, out_dtype=bf16, name="d_in",
        xchg=([gw["w_in"]], True))
    grad_x, d_sc1, d_sh1, _ = _final_bwd(dh1, xs, dr1, sc1, alpha, name="final_bwd")

    grads, delta, new_m, new_v = {}, {}, {}, {}
    for n in big:
        ride = ([_pack([d_sh1, d_sc1])], False) if n == big[0] else None
        res = _sum_adamw(received[n], weights[n][0], mom_m[n][0], mom_v[n][0], name="adamw_" + n, xchg=ride)
        grads[n], delta[n], new_m[n], new_v[n] = res[:4]
        if ride is not None:
            last_all, = res[4]

    small_sum = dict(zip(small_names, _unpack(_sum_parts(small_all, name="sum_small").reshape(-1), small_shapes)))
    last_sum = _unpack(_sum_parts(last_all, name="sum_last").reshape(-1), [(1, d), (1, d)])
    per_dev = dict(zip(small_names, _unpack(small_all.reshape(NDEV, -1), small_shapes)))
    last_dev = _unpack(last_all.reshape(NDEV, -1), [(1, d), (1, d)])
    dmod_all = jnp.concatenate(last_dev + [per_dev[k] for k in ("mod_g1", "mod_sh2", "mod_sc2", "mod_g2")], axis=-1).reshape(NDEV, 6 * d)
    dmod_cols = lax.dynamic_slice(dmod_all.reshape(NDEV, NDEV, ncond), (0, me, 0), (NDEV, 1, ncond)).reshape(NDEV, ncond)
    grads["w_cond"] = _cond_bwd(c_all, dmod_cols, name="cond_bwd")
    loss = small_sum.pop("loss").reshape(())
    grads["b_cond"] = jnp.concatenate(last_sum + [small_sum.pop(k) for k in ("mod_g1", "mod_sh2", "mod_sc2", "mod_g2")], axis=1)
    for n, g in small_sum.items():
        grads[n] = g
    ntap = cv_dw_w.shape[3]
    grads["cv_dw_w"] = lax.dynamic_slice(grads["cv_dw_w"], (0, me * ntap), (CONV_K, ntap))
    nffn = ffn_dw_w.shape[3]
    grads["ffn_dw_w"] = lax.dynamic_slice(grads["ffn_dw_w"], (0, me * nffn), (FFN_K, nffn))
    grads = {n: grads[n].reshape(weights[n].shape) for n in names}

    delta["w_cond"], new_m["w_cond"], new_v["w_cond"] = _adamw(w_cond[0], grads["w_cond"][0], m_w_cond[0], v_w_cond[0],
                                                               name="adamw_w_cond")
    rest = [n for n in names if n not in ["w_cond"] + big]
    rest_shapes = [weights[n].shape for n in rest]
    packed = [_pack([t[n] for n in rest]) for t in (weights, grads, mom_m, mom_v)]
    for tgt_dict, res in zip((delta, new_m, new_v), _adamw(*packed, name="adamw_small")):
        for n, a in zip(rest, _unpack(res.reshape(-1), rest_shapes)):
            tgt_dict[n] = a
    shaped = lambda t: [t[n].reshape(weights[n].shape) for n in names]

    return (loss, grad_x[None], *shaped(grads), *shaped(delta), *shaped(new_m), *shaped(new_v))
```

```python
import functools
import math

import jax
import jax.numpy as jnp
from jax import lax
from jax.experimental import pallas as pl
from jax.experimental.pallas import tpu as pltpu

f32 = jnp.float32
bf16 = jnp.bfloat16

NDEV = 8
LANES = 128
SUBLANES = 8
SSM_GROUP = 16
SSM_STATE = 64
QW = 128
QS = 512
CONV_K = 31
CONV_HALO = 32
FFN_K = 3
FFN_HALO = 8
LN_EPS = 1e-5
ADAM_LR, ADAM_B1, ADAM_B2, ADAM_EPS, ADAM_WD, ADAM_STEP = 0.001, 0.9, 0.999, 1e-08, 0.01, 10
VMEM_LIMIT = 56 * 1024 * 1024
W_TILE_BYTES = 6 * 1024 * 1024
SUM_ROWS = 512
EW_BLOCK_BYTES = 2 * 1024 * 1024
INV_SQRT2 = 1.0 / math.sqrt(2.0)
INV_SQRT_2PI = 1.0 / math.sqrt(2.0 * math.pi)
MESH = pl.DeviceIdType.MESH


def _tile(n, want):
    t = min(n, want)
    while n % t:
        t //= 2
    return t


def _col_tile(n, rows, bytes_per):
    best = LANES if n % LANES == 0 else n
    for t in range(LANES, n + 1, LANES):
        if n % t == 0 and rows * t * bytes_per <= W_TILE_BYTES:
            best = t
    return best


def _params(*sem):
    return pltpu.CompilerParams(dimension_semantics=sem, vmem_limit_bytes=VMEM_LIMIT)


def _row(i):
    return (0, 0)


def _full(shape):
    nd = len(shape)
    return pl.BlockSpec(shape, lambda *a: (0,) * nd)


def _ln(x):
    mu = jnp.mean(x, axis=-1, keepdims=True)
    xc = x - mu
    var = jnp.mean(xc * xc, axis=-1, keepdims=True)
    rstd = lax.rsqrt(var + LN_EPS)
    return xc * rstd, rstd


def _ln_bwd(dxhat, xhat, rstd):
    return rstd * (dxhat - jnp.mean(dxhat, axis=-1, keepdims=True) - xhat * jnp.mean(dxhat * xhat, axis=-1, keepdims=True))


def _sig(x):
    return 1.0 / (1.0 + jnp.exp(-x))


def _gelu(x):
    return 0.5 * x * (1.0 + lax.erf(x * INV_SQRT2))


def _gelu_grad(x):
    return 0.5 * (1.0 + lax.erf(x * INV_SQRT2)) + x * jnp.exp(-0.5 * x * x) * INV_SQRT_2PI


def _colsum(x):
    return jnp.sum(x, axis=0, keepdims=True)


def _mm(pairs, bias=None, *, trans_w=False, out_dtype=f32, name, xchg=None):
    n_p = len(pairs)
    m = pairs[0][0].shape[0]
    n = pairs[0][1].shape[0 if trans_w else 1]
    ktot = sum(x.shape[1] for x, _ in pairs)
    tm = _tile(m, 512)
    tn = _col_tile(n, ktot, 2)
    dn = (((1,), (1,)), ((), ())) if trans_w else (((1,), (0,)), ((), ()))

    def body(*refs):
        o_ref = refs[-1]
        acc = None
        for xr, wr in zip(refs[:n_p], refs[n_p:2 * n_p]):
            r = lax.dot_general(xr[...].astype(bf16), wr[...].astype(bf16), dn, preferred_element_type=f32)
            acc = r if acc is None else acc + r
        if bias is not None:
            acc = acc + refs[2 * n_p][...]
        o_ref[...] = acc.astype(out_dtype)

    in_specs = [pl.BlockSpec((tm, x.shape[1]), lambda j, i: (i, 0)) for x, _ in pairs]
    if trans_w:
        in_specs += [pl.BlockSpec((tn, w.shape[1]), lambda j, i: (j, 0)) for _, w in pairs]
    else:
        in_specs += [pl.BlockSpec((w.shape[0], tn), lambda j, i: (0, j)) for _, w in pairs]
    args = [x for x, _ in pairs] + [w for _, w in pairs]
    if bias is not None:
        in_specs.append(pl.BlockSpec((1, tn), lambda j, i: (0, j)))
        args.append(bias)
    (out,), got = _call(
        body, args, name=name, grid=(n // tn, m // tm), in_specs=in_specs,
        out_specs=[pl.BlockSpec((tm, tn), lambda j, i: (i, j))], out_shape=[jax.ShapeDtypeStruct((m, n), out_dtype)],
        sem=("parallel", "arbitrary"), xchg=xchg)
    return out if xchg is None else (out, got)


def _mm_fanout(x, ws, biases, out_dtypes, *, name):
    m, k = x.shape
    tm = _tile(m, 512)
    n_w = len(ws)

    def body(x_ref, *refs):
        xb = x_ref[...].astype(bf16)
        for w_ref, b_ref, o_ref, dt in zip(refs[:n_w], refs[n_w:2 * n_w], refs[2 * n_w:], out_dtypes):
            o_ref[...] = (jnp.dot(xb, w_ref[...], preferred_element_type=f32) + b_ref[...]).astype(dt)

    return pl.pallas_call(
        body, name=name, grid=(m // tm,),
        in_specs=[pl.BlockSpec((tm, k), lambda i: (i, 0))] + [_full(w.shape) for w in ws] + [_full(b.shape) for b in biases],
        out_specs=[pl.BlockSpec((tm, w.shape[1]), lambda i: (i, 0)) for w in ws],
        out_shape=[jax.ShapeDtypeStruct((m, w.shape[1]), dt) for w, dt in zip(ws, out_dtypes)],
        compiler_params=_params("parallel"))(x, *ws, *biases)


def _mm_tn(x, dy, *, out_dtype=f32, name):
    m, k = x.shape
    n = dy.shape[1]
    tm = _tile(m, 512)
    tn = _col_tile(n, k, 4)
    steps = m // tm

    def body(x_ref, dy_ref, o_ref, *scratch):
        acc = scratch[0] if scratch else o_ref

        @pl.when(pl.program_id(1) == 0)
        def _():
            acc[...] = jnp.zeros_like(acc)

        acc[...] += lax.dot_general(x_ref[...].astype(bf16), dy_ref[...].astype(bf16), (((0,), (0,)), ((), ())),
                                    preferred_element_type=f32)
        if scratch:
            @pl.when(pl.program_id(1) == steps - 1)
            def _():
                o_ref[...] = acc[...].astype(out_dtype)

    return pl.pallas_call(
        body, name=name, grid=(n // tn, steps),
        in_specs=[pl.BlockSpec((tm, k), lambda j, i: (i, 0)), pl.BlockSpec((tm, tn), lambda j, i: (i, j))],
        out_specs=pl.BlockSpec((k, tn), lambda j, i: (0, j)),
        out_shape=jax.ShapeDtypeStruct((k, n), out_dtype),
        scratch_shapes=[] if out_dtype == f32 else [pltpu.VMEM((k, tn), f32)],
        compiler_params=_params("parallel", "arbitrary"),
    )(x, dy)


def _mm_tn_sharded(x, dys, *, out_dtype, name, xchg=None):
    m, k = x.shape
    widths = [dy.shape[1] for dy in dys]
    n = sum(widths) // NDEV
    tm = _tile(m, 512)
    steps = m // tm
    n_d = len(dys)

    def body(x_ref, *refs):
        dy_refs, o_ref, acc = refs[:n_d], refs[n_d], refs[n_d + 1]
        i = pl.program_id(0)

        @pl.when(i == 0)
        def _():
            acc[...] = jnp.zeros_like(acc)

        xb = x_ref[...].astype(bf16)
        off = 0
        for dy_ref, w in zip(dy_refs, widths):
            acc[:, off:off + w] += lax.dot_general(xb, dy_ref[...].astype(bf16), (((0,), (0,)), ((), ())), preferred_element_type=f32)
            off += w

        @pl.when(i == steps - 1)
        def _():
            for j in range(NDEV):
                o_ref[j] = acc[:, n * j:n * (j + 1)].astype(out_dtype)

    (out,), got = _call(
        body, (x, *dys), name=name, grid=(steps,),
        in_specs=[pl.BlockSpec((tm, k), lambda i: (i, 0))] + [pl.BlockSpec((tm, w), lambda i: (i, 0)) for w in widths],
        out_specs=[pl.BlockSpec((NDEV, k, n), lambda i: (0, 0, 0))], out_shape=[jax.ShapeDtypeStruct((NDEV, k, n), out_dtype)],
        scratch_shapes=[pltpu.VMEM((k, sum(widths)), f32)], sem=("arbitrary",), xchg=xchg)
    return out if xchg is None else (out, got)


def _exchange(arrs, *, scatter, name):
    n = len(arrs)

    def body(*refs):
        _exchange_copies(refs[:n], refs[n:2 * n], refs[2 * n:], scatter, True, True)

    return pl.pallas_call(
        body, name=name, in_specs=[HBM_SPEC] * n, out_specs=[HBM_SPEC] * n, out_shape=_exchange_out_shape(arrs, scatter),
        scratch_shapes=_exchange_sems(n),
    )(*arrs)


HBM_SPEC = pl.BlockSpec(memory_space=pltpu.HBM)


def _flags(scatter, n):
    return list(scatter) if isinstance(scatter, (list, tuple)) else [scatter] * n


def _exchange_out_shape(arrs, scatter):
    return [jax.ShapeDtypeStruct(a.shape if sc else (NDEV,) + a.shape, a.dtype) for a, sc in zip(arrs, _flags(scatter, len(arrs)))]


def _exchange_sems(n):
    return [pltpu.SemaphoreType.DMA(((NDEV - 1) * n,)), pltpu.SemaphoreType.DMA(((NDEV - 1) * n,)), pltpu.SemaphoreType.DMA((n,))]


def _exchange_copies(x_refs, o_refs, sems, scatter, start, wait):
    n = len(x_refs)
    flags = _flags(scatter, n)
    send_sems, recv_sems, local_sems = sems
    ix, iy, ic = lax.axis_index("x"), lax.axis_index("y"), lax.axis_index("c")
    me = 4 * ix + 2 * iy + ic
    local = [pltpu.make_async_copy(x.at[me] if sc else x, o.at[me], local_sems.at[a])
             for a, (x, o, sc) in enumerate(zip(x_refs, o_refs, flags))]

    def peer(k):
        return (1 - ix if k & 4 else ix, 1 - iy if k & 2 else iy, 1 - ic if k & 1 else ic)

    def index(p):
        return 4 * p[0] + 2 * p[1] + p[2]

    def copy(a, k, src, dst, to):
        sem = (k - 1) * n + a
        return pltpu.make_async_remote_copy(src_ref=src, dst_ref=dst, send_sem=send_sems.at[sem], recv_sem=recv_sems.at[sem],
                                            device_id=to, device_id_type=MESH)

    sends, arrivals, passed_on = [], [], []
    for a, (x, o, sc) in enumerate(zip(x_refs, o_refs, flags)):
        if sc:
            for k in range(1, NDEV):
                p = peer(k)
                sends.append(copy(a, k, x.at[index(p)], o.at[me], p))
                arrivals.append(copy(a, k, x.at[me], o.at[index(p)], p))
        else:
            sib = peer(1)
            sends.append(copy(a, 1, x, o.at[me], sib))
            arrivals.append(copy(a, 1, x, o.at[index(sib)], sib))
            for k in (2, 4, 6):
                p, q = peer(k), peer(k + 1)
                sends.append(copy(a, k, x, o.at[me], p))
                passed_on.append((copy(a, k, x, o.at[index(p)], p), copy(a, k + 1, o.at[index(p)], o.at[index(p)], sib)))
                arrivals.append(copy(a, k + 1, o.at[index(q)], o.at[index(q)], sib))
    if start:
        for cp in local + sends:
            cp.start()
    if wait:
        for landed, hand_over in passed_on:
            landed.wait_recv()
            hand_over.start()
        for cp in arrivals:
            cp.wait_recv()
        for cp in sends + [hand_over for _, hand_over in passed_on]:
            cp.wait_send()
        for cp in local:
            cp.wait()


def _call(body, args, *, name, grid, in_specs, out_specs, out_shape, scratch_shapes=(), sem, xchg=None):
    if xchg is None:
        return pl.pallas_call(body, name=name, grid=grid, in_specs=in_specs, out_specs=out_specs, out_shape=out_shape,
                              scratch_shapes=list(scratch_shapes), compiler_params=_params(*sem))(*args), None
    arrs, scatter = xchg
    n, ni, no, ns = len(arrs), len(in_specs), len(out_specs), len(scratch_shapes)

    def wrapped(*refs):
        ins, x_refs = refs[:ni], refs[ni:ni + n]
        outs, o_refs = refs[ni + n:ni + n + no], refs[ni + n + no:ni + 2 * n + no]
        scratch, sems = refs[ni + 2 * n + no:ni + 2 * n + no + ns], refs[ni + 2 * n + no + ns:]
        ids = [pl.program_id(a) for a in range(len(grid))]
        first = functools.reduce(jnp.logical_and, [p == 0 for p in ids])
        last = functools.reduce(jnp.logical_and, [p == g - 1 for p, g in zip(ids, grid)])

        @pl.when(first)
        def _():
            _exchange_copies(x_refs, o_refs, sems, scatter, True, False)

        body(*ins, *outs, *scratch)

        @pl.when(last)
        def _():
            _exchange_copies(x_refs, o_refs, sems, scatter, False, True)

    res = pl.pallas_call(
        wrapped, name=name, grid=grid, in_specs=list(in_specs) + [HBM_SPEC] * n, out_specs=list(out_specs) + [HBM_SPEC] * n,
        out_shape=list(out_shape) + _exchange_out_shape(arrs, scatter),
        scratch_shapes=list(scratch_shapes) + _exchange_sems(n),
        compiler_params=_params(*("arbitrary",) * len(grid)))(*args, *arrs)
    return res[:no], res[no:]


def _sum_parts(parts, *, name):
    r = parts.shape[1]

    def body(p_ref, o_ref):
        acc = p_ref[0]
        for j in range(1, NDEV):
            acc = acc + p_ref[j]
        o_ref[...] = acc

    return pl.pallas_call(body, name=name, out_shape=jax.ShapeDtypeStruct((r, LANES), f32), compiler_params=_params())(parts)


def _col_pieces(n, bounds):
    out = []
    for p, (a, b) in enumerate(bounds):
        for j in range(NDEV):
            lo, hi = max(a, n * j), min(b, n * (j + 1))
            if lo < hi:
                out.append((p, j, lo - a, lo - n * j, hi - lo))
    return out


def _unshard_cols(stacked, bounds, *, name):
    _, k, n = stacked.shape
    tk = _tile(k, 256)
    plan = _col_pieces(n, bounds)

    def body(x_ref, *o_refs):
        for p, j, po, so, w in plan:
            o_refs[p][:, po:po + w] = x_ref[j, :, so:so + w]

    return pl.pallas_call(
        body, name=name, grid=(k // tk,), in_specs=[pl.BlockSpec((NDEV, tk, n), lambda i: (0, i, 0))],
        out_specs=[pl.BlockSpec((tk, b - a), lambda i: (i, 0)) for a, b in bounds],
        out_shape=[jax.ShapeDtypeStruct((k, b - a), stacked.dtype) for a, b in bounds],
        compiler_params=_params("parallel"))(stacked)


def _shard_cols(pieces, *, out_dtype, name):
    k = pieces[0].shape[0]
    bounds, off = [], 0
    for p in pieces:
        bounds.append((off, off + p.shape[1]))
        off += p.shape[1]
    n = off // NDEV
    tk = _tile(k, 256)
    plan = _col_pieces(n, bounds)

    def body(*refs):
        o_ref = refs[-1]
        for p, j, po, so, w in plan:
            o_ref[j, :, so:so + w] = refs[p][:, po:po + w].astype(out_dtype)

    return pl.pallas_call(
        body, name=name, grid=(k // tk,), in_specs=[pl.BlockSpec((tk, b - a), lambda i: (i, 0)) for a, b in bounds],
        out_specs=pl.BlockSpec((NDEV, tk, n), lambda i: (0, i, 0)),
        out_shape=jax.ShapeDtypeStruct((NDEV, k, n), out_dtype),
        compiler_params=_params("parallel"))(*pieces)


def _pack(arrs):
    flat = jnp.concatenate([a.reshape(-1) for a in arrs])
    pad = (-flat.shape[0]) % (SUBLANES * LANES)
    return jnp.pad(flat, (0, pad)).reshape(-1, LANES)


def _unpack(flat, shapes):
    out, off = [], 0
    for s in shapes:
        n = math.prod(s)
        out.append(flat[..., off:off + n].reshape(flat.shape[:-1] + tuple(s)))
        off += n
    return out


def _adamw_math(w, gg, m, v):
    nm = ADAM_B1 * m + (1.0 - ADAM_B1) * gg
    nv = ADAM_B2 * v + (1.0 - ADAM_B2) * (gg * gg)
    m_hat = nm / (1.0 - ADAM_B1 ** ADAM_STEP)
    v_hat = nv / (1.0 - ADAM_B2 ** ADAM_STEP)
    return -ADAM_LR * (m_hat / (jnp.sqrt(v_hat) + ADAM_EPS) + ADAM_WD * w), nm, nv


def _row_block(r, c, copies):
    tr = r
    while copies * tr * c * 4 > EW_BLOCK_BYTES and tr % (4 * SUBLANES) == 0:
        tr //= 2
    return tr


def _adamw(w, g, m, v, *, name):
    r, c = w.shape
    tr = _row_block(r, c, 1)

    def body(w_ref, g_ref, m_ref, v_ref, d_ref, nm_ref, nv_ref):
        d_ref[...], nm_ref[...], nv_ref[...] = _adamw_math(w_ref[...], g_ref[...], m_ref[...], v_ref[...])

    spec = pl.BlockSpec((tr, c), lambda i: (i, 0))
    shp = jax.ShapeDtypeStruct((r, c), f32)
    return pl.pallas_call(
        body, name=name, grid=(r // tr,), in_specs=[spec] * 4, out_specs=[spec] * 3, out_shape=[shp] * 3,
        compiler_params=_params("parallel"),
    )(w, g, m, v)


def _sum_adamw(parts, w, m, v, *, name, xchg=None):
    r, c = w.shape
    tr = _row_block(r, c, NDEV)

    def body(p_ref, w_ref, m_ref, v_ref, g_ref, d_ref, nm_ref, nv_ref):
        gg = p_ref[0].astype(f32)
        for j in range(1, NDEV):
            gg = gg + p_ref[j].astype(f32)
        g_ref[...] = gg
        d_ref[...], nm_ref[...], nv_ref[...] = _adamw_math(w_ref[...], gg, m_ref[...], v_ref[...])

    spec = pl.BlockSpec((tr, c), lambda i: (i, 0))
    shp = jax.ShapeDtypeStruct((r, c), f32)
    res, got = _call(
        body, (parts, w, m, v), name=name, grid=(r // tr,),
        in_specs=[pl.BlockSpec((NDEV, tr, c), lambda i: (0, i, 0))] + [spec] * 3,
        out_specs=[spec] * 4, out_shape=[shp] * 4, sem=("parallel",), xchg=xchg)
    return tuple(res) if xchg is None else tuple(res) + (got,)


def _cond_fwd(c_all, w, b, *, name):
    nb, n = c_all.shape[0], w.shape[1]

    def body(c_ref, w_ref, b_ref, o_ref):
        cc = c_ref[...]
        o_ref[...] = jnp.dot(cc * _sig(cc), w_ref[...], preferred_element_type=f32,
                             precision=lax.Precision.HIGHEST) + b_ref[...]

    return pl.pallas_call(body, name=name, out_shape=jax.ShapeDtypeStruct((nb, n), f32),
                          compiler_params=_params())(c_all, w, b)


def _cond_bwd(c_all, dmod, *, name):
    d, n = c_all.shape[1], dmod.shape[1]

    def body(c_ref, g_ref, o_ref):
        cc = c_ref[...]
        o_ref[...] = lax.dot_general(cc * _sig(cc), g_ref[...], (((0,), (0,)), ((), ())), preferred_element_type=f32,
                                     precision=lax.Precision.HIGHEST)

    return pl.pallas_call(body, name=name, out_shape=jax.ShapeDtypeStruct((d, n), f32),
                          compiler_params=_params())(c_all, dmod)


def _ssm_disc(lam_re, lam_im, log_dt):
    lr = jnp.minimum(lam_re, -1e-4)
    li = lam_im
    dt = jnp.exp(log_dt)
    mag = jnp.exp(lr * dt)
    ang = li * dt
    lbr, lbi = mag * jnp.cos(ang), mag * jnp.sin(ang)
    num_r, num_i = lbr - 1.0, lbi
    den = lr * lr + li * li
    return lbr, lbi, (num_r * lr + num_i * li) / den, (num_i * lr - num_r * li) / den


def _ssm_prep(lam_re, lam_im, log_dt, *, name):
    def body(a, b, c, o1, o2, o3, o4):
        o1[...], o2[...], o3[...], o4[...] = _ssm_disc(a[...], b[...], c[...])

    shp = jax.ShapeDtypeStruct(lam_re.shape, f32)
    return pl.pallas_call(body, name=name, out_shape=[shp] * 4, compiler_params=_params())(lam_re, lam_im, log_dt)


def _ssm_prep_bwd(lam_re, lam_im, log_dt, cts, *, name):
    def body(a, b, c, g1, g2, g3, g4, o1, o2, o3):
        _, vjp = jax.vjp(_ssm_disc, a[...], b[...], c[...])
        o1[...], o2[...], o3[...] = vjp((g1[...], g2[...], g3[...], g4[...]))

    shp = jax.ShapeDtypeStruct(lam_re.shape, f32)
    return pl.pallas_call(body, name=name, out_shape=[shp, shp, jax.ShapeDtypeStruct(log_dt.shape, f32)],
                          compiler_params=_params())(lam_re, lam_im, log_dt, *cts)


def _step_major(x3):
    k, nt, c = x3.shape
    return jnp.swapaxes(x3, 0, 1).reshape(k * nt, c)


def _chunk_major(x2, nt):
    return jnp.swapaxes(x2.reshape(nt, SUBLANES, x2.shape[1]), 0, 1)


def _chain_carries(loc_r, loc_i, pr, pi_, forward):
    row = lax.broadcasted_iota(jnp.int32, loc_r.shape, 0)
    shift = 1 if forward else SUBLANES - 1
    order = range(1, SUBLANES) if forward else range(SUBLANES - 2, -1, -1)
    er, ei = loc_r, loc_i
    for k in order:
        sr, si = pltpu.roll(er, shift, 0), pltpu.roll(ei, shift, 0)
        er = jnp.where(row == k, loc_r + pr * sr - pi_ * si, er)
        ei = jnp.where(row == k, loc_i + pr * si + pi_ * sr, ei)
    edge = 0 if forward else SUBLANES - 1
    return (jnp.where(row == edge, 0.0, pltpu.roll(er, shift, 0)), jnp.where(row == edge, 0.0, pltpu.roll(ei, shift, 0)))


def _chunk_power(ar, ai, chunk_len):
    pr, pi_ = ar, ai
    for _ in range(int(math.log2(chunk_len))):
        pr, pi_ = pr * pr - pi_ * pi_, 2.0 * pr * pi_
    return pr, pi_


def _ssm_mats(bre_ref, bim_ref, cre_ref, cim_ref, cfr_ref, cfi_ref, bbar_s, cmat_s, nq):
    for q in range(nq):
        cr, ci, br, bi = cfr_ref[q], cfi_ref[q], bre_ref[q], bim_ref[q]
        bbar_s[q, :, 0:QS] = (cr * br - ci * bi).astype(bf16)
        bbar_s[q, :, QS:2 * QS] = (cr * bi + ci * br).astype(bf16)
        cmat_s[q, 0:QS, :] = cre_ref[q].astype(bf16)
        cmat_s[q, QS:2 * QS, :] = (-cim_ref[q]).astype(bf16)


def _ssm_fwd(u, ar, ai, bre, bim, cre, cim, cfr, cfi, dvec, *, name, xchg=None):
    s, sw = u.shape
    nq = sw // QW
    st = nq * 2 * QS
    tb = _tile(s, 256)
    nb, nt, chunk_len = s // tb, tb // SUBLANES, s // SUBLANES
    assert chunk_len & (chunk_len - 1) == 0 and nt % 16 == 0

    def body(u_ref, ar_ref, ai_ref, bre_ref, bim_ref, cre_ref, cim_ref, cfr_ref, cfi_ref, d_ref,
             h_out, yraw_out, y_out, buf, hc, bbar_s, cmat_s):
        ph, i = pl.program_id(0), pl.program_id(1)

        @pl.when(i == 0)
        def _():
            _ssm_mats(bre_ref, bim_ref, cre_ref, cim_ref, cfr_ref, cfi_ref, bbar_s, cmat_s, nq)

        @pl.when((ph == 0) & (i == 0))
        def _():
            hc[...] = jnp.zeros_like(hc)

        @pl.when((ph == 1) & (i == 0))
        def _():
            for q in range(nq):
                o = q * 2 * QS
                pr, pi_ = _chunk_power(ar_ref[q], ai_ref[q], chunk_len)
                sr, si = _chain_carries(hc[:, o:o + QS], hc[:, o + QS:o + 2 * QS], pr, pi_, True)
                hc[:, o:o + QS] = sr
                hc[:, o + QS:o + 2 * QS] = si

        uu = u_ref[...]
        up = _step_major(uu).astype(bf16)
        for q in range(nq):
            o = q * 2 * QS
            buf[:, o:o + 2 * QS] = jnp.dot(up[:, q * QW:(q + 1) * QW], bbar_s[q], preferred_element_type=f32)

        for q in range(nq):
            o = q * 2 * QS
            a_r = jnp.broadcast_to(ar_ref[q], (SUBLANES, QS))
            a_i = jnp.broadcast_to(ai_ref[q], (SUBLANES, QS))

            def step(t, carry, o=o, a_r=a_r, a_i=a_i):
                hr, hi = carry
                r0 = pl.multiple_of(t * SUBLANES, SUBLANES)
                nr = a_r * hr - a_i * hi + buf[pl.ds(r0, SUBLANES), o:o + QS]
                ni = a_r * hi + a_i * hr + buf[pl.ds(r0, SUBLANES), o + QS:o + 2 * QS]
                buf[pl.ds(r0, SUBLANES), o:o + QS] = nr
                buf[pl.ds(r0, SUBLANES), o + QS:o + 2 * QS] = ni
                return nr, ni

            hr, hi = lax.fori_loop(0, nt, step, (hc[:, o:o + QS], hc[:, o + QS:o + 2 * QS]))
            hc[:, o:o + QS] = hr
            hc[:, o + QS:o + 2 * QS] = hi

        @pl.when(ph == 1)
        def _():
            for q in range(nq):
                o = q * 2 * QS
                cs = slice(q * QW, (q + 1) * QW)
                hq = buf[:, o:o + 2 * QS].astype(bf16)
                h_out[:, o:o + 2 * QS] = hq
                yq = _chunk_major(jnp.dot(hq, cmat_s[q], preferred_element_type=f32), nt) + d_ref[:, cs] * uu[:, :, cs]
                yraw_out[:, :, cs] = yq
                y_out[:, :, cs] = _gelu(yq).astype(bf16)

    blk = lambda ph, i: (0, i, 0)
    oblk = lambda ph, i: (0, i * ph, 0)
    act = lambda dt: jax.ShapeDtypeStruct((SUBLANES, chunk_len, sw), dt)
    (h_p, yraw3, y3), got = _call(
        body, (u.reshape(SUBLANES, chunk_len, sw), ar, ai, bre, bim, cre, cim, cfr, cfi, dvec), name=name, grid=(2, nb),
        in_specs=[pl.BlockSpec((SUBLANES, nt, sw), blk), _full(ar.shape), _full(ai.shape), _full(bre.shape), _full(bim.shape),
                  _full(cre.shape), _full(cim.shape), _full(cfr.shape), _full(cfi.shape), _full(dvec.shape)],
        out_specs=[pl.BlockSpec((tb, st), lambda ph, i: (i * ph, 0)), pl.BlockSpec((SUBLANES, nt, sw), oblk),
                   pl.BlockSpec((SUBLANES, nt, sw), oblk)],
        out_shape=[jax.ShapeDtypeStruct((s, st), bf16), act(f32), act(bf16)],
        scratch_shapes=[pltpu.VMEM((tb, st), f32), pltpu.VMEM((SUBLANES, st), f32),
                        pltpu.VMEM((nq, QW, 2 * QS), bf16), pltpu.VMEM((nq, 2 * QS, QW), bf16)],
        sem=("arbitrary", "arbitrary"), xchg=xchg)
    return h_p, yraw3, y3.reshape(s, sw), got


def _ssm_bwd(dy, yraw3, u, h_p, ar, ai, bre, bim, cre, cim, cfr, cfi, dvec, *, name, xchg=None):
    s, sw = u.shape
    nq = sw // QW
    st = nq * 2 * QS
    tb = _tile(s, 256)
    nb, nt, chunk_len = s // tb, tb // SUBLANES, s // SUBLANES

    def body(dy_ref, yraw_ref, u_ref, h_ref, ar_ref, ai_ref, bre_ref, bim_ref, cre_ref, cim_ref, cfr_ref, cfi_ref, d_ref,
             du_out, dbre_out, dbim_out, dcre_out, dcim_out, dcfr_out, dcfi_out, dlbr_out, dlbi_out, dd_out, dbu_out,
             buf, hf, rc, acc, dbbar, dcmat, bbar_s, cmat_s):
        ph, i = pl.program_id(0), pl.program_id(1)

        @pl.when(i == 0)
        def _():
            _ssm_mats(bre_ref, bim_ref, cre_ref, cim_ref, cfr_ref, cfi_ref, bbar_s, cmat_s, nq)

        @pl.when((ph == 0) & (i == 0))
        def _():
            rc[...] = jnp.zeros_like(rc)

        @pl.when((ph == 1) & (i == 0))
        def _():
            for q in range(nq):
                o = q * 2 * QS
                pr, pi_ = _chunk_power(ar_ref[q], ai_ref[q], chunk_len)
                sr, si = _chain_carries(rc[:, o:o + QS], rc[:, o + QS:o + 2 * QS], pr, -pi_, False)
                rc[:, o:o + QS] = sr
                rc[:, o + QS:o + 2 * QS] = si
            acc[...] = jnp.zeros_like(acc)
            dbbar[...] = jnp.zeros_like(dbbar)
            dcmat[...] = jnp.zeros_like(dcmat)
            dd_out[...] = jnp.zeros_like(dd_out)
            dbu_out[...] = jnp.zeros_like(dbu_out)

        dyraw = dy_ref[...].astype(f32) * _gelu_grad(yraw_ref[...])
        dyp = _step_major(dyraw).astype(bf16)
        for q in range(nq):
            o = q * 2 * QS
            buf[:, o:o + 2 * QS] = lax.dot_general(dyp[:, q * QW:(q + 1) * QW], cmat_s[q], (((1,), (1,)), ((), ())),
                                                   preferred_element_type=f32)

        def recur(with_grad):
            for q in range(nq):
                o = q * 2 * QS
                a_r = jnp.broadcast_to(ar_ref[q], (SUBLANES, QS))
                a_i = jnp.broadcast_to(ai_ref[q], (SUBLANES, QS))

                def step(j, carry, o=o, a_r=a_r, a_i=a_i):
                    r0 = pl.multiple_of((nt - 1 - j) * SUBLANES, SUBLANES)
                    if with_grad:
                        rr, ri, gr, gi = carry
                        hr = hf[pl.ds(r0, SUBLANES), o:o + QS]
                        hi = hf[pl.ds(r0, SUBLANES), o + QS:o + 2 * QS]
                        gr = gr + hr * rr + hi * ri
                        gi = gi + hr * ri - hi * rr
                    else:
                        rr, ri = carry
                    nr = buf[pl.ds(r0, SUBLANES), o:o + QS] + a_r * rr + a_i * ri
                    ni = buf[pl.ds(r0, SUBLANES), o + QS:o + 2 * QS] + a_r * ri - a_i * rr
                    buf[pl.ds(r0, SUBLANES), o:o + QS] = nr
                    buf[pl.ds(r0, SUBLANES), o + QS:o + 2 * QS] = ni
                    return (nr, ni, gr, gi) if with_grad else (nr, ni)

                init = (rc[:, o:o + QS], rc[:, o + QS:o + 2 * QS])
                if with_grad:
                    init = init + (acc[:, o:o + QS], acc[:, o + QS:o + 2 * QS])
                res = lax.fori_loop(0, nt, step, init)
                rc[:, o:o + QS] = res[0]
                rc[:, o + QS:o + 2 * QS] = res[1]
                if with_grad:
                    acc[:, o:o + QS] = res[2]
                    acc[:, o + QS:o + 2 * QS] = res[3]

        @pl.when(ph == 0)
        def _():
            recur(False)

        @pl.when(ph == 1)
        def _():
            hf[...] = h_ref[...].astype(f32)
            recur(True)
            uu = u_ref[...]
            up = _step_major(uu).astype(bf16)
            dd_out[...] += _colsum((dyraw * uu).reshape(tb, sw))
            for q in range(nq):
                o = q * 2 * QS
                cs = slice(q * QW, (q + 1) * QW)
                lam = buf[:, o:o + 2 * QS].astype(bf16)
                duq = _chunk_major(lax.dot_general(lam, bbar_s[q], (((1,), (1,)), ((), ())), preferred_element_type=f32), nt) \
                    + d_ref[:, cs] * dyraw[:, :, cs]
                du_out[:, :, cs] = duq.astype(bf16)
                dbu_out[:, cs] += _colsum(duq.reshape(tb, QW))
                dbbar[q] += lax.dot_general(up[:, cs], lam, (((0,), (0,)), ((), ())), preferred_element_type=f32)
                dcmat[q] += lax.dot_general(h_ref[:, o:o + 2 * QS], dyp[:, cs], (((0,), (0,)), ((), ())),
                                            preferred_element_type=f32)

        @pl.when((ph == 1) & (i == nb - 1))
        def _():
            for q in range(nq):
                o = q * 2 * QS
                cr, ci, br, bi = cfr_ref[q], cfi_ref[q], bre_ref[q], bim_ref[q]
                gr, gi = dbbar[q, :, 0:QS], dbbar[q, :, QS:2 * QS]
                dbre_out[q] = cr * gr + ci * gi
                dbim_out[q] = cr * gi - ci * gr
                dcfr_out[q] = _colsum(gr * br + gi * bi)
                dcfi_out[q] = _colsum(gi * br - gr * bi)
                dcre_out[q] = dcmat[q, 0:QS, :]
                dcim_out[q] = -dcmat[q, QS:2 * QS, :]
                dlbr_out[q] = _colsum(acc[:, o:o + QS])
                dlbi_out[q] = _colsum(acc[:, o + QS:o + 2 * QS])

    blk = lambda ph, i: (0, nb - 1 - i, 0)
    oblk = lambda ph, i: (0, (nb - 1 - i) * ph + (nb - 1) * (1 - ph), 0)
    pshapes = [ar.shape, ai.shape, bre.shape, bim.shape, cre.shape, cim.shape, cfr.shape, cfi.shape, dvec.shape]
    oshapes = [bre.shape, bim.shape, cre.shape, cim.shape, cfr.shape, cfi.shape, ar.shape, ai.shape, dvec.shape, dvec.shape]
    act = pl.BlockSpec((SUBLANES, nt, sw), blk)
    view = lambda a: a.reshape(SUBLANES, chunk_len, sw)
    res, got = _call(
        body, (view(dy), yraw3, view(u), h_p, ar, ai, bre, bim, cre, cim, cfr, cfi, dvec), name=name, grid=(2, nb),
        in_specs=[act, act, act, pl.BlockSpec((tb, st), lambda ph, i: (nb - 1 - i, 0))] + [_full(p) for p in pshapes],
        out_specs=[pl.BlockSpec((SUBLANES, nt, sw), oblk)] + [_full(p) for p in oshapes],
        out_shape=[jax.ShapeDtypeStruct((SUBLANES, chunk_len, sw), bf16)] + [jax.ShapeDtypeStruct(p, f32) for p in oshapes],
        scratch_shapes=[pltpu.VMEM((tb, st), f32), pltpu.VMEM((tb, st), f32),
                        pltpu.VMEM((SUBLANES, st), f32), pltpu.VMEM((SUBLANES, st), f32),
                        pltpu.VMEM((nq, QW, 2 * QS), f32), pltpu.VMEM((nq, 2 * QS, QW), f32),
                        pltpu.VMEM((nq, QW, 2 * QS), bf16), pltpu.VMEM((nq, 2 * QS, QW), bf16)],
        sem=("arbitrary", "arbitrary"), xchg=xchg)
    return (res[0].reshape(s, sw),) + tuple(res[1:]) + (got,)


def _lnmod(x, sc, sh, *, name):
    s, d = x.shape
    tb = _tile(s, 512)

    def body(x_ref, sc_ref, sh_ref, o_ref):
        xh, _ = _ln(x_ref[...])
        o_ref[...] = (xh * (1.0 + sc_ref[...]) + sh_ref[...]).astype(bf16)

    blk = pl.BlockSpec((tb, d), lambda i: (i, 0))
    vec = pl.BlockSpec((1, d), _row)
    return pl.pallas_call(body, name=name, grid=(s // tb,), in_specs=[blk, vec, vec], out_specs=blk,
                          out_shape=jax.ShapeDtypeStruct((s, d), bf16), compiler_params=_params("parallel"))(x, sc, sh)


ROWS = 32


def _row_chunks(n_rows, rows, fn, init, start=0):
    return lax.fori_loop(start, n_rows // rows, lambda c, carry: fn(pl.multiple_of(c * rows, rows), carry), init)


def _rows_from(win, o, rows):
    if o % SUBLANES == 0:
        return win[o:o + rows]
    n = win.shape[0]
    return pltpu.roll(win, (n - o) % n, 0)[0:rows]


def _window_before(ref, halo, r0, rows, first, cols):
    if first:
        return jnp.concatenate([halo, ref[pl.ds(0, rows), cols]], axis=0)
    return ref[pl.ds(pl.multiple_of(r0 - SUBLANES, SUBLANES), rows + SUBLANES), cols]


def _taps3(win, w, off, rows):
    return _rows_from(win, off, rows) * w[0] + _rows_from(win, off + 1, rows) * w[1] + _rows_from(win, off + 2, rows) * w[2]


def _fold8(x):
    acc = x[0:SUBLANES]
    for r in range(1, x.shape[0] // SUBLANES):
        acc = acc + x[r * SUBLANES:(r + 1) * SUBLANES]
    return acc


def _conv_halo_specs(tb, cw, halo, s):
    per = tb // halo
    prev = pl.BlockSpec((halo, cw), lambda i: (jnp.maximum(i * per - 1, 0), 0))
    nxt = pl.BlockSpec((halo, cw), lambda i: (jnp.minimum((i + 1) * per, s // halo - 1), 0))
    return prev, nxt


WIDE_ROWS = 16


def _shift_groups(lo, hi):
    return [(b, [o for o in range(lo, hi + 1) if o % SUBLANES == b]) for b in range(SUBLANES)]


def _shifted(win, b):
    return win if b == 0 else _rows_from(win, b, win.shape[0] - SUBLANES)


def _conv31(win, w_ref, cols, rows, lo, hi, tap_of):
    acc = None
    for b, offs in _shift_groups(lo, hi):
        if offs:
            wb = _shifted(win, b)
            for o in offs:
                term = wb[o - b:o - b + rows] * w_ref[pl.ds(tap_of(o), 1), cols]
                acc = term if acc is None else acc + term
    return acc


def _gate_into(ext, a_ref, g_ref, ah_ref, gh_ref, tb, i):
    ext[pl.ds(0, CONV_HALO), :] = jnp.where(i > 0, ah_ref[...] * _sig(gh_ref[...]), 0.0)

    def chunk(r0, carry):
        ext[pl.ds(pl.multiple_of(r0 + CONV_HALO, SUBLANES), WIDE_ROWS), :] = \
            a_ref[pl.ds(r0, WIDE_ROWS), :] * _sig(g_ref[pl.ds(r0, WIDE_ROWS), :])
        return carry

    _row_chunks(tb, WIDE_ROWS, chunk, 0)


def _causal_conv_into(v2buf, ext, w_ref, b_ref, tb, cw):
    for ct in range(cw // LANES):
        cols = slice(ct * LANES, (ct + 1) * LANES)

        def chunk(r0, carry, cols=cols):
            win = ext[pl.ds(r0, ROWS + CONV_HALO), cols]
            v2buf[pl.ds(r0, ROWS), cols] = _conv31(win, w_ref, cols, ROWS, 2, CONV_K + 1, lambda o: o - 2) + b_ref[:, cols]
            return carry

        _row_chunks(tb, ROWS, chunk, 0)


def _silu_grad(x):
    sg = _sig(x)
    return sg * (1.0 + x * (1.0 - sg))


def _conv_fwd(cva, cvg, w, b, lng, lnb, *, name, xchg=None):
    s, cw = cva.shape
    tb = _tile(s, 256)
    prev, _ = _conv_halo_specs(tb, cw, CONV_HALO, s)

    def body(a_ref, g_ref, ah_ref, gh_ref, w_ref, b_ref, lng_ref, lnb_ref, o_ref, ext, v2buf):
        _gate_into(ext, a_ref, g_ref, ah_ref, gh_ref, tb, pl.program_id(0))
        _causal_conv_into(v2buf, ext, w_ref, b_ref, tb, cw)
        xh, _ = _ln(v2buf[...])
        v3 = xh * lng_ref[...] + lnb_ref[...]
        o_ref[...] = (v3 * _sig(v3)).astype(bf16)

    blk = pl.BlockSpec((tb, cw), lambda i: (i, 0))
    vec = pl.BlockSpec((1, cw), _row)
    (v4,), got = _call(
        body, (cva, cvg, cva, cvg, w, b, lng, lnb), name=name, grid=(s // tb,),
        in_specs=[blk, blk, prev, prev, _full(w.shape), vec, vec, vec], out_specs=[blk],
        out_shape=[jax.ShapeDtypeStruct((s, cw), bf16)],
        scratch_shapes=[pltpu.VMEM((tb + CONV_HALO, cw), f32), pltpu.VMEM((tb, cw), f32)], sem=("parallel",), xchg=xchg)
    return v4, got


def _conv_bwd_ln(dv4, cva, cvg, w, b, lng, lnb, *, name):
    s, cw = cva.shape
    tb = _tile(s, 256)
    prev, _ = _conv_halo_specs(tb, cw, CONV_HALO, s)

    def body(d_ref, a_ref, g_ref, ah_ref, gh_ref, w_ref, b_ref, lng_ref, lnb_ref, o_ref, dg_ref, db_ref, ext, v2buf):
        i = pl.program_id(0)

        @pl.when(i == 0)
        def _():
            dg_ref[...] = jnp.zeros_like(dg_ref)
            db_ref[...] = jnp.zeros_like(db_ref)

        _gate_into(ext, a_ref, g_ref, ah_ref, gh_ref, tb, i)
        _causal_conv_into(v2buf, ext, w_ref, b_ref, tb, cw)
        xh, rstd = _ln(v2buf[...])
        v3 = xh * lng_ref[...] + lnb_ref[...]
        dv3 = d_ref[...].astype(f32) * _silu_grad(v3)
        dg_ref[...] += _colsum(dv3 * xh)
        db_ref[...] += _colsum(dv3)
        o_ref[...] = _ln_bwd(dv3 * lng_ref[...], xh, rstd)

    blk = pl.BlockSpec((tb, cw), lambda i: (i, 0))
    vec = pl.BlockSpec((1, cw), _row)
    vshape = jax.ShapeDtypeStruct((1, cw), f32)
    return pl.pallas_call(
        body, name=name, grid=(s // tb,), in_specs=[blk, blk, blk, prev, prev, _full(w.shape), vec, vec, vec],
        out_specs=[blk, vec, vec], out_shape=[jax.ShapeDtypeStruct((s, cw), f32), vshape, vshape],
        scratch_shapes=[pltpu.VMEM((tb + CONV_HALO, cw), f32), pltpu.VMEM((tb, cw), f32)],
        compiler_params=_params("arbitrary"))(dv4, cva, cvg, cva, cvg, w, b, lng, lnb)


def _conv_bwd_taps(dv2, cva, cvg, w, *, name, xchg=None):
    s, cw = cva.shape
    tb = _tile(s, 256)
    nb = s // tb
    prev, nxt = _conv_halo_specs(tb, cw, CONV_HALO, s)

    def body(d_ref, dn_ref, a_ref, g_ref, ah_ref, gh_ref, w_ref, da_ref, dg_ref, dw_ref, db_ref, sa_ref, sg_ref,
             ext, dext, dvbuf, tap_sums):
        i = pl.program_id(0)

        @pl.when(i == 0)
        def _():
            for r in (dw_ref, db_ref, sa_ref, sg_ref):
                r[...] = jnp.zeros_like(r)

        _gate_into(ext, a_ref, g_ref, ah_ref, gh_ref, tb, i)
        dext[pl.ds(tb, CONV_HALO), :] = jnp.where(i < nb - 1, dn_ref[...], 0.0)

        def copy(r0, carry):
            dext[pl.ds(r0, WIDE_ROWS), :] = d_ref[pl.ds(r0, WIDE_ROWS), :]
            return carry

        _row_chunks(tb, WIDE_ROWS, copy, 0)

        for ct in range(cw // LANES):
            cols = slice(ct * LANES, (ct + 1) * LANES)

            tap_sums[...] = jnp.zeros_like(tap_sums)

            def back(r0, carry, cols=cols):
                win = dext[pl.ds(r0, ROWS + CONV_HALO), cols]
                dvbuf[pl.ds(r0, ROWS), cols] = _conv31(win, w_ref, cols, ROWS, 0, CONV_K - 1, lambda o: CONV_K - 1 - o)
                win = ext[pl.ds(r0, ROWS + CONV_HALO), cols]
                dd = d_ref[pl.ds(r0, ROWS), cols]
                for b, offs in _shift_groups(2, CONV_K + 1):
                    wb = _shifted(win, b)
                    for o in offs:
                        tap_sums[o - 2] += _fold8(dd * wb[o - b:o - b + ROWS])
                return carry

            _row_chunks(tb, ROWS, back, 0)
            for k in range(CONV_K):
                dw_ref[pl.ds(k, 1), cols] += _colsum(tap_sums[k])

        def gate_back(r0, sums):
            rows = pl.ds(r0, WIDE_ROWS)
            aa, sg, dv = a_ref[rows, :], _sig(g_ref[rows, :]), dvbuf[rows, :]
            da = dv * sg
            dgate = dv * aa * sg * (1.0 - sg)
            da_ref[rows, :] = da.astype(bf16)
            dg_ref[rows, :] = dgate.astype(bf16)
            return sums[0] + _fold8(da), sums[1] + _fold8(dgate), sums[2] + _fold8(d_ref[rows, :])

        zero = jnp.zeros((SUBLANES, cw), f32)
        sums = _row_chunks(tb, WIDE_ROWS, gate_back, (zero, zero, zero))
        sa_ref[...] += _colsum(sums[0])
        sg_ref[...] += _colsum(sums[1])
        db_ref[...] += _colsum(sums[2])

    blk = pl.BlockSpec((tb, cw), lambda i: (i, 0))
    vec = pl.BlockSpec((1, cw), _row)
    vshape = jax.ShapeDtypeStruct((1, cw), f32)
    act = jax.ShapeDtypeStruct((s, cw), bf16)
    res, got = _call(
        body, (dv2, dv2, cva, cvg, cva, cvg, w), name=name, grid=(nb,), in_specs=[blk, nxt, blk, blk, prev, prev, _full(w.shape)],
        out_specs=[blk, blk, _full(w.shape), vec, vec, vec],
        out_shape=[act, act, jax.ShapeDtypeStruct(w.shape, f32), vshape, vshape, vshape],
        scratch_shapes=[pltpu.VMEM((tb + CONV_HALO, cw), f32), pltpu.VMEM((tb + CONV_HALO, cw), f32), pltpu.VMEM((tb, cw), f32),
                        pltpu.VMEM((CONV_HALO, SUBLANES, LANES), f32)],
        sem=("arbitrary",), xchg=xchg)
    return tuple(res) + (got,)


def _glu_merge(ya, yb, ycv, gs, gc, *, name):
    s, d = ya.shape
    tb = _tile(s, 512)

    def body(ya_ref, yb_ref, ycv_ref, gs_ref, gc_ref, o_ref):
        ld = lambda r: r[...].astype(f32)
        z = ld(ya_ref) * _sig(ld(yb_ref))
        o_ref[...] = (_sig(ld(gs_ref)) * z + _sig(ld(gc_ref)) * ld(ycv_ref)).astype(bf16)

    blk = pl.BlockSpec((tb, d), lambda i: (i, 0))
    return pl.pallas_call(body, name=name, grid=(s // tb,), in_specs=[blk] * 5, out_specs=blk,
                          out_shape=jax.ShapeDtypeStruct((s, d), bf16), compiler_params=_params("parallel"))(ya, yb, ycv, gs, gc)


def _glu_merge_bwd(dm, ya, yb, ycv, gs, gc, *, name):
    s, d = ya.shape
    tb = _tile(s, 512)

    def body(dm_ref, ya_ref, yb_ref, ycv_ref, gs_ref, gc_ref, dya_ref, dyb_ref, dycv_ref, dgs_ref, dgc_ref, sgs_ref, sgc_ref):
        @pl.when(pl.program_id(0) == 0)
        def _():
            sgs_ref[...] = jnp.zeros_like(sgs_ref)
            sgc_ref[...] = jnp.zeros_like(sgc_ref)

        ld = lambda r: r[...].astype(f32)
        dmv, yav = ld(dm_ref), ld(ya_ref)
        sb, ss, scv = _sig(ld(yb_ref)), _sig(ld(gs_ref)), _sig(ld(gc_ref))
        z = yav * sb
        dz = dmv * ss
        dgs = dmv * z * ss * (1.0 - ss)
        dgc = dmv * ld(ycv_ref) * scv * (1.0 - scv)
        dya_ref[...] = (dz * sb).astype(bf16)
        dyb_ref[...] = (dz * yav * sb * (1.0 - sb)).astype(bf16)
        dycv_ref[...] = (dmv * scv).astype(bf16)
        dgs_ref[...] = dgs.astype(bf16)
        dgc_ref[...] = dgc.astype(bf16)
        sgs_ref[...] += _colsum(dgs)
        sgc_ref[...] += _colsum(dgc)

    blk = pl.BlockSpec((tb, d), lambda i: (i, 0))
    vec = pl.BlockSpec((1, d), _row)
    act = jax.ShapeDtypeStruct((s, d), bf16)
    vshape = jax.ShapeDtypeStruct((1, d), f32)
    return pl.pallas_call(body, name=name, grid=(s // tb,), in_specs=[blk] * 6, out_specs=[blk] * 5 + [vec, vec],
                          out_shape=[act] * 5 + [vshape, vshape], compiler_params=_params("arbitrary"))(dm, ya, yb, ycv, gs, gc)


def _resid_ln_mod(x, o, g, lng, lnb, sc, sh, alpha, *, name):
    s, d = x.shape
    tb = _tile(s, 512)

    def body(x_ref, o_ref, g_ref, lng_ref, lnb_ref, sc_ref, sh_ref, x1_ref, h_ref):
        xh, _ = _ln(alpha * x_ref[...] + g_ref[...] * o_ref[...].astype(f32))
        x1 = xh * lng_ref[...] + lnb_ref[...]
        x1_ref[...] = x1
        xh1, _ = _ln(x1)
        h_ref[...] = (xh1 * (1.0 + sc_ref[...]) + sh_ref[...]).astype(bf16)

    blk = pl.BlockSpec((tb, d), lambda i: (i, 0))
    vec = pl.BlockSpec((1, d), _row)
    return pl.pallas_call(body, name=name, grid=(s // tb,), in_specs=[blk, blk] + [vec] * 5, out_specs=[blk, blk],
                          out_shape=[jax.ShapeDtypeStruct((s, d), f32), jax.ShapeDtypeStruct((s, d), bf16)],
                          compiler_params=_params("parallel"))(x, o, g, lng, lnb, sc, sh)


def _resid_ln_loss(x1, y2, g, lng, lnb, tgt, alpha, *, name):
    s, d = x1.shape
    tb = _tile(s, 512)

    def body(x1_ref, y_ref, g_ref, lng_ref, lnb_ref, t_ref, dr_ref, dy_ref, loss_ref, dlg_ref, dlb_ref, dg_ref):
        @pl.when(pl.program_id(0) == 0)
        def _():
            for r in (loss_ref, dlg_ref, dlb_ref, dg_ref):
                r[...] = jnp.zeros_like(r)

        yv = y_ref[...]
        xh, rstd = _ln(alpha * x1_ref[...] + g_ref[...] * yv)
        err = xh * lng_ref[...] + lnb_ref[...] - t_ref[...]
        loss_ref[...] += 0.5 * jnp.sum(jnp.sum(err * err, axis=-1, keepdims=True) / d, axis=0, keepdims=True)
        dx2 = err / d
        dlg_ref[...] += _colsum(dx2 * xh)
        dlb_ref[...] += _colsum(dx2)
        dr = _ln_bwd(dx2 * lng_ref[...], xh, rstd)
        dg_ref[...] += _colsum(dr * yv)
        dr_ref[...] = dr
        dy_ref[...] = (g_ref[...] * dr).astype(bf16)

    blk = pl.BlockSpec((tb, d), lambda i: (i, 0))
    vec = pl.BlockSpec((1, d), _row)
    vshape = jax.ShapeDtypeStruct((1, d), f32)
    return pl.pallas_call(
        body, name=name, grid=(s // tb,), in_specs=[blk, blk, vec, vec, vec, blk],
        out_specs=[blk, blk, pl.BlockSpec((1, 1), _row), vec, vec, vec],
        out_shape=[jax.ShapeDtypeStruct((s, d), f32), jax.ShapeDtypeStruct((s, d), bf16),
                   jax.ShapeDtypeStruct((1, 1), f32), vshape, vshape, vshape],
        compiler_params=_params("arbitrary"))(x1, y2, g, lng, lnb, tgt)


def _mid_bwd(dh2, x1, dr2, x, o, g, sc, lng, alpha, *, name):
    s, d = x.shape
    tb = _tile(s, 512)

    def body(dh_ref, x1_ref, dr2_ref, x_ref, o_ref, g_ref, sc_ref, lng_ref,
             dr1_ref, do_ref, dsc_ref, dsh_ref, dlg_ref, dlb_ref, dg_ref):
        @pl.when(pl.program_id(0) == 0)
        def _():
            for r in (dsc_ref, dsh_ref, dlg_ref, dlb_ref, dg_ref):
                r[...] = jnp.zeros_like(r)

        dh = dh_ref[...].astype(f32)
        xh1, rstd1 = _ln(x1_ref[...])
        dsc_ref[...] += _colsum(dh * xh1)
        dsh_ref[...] += _colsum(dh)
        dx1 = alpha * dr2_ref[...] + _ln_bwd(dh * (1.0 + sc_ref[...]), xh1, rstd1)
        ov = o_ref[...].astype(f32)
        xhr, rstdr = _ln(alpha * x_ref[...] + g_ref[...] * ov)
        dlg_ref[...] += _colsum(dx1 * xhr)
        dlb_ref[...] += _colsum(dx1)
        dr1 = _ln_bwd(dx1 * lng_ref[...], xhr, rstdr)
        dg_ref[...] += _colsum(dr1 * ov)
        dr1_ref[...] = dr1
        do_ref[...] = (g_ref[...] * dr1).astype(bf16)

    blk = pl.BlockSpec((tb, d), lambda i: (i, 0))
    vec = pl.BlockSpec((1, d), _row)
    vshape = jax.ShapeDtypeStruct((1, d), f32)
    return pl.pallas_call(
        body, name=name, grid=(s // tb,), in_specs=[blk] * 5 + [vec] * 3, out_specs=[blk, blk] + [vec] * 5,
        out_shape=[jax.ShapeDtypeStruct((s, d), f32), jax.ShapeDtypeStruct((s, d), bf16)] + [vshape] * 5,
        compiler_params=_params("arbitrary"))(dh2, x1, dr2, x, o, g, sc, lng)


def _final_bwd(dh1, x, dr1, sc, alpha, *, name, xchg=None):
    s, d = x.shape
    tb = _tile(s, 512)

    def body(dh_ref, x_ref, dr1_ref, sc_ref, dx_ref, dsc_ref, dsh_ref):
        @pl.when(pl.program_id(0) == 0)
        def _():
            dsc_ref[...] = jnp.zeros_like(dsc_ref)
            dsh_ref[...] = jnp.zeros_like(dsh_ref)

        dh = dh_ref[...].astype(f32)
        xh, rstd = _ln(x_ref[...])
        dsc_ref[...] += _colsum(dh * xh)
        dsh_ref[...] += _colsum(dh)
        dx_ref[...] = alpha * dr1_ref[...] + _ln_bwd(dh * (1.0 + sc_ref[...]), xh, rstd)

    blk = pl.BlockSpec((tb, d), lambda i: (i, 0))
    vec = pl.BlockSpec((1, d), _row)
    vshape = jax.ShapeDtypeStruct((1, d), f32)
    res, got = _call(body, (dh1, x, dr1, sc), name=name, grid=(s // tb,), in_specs=[blk, blk, blk, vec], out_specs=[blk, vec, vec],
                     out_shape=[jax.ShapeDtypeStruct((s, d), f32), vshape, vshape], sem=("arbitrary",), xchg=xchg)
    return tuple(res) + (got,)


TALL_ROWS = 64


def _ffn_col_tile(fh):
    return fh // 2 if (fh // 2) % LANES == 0 else fh


def _ffn_specs(s, fh, tb, tc):
    per = tb // FFN_HALO
    blk = pl.BlockSpec((tb, tc), lambda j, i: (i, j))
    prev = pl.BlockSpec((FFN_HALO, tc), lambda j, i: (jnp.maximum(i * per - 1, 0), j))
    nxt = pl.BlockSpec((FFN_HALO, tc), lambda j, i: (jnp.minimum((i + 1) * per, s // FFN_HALO - 1), j))
    taps = pl.BlockSpec((FFN_HALO, tc), lambda j, i: (0, j))
    vec = pl.BlockSpec((1, tc), lambda j, i: (0, j))
    return blk, prev, nxt, taps, vec


def _ffn_mid(upa, upv, wa, wv, ba, bv, *, name, xchg=None):
    s, fh = upa.shape
    tb, tc = _tile(s, 512), _ffn_col_tile(fh)
    blk, prev, _, taps, vec = _ffn_specs(s, fh, tb, tc)
    off = FFN_HALO - FFN_K + 1

    def body(a_ref, v_ref, ah_ref, vh_ref, wa_ref, wv_ref, ba_ref, bv_ref, o_ref):
        first = pl.program_id(1) == 0
        for lt in range(tc // LANES):
            cols = slice(lt * LANES, (lt + 1) * LANES)
            halo_a, halo_v = jnp.where(first, 0.0, ah_ref[:, cols]), jnp.where(first, 0.0, vh_ref[:, cols])
            wa = [wa_ref[pl.ds(k, 1), cols] for k in range(FFN_K)]
            wv = [wv_ref[pl.ds(k, 1), cols] for k in range(FFN_K)]
            ba, bv = ba_ref[:, cols], bv_ref[:, cols]

            def chunk(r0, carry, head=False, cols=cols, halo_a=halo_a, halo_v=halo_v, wa=wa, wv=wv, ba=ba, bv=bv):
                a2 = _taps3(_window_before(a_ref, halo_a, r0, TALL_ROWS, head, cols), wa, off, TALL_ROWS) + ba
                v2 = _taps3(_window_before(v_ref, halo_v, r0, TALL_ROWS, head, cols), wv, off, TALL_ROWS) + bv
                o_ref[pl.ds(r0, TALL_ROWS), cols] = (_gelu(a2) * v2).astype(bf16)
                return carry

            chunk(0, 0, head=True)
            _row_chunks(tb, TALL_ROWS, chunk, 0, start=1)

    (f,), got = _call(
        body, (upa, upv, upa, upv, wa, wv, ba, bv), name=name, grid=(fh // tc, s // tb),
        in_specs=[blk, blk, prev, prev, taps, taps, vec, vec], out_specs=[blk], out_shape=[jax.ShapeDtypeStruct((s, fh), bf16)],
        sem=("parallel", "arbitrary"), xchg=xchg)
    return f, got


def _ffn_mid_bwd_tile(cols, first, last, tb, off, df_ref, dfn_ref, a_ref, v_ref, ah_ref, vh_ref, an_ref, vn_ref, wa_ref, wv_ref,
                      ba_ref, bv_ref, da_ref, dv_ref, dwa_ref, dwv_ref, dba_ref, dbv_ref, dexta, dextv):
    rows_c = TALL_ROWS
    halo_a, halo_v = jnp.where(first, 0.0, ah_ref[:, cols]), jnp.where(first, 0.0, vh_ref[:, cols])
    wa = [wa_ref[pl.ds(k, 1), cols] for k in range(FFN_K)]
    wv = [wv_ref[pl.ds(k, 1), cols] for k in range(FFN_K)]
    ba, bv = ba_ref[:, cols], bv_ref[:, cols]

    def conv_cotangents(r0, rows, xa, xv, dfe):
        sa = [_rows_from(xa, off + k, rows) for k in range(FFN_K)]
        sv = [_rows_from(xv, off + k, rows) for k in range(FFN_K)]
        a2 = sa[0] * wa[0] + sa[1] * wa[1] + sa[2] * wa[2] + ba
        v2 = sv[0] * wv[0] + sv[1] * wv[1] + sv[2] * wv[2] + bv
        cdf = 0.5 * (1.0 + lax.erf(a2 * INV_SQRT2))
        da2 = dfe * v2 * (cdf + a2 * jnp.exp(-0.5 * a2 * a2) * INV_SQRT_2PI)
        dv2 = dfe * (a2 * cdf)
        dexta[pl.ds(r0, rows), cols] = da2
        dextv[pl.ds(r0, rows), cols] = dv2
        return da2, dv2, sa, sv

    def chunk(r0, sums, head=False):
        da2, dv2, sa, sv = conv_cotangents(r0, rows_c, _window_before(a_ref, halo_a, r0, rows_c, head, cols),
                                           _window_before(v_ref, halo_v, r0, rows_c, head, cols), df_ref[pl.ds(r0, rows_c), cols])
        new = [sums[k] + _fold8(da2 * sa[k]) for k in range(FFN_K)] + [sums[FFN_K] + _fold8(da2)]
        new += [sums[FFN_K + 1 + k] + _fold8(dv2 * sv[k]) for k in range(FFN_K)] + [sums[2 * FFN_K + 1] + _fold8(dv2)]
        return tuple(new)

    sums = chunk(0, tuple(jnp.zeros((SUBLANES, LANES), f32) for _ in range(2 * FFN_K + 2)), head=True)
    sums = _row_chunks(tb, rows_c, chunk, sums, start=1)
    conv_cotangents(tb, FFN_HALO,
                    jnp.concatenate([a_ref[pl.ds(tb - FFN_HALO, FFN_HALO), cols], jnp.where(last, 0.0, an_ref[:, cols])], axis=0),
                    jnp.concatenate([v_ref[pl.ds(tb - FFN_HALO, FFN_HALO), cols], jnp.where(last, 0.0, vn_ref[:, cols])], axis=0),
                    jnp.where(last, 0.0, dfn_ref[:, cols]))
    for k in range(FFN_K):
        dwa_ref[pl.ds(k, 1), cols] += _colsum(sums[k])
        dwv_ref[pl.ds(k, 1), cols] += _colsum(sums[FFN_K + 1 + k])
    dba_ref[:, cols] += _colsum(sums[FFN_K])
    dbv_ref[:, cols] += _colsum(sums[2 * FFN_K + 1])

    def back(r0, carry):
        for dext, w, o_ref in ((dexta, wa, da_ref), (dextv, wv, dv_ref)):
            dd = dext[pl.ds(r0, rows_c + FFN_HALO), cols]
            o_ref[pl.ds(r0, rows_c), cols] = (_rows_from(dd, 2, rows_c) * w[0] + _rows_from(dd, 1, rows_c) * w[1]
                                              + dd[0:rows_c] * w[2]).astype(bf16)
        return carry

    _row_chunks(tb, rows_c, back, 0)


def _ffn_mid_bwd(df, upa, upv, wa, wv, ba, bv, *, name, xchg=None):
    s, fh = upa.shape
    tb, tc = _tile(s, 512), _ffn_col_tile(fh)
    nb = s // tb
    blk, prev, nxt, taps, vec = _ffn_specs(s, fh, tb, tc)
    off = FFN_HALO - FFN_K + 1
    te = tb + FFN_HALO

    def body(df_ref, dfn_ref, a_ref, v_ref, ah_ref, vh_ref, an_ref, vn_ref, wa_ref, wv_ref, ba_ref, bv_ref,
             da_ref, dv_ref, dwa_ref, dwv_ref, dba_ref, dbv_ref, dexta, dextv):
        i = pl.program_id(1)

        @pl.when(i == 0)
        def _():
            for r in (dwa_ref, dwv_ref, dba_ref, dbv_ref):
                r[...] = jnp.zeros_like(r)

        last = i == nb - 1
        for lt in range(tc // LANES):
            _ffn_mid_bwd_tile(slice(lt * LANES, (lt + 1) * LANES), i == 0, last, tb, off, df_ref, dfn_ref, a_ref, v_ref,
                              ah_ref, vh_ref, an_ref, vn_ref, wa_ref, wv_ref, ba_ref, bv_ref, da_ref, dv_ref, dwa_ref, dwv_ref,
                              dba_ref, dbv_ref, dexta, dextv)

    act = jax.ShapeDtypeStruct((s, fh), bf16)
    wshape = jax.ShapeDtypeStruct((FFN_HALO, fh), f32)
    vshape = jax.ShapeDtypeStruct((1, fh), f32)
    res, got = _call(
        body, (df, df, upa, upv, upa, upv, upa, upv, wa, wv, ba, bv), name=name, grid=(fh // tc, nb),
        in_specs=[blk, nxt, blk, blk, prev, prev, nxt, nxt, taps, taps, vec, vec],
        out_specs=[blk, blk, taps, taps, vec, vec], out_shape=[act, act, wshape, wshape, vshape, vshape],
        scratch_shapes=[pltpu.VMEM((te, tc), f32)] * 2, sem=("parallel", "arbitrary"), xchg=xchg)
    return tuple(res) + (got,)


def _cols_from_shards(stacked):
    _, k, n = stacked.shape
    return stacked.transpose(1, 0, 2).reshape(k, NDEV * n)


def _pad_rows(w, rows):
    return jnp.pad(w, ((0, rows - w.shape[0]), (0, 0)))


def kernel(x, c, w_cond, b_cond, w_in, b_in, ssm_lambda_re, ssm_lambda_im, ssm_log_dt, ssm_b_re, ssm_b_im, ssm_c_re, ssm_c_im, ssm_d, ssm_glu_w_a, ssm_glu_w_b, cv_dw_w, cv_dw_b, cv_ln_g, cv_ln_b, cv_w_pw, w_out, ln1_g, ln1_b, ffn_w_up, ffn_dw_w, ffn_dw_b, ffn_w_down, ln2_g, ln2_b, loss_target, m_w_cond, m_b_cond, m_w_in, m_b_in, m_ssm_lambda_re, m_ssm_lambda_im, m_ssm_log_dt, m_ssm_b_re, m_ssm_b_im, m_ssm_c_re, m_ssm_c_im, m_ssm_d, m_ssm_glu_w_a, m_ssm_glu_w_b, m_cv_dw_w, m_cv_dw_b, m_cv_ln_g, m_cv_ln_b, m_cv_w_pw, m_w_out, m_ln1_g, m_ln1_b, m_ffn_w_up, m_ffn_dw_w, m_ffn_dw_b, m_ffn_w_down, m_ln2_g, m_ln2_b, v_w_cond, v_b_cond, v_w_in, v_b_in, v_ssm_lambda_re, v_ssm_lambda_im, v_ssm_log_dt, v_ssm_b_re, v_ssm_b_im, v_ssm_c_re, v_ssm_c_im, v_ssm_d, v_ssm_glu_w_a, v_ssm_glu_w_b, v_cv_dw_w, v_cv_dw_b, v_cv_ln_g, v_cv_ln_b, v_cv_w_pw, v_w_out, v_ln1_g, v_ln1_b, v_ffn_w_up, v_ffn_dw_w, v_ffn_dw_b, v_ffn_w_down, v_ln2_g, v_ln2_b):
    weights = dict(w_cond=w_cond, b_cond=b_cond, w_in=w_in, b_in=b_in, ssm_lambda_re=ssm_lambda_re, ssm_lambda_im=ssm_lambda_im, ssm_log_dt=ssm_log_dt, ssm_b_re=ssm_b_re, ssm_b_im=ssm_b_im, ssm_c_re=ssm_c_re, ssm_c_im=ssm_c_im, ssm_d=ssm_d, ssm_glu_w_a=ssm_glu_w_a, ssm_glu_w_b=ssm_glu_w_b, cv_dw_w=cv_dw_w, cv_dw_b=cv_dw_b, cv_ln_g=cv_ln_g, cv_ln_b=cv_ln_b, cv_w_pw=cv_w_pw, w_out=w_out, ln1_g=ln1_g, ln1_b=ln1_b, ffn_w_up=ffn_w_up, ffn_dw_w=ffn_dw_w, ffn_dw_b=ffn_dw_b, ffn_w_down=ffn_w_down, ln2_g=ln2_g, ln2_b=ln2_b)
    mom_m = dict(w_cond=m_w_cond, b_cond=m_b_cond, w_in=m_w_in, b_in=m_b_in, ssm_lambda_re=m_ssm_lambda_re, ssm_lambda_im=m_ssm_lambda_im, ssm_log_dt=m_ssm_log_dt, ssm_b_re=m_ssm_b_re, ssm_b_im=m_ssm_b_im, ssm_c_re=m_ssm_c_re, ssm_c_im=m_ssm_c_im, ssm_d=m_ssm_d, ssm_glu_w_a=m_ssm_glu_w_a, ssm_glu_w_b=m_ssm_glu_w_b, cv_dw_w=m_cv_dw_w, cv_dw_b=m_cv_dw_b, cv_ln_g=m_cv_ln_g, cv_ln_b=m_cv_ln_b, cv_w_pw=m_cv_w_pw, w_out=m_w_out, ln1_g=m_ln1_g, ln1_b=m_ln1_b, ffn_w_up=m_ffn_w_up, ffn_dw_w=m_ffn_dw_w, ffn_dw_b=m_ffn_dw_b, ffn_w_down=m_ffn_w_down, ln2_g=m_ln2_g, ln2_b=m_ln2_b)
    mom_v = dict(w_cond=v_w_cond, b_cond=v_b_cond, w_in=v_w_in, b_in=v_b_in, ssm_lambda_re=v_ssm_lambda_re, ssm_lambda_im=v_ssm_lambda_im, ssm_log_dt=v_ssm_log_dt, ssm_b_re=v_ssm_b_re, ssm_b_im=v_ssm_b_im, ssm_c_re=v_ssm_c_re, ssm_c_im=v_ssm_c_im, ssm_d=v_ssm_d, ssm_glu_w_a=v_ssm_glu_w_a, ssm_glu_w_b=v_ssm_glu_w_b, cv_dw_w=v_cv_dw_w, cv_dw_b=v_cv_dw_b, cv_ln_g=v_cv_ln_g, cv_ln_b=v_cv_ln_b, cv_w_pw=v_cv_w_pw, w_out=v_w_out, ln1_g=v_ln1_g, ln1_b=v_ln1_b, ffn_w_up=v_ffn_w_up, ffn_dw_w=v_ffn_dw_w, ffn_dw_b=v_ffn_dw_b, ffn_w_down=v_ffn_w_down, ln2_g=v_ln2_g, ln2_b=v_ln2_b)
    names = list(weights)

    s, d = x.shape[1], x.shape[2]
    sw = cw = d // 2
    fh = ffn_w_down.shape[1] * NDEV
    ng, nq = sw // SSM_GROUP, sw // QW
    gq = ng // nq
    alpha = 2.0 ** 0.25
    me = 4 * lax.axis_index("x") + 2 * lax.axis_index("y") + lax.axis_index("c")
    xs, tgt = x[0], loss_target[0]

    col_names = ["w_in", "ssm_glu_w_a", "ssm_glu_w_b", "cv_w_pw", "ffn_w_up"]
    row_names = ["w_out", "ffn_w_down"]
    big = col_names + row_names
    sent = lambda ns: [weights[n][0].astype(bf16) for n in ns]
    got_in, got_c, got_cv_taps, got_ffn_taps = _exchange(sent(["w_in"]) + [c, cv_dw_w[0, :, 0], ffn_dw_w[0, :, 0]],
                                                         scatter=False, name="gather_in")
    o1, o2, o3, o4 = sw, sw + cw, sw + 2 * cw, sw + 2 * cw + d
    in_bounds = ((0, o1), (o1, o2), (o2, o3), (o3, o4), (o4, o4 + d))
    w_u, w_cva, w_cvg, w_gs, w_gc = _unshard_cols(got_in, in_bounds, name="unshard_w_in")
    b_u, b_cva, b_cvg, b_gs, b_gc = (b_in[:, a:b] for a, b in in_bounds)
    c_all = got_c.reshape(NDEV, d)
    cv_taps = _cols_from_shards(got_cv_taps)
    ffn_taps = _cols_from_shards(got_ffn_taps)
    cv_w32 = _pad_rows(cv_taps, CONV_HALO)
    ffn_wa, ffn_wv = _pad_rows(ffn_taps[:, :fh], FFN_HALO), _pad_rows(ffn_taps[:, fh:], FFN_HALO)
    ffn_ba, ffn_bv = ffn_dw_b[:, :fh], ffn_dw_b[:, fh:]

    ncond = w_cond.shape[2]
    b_cond_mine = lax.dynamic_slice(b_cond, (0, me * ncond), (1, ncond))
    mod_cols = _cond_fwd(c_all, w_cond[0], b_cond_mine, name="cond_fwd")
    mod_all, = _exchange([mod_cols], scatter=False, name="gather_mod")
    mod_mine = lax.dynamic_slice(mod_all, (0, me, 0), (NDEV, 1, ncond)).reshape(1, 6 * d)
    sh1, sc1, g1, sh2, sc2, g2 = (mod_mine[:, k * d:(k + 1) * d] for k in range(6))

    lam_re, lam_im, log_dt = ssm_lambda_re[0], ssm_lambda_im[0], ssm_log_dt[0][:, None]
    lbr, lbi, cfr, cfi = _ssm_prep(lam_re, lam_im, log_dt, name="ssm_prep")
    rows_q = lambda a: a.reshape(nq, 1, QS)
    eye = jnp.eye(gq, dtype=f32)

    def b_mat(b):
        bt = b.reshape(nq, gq, SSM_STATE, SSM_GROUP).transpose(0, 1, 3, 2)
        return jnp.einsum("qgpn,gh->qgphn", bt, eye).reshape(nq, QW, QS)

    def c_mat(cc):
        ct = cc.reshape(nq, gq, SSM_GROUP, SSM_STATE)
        return jnp.einsum("qgpn,gh->qhngp", ct, eye).reshape(nq, QS, QW)

    def b_unmat(mt):
        return jnp.einsum("qgpgn->qgnp", mt.reshape(nq, gq, SSM_GROUP, gq, SSM_STATE)).reshape(ng, SSM_STATE, SSM_GROUP)

    def c_unmat(mt):
        return jnp.einsum("qgngp->qgpn", mt.reshape(nq, gq, SSM_STATE, gq, SSM_GROUP)).reshape(ng, SSM_GROUP, SSM_STATE)

    ssm_args = (rows_q(lbr), rows_q(lbi), b_mat(ssm_b_re[0]), b_mat(ssm_b_im[0]), c_mat(ssm_c_re[0]), c_mat(ssm_c_im[0]),
                rows_q(cfr), rows_q(cfi), ssm_d[0].reshape(1, sw))

    h1 = _lnmod(xs, sc1, sh1, name="ln_mod1")
    u, cva, cvg, gs, gc = _mm_fanout(h1, [w_u, w_cva, w_cvg, w_gs, w_gc], [b_u, b_cva, b_cvg, b_gs, b_gc],
                                     [f32, f32, f32, bf16, bf16], name="in_proj")
    v4, (got_a, got_b, got_pw, got_o) = _conv_fwd(
        cva, cvg, cv_w32, cv_dw_b, cv_ln_g, cv_ln_b, name="conv_fwd",
        xchg=(sent(["ssm_glu_w_a", "ssm_glu_w_b", "cv_w_pw", "w_out"]), False))
    h_p, yraw3, y, (got_up,) = _ssm_fwd(u, *ssm_args, name="ssm_fwd", xchg=(sent(["ffn_w_up"]), False))
    w_a, = _unshard_cols(got_a, ((0, d),), name="unshard_glu_a")
    w_b, = _unshard_cols(got_b, ((0, d),), name="unshard_glu_b")
    w_pw, = _unshard_cols(got_pw, ((0, d),), name="unshard_conv_pw")
    w_upa, w_upv = _unshard_cols(got_up, ((0, fh), (fh, 2 * fh)), name="unshard_ffn_up")
    w_o = got_o.reshape(d, d)
    ya = _mm([(y, w_a)], out_dtype=bf16, name="glu_a")
    yb = _mm([(y, w_b)], out_dtype=bf16, name="glu_b")
    ycv = _mm([(v4, w_pw)], out_dtype=bf16, name="conv_pw")
    merged = _glu_merge(ya, yb, ycv, gs, gc, name="merge")
    o = _mm([(merged, w_o)], out_dtype=bf16, name="out_proj")
    x1, h2 = _resid_ln_mod(xs, o, g1, ln1_g, ln1_b, sc2, sh2, alpha, name="resid_ln1")
    upa = _mm([(h2, w_upa)], name="ffn_up_a")
    upv = _mm([(h2, w_upv)], name="ffn_up_v")
    f, (got_dn,) = _ffn_mid(upa, upv, ffn_wa, ffn_wv, ffn_ba, ffn_bv, name="ffn_mid", xchg=(sent(["ffn_w_down"]), False))
    w_dn = got_dn.reshape(fh, d)
    y2 = _mm([(f, w_dn)], name="ffn_down")
    dr2, dy2, loss_part, d_ln2_g, d_ln2_b, d_g2 = _resid_ln_loss(x1, y2, g2, ln2_g, ln2_b, tgt, alpha, name="resid_ln2_loss")

    gw = {}
    df = _mm([(dy2, w_dn)], trans_w=True, name="d_ffn_down")
    gw["ffn_w_down"] = _mm_tn(f, dy2, out_dtype=bf16, name="g_ffn_down").reshape((NDEV,) + ffn_w_down[0].shape)
    received = {}
    dupa, dupv, d_ffn_wa, d_ffn_wv, d_ffn_ba, d_ffn_bv, (received["ffn_w_down"],) = _ffn_mid_bwd(
        df, upa, upv, ffn_wa, ffn_wv, ffn_ba, ffn_bv, name="ffn_mid_bwd", xchg=([gw["ffn_w_down"]], True))
    dh2 = _mm([(dupa, w_upa), (dupv, w_upv)], trans_w=True, out_dtype=bf16, name="d_ffn_up")
    gw["ffn_w_up"] = _shard_cols([_mm_tn(h2, dupa, name="g_ffn_up_a"), _mm_tn(h2, dupv, name="g_ffn_up_v")], out_dtype=bf16,
                                 name="shard_ffn_up")
    dr1, do, d_sc2, d_sh2, d_ln1_g, d_ln1_b, d_g1 = _mid_bwd(dh2, x1, dr2, xs, o, g1, sc2, ln1_g, alpha, name="mid_bwd")
    dmerged = _mm([(do, w_o)], trans_w=True, out_dtype=bf16, name="d_out_proj")
    gw["w_out"] = _mm_tn(merged, do, out_dtype=bf16, name="g_out_proj").reshape((NDEV,) + w_out[0].shape)
    dya, dyb, dycv, dgs, dgc, s_gs, s_gc = _glu_merge_bwd(dmerged, ya, yb, ycv, gs, gc, name="merge_bwd")
    dy = _mm([(dya, w_a), (dyb, w_b)], trans_w=True, out_dtype=bf16, name="d_glu")
    gw["ssm_glu_w_a"] = _shard_cols([_mm_tn(y, dya, name="g_glu_a")], out_dtype=bf16, name="shard_glu_a")
    gw["ssm_glu_w_b"] = _shard_cols([_mm_tn(y, dyb, name="g_glu_b")], out_dtype=bf16, name="shard_glu_b")
    dv4 = _mm([(dycv, w_pw)], trans_w=True, out_dtype=bf16, name="d_conv_pw")
    gw["cv_w_pw"] = _shard_cols([_mm_tn(v4, dycv, name="g_conv_pw")], out_dtype=bf16, name="shard_conv_pw")
    dv2, d_cv_ln_g, d_cv_ln_b = _conv_bwd_ln(dv4, cva, cvg, cv_w32, cv_dw_b, cv_ln_g, cv_ln_b, name="conv_bwd_ln")
    dcva, dcvg, d_cv_w32, d_cv_b, s_cva, s_cvg, (received["ffn_w_up"],) = _conv_bwd_taps(
        dv2, cva, cvg, cv_w32, name="conv_bwd_taps", xchg=([gw["ffn_w_up"]], True))
    late = ["w_out", "ssm_glu_w_a", "ssm_glu_w_b", "cv_w_pw"]
    (du, d_bre_m, d_bim_m, d_cre_m, d_cim_m, d_cfr, d_cfi, d_lbr, d_lbi, d_d, s_u, got_late) = _ssm_bwd(
        dy, yraw3, u, h_p, *ssm_args, name="ssm_bwd", xchg=([gw[n] for n in late], True))
    received.update(zip(late, got_late))
    gshape = lam_re.shape
    d_lam_re, d_lam_im, d_log_dt = _ssm_prep_bwd(
        lam_re, lam_im, log_dt, [a.reshape(gshape) for a in (d_lbr, d_lbi, d_cfr, d_cfi)], name="ssm_prep_bwd")
    small = {
        "b_in": jnp.concatenate([s_u, s_cva, s_cvg, s_gs, s_gc], axis=1),
        "ssm_lambda_re": d_lam_re, "ssm_lambda_im": d_lam_im, "ssm_log_dt": d_log_dt,
        "ssm_b_re": b_unmat(d_bre_m), "ssm_b_im": b_unmat(d_bim_m), "ssm_c_re": c_unmat(d_cre_m), "ssm_c_im": c_unmat(d_cim_m),
        "ssm_d": d_d, "cv_dw_w": d_cv_w32[:CONV_K], "cv_dw_b": d_cv_b, "cv_ln_g": d_cv_ln_g, "cv_ln_b": d_cv_ln_b,
        "ln1_g": d_ln1_g, "ln1_b": d_ln1_b,
        "ffn_dw_w": jnp.concatenate([d_ffn_wa[:FFN_K], d_ffn_wv[:FFN_K]], axis=1),
        "ffn_dw_b": jnp.concatenate([d_ffn_ba, d_ffn_bv], axis=1), "ln2_g": d_ln2_g, "ln2_b": d_ln2_b,
        "mod_g1": d_g1, "mod_sh2": d_sh2, "mod_sc2": d_sc2, "mod_g2": d_g2, "loss": loss_part,
    }
    small_names = list(small)
    small_shapes = [small[n].shape for n in small_names]
    gw["w_in"], (small_all,) = _mm_tn_sharded(h1, [du, dcva, dcvg, dgs, dgc], out_dtype=bf16, name="g_in",
                                              xchg=([_pack([small[n] for n in small_names])], False))
    dh1, (received["w_in"],) = _mm(
        [(du, w_u), (dcva, w_cva), (dcvg, w_cvg), (dgs, w_gs), (dgc, w_gc)], trans_w=True, out_dtype=bf16, name="d_in",
        xchg=([gw["w_in"]], True))
    grad_x, d_sc1, d_sh1, _ = _final_bwd(dh1, xs, dr1, sc1, alpha, name="final_bwd")

    grads, delta, new_m, new_v = {}, {}, {}, {}
    for n in big:
        ride = ([_pack([d_sh1, d_sc1])], False) if n == big[0] else None
        res = _sum_adamw(received[n], weights[n][0], mom_m[n][0], mom_v[n][0], name="adamw_" + n, xchg=ride)
        grads[n], delta[n], new_m[n], new_v[n] = res[:4]
        if ride is not None:
            last_all, = res[4]

    small_sum = dict(zip(small_names, _unpack(_sum_parts(small_all, name="sum_small").reshape(-1), small_shapes)))
    last_sum = _unpack(_sum_parts(last_all, name="sum_last").reshape(-1), [(1, d), (1, d)])
    per_dev = dict(zip(small_names, _unpack(small_all.reshape(NDEV, -1), small_shapes)))
    last_dev = _unpack(last_all.reshape(NDEV, -1), [(1, d), (1, d)])
    dmod_all = jnp.concatenate(last_dev + [per_dev[k] for k in ("mod_g1", "mod_sh2", "mod_sc2", "mod_g2")], axis=-1).reshape(NDEV, 6 * d)
    dmod_cols = lax.dynamic_slice(dmod_all.reshape(NDEV, NDEV, ncond), (0, me, 0), (NDEV, 1, ncond)).reshape(NDEV, ncond)
    grads["w_cond"] = _cond_bwd(c_all, dmod_cols, name="cond_bwd")
    loss = small_sum.pop("loss").reshape(())
    grads["b_cond"] = jnp.concatenate(last_sum + [small_sum.pop(k) for k in ("mod_g1", "mod_sh2", "mod_sc2", "mod_g2")], axis=1)
    for n, g in small_sum.items():
        grads[n] = g
    ntap = cv_dw_w.shape[3]
    grads["cv_dw_w"] = lax.dynamic_slice(grads["cv_dw_w"], (0, me * ntap), (CONV_K, ntap))
    nffn = ffn_dw_w.shape[3]
    grads["ffn_dw_w"] = lax.dynamic_slice(grads["ffn_dw_w"], (0, me * nffn), (FFN_K, nffn))
    grads = {n: grads[n].reshape(weights[n].shape) for n in names}

    delta["w_cond"], new_m["w_cond"], new_v["w_cond"] = _adamw(w_cond[0], grads["w_cond"][0], m_w_cond[0], v_w_cond[0],
                                                               name="adamw_w_cond")
    rest = [n for n in names if n not in ["w_cond"] + big]
    rest_shapes = [weights[n].shape for n in rest]
    packed = [_pack([t[n] for n in rest]) for t in (weights, grads, mom_m, mom_v)]
    for tgt_dict, res in zip((delta, new_m, new_v), _adamw(*packed, name="adamw_small")):
        for n, a in zip(rest, _unpack(res.reshape(-1), rest_shapes)):
            tgt_dict[n] = a
    shaped = lambda t: [t[n].reshape(weights[n].shape) for n in names]

    return (loss, grad_x[None], *shaped(grads), *shaped(delta), *shaped(new_m), *shaped(new_v))
```

```python
import functools
import math

import jax
import jax.numpy as jnp
from jax import lax
from jax.experimental import pallas as pl
from jax.experimental.pallas import tpu as pltpu

f32 = jnp.float32
bf16 = jnp.bfloat16

NDEV = 8
LANES = 128
SUBLANES = 8
SSM_GROUP = 16
SSM_STATE = 64
QW = 128
QS = 512
CONV_K = 31
CONV_HALO = 32
FFN_K = 3
FFN_HALO = 8
LN_EPS = 1e-5
ADAM_LR, ADAM_B1, ADAM_B2, ADAM_EPS, ADAM_WD, ADAM_STEP = 0.001, 0.9, 0.999, 1e-08, 0.01, 10
VMEM_LIMIT = 56 * 1024 * 1024
W_TILE_BYTES = 6 * 1024 * 1024
SUM_ROWS = 512
EW_BLOCK_BYTES = 2 * 1024 * 1024
INV_SQRT2 = 1.0 / math.sqrt(2.0)
INV_SQRT_2PI = 1.0 / math.sqrt(2.0 * math.pi)
MESH = pl.DeviceIdType.MESH


def _tile(n, want):
    t = min(n, want)
    while n % t:
        t //= 2
    return t


def _col_tile(n, rows, bytes_per):
    best = LANES if n % LANES == 0 else n
    for t in range(LANES, n + 1, LANES):
        if n % t == 0 and rows * t * bytes_per <= W_TILE_BYTES:
            best = t
    return best


def _params(*sem):
    return pltpu.CompilerParams(dimension_semantics=sem, vmem_limit_bytes=VMEM_LIMIT)


def _row(i):
    return (0, 0)


def _full(shape):
    nd = len(shape)
    return pl.BlockSpec(shape, lambda *a: (0,) * nd)


def _ln(x):
    mu = jnp.mean(x, axis=-1, keepdims=True)
    xc = x - mu
    var = jnp.mean(xc * xc, axis=-1, keepdims=True)
    rstd = lax.rsqrt(var + LN_EPS)
    return xc * rstd, rstd


def _ln_bwd(dxhat, xhat, rstd):
    return rstd * (dxhat - jnp.mean(dxhat, axis=-1, keepdims=True) - xhat * jnp.mean(dxhat * xhat, axis=-1, keepdims=True))


def _sig(x):
    return 1.0 / (1.0 + jnp.exp(-x))


def _gelu(x):
    return 0.5 * x * (1.0 + lax.erf(x * INV_SQRT2))


def _gelu_grad(x):
    return 0.5 * (1.0 + lax.erf(x * INV_SQRT2)) + x * jnp.exp(-0.5 * x * x) * INV_SQRT_2PI


def _colsum(x):
    return jnp.sum(x, axis=0, keepdims=True)


def _mm(pairs, bias=None, *, trans_w=False, out_dtype=f32, name, xchg=None):
    n_p = len(pairs)
    m = pairs[0][0].shape[0]
    n = pairs[0][1].shape[0 if trans_w else 1]
    ktot = sum(x.shape[1] for x, _ in pairs)
    tm = _tile(m, 512)
    tn = _col_tile(n, ktot, 2)
    dn = (((1,), (1,)), ((), ())) if trans_w else (((1,), (0,)), ((), ()))

    def body(*refs):
        o_ref = refs[-1]
        acc = None
        for xr, wr in zip(refs[:n_p], refs[n_p:2 * n_p]):
            r = lax.dot_general(xr[...].astype(bf16), wr[...].astype(bf16), dn, preferred_element_type=f32)
            acc = r if acc is None else acc + r
        if bias is not None:
            acc = acc + refs[2 * n_p][...]
        o_ref[...] = acc.astype(out_dtype)

    in_specs = [pl.BlockSpec((tm, x.shape[1]), lambda j, i: (i, 0)) for x, _ in pairs]
    if trans_w:
        in_specs += [pl.BlockSpec((tn, w.shape[1]), lambda j, i: (j, 0)) for _, w in pairs]
    else:
        in_specs += [pl.BlockSpec((w.shape[0], tn), lambda j, i: (0, j)) for _, w in pairs]
    args = [x for x, _ in pairs] + [w for _, w in pairs]
    if bias is not None:
        in_specs.append(pl.BlockSpec((1, tn), lambda j, i: (0, j)))
        args.append(bias)
    (out,), got = _call(
        body, args, name=name, grid=(n // tn, m // tm), in_specs=in_specs,
        out_specs=[pl.BlockSpec((tm, tn), lambda j, i: (i, j))], out_shape=[jax.ShapeDtypeStruct((m, n), out_dtype)],
        sem=("parallel", "arbitrary"), xchg=xchg)
    return out if xchg is None else (out, got)


def _mm_fanout(x, ws, biases, out_dtypes, *, name):
    m, k = x.shape
    tm = _tile(m, 512)
    n_w = len(ws)

    def body(x_ref, *refs):
        xb = x_ref[...].astype(bf16)
        for w_ref, b_ref, o_ref, dt in zip(refs[:n_w], refs[n_w:2 * n_w], refs[2 * n_w:], out_dtypes):
            o_ref[...] = (jnp.dot(xb, w_ref[...], preferred_element_type=f32) + b_ref[...]).astype(dt)

    return pl.pallas_call(
        body, name=name, grid=(m // tm,),
        in_specs=[pl.BlockSpec((tm, k), lambda i: (i, 0))] + [_full(w.shape) for w in ws] + [_full(b.shape) for b in biases],
        out_specs=[pl.BlockSpec((tm, w.shape[1]), lambda i: (i, 0)) for w in ws],
        out_shape=[jax.ShapeDtypeStruct((m, w.shape[1]), dt) for w, dt in zip(ws, out_dtypes)],
        compiler_params=_params("parallel"))(x, *ws, *biases)


def _mm_tn(x, dy, *, out_dtype=f32, name):
    m, k = x.shape
    n = dy.shape[1]
    tm = _tile(m, 512)
    tn = _col_tile(n, k, 4)
    steps = m // tm

    def body(x_ref, dy_ref, o_ref, *scratch):
        acc = scratch[0] if scratch else o_ref

        @pl.when(pl.program_id(1) == 0)
        def _():
            acc[...] = jnp.zeros_like(acc)

        acc[...] += lax.dot_general(x_ref[...].astype(bf16), dy_ref[...].astype(bf16), (((0,), (0,)), ((), ())),
                                    preferred_element_type=f32)
        if scratch:
            @pl.when(pl.program_id(1) == steps - 1)
            def _():
                o_ref[...] = acc[...].astype(out_dtype)

    return pl.pallas_call(
        body, name=name, grid=(n // tn, steps),
        in_specs=[pl.BlockSpec((tm, k), lambda j, i: (i, 0)), pl.BlockSpec((tm, tn), lambda j, i: (i, j))],
        out_specs=pl.BlockSpec((k, tn), lambda j, i: (0, j)),
        out_shape=jax.ShapeDtypeStruct((k, n), out_dtype),
        scratch_shapes=[] if out_dtype == f32 else [pltpu.VMEM((k, tn), f32)],
        compiler_params=_params("parallel", "arbitrary"),
    )(x, dy)


def _mm_tn_sharded(x, dys, *, out_dtype, name, xchg=None):
    m, k = x.shape
    widths = [dy.shape[1] for dy in dys]
    n = sum(widths) // NDEV
    tm = _tile(m, 512)
    steps = m // tm
    n_d = len(dys)

    def body(x_ref, *refs):
        dy_refs, o_ref, acc = refs[:n_d], refs[n_d], refs[n_d + 1]
        i = pl.program_id(0)

        @pl.when(i == 0)
        def _():
            acc[...] = jnp.zeros_like(acc)

        xb = x_ref[...].astype(bf16)
        off = 0
        for dy_ref, w in zip(dy_refs, widths):
            acc[:, off:off + w] += lax.dot_general(xb, dy_ref[...].astype(bf16), (((0,), (0,)), ((), ())), preferred_element_type=f32)
            off += w

        @pl.when(i == steps - 1)
        def _():
            for j in range(NDEV):
                o_ref[j] = acc[:, n * j:n * (j + 1)].astype(out_dtype)

    (out,), got = _call(
        body, (x, *dys), name=name, grid=(steps,),
        in_specs=[pl.BlockSpec((tm, k), lambda i: (i, 0))] + [pl.BlockSpec((tm, w), lambda i: (i, 0)) for w in widths],
        out_specs=[pl.BlockSpec((NDEV, k, n), lambda i: (0, 0, 0))], out_shape=[jax.ShapeDtypeStruct((NDEV, k, n), out_dtype)],
        scratch_shapes=[pltpu.VMEM((k, sum(widths)), f32)], sem=("arbitrary",), xchg=xchg)
    return out if xchg is None else (out, got)


def _exchange(arrs, *, scatter, name):
    n = len(arrs)

    def body(*refs):
        _exchange_copies(refs[:n], refs[n:2 * n], refs[2 * n:], scatter, True, True)

    return pl.pallas_call(
        body, name=name, in_specs=[HBM_SPEC] * n, out_specs=[HBM_SPEC] * n, out_shape=_exchange_out_shape(arrs, scatter),
        scratch_shapes=_exchange_sems(n),
    )(*arrs)


HBM_SPEC = pl.BlockSpec(memory_space=pltpu.HBM)


def _flags(scatter, n):
    return list(scatter) if isinstance(scatter, (list, tuple)) else [scatter] * n


def _exchange_out_shape(arrs, scatter):
    return [jax.ShapeDtypeStruct(a.shape if sc else (NDEV,) + a.shape, a.dtype) for a, sc in zip(arrs, _flags(scatter, len(arrs)))]


def _exchange_sems(n):
    return [pltpu.SemaphoreType.DMA(((NDEV - 1) * n,)), pltpu.SemaphoreType.DMA(((NDEV - 1) * n,)), pltpu.SemaphoreType.DMA((n,))]


def _exchange_copies(x_refs, o_refs, sems, scatter, start, wait):
    n = len(x_refs)
    flags = _flags(scatter, n)
    send_sems, recv_sems, local_sems = sems
    ix, iy, ic = lax.axis_index("x"), lax.axis_index("y"), lax.axis_index("c")
    me = 4 * ix + 2 * iy + ic
    local = [pltpu.make_async_copy(x.at[me] if sc else x, o.at[me], local_sems.at[a])
             for a, (x, o, sc) in enumerate(zip(x_refs, o_refs, flags))]

    def peer(k):
        return (1 - ix if k & 4 else ix, 1 - iy if k & 2 else iy, 1 - ic if k & 1 else ic)

    def index(p):
        return 4 * p[0] + 2 * p[1] + p[2]

    def copy(a, k, src, dst, to):
        sem = (k - 1) * n + a
        return pltpu.make_async_remote_copy(src_ref=src, dst_ref=dst, send_sem=send_sems.at[sem], recv_sem=recv_sems.at[sem],
                                            device_id=to, device_id_type=MESH)

    sends, arrivals, passed_on = [], [], []
    for a, (x, o, sc) in enumerate(zip(x_refs, o_refs, flags)):
        if sc:
            for k in range(1, NDEV):
                p = peer(k)
                sends.append(copy(a, k, x.at[index(p)], o.at[me], p))
                arrivals.append(copy(a, k, x.at[me], o.at[index(p)], p))
        else:
            sib = peer(1)
            sends.append(copy(a, 1, x, o.at[me], sib))
            arrivals.append(copy(a, 1, x, o.at[index(sib)], sib))
            for k in (2, 4, 6):
                p, q = peer(k), peer(k + 1)
                sends.append(copy(a, k, x, o.at[me], p))
                passed_on.append((copy(a, k, x, o.at[index(p)], p), copy(a, k + 1, o.at[index(p)], o.at[index(p)], sib)))
                arrivals.append(copy(a, k + 1, o.at[index(q)], o.at[index(q)], sib))
    if start:
        for cp in local + sends:
            cp.start()
    if wait:
        for landed, hand_over in passed_on:
            landed.wait_recv()
            hand_over.start()
        for cp in arrivals:
            cp.wait_recv()
        for cp in sends + [hand_over for _, hand_over in passed_on]:
            cp.wait_send()
        for cp in local:
            cp.wait()


def _call(body, args, *, name, grid, in_specs, out_specs, out_shape, scratch_shapes=(), sem, xchg=None):
    if xchg is None:
        return pl.pallas_call(body, name=name, grid=grid, in_specs=in_specs, out_specs=out_specs, out_shape=out_shape,
                              scratch_shapes=list(scratch_shapes), compiler_params=_params(*sem))(*args), None
    arrs, scatter = xchg
    n, ni, no, ns = len(arrs), len(in_specs), len(out_specs), len(scratch_shapes)

    def wrapped(*refs):
        ins, x_refs = refs[:ni], refs[ni:ni + n]
        outs, o_refs = refs[ni + n:ni + n + no], refs[ni + n + no:ni + 2 * n + no]
        scratch, sems = refs[ni + 2 * n + no:ni + 2 * n + no + ns], refs[ni + 2 * n + no + ns:]
        ids = [pl.program_id(a) for a in range(len(grid))]
        first = functools.reduce(jnp.logical_and, [p == 0 for p in ids])
        last = functools.reduce(jnp.logical_and, [p == g - 1 for p, g in zip(ids, grid)])

        @pl.when(first)
        def _():
            _exchange_copies(x_refs, o_refs, sems, scatter, True, False)

        body(*ins, *outs, *scratch)

        @pl.when(last)
        def _():
            _exchange_copies(x_refs, o_refs, sems, scatter, False, True)

    res = pl.pallas_call(
        wrapped, name=name, grid=grid, in_specs=list(in_specs) + [HBM_SPEC] * n, out_specs=list(out_specs) + [HBM_SPEC] * n,
        out_shape=list(out_shape) + _exchange_out_shape(arrs, scatter),
        scratch_shapes=list(scratch_shapes) + _exchange_sems(n),
        compiler_params=_params(*("arbitrary",) * len(grid)))(*args, *arrs)
    return res[:no], res[no:]


def _sum_parts(parts, *, name):
    r = parts.shape[1]

    def body(p_ref, o_ref):
        acc = p_ref[0]
        for j in range(1, NDEV):
            acc = acc + p_ref[j]
        o_ref[...] = acc

    return pl.pallas_call(body, name=name, out_shape=jax.ShapeDtypeStruct((r, LANES), f32), compiler_params=_params())(parts)


def _col_pieces(n, bounds):
    out = []
    for p, (a, b) in enumerate(bounds):
        for j in range(NDEV):
            lo, hi = max(a, n * j), min(b, n * (j + 1))
            if lo < hi:
                out.append((p, j, lo - a, lo - n * j, hi - lo))
    return out


def _unshard_cols(stacked, bounds, *, name):
    _, k, n = stacked.shape
    tk = _tile(k, 256)
    plan = _col_pieces(n, bounds)

    def body(x_ref, *o_refs):
        for p, j, po, so, w in plan:
            o_refs[p][:, po:po + w] = x_ref[j, :, so:so + w]

    return pl.pallas_call(
        body, name=name, grid=(k // tk,), in_specs=[pl.BlockSpec((NDEV, tk, n), lambda i: (0, i, 0))],
        out_specs=[pl.BlockSpec((tk, b - a), lambda i: (i, 0)) for a, b in bounds],
        out_shape=[jax.ShapeDtypeStruct((k, b - a), stacked.dtype) for a, b in bounds],
        compiler_params=_params("parallel"))(stacked)


def _shard_cols(pieces, *, out_dtype, name):
    k = pieces[0].shape[0]
    bounds, off = [], 0
    for p in pieces:
        bounds.append((off, off + p.shape[1]))
        off += p.shape[1]
    n = off // NDEV
    tk = _tile(k, 256)
    plan = _col_pieces(n, bounds)

    def body(*refs):
        o_ref = refs[-1]
        for p, j, po, so, w in plan:
            o_ref[j, :, so:so + w] = refs[p][:, po:po + w].astype(out_dtype)

    return pl.pallas_call(
        body, name=name, grid=(k // tk,), in_specs=[pl.BlockSpec((tk, b - a), lambda i: (i, 0)) for a, b in bounds],
        out_specs=pl.BlockSpec((NDEV, tk, n), lambda i: (0, i, 0)),
        out_shape=jax.ShapeDtypeStruct((NDEV, k, n), out_dtype),
        compiler_params=_params("parallel"))(*pieces)


def _pack(arrs):
    flat = jnp.concatenate([a.reshape(-1) for a in arrs])
    pad = (-flat.shape[0]) % (SUBLANES * LANES)
    return jnp.pad(flat, (0, pad)).reshape(-1, LANES)


def _unpack(flat, shapes):
    out, off = [], 0
    for s in shapes:
        n = math.prod(s)
        out.append(flat[..., off:off + n].reshape(flat.shape[:-1] + tuple(s)))
        off += n
    return out


def _adamw_math(w, gg, m, v):
    nm = ADAM_B1 * m + (1.0 - ADAM_B1) * gg
    nv = ADAM_B2 * v + (1.0 - ADAM_B2) * (gg * gg)
    m_hat = nm / (1.0 - ADAM_B1 ** ADAM_STEP)
    v_hat = nv / (1.0 - ADAM_B2 ** ADAM_STEP)
    return -ADAM_LR * (m_hat / (jnp.sqrt(v_hat) + ADAM_EPS) + ADAM_WD * w), nm, nv


def _row_block(r, c, copies):
    tr = r
    while copies * tr * c * 4 > EW_BLOCK_BYTES and tr % (4 * SUBLANES) == 0:
        tr //= 2
    return tr


def _adamw(w, g, m, v, *, name):
    r, c = w.shape
    tr = _row_block(r, c, 1)

    def body(w_ref, g_ref, m_ref, v_ref, d_ref, nm_ref, nv_ref):
        d_ref[...], nm_ref[...], nv_ref[...] = _adamw_math(w_ref[...], g_ref[...], m_ref[...], v_ref[...])

    spec = pl.BlockSpec((tr, c), lambda i: (i, 0))
    shp = jax.ShapeDtypeStruct((r, c), f32)
    return pl.pallas_call(
        body, name=name, grid=(r // tr,), in_specs=[spec] * 4, out_specs=[spec] * 3, out_shape=[shp] * 3,
        compiler_params=_params("parallel"),
    )(w, g, m, v)


def _adamw_many(ws, gs, ms, vs, *, name):
    n = len(ws)

    def body(*refs):
        outs = refs[4 * n:]
        for i in range(n):
            res = _adamw_math(refs[i][...], refs[n + i][...], refs[2 * n + i][...], refs[3 * n + i][...])
            for o_ref, r in zip(outs[3 * i:3 * i + 3], res):
                o_ref[...] = r

    res = pl.pallas_call(body, name=name, out_shape=[jax.ShapeDtypeStruct(w.shape, f32) for w in ws for _ in range(3)],
                         compiler_params=_params())(*ws, *gs, *ms, *vs)
    return [res[3 * i:3 * i + 3] for i in range(n)]


def _sum_adamw(parts, w, m, v, *, name, xchg=None):
    r, c = w.shape
    tr = _row_block(r, c, NDEV)

    def body(p_ref, w_ref, m_ref, v_ref, g_ref, d_ref, nm_ref, nv_ref):
        gg = p_ref[0].astype(f32)
        for j in range(1, NDEV):
            gg = gg + p_ref[j].astype(f32)
        g_ref[...] = gg
        d_ref[...], nm_ref[...], nv_ref[...] = _adamw_math(w_ref[...], gg, m_ref[...], v_ref[...])

    spec = pl.BlockSpec((tr, c), lambda i: (i, 0))
    shp = jax.ShapeDtypeStruct((r, c), f32)
    res, got = _call(
        body, (parts, w, m, v), name=name, grid=(r // tr,),
        in_specs=[pl.BlockSpec((NDEV, tr, c), lambda i: (0, i, 0))] + [spec] * 3,
        out_specs=[spec] * 4, out_shape=[shp] * 4, sem=("parallel",), xchg=xchg)
    return tuple(res) if xchg is None else tuple(res) + (got,)


def _cond_fwd(c_all, w, b, *, name):
    nb, n = c_all.shape[0], w.shape[1]

    def body(c_ref, w_ref, b_ref, o_ref):
        cc = c_ref[...]
        o_ref[...] = jnp.dot(cc * _sig(cc), w_ref[...], preferred_element_type=f32,
                             precision=lax.Precision.HIGHEST) + b_ref[...]

    return pl.pallas_call(body, name=name, out_shape=jax.ShapeDtypeStruct((nb, n), f32),
                          compiler_params=_params())(c_all, w, b)


def _cond_bwd(c_all, dmod, *, name):
    d, n = c_all.shape[1], dmod.shape[1]

    def body(c_ref, g_ref, o_ref):
        cc = c_ref[...]
        o_ref[...] = lax.dot_general(cc * _sig(cc), g_ref[...], (((0,), (0,)), ((), ())), preferred_element_type=f32,
                                     precision=lax.Precision.HIGHEST)

    return pl.pallas_call(body, name=name, out_shape=jax.ShapeDtypeStruct((d, n), f32),
                          compiler_params=_params())(c_all, dmod)


def _ssm_disc(lam_re, lam_im, log_dt):
    lr = jnp.minimum(lam_re, -1e-4)
    li = lam_im
    dt = jnp.exp(log_dt)
    mag = jnp.exp(lr * dt)
    ang = li * dt
    lbr, lbi = mag * jnp.cos(ang), mag * jnp.sin(ang)
    num_r, num_i = lbr - 1.0, lbi
    den = lr * lr + li * li
    return lbr, lbi, (num_r * lr + num_i * li) / den, (num_i * lr - num_r * li) / den


def _ssm_prep(lam_re, lam_im, log_dt, *, name):
    def body(a, b, c, o1, o2, o3, o4):
        o1[...], o2[...], o3[...], o4[...] = _ssm_disc(a[...], b[...], c[...])

    shp = jax.ShapeDtypeStruct(lam_re.shape, f32)
    return pl.pallas_call(body, name=name, out_shape=[shp] * 4, compiler_params=_params())(lam_re, lam_im, log_dt)


def _ssm_prep_bwd(lam_re, lam_im, log_dt, cts, *, name):
    def body(a, b, c, g1, g2, g3, g4, o1, o2, o3):
        _, vjp = jax.vjp(_ssm_disc, a[...], b[...], c[...])
        o1[...], o2[...], o3[...] = vjp((g1[...], g2[...], g3[...], g4[...]))

    shp = jax.ShapeDtypeStruct(lam_re.shape, f32)
    return pl.pallas_call(body, name=name, out_shape=[shp, shp, jax.ShapeDtypeStruct(log_dt.shape, f32)],
                          compiler_params=_params())(lam_re, lam_im, log_dt, *cts)


def _step_major(x3):
    k, nt, c = x3.shape
    return jnp.swapaxes(x3, 0, 1).reshape(k * nt, c)


def _chunk_major(x2, nt):
    return jnp.swapaxes(x2.reshape(nt, SUBLANES, x2.shape[1]), 0, 1)


def _chain_carries(loc_r, loc_i, pr, pi_, forward):
    row = lax.broadcasted_iota(jnp.int32, loc_r.shape, 0)
    shift = 1 if forward else SUBLANES - 1
    order = range(1, SUBLANES) if forward else range(SUBLANES - 2, -1, -1)
    er, ei = loc_r, loc_i
    for k in order:
        sr, si = pltpu.roll(er, shift, 0), pltpu.roll(ei, shift, 0)
        er = jnp.where(row == k, loc_r + pr * sr - pi_ * si, er)
        ei = jnp.where(row == k, loc_i + pr * si + pi_ * sr, ei)
    edge = 0 if forward else SUBLANES - 1
    return (jnp.where(row == edge, 0.0, pltpu.roll(er, shift, 0)), jnp.where(row == edge, 0.0, pltpu.roll(ei, shift, 0)))


def _chunk_power(ar, ai, chunk_len):
    pr, pi_ = ar, ai
    for _ in range(int(math.log2(chunk_len))):
        pr, pi_ = pr * pr - pi_ * pi_, 2.0 * pr * pi_
    return pr, pi_


def _ssm_mats(bre_ref, bim_ref, cre_ref, cim_ref, cfr_ref, cfi_ref, bbar_s, cmat_s, nq):
    for q in range(nq):
        cr, ci, br, bi = cfr_ref[q], cfi_ref[q], bre_ref[q], bim_ref[q]
        bbar_s[q, :, 0:QS] = (cr * br - ci * bi).astype(bf16)
        bbar_s[q, :, QS:2 * QS] = (cr * bi + ci * br).astype(bf16)
        cmat_s[q, 0:QS, :] = cre_ref[q].astype(bf16)
        cmat_s[q, QS:2 * QS, :] = (-cim_ref[q]).astype(bf16)


def _ssm_fwd(u, ar, ai, bre, bim, cre, cim, cfr, cfi, dvec, *, name, xchg=None):
    s, sw = u.shape
    nq = sw // QW
    st = nq * 2 * QS
    tb = _tile(s, 256)
    nb, nt, chunk_len = s // tb, tb // SUBLANES, s // SUBLANES
    assert chunk_len & (chunk_len - 1) == 0 and nt % 16 == 0

    def body(u_ref, ar_ref, ai_ref, bre_ref, bim_ref, cre_ref, cim_ref, cfr_ref, cfi_ref, d_ref,
             h_out, yraw_out, y_out, buf, hc, bbar_s, cmat_s):
        ph, i = pl.program_id(0), pl.program_id(1)

        @pl.when(i == 0)
        def _():
            _ssm_mats(bre_ref, bim_ref, cre_ref, cim_ref, cfr_ref, cfi_ref, bbar_s, cmat_s, nq)

        @pl.when((ph == 0) & (i == 0))
        def _():
            hc[...] = jnp.zeros_like(hc)

        @pl.when((ph == 1) & (i == 0))
        def _():
            for q in range(nq):
                o = q * 2 * QS
                pr, pi_ = _chunk_power(ar_ref[q], ai_ref[q], chunk_len)
                sr, si = _chain_carries(hc[:, o:o + QS], hc[:, o + QS:o + 2 * QS], pr, pi_, True)
                hc[:, o:o + QS] = sr
                hc[:, o + QS:o + 2 * QS] = si

        uu = u_ref[...]
        up = _step_major(uu).astype(bf16)
        for q in range(nq):
            o = q * 2 * QS
            buf[:, o:o + 2 * QS] = jnp.dot(up[:, q * QW:(q + 1) * QW], bbar_s[q], preferred_element_type=f32)

        for q in range(nq):
            o = q * 2 * QS
            a_r = jnp.broadcast_to(ar_ref[q], (SUBLANES, QS))
            a_i = jnp.broadcast_to(ai_ref[q], (SUBLANES, QS))

            def step(t, carry, o=o, a_r=a_r, a_i=a_i):
                hr, hi = carry
                r0 = pl.multiple_of(t * SUBLANES, SUBLANES)
                nr = a_r * hr - a_i * hi + buf[pl.ds(r0, SUBLANES), o:o + QS]
                ni = a_r * hi + a_i * hr + buf[pl.ds(r0, SUBLANES), o + QS:o + 2 * QS]
                buf[pl.ds(r0, SUBLANES), o:o + QS] = nr
                buf[pl.ds(r0, SUBLANES), o + QS:o + 2 * QS] = ni
                return nr, ni

            hr, hi = lax.fori_loop(0, nt, step, (hc[:, o:o + QS], hc[:, o + QS:o + 2 * QS]))
            hc[:, o:o + QS] = hr
            hc[:, o + QS:o + 2 * QS] = hi

        @pl.when(ph == 1)
        def _():
            for q in range(nq):
                o = q * 2 * QS
                cs = slice(q * QW, (q + 1) * QW)
                hq = buf[:, o:o + 2 * QS].astype(bf16)
                h_out[:, o:o + 2 * QS] = hq
                yq = _chunk_major(jnp.dot(hq, cmat_s[q], preferred_element_type=f32), nt) + d_ref[:, cs] * uu[:, :, cs]
                yraw_out[:, :, cs] = yq
                y_out[:, :, cs] = _gelu(yq).astype(bf16)

    blk = lambda ph, i: (0, i, 0)
    oblk = lambda ph, i: (0, i * ph, 0)
    act = lambda dt: jax.ShapeDtypeStruct((SUBLANES, chunk_len, sw), dt)
    (h_p, yraw3, y3), got = _call(
        body, (u.reshape(SUBLANES, chunk_len, sw), ar, ai, bre, bim, cre, cim, cfr, cfi, dvec), name=name, grid=(2, nb),
        in_specs=[pl.BlockSpec((SUBLANES, nt, sw), blk), _full(ar.shape), _full(ai.shape), _full(bre.shape), _full(bim.shape),
                  _full(cre.shape), _full(cim.shape), _full(cfr.shape), _full(cfi.shape), _full(dvec.shape)],
        out_specs=[pl.BlockSpec((tb, st), lambda ph, i: (i * ph, 0)), pl.BlockSpec((SUBLANES, nt, sw), oblk),
                   pl.BlockSpec((SUBLANES, nt, sw), oblk)],
        out_shape=[jax.ShapeDtypeStruct((s, st), bf16), act(f32), act(bf16)],
        scratch_shapes=[pltpu.VMEM((tb, st), f32), pltpu.VMEM((SUBLANES, st), f32),
                        pltpu.VMEM((nq, QW, 2 * QS), bf16), pltpu.VMEM((nq, 2 * QS, QW), bf16)],
        sem=("arbitrary", "arbitrary"), xchg=xchg)
    return h_p, yraw3, y3.reshape(s, sw), got


def _ssm_bwd(dy, yraw3, u, h_p, ar, ai, bre, bim, cre, cim, cfr, cfi, dvec, *, name, xchg=None):
    s, sw = u.shape
    nq = sw // QW
    st = nq * 2 * QS
    tb = _tile(s, 256)
    nb, nt, chunk_len = s // tb, tb // SUBLANES, s // SUBLANES

    def body(dy_ref, yraw_ref, u_ref, h_ref, ar_ref, ai_ref, bre_ref, bim_ref, cre_ref, cim_ref, cfr_ref, cfi_ref, d_ref,
             du_out, dbre_out, dbim_out, dcre_out, dcim_out, dcfr_out, dcfi_out, dlbr_out, dlbi_out, dd_out, dbu_out,
             buf, hf, rc, acc, dbbar, dcmat, bbar_s, cmat_s):
        ph, i = pl.program_id(0), pl.program_id(1)

        @pl.when(i == 0)
        def _():
            _ssm_mats(bre_ref, bim_ref, cre_ref, cim_ref, cfr_ref, cfi_ref, bbar_s, cmat_s, nq)

        @pl.when((ph == 0) & (i == 0))
        def _():
            rc[...] = jnp.zeros_like(rc)

        @pl.when((ph == 1) & (i == 0))
        def _():
            for q in range(nq):
                o = q * 2 * QS
                pr, pi_ = _chunk_power(ar_ref[q], ai_ref[q], chunk_len)
                sr, si = _chain_carries(rc[:, o:o + QS], rc[:, o + QS:o + 2 * QS], pr, -pi_, False)
                rc[:, o:o + QS] = sr
                rc[:, o + QS:o + 2 * QS] = si
            acc[...] = jnp.zeros_like(acc)
            dbbar[...] = jnp.zeros_like(dbbar)
            dcmat[...] = jnp.zeros_like(dcmat)
            dd_out[...] = jnp.zeros_like(dd_out)
            dbu_out[...] = jnp.zeros_like(dbu_out)

        dyraw = dy_ref[...].astype(f32) * _gelu_grad(yraw_ref[...])
        dyp = _step_major(dyraw).astype(bf16)
        for q in range(nq):
            o = q * 2 * QS
            buf[:, o:o + 2 * QS] = lax.dot_general(dyp[:, q * QW:(q + 1) * QW], cmat_s[q], (((1,), (1,)), ((), ())),
                                                   preferred_element_type=f32)

        def recur(with_grad):
            for q in range(nq):
                o = q * 2 * QS
                a_r = jnp.broadcast_to(ar_ref[q], (SUBLANES, QS))
                a_i = jnp.broadcast_to(ai_ref[q], (SUBLANES, QS))

                def step(j, carry, o=o, a_r=a_r, a_i=a_i):
                    r0 = pl.multiple_of((nt - 1 - j) * SUBLANES, SUBLANES)
                    if with_grad:
                        rr, ri, gr, gi = carry
                        hr = hf[pl.ds(r0, SUBLANES), o:o + QS]
                        hi = hf[pl.ds(r0, SUBLANES), o + QS:o + 2 * QS]
                        gr = gr + hr * rr + hi * ri
                        gi = gi + hr * ri - hi * rr
                    else:
                        rr, ri = carry
                    nr = buf[pl.ds(r0, SUBLANES), o:o + QS] + a_r * rr + a_i * ri
                    ni = buf[pl.ds(r0, SUBLANES), o + QS:o + 2 * QS] + a_r * ri - a_i * rr
                    buf[pl.ds(r0, SUBLANES), o:o + QS] = nr
                    buf[pl.ds(r0, SUBLANES), o + QS:o + 2 * QS] = ni
                    return (nr, ni, gr, gi) if with_grad else (nr, ni)

                init = (rc[:, o:o + QS], rc[:, o + QS:o + 2 * QS])
                if with_grad:
                    init = init + (acc[:, o:o + QS], acc[:, o + QS:o + 2 * QS])
                res = lax.fori_loop(0, nt, step, init)
                rc[:, o:o + QS] = res[0]
                rc[:, o + QS:o + 2 * QS] = res[1]
                if with_grad:
                    acc[:, o:o + QS] = res[2]
                    acc[:, o + QS:o + 2 * QS] = res[3]

        @pl.when(ph == 0)
        def _():
            recur(False)

        @pl.when(ph == 1)
        def _():
            hf[...] = h_ref[...].astype(f32)
            recur(True)
            uu = u_ref[...]
            up = _step_major(uu).astype(bf16)
            dd_out[...] += _colsum((dyraw * uu).reshape(tb, sw))
            for q in range(nq):
                o = q * 2 * QS
                cs = slice(q * QW, (q + 1) * QW)
                lam = buf[:, o:o + 2 * QS].astype(bf16)
                duq = _chunk_major(lax.dot_general(lam, bbar_s[q], (((1,), (1,)), ((), ())), preferred_element_type=f32), nt) \
                    + d_ref[:, cs] * dyraw[:, :, cs]
                du_out[:, :, cs] = duq.astype(bf16)
                dbu_out[:, cs] += _colsum(duq.reshape(tb, QW))
                dbbar[q] += lax.dot_general(up[:, cs], lam, (((0,), (0,)), ((), ())), preferred_element_type=f32)
                dcmat[q] += lax.dot_general(dyp[:, cs], h_ref[:, o:o + 2 * QS], (((0,), (0,)), ((), ())),
                                            preferred_element_type=f32)

        @pl.when((ph == 1) & (i == nb - 1))
        def _():
            for q in range(nq):
                o = q * 2 * QS
                cr, ci, br, bi = cfr_ref[q], cfi_ref[q], bre_ref[q], bim_ref[q]
                gr, gi = dbbar[q, :, 0:QS], dbbar[q, :, QS:2 * QS]
                dbre_out[q] = cr * gr + ci * gi
                dbim_out[q] = cr * gi - ci * gr
                dcfr_out[q] = _colsum(gr * br + gi * bi)
                dcfi_out[q] = _colsum(gi * br - gr * bi)
                dcre_out[q] = dcmat[q, :, 0:QS].T
                dcim_out[q] = -dcmat[q, :, QS:2 * QS].T
                dlbr_out[q] = _colsum(acc[:, o:o + QS])
                dlbi_out[q] = _colsum(acc[:, o + QS:o + 2 * QS])

    blk = lambda ph, i: (0, nb - 1 - i, 0)
    oblk = lambda ph, i: (0, (nb - 1 - i) * ph + (nb - 1) * (1 - ph), 0)
    pshapes = [ar.shape, ai.shape, bre.shape, bim.shape, cre.shape, cim.shape, cfr.shape, cfi.shape, dvec.shape]
    oshapes = [bre.shape, bim.shape, cre.shape, cim.shape, cfr.shape, cfi.shape, ar.shape, ai.shape, dvec.shape, dvec.shape]
    act = pl.BlockSpec((SUBLANES, nt, sw), blk)
    view = lambda a: a.reshape(SUBLANES, chunk_len, sw)
    res, got = _call(
        body, (view(dy), yraw3, view(u), h_p, ar, ai, bre, bim, cre, cim, cfr, cfi, dvec), name=name, grid=(2, nb),
        in_specs=[act, act, act, pl.BlockSpec((tb, st), lambda ph, i: (nb - 1 - i, 0))] + [_full(p) for p in pshapes],
        out_specs=[pl.BlockSpec((SUBLANES, nt, sw), oblk)] + [_full(p) for p in oshapes],
        out_shape=[jax.ShapeDtypeStruct((SUBLANES, chunk_len, sw), bf16)] + [jax.ShapeDtypeStruct(p, f32) for p in oshapes],
        scratch_shapes=[pltpu.VMEM((tb, st), f32), pltpu.VMEM((tb, st), f32),
                        pltpu.VMEM((SUBLANES, st), f32), pltpu.VMEM((SUBLANES, st), f32),
                        pltpu.VMEM((nq, QW, 2 * QS), f32), pltpu.VMEM((nq, QW, 2 * QS), f32),
                        pltpu.VMEM((nq, QW, 2 * QS), bf16), pltpu.VMEM((nq, 2 * QS, QW), bf16)],
        sem=("arbitrary", "arbitrary"), xchg=xchg)
    return (res[0].reshape(s, sw),) + tuple(res[1:]) + (got,)


def _lnmod(x, sc, sh, *, name):
    s, d = x.shape
    tb = _tile(s, 512)

    def body(x_ref, sc_ref, sh_ref, o_ref):
        xh, _ = _ln(x_ref[...])
        o_ref[...] = (xh * (1.0 + sc_ref[...]) + sh_ref[...]).astype(bf16)

    blk = pl.BlockSpec((tb, d), lambda i: (i, 0))
    vec = pl.BlockSpec((1, d), _row)
    return pl.pallas_call(body, name=name, grid=(s // tb,), in_specs=[blk, vec, vec], out_specs=blk,
                          out_shape=jax.ShapeDtypeStruct((s, d), bf16), compiler_params=_params("parallel"))(x, sc, sh)


ROWS = 32


def _row_chunks(n_rows, rows, fn, init, start=0):
    return lax.fori_loop(start, n_rows // rows, lambda c, carry: fn(pl.multiple_of(c * rows, rows), carry), init)


def _rows_from(win, o, rows):
    if o % SUBLANES == 0:
        return win[o:o + rows]
    n = win.shape[0]
    return pltpu.roll(win, (n - o) % n, 0)[0:rows]


def _window_before(ref, halo, r0, rows, first, cols):
    if first:
        return jnp.concatenate([halo, ref[pl.ds(0, rows), cols]], axis=0)
    return ref[pl.ds(pl.multiple_of(r0 - SUBLANES, SUBLANES), rows + SUBLANES), cols]


def _taps3(win, w, off, rows):
    return _rows_from(win, off, rows) * w[0] + _rows_from(win, off + 1, rows) * w[1] + _rows_from(win, off + 2, rows) * w[2]


def _fold8(x):
    acc = x[0:SUBLANES]
    for r in range(1, x.shape[0] // SUBLANES):
        acc = acc + x[r * SUBLANES:(r + 1) * SUBLANES]
    return acc


def _conv_halo_specs(tb, cw, halo, s):
    per = tb // halo
    prev = pl.BlockSpec((halo, cw), lambda i: (jnp.maximum(i * per - 1, 0), 0))
    nxt = pl.BlockSpec((halo, cw), lambda i: (jnp.minimum((i + 1) * per, s // halo - 1), 0))
    return prev, nxt


WIDE_ROWS = 16


def _shift_groups(lo, hi):
    return [(b, [o for o in range(lo, hi + 1) if o % SUBLANES == b]) for b in range(SUBLANES)]


def _shifted(win, b):
    return win if b == 0 else _rows_from(win, b, win.shape[0] - SUBLANES)


def _conv31(win, w_ref, cols, rows, lo, hi, tap_of):
    acc = None
    for b, offs in _shift_groups(lo, hi):
        if offs:
            wb = _shifted(win, b)
            for o in offs:
                term = wb[o - b:o - b + rows] * w_ref[pl.ds(tap_of(o), 1), cols]
                acc = term if acc is None else acc + term
    return acc


def _gate_into(ext, a_ref, g_ref, ah_ref, gh_ref, tb, i):
    ext[pl.ds(0, CONV_HALO), :] = jnp.where(i > 0, ah_ref[...] * _sig(gh_ref[...]), 0.0)

    def chunk(r0, carry):
        ext[pl.ds(pl.multiple_of(r0 + CONV_HALO, SUBLANES), WIDE_ROWS), :] = \
            a_ref[pl.ds(r0, WIDE_ROWS), :] * _sig(g_ref[pl.ds(r0, WIDE_ROWS), :])
        return carry

    _row_chunks(tb, WIDE_ROWS, chunk, 0)


def _causal_conv_into(v2buf, ext, w_ref, b_ref, tb, cw):
    for ct in range(cw // LANES):
        cols = slice(ct * LANES, (ct + 1) * LANES)

        def chunk(r0, carry, cols=cols):
            win = ext[pl.ds(r0, ROWS + CONV_HALO), cols]
            v2buf[pl.ds(r0, ROWS), cols] = _conv31(win, w_ref, cols, ROWS, 2, CONV_K + 1, lambda o: o - 2) + b_ref[:, cols]
            return carry

        _row_chunks(tb, ROWS, chunk, 0)


def _silu_grad(x):
    sg = _sig(x)
    return sg * (1.0 + x * (1.0 - sg))


def _conv_fwd(cva, cvg, w, b, lng, lnb, *, name, xchg=None):
    s, cw = cva.shape
    tb = _tile(s, 256)
    prev, _ = _conv_halo_specs(tb, cw, CONV_HALO, s)

    def body(a_ref, g_ref, ah_ref, gh_ref, w_ref, b_ref, lng_ref, lnb_ref, o_ref, v2_ref, ext):
        _gate_into(ext, a_ref, g_ref, ah_ref, gh_ref, tb, pl.program_id(0))
        _causal_conv_into(v2_ref, ext, w_ref, b_ref, tb, cw)
        xh, _ = _ln(v2_ref[...])
        v3 = xh * lng_ref[...] + lnb_ref[...]
        o_ref[...] = (v3 * _sig(v3)).astype(bf16)

    blk = pl.BlockSpec((tb, cw), lambda i: (i, 0))
    vec = pl.BlockSpec((1, cw), _row)
    (v4, v2), got = _call(
        body, (cva, cvg, cva, cvg, w, b, lng, lnb), name=name, grid=(s // tb,),
        in_specs=[blk, blk, prev, prev, _full(w.shape), vec, vec, vec], out_specs=[blk, blk],
        out_shape=[jax.ShapeDtypeStruct((s, cw), bf16), jax.ShapeDtypeStruct((s, cw), f32)],
        scratch_shapes=[pltpu.VMEM((tb + CONV_HALO, cw), f32)], sem=("parallel",), xchg=xchg)
    return v4, v2, got


def _conv_bwd_ln(dv4, v2, lng, lnb, *, name):
    s, cw = v2.shape
    tb = _tile(s, 256)

    def body(d_ref, v2_ref, lng_ref, lnb_ref, o_ref, dg_ref, db_ref):
        @pl.when(pl.program_id(0) == 0)
        def _():
            dg_ref[...] = jnp.zeros_like(dg_ref)
            db_ref[...] = jnp.zeros_like(db_ref)

        xh, rstd = _ln(v2_ref[...])
        v3 = xh * lng_ref[...] + lnb_ref[...]
        dv3 = d_ref[...].astype(f32) * _silu_grad(v3)
        dg_ref[...] += _colsum(dv3 * xh)
        db_ref[...] += _colsum(dv3)
        o_ref[...] = _ln_bwd(dv3 * lng_ref[...], xh, rstd)

    blk = pl.BlockSpec((tb, cw), lambda i: (i, 0))
    vec = pl.BlockSpec((1, cw), _row)
    vshape = jax.ShapeDtypeStruct((1, cw), f32)
    return pl.pallas_call(
        body, name=name, grid=(s // tb,), in_specs=[blk, blk, vec, vec],
        out_specs=[blk, vec, vec], out_shape=[jax.ShapeDtypeStruct((s, cw), f32), vshape, vshape],
        compiler_params=_params("arbitrary"))(dv4, v2, lng, lnb)


def _conv_bwd_taps(dv2, cva, cvg, w, *, name, xchg=None):
    s, cw = cva.shape
    tb = _tile(s, 256)
    nb = s // tb
    prev, nxt = _conv_halo_specs(tb, cw, CONV_HALO, s)

    def body(d_ref, dn_ref, a_ref, g_ref, ah_ref, gh_ref, w_ref, da_ref, dg_ref, dw_ref, db_ref, sa_ref, sg_ref,
             ext, dext, dvbuf, tap_sums):
        i = pl.program_id(0)

        @pl.when(i == 0)
        def _():
            for r in (dw_ref, db_ref, sa_ref, sg_ref):
                r[...] = jnp.zeros_like(r)

        _gate_into(ext, a_ref, g_ref, ah_ref, gh_ref, tb, i)
        dext[pl.ds(tb, CONV_HALO), :] = jnp.where(i < nb - 1, dn_ref[...], 0.0)

        def copy(r0, carry):
            dext[pl.ds(r0, WIDE_ROWS), :] = d_ref[pl.ds(r0, WIDE_ROWS), :]
            return carry

        _row_chunks(tb, WIDE_ROWS, copy, 0)

        for ct in range(cw // LANES):
            cols = slice(ct * LANES, (ct + 1) * LANES)

            tap_sums[...] = jnp.zeros_like(tap_sums)

            def back(r0, carry, cols=cols):
                win = dext[pl.ds(r0, ROWS + CONV_HALO), cols]
                dvbuf[pl.ds(r0, ROWS), cols] = _conv31(win, w_ref, cols, ROWS, 0, CONV_K - 1, lambda o: CONV_K - 1 - o)
                win = ext[pl.ds(r0, ROWS + CONV_HALO), cols]
                dd = d_ref[pl.ds(r0, ROWS), cols]
                for b, offs in _shift_groups(2, CONV_K + 1):
                    wb = _shifted(win, b)
                    for o in offs:
                        tap_sums[o - 2] += _fold8(dd * wb[o - b:o - b + ROWS])
                return carry

            _row_chunks(tb, ROWS, back, 0)
            for k in range(CONV_K):
                dw_ref[pl.ds(k, 1), cols] += _colsum(tap_sums[k])

        def gate_back(r0, sums):
            rows = pl.ds(r0, WIDE_ROWS)
            aa, sg, dv = a_ref[rows, :], _sig(g_ref[rows, :]), dvbuf[rows, :]
            da = dv * sg
            dgate = dv * aa * sg * (1.0 - sg)
            da_ref[rows, :] = da.astype(bf16)
            dg_ref[rows, :] = dgate.astype(bf16)
            return sums[0] + _fold8(da), sums[1] + _fold8(dgate), sums[2] + _fold8(d_ref[rows, :])

        zero = jnp.zeros((SUBLANES, cw), f32)
        sums = _row_chunks(tb, WIDE_ROWS, gate_back, (zero, zero, zero))
        sa_ref[...] += _colsum(sums[0])
        sg_ref[...] += _colsum(sums[1])
        db_ref[...] += _colsum(sums[2])

    blk = pl.BlockSpec((tb, cw), lambda i: (i, 0))
    vec = pl.BlockSpec((1, cw), _row)
    vshape = jax.ShapeDtypeStruct((1, cw), f32)
    act = jax.ShapeDtypeStruct((s, cw), bf16)
    res, got = _call(
        body, (dv2, dv2, cva, cvg, cva, cvg, w), name=name, grid=(nb,), in_specs=[blk, nxt, blk, blk, prev, prev, _full(w.shape)],
        out_specs=[blk, blk, _full(w.shape), vec, vec, vec],
        out_shape=[act, act, jax.ShapeDtypeStruct(w.shape, f32), vshape, vshape, vshape],
        scratch_shapes=[pltpu.VMEM((tb + CONV_HALO, cw), f32), pltpu.VMEM((tb + CONV_HALO, cw), f32), pltpu.VMEM((tb, cw), f32),
                        pltpu.VMEM((CONV_HALO, SUBLANES, LANES), f32)],
        sem=("arbitrary",), xchg=xchg)
    return tuple(res) + (got,)


def _glu_merge(ya, yb, ycv, gs, gc, *, name):
    s, d = ya.shape
    tb = _tile(s, 512)

    def body(ya_ref, yb_ref, ycv_ref, gs_ref, gc_ref, o_ref):
        ld = lambda r: r[...].astype(f32)
        z = ld(ya_ref) * _sig(ld(yb_ref))
        o_ref[...] = (_sig(ld(gs_ref)) * z + _sig(ld(gc_ref)) * ld(ycv_ref)).astype(bf16)

    blk = pl.BlockSpec((tb, d), lambda i: (i, 0))
    return pl.pallas_call(body, name=name, grid=(s // tb,), in_specs=[blk] * 5, out_specs=blk,
                          out_shape=jax.ShapeDtypeStruct((s, d), bf16), compiler_params=_params("parallel"))(ya, yb, ycv, gs, gc)


def _glu_merge_bwd(dm, ya, yb, ycv, gs, gc, *, name):
    s, d = ya.shape
    tb = _tile(s, 512)

    def body(dm_ref, ya_ref, yb_ref, ycv_ref, gs_ref, gc_ref, dya_ref, dyb_ref, dycv_ref, dgs_ref, dgc_ref, sgs_ref, sgc_ref):
        @pl.when(pl.program_id(0) == 0)
        def _():
            sgs_ref[...] = jnp.zeros_like(sgs_ref)
            sgc_ref[...] = jnp.zeros_like(sgc_ref)

        ld = lambda r: r[...].astype(f32)
        dmv, yav = ld(dm_ref), ld(ya_ref)
        sb, ss, scv = _sig(ld(yb_ref)), _sig(ld(gs_ref)), _sig(ld(gc_ref))
        z = yav * sb
        dz = dmv * ss
        dgs = dmv * z * ss * (1.0 - ss)
        dgc = dmv * ld(ycv_ref) * scv * (1.0 - scv)
        dya_ref[...] = (dz * sb).astype(bf16)
        dyb_ref[...] = (dz * yav * sb * (1.0 - sb)).astype(bf16)
        dycv_ref[...] = (dmv * scv).astype(bf16)
        dgs_ref[...] = dgs.astype(bf16)
        dgc_ref[...] = dgc.astype(bf16)
        sgs_ref[...] += _colsum(dgs)
        sgc_ref[...] += _colsum(dgc)

    blk = pl.BlockSpec((tb, d), lambda i: (i, 0))
    vec = pl.BlockSpec((1, d), _row)
    act = jax.ShapeDtypeStruct((s, d), bf16)
    vshape = jax.ShapeDtypeStruct((1, d), f32)
    return pl.pallas_call(body, name=name, grid=(s // tb,), in_specs=[blk] * 6, out_specs=[blk] * 5 + [vec, vec],
                          out_shape=[act] * 5 + [vshape, vshape], compiler_params=_params("arbitrary"))(dm, ya, yb, ycv, gs, gc)


def _resid_ln_mod(x, o, g, lng, lnb, sc, sh, alpha, *, name):
    s, d = x.shape
    tb = _tile(s, 512)

    def body(x_ref, o_ref, g_ref, lng_ref, lnb_ref, sc_ref, sh_ref, x1_ref, h_ref):
        xh, _ = _ln(alpha * x_ref[...] + g_ref[...] * o_ref[...].astype(f32))
        x1 = xh * lng_ref[...] + lnb_ref[...]
        x1_ref[...] = x1
        xh1, _ = _ln(x1)
        h_ref[...] = (xh1 * (1.0 + sc_ref[...]) + sh_ref[...]).astype(bf16)

    blk = pl.BlockSpec((tb, d), lambda i: (i, 0))
    vec = pl.BlockSpec((1, d), _row)
    return pl.pallas_call(body, name=name, grid=(s // tb,), in_specs=[blk, blk] + [vec] * 5, out_specs=[blk, blk],
                          out_shape=[jax.ShapeDtypeStruct((s, d), f32), jax.ShapeDtypeStruct((s, d), bf16)],
                          compiler_params=_params("parallel"))(x, o, g, lng, lnb, sc, sh)


def _resid_ln_loss(x1, y2, g, lng, lnb, tgt, alpha, *, name):
    s, d = x1.shape
    tb = _tile(s, 512)

    def body(x1_ref, y_ref, g_ref, lng_ref, lnb_ref, t_ref, dr_ref, dy_ref, loss_ref, dlg_ref, dlb_ref, dg_ref):
        @pl.when(pl.program_id(0) == 0)
        def _():
            for r in (loss_ref, dlg_ref, dlb_ref, dg_ref):
                r[...] = jnp.zeros_like(r)

        yv = y_ref[...]
        xh, rstd = _ln(alpha * x1_ref[...] + g_ref[...] * yv)
        err = xh * lng_ref[...] + lnb_ref[...] - t_ref[...]
        loss_ref[...] += 0.5 * jnp.sum(jnp.sum(err * err, axis=-1, keepdims=True) / d, axis=0, keepdims=True)
        dx2 = err / d
        dlg_ref[...] += _colsum(dx2 * xh)
        dlb_ref[...] += _colsum(dx2)
        dr = _ln_bwd(dx2 * lng_ref[...], xh, rstd)
        dg_ref[...] += _colsum(dr * yv)
        dr_ref[...] = dr
        dy_ref[...] = (g_ref[...] * dr).astype(bf16)

    blk = pl.BlockSpec((tb, d), lambda i: (i, 0))
    vec = pl.BlockSpec((1, d), _row)
    vshape = jax.ShapeDtypeStruct((1, d), f32)
    return pl.pallas_call(
        body, name=name, grid=(s // tb,), in_specs=[blk, blk, vec, vec, vec, blk],
        out_specs=[blk, blk, pl.BlockSpec((1, 1), _row), vec, vec, vec],
        out_shape=[jax.ShapeDtypeStruct((s, d), f32), jax.ShapeDtypeStruct((s, d), bf16),
                   jax.ShapeDtypeStruct((1, 1), f32), vshape, vshape, vshape],
        compiler_params=_params("arbitrary"))(x1, y2, g, lng, lnb, tgt)


def _mid_bwd(dh2, x1, dr2, x, o, g, sc, lng, alpha, *, name):
    s, d = x.shape
    tb = _tile(s, 512)

    def body(dh_ref, x1_ref, dr2_ref, x_ref, o_ref, g_ref, sc_ref, lng_ref,
             dr1_ref, do_ref, dsc_ref, dsh_ref, dlg_ref, dlb_ref, dg_ref):
        @pl.when(pl.program_id(0) == 0)
        def _():
            for r in (dsc_ref, dsh_ref, dlg_ref, dlb_ref, dg_ref):
                r[...] = jnp.zeros_like(r)

        dh = dh_ref[...].astype(f32)
        xh1, rstd1 = _ln(x1_ref[...])
        dsc_ref[...] += _colsum(dh * xh1)
        dsh_ref[...] += _colsum(dh)
        dx1 = alpha * dr2_ref[...] + _ln_bwd(dh * (1.0 + sc_ref[...]), xh1, rstd1)
        ov = o_ref[...].astype(f32)
        xhr, rstdr = _ln(alpha * x_ref[...] + g_ref[...] * ov)
        dlg_ref[...] += _colsum(dx1 * xhr)
        dlb_ref[...] += _colsum(dx1)
        dr1 = _ln_bwd(dx1 * lng_ref[...], xhr, rstdr)
        dg_ref[...] += _colsum(dr1 * ov)
        dr1_ref[...] = dr1
        do_ref[...] = (g_ref[...] * dr1).astype(bf16)

    blk = pl.BlockSpec((tb, d), lambda i: (i, 0))
    vec = pl.BlockSpec((1, d), _row)
    vshape = jax.ShapeDtypeStruct((1, d), f32)
    return pl.pallas_call(
        body, name=name, grid=(s // tb,), in_specs=[blk] * 5 + [vec] * 3, out_specs=[blk, blk] + [vec] * 5,
        out_shape=[jax.ShapeDtypeStruct((s, d), f32), jax.ShapeDtypeStruct((s, d), bf16)] + [vshape] * 5,
        compiler_params=_params("arbitrary"))(dh2, x1, dr2, x, o, g, sc, lng)


def _final_bwd(dh1, x, dr1, sc, alpha, *, name, xchg=None):
    s, d = x.shape
    tb = _tile(s, 512)

    def body(dh_ref, x_ref, dr1_ref, sc_ref, dx_ref, dsc_ref, dsh_ref):
        @pl.when(pl.program_id(0) == 0)
        def _():
            dsc_ref[...] = jnp.zeros_like(dsc_ref)
            dsh_ref[...] = jnp.zeros_like(dsh_ref)

        dh = dh_ref[...].astype(f32)
        xh, rstd = _ln(x_ref[...])
        dsc_ref[...] += _colsum(dh * xh)
        dsh_ref[...] += _colsum(dh)
        dx_ref[...] = alpha * dr1_ref[...] + _ln_bwd(dh * (1.0 + sc_ref[...]), xh, rstd)

    blk = pl.BlockSpec((tb, d), lambda i: (i, 0))
    vec = pl.BlockSpec((1, d), _row)
    vshape = jax.ShapeDtypeStruct((1, d), f32)
    res, got = _call(body, (dh1, x, dr1, sc), name=name, grid=(s // tb,), in_specs=[blk, blk, blk, vec], out_specs=[blk, vec, vec],
                     out_shape=[jax.ShapeDtypeStruct((s, d), f32), vshape, vshape], sem=("arbitrary",), xchg=xchg)
    return tuple(res) + (got,)


TALL_ROWS = 64


def _ffn_col_tile(fh):
    return fh // 2 if (fh // 2) % LANES == 0 else fh


def _ffn_specs(s, fh, tb, tc):
    per = tb // FFN_HALO
    blk = pl.BlockSpec((tb, tc), lambda j, i: (i, j))
    prev = pl.BlockSpec((FFN_HALO, tc), lambda j, i: (jnp.maximum(i * per - 1, 0), j))
    nxt = pl.BlockSpec((FFN_HALO, tc), lambda j, i: (jnp.minimum((i + 1) * per, s // FFN_HALO - 1), j))
    taps = pl.BlockSpec((FFN_HALO, tc), lambda j, i: (0, j))
    vec = pl.BlockSpec((1, tc), lambda j, i: (0, j))
    return blk, prev, nxt, taps, vec


def _ffn_mid(upa, upv, wa, wv, ba, bv, *, name, xchg=None):
    s, fh = upa.shape
    tb, tc = _tile(s, 512), _ffn_col_tile(fh)
    blk, prev, _, taps, vec = _ffn_specs(s, fh, tb, tc)
    off = FFN_HALO - FFN_K + 1

    def body(a_ref, v_ref, ah_ref, vh_ref, wa_ref, wv_ref, ba_ref, bv_ref, o_ref):
        first = pl.program_id(1) == 0
        for lt in range(tc // LANES):
            cols = slice(lt * LANES, (lt + 1) * LANES)
            halo_a, halo_v = jnp.where(first, 0.0, ah_ref[:, cols]), jnp.where(first, 0.0, vh_ref[:, cols])
            wa = [wa_ref[pl.ds(k, 1), cols] for k in range(FFN_K)]
            wv = [wv_ref[pl.ds(k, 1), cols] for k in range(FFN_K)]
            ba, bv = ba_ref[:, cols], bv_ref[:, cols]

            def chunk(r0, carry, head=False, cols=cols, halo_a=halo_a, halo_v=halo_v, wa=wa, wv=wv, ba=ba, bv=bv):
                a2 = _taps3(_window_before(a_ref, halo_a, r0, TALL_ROWS, head, cols), wa, off, TALL_ROWS) + ba
                v2 = _taps3(_window_before(v_ref, halo_v, r0, TALL_ROWS, head, cols), wv, off, TALL_ROWS) + bv
                o_ref[pl.ds(r0, TALL_ROWS), cols] = (_gelu(a2) * v2).astype(bf16)
                return carry

            chunk(0, 0, head=True)
            _row_chunks(tb, TALL_ROWS, chunk, 0, start=1)

    (f,), got = _call(
        body, (upa, upv, upa, upv, wa, wv, ba, bv), name=name, grid=(fh // tc, s // tb),
        in_specs=[blk, blk, prev, prev, taps, taps, vec, vec], out_specs=[blk], out_shape=[jax.ShapeDtypeStruct((s, fh), bf16)],
        sem=("parallel", "arbitrary"), xchg=xchg)
    return f, got


def _ffn_mid_bwd_tile(cols, first, last, tb, off, df_ref, dfn_ref, a_ref, v_ref, ah_ref, vh_ref, an_ref, vn_ref, wa_ref, wv_ref,
                      ba_ref, bv_ref, da_ref, dv_ref, dwa_ref, dwv_ref, dba_ref, dbv_ref, dexta, dextv):
    rows_c = TALL_ROWS
    halo_a, halo_v = jnp.where(first, 0.0, ah_ref[:, cols]), jnp.where(first, 0.0, vh_ref[:, cols])
    wa = [wa_ref[pl.ds(k, 1), cols] for k in range(FFN_K)]
    wv = [wv_ref[pl.ds(k, 1), cols] for k in range(FFN_K)]
    ba, bv = ba_ref[:, cols], bv_ref[:, cols]

    def conv_cotangents(r0, rows, xa, xv, dfe):
        sa = [_rows_from(xa, off + k, rows) for k in range(FFN_K)]
        sv = [_rows_from(xv, off + k, rows) for k in range(FFN_K)]
        a2 = sa[0] * wa[0] + sa[1] * wa[1] + sa[2] * wa[2] + ba
        v2 = sv[0] * wv[0] + sv[1] * wv[1] + sv[2] * wv[2] + bv
        cdf = 0.5 * (1.0 + lax.erf(a2 * INV_SQRT2))
        da2 = dfe * v2 * (cdf + a2 * jnp.exp(-0.5 * a2 * a2) * INV_SQRT_2PI)
        dv2 = dfe * (a2 * cdf)
        dexta[pl.ds(r0, rows), cols] = da2
        dextv[pl.ds(r0, rows), cols] = dv2
        return da2, dv2, sa, sv

    def chunk(r0, sums, head=False):
        da2, dv2, sa, sv = conv_cotangents(r0, rows_c, _window_before(a_ref, halo_a, r0, rows_c, head, cols),
                                           _window_before(v_ref, halo_v, r0, rows_c, head, cols), df_ref[pl.ds(r0, rows_c), cols])
        new = [sums[k] + _fold8(da2 * sa[k]) for k in range(FFN_K)] + [sums[FFN_K] + _fold8(da2)]
        new += [sums[FFN_K + 1 + k] + _fold8(dv2 * sv[k]) for k in range(FFN_K)] + [sums[2 * FFN_K + 1] + _fold8(dv2)]
        return tuple(new)

    sums = chunk(0, tuple(jnp.zeros((SUBLANES, LANES), f32) for _ in range(2 * FFN_K + 2)), head=True)
    sums = _row_chunks(tb, rows_c, chunk, sums, start=1)
    conv_cotangents(tb, FFN_HALO,
                    jnp.concatenate([a_ref[pl.ds(tb - FFN_HALO, FFN_HALO), cols], jnp.where(last, 0.0, an_ref[:, cols])], axis=0),
                    jnp.concatenate([v_ref[pl.ds(tb - FFN_HALO, FFN_HALO), cols], jnp.where(last, 0.0, vn_ref[:, cols])], axis=0),
                    jnp.where(last, 0.0, dfn_ref[:, cols]))
    for k in range(FFN_K):
        dwa_ref[pl.ds(k, 1), cols] += _colsum(sums[k])
        dwv_ref[pl.ds(k, 1), cols] += _colsum(sums[FFN_K + 1 + k])
    dba_ref[:, cols] += _colsum(sums[FFN_K])
    dbv_ref[:, cols] += _colsum(sums[2 * FFN_K + 1])

    def back(r0, carry):
        for dext, w, o_ref in ((dexta, wa, da_ref), (dextv, wv, dv_ref)):
            dd = dext[pl.ds(r0, rows_c + FFN_HALO), cols]
            o_ref[pl.ds(r0, rows_c), cols] = (_rows_from(dd, 2, rows_c) * w[0] + _rows_from(dd, 1, rows_c) * w[1]
                                              + dd[0:rows_c] * w[2]).astype(bf16)
        return carry

    _row_chunks(tb, rows_c, back, 0)


def _ffn_mid_bwd(df, upa, upv, wa, wv, ba, bv, *, name, xchg=None):
    s, fh = upa.shape
    tb, tc = _tile(s, 512), _ffn_col_tile(fh)
    nb = s // tb
    blk, prev, nxt, taps, vec = _ffn_specs(s, fh, tb, tc)
    off = FFN_HALO - FFN_K + 1
    te = tb + FFN_HALO

    def body(df_ref, dfn_ref, a_ref, v_ref, ah_ref, vh_ref, an_ref, vn_ref, wa_ref, wv_ref, ba_ref, bv_ref,
             da_ref, dv_ref, dwa_ref, dwv_ref, dba_ref, dbv_ref, dexta, dextv):
        i = pl.program_id(1)

        @pl.when(i == 0)
        def _():
            for r in (dwa_ref, dwv_ref, dba_ref, dbv_ref):
                r[...] = jnp.zeros_like(r)

        last = i == nb - 1
        for lt in range(tc // LANES):
            _ffn_mid_bwd_tile(slice(lt * LANES, (lt + 1) * LANES), i == 0, last, tb, off, df_ref, dfn_ref, a_ref, v_ref,
                              ah_ref, vh_ref, an_ref, vn_ref, wa_ref, wv_ref, ba_ref, bv_ref, da_ref, dv_ref, dwa_ref, dwv_ref,
                              dba_ref, dbv_ref, dexta, dextv)

    act = jax.ShapeDtypeStruct((s, fh), bf16)
    wshape = jax.ShapeDtypeStruct((FFN_HALO, fh), f32)
    vshape = jax.ShapeDtypeStruct((1, fh), f32)
    res, got = _call(
        body, (df, df, upa, upv, upa, upv, upa, upv, wa, wv, ba, bv), name=name, grid=(fh // tc, nb),
        in_specs=[blk, nxt, blk, blk, prev, prev, nxt, nxt, taps, taps, vec, vec],
        out_specs=[blk, blk, taps, taps, vec, vec], out_shape=[act, act, wshape, wshape, vshape, vshape],
        scratch_shapes=[pltpu.VMEM((te, tc), f32)] * 2, sem=("parallel", "arbitrary"), xchg=xchg)
    return tuple(res) + (got,)


def _cols_from_shards(stacked):
    _, k, n = stacked.shape
    return stacked.transpose(1, 0, 2).reshape(k, NDEV * n)


def _pad_rows(w, rows):
    return jnp.pad(w, ((0, rows - w.shape[0]), (0, 0)))


def kernel(x, c, w_cond, b_cond, w_in, b_in, ssm_lambda_re, ssm_lambda_im, ssm_log_dt, ssm_b_re, ssm_b_im, ssm_c_re, ssm_c_im, ssm_d, ssm_glu_w_a, ssm_glu_w_b, cv_dw_w, cv_dw_b, cv_ln_g, cv_ln_b, cv_w_pw, w_out, ln1_g, ln1_b, ffn_w_up, ffn_dw_w, ffn_dw_b, ffn_w_down, ln2_g, ln2_b, loss_target, m_w_cond, m_b_cond, m_w_in, m_b_in, m_ssm_lambda_re, m_ssm_lambda_im, m_ssm_log_dt, m_ssm_b_re, m_ssm_b_im, m_ssm_c_re, m_ssm_c_im, m_ssm_d, m_ssm_glu_w_a, m_ssm_glu_w_b, m_cv_dw_w, m_cv_dw_b, m_cv_ln_g, m_cv_ln_b, m_cv_w_pw, m_w_out, m_ln1_g, m_ln1_b, m_ffn_w_up, m_ffn_dw_w, m_ffn_dw_b, m_ffn_w_down, m_ln2_g, m_ln2_b, v_w_cond, v_b_cond, v_w_in, v_b_in, v_ssm_lambda_re, v_ssm_lambda_im, v_ssm_log_dt, v_ssm_b_re, v_ssm_b_im, v_ssm_c_re, v_ssm_c_im, v_ssm_d, v_ssm_glu_w_a, v_ssm_glu_w_b, v_cv_dw_w, v_cv_dw_b, v_cv_ln_g, v_cv_ln_b, v_cv_w_pw, v_w_out, v_ln1_g, v_ln1_b, v_ffn_w_up, v_ffn_dw_w, v_ffn_dw_b, v_ffn_w_down, v_ln2_g, v_ln2_b):
    weights = dict(w_cond=w_cond, b_cond=b_cond, w_in=w_in, b_in=b_in, ssm_lambda_re=ssm_lambda_re, ssm_lambda_im=ssm_lambda_im, ssm_log_dt=ssm_log_dt, ssm_b_re=ssm_b_re, ssm_b_im=ssm_b_im, ssm_c_re=ssm_c_re, ssm_c_im=ssm_c_im, ssm_d=ssm_d, ssm_glu_w_a=ssm_glu_w_a, ssm_glu_w_b=ssm_glu_w_b, cv_dw_w=cv_dw_w, cv_dw_b=cv_dw_b, cv_ln_g=cv_ln_g, cv_ln_b=cv_ln_b, cv_w_pw=cv_w_pw, w_out=w_out, ln1_g=ln1_g, ln1_b=ln1_b, ffn_w_up=ffn_w_up, ffn_dw_w=ffn_dw_w, ffn_dw_b=ffn_dw_b, ffn_w_down=ffn_w_down, ln2_g=ln2_g, ln2_b=ln2_b)
    mom_m = dict(w_cond=m_w_cond, b_cond=m_b_cond, w_in=m_w_in, b_in=m_b_in, ssm_lambda_re=m_ssm_lambda_re, ssm_lambda_im=m_ssm_lambda_im, ssm_log_dt=m_ssm_log_dt, ssm_b_re=m_ssm_b_re, ssm_b_im=m_ssm_b_im, ssm_c_re=m_ssm_c_re, ssm_c_im=m_ssm_c_im, ssm_d=m_ssm_d, ssm_glu_w_a=m_ssm_glu_w_a, ssm_glu_w_b=m_ssm_glu_w_b, cv_dw_w=m_cv_dw_w, cv_dw_b=m_cv_dw_b, cv_ln_g=m_cv_ln_g, cv_ln_b=m_cv_ln_b, cv_w_pw=m_cv_w_pw, w_out=m_w_out, ln1_g=m_ln1_g, ln1_b=m_ln1_b, ffn_w_up=m_ffn_w_up, ffn_dw_w=m_ffn_dw_w, ffn_dw_b=m_ffn_dw_b, ffn_w_down=m_ffn_w_down, ln2_g=m_ln2_g, ln2_b=m_ln2_b)
    mom_v = dict(w_cond=v_w_cond, b_cond=v_b_cond, w_in=v_w_in, b_in=v_b_in, ssm_lambda_re=v_ssm_lambda_re, ssm_lambda_im=v_ssm_lambda_im, ssm_log_dt=v_ssm_log_dt, ssm_b_re=v_ssm_b_re, ssm_b_im=v_ssm_b_im, ssm_c_re=v_ssm_c_re, ssm_c_im=v_ssm_c_im, ssm_d=v_ssm_d, ssm_glu_w_a=v_ssm_glu_w_a, ssm_glu_w_b=v_ssm_glu_w_b, cv_dw_w=v_cv_dw_w, cv_dw_b=v_cv_dw_b, cv_ln_g=v_cv_ln_g, cv_ln_b=v_cv_ln_b, cv_w_pw=v_cv_w_pw, w_out=v_w_out, ln1_g=v_ln1_g, ln1_b=v_ln1_b, ffn_w_up=v_ffn_w_up, ffn_dw_w=v_ffn_dw_w, ffn_dw_b=v_ffn_dw_b, ffn_w_down=v_ffn_w_down, ln2_g=v_ln2_g, ln2_b=v_ln2_b)
    names = list(weights)

    s, d = x.shape[1], x.shape[2]
    sw = cw = d // 2
    fh = ffn_w_down.shape[1] * NDEV
    ng, nq = sw // SSM_GROUP, sw // QW
    gq = ng // nq
    alpha = 2.0 ** 0.25
    me = 4 * lax.axis_index("x") + 2 * lax.axis_index("y") + lax.axis_index("c")
    xs, tgt = x[0], loss_target[0]

    col_names = ["w_in", "ssm_glu_w_a", "ssm_glu_w_b", "cv_w_pw", "ffn_w_up"]
    row_names = ["w_out", "ffn_w_down"]
    big = col_names + row_names
    sent = lambda ns: [weights[n][0].astype(bf16) for n in ns]
    got_in, got_c, got_cv_taps, got_ffn_taps = _exchange(sent(["w_in"]) + [c, cv_dw_w[0, :, 0], ffn_dw_w[0, :, 0]],
                                                         scatter=False, name="gather_in")
    o1, o2, o3, o4 = sw, sw + cw, sw + 2 * cw, sw + 2 * cw + d
    in_bounds = ((0, o1), (o1, o2), (o2, o3), (o3, o4), (o4, o4 + d))
    w_u, w_cva, w_cvg, w_gs, w_gc = _unshard_cols(got_in, in_bounds, name="unshard_w_in")
    b_u, b_cva, b_cvg, b_gs, b_gc = (b_in[:, a:b] for a, b in in_bounds)
    c_all = got_c.reshape(NDEV, d)
    cv_taps = _cols_from_shards(got_cv_taps)
    ffn_taps = _cols_from_shards(got_ffn_taps)
    cv_w32 = _pad_rows(cv_taps, CONV_HALO)
    ffn_wa, ffn_wv = _pad_rows(ffn_taps[:, :fh], FFN_HALO), _pad_rows(ffn_taps[:, fh:], FFN_HALO)
    ffn_ba, ffn_bv = ffn_dw_b[:, :fh], ffn_dw_b[:, fh:]

    ncond = w_cond.shape[2]
    b_cond_mine = lax.dynamic_slice(b_cond, (0, me * ncond), (1, ncond))
    mod_cols = _cond_fwd(c_all, w_cond[0], b_cond_mine, name="cond_fwd")
    mod_all, = _exchange([mod_cols], scatter=False, name="gather_mod")
    mod_mine = lax.dynamic_slice(mod_all, (0, me, 0), (NDEV, 1, ncond)).reshape(1, 6 * d)
    sh1, sc1, g1, sh2, sc2, g2 = (mod_mine[:, k * d:(k + 1) * d] for k in range(6))

    lam_re, lam_im, log_dt = ssm_lambda_re[0], ssm_lambda_im[0], ssm_log_dt[0][:, None]
    lbr, lbi, cfr, cfi = _ssm_prep(lam_re, lam_im, log_dt, name="ssm_prep")
    rows_q = lambda a: a.reshape(nq, 1, QS)
    eye = jnp.eye(gq, dtype=f32)

    def b_mat(b):
        bt = b.reshape(nq, gq, SSM_STATE, SSM_GROUP).transpose(0, 1, 3, 2)
        return jnp.einsum("qgpn,gh->qgphn", bt, eye).reshape(nq, QW, QS)

    def c_mat(cc):
        ct = cc.reshape(nq, gq, SSM_GROUP, SSM_STATE)
        return jnp.einsum("qgpn,gh->qhngp", ct, eye).reshape(nq, QS, QW)

    def b_unmat(mt):
        return jnp.einsum("qgpgn->qgnp", mt.reshape(nq, gq, SSM_GROUP, gq, SSM_STATE)).reshape(ng, SSM_STATE, SSM_GROUP)

    def c_unmat(mt):
        return jnp.einsum("qgngp->qgpn", mt.reshape(nq, gq, SSM_STATE, gq, SSM_GROUP)).reshape(ng, SSM_GROUP, SSM_STATE)

    ssm_args = (rows_q(lbr), rows_q(lbi), b_mat(ssm_b_re[0]), b_mat(ssm_b_im[0]), c_mat(ssm_c_re[0]), c_mat(ssm_c_im[0]),
                rows_q(cfr), rows_q(cfi), ssm_d[0].reshape(1, sw))

    h1 = _lnmod(xs, sc1, sh1, name="ln_mod1")
    u, cva, cvg, gs, gc = _mm_fanout(h1, [w_u, w_cva, w_cvg, w_gs, w_gc], [b_u, b_cva, b_cvg, b_gs, b_gc],
                                     [f32, f32, f32, bf16, bf16], name="in_proj")
    v4, cv2, (got_a, got_b, got_pw, got_o) = _conv_fwd(
        cva, cvg, cv_w32, cv_dw_b, cv_ln_g, cv_ln_b, name="conv_fwd",
        xchg=(sent(["ssm_glu_w_a", "ssm_glu_w_b", "cv_w_pw", "w_out"]), False))
    h_p, yraw3, y, (got_up,) = _ssm_fwd(u, *ssm_args, name="ssm_fwd", xchg=(sent(["ffn_w_up"]), False))
    w_a, = _unshard_cols(got_a, ((0, d),), name="unshard_glu_a")
    w_b, = _unshard_cols(got_b, ((0, d),), name="unshard_glu_b")
    w_pw, = _unshard_cols(got_pw, ((0, d),), name="unshard_conv_pw")
    w_upa, w_upv = _unshard_cols(got_up, ((0, fh), (fh, 2 * fh)), name="unshard_ffn_up")
    w_o = got_o.reshape(d, d)
    ya = _mm([(y, w_a)], out_dtype=bf16, name="glu_a")
    yb = _mm([(y, w_b)], out_dtype=bf16, name="glu_b")
    ycv = _mm([(v4, w_pw)], out_dtype=bf16, name="conv_pw")
    merged = _glu_merge(ya, yb, ycv, gs, gc, name="merge")
    o = _mm([(merged, w_o)], out_dtype=bf16, name="out_proj")
    x1, h2 = _resid_ln_mod(xs, o, g1, ln1_g, ln1_b, sc2, sh2, alpha, name="resid_ln1")
    upa = _mm([(h2, w_upa)], name="ffn_up_a")
    upv = _mm([(h2, w_upv)], name="ffn_up_v")
    f, (got_dn,) = _ffn_mid(upa, upv, ffn_wa, ffn_wv, ffn_ba, ffn_bv, name="ffn_mid", xchg=(sent(["ffn_w_down"]), False))
    w_dn = got_dn.reshape(fh, d)
    y2 = _mm([(f, w_dn)], name="ffn_down")
    dr2, dy2, loss_part, d_ln2_g, d_ln2_b, d_g2 = _resid_ln_loss(x1, y2, g2, ln2_g, ln2_b, tgt, alpha, name="resid_ln2_loss")

    gw = {}
    df = _mm([(dy2, w_dn)], trans_w=True, name="d_ffn_down")
    gw["ffn_w_down"] = _mm_tn(f, dy2, out_dtype=bf16, name="g_ffn_down").reshape((NDEV,) + ffn_w_down[0].shape)
    received = {}
    dupa, dupv, d_ffn_wa, d_ffn_wv, d_ffn_ba, d_ffn_bv, (received["ffn_w_down"],) = _ffn_mid_bwd(
        df, upa, upv, ffn_wa, ffn_wv, ffn_ba, ffn_bv, name="ffn_mid_bwd", xchg=([gw["ffn_w_down"]], True))
    dh2 = _mm([(dupa, w_upa), (dupv, w_upv)], trans_w=True, out_dtype=bf16, name="d_ffn_up")
    gw["ffn_w_up"] = _shard_cols([_mm_tn(h2, dupa, name="g_ffn_up_a"), _mm_tn(h2, dupv, name="g_ffn_up_v")], out_dtype=bf16,
                                 name="shard_ffn_up")
    dr1, do, d_sc2, d_sh2, d_ln1_g, d_ln1_b, d_g1 = _mid_bwd(dh2, x1, dr2, xs, o, g1, sc2, ln1_g, alpha, name="mid_bwd")
    dmerged = _mm([(do, w_o)], trans_w=True, out_dtype=bf16, name="d_out_proj")
    gw["w_out"] = _mm_tn(merged, do, out_dtype=bf16, name="g_out_proj").reshape((NDEV,) + w_out[0].shape)
    dya, dyb, dycv, dgs, dgc, s_gs, s_gc = _glu_merge_bwd(dmerged, ya, yb, ycv, gs, gc, name="merge_bwd")
    dy = _mm([(dya, w_a), (dyb, w_b)], trans_w=True, out_dtype=bf16, name="d_glu")
    gw["ssm_glu_w_a"] = _shard_cols([_mm_tn(y, dya, name="g_glu_a")], out_dtype=bf16, name="shard_glu_a")
    gw["ssm_glu_w_b"] = _shard_cols([_mm_tn(y, dyb, name="g_glu_b")], out_dtype=bf16, name="shard_glu_b")
    dv4 = _mm([(dycv, w_pw)], trans_w=True, out_dtype=bf16, name="d_conv_pw")
    gw["cv_w_pw"] = _shard_cols([_mm_tn(v4, dycv, name="g_conv_pw")], out_dtype=bf16, name="shard_conv_pw")
    dv2, d_cv_ln_g, d_cv_ln_b = _conv_bwd_ln(dv4, cv2, cv_ln_g, cv_ln_b, name="conv_bwd_ln")
    dcva, dcvg, d_cv_w32, d_cv_b, s_cva, s_cvg, (received["ffn_w_up"],) = _conv_bwd_taps(
        dv2, cva, cvg, cv_w32, name="conv_bwd_taps", xchg=([gw["ffn_w_up"]], True))
    late = ["w_out", "ssm_glu_w_a", "ssm_glu_w_b", "cv_w_pw"]
    (du, d_bre_m, d_bim_m, d_cre_m, d_cim_m, d_cfr, d_cfi, d_lbr, d_lbi, d_d, s_u, got_late) = _ssm_bwd(
        dy, yraw3, u, h_p, *ssm_args, name="ssm_bwd", xchg=([gw[n] for n in late], True))
    received.update(zip(late, got_late))
    gshape = lam_re.shape
    d_lam_re, d_lam_im, d_log_dt = _ssm_prep_bwd(
        lam_re, lam_im, log_dt, [a.reshape(gshape) for a in (d_lbr, d_lbi, d_cfr, d_cfi)], name="ssm_prep_bwd")
    small = {
        "b_in": jnp.concatenate([s_u, s_cva, s_cvg, s_gs, s_gc], axis=1),
        "ssm_lambda_re": d_lam_re, "ssm_lambda_im": d_lam_im, "ssm_log_dt": d_log_dt,
        "ssm_b_re": b_unmat(d_bre_m), "ssm_b_im": b_unmat(d_bim_m), "ssm_c_re": c_unmat(d_cre_m), "ssm_c_im": c_unmat(d_cim_m),
        "ssm_d": d_d, "cv_dw_w": d_cv_w32[:CONV_K], "cv_dw_b": d_cv_b, "cv_ln_g": d_cv_ln_g, "cv_ln_b": d_cv_ln_b,
        "ln1_g": d_ln1_g, "ln1_b": d_ln1_b,
        "ffn_dw_w": jnp.concatenate([d_ffn_wa[:FFN_K], d_ffn_wv[:FFN_K]], axis=1),
        "ffn_dw_b": jnp.concatenate([d_ffn_ba, d_ffn_bv], axis=1), "ln2_g": d_ln2_g, "ln2_b": d_ln2_b,
        "mod_g1": d_g1, "mod_sh2": d_sh2, "mod_sc2": d_sc2, "mod_g2": d_g2, "loss": loss_part,
    }
    small_names = list(small)
    small_shapes = [small[n].shape for n in small_names]
    gw["w_in"], (small_all,) = _mm_tn_sharded(h1, [du, dcva, dcvg, dgs, dgc], out_dtype=bf16, name="g_in",
                                              xchg=([_pack([small[n] for n in small_names])], False))
    dh1, (received["w_in"],) = _mm(
        [(du, w_u), (dcva, w_cva), (dcvg, w_cvg), (dgs, w_gs), (dgc, w_gc)], trans_w=True, out_dtype=bf16, name="d_in",
        xchg=([gw["w_in"]], True))
    grad_x, d_sc1, d_sh1, _ = _final_bwd(dh1, xs, dr1, sc1, alpha, name="final_bwd")

    grads, delta, new_m, new_v = {}, {}, {}, {}
    for n in big:
        ride = ([_pack([d_sh1, d_sc1])], False) if n == "w_out" else None
        res = _sum_adamw(received[n], weights[n][0], mom_m[n][0], mom_v[n][0], name="adamw_" + n, xchg=ride)
        grads[n], delta[n], new_m[n], new_v[n] = res[:4]
        if ride is not None:
            last_all, = res[4]

    small_sum = dict(zip(small_names, _unpack(_sum_parts(small_all, name="sum_small").reshape(-1), small_shapes)))
    last_sum = _unpack(_sum_parts(last_all, name="sum_last").reshape(-1), [(1, d), (1, d)])
    per_dev = dict(zip(small_names, _unpack(small_all.reshape(NDEV, -1), small_shapes)))
    last_dev = _unpack(last_all.reshape(NDEV, -1), [(1, d), (1, d)])
    dmod_all = jnp.concatenate(last_dev + [per_dev[k] for k in ("mod_g1", "mod_sh2", "mod_sc2", "mod_g2")], axis=-1).reshape(NDEV, 6 * d)
    dmod_cols = lax.dynamic_slice(dmod_all.reshape(NDEV, NDEV, ncond), (0, me, 0), (NDEV, 1, ncond)).reshape(NDEV, ncond)
    grads["w_cond"] = _cond_bwd(c_all, dmod_cols, name="cond_bwd")
    loss = small_sum.pop("loss").reshape(())
    grads["b_cond"] = jnp.concatenate(last_sum + [small_sum.pop(k) for k in ("mod_g1", "mod_sh2", "mod_sc2", "mod_g2")], axis=1)
    for n, g in small_sum.items():
        grads[n] = g
    ntap = cv_dw_w.shape[3]
    grads["cv_dw_w"] = lax.dynamic_slice(grads["cv_dw_w"], (0, me * ntap), (CONV_K, ntap))
    nffn = ffn_dw_w.shape[3]
    grads["ffn_dw_w"] = lax.dynamic_slice(grads["ffn_dw_w"], (0, me * nffn), (FFN_K, nffn))
    grads = {n: grads[n].reshape(weights[n].shape) for n in names}

    delta["w_cond"], new_m["w_cond"], new_v["w_cond"] = _adamw(w_cond[0], grads["w_cond"][0], m_w_cond[0], v_w_cond[0],
                                                               name="adamw_w_cond")
    rest = [n for n in names if n not in ["w_cond"] + big]
    squeeze = lambda a: a if a.ndim == 2 else a[0]
    results = _adamw_many(*[[squeeze(t[n].reshape(weights[n].shape)) for n in rest] for t in (weights, grads, mom_m, mom_v)],
                          name="adamw_small")
    for n, (dl, nm, nv) in zip(rest, results):
        delta[n], new_m[n], new_v[n] = dl, nm, nv
    shaped = lambda t: [t[n].reshape(weights[n].shape) for n in names]

    return (loss, grad_x[None], *shaped(grads), *shaped(delta), *shaped(new_m), *shaped(new_v))
```

```python
import functools
import math

import jax
import jax.numpy as jnp
from jax import lax
from jax.experimental import pallas as pl
from jax.experimental.pallas import tpu as pltpu

f32 = jnp.float32
bf16 = jnp.bfloat16

NDEV = 8
LANES = 128
SUBLANES = 8
SSM_GROUP = 16
SSM_STATE = 64
QW = 128
QS = 512
CONV_K = 31
CONV_HALO = 32
FFN_K = 3
FFN_HALO = 8
LN_EPS = 1e-5
ADAM_LR, ADAM_B1, ADAM_B2, ADAM_EPS, ADAM_WD, ADAM_STEP = 0.001, 0.9, 0.999, 1e-08, 0.01, 10
VMEM_LIMIT = 56 * 1024 * 1024
W_TILE_BYTES = 6 * 1024 * 1024
SUM_ROWS = 512
EW_BLOCK_BYTES = 2 * 1024 * 1024
INV_SQRT2 = 1.0 / math.sqrt(2.0)
INV_SQRT_2PI = 1.0 / math.sqrt(2.0 * math.pi)
MESH = pl.DeviceIdType.MESH


def _tile(n, want):
    t = min(n, want)
    while n % t:
        t //= 2
    return t


def _col_tile(n, rows, bytes_per):
    best = LANES if n % LANES == 0 else n
    for t in range(LANES, n + 1, LANES):
        if n % t == 0 and rows * t * bytes_per <= W_TILE_BYTES:
            best = t
    return best


def _params(*sem):
    return pltpu.CompilerParams(dimension_semantics=sem, vmem_limit_bytes=VMEM_LIMIT)


def _row(i):
    return (0, 0)


def _full(shape):
    nd = len(shape)
    return pl.BlockSpec(shape, lambda *a: (0,) * nd)


def _ln(x):
    mu = jnp.mean(x, axis=-1, keepdims=True)
    xc = x - mu
    var = jnp.mean(xc * xc, axis=-1, keepdims=True)
    rstd = lax.rsqrt(var + LN_EPS)
    return xc * rstd, rstd


def _ln_bwd(dxhat, xhat, rstd):
    return rstd * (dxhat - jnp.mean(dxhat, axis=-1, keepdims=True) - xhat * jnp.mean(dxhat * xhat, axis=-1, keepdims=True))


def _sig(x):
    return 1.0 / (1.0 + jnp.exp(-x))


def _gelu(x):
    return 0.5 * x * (1.0 + lax.erf(x * INV_SQRT2))


def _gelu_grad(x):
    return 0.5 * (1.0 + lax.erf(x * INV_SQRT2)) + x * jnp.exp(-0.5 * x * x) * INV_SQRT_2PI


def _colsum(x):
    return jnp.sum(x, axis=0, keepdims=True)


def _mm(pairs, bias=None, *, trans_w=False, out_dtype=f32, name, xchg=None):
    n_p = len(pairs)
    m = pairs[0][0].shape[0]
    n = pairs[0][1].shape[0 if trans_w else 1]
    ktot = sum(x.shape[1] for x, _ in pairs)
    tm = _tile(m, 512)
    tn = _col_tile(n, ktot, 2)
    dn = (((1,), (1,)), ((), ())) if trans_w else (((1,), (0,)), ((), ()))

    def body(*refs):
        o_ref = refs[-1]
        acc = None
        for xr, wr in zip(refs[:n_p], refs[n_p:2 * n_p]):
            r = lax.dot_general(xr[...].astype(bf16), wr[...].astype(bf16), dn, preferred_element_type=f32)
            acc = r if acc is None else acc + r
        if bias is not None:
            acc = acc + refs[2 * n_p][...]
        o_ref[...] = acc.astype(out_dtype)

    in_specs = [pl.BlockSpec((tm, x.shape[1]), lambda j, i: (i, 0)) for x, _ in pairs]
    if trans_w:
        in_specs += [pl.BlockSpec((tn, w.shape[1]), lambda j, i: (j, 0)) for _, w in pairs]
    else:
        in_specs += [pl.BlockSpec((w.shape[0], tn), lambda j, i: (0, j)) for _, w in pairs]
    args = [x for x, _ in pairs] + [w for _, w in pairs]
    if bias is not None:
        in_specs.append(pl.BlockSpec((1, tn), lambda j, i: (0, j)))
        args.append(bias)
    (out,), got = _call(
        body, args, name=name, grid=(n // tn, m // tm), in_specs=in_specs,
        out_specs=[pl.BlockSpec((tm, tn), lambda j, i: (i, j))], out_shape=[jax.ShapeDtypeStruct((m, n), out_dtype)],
        sem=("parallel", "arbitrary"), xchg=xchg)
    return out if xchg is None else (out, got)


def _mm_fanout(x, ws, biases, out_dtypes, *, name):
    m, k = x.shape
    tm = _tile(m, 512)
    n_w = len(ws)
    biases = list(biases or [])

    def body(x_ref, *refs):
        xb = x_ref[...].astype(bf16)
        o_refs = refs[n_w + len(biases):]
        for p, (w_ref, o_ref, dt) in enumerate(zip(refs[:n_w], o_refs, out_dtypes)):
            acc = jnp.dot(xb, w_ref[...], preferred_element_type=f32)
            if biases:
                acc = acc + refs[n_w + p][...]
            o_ref[...] = acc.astype(dt)

    return pl.pallas_call(
        body, name=name, grid=(m // tm,),
        in_specs=[pl.BlockSpec((tm, k), lambda i: (i, 0))] + [_full(w.shape) for w in ws] + [_full(b.shape) for b in biases],
        out_specs=[pl.BlockSpec((tm, w.shape[1]), lambda i: (i, 0)) for w in ws],
        out_shape=[jax.ShapeDtypeStruct((m, w.shape[1]), dt) for w, dt in zip(ws, out_dtypes)],
        compiler_params=_params("parallel"))(x, *ws, *biases)


def _mm_tn(x, dy, *, out_dtype=f32, name):
    m, k = x.shape
    n = dy.shape[1]
    tm = _tile(m, 512)
    tn = _col_tile(n, k, 4)
    steps = m // tm

    def body(x_ref, dy_ref, o_ref, *scratch):
        acc = scratch[0] if scratch else o_ref

        @pl.when(pl.program_id(1) == 0)
        def _():
            acc[...] = jnp.zeros_like(acc)

        acc[...] += lax.dot_general(x_ref[...].astype(bf16), dy_ref[...].astype(bf16), (((0,), (0,)), ((), ())),
                                    preferred_element_type=f32)
        if scratch:
            @pl.when(pl.program_id(1) == steps - 1)
            def _():
                o_ref[...] = acc[...].astype(out_dtype)

    return pl.pallas_call(
        body, name=name, grid=(n // tn, steps),
        in_specs=[pl.BlockSpec((tm, k), lambda j, i: (i, 0)), pl.BlockSpec((tm, tn), lambda j, i: (i, j))],
        out_specs=pl.BlockSpec((k, tn), lambda j, i: (0, j)),
        out_shape=jax.ShapeDtypeStruct((k, n), out_dtype),
        scratch_shapes=[] if out_dtype == f32 else [pltpu.VMEM((k, tn), f32)],
        compiler_params=_params("parallel", "arbitrary"),
    )(x, dy)


def _mm_tn_sharded(x, dys, *, out_dtype, name, xchg=None):
    m, k = x.shape
    widths = [dy.shape[1] for dy in dys]
    n = sum(widths) // NDEV
    tm = _tile(m, 512)
    steps = m // tm
    n_d = len(dys)

    def body(x_ref, *refs):
        dy_refs, o_ref, acc = refs[:n_d], refs[n_d], refs[n_d + 1]
        i = pl.program_id(0)

        @pl.when(i == 0)
        def _():
            acc[...] = jnp.zeros_like(acc)

        xb = x_ref[...].astype(bf16)
        off = 0
        for dy_ref, w in zip(dy_refs, widths):
            acc[:, off:off + w] += lax.dot_general(xb, dy_ref[...].astype(bf16), (((0,), (0,)), ((), ())), preferred_element_type=f32)
            off += w

        @pl.when(i == steps - 1)
        def _():
            for j in range(NDEV):
                o_ref[j] = acc[:, n * j:n * (j + 1)].astype(out_dtype)

    (out,), got = _call(
        body, (x, *dys), name=name, grid=(steps,),
        in_specs=[pl.BlockSpec((tm, k), lambda i: (i, 0))] + [pl.BlockSpec((tm, w), lambda i: (i, 0)) for w in widths],
        out_specs=[pl.BlockSpec((NDEV, k, n), lambda i: (0, 0, 0))], out_shape=[jax.ShapeDtypeStruct((NDEV, k, n), out_dtype)],
        scratch_shapes=[pltpu.VMEM((k, sum(widths)), f32)], sem=("arbitrary",), xchg=xchg)
    return out if xchg is None else (out, got)


def _exchange(arrs, *, scatter, name):
    n = len(arrs)

    def body(*refs):
        _exchange_copies(refs[:n], refs[n:2 * n], refs[2 * n:], scatter, True, True)

    return pl.pallas_call(
        body, name=name, in_specs=[HBM_SPEC] * n, out_specs=[HBM_SPEC] * n, out_shape=_exchange_out_shape(arrs, scatter),
        scratch_shapes=_exchange_sems(n),
    )(*arrs)


HBM_SPEC = pl.BlockSpec(memory_space=pltpu.HBM)


def _flags(scatter, n):
    return list(scatter) if isinstance(scatter, (list, tuple)) else [scatter] * n


def _exchange_out_shape(arrs, scatter):
    return [jax.ShapeDtypeStruct(a.shape if sc else (NDEV,) + a.shape, a.dtype) for a, sc in zip(arrs, _flags(scatter, len(arrs)))]


def _exchange_sems(n):
    return [pltpu.SemaphoreType.DMA(((NDEV - 1) * n,)), pltpu.SemaphoreType.DMA(((NDEV - 1) * n,)), pltpu.SemaphoreType.DMA((n,))]


def _exchange_copies(x_refs, o_refs, sems, scatter, start, wait):
    n = len(x_refs)
    flags = _flags(scatter, n)
    send_sems, recv_sems, local_sems = sems
    ix, iy, ic = lax.axis_index("x"), lax.axis_index("y"), lax.axis_index("c")
    me = 4 * ix + 2 * iy + ic
    local = [pltpu.make_async_copy(x.at[me] if sc else x, o.at[me], local_sems.at[a])
             for a, (x, o, sc) in enumerate(zip(x_refs, o_refs, flags))]

    def peer(k):
        return (1 - ix if k & 4 else ix, 1 - iy if k & 2 else iy, 1 - ic if k & 1 else ic)

    def index(p):
        return 4 * p[0] + 2 * p[1] + p[2]

    def copy(a, k, src, dst, to):
        sem = (k - 1) * n + a
        return pltpu.make_async_remote_copy(src_ref=src, dst_ref=dst, send_sem=send_sems.at[sem], recv_sem=recv_sems.at[sem],
                                            device_id=to, device_id_type=MESH)

    sends, arrivals, passed_on = [], [], []
    for a, (x, o, sc) in enumerate(zip(x_refs, o_refs, flags)):
        if sc:
            for k in range(1, NDEV):
                p = peer(k)
                sends.append(copy(a, k, x.at[index(p)], o.at[me], p))
                arrivals.append(copy(a, k, x.at[me], o.at[index(p)], p))
        else:
            sib = peer(1)
            sends.append(copy(a, 1, x, o.at[me], sib))
            arrivals.append(copy(a, 1, x, o.at[index(sib)], sib))
            for k in (2, 4, 6):
                p, q = peer(k), peer(k + 1)
                sends.append(copy(a, k, x, o.at[me], p))
                passed_on.append((copy(a, k, x, o.at[index(p)], p), copy(a, k + 1, o.at[index(p)], o.at[index(p)], sib)))
                arrivals.append(copy(a, k + 1, o.at[index(q)], o.at[index(q)], sib))
    if start:
        for cp in local + sends:
            cp.start()
    if wait:
        for landed, hand_over in passed_on:
            landed.wait_recv()
            hand_over.start()
        for cp in arrivals:
            cp.wait_recv()
        for cp in sends + [hand_over for _, hand_over in passed_on]:
            cp.wait_send()
        for cp in local:
            cp.wait()


def _call(body, args, *, name, grid, in_specs, out_specs, out_shape, scratch_shapes=(), sem, xchg=None):
    if xchg is None:
        return pl.pallas_call(body, name=name, grid=grid, in_specs=in_specs, out_specs=out_specs, out_shape=out_shape,
                              scratch_shapes=list(scratch_shapes), compiler_params=_params(*sem))(*args), None
    arrs, scatter = xchg
    n, ni, no, ns = len(arrs), len(in_specs), len(out_specs), len(scratch_shapes)

    def wrapped(*refs):
        ins, x_refs = refs[:ni], refs[ni:ni + n]
        outs, o_refs = refs[ni + n:ni + n + no], refs[ni + n + no:ni + 2 * n + no]
        scratch, sems = refs[ni + 2 * n + no:ni + 2 * n + no + ns], refs[ni + 2 * n + no + ns:]
        ids = [pl.program_id(a) for a in range(len(grid))]
        first = functools.reduce(jnp.logical_and, [p == 0 for p in ids])
        last = functools.reduce(jnp.logical_and, [p == g - 1 for p, g in zip(ids, grid)])

        @pl.when(first)
        def _():
            _exchange_copies(x_refs, o_refs, sems, scatter, True, False)

        body(*ins, *outs, *scratch)

        @pl.when(last)
        def _():
            _exchange_copies(x_refs, o_refs, sems, scatter, False, True)

    res = pl.pallas_call(
        wrapped, name=name, grid=grid, in_specs=list(in_specs) + [HBM_SPEC] * n, out_specs=list(out_specs) + [HBM_SPEC] * n,
        out_shape=list(out_shape) + _exchange_out_shape(arrs, scatter),
        scratch_shapes=list(scratch_shapes) + _exchange_sems(n),
        compiler_params=_params(*("arbitrary",) * len(grid)))(*args, *arrs)
    return res[:no], res[no:]


def _sum_parts(parts, *, name):
    r = parts.shape[1]

    def body(p_ref, o_ref):
        acc = p_ref[0]
        for j in range(1, NDEV):
            acc = acc + p_ref[j]
        o_ref[...] = acc

    return pl.pallas_call(body, name=name, out_shape=jax.ShapeDtypeStruct((r, LANES), f32), compiler_params=_params())(parts)


def _col_pieces(n, bounds):
    out = []
    for p, (a, b) in enumerate(bounds):
        for j in range(NDEV):
            lo, hi = max(a, n * j), min(b, n * (j + 1))
            if lo < hi:
                out.append((p, j, lo - a, lo - n * j, hi - lo))
    return out


def _unshard_cols(stacked, bounds, *, name):
    _, k, n = stacked.shape
    tk = _tile(k, 256)
    plan = _col_pieces(n, bounds)

    def body(x_ref, *o_refs):
        for p, j, po, so, w in plan:
            o_refs[p][:, po:po + w] = x_ref[j, :, so:so + w]

    return pl.pallas_call(
        body, name=name, grid=(k // tk,), in_specs=[pl.BlockSpec((NDEV, tk, n), lambda i: (0, i, 0))],
        out_specs=[pl.BlockSpec((tk, b - a), lambda i: (i, 0)) for a, b in bounds],
        out_shape=[jax.ShapeDtypeStruct((k, b - a), stacked.dtype) for a, b in bounds],
        compiler_params=_params("parallel"))(stacked)


def _shard_cols(pieces, *, out_dtype, name):
    k = pieces[0].shape[0]
    bounds, off = [], 0
    for p in pieces:
        bounds.append((off, off + p.shape[1]))
        off += p.shape[1]
    n = off // NDEV
    tk = _tile(k, 256)
    plan = _col_pieces(n, bounds)

    def body(*refs):
        o_ref = refs[-1]
        for p, j, po, so, w in plan:
            o_ref[j, :, so:so + w] = refs[p][:, po:po + w].astype(out_dtype)

    return pl.pallas_call(
        body, name=name, grid=(k // tk,), in_specs=[pl.BlockSpec((tk, b - a), lambda i: (i, 0)) for a, b in bounds],
        out_specs=pl.BlockSpec((NDEV, tk, n), lambda i: (0, i, 0)),
        out_shape=jax.ShapeDtypeStruct((NDEV, k, n), out_dtype),
        compiler_params=_params("parallel"))(*pieces)


def _pack(arrs):
    flat = jnp.concatenate([a.reshape(-1) for a in arrs])
    pad = (-flat.shape[0]) % (SUBLANES * LANES)
    return jnp.pad(flat, (0, pad)).reshape(-1, LANES)


def _unpack(flat, shapes):
    out, off = [], 0
    for s in shapes:
        n = math.prod(s)
        out.append(flat[..., off:off + n].reshape(flat.shape[:-1] + tuple(s)))
        off += n
    return out


def _adamw_math(w, gg, m, v):
    nm = ADAM_B1 * m + (1.0 - ADAM_B1) * gg
    nv = ADAM_B2 * v + (1.0 - ADAM_B2) * (gg * gg)
    m_hat = nm / (1.0 - ADAM_B1 ** ADAM_STEP)
    v_hat = nv / (1.0 - ADAM_B2 ** ADAM_STEP)
    return -ADAM_LR * (m_hat / (jnp.sqrt(v_hat) + ADAM_EPS) + ADAM_WD * w), nm, nv


def _row_block(r, c, copies):
    tr = r
    while copies * tr * c * 4 > EW_BLOCK_BYTES and tr % (4 * SUBLANES) == 0:
        tr //= 2
    return tr


def _adamw(w, g, m, v, *, name):
    r, c = w.shape
    tr = _row_block(r, c, 1)

    def body(w_ref, g_ref, m_ref, v_ref, d_ref, nm_ref, nv_ref):
        d_ref[...], nm_ref[...], nv_ref[...] = _adamw_math(w_ref[...], g_ref[...], m_ref[...], v_ref[...])

    spec = pl.BlockSpec((tr, c), lambda i: (i, 0))
    shp = jax.ShapeDtypeStruct((r, c), f32)
    return pl.pallas_call(
        body, name=name, grid=(r // tr,), in_specs=[spec] * 4, out_specs=[spec] * 3, out_shape=[shp] * 3,
        compiler_params=_params("parallel"),
    )(w, g, m, v)


def _adamw_many(ws, gs, ms, vs, *, name):
    n = len(ws)

    def body(*refs):
        outs = refs[4 * n:]
        for i in range(n):
            res = _adamw_math(refs[i][...], refs[n + i][...], refs[2 * n + i][...], refs[3 * n + i][...])
            for o_ref, r in zip(outs[3 * i:3 * i + 3], res):
                o_ref[...] = r

    res = pl.pallas_call(body, name=name, out_shape=[jax.ShapeDtypeStruct(w.shape, f32) for w in ws for _ in range(3)],
                         compiler_params=_params())(*ws, *gs, *ms, *vs)
    return [res[3 * i:3 * i + 3] for i in range(n)]


def _sum_adamw(parts, w, m, v, *, name, xchg=None):
    r, c = w.shape
    tr = _row_block(r, c, NDEV)

    def body(p_ref, w_ref, m_ref, v_ref, g_ref, d_ref, nm_ref, nv_ref):
        gg = p_ref[0].astype(f32)
        for j in range(1, NDEV):
            gg = gg + p_ref[j].astype(f32)
        g_ref[...] = gg
        d_ref[...], nm_ref[...], nv_ref[...] = _adamw_math(w_ref[...], gg, m_ref[...], v_ref[...])

    spec = pl.BlockSpec((tr, c), lambda i: (i, 0))
    shp = jax.ShapeDtypeStruct((r, c), f32)
    res, got = _call(
        body, (parts, w, m, v), name=name, grid=(r // tr,),
        in_specs=[pl.BlockSpec((NDEV, tr, c), lambda i: (0, i, 0))] + [spec] * 3,
        out_specs=[spec] * 4, out_shape=[shp] * 4, sem=("parallel",), xchg=xchg)
    return tuple(res) if xchg is None else tuple(res) + (got,)


def _cond_fwd(c_all, w, b, *, name):
    nb, n = c_all.shape[0], w.shape[1]

    def body(c_ref, w_ref, b_ref, o_ref):
        cc = c_ref[...]
        o_ref[...] = jnp.dot(cc * _sig(cc), w_ref[...], preferred_element_type=f32,
                             precision=lax.Precision.HIGHEST) + b_ref[...]

    return pl.pallas_call(body, name=name, out_shape=jax.ShapeDtypeStruct((nb, n), f32),
                          compiler_params=_params())(c_all, w, b)


def _cond_bwd(c_all, dmod, *, name):
    d, n = c_all.shape[1], dmod.shape[1]

    def body(c_ref, g_ref, o_ref):
        cc = c_ref[...]
        o_ref[...] = lax.dot_general(cc * _sig(cc), g_ref[...], (((0,), (0,)), ((), ())), preferred_element_type=f32,
                                     precision=lax.Precision.HIGHEST)

    return pl.pallas_call(body, name=name, out_shape=jax.ShapeDtypeStruct((d, n), f32),
                          compiler_params=_params())(c_all, dmod)


def _ssm_disc(lam_re, lam_im, log_dt):
    lr = jnp.minimum(lam_re, -1e-4)
    li = lam_im
    dt = jnp.exp(log_dt)
    mag = jnp.exp(lr * dt)
    ang = li * dt
    lbr, lbi = mag * jnp.cos(ang), mag * jnp.sin(ang)
    num_r, num_i = lbr - 1.0, lbi
    den = lr * lr + li * li
    return lbr, lbi, (num_r * lr + num_i * li) / den, (num_i * lr - num_r * li) / den


def _ssm_prep(lam_re, lam_im, log_dt, *, name):
    def body(a, b, c, o1, o2, o3, o4):
        o1[...], o2[...], o3[...], o4[...] = _ssm_disc(a[...], b[...], c[...])

    shp = jax.ShapeDtypeStruct(lam_re.shape, f32)
    return pl.pallas_call(body, name=name, out_shape=[shp] * 4, compiler_params=_params())(lam_re, lam_im, log_dt)


def _ssm_prep_bwd(lam_re, lam_im, log_dt, cts, *, name):
    def body(a, b, c, g1, g2, g3, g4, o1, o2, o3):
        _, vjp = jax.vjp(_ssm_disc, a[...], b[...], c[...])
        o1[...], o2[...], o3[...] = vjp((g1[...], g2[...], g3[...], g4[...]))

    shp = jax.ShapeDtypeStruct(lam_re.shape, f32)
    return pl.pallas_call(body, name=name, out_shape=[shp, shp, jax.ShapeDtypeStruct(log_dt.shape, f32)],
                          compiler_params=_params())(lam_re, lam_im, log_dt, *cts)


S5_ROWS = 512


def _step_major(x3):
    k, nt, c = x3.shape
    return jnp.swapaxes(x3, 0, 1).reshape(k * nt, c)


def _chunk_major(x2, nt):
    return jnp.swapaxes(x2.reshape(nt, SUBLANES, x2.shape[1]), 0, 1)


def _chain_carries(loc_r, loc_i, pr, pi_, forward):
    row = lax.broadcasted_iota(jnp.int32, loc_r.shape, 0)
    shift = 1 if forward else SUBLANES - 1
    order = range(1, SUBLANES) if forward else range(SUBLANES - 2, -1, -1)
    er, ei = loc_r, loc_i
    for k in order:
        sr, si = pltpu.roll(er, shift, 0), pltpu.roll(ei, shift, 0)
        er = jnp.where(row == k, loc_r + pr * sr - pi_ * si, er)
        ei = jnp.where(row == k, loc_i + pr * si + pi_ * sr, ei)
    edge = 0 if forward else SUBLANES - 1
    return (jnp.where(row == edge, 0.0, pltpu.roll(er, shift, 0)), jnp.where(row == edge, 0.0, pltpu.roll(ei, shift, 0)))


def _chunk_power(ar, ai, chunk_len):
    pr, pi_ = ar, ai
    for _ in range(int(math.log2(chunk_len))):
        pr, pi_ = pr * pr - pi_ * pi_, 2.0 * pr * pi_
    return pr, pi_


def _ssm_mats(bre_ref, bim_ref, cre_ref, cim_ref, cfr_ref, cfi_ref, bbar_s, cmat_s, nq):
    for q in range(nq):
        cr, ci, br, bi = cfr_ref[q], cfi_ref[q], bre_ref[q], bim_ref[q]
        bbar_s[q, :, 0:QS] = (cr * br - ci * bi).astype(bf16)
        bbar_s[q, :, QS:2 * QS] = (cr * bi + ci * br).astype(bf16)
        cmat_s[q, 0:QS, :] = cre_ref[q].astype(bf16)
        cmat_s[q, QS:2 * QS, :] = (-cim_ref[q]).astype(bf16)


def _ssm_fwd(u, ar, ai, bre, bim, cre, cim, cfr, cfi, dvec, *, name, xchg=None):
    s, sw = u.shape
    nq = sw // QW
    st = nq * 2 * QS
    tb = _tile(s, S5_ROWS)
    nb, nt, chunk_len = s // tb, tb // SUBLANES, s // SUBLANES
    assert chunk_len & (chunk_len - 1) == 0 and nt % 16 == 0

    def body(u_ref, ar_ref, ai_ref, bre_ref, bim_ref, cre_ref, cim_ref, cfr_ref, cfi_ref, d_ref,
             h_out, yraw_out, y_out, buf, hc, bbar_s, cmat_s):
        ph, i = pl.program_id(0), pl.program_id(1)

        @pl.when(i == 0)
        def _():
            _ssm_mats(bre_ref, bim_ref, cre_ref, cim_ref, cfr_ref, cfi_ref, bbar_s, cmat_s, nq)

        @pl.when((ph == 0) & (i == 0))
        def _():
            hc[...] = jnp.zeros_like(hc)

        @pl.when((ph == 1) & (i == 0))
        def _():
            for q in range(nq):
                o = q * 2 * QS
                pr, pi_ = _chunk_power(ar_ref[q], ai_ref[q], chunk_len)
                sr, si = _chain_carries(hc[:, o:o + QS], hc[:, o + QS:o + 2 * QS], pr, pi_, True)
                hc[:, o:o + QS] = sr
                hc[:, o + QS:o + 2 * QS] = si

        uu = u_ref[...]
        up = _step_major(uu).astype(bf16)
        for q in range(nq):
            o = q * 2 * QS
            buf[:, o:o + 2 * QS] = jnp.dot(up[:, q * QW:(q + 1) * QW], bbar_s[q], preferred_element_type=f32)

        for q in range(nq):
            o = q * 2 * QS
            a_r = jnp.broadcast_to(ar_ref[q], (SUBLANES, QS))
            a_i = jnp.broadcast_to(ai_ref[q], (SUBLANES, QS))

            def step(t, carry, o=o, a_r=a_r, a_i=a_i):
                hr, hi = carry
                r0 = pl.multiple_of(t * SUBLANES, SUBLANES)
                nr = a_r * hr - a_i * hi + buf[pl.ds(r0, SUBLANES), o:o + QS]
                ni = a_r * hi + a_i * hr + buf[pl.ds(r0, SUBLANES), o + QS:o + 2 * QS]
                buf[pl.ds(r0, SUBLANES), o:o + QS] = nr
                buf[pl.ds(r0, SUBLANES), o + QS:o + 2 * QS] = ni
                return nr, ni

            hr, hi = lax.fori_loop(0, nt, step, (hc[:, o:o + QS], hc[:, o + QS:o + 2 * QS]))
            hc[:, o:o + QS] = hr
            hc[:, o + QS:o + 2 * QS] = hi

        @pl.when(ph == 1)
        def _():
            for q in range(nq):
                o = q * 2 * QS
                cs = slice(q * QW, (q + 1) * QW)
                hq = buf[:, o:o + 2 * QS].astype(bf16)
                h_out[:, o:o + 2 * QS] = hq
                yq = _chunk_major(jnp.dot(hq, cmat_s[q], preferred_element_type=f32), nt) + d_ref[:, cs] * uu[:, :, cs]
                yraw_out[:, :, cs] = yq
                y_out[:, :, cs] = _gelu(yq).astype(bf16)

    blk = lambda ph, i: (0, i, 0)
    oblk = lambda ph, i: (0, i * ph, 0)
    act = lambda dt: jax.ShapeDtypeStruct((SUBLANES, chunk_len, sw), dt)
    (h_p, yraw3, y3), got = _call(
        body, (u.reshape(SUBLANES, chunk_len, sw), ar, ai, bre, bim, cre, cim, cfr, cfi, dvec), name=name, grid=(2, nb),
        in_specs=[pl.BlockSpec((SUBLANES, nt, sw), blk), _full(ar.shape), _full(ai.shape), _full(bre.shape), _full(bim.shape),
                  _full(cre.shape), _full(cim.shape), _full(cfr.shape), _full(cfi.shape), _full(dvec.shape)],
        out_specs=[pl.BlockSpec((tb, st), lambda ph, i: (i * ph, 0)), pl.BlockSpec((SUBLANES, nt, sw), oblk),
                   pl.BlockSpec((SUBLANES, nt, sw), oblk)],
        out_shape=[jax.ShapeDtypeStruct((s, st), bf16), act(f32), act(bf16)],
        scratch_shapes=[pltpu.VMEM((tb, st), f32), pltpu.VMEM((SUBLANES, st), f32),
                        pltpu.VMEM((nq, QW, 2 * QS), bf16), pltpu.VMEM((nq, 2 * QS, QW), bf16)],
        sem=("arbitrary", "arbitrary"), xchg=xchg)
    return h_p, yraw3, y3.reshape(s, sw), got


def _ssm_bwd(dy, yraw3, u, h_p, ar, ai, bre, bim, cre, cim, cfr, cfi, dvec, *, name, xchg=None):
    s, sw = u.shape
    nq = sw // QW
    st = nq * 2 * QS
    tb = _tile(s, S5_ROWS)
    nb, nt, chunk_len = s // tb, tb // SUBLANES, s // SUBLANES

    def body(dy_ref, yraw_ref, u_ref, h_ref, ar_ref, ai_ref, bre_ref, bim_ref, cre_ref, cim_ref, cfr_ref, cfi_ref, d_ref,
             du_out, dbre_out, dbim_out, dcre_out, dcim_out, dcfr_out, dcfi_out, dlbr_out, dlbi_out, dd_out, dbu_out,
             buf, hf, rc, acc, dbbar, dcmat, bbar_s, cmat_s):
        ph, i = pl.program_id(0), pl.program_id(1)

        @pl.when(i == 0)
        def _():
            _ssm_mats(bre_ref, bim_ref, cre_ref, cim_ref, cfr_ref, cfi_ref, bbar_s, cmat_s, nq)

        @pl.when((ph == 0) & (i == 0))
        def _():
            rc[...] = jnp.zeros_like(rc)

        @pl.when((ph == 1) & (i == 0))
        def _():
            for q in range(nq):
                o = q * 2 * QS
                pr, pi_ = _chunk_power(ar_ref[q], ai_ref[q], chunk_len)
                sr, si = _chain_carries(rc[:, o:o + QS], rc[:, o + QS:o + 2 * QS], pr, -pi_, False)
                rc[:, o:o + QS] = sr
                rc[:, o + QS:o + 2 * QS] = si
            acc[...] = jnp.zeros_like(acc)
            dbbar[...] = jnp.zeros_like(dbbar)
            dcmat[...] = jnp.zeros_like(dcmat)
            dd_out[...] = jnp.zeros_like(dd_out)
            dbu_out[...] = jnp.zeros_like(dbu_out)

        dyraw = dy_ref[...].astype(f32) * _gelu_grad(yraw_ref[...])
        dyp = _step_major(dyraw).astype(bf16)
        for q in range(nq):
            o = q * 2 * QS
            buf[:, o:o + 2 * QS] = lax.dot_general(dyp[:, q * QW:(q + 1) * QW], cmat_s[q], (((1,), (1,)), ((), ())),
                                                   preferred_element_type=f32)

        def recur(with_grad):
            for q in range(nq):
                o = q * 2 * QS
                a_r = jnp.broadcast_to(ar_ref[q], (SUBLANES, QS))
                a_i = jnp.broadcast_to(ai_ref[q], (SUBLANES, QS))

                def step(j, carry, o=o, a_r=a_r, a_i=a_i):
                    r0 = pl.multiple_of((nt - 1 - j) * SUBLANES, SUBLANES)
                    if with_grad:
                        rr, ri, gr, gi = carry
                        hr = hf[pl.ds(r0, SUBLANES), o:o + QS]
                        hi = hf[pl.ds(r0, SUBLANES), o + QS:o + 2 * QS]
                        gr = gr + hr * rr + hi * ri
                        gi = gi + hr * ri - hi * rr
                    else:
                        rr, ri = carry
                    nr = buf[pl.ds(r0, SUBLANES), o:o + QS] + a_r * rr + a_i * ri
                    ni = buf[pl.ds(r0, SUBLANES), o + QS:o + 2 * QS] + a_r * ri - a_i * rr
                    buf[pl.ds(r0, SUBLANES), o:o + QS] = nr
                    buf[pl.ds(r0, SUBLANES), o + QS:o + 2 * QS] = ni
                    return (nr, ni, gr, gi) if with_grad else (nr, ni)

                init = (rc[:, o:o + QS], rc[:, o + QS:o + 2 * QS])
                if with_grad:
                    init = init + (acc[:, o:o + QS], acc[:, o + QS:o + 2 * QS])
                res = lax.fori_loop(0, nt, step, init)
                rc[:, o:o + QS] = res[0]
                rc[:, o + QS:o + 2 * QS] = res[1]
                if with_grad:
                    acc[:, o:o + QS] = res[2]
                    acc[:, o + QS:o + 2 * QS] = res[3]

        @pl.when(ph == 0)
        def _():
            recur(False)

        @pl.when(ph == 1)
        def _():
            hf[...] = h_ref[...].astype(f32)
            recur(True)
            uu = u_ref[...]
            up = _step_major(uu).astype(bf16)
            dd_out[...] += _colsum((dyraw * uu).reshape(tb, sw))
            for q in range(nq):
                o = q * 2 * QS
                cs = slice(q * QW, (q + 1) * QW)
                lam = buf[:, o:o + 2 * QS].astype(bf16)
                duq = _chunk_major(lax.dot_general(lam, bbar_s[q], (((1,), (1,)), ((), ())), preferred_element_type=f32), nt) \
                    + d_ref[:, cs] * dyraw[:, :, cs]
                du_out[:, :, cs] = duq.astype(bf16)
                dbu_out[:, cs] += _colsum(duq.reshape(tb, QW))
                dbbar[q] += lax.dot_general(up[:, cs], lam, (((0,), (0,)), ((), ())), preferred_element_type=f32)
                dcmat[q] += lax.dot_general(dyp[:, cs], h_ref[:, o:o + 2 * QS], (((0,), (0,)), ((), ())),
                                            preferred_element_type=f32)

        @pl.when((ph == 1) & (i == nb - 1))
        def _():
            for q in range(nq):
                o = q * 2 * QS
                cr, ci, br, bi = cfr_ref[q], cfi_ref[q], bre_ref[q], bim_ref[q]
                gr, gi = dbbar[q, :, 0:QS], dbbar[q, :, QS:2 * QS]
                dbre_out[q] = cr * gr + ci * gi
                dbim_out[q] = cr * gi - ci * gr
                dcfr_out[q] = _colsum(gr * br + gi * bi)
                dcfi_out[q] = _colsum(gi * br - gr * bi)
                dcre_out[q] = dcmat[q, :, 0:QS].T
                dcim_out[q] = -dcmat[q, :, QS:2 * QS].T
                dlbr_out[q] = _colsum(acc[:, o:o + QS])
                dlbi_out[q] = _colsum(acc[:, o + QS:o + 2 * QS])

    blk = lambda ph, i: (0, nb - 1 - i, 0)
    oblk = lambda ph, i: (0, (nb - 1 - i) * ph + (nb - 1) * (1 - ph), 0)
    pshapes = [ar.shape, ai.shape, bre.shape, bim.shape, cre.shape, cim.shape, cfr.shape, cfi.shape, dvec.shape]
    oshapes = [bre.shape, bim.shape, cre.shape, cim.shape, cfr.shape, cfi.shape, ar.shape, ai.shape, dvec.shape, dvec.shape]
    act = pl.BlockSpec((SUBLANES, nt, sw), blk)
    view = lambda a: a.reshape(SUBLANES, chunk_len, sw)
    res, got = _call(
        body, (view(dy), yraw3, view(u), h_p, ar, ai, bre, bim, cre, cim, cfr, cfi, dvec), name=name, grid=(2, nb),
        in_specs=[act, act, act, pl.BlockSpec((tb, st), lambda ph, i: (nb - 1 - i, 0))] + [_full(p) for p in pshapes],
        out_specs=[pl.BlockSpec((SUBLANES, nt, sw), oblk)] + [_full(p) for p in oshapes],
        out_shape=[jax.ShapeDtypeStruct((SUBLANES, chunk_len, sw), bf16)] + [jax.ShapeDtypeStruct(p, f32) for p in oshapes],
        scratch_shapes=[pltpu.VMEM((tb, st), f32), pltpu.VMEM((tb, st), f32),
                        pltpu.VMEM((SUBLANES, st), f32), pltpu.VMEM((SUBLANES, st), f32),
                        pltpu.VMEM((nq, QW, 2 * QS), f32), pltpu.VMEM((nq, QW, 2 * QS), f32),
                        pltpu.VMEM((nq, QW, 2 * QS), bf16), pltpu.VMEM((nq, 2 * QS, QW), bf16)],
        sem=("arbitrary", "arbitrary"), xchg=xchg)
    return (res[0].reshape(s, sw),) + tuple(res[1:]) + (got,)


def _lnmod(x, sc, sh, *, name):
    s, d = x.shape
    tb = _tile(s, 512)

    def body(x_ref, sc_ref, sh_ref, o_ref):
        xh, _ = _ln(x_ref[...])
        o_ref[...] = (xh * (1.0 + sc_ref[...]) + sh_ref[...]).astype(bf16)

    blk = pl.BlockSpec((tb, d), lambda i: (i, 0))
    vec = pl.BlockSpec((1, d), _row)
    return pl.pallas_call(body, name=name, grid=(s // tb,), in_specs=[blk, vec, vec], out_specs=blk,
                          out_shape=jax.ShapeDtypeStruct((s, d), bf16), compiler_params=_params("parallel"))(x, sc, sh)


ROWS = 32


def _row_chunks(n_rows, rows, fn, init, start=0):
    return lax.fori_loop(start, n_rows // rows, lambda c, carry: fn(pl.multiple_of(c * rows, rows), carry), init)


def _rows_from(win, o, rows):
    if o % SUBLANES == 0:
        return win[o:o + rows]
    n = win.shape[0]
    return pltpu.roll(win, (n - o) % n, 0)[0:rows]


def _window_before(ref, halo, r0, rows, first, cols):
    if first:
        return jnp.concatenate([halo, ref[pl.ds(0, rows), cols]], axis=0)
    return ref[pl.ds(pl.multiple_of(r0 - SUBLANES, SUBLANES), rows + SUBLANES), cols]


def _taps3(win, w, off, rows):
    return _rows_from(win, off, rows) * w[0] + _rows_from(win, off + 1, rows) * w[1] + _rows_from(win, off + 2, rows) * w[2]


def _fold8(x):
    acc = x[0:SUBLANES]
    for r in range(1, x.shape[0] // SUBLANES):
        acc = acc + x[r * SUBLANES:(r + 1) * SUBLANES]
    return acc


def _conv_halo_specs(tb, cw, halo, s):
    per = tb // halo
    prev = pl.BlockSpec((halo, cw), lambda i: (jnp.maximum(i * per - 1, 0), 0))
    nxt = pl.BlockSpec((halo, cw), lambda i: (jnp.minimum((i + 1) * per, s // halo - 1), 0))
    return prev, nxt


WIDE_ROWS = 16


def _shift_groups(lo, hi):
    return [(b, [o for o in range(lo, hi + 1) if o % SUBLANES == b]) for b in range(SUBLANES)]


def _shifted(win, b):
    return win if b == 0 else _rows_from(win, b, win.shape[0] - SUBLANES)


def _conv31(win, w_ref, cols, rows, lo, hi, tap_of):
    acc = None
    for b, offs in _shift_groups(lo, hi):
        if offs:
            wb = _shifted(win, b)
            for o in offs:
                term = wb[o - b:o - b + rows] * w_ref[pl.ds(tap_of(o), 1), cols]
                acc = term if acc is None else acc + term
    return acc


def _gate_into(ext, a_ref, g_ref, ah_ref, gh_ref, tb, i):
    ext[pl.ds(0, CONV_HALO), :] = jnp.where(i > 0, ah_ref[...] * _sig(gh_ref[...]), 0.0)

    def chunk(r0, carry):
        ext[pl.ds(pl.multiple_of(r0 + CONV_HALO, SUBLANES), WIDE_ROWS), :] = \
            a_ref[pl.ds(r0, WIDE_ROWS), :] * _sig(g_ref[pl.ds(r0, WIDE_ROWS), :])
        return carry

    _row_chunks(tb, WIDE_ROWS, chunk, 0)


def _causal_conv_into(v2buf, ext, w_ref, b_ref, tb, cw):
    for ct in range(cw // LANES):
        cols = slice(ct * LANES, (ct + 1) * LANES)

        def chunk(r0, carry, cols=cols):
            win = ext[pl.ds(r0, ROWS + CONV_HALO), cols]
            v2buf[pl.ds(r0, ROWS), cols] = _conv31(win, w_ref, cols, ROWS, 2, CONV_K + 1, lambda o: o - 2) + b_ref[:, cols]
            return carry

        _row_chunks(tb, ROWS, chunk, 0)


def _silu_grad(x):
    sg = _sig(x)
    return sg * (1.0 + x * (1.0 - sg))


def _conv_fwd(cva, cvg, w, b, lng, lnb, *, name, xchg=None):
    s, cw = cva.shape
    tb = _tile(s, 256)
    prev, _ = _conv_halo_specs(tb, cw, CONV_HALO, s)

    def body(a_ref, g_ref, ah_ref, gh_ref, w_ref, b_ref, lng_ref, lnb_ref, o_ref, v2_ref, ext):
        _gate_into(ext, a_ref, g_ref, ah_ref, gh_ref, tb, pl.program_id(0))
        _causal_conv_into(v2_ref, ext, w_ref, b_ref, tb, cw)
        xh, _ = _ln(v2_ref[...])
        v3 = xh * lng_ref[...] + lnb_ref[...]
        o_ref[...] = (v3 * _sig(v3)).astype(bf16)

    blk = pl.BlockSpec((tb, cw), lambda i: (i, 0))
    vec = pl.BlockSpec((1, cw), _row)
    (v4, v2), got = _call(
        body, (cva, cvg, cva, cvg, w, b, lng, lnb), name=name, grid=(s // tb,),
        in_specs=[blk, blk, prev, prev, _full(w.shape), vec, vec, vec], out_specs=[blk, blk],
        out_shape=[jax.ShapeDtypeStruct((s, cw), bf16), jax.ShapeDtypeStruct((s, cw), f32)],
        scratch_shapes=[pltpu.VMEM((tb + CONV_HALO, cw), f32)], sem=("parallel",), xchg=xchg)
    return v4, v2, got


def _conv_bwd_ln(dv4, v2, lng, lnb, *, name):
    s, cw = v2.shape
    tb = _tile(s, 256)

    def body(d_ref, v2_ref, lng_ref, lnb_ref, o_ref, dg_ref, db_ref):
        @pl.when(pl.program_id(0) == 0)
        def _():
            dg_ref[...] = jnp.zeros_like(dg_ref)
            db_ref[...] = jnp.zeros_like(db_ref)

        xh, rstd = _ln(v2_ref[...])
        v3 = xh * lng_ref[...] + lnb_ref[...]
        dv3 = d_ref[...].astype(f32) * _silu_grad(v3)
        dg_ref[...] += _colsum(dv3 * xh)
        db_ref[...] += _colsum(dv3)
        o_ref[...] = _ln_bwd(dv3 * lng_ref[...], xh, rstd)

    blk = pl.BlockSpec((tb, cw), lambda i: (i, 0))
    vec = pl.BlockSpec((1, cw), _row)
    vshape = jax.ShapeDtypeStruct((1, cw), f32)
    return pl.pallas_call(
        body, name=name, grid=(s // tb,), in_specs=[blk, blk, vec, vec],
        out_specs=[blk, vec, vec], out_shape=[jax.ShapeDtypeStruct((s, cw), f32), vshape, vshape],
        compiler_params=_params("arbitrary"))(dv4, v2, lng, lnb)


def _conv_bwd_taps(dv2, cva, cvg, w, *, name, xchg=None):
    s, cw = cva.shape
    tb = _tile(s, 256)
    nb = s // tb
    prev, nxt = _conv_halo_specs(tb, cw, CONV_HALO, s)

    def body(d_ref, dn_ref, a_ref, g_ref, ah_ref, gh_ref, w_ref, da_ref, dg_ref, dw_ref, db_ref, sa_ref, sg_ref,
             ext, dext, dvbuf, tap_sums):
        i = pl.program_id(0)

        @pl.when(i == 0)
        def _():
            for r in (dw_ref, db_ref, sa_ref, sg_ref):
                r[...] = jnp.zeros_like(r)

        _gate_into(ext, a_ref, g_ref, ah_ref, gh_ref, tb, i)
        dext[pl.ds(tb, CONV_HALO), :] = jnp.where(i < nb - 1, dn_ref[...], 0.0)

        def copy(r0, carry):
            dext[pl.ds(r0, WIDE_ROWS), :] = d_ref[pl.ds(r0, WIDE_ROWS), :]
            return carry

        _row_chunks(tb, WIDE_ROWS, copy, 0)

        for ct in range(cw // LANES):
            cols = slice(ct * LANES, (ct + 1) * LANES)

            tap_sums[...] = jnp.zeros_like(tap_sums)

            def back(r0, carry, cols=cols):
                win = dext[pl.ds(r0, ROWS + CONV_HALO), cols]
                dvbuf[pl.ds(r0, ROWS), cols] = _conv31(win, w_ref, cols, ROWS, 0, CONV_K - 1, lambda o: CONV_K - 1 - o)
                win = ext[pl.ds(r0, ROWS + CONV_HALO), cols]
                dd = d_ref[pl.ds(r0, ROWS), cols]
                for b, offs in _shift_groups(2, CONV_K + 1):
                    wb = _shifted(win, b)
                    for o in offs:
                        tap_sums[o - 2] += _fold8(dd * wb[o - b:o - b + ROWS])
                return carry

            _row_chunks(tb, ROWS, back, 0)
            for k in range(CONV_K):
                dw_ref[pl.ds(k, 1), cols] += _colsum(tap_sums[k])

        def gate_back(r0, sums):
            rows = pl.ds(r0, WIDE_ROWS)
            aa, sg, dv = a_ref[rows, :], _sig(g_ref[rows, :]), dvbuf[rows, :]
            da = dv * sg
            dgate = dv * aa * sg * (1.0 - sg)
            da_ref[rows, :] = da.astype(bf16)
            dg_ref[rows, :] = dgate.astype(bf16)
            return sums[0] + _fold8(da), sums[1] + _fold8(dgate), sums[2] + _fold8(d_ref[rows, :])

        zero = jnp.zeros((SUBLANES, cw), f32)
        sums = _row_chunks(tb, WIDE_ROWS, gate_back, (zero, zero, zero))
        sa_ref[...] += _colsum(sums[0])
        sg_ref[...] += _colsum(sums[1])
        db_ref[...] += _colsum(sums[2])

    blk = pl.BlockSpec((tb, cw), lambda i: (i, 0))
    vec = pl.BlockSpec((1, cw), _row)
    vshape = jax.ShapeDtypeStruct((1, cw), f32)
    act = jax.ShapeDtypeStruct((s, cw), bf16)
    res, got = _call(
        body, (dv2, dv2, cva, cvg, cva, cvg, w), name=name, grid=(nb,), in_specs=[blk, nxt, blk, blk, prev, prev, _full(w.shape)],
        out_specs=[blk, blk, _full(w.shape), vec, vec, vec],
        out_shape=[act, act, jax.ShapeDtypeStruct(w.shape, f32), vshape, vshape, vshape],
        scratch_shapes=[pltpu.VMEM((tb + CONV_HALO, cw), f32), pltpu.VMEM((tb + CONV_HALO, cw), f32), pltpu.VMEM((tb, cw), f32),
                        pltpu.VMEM((CONV_HALO, SUBLANES, LANES), f32)],
        sem=("arbitrary",), xchg=xchg)
    return tuple(res) + (got,)


def _glu_merge(ya, yb, ycv, gs, gc, *, name):
    s, d = ya.shape
    tb = _tile(s, 512)

    def body(ya_ref, yb_ref, ycv_ref, gs_ref, gc_ref, o_ref):
        ld = lambda r: r[...].astype(f32)
        z = ld(ya_ref) * _sig(ld(yb_ref))
        o_ref[...] = (_sig(ld(gs_ref)) * z + _sig(ld(gc_ref)) * ld(ycv_ref)).astype(bf16)

    blk = pl.BlockSpec((tb, d), lambda i: (i, 0))
    return pl.pallas_call(body, name=name, grid=(s // tb,), in_specs=[blk] * 5, out_specs=blk,
                          out_shape=jax.ShapeDtypeStruct((s, d), bf16), compiler_params=_params("parallel"))(ya, yb, ycv, gs, gc)


def _glu_merge_bwd(dm, ya, yb, ycv, gs, gc, *, name):
    s, d = ya.shape
    tb = _tile(s, 512)

    def body(dm_ref, ya_ref, yb_ref, ycv_ref, gs_ref, gc_ref, dya_ref, dyb_ref, dycv_ref, dgs_ref, dgc_ref, sgs_ref, sgc_ref):
        @pl.when(pl.program_id(0) == 0)
        def _():
            sgs_ref[...] = jnp.zeros_like(sgs_ref)
            sgc_ref[...] = jnp.zeros_like(sgc_ref)

        ld = lambda r: r[...].astype(f32)
        dmv, yav = ld(dm_ref), ld(ya_ref)
        sb, ss, scv = _sig(ld(yb_ref)), _sig(ld(gs_ref)), _sig(ld(gc_ref))
        z = yav * sb
        dz = dmv * ss
        dgs = dmv * z * ss * (1.0 - ss)
        dgc = dmv * ld(ycv_ref) * scv * (1.0 - scv)
        dya_ref[...] = (dz * sb).astype(bf16)
        dyb_ref[...] = (dz * yav * sb * (1.0 - sb)).astype(bf16)
        dycv_ref[...] = (dmv * scv).astype(bf16)
        dgs_ref[...] = dgs.astype(bf16)
        dgc_ref[...] = dgc.astype(bf16)
        sgs_ref[...] += _colsum(dgs)
        sgc_ref[...] += _colsum(dgc)

    blk = pl.BlockSpec((tb, d), lambda i: (i, 0))
    vec = pl.BlockSpec((1, d), _row)
    act = jax.ShapeDtypeStruct((s, d), bf16)
    vshape = jax.ShapeDtypeStruct((1, d), f32)
    return pl.pallas_call(body, name=name, grid=(s // tb,), in_specs=[blk] * 6, out_specs=[blk] * 5 + [vec, vec],
                          out_shape=[act] * 5 + [vshape, vshape], compiler_params=_params("arbitrary"))(dm, ya, yb, ycv, gs, gc)


def _resid_ln_mod(x, o, g, lng, lnb, sc, sh, alpha, *, name):
    s, d = x.shape
    tb = _tile(s, 512)

    def body(x_ref, o_ref, g_ref, lng_ref, lnb_ref, sc_ref, sh_ref, x1_ref, h_ref):
        xh, _ = _ln(alpha * x_ref[...] + g_ref[...] * o_ref[...].astype(f32))
        x1 = xh * lng_ref[...] + lnb_ref[...]
        x1_ref[...] = x1
        xh1, _ = _ln(x1)
        h_ref[...] = (xh1 * (1.0 + sc_ref[...]) + sh_ref[...]).astype(bf16)

    blk = pl.BlockSpec((tb, d), lambda i: (i, 0))
    vec = pl.BlockSpec((1, d), _row)
    return pl.pallas_call(body, name=name, grid=(s // tb,), in_specs=[blk, blk] + [vec] * 5, out_specs=[blk, blk],
                          out_shape=[jax.ShapeDtypeStruct((s, d), f32), jax.ShapeDtypeStruct((s, d), bf16)],
                          compiler_params=_params("parallel"))(x, o, g, lng, lnb, sc, sh)


def _resid_ln_loss(x1, f, w_dn, g, lng, lnb, tgt, alpha, *, name):
    s, d = x1.shape
    tb = _tile(s, 512)

    def body(x1_ref, f_ref, w_ref, g_ref, lng_ref, lnb_ref, t_ref, dr_ref, dy_ref, loss_ref, dlg_ref, dlb_ref, dg_ref):
        @pl.when(pl.program_id(0) == 0)
        def _():
            for r in (loss_ref, dlg_ref, dlb_ref, dg_ref):
                r[...] = jnp.zeros_like(r)

        yv = jnp.dot(f_ref[...], w_ref[...], preferred_element_type=f32)
        xh, rstd = _ln(alpha * x1_ref[...] + g_ref[...] * yv)
        err = xh * lng_ref[...] + lnb_ref[...] - t_ref[...]
        loss_ref[...] += 0.5 * jnp.sum(jnp.sum(err * err, axis=-1, keepdims=True) / d, axis=0, keepdims=True)
        dx2 = err / d
        dlg_ref[...] += _colsum(dx2 * xh)
        dlb_ref[...] += _colsum(dx2)
        dr = _ln_bwd(dx2 * lng_ref[...], xh, rstd)
        dg_ref[...] += _colsum(dr * yv)
        dr_ref[...] = dr
        dy_ref[...] = (g_ref[...] * dr).astype(bf16)

    blk = pl.BlockSpec((tb, d), lambda i: (i, 0))
    vec = pl.BlockSpec((1, d), _row)
    vshape = jax.ShapeDtypeStruct((1, d), f32)
    return pl.pallas_call(
        body, name=name, grid=(s // tb,),
        in_specs=[blk, pl.BlockSpec((tb, f.shape[1]), lambda i: (i, 0)), _full(w_dn.shape), vec, vec, vec, blk],
        out_specs=[blk, blk, pl.BlockSpec((1, 1), _row), vec, vec, vec],
        out_shape=[jax.ShapeDtypeStruct((s, d), f32), jax.ShapeDtypeStruct((s, d), bf16),
                   jax.ShapeDtypeStruct((1, 1), f32), vshape, vshape, vshape],
        compiler_params=_params("arbitrary"))(x1, f, w_dn, g, lng, lnb, tgt)


def _mid_bwd(dh2, x1, dr2, x, o, g, sc, lng, alpha, *, name):
    s, d = x.shape
    tb = _tile(s, 512)

    def body(dh_ref, x1_ref, dr2_ref, x_ref, o_ref, g_ref, sc_ref, lng_ref,
             dr1_ref, do_ref, dsc_ref, dsh_ref, dlg_ref, dlb_ref, dg_ref):
        @pl.when(pl.program_id(0) == 0)
        def _():
            for r in (dsc_ref, dsh_ref, dlg_ref, dlb_ref, dg_ref):
                r[...] = jnp.zeros_like(r)

        dh = dh_ref[...].astype(f32)
        xh1, rstd1 = _ln(x1_ref[...])
        dsc_ref[...] += _colsum(dh * xh1)
        dsh_ref[...] += _colsum(dh)
        dx1 = alpha * dr2_ref[...] + _ln_bwd(dh * (1.0 + sc_ref[...]), xh1, rstd1)
        ov = o_ref[...].astype(f32)
        xhr, rstdr = _ln(alpha * x_ref[...] + g_ref[...] * ov)
        dlg_ref[...] += _colsum(dx1 * xhr)
        dlb_ref[...] += _colsum(dx1)
        dr1 = _ln_bwd(dx1 * lng_ref[...], xhr, rstdr)
        dg_ref[...] += _colsum(dr1 * ov)
        dr1_ref[...] = dr1
        do_ref[...] = (g_ref[...] * dr1).astype(bf16)

    blk = pl.BlockSpec((tb, d), lambda i: (i, 0))
    vec = pl.BlockSpec((1, d), _row)
    vshape = jax.ShapeDtypeStruct((1, d), f32)
    return pl.pallas_call(
        body, name=name, grid=(s // tb,), in_specs=[blk] * 5 + [vec] * 3, out_specs=[blk, blk] + [vec] * 5,
        out_shape=[jax.ShapeDtypeStruct((s, d), f32), jax.ShapeDtypeStruct((s, d), bf16)] + [vshape] * 5,
        compiler_params=_params("arbitrary"))(dh2, x1, dr2, x, o, g, sc, lng)


def _final_bwd(dh1, x, dr1, sc, alpha, *, name, xchg=None):
    s, d = x.shape
    tb = _tile(s, 512)

    def body(dh_ref, x_ref, dr1_ref, sc_ref, dx_ref, dsc_ref, dsh_ref):
        @pl.when(pl.program_id(0) == 0)
        def _():
            dsc_ref[...] = jnp.zeros_like(dsc_ref)
            dsh_ref[...] = jnp.zeros_like(dsh_ref)

        dh = dh_ref[...].astype(f32)
        xh, rstd = _ln(x_ref[...])
        dsc_ref[...] += _colsum(dh * xh)
        dsh_ref[...] += _colsum(dh)
        dx_ref[...] = alpha * dr1_ref[...] + _ln_bwd(dh * (1.0 + sc_ref[...]), xh, rstd)

    blk = pl.BlockSpec((tb, d), lambda i: (i, 0))
    vec = pl.BlockSpec((1, d), _row)
    vshape = jax.ShapeDtypeStruct((1, d), f32)
    res, got = _call(body, (dh1, x, dr1, sc), name=name, grid=(s // tb,), in_specs=[blk, blk, blk, vec], out_specs=[blk, vec, vec],
                     out_shape=[jax.ShapeDtypeStruct((s, d), f32), vshape, vshape], sem=("arbitrary",), xchg=xchg)
    return tuple(res) + (got,)


TALL_ROWS = 64


def _ffn_col_tile(fh):
    return fh // 2 if (fh // 2) % LANES == 0 else fh


def _ffn_specs(s, fh, tb, tc):
    per = tb // FFN_HALO
    blk = pl.BlockSpec((tb, tc), lambda j, i: (i, j))
    prev = pl.BlockSpec((FFN_HALO, tc), lambda j, i: (jnp.maximum(i * per - 1, 0), j))
    nxt = pl.BlockSpec((FFN_HALO, tc), lambda j, i: (jnp.minimum((i + 1) * per, s // FFN_HALO - 1), j))
    taps = pl.BlockSpec((FFN_HALO, tc), lambda j, i: (0, j))
    vec = pl.BlockSpec((1, tc), lambda j, i: (0, j))
    return blk, prev, nxt, taps, vec


def _ffn_mid(upa, upv, wa, wv, ba, bv, *, name, xchg=None):
    s, fh = upa.shape
    tb, tc = _tile(s, 512), _ffn_col_tile(fh)
    blk, prev, _, taps, vec = _ffn_specs(s, fh, tb, tc)
    off = FFN_HALO - FFN_K + 1

    def body(a_ref, v_ref, ah_ref, vh_ref, wa_ref, wv_ref, ba_ref, bv_ref, o_ref):
        first = pl.program_id(1) == 0
        for lt in range(tc // LANES):
            cols = slice(lt * LANES, (lt + 1) * LANES)
            halo_a, halo_v = jnp.where(first, 0.0, ah_ref[:, cols]), jnp.where(first, 0.0, vh_ref[:, cols])
            wa = [wa_ref[pl.ds(k, 1), cols] for k in range(FFN_K)]
            wv = [wv_ref[pl.ds(k, 1), cols] for k in range(FFN_K)]
            ba, bv = ba_ref[:, cols], bv_ref[:, cols]

            def chunk(r0, carry, head=False, cols=cols, halo_a=halo_a, halo_v=halo_v, wa=wa, wv=wv, ba=ba, bv=bv):
                a2 = _taps3(_window_before(a_ref, halo_a, r0, TALL_ROWS, head, cols), wa, off, TALL_ROWS) + ba
                v2 = _taps3(_window_before(v_ref, halo_v, r0, TALL_ROWS, head, cols), wv, off, TALL_ROWS) + bv
                o_ref[pl.ds(r0, TALL_ROWS), cols] = (_gelu(a2) * v2).astype(bf16)
                return carry

            chunk(0, 0, head=True)
            _row_chunks(tb, TALL_ROWS, chunk, 0, start=1)

    (f,), got = _call(
        body, (upa, upv, upa, upv, wa, wv, ba, bv), name=name, grid=(fh // tc, s // tb),
        in_specs=[blk, blk, prev, prev, taps, taps, vec, vec], out_specs=[blk], out_shape=[jax.ShapeDtypeStruct((s, fh), bf16)],
        sem=("parallel", "arbitrary"), xchg=xchg)
    return f, got


def _ffn_mid_bwd_tile(cols, first, last, tb, off, df_ref, dfn_ref, a_ref, v_ref, ah_ref, vh_ref, an_ref, vn_ref, wa_ref, wv_ref,
                      ba_ref, bv_ref, da_ref, dv_ref, dwa_ref, dwv_ref, dba_ref, dbv_ref, dexta, dextv):
    rows_c = TALL_ROWS
    halo_a, halo_v = jnp.where(first, 0.0, ah_ref[:, cols]), jnp.where(first, 0.0, vh_ref[:, cols])
    wa = [wa_ref[pl.ds(k, 1), cols] for k in range(FFN_K)]
    wv = [wv_ref[pl.ds(k, 1), cols] for k in range(FFN_K)]
    ba, bv = ba_ref[:, cols], bv_ref[:, cols]

    def conv_cotangents(r0, rows, xa, xv, dfe):
        sa = [_rows_from(xa, off + k, rows) for k in range(FFN_K)]
        sv = [_rows_from(xv, off + k, rows) for k in range(FFN_K)]
        a2 = sa[0] * wa[0] + sa[1] * wa[1] + sa[2] * wa[2] + ba
        v2 = sv[0] * wv[0] + sv[1] * wv[1] + sv[2] * wv[2] + bv
        cdf = 0.5 * (1.0 + lax.erf(a2 * INV_SQRT2))
        da2 = dfe * v2 * (cdf + a2 * jnp.exp(-0.5 * a2 * a2) * INV_SQRT_2PI)
        dv2 = dfe * (a2 * cdf)
        dexta[pl.ds(r0, rows), cols] = da2
        dextv[pl.ds(r0, rows), cols] = dv2
        return da2, dv2, sa, sv

    def chunk(r0, sums, head=False):
        da2, dv2, sa, sv = conv_cotangents(r0, rows_c, _window_before(a_ref, halo_a, r0, rows_c, head, cols),
                                           _window_before(v_ref, halo_v, r0, rows_c, head, cols), df_ref[pl.ds(r0, rows_c), cols])
        new = [sums[k] + _fold8(da2 * sa[k]) for k in range(FFN_K)] + [sums[FFN_K] + _fold8(da2)]
        new += [sums[FFN_K + 1 + k] + _fold8(dv2 * sv[k]) for k in range(FFN_K)] + [sums[2 * FFN_K + 1] + _fold8(dv2)]
        return tuple(new)

    sums = chunk(0, tuple(jnp.zeros((SUBLANES, LANES), f32) for _ in range(2 * FFN_K + 2)), head=True)
    sums = _row_chunks(tb, rows_c, chunk, sums, start=1)
    conv_cotangents(tb, FFN_HALO,
                    jnp.concatenate([a_ref[pl.ds(tb - FFN_HALO, FFN_HALO), cols], jnp.where(last, 0.0, an_ref[:, cols])], axis=0),
                    jnp.concatenate([v_ref[pl.ds(tb - FFN_HALO, FFN_HALO), cols], jnp.where(last, 0.0, vn_ref[:, cols])], axis=0),
                    jnp.where(last, 0.0, dfn_ref[:, cols]))
    for k in range(FFN_K):
        dwa_ref[pl.ds(k, 1), cols] += _colsum(sums[k])
        dwv_ref[pl.ds(k, 1), cols] += _colsum(sums[FFN_K + 1 + k])
    dba_ref[:, cols] += _colsum(sums[FFN_K])
    dbv_ref[:, cols] += _colsum(sums[2 * FFN_K + 1])

    def back(r0, carry):
        for dext, w, o_ref in ((dexta, wa, da_ref), (dextv, wv, dv_ref)):
            dd = dext[pl.ds(r0, rows_c + FFN_HALO), cols]
            o_ref[pl.ds(r0, rows_c), cols] = (_rows_from(dd, 2, rows_c) * w[0] + _rows_from(dd, 1, rows_c) * w[1]
                                              + dd[0:rows_c] * w[2]).astype(bf16)
        return carry

    _row_chunks(tb, rows_c, back, 0)


def _ffn_mid_bwd(df, upa, upv, wa, wv, ba, bv, *, name, xchg=None):
    s, fh = upa.shape
    tb, tc = _tile(s, 512), _ffn_col_tile(fh)
    nb = s // tb
    blk, prev, nxt, taps, vec = _ffn_specs(s, fh, tb, tc)
    off = FFN_HALO - FFN_K + 1
    te = tb + FFN_HALO

    def body(df_ref, dfn_ref, a_ref, v_ref, ah_ref, vh_ref, an_ref, vn_ref, wa_ref, wv_ref, ba_ref, bv_ref,
             da_ref, dv_ref, dwa_ref, dwv_ref, dba_ref, dbv_ref, dexta, dextv):
        i = pl.program_id(1)

        @pl.when(i == 0)
        def _():
            for r in (dwa_ref, dwv_ref, dba_ref, dbv_ref):
                r[...] = jnp.zeros_like(r)

        last = i == nb - 1
        for lt in range(tc // LANES):
            _ffn_mid_bwd_tile(slice(lt * LANES, (lt + 1) * LANES), i == 0, last, tb, off, df_ref, dfn_ref, a_ref, v_ref,
                              ah_ref, vh_ref, an_ref, vn_ref, wa_ref, wv_ref, ba_ref, bv_ref, da_ref, dv_ref, dwa_ref, dwv_ref,
                              dba_ref, dbv_ref, dexta, dextv)

    act = jax.ShapeDtypeStruct((s, fh), bf16)
    wshape = jax.ShapeDtypeStruct((FFN_HALO, fh), f32)
    vshape = jax.ShapeDtypeStruct((1, fh), f32)
    res, got = _call(
        body, (df, df, upa, upv, upa, upv, upa, upv, wa, wv, ba, bv), name=name, grid=(fh // tc, nb),
        in_specs=[blk, nxt, blk, blk, prev, prev, nxt, nxt, taps, taps, vec, vec],
        out_specs=[blk, blk, taps, taps, vec, vec], out_shape=[act, act, wshape, wshape, vshape, vshape],
        scratch_shapes=[pltpu.VMEM((te, tc), f32)] * 2, sem=("parallel", "arbitrary"), xchg=xchg)
    return tuple(res) + (got,)


def _cols_from_shards(stacked):
    _, k, n = stacked.shape
    return stacked.transpose(1, 0, 2).reshape(k, NDEV * n)


def _pad_rows(w, rows):
    return jnp.pad(w, ((0, rows - w.shape[0]), (0, 0)))


def kernel(x, c, w_cond, b_cond, w_in, b_in, ssm_lambda_re, ssm_lambda_im, ssm_log_dt, ssm_b_re, ssm_b_im, ssm_c_re, ssm_c_im, ssm_d, ssm_glu_w_a, ssm_glu_w_b, cv_dw_w, cv_dw_b, cv_ln_g, cv_ln_b, cv_w_pw, w_out, ln1_g, ln1_b, ffn_w_up, ffn_dw_w, ffn_dw_b, ffn_w_down, ln2_g, ln2_b, loss_target, m_w_cond, m_b_cond, m_w_in, m_b_in, m_ssm_lambda_re, m_ssm_lambda_im, m_ssm_log_dt, m_ssm_b_re, m_ssm_b_im, m_ssm_c_re, m_ssm_c_im, m_ssm_d, m_ssm_glu_w_a, m_ssm_glu_w_b, m_cv_dw_w, m_cv_dw_b, m_cv_ln_g, m_cv_ln_b, m_cv_w_pw, m_w_out, m_ln1_g, m_ln1_b, m_ffn_w_up, m_ffn_dw_w, m_ffn_dw_b, m_ffn_w_down, m_ln2_g, m_ln2_b, v_w_cond, v_b_cond, v_w_in, v_b_in, v_ssm_lambda_re, v_ssm_lambda_im, v_ssm_log_dt, v_ssm_b_re, v_ssm_b_im, v_ssm_c_re, v_ssm_c_im, v_ssm_d, v_ssm_glu_w_a, v_ssm_glu_w_b, v_cv_dw_w, v_cv_dw_b, v_cv_ln_g, v_cv_ln_b, v_cv_w_pw, v_w_out, v_ln1_g, v_ln1_b, v_ffn_w_up, v_ffn_dw_w, v_ffn_dw_b, v_ffn_w_down, v_ln2_g, v_ln2_b):
    weights = dict(w_cond=w_cond, b_cond=b_cond, w_in=w_in, b_in=b_in, ssm_lambda_re=ssm_lambda_re, ssm_lambda_im=ssm_lambda_im, ssm_log_dt=ssm_log_dt, ssm_b_re=ssm_b_re, ssm_b_im=ssm_b_im, ssm_c_re=ssm_c_re, ssm_c_im=ssm_c_im, ssm_d=ssm_d, ssm_glu_w_a=ssm_glu_w_a, ssm_glu_w_b=ssm_glu_w_b, cv_dw_w=cv_dw_w, cv_dw_b=cv_dw_b, cv_ln_g=cv_ln_g, cv_ln_b=cv_ln_b, cv_w_pw=cv_w_pw, w_out=w_out, ln1_g=ln1_g, ln1_b=ln1_b, ffn_w_up=ffn_w_up, ffn_dw_w=ffn_dw_w, ffn_dw_b=ffn_dw_b, ffn_w_down=ffn_w_down, ln2_g=ln2_g, ln2_b=ln2_b)
    mom_m = dict(w_cond=m_w_cond, b_cond=m_b_cond, w_in=m_w_in, b_in=m_b_in, ssm_lambda_re=m_ssm_lambda_re, ssm_lambda_im=m_ssm_lambda_im, ssm_log_dt=m_ssm_log_dt, ssm_b_re=m_ssm_b_re, ssm_b_im=m_ssm_b_im, ssm_c_re=m_ssm_c_re, ssm_c_im=m_ssm_c_im, ssm_d=m_ssm_d, ssm_glu_w_a=m_ssm_glu_w_a, ssm_glu_w_b=m_ssm_glu_w_b, cv_dw_w=m_cv_dw_w, cv_dw_b=m_cv_dw_b, cv_ln_g=m_cv_ln_g, cv_ln_b=m_cv_ln_b, cv_w_pw=m_cv_w_pw, w_out=m_w_out, ln1_g=m_ln1_g, ln1_b=m_ln1_b, ffn_w_up=m_ffn_w_up, ffn_dw_w=m_ffn_dw_w, ffn_dw_b=m_ffn_dw_b, ffn_w_down=m_ffn_w_down, ln2_g=m_ln2_g, ln2_b=m_ln2_b)
    mom_v = dict(w_cond=v_w_cond, b_cond=v_b_cond, w_in=v_w_in, b_in=v_b_in, ssm_lambda_re=v_ssm_lambda_re, ssm_lambda_im=v_ssm_lambda_im, ssm_log_dt=v_ssm_log_dt, ssm_b_re=v_ssm_b_re, ssm_b_im=v_ssm_b_im, ssm_c_re=v_ssm_c_re, ssm_c_im=v_ssm_c_im, ssm_d=v_ssm_d, ssm_glu_w_a=v_ssm_glu_w_a, ssm_glu_w_b=v_ssm_glu_w_b, cv_dw_w=v_cv_dw_w, cv_dw_b=v_cv_dw_b, cv_ln_g=v_cv_ln_g, cv_ln_b=v_cv_ln_b, cv_w_pw=v_cv_w_pw, w_out=v_w_out, ln1_g=v_ln1_g, ln1_b=v_ln1_b, ffn_w_up=v_ffn_w_up, ffn_dw_w=v_ffn_dw_w, ffn_dw_b=v_ffn_dw_b, ffn_w_down=v_ffn_w_down, ln2_g=v_ln2_g, ln2_b=v_ln2_b)
    names = list(weights)

    s, d = x.shape[1], x.shape[2]
    sw = cw = d // 2
    fh = ffn_w_down.shape[1] * NDEV
    ng, nq = sw // SSM_GROUP, sw // QW
    gq = ng // nq
    alpha = 2.0 ** 0.25
    me = 4 * lax.axis_index("x") + 2 * lax.axis_index("y") + lax.axis_index("c")
    xs, tgt = x[0], loss_target[0]

    col_names = ["w_in", "ssm_glu_w_a", "ssm_glu_w_b", "cv_w_pw", "ffn_w_up"]
    row_names = ["w_out", "ffn_w_down"]
    big = col_names + row_names
    sent = lambda ns: [weights[n][0].astype(bf16) for n in ns]
    got_in, got_c, got_cv_taps, got_ffn_taps = _exchange(sent(["w_in"]) + [c, cv_dw_w[0, :, 0], ffn_dw_w[0, :, 0]],
                                                         scatter=False, name="gather_in")
    o1, o2, o3, o4 = sw, sw + cw, sw + 2 * cw, sw + 2 * cw + d
    in_bounds = ((0, o1), (o1, o2), (o2, o3), (o3, o4), (o4, o4 + d))
    w_u, w_cva, w_cvg, w_gs, w_gc = _unshard_cols(got_in, in_bounds, name="unshard_w_in")
    b_u, b_cva, b_cvg, b_gs, b_gc = (b_in[:, a:b] for a, b in in_bounds)
    c_all = got_c.reshape(NDEV, d)
    cv_taps = _cols_from_shards(got_cv_taps)
    ffn_taps = _cols_from_shards(got_ffn_taps)
    cv_w32 = _pad_rows(cv_taps, CONV_HALO)
    ffn_wa, ffn_wv = _pad_rows(ffn_taps[:, :fh], FFN_HALO), _pad_rows(ffn_taps[:, fh:], FFN_HALO)
    ffn_ba, ffn_bv = ffn_dw_b[:, :fh], ffn_dw_b[:, fh:]

    ncond = w_cond.shape[2]
    b_cond_mine = lax.dynamic_slice(b_cond, (0, me * ncond), (1, ncond))
    mod_cols = _cond_fwd(c_all, w_cond[0], b_cond_mine, name="cond_fwd")
    mod_all, = _exchange([mod_cols], scatter=False, name="gather_mod")
    mod_mine = lax.dynamic_slice(mod_all, (0, me, 0), (NDEV, 1, ncond)).reshape(1, 6 * d)
    sh1, sc1, g1, sh2, sc2, g2 = (mod_mine[:, k * d:(k + 1) * d] for k in range(6))

    lam_re, lam_im, log_dt = ssm_lambda_re[0], ssm_lambda_im[0], ssm_log_dt[0][:, None]
    lbr, lbi, cfr, cfi = _ssm_prep(lam_re, lam_im, log_dt, name="ssm_prep")
    rows_q = lambda a: a.reshape(nq, 1, QS)
    eye = jnp.eye(gq, dtype=f32)

    def b_mat(b):
        bt = b.reshape(nq, gq, SSM_STATE, SSM_GROUP).transpose(0, 1, 3, 2)
        return jnp.einsum("qgpn,gh->qgphn", bt, eye).reshape(nq, QW, QS)

    def c_mat(cc):
        ct = cc.reshape(nq, gq, SSM_GROUP, SSM_STATE)
        return jnp.einsum("qgpn,gh->qhngp", ct, eye).reshape(nq, QS, QW)

    def b_unmat(mt):
        return jnp.einsum("qgpgn->qgnp", mt.reshape(nq, gq, SSM_GROUP, gq, SSM_STATE)).reshape(ng, SSM_STATE, SSM_GROUP)

    def c_unmat(mt):
        return jnp.einsum("qgngp->qgpn", mt.reshape(nq, gq, SSM_STATE, gq, SSM_GROUP)).reshape(ng, SSM_GROUP, SSM_STATE)

    ssm_args = (rows_q(lbr), rows_q(lbi), b_mat(ssm_b_re[0]), b_mat(ssm_b_im[0]), c_mat(ssm_c_re[0]), c_mat(ssm_c_im[0]),
                rows_q(cfr), rows_q(cfi), ssm_d[0].reshape(1, sw))

    h1 = _lnmod(xs, sc1, sh1, name="ln_mod1")
    u, cva, cvg, gs, gc = _mm_fanout(h1, [w_u, w_cva, w_cvg, w_gs, w_gc], [b_u, b_cva, b_cvg, b_gs, b_gc],
                                     [f32, f32, f32, bf16, bf16], name="in_proj")
    v4, cv2, (got_a, got_b, got_pw, got_o) = _conv_fwd(
        cva, cvg, cv_w32, cv_dw_b, cv_ln_g, cv_ln_b, name="conv_fwd",
        xchg=(sent(["ssm_glu_w_a", "ssm_glu_w_b", "cv_w_pw", "w_out"]), False))
    h_p, yraw3, y, (got_up,) = _ssm_fwd(u, *ssm_args, name="ssm_fwd", xchg=(sent(["ffn_w_up"]), False))
    w_a, = _unshard_cols(got_a, ((0, d),), name="unshard_glu_a")
    w_b, = _unshard_cols(got_b, ((0, d),), name="unshard_glu_b")
    w_pw, = _unshard_cols(got_pw, ((0, d),), name="unshard_conv_pw")
    w_upa, w_upv = _unshard_cols(got_up, ((0, fh), (fh, 2 * fh)), name="unshard_ffn_up")
    w_o = got_o.reshape(d, d)
    ya, yb = _mm_fanout(y, [w_a, w_b], None, [bf16, bf16], name="glu")
    ycv = _mm([(v4, w_pw)], out_dtype=bf16, name="conv_pw")
    merged = _glu_merge(ya, yb, ycv, gs, gc, name="merge")
    o = _mm([(merged, w_o)], out_dtype=bf16, name="out_proj")
    x1, h2 = _resid_ln_mod(xs, o, g1, ln1_g, ln1_b, sc2, sh2, alpha, name="resid_ln1")
    upa = _mm([(h2, w_upa)], name="ffn_up_a")
    upv = _mm([(h2, w_upv)], name="ffn_up_v")
    f, (got_dn,) = _ffn_mid(upa, upv, ffn_wa, ffn_wv, ffn_ba, ffn_bv, name="ffn_mid", xchg=(sent(["ffn_w_down"]), False))
    w_dn = got_dn.reshape(fh, d)
    dr2, dy2, loss_part, d_ln2_g, d_ln2_b, d_g2 = _resid_ln_loss(x1, f, w_dn, g2, ln2_g, ln2_b, tgt, alpha, name="ffn_down_ln2_loss")

    gw = {}
    df = _mm([(dy2, w_dn)], trans_w=True, name="d_ffn_down")
    gw["ffn_w_down"] = _mm_tn(f, dy2, out_dtype=bf16, name="g_ffn_down").reshape((NDEV,) + ffn_w_down[0].shape)
    received = {}
    dupa, dupv, d_ffn_wa, d_ffn_wv, d_ffn_ba, d_ffn_bv, (received["ffn_w_down"],) = _ffn_mid_bwd(
        df, upa, upv, ffn_wa, ffn_wv, ffn_ba, ffn_bv, name="ffn_mid_bwd", xchg=([gw["ffn_w_down"]], True))
    dh2 = _mm([(dupa, w_upa), (dupv, w_upv)], trans_w=True, out_dtype=bf16, name="d_ffn_up")
    gw["ffn_w_up"] = _shard_cols([_mm_tn(h2, dupa, name="g_ffn_up_a"), _mm_tn(h2, dupv, name="g_ffn_up_v")], out_dtype=bf16,
                                 name="shard_ffn_up")
    dr1, do, d_sc2, d_sh2, d_ln1_g, d_ln1_b, d_g1 = _mid_bwd(dh2, x1, dr2, xs, o, g1, sc2, ln1_g, alpha, name="mid_bwd")
    dmerged = _mm([(do, w_o)], trans_w=True, out_dtype=bf16, name="d_out_proj")
    gw["w_out"] = _mm_tn(merged, do, out_dtype=bf16, name="g_out_proj").reshape((NDEV,) + w_out[0].shape)
    dya, dyb, dycv, dgs, dgc, s_gs, s_gc = _glu_merge_bwd(dmerged, ya, yb, ycv, gs, gc, name="merge_bwd")
    dy = _mm([(dya, w_a), (dyb, w_b)], trans_w=True, out_dtype=bf16, name="d_glu")
    gw["ssm_glu_w_a"] = _shard_cols([_mm_tn(y, dya, name="g_glu_a")], out_dtype=bf16, name="shard_glu_a")
    gw["ssm_glu_w_b"] = _shard_cols([_mm_tn(y, dyb, name="g_glu_b")], out_dtype=bf16, name="shard_glu_b")
    dv4 = _mm([(dycv, w_pw)], trans_w=True, out_dtype=bf16, name="d_conv_pw")
    gw["cv_w_pw"] = _shard_cols([_mm_tn(v4, dycv, name="g_conv_pw")], out_dtype=bf16, name="shard_conv_pw")
    dv2, d_cv_ln_g, d_cv_ln_b = _conv_bwd_ln(dv4, cv2, cv_ln_g, cv_ln_b, name="conv_bwd_ln")
    dcva, dcvg, d_cv_w32, d_cv_b, s_cva, s_cvg, (received["ffn_w_up"],) = _conv_bwd_taps(
        dv2, cva, cvg, cv_w32, name="conv_bwd_taps", xchg=([gw["ffn_w_up"]], True))
    late = ["w_out", "ssm_glu_w_a", "ssm_glu_w_b", "cv_w_pw"]
    (du, d_bre_m, d_bim_m, d_cre_m, d_cim_m, d_cfr, d_cfi, d_lbr, d_lbi, d_d, s_u, got_late) = _ssm_bwd(
        dy, yraw3, u, h_p, *ssm_args, name="ssm_bwd", xchg=([gw[n] for n in late], True))
    received.update(zip(late, got_late))
    gshape = lam_re.shape
    d_lam_re, d_lam_im, d_log_dt = _ssm_prep_bwd(
        lam_re, lam_im, log_dt, [a.reshape(gshape) for a in (d_lbr, d_lbi, d_cfr, d_cfi)], name="ssm_prep_bwd")
    small = {
        "b_in": jnp.concatenate([s_u, s_cva, s_cvg, s_gs, s_gc], axis=1),
        "ssm_lambda_re": d_lam_re, "ssm_lambda_im": d_lam_im, "ssm_log_dt": d_log_dt,
        "ssm_b_re": b_unmat(d_bre_m), "ssm_b_im": b_unmat(d_bim_m), "ssm_c_re": c_unmat(d_cre_m), "ssm_c_im": c_unmat(d_cim_m),
        "ssm_d": d_d, "cv_dw_w": d_cv_w32[:CONV_K], "cv_dw_b": d_cv_b, "cv_ln_g": d_cv_ln_g, "cv_ln_b": d_cv_ln_b,
        "ln1_g": d_ln1_g, "ln1_b": d_ln1_b,
        "ffn_dw_w": jnp.concatenate([d_ffn_wa[:FFN_K], d_ffn_wv[:FFN_K]], axis=1),
        "ffn_dw_b": jnp.concatenate([d_ffn_ba, d_ffn_bv], axis=1), "ln2_g": d_ln2_g, "ln2_b": d_ln2_b,
        "mod_g1": d_g1, "mod_sh2": d_sh2, "mod_sc2": d_sc2, "mod_g2": d_g2, "loss": loss_part,
    }
    small_names = list(small)
    small_shapes = [small[n].shape for n in small_names]
    gw["w_in"], (small_all,) = _mm_tn_sharded(h1, [du, dcva, dcvg, dgs, dgc], out_dtype=bf16, name="g_in",
                                              xchg=([_pack([small[n] for n in small_names])], False))
    dh1, (received["w_in"],) = _mm(
        [(du, w_u), (dcva, w_cva), (dcvg, w_cvg), (dgs, w_gs), (dgc, w_gc)], trans_w=True, out_dtype=bf16, name="d_in",
        xchg=([gw["w_in"]], True))
    grad_x, d_sc1, d_sh1, _ = _final_bwd(dh1, xs, dr1, sc1, alpha, name="final_bwd")

    grads, delta, new_m, new_v = {}, {}, {}, {}
    for n in big:
        ride = ([_pack([d_sh1, d_sc1])], False) if n == "w_out" else None
        res = _sum_adamw(received[n], weights[n][0], mom_m[n][0], mom_v[n][0], name="adamw_" + n, xchg=ride)
        grads[n], delta[n], new_m[n], new_v[n] = res[:4]
        if ride is not None:
            last_all, = res[4]

    small_sum = dict(zip(small_names, _unpack(_sum_parts(small_all, name="sum_small").reshape(-1), small_shapes)))
    last_sum = _unpack(_sum_parts(last_all, name="sum_last").reshape(-1), [(1, d), (1, d)])
    per_dev = dict(zip(small_names, _unpack(small_all.reshape(NDEV, -1), small_shapes)))
    last_dev = _unpack(last_all.reshape(NDEV, -1), [(1, d), (1, d)])
    dmod_all = jnp.concatenate(last_dev + [per_dev[k] for k in ("mod_g1", "mod_sh2", "mod_sc2", "mod_g2")], axis=-1).reshape(NDEV, 6 * d)
    dmod_cols = lax.dynamic_slice(dmod_all.reshape(NDEV, NDEV, ncond), (0, me, 0), (NDEV, 1, ncond)).reshape(NDEV, ncond)
    grads["w_cond"] = _cond_bwd(c_all, dmod_cols, name="cond_bwd")
    loss = small_sum.pop("loss").reshape(())
    grads["b_cond"] = jnp.concatenate(last_sum + [small_sum.pop(k) for k in ("mod_g1", "mod_sh2", "mod_sc2", "mod_g2")], axis=1)
    for n, g in small_sum.items():
        grads[n] = g
    ntap = cv_dw_w.shape[3]
    grads["cv_dw_w"] = lax.dynamic_slice(grads["cv_dw_w"], (0, me * ntap), (CONV_K, ntap))
    nffn = ffn_dw_w.shape[3]
    grads["ffn_dw_w"] = lax.dynamic_slice(grads["ffn_dw_w"], (0, me * nffn), (FFN_K, nffn))
    grads = {n: grads[n].reshape(weights[n].shape) for n in names}

    delta["w_cond"], new_m["w_cond"], new_v["w_cond"] = _adamw(w_cond[0], grads["w_cond"][0], m_w_cond[0], v_w_cond[0],
                                                               name="adamw_w_cond")
    rest = [n for n in names if n not in ["w_cond"] + big]
    squeeze = lambda a: a if a.ndim == 2 else a[0]
    results = _adamw_many(*[[squeeze(t[n].reshape(weights[n].shape)) for n in rest] for t in (weights, grads, mom_m, mom_v)],
                          name="adamw_small")
    for n, (dl, nm, nv) in zip(rest, results):
        delta[n], new_m[n], new_v[n] = dl, nm, nv
    shaped = lambda t: [t[n].reshape(weights[n].shape) for n in names]

    return (loss, grad_x[None], *shaped(grads), *shaped(delta), *shaped(new_m), *shaped(new_v))
```

```python
import functools
import math

import jax
import jax.numpy as jnp
from jax import lax
from jax.experimental import pallas as pl
from jax.experimental.pallas import tpu as pltpu

f32 = jnp.float32
bf16 = jnp.bfloat16

NDEV = 8
LANES = 128
SUBLANES = 8
SSM_GROUP = 16
SSM_STATE = 64
QW = 128
QS = 512
CONV_K = 31
CONV_HALO = 32
FFN_K = 3
FFN_HALO = 8
LN_EPS = 1e-5
ADAM_LR, ADAM_B1, ADAM_B2, ADAM_EPS, ADAM_WD, ADAM_STEP = 0.001, 0.9, 0.999, 1e-08, 0.01, 10
VMEM_LIMIT = 56 * 1024 * 1024
W_TILE_BYTES = 6 * 1024 * 1024
SUM_ROWS = 512
EW_BLOCK_BYTES = 2 * 1024 * 1024
INV_SQRT2 = 1.0 / math.sqrt(2.0)
INV_SQRT_2PI = 1.0 / math.sqrt(2.0 * math.pi)
MESH = pl.DeviceIdType.MESH


def _tile(n, want):
    t = min(n, want)
    while n % t:
        t //= 2
    return t


def _col_tile(n, rows, bytes_per):
    best = LANES if n % LANES == 0 else n
    for t in range(LANES, n + 1, LANES):
        if n % t == 0 and rows * t * bytes_per <= W_TILE_BYTES:
            best = t
    return best


def _params(*sem):
    return pltpu.CompilerParams(dimension_semantics=sem, vmem_limit_bytes=VMEM_LIMIT)


def _row(i):
    return (0, 0)


def _full(shape):
    nd = len(shape)
    return pl.BlockSpec(shape, lambda *a: (0,) * nd)


def _ln(x):
    mu = jnp.mean(x, axis=-1, keepdims=True)
    xc = x - mu
    var = jnp.mean(xc * xc, axis=-1, keepdims=True)
    rstd = lax.rsqrt(var + LN_EPS)
    return xc * rstd, rstd


def _ln_bwd(dxhat, xhat, rstd):
    return rstd * (dxhat - jnp.mean(dxhat, axis=-1, keepdims=True) - xhat * jnp.mean(dxhat * xhat, axis=-1, keepdims=True))


def _sig(x):
    return 1.0 / (1.0 + jnp.exp(-x))


def _gelu(x):
    return 0.5 * x * (1.0 + lax.erf(x * INV_SQRT2))


def _gelu_grad(x):
    return 0.5 * (1.0 + lax.erf(x * INV_SQRT2)) + x * jnp.exp(-0.5 * x * x) * INV_SQRT_2PI


def _colsum(x):
    return jnp.sum(x, axis=0, keepdims=True)


def _mm(pairs, bias=None, *, trans_w=False, out_dtype=f32, name, xchg=None):
    n_p = len(pairs)
    m = pairs[0][0].shape[0]
    n = pairs[0][1].shape[0 if trans_w else 1]
    ktot = sum(x.shape[1] for x, _ in pairs)
    tm = _tile(m, 512)
    tn = _col_tile(n, ktot, 2)
    dn = (((1,), (1,)), ((), ())) if trans_w else (((1,), (0,)), ((), ()))

    def body(*refs):
        o_ref = refs[-1]
        acc = None
        for xr, wr in zip(refs[:n_p], refs[n_p:2 * n_p]):
            r = lax.dot_general(xr[...].astype(bf16), wr[...].astype(bf16), dn, preferred_element_type=f32)
            acc = r if acc is None else acc + r
        if bias is not None:
            acc = acc + refs[2 * n_p][...]
        o_ref[...] = acc.astype(out_dtype)

    in_specs = [pl.BlockSpec((tm, x.shape[1]), lambda j, i: (i, 0)) for x, _ in pairs]
    if trans_w:
        in_specs += [pl.BlockSpec((tn, w.shape[1]), lambda j, i: (j, 0)) for _, w in pairs]
    else:
        in_specs += [pl.BlockSpec((w.shape[0], tn), lambda j, i: (0, j)) for _, w in pairs]
    args = [x for x, _ in pairs] + [w for _, w in pairs]
    if bias is not None:
        in_specs.append(pl.BlockSpec((1, tn), lambda j, i: (0, j)))
        args.append(bias)
    (out,), got = _call(
        body, args, name=name, grid=(n // tn, m // tm), in_specs=in_specs,
        out_specs=[pl.BlockSpec((tm, tn), lambda j, i: (i, j))], out_shape=[jax.ShapeDtypeStruct((m, n), out_dtype)],
        sem=("parallel", "arbitrary"), xchg=xchg)
    return out if xchg is None else (out, got)


def _mm_fanout(x, ws, biases, out_dtypes, *, name):
    m, k = x.shape
    tm = _tile(m, 512)
    n_w = len(ws)
    biases = list(biases or [])

    def body(x_ref, *refs):
        xb = x_ref[...].astype(bf16)
        o_refs = refs[n_w + len(biases):]
        for p, (w_ref, o_ref, dt) in enumerate(zip(refs[:n_w], o_refs, out_dtypes)):
            acc = jnp.dot(xb, w_ref[...], preferred_element_type=f32)
            if biases:
                acc = acc + refs[n_w + p][...]
            o_ref[...] = acc.astype(dt)

    return pl.pallas_call(
        body, name=name, grid=(m // tm,),
        in_specs=[pl.BlockSpec((tm, k), lambda i: (i, 0))] + [_full(w.shape) for w in ws] + [_full(b.shape) for b in biases],
        out_specs=[pl.BlockSpec((tm, w.shape[1]), lambda i: (i, 0)) for w in ws],
        out_shape=[jax.ShapeDtypeStruct((m, w.shape[1]), dt) for w, dt in zip(ws, out_dtypes)],
        compiler_params=_params("parallel"))(x, *ws, *biases)


def _mm_tn(x, dy, *, out_dtype=f32, name):
    m, k = x.shape
    n = dy.shape[1]
    tm = _tile(m, 512)
    tn = _col_tile(n, k, 4)
    steps = m // tm

    def body(x_ref, dy_ref, o_ref, *scratch):
        acc = scratch[0] if scratch else o_ref

        @pl.when(pl.program_id(1) == 0)
        def _():
            acc[...] = jnp.zeros_like(acc)

        acc[...] += lax.dot_general(x_ref[...].astype(bf16), dy_ref[...].astype(bf16), (((0,), (0,)), ((), ())),
                                    preferred_element_type=f32)
        if scratch:
            @pl.when(pl.program_id(1) == steps - 1)
            def _():
                o_ref[...] = acc[...].astype(out_dtype)

    return pl.pallas_call(
        body, name=name, grid=(n // tn, steps),
        in_specs=[pl.BlockSpec((tm, k), lambda j, i: (i, 0)), pl.BlockSpec((tm, tn), lambda j, i: (i, j))],
        out_specs=pl.BlockSpec((k, tn), lambda j, i: (0, j)),
        out_shape=jax.ShapeDtypeStruct((k, n), out_dtype),
        scratch_shapes=[] if out_dtype == f32 else [pltpu.VMEM((k, tn), f32)],
        compiler_params=_params("parallel", "arbitrary"),
    )(x, dy)


def _mm_tn_sharded(x, dys, *, out_dtype, name, xchg=None):
    m, k = x.shape
    widths = [dy.shape[1] for dy in dys]
    n = sum(widths) // NDEV
    tm = _tile(m, 512)
    steps = m // tm
    n_d = len(dys)

    def body(x_ref, *refs):
        dy_refs, o_ref, acc = refs[:n_d], refs[n_d], refs[n_d + 1]
        i = pl.program_id(0)

        @pl.when(i == 0)
        def _():
            acc[...] = jnp.zeros_like(acc)

        xb = x_ref[...].astype(bf16)
        off = 0
        for dy_ref, w in zip(dy_refs, widths):
            acc[:, off:off + w] += lax.dot_general(xb, dy_ref[...].astype(bf16), (((0,), (0,)), ((), ())), preferred_element_type=f32)
            off += w

        @pl.when(i == steps - 1)
        def _():
            for j in range(NDEV):
                o_ref[j] = acc[:, n * j:n * (j + 1)].astype(out_dtype)

    (out,), got = _call(
        body, (x, *dys), name=name, grid=(steps,),
        in_specs=[pl.BlockSpec((tm, k), lambda i: (i, 0))] + [pl.BlockSpec((tm, w), lambda i: (i, 0)) for w in widths],
        out_specs=[pl.BlockSpec((NDEV, k, n), lambda i: (0, 0, 0))], out_shape=[jax.ShapeDtypeStruct((NDEV, k, n), out_dtype)],
        scratch_shapes=[pltpu.VMEM((k, sum(widths)), f32)], sem=("arbitrary",), xchg=xchg)
    return out if xchg is None else (out, got)


def _exchange(arrs, *, scatter, name):
    n = len(arrs)

    def body(*refs):
        _exchange_copies(refs[:n], refs[n:2 * n], refs[2 * n:], scatter, True, True)

    return pl.pallas_call(
        body, name=name, in_specs=[HBM_SPEC] * n, out_specs=[HBM_SPEC] * n, out_shape=_exchange_out_shape(arrs, scatter),
        scratch_shapes=_exchange_sems(n),
    )(*arrs)


HBM_SPEC = pl.BlockSpec(memory_space=pltpu.HBM)


def _flags(scatter, n):
    return list(scatter) if isinstance(scatter, (list, tuple)) else [scatter] * n


def _exchange_out_shape(arrs, scatter):
    return [jax.ShapeDtypeStruct(a.shape if sc else (NDEV,) + a.shape, a.dtype) for a, sc in zip(arrs, _flags(scatter, len(arrs)))]


def _exchange_sems(n):
    return [pltpu.SemaphoreType.DMA(((NDEV - 1) * n,)), pltpu.SemaphoreType.DMA(((NDEV - 1) * n,)), pltpu.SemaphoreType.DMA((n,))]


def _exchange_copies(x_refs, o_refs, sems, scatter, start, wait):
    n = len(x_refs)
    flags = _flags(scatter, n)
    send_sems, recv_sems, local_sems = sems
    ix, iy, ic = lax.axis_index("x"), lax.axis_index("y"), lax.axis_index("c")
    me = 4 * ix + 2 * iy + ic
    local = [pltpu.make_async_copy(x.at[me] if sc else x, o.at[me], local_sems.at[a])
             for a, (x, o, sc) in enumerate(zip(x_refs, o_refs, flags))]

    def peer(k):
        return (1 - ix if k & 4 else ix, 1 - iy if k & 2 else iy, 1 - ic if k & 1 else ic)

    def index(p):
        return 4 * p[0] + 2 * p[1] + p[2]

    def copy(a, k, src, dst, to):
        sem = (k - 1) * n + a
        return pltpu.make_async_remote_copy(src_ref=src, dst_ref=dst, send_sem=send_sems.at[sem], recv_sem=recv_sems.at[sem],
                                            device_id=to, device_id_type=MESH)

    sends, arrivals, passed_on = [], [], []
    for a, (x, o, sc) in enumerate(zip(x_refs, o_refs, flags)):
        if sc:
            for k in range(1, NDEV):
                p = peer(k)
                sends.append(copy(a, k, x.at[index(p)], o.at[me], p))
                arrivals.append(copy(a, k, x.at[me], o.at[index(p)], p))
        else:
            sib = peer(1)
            sends.append(copy(a, 1, x, o.at[me], sib))
            arrivals.append(copy(a, 1, x, o.at[index(sib)], sib))
            for k in (2, 4, 6):
                p, q = peer(k), peer(k + 1)
                sends.append(copy(a, k, x, o.at[me], p))
                passed_on.append((copy(a, k, x, o.at[index(p)], p), copy(a, k + 1, o.at[index(p)], o.at[index(p)], sib)))
                arrivals.append(copy(a, k + 1, o.at[index(q)], o.at[index(q)], sib))
    if start:
        for cp in local + sends:
            cp.start()
    if wait:
        for landed, hand_over in passed_on:
            landed.wait_recv()
            hand_over.start()
        for cp in arrivals:
            cp.wait_recv()
        for cp in sends + [hand_over for _, hand_over in passed_on]:
            cp.wait_send()
        for cp in local:
            cp.wait()


def _call(body, args, *, name, grid, in_specs, out_specs, out_shape, scratch_shapes=(), sem, xchg=None):
    if xchg is None:
        return pl.pallas_call(body, name=name, grid=grid, in_specs=in_specs, out_specs=out_specs, out_shape=out_shape,
                              scratch_shapes=list(scratch_shapes), compiler_params=_params(*sem))(*args), None
    arrs, scatter = xchg
    n, ni, no, ns = len(arrs), len(in_specs), len(out_specs), len(scratch_shapes)

    def wrapped(*refs):
        ins, x_refs = refs[:ni], refs[ni:ni + n]
        outs, o_refs = refs[ni + n:ni + n + no], refs[ni + n + no:ni + 2 * n + no]
        scratch, sems = refs[ni + 2 * n + no:ni + 2 * n + no + ns], refs[ni + 2 * n + no + ns:]
        ids = [pl.program_id(a) for a in range(len(grid))]
        first = functools.reduce(jnp.logical_and, [p == 0 for p in ids])
        last = functools.reduce(jnp.logical_and, [p == g - 1 for p, g in zip(ids, grid)])

        @pl.when(first)
        def _():
            _exchange_copies(x_refs, o_refs, sems, scatter, True, False)

        body(*ins, *outs, *scratch)

        @pl.when(last)
        def _():
            _exchange_copies(x_refs, o_refs, sems, scatter, False, True)

    res = pl.pallas_call(
        wrapped, name=name, grid=grid, in_specs=list(in_specs) + [HBM_SPEC] * n, out_specs=list(out_specs) + [HBM_SPEC] * n,
        out_shape=list(out_shape) + _exchange_out_shape(arrs, scatter),
        scratch_shapes=list(scratch_shapes) + _exchange_sems(n),
        compiler_params=_params(*("arbitrary",) * len(grid)))(*args, *arrs)
    return res[:no], res[no:]


def _sum_parts(parts, *, name):
    r = parts.shape[1]

    def body(p_ref, o_ref):
        acc = p_ref[0]
        for j in range(1, NDEV):
            acc = acc + p_ref[j]
        o_ref[...] = acc

    return pl.pallas_call(body, name=name, out_shape=jax.ShapeDtypeStruct((r, LANES), f32), compiler_params=_params())(parts)


def _col_pieces(n, bounds):
    out = []
    for p, (a, b) in enumerate(bounds):
        for j in range(NDEV):
            lo, hi = max(a, n * j), min(b, n * (j + 1))
            if lo < hi:
                out.append((p, j, lo - a, lo - n * j, hi - lo))
    return out


def _unshard_cols(stacked, bounds, *, name):
    _, k, n = stacked.shape
    tk = _tile(k, 256)
    plan = _col_pieces(n, bounds)

    def body(x_ref, *o_refs):
        for p, j, po, so, w in plan:
            o_refs[p][:, po:po + w] = x_ref[j, :, so:so + w]

    return pl.pallas_call(
        body, name=name, grid=(k // tk,), in_specs=[pl.BlockSpec((NDEV, tk, n), lambda i: (0, i, 0))],
        out_specs=[pl.BlockSpec((tk, b - a), lambda i: (i, 0)) for a, b in bounds],
        out_shape=[jax.ShapeDtypeStruct((k, b - a), stacked.dtype) for a, b in bounds],
        compiler_params=_params("parallel"))(stacked)


def _shard_cols(pieces, *, out_dtype, name):
    k = pieces[0].shape[0]
    bounds, off = [], 0
    for p in pieces:
        bounds.append((off, off + p.shape[1]))
        off += p.shape[1]
    n = off // NDEV
    tk = _tile(k, 256)
    plan = _col_pieces(n, bounds)

    def body(*refs):
        o_ref = refs[-1]
        for p, j, po, so, w in plan:
            o_ref[j, :, so:so + w] = refs[p][:, po:po + w].astype(out_dtype)

    return pl.pallas_call(
        body, name=name, grid=(k // tk,), in_specs=[pl.BlockSpec((tk, b - a), lambda i: (i, 0)) for a, b in bounds],
        out_specs=pl.BlockSpec((NDEV, tk, n), lambda i: (0, i, 0)),
        out_shape=jax.ShapeDtypeStruct((NDEV, k, n), out_dtype),
        compiler_params=_params("parallel"))(*pieces)


def _pack(arrs):
    flat = jnp.concatenate([a.reshape(-1) for a in arrs])
    pad = (-flat.shape[0]) % (SUBLANES * LANES)
    return jnp.pad(flat, (0, pad)).reshape(-1, LANES)


def _unpack(flat, shapes):
    out, off = [], 0
    for s in shapes:
        n = math.prod(s)
        out.append(flat[..., off:off + n].reshape(flat.shape[:-1] + tuple(s)))
        off += n
    return out


def _adamw_math(w, gg, m, v):
    nm = ADAM_B1 * m + (1.0 - ADAM_B1) * gg
    nv = ADAM_B2 * v + (1.0 - ADAM_B2) * (gg * gg)
    m_hat = nm / (1.0 - ADAM_B1 ** ADAM_STEP)
    v_hat = nv / (1.0 - ADAM_B2 ** ADAM_STEP)
    return -ADAM_LR * (m_hat / (jnp.sqrt(v_hat) + ADAM_EPS) + ADAM_WD * w), nm, nv


def _row_block(r, c, copies):
    tr = r
    while copies * tr * c * 4 > EW_BLOCK_BYTES and tr % (4 * SUBLANES) == 0:
        tr //= 2
    return tr


def _adamw(w, g, m, v, *, name):
    r, c = w.shape
    tr = _row_block(r, c, 1)

    def body(w_ref, g_ref, m_ref, v_ref, d_ref, nm_ref, nv_ref):
        d_ref[...], nm_ref[...], nv_ref[...] = _adamw_math(w_ref[...], g_ref[...], m_ref[...], v_ref[...])

    spec = pl.BlockSpec((tr, c), lambda i: (i, 0))
    shp = jax.ShapeDtypeStruct((r, c), f32)
    return pl.pallas_call(
        body, name=name, grid=(r // tr,), in_specs=[spec] * 4, out_specs=[spec] * 3, out_shape=[shp] * 3,
        compiler_params=_params("parallel"),
    )(w, g, m, v)


def _adamw_many(ws, gs, ms, vs, *, name):
    n = len(ws)

    def body(*refs):
        outs = refs[4 * n:]
        for i in range(n):
            res = _adamw_math(refs[i][...], refs[n + i][...], refs[2 * n + i][...], refs[3 * n + i][...])
            for o_ref, r in zip(outs[3 * i:3 * i + 3], res):
                o_ref[...] = r

    res = pl.pallas_call(body, name=name, out_shape=[jax.ShapeDtypeStruct(w.shape, f32) for w in ws for _ in range(3)],
                         compiler_params=_params())(*ws, *gs, *ms, *vs)
    return [res[3 * i:3 * i + 3] for i in range(n)]


def _sum_adamw(parts, w, m, v, *, name, xchg=None):
    r, c = w.shape
    tr = _row_block(r, c, NDEV)

    def body(p_ref, w_ref, m_ref, v_ref, g_ref, d_ref, nm_ref, nv_ref):
        gg = p_ref[0].astype(f32)
        for j in range(1, NDEV):
            gg = gg + p_ref[j].astype(f32)
        g_ref[...] = gg
        d_ref[...], nm_ref[...], nv_ref[...] = _adamw_math(w_ref[...], gg, m_ref[...], v_ref[...])

    spec = pl.BlockSpec((tr, c), lambda i: (i, 0))
    shp = jax.ShapeDtypeStruct((r, c), f32)
    res, got = _call(
        body, (parts, w, m, v), name=name, grid=(r // tr,),
        in_specs=[pl.BlockSpec((NDEV, tr, c), lambda i: (0, i, 0))] + [spec] * 3,
        out_specs=[spec] * 4, out_shape=[shp] * 4, sem=("parallel",), xchg=xchg)
    return tuple(res) if xchg is None else tuple(res) + (got,)


def _cond_fwd(c_all, w, b, *, name):
    nb, n = c_all.shape[0], w.shape[1]

    def body(c_ref, w_ref, b_ref, o_ref):
        cc = c_ref[...]
        o_ref[...] = jnp.dot(cc * _sig(cc), w_ref[...], preferred_element_type=f32,
                             precision=lax.Precision.HIGHEST) + b_ref[...]

    return pl.pallas_call(body, name=name, out_shape=jax.ShapeDtypeStruct((nb, n), f32),
                          compiler_params=_params())(c_all, w, b)


def _cond_bwd(c_all, dmod, *, name):
    d, n = c_all.shape[1], dmod.shape[1]

    def body(c_ref, g_ref, o_ref):
        cc = c_ref[...]
        o_ref[...] = lax.dot_general(cc * _sig(cc), g_ref[...], (((0,), (0,)), ((), ())), preferred_element_type=f32,
                                     precision=lax.Precision.HIGHEST)

    return pl.pallas_call(body, name=name, out_shape=jax.ShapeDtypeStruct((d, n), f32),
                          compiler_params=_params())(c_all, dmod)


def _ssm_disc(lam_re, lam_im, log_dt):
    lr = jnp.minimum(lam_re, -1e-4)
    li = lam_im
    dt = jnp.exp(log_dt)
    mag = jnp.exp(lr * dt)
    ang = li * dt
    lbr, lbi = mag * jnp.cos(ang), mag * jnp.sin(ang)
    num_r, num_i = lbr - 1.0, lbi
    den = lr * lr + li * li
    return lbr, lbi, (num_r * lr + num_i * li) / den, (num_i * lr - num_r * li) / den


def _ssm_prep(lam_re, lam_im, log_dt, *, name):
    def body(a, b, c, o1, o2, o3, o4):
        o1[...], o2[...], o3[...], o4[...] = _ssm_disc(a[...], b[...], c[...])

    shp = jax.ShapeDtypeStruct(lam_re.shape, f32)
    return pl.pallas_call(body, name=name, out_shape=[shp] * 4, compiler_params=_params())(lam_re, lam_im, log_dt)


def _ssm_prep_bwd(lam_re, lam_im, log_dt, cts, *, name):
    def body(a, b, c, g1, g2, g3, g4, o1, o2, o3):
        _, vjp = jax.vjp(_ssm_disc, a[...], b[...], c[...])
        o1[...], o2[...], o3[...] = vjp((g1[...], g2[...], g3[...], g4[...]))

    shp = jax.ShapeDtypeStruct(lam_re.shape, f32)
    return pl.pallas_call(body, name=name, out_shape=[shp, shp, jax.ShapeDtypeStruct(log_dt.shape, f32)],
                          compiler_params=_params())(lam_re, lam_im, log_dt, *cts)


S5_ROWS = 512


def _step_major(x3):
    k, nt, c = x3.shape
    return jnp.swapaxes(x3, 0, 1).reshape(k * nt, c)


def _chunk_major(x2, nt):
    return jnp.swapaxes(x2.reshape(nt, SUBLANES, x2.shape[1]), 0, 1)


def _chain_carries(loc_r, loc_i, pr, pi_, forward):
    row = lax.broadcasted_iota(jnp.int32, loc_r.shape, 0)
    shift = 1 if forward else SUBLANES - 1
    order = range(1, SUBLANES) if forward else range(SUBLANES - 2, -1, -1)
    er, ei = loc_r, loc_i
    for k in order:
        sr, si = pltpu.roll(er, shift, 0), pltpu.roll(ei, shift, 0)
        er = jnp.where(row == k, loc_r + pr * sr - pi_ * si, er)
        ei = jnp.where(row == k, loc_i + pr * si + pi_ * sr, ei)
    edge = 0 if forward else SUBLANES - 1
    return (jnp.where(row == edge, 0.0, pltpu.roll(er, shift, 0)), jnp.where(row == edge, 0.0, pltpu.roll(ei, shift, 0)))


def _chunk_power(ar, ai, chunk_len):
    pr, pi_ = ar, ai
    for _ in range(int(math.log2(chunk_len))):
        pr, pi_ = pr * pr - pi_ * pi_, 2.0 * pr * pi_
    return pr, pi_


def _ssm_mats(bre_ref, bim_ref, cre_ref, cim_ref, cfr_ref, cfi_ref, bbar_s, cmat_s, nq):
    for q in range(nq):
        cr, ci, br, bi = cfr_ref[q], cfi_ref[q], bre_ref[q], bim_ref[q]
        bbar_s[q, :, 0:QS] = (cr * br - ci * bi).astype(bf16)
        bbar_s[q, :, QS:2 * QS] = (cr * bi + ci * br).astype(bf16)
        cmat_s[q, 0:QS, :] = cre_ref[q].astype(bf16)
        cmat_s[q, QS:2 * QS, :] = (-cim_ref[q]).astype(bf16)


def _ssm_fwd(u, ar, ai, bre, bim, cre, cim, cfr, cfi, dvec, *, name, xchg=None):
    s, sw = u.shape
    nq = sw // QW
    st = nq * 2 * QS
    tb = _tile(s, S5_ROWS)
    nb, nt, chunk_len = s // tb, tb // SUBLANES, s // SUBLANES
    assert chunk_len & (chunk_len - 1) == 0 and nt % 16 == 0

    def body(u_ref, ar_ref, ai_ref, bre_ref, bim_ref, cre_ref, cim_ref, cfr_ref, cfi_ref, d_ref,
             h_out, yraw_out, y_out, buf, hc, bbar_s, cmat_s):
        ph, i = pl.program_id(0), pl.program_id(1)

        @pl.when(i == 0)
        def _():
            _ssm_mats(bre_ref, bim_ref, cre_ref, cim_ref, cfr_ref, cfi_ref, bbar_s, cmat_s, nq)

        @pl.when((ph == 0) & (i == 0))
        def _():
            hc[...] = jnp.zeros_like(hc)

        @pl.when((ph == 1) & (i == 0))
        def _():
            for q in range(nq):
                o = q * 2 * QS
                pr, pi_ = _chunk_power(ar_ref[q], ai_ref[q], chunk_len)
                sr, si = _chain_carries(hc[:, o:o + QS], hc[:, o + QS:o + 2 * QS], pr, pi_, True)
                hc[:, o:o + QS] = sr
                hc[:, o + QS:o + 2 * QS] = si

        uu = u_ref[...]
        up = _step_major(uu).astype(bf16)
        for q in range(nq):
            o = q * 2 * QS
            buf[:, o:o + 2 * QS] = jnp.dot(up[:, q * QW:(q + 1) * QW], bbar_s[q], preferred_element_type=f32)

        for q in range(nq):
            o = q * 2 * QS
            a_r = jnp.broadcast_to(ar_ref[q], (SUBLANES, QS))
            a_i = jnp.broadcast_to(ai_ref[q], (SUBLANES, QS))

            def step(t, carry, o=o, a_r=a_r, a_i=a_i):
                hr, hi = carry
                r0 = pl.multiple_of(t * SUBLANES, SUBLANES)
                nr = a_r * hr - a_i * hi + buf[pl.ds(r0, SUBLANES), o:o + QS]
                ni = a_r * hi + a_i * hr + buf[pl.ds(r0, SUBLANES), o + QS:o + 2 * QS]
                buf[pl.ds(r0, SUBLANES), o:o + QS] = nr
                buf[pl.ds(r0, SUBLANES), o + QS:o + 2 * QS] = ni
                return nr, ni

            hr, hi = lax.fori_loop(0, nt, step, (hc[:, o:o + QS], hc[:, o + QS:o + 2 * QS]))
            hc[:, o:o + QS] = hr
            hc[:, o + QS:o + 2 * QS] = hi

        @pl.when(ph == 1)
        def _():
            for q in range(nq):
                o = q * 2 * QS
                cs = slice(q * QW, (q + 1) * QW)
                hq = buf[:, o:o + 2 * QS].astype(bf16)
                h_out[:, o:o + 2 * QS] = hq
                yq = _chunk_major(jnp.dot(hq, cmat_s[q], preferred_element_type=f32), nt) + d_ref[:, cs] * uu[:, :, cs]
                yraw_out[:, :, cs] = yq
                y_out[:, :, cs] = _gelu(yq).astype(bf16)

    blk = lambda ph, i: (0, i, 0)
    oblk = lambda ph, i: (0, i * ph, 0)
    act = lambda dt: jax.ShapeDtypeStruct((SUBLANES, chunk_len, sw), dt)
    (h_p, yraw3, y3), got = _call(
        body, (u.reshape(SUBLANES, chunk_len, sw), ar, ai, bre, bim, cre, cim, cfr, cfi, dvec), name=name, grid=(2, nb),
        in_specs=[pl.BlockSpec((SUBLANES, nt, sw), blk), _full(ar.shape), _full(ai.shape), _full(bre.shape), _full(bim.shape),
                  _full(cre.shape), _full(cim.shape), _full(cfr.shape), _full(cfi.shape), _full(dvec.shape)],
        out_specs=[pl.BlockSpec((tb, st), lambda ph, i: (i * ph, 0)), pl.BlockSpec((SUBLANES, nt, sw), oblk),
                   pl.BlockSpec((SUBLANES, nt, sw), oblk)],
        out_shape=[jax.ShapeDtypeStruct((s, st), bf16), act(f32), act(bf16)],
        scratch_shapes=[pltpu.VMEM((tb, st), f32), pltpu.VMEM((SUBLANES, st), f32),
                        pltpu.VMEM((nq, QW, 2 * QS), bf16), pltpu.VMEM((nq, 2 * QS, QW), bf16)],
        sem=("arbitrary", "arbitrary"), xchg=xchg)
    return h_p, yraw3, y3.reshape(s, sw), got


def _ssm_bwd(dy, yraw3, u, h_p, ar, ai, bre, bim, cre, cim, cfr, cfi, dvec, *, name, xchg=None):
    s, sw = u.shape
    nq = sw // QW
    st = nq * 2 * QS
    tb = _tile(s, S5_ROWS)
    nb, nt, chunk_len = s // tb, tb // SUBLANES, s // SUBLANES

    def body(dy_ref, yraw_ref, u_ref, h_ref, ar_ref, ai_ref, bre_ref, bim_ref, cre_ref, cim_ref, cfr_ref, cfi_ref, d_ref,
             du_out, dbre_out, dbim_out, dcre_out, dcim_out, dcfr_out, dcfi_out, dlbr_out, dlbi_out, dd_out, dbu_out,
             buf, rc, acc, dbbar, dcmat, bbar_s, cmat_s):
        ph, i = pl.program_id(0), pl.program_id(1)

        @pl.when(i == 0)
        def _():
            _ssm_mats(bre_ref, bim_ref, cre_ref, cim_ref, cfr_ref, cfi_ref, bbar_s, cmat_s, nq)

        @pl.when((ph == 0) & (i == 0))
        def _():
            rc[...] = jnp.zeros_like(rc)

        @pl.when((ph == 1) & (i == 0))
        def _():
            for q in range(nq):
                o = q * 2 * QS
                pr, pi_ = _chunk_power(ar_ref[q], ai_ref[q], chunk_len)
                sr, si = _chain_carries(rc[:, o:o + QS], rc[:, o + QS:o + 2 * QS], pr, -pi_, False)
                rc[:, o:o + QS] = sr
                rc[:, o + QS:o + 2 * QS] = si
            acc[...] = jnp.zeros_like(acc)
            dbbar[...] = jnp.zeros_like(dbbar)
            dcmat[...] = jnp.zeros_like(dcmat)
            dd_out[...] = jnp.zeros_like(dd_out)
            dbu_out[...] = jnp.zeros_like(dbu_out)

        dyraw = dy_ref[...].astype(f32) * _gelu_grad(yraw_ref[...])
        dyp = _step_major(dyraw).astype(bf16)
        for q in range(nq):
            o = q * 2 * QS
            buf[:, o:o + 2 * QS] = lax.dot_general(dyp[:, q * QW:(q + 1) * QW], cmat_s[q], (((1,), (1,)), ((), ())),
                                                   preferred_element_type=f32)

        def recur(with_grad):
            for q in range(nq):
                o = q * 2 * QS
                a_r = jnp.broadcast_to(ar_ref[q], (SUBLANES, QS))
                a_i = jnp.broadcast_to(ai_ref[q], (SUBLANES, QS))

                def step(j, carry, o=o, a_r=a_r, a_i=a_i):
                    r16 = pl.multiple_of((nt // 2 - 1 - j) * 2 * SUBLANES, 2 * SUBLANES)
                    if with_grad:
                        rr, ri, gr, gi = carry
                        h_re = h_ref[pl.ds(r16, 2 * SUBLANES), o:o + QS].astype(f32)
                        h_im = h_ref[pl.ds(r16, 2 * SUBLANES), o + QS:o + 2 * QS].astype(f32)
                    else:
                        rr, ri = carry
                    for half in (1, 0):
                        rows = pl.ds(pl.multiple_of(r16 + half * SUBLANES, SUBLANES), SUBLANES)
                        if with_grad:
                            hr = h_re[half * SUBLANES:(half + 1) * SUBLANES]
                            hi = h_im[half * SUBLANES:(half + 1) * SUBLANES]
                            gr = gr + hr * rr + hi * ri
                            gi = gi + hr * ri - hi * rr
                        nr = buf[rows, o:o + QS] + a_r * rr + a_i * ri
                        ni = buf[rows, o + QS:o + 2 * QS] + a_r * ri - a_i * rr
                        buf[rows, o:o + QS] = nr
                        buf[rows, o + QS:o + 2 * QS] = ni
                        rr, ri = nr, ni
                    return (rr, ri, gr, gi) if with_grad else (rr, ri)

                init = (rc[:, o:o + QS], rc[:, o + QS:o + 2 * QS])
                if with_grad:
                    init = init + (acc[:, o:o + QS], acc[:, o + QS:o + 2 * QS])
                res = lax.fori_loop(0, nt // 2, step, init)
                rc[:, o:o + QS] = res[0]
                rc[:, o + QS:o + 2 * QS] = res[1]
                if with_grad:
                    acc[:, o:o + QS] = res[2]
                    acc[:, o + QS:o + 2 * QS] = res[3]

        @pl.when(ph == 0)
        def _():
            recur(False)

        @pl.when(ph == 1)
        def _():
            recur(True)
            uu = u_ref[...]
            up = _step_major(uu).astype(bf16)
            dd_out[...] += _colsum((dyraw * uu).reshape(tb, sw))
            for q in range(nq):
                o = q * 2 * QS
                cs = slice(q * QW, (q + 1) * QW)
                lam = buf[:, o:o + 2 * QS].astype(bf16)
                duq = _chunk_major(lax.dot_general(lam, bbar_s[q], (((1,), (1,)), ((), ())), preferred_element_type=f32), nt) \
                    + d_ref[:, cs] * dyraw[:, :, cs]
                du_out[:, :, cs] = duq.astype(bf16)
                dbu_out[:, cs] += _colsum(duq.reshape(tb, QW))
                dbbar[q] += lax.dot_general(up[:, cs], lam, (((0,), (0,)), ((), ())), preferred_element_type=f32)
                dcmat[q] += lax.dot_general(dyp[:, cs], h_ref[:, o:o + 2 * QS], (((0,), (0,)), ((), ())),
                                            preferred_element_type=f32)

        @pl.when((ph == 1) & (i == nb - 1))
        def _():
            for q in range(nq):
                o = q * 2 * QS
                cr, ci, br, bi = cfr_ref[q], cfi_ref[q], bre_ref[q], bim_ref[q]
                gr, gi = dbbar[q, :, 0:QS], dbbar[q, :, QS:2 * QS]
                dbre_out[q] = cr * gr + ci * gi
                dbim_out[q] = cr * gi - ci * gr
                dcfr_out[q] = _colsum(gr * br + gi * bi)
                dcfi_out[q] = _colsum(gi * br - gr * bi)
                dcre_out[q] = dcmat[q, :, 0:QS].T
                dcim_out[q] = -dcmat[q, :, QS:2 * QS].T
                dlbr_out[q] = _colsum(acc[:, o:o + QS])
                dlbi_out[q] = _colsum(acc[:, o + QS:o + 2 * QS])

    blk = lambda ph, i: (0, nb - 1 - i, 0)
    oblk = lambda ph, i: (0, (nb - 1 - i) * ph + (nb - 1) * (1 - ph), 0)
    pshapes = [ar.shape, ai.shape, bre.shape, bim.shape, cre.shape, cim.shape, cfr.shape, cfi.shape, dvec.shape]
    oshapes = [bre.shape, bim.shape, cre.shape, cim.shape, cfr.shape, cfi.shape, ar.shape, ai.shape, dvec.shape, dvec.shape]
    act = pl.BlockSpec((SUBLANES, nt, sw), blk)
    view = lambda a: a.reshape(SUBLANES, chunk_len, sw)
    res, got = _call(
        body, (view(dy), yraw3, view(u), h_p, ar, ai, bre, bim, cre, cim, cfr, cfi, dvec), name=name, grid=(2, nb),
        in_specs=[act, act, act, pl.BlockSpec((tb, st), lambda ph, i: (nb - 1 - i, 0))] + [_full(p) for p in pshapes],
        out_specs=[pl.BlockSpec((SUBLANES, nt, sw), oblk)] + [_full(p) for p in oshapes],
        out_shape=[jax.ShapeDtypeStruct((SUBLANES, chunk_len, sw), bf16)] + [jax.ShapeDtypeStruct(p, f32) for p in oshapes],
        scratch_shapes=[pltpu.VMEM((tb, st), f32),
                        pltpu.VMEM((SUBLANES, st), f32), pltpu.VMEM((SUBLANES, st), f32),
                        pltpu.VMEM((nq, QW, 2 * QS), f32), pltpu.VMEM((nq, QW, 2 * QS), f32),
                        pltpu.VMEM((nq, QW, 2 * QS), bf16), pltpu.VMEM((nq, 2 * QS, QW), bf16)],
        sem=("arbitrary", "arbitrary"), xchg=xchg)
    return (res[0].reshape(s, sw),) + tuple(res[1:]) + (got,)


def _lnmod(x, sc, sh, *, name):
    s, d = x.shape
    tb = _tile(s, 512)

    def body(x_ref, sc_ref, sh_ref, o_ref):
        xh, _ = _ln(x_ref[...])
        o_ref[...] = (xh * (1.0 + sc_ref[...]) + sh_ref[...]).astype(bf16)

    blk = pl.BlockSpec((tb, d), lambda i: (i, 0))
    vec = pl.BlockSpec((1, d), _row)
    return pl.pallas_call(body, name=name, grid=(s // tb,), in_specs=[blk, vec, vec], out_specs=blk,
                          out_shape=jax.ShapeDtypeStruct((s, d), bf16), compiler_params=_params("parallel"))(x, sc, sh)


ROWS = 32


def _row_chunks(n_rows, rows, fn, init, start=0):
    return lax.fori_loop(start, n_rows // rows, lambda c, carry: fn(pl.multiple_of(c * rows, rows), carry), init)


def _rows_from(win, o, rows):
    if o % SUBLANES == 0:
        return win[o:o + rows]
    n = win.shape[0]
    return pltpu.roll(win, (n - o) % n, 0)[0:rows]


def _window_before(ref, halo, r0, rows, first, cols):
    if first:
        return jnp.concatenate([halo, ref[pl.ds(0, rows), cols]], axis=0)
    return ref[pl.ds(pl.multiple_of(r0 - SUBLANES, SUBLANES), rows + SUBLANES), cols]


def _taps3(win, w, off, rows):
    return _rows_from(win, off, rows) * w[0] + _rows_from(win, off + 1, rows) * w[1] + _rows_from(win, off + 2, rows) * w[2]


def _fold8(x):
    acc = x[0:SUBLANES]
    for r in range(1, x.shape[0] // SUBLANES):
        acc = acc + x[r * SUBLANES:(r + 1) * SUBLANES]
    return acc


def _conv_halo_specs(tb, cw, halo, s):
    per = tb // halo
    prev = pl.BlockSpec((halo, cw), lambda i: (jnp.maximum(i * per - 1, 0), 0))
    nxt = pl.BlockSpec((halo, cw), lambda i: (jnp.minimum((i + 1) * per, s // halo - 1), 0))
    return prev, nxt


WIDE_ROWS = 16


def _shift_groups(lo, hi):
    return [(b, [o for o in range(lo, hi + 1) if o % SUBLANES == b]) for b in range(SUBLANES)]


def _shifted(win, b):
    return win if b == 0 else _rows_from(win, b, win.shape[0] - SUBLANES)


def _conv31(win, w_ref, cols, rows, lo, hi, tap_of):
    acc = None
    for b, offs in _shift_groups(lo, hi):
        if offs:
            wb = _shifted(win, b)
            for o in offs:
                term = wb[o - b:o - b + rows] * w_ref[pl.ds(tap_of(o), 1), cols]
                acc = term if acc is None else acc + term
    return acc


def _gate_into(ext, a_ref, g_ref, ah_ref, gh_ref, tb, i):
    ext[pl.ds(0, CONV_HALO), :] = jnp.where(i > 0, ah_ref[...] * _sig(gh_ref[...]), 0.0)

    def chunk(r0, carry):
        ext[pl.ds(pl.multiple_of(r0 + CONV_HALO, SUBLANES), WIDE_ROWS), :] = \
            a_ref[pl.ds(r0, WIDE_ROWS), :] * _sig(g_ref[pl.ds(r0, WIDE_ROWS), :])
        return carry

    _row_chunks(tb, WIDE_ROWS, chunk, 0)


def _causal_conv_into(v2buf, ext, w_ref, b_ref, tb, cw):
    for ct in range(cw // LANES):
        cols = slice(ct * LANES, (ct + 1) * LANES)

        def chunk(r0, carry, cols=cols):
            win = ext[pl.ds(r0, ROWS + CONV_HALO), cols]
            v2buf[pl.ds(r0, ROWS), cols] = _conv31(win, w_ref, cols, ROWS, 2, CONV_K + 1, lambda o: o - 2) + b_ref[:, cols]
            return carry

        _row_chunks(tb, ROWS, chunk, 0)


def _silu_grad(x):
    sg = _sig(x)
    return sg * (1.0 + x * (1.0 - sg))


def _conv_fwd(cva, cvg, w, b, lng, lnb, *, name, xchg=None):
    s, cw = cva.shape
    tb = _tile(s, 256)
    prev, _ = _conv_halo_specs(tb, cw, CONV_HALO, s)

    def body(a_ref, g_ref, ah_ref, gh_ref, w_ref, b_ref, lng_ref, lnb_ref, o_ref, v2_ref, ext):
        _gate_into(ext, a_ref, g_ref, ah_ref, gh_ref, tb, pl.program_id(0))
        _causal_conv_into(v2_ref, ext, w_ref, b_ref, tb, cw)
        xh, _ = _ln(v2_ref[...])
        v3 = xh * lng_ref[...] + lnb_ref[...]
        o_ref[...] = (v3 * _sig(v3)).astype(bf16)

    blk = pl.BlockSpec((tb, cw), lambda i: (i, 0))
    vec = pl.BlockSpec((1, cw), _row)
    (v4, v2), got = _call(
        body, (cva, cvg, cva, cvg, w, b, lng, lnb), name=name, grid=(s // tb,),
        in_specs=[blk, blk, prev, prev, _full(w.shape), vec, vec, vec], out_specs=[blk, blk],
        out_shape=[jax.ShapeDtypeStruct((s, cw), bf16), jax.ShapeDtypeStruct((s, cw), f32)],
        scratch_shapes=[pltpu.VMEM((tb + CONV_HALO, cw), f32)], sem=("parallel",), xchg=xchg)
    return v4, v2, got


def _conv_bwd_ln(dv4, v2, lng, lnb, *, name):
    s, cw = v2.shape
    tb = _tile(s, 256)

    def body(d_ref, v2_ref, lng_ref, lnb_ref, o_ref, dg_ref, db_ref):
        @pl.when(pl.program_id(0) == 0)
        def _():
            dg_ref[...] = jnp.zeros_like(dg_ref)
            db_ref[...] = jnp.zeros_like(db_ref)

        xh, rstd = _ln(v2_ref[...])
        v3 = xh * lng_ref[...] + lnb_ref[...]
        dv3 = d_ref[...].astype(f32) * _silu_grad(v3)
        dg_ref[...] += _colsum(dv3 * xh)
        db_ref[...] += _colsum(dv3)
        o_ref[...] = _ln_bwd(dv3 * lng_ref[...], xh, rstd)

    blk = pl.BlockSpec((tb, cw), lambda i: (i, 0))
    vec = pl.BlockSpec((1, cw), _row)
    vshape = jax.ShapeDtypeStruct((1, cw), f32)
    return pl.pallas_call(
        body, name=name, grid=(s // tb,), in_specs=[blk, blk, vec, vec],
        out_specs=[blk, vec, vec], out_shape=[jax.ShapeDtypeStruct((s, cw), f32), vshape, vshape],
        compiler_params=_params("arbitrary"))(dv4, v2, lng, lnb)


def _conv_bwd_taps(dv2, cva, cvg, w, *, name, xchg=None):
    s, cw = cva.shape
    tb = _tile(s, 256)
    nb = s // tb
    prev, nxt = _conv_halo_specs(tb, cw, CONV_HALO, s)

    def body(d_ref, dn_ref, a_ref, g_ref, ah_ref, gh_ref, w_ref, da_ref, dg_ref, dw_ref, db_ref, sa_ref, sg_ref,
             ext, dext, dvbuf, tap_sums):
        i = pl.program_id(0)

        @pl.when(i == 0)
        def _():
            for r in (dw_ref, db_ref, sa_ref, sg_ref):
                r[...] = jnp.zeros_like(r)

        _gate_into(ext, a_ref, g_ref, ah_ref, gh_ref, tb, i)
        dext[pl.ds(tb, CONV_HALO), :] = jnp.where(i < nb - 1, dn_ref[...], 0.0)

        def copy(r0, carry):
            dext[pl.ds(r0, WIDE_ROWS), :] = d_ref[pl.ds(r0, WIDE_ROWS), :]
            return carry

        _row_chunks(tb, WIDE_ROWS, copy, 0)

        for ct in range(cw // LANES):
            cols = slice(ct * LANES, (ct + 1) * LANES)

            tap_sums[...] = jnp.zeros_like(tap_sums)

            def back(r0, carry, cols=cols):
                win = dext[pl.ds(r0, ROWS + CONV_HALO), cols]
                dvbuf[pl.ds(r0, ROWS), cols] = _conv31(win, w_ref, cols, ROWS, 0, CONV_K - 1, lambda o: CONV_K - 1 - o)
                win = ext[pl.ds(r0, ROWS + CONV_HALO), cols]
                dd = d_ref[pl.ds(r0, ROWS), cols]
                for b, offs in _shift_groups(2, CONV_K + 1):
                    wb = _shifted(win, b)
                    for o in offs:
                        tap_sums[o - 2] += _fold8(dd * wb[o - b:o - b + ROWS])
                return carry

            _row_chunks(tb, ROWS, back, 0)
            for k in range(CONV_K):
                dw_ref[pl.ds(k, 1), cols] += _colsum(tap_sums[k])

        def gate_back(r0, sums):
            rows = pl.ds(r0, WIDE_ROWS)
            aa, sg, dv = a_ref[rows, :], _sig(g_ref[rows, :]), dvbuf[rows, :]
            da = dv * sg
            dgate = dv * aa * sg * (1.0 - sg)
            da_ref[rows, :] = da.astype(bf16)
            dg_ref[rows, :] = dgate.astype(bf16)
            return sums[0] + _fold8(da), sums[1] + _fold8(dgate), sums[2] + _fold8(d_ref[rows, :])

        zero = jnp.zeros((SUBLANES, cw), f32)
        sums = _row_chunks(tb, WIDE_ROWS, gate_back, (zero, zero, zero))
        sa_ref[...] += _colsum(sums[0])
        sg_ref[...] += _colsum(sums[1])
        db_ref[...] += _colsum(sums[2])

    blk = pl.BlockSpec((tb, cw), lambda i: (i, 0))
    vec = pl.BlockSpec((1, cw), _row)
    vshape = jax.ShapeDtypeStruct((1, cw), f32)
    act = jax.ShapeDtypeStruct((s, cw), bf16)
    res, got = _call(
        body, (dv2, dv2, cva, cvg, cva, cvg, w), name=name, grid=(nb,), in_specs=[blk, nxt, blk, blk, prev, prev, _full(w.shape)],
        out_specs=[blk, blk, _full(w.shape), vec, vec, vec],
        out_shape=[act, act, jax.ShapeDtypeStruct(w.shape, f32), vshape, vshape, vshape],
        scratch_shapes=[pltpu.VMEM((tb + CONV_HALO, cw), f32), pltpu.VMEM((tb + CONV_HALO, cw), f32), pltpu.VMEM((tb, cw), f32),
                        pltpu.VMEM((CONV_HALO, SUBLANES, LANES), f32)],
        sem=("arbitrary",), xchg=xchg)
    return tuple(res) + (got,)


def _glu_merge(ya, yb, ycv, gs, gc, *, name):
    s, d = ya.shape
    tb = _tile(s, 512)

    def body(ya_ref, yb_ref, ycv_ref, gs_ref, gc_ref, o_ref):
        ld = lambda r: r[...].astype(f32)
        z = ld(ya_ref) * _sig(ld(yb_ref))
        o_ref[...] = (_sig(ld(gs_ref)) * z + _sig(ld(gc_ref)) * ld(ycv_ref)).astype(bf16)

    blk = pl.BlockSpec((tb, d), lambda i: (i, 0))
    return pl.pallas_call(body, name=name, grid=(s // tb,), in_specs=[blk] * 5, out_specs=blk,
                          out_shape=jax.ShapeDtypeStruct((s, d), bf16), compiler_params=_params("parallel"))(ya, yb, ycv, gs, gc)


def _glu_merge_bwd(dm, ya, yb, ycv, gs, gc, *, name):
    s, d = ya.shape
    tb = _tile(s, 512)

    def body(dm_ref, ya_ref, yb_ref, ycv_ref, gs_ref, gc_ref, dya_ref, dyb_ref, dycv_ref, dgs_ref, dgc_ref, sgs_ref, sgc_ref):
        @pl.when(pl.program_id(0) == 0)
        def _():
            sgs_ref[...] = jnp.zeros_like(sgs_ref)
            sgc_ref[...] = jnp.zeros_like(sgc_ref)

        ld = lambda r: r[...].astype(f32)
        dmv, yav = ld(dm_ref), ld(ya_ref)
        sb, ss, scv = _sig(ld(yb_ref)), _sig(ld(gs_ref)), _sig(ld(gc_ref))
        z = yav * sb
        dz = dmv * ss
        dgs = dmv * z * ss * (1.0 - ss)
        dgc = dmv * ld(ycv_ref) * scv * (1.0 - scv)
        dya_ref[...] = (dz * sb).astype(bf16)
        dyb_ref[...] = (dz * yav * sb * (1.0 - sb)).astype(bf16)
        dycv_ref[...] = (dmv * scv).astype(bf16)
        dgs_ref[...] = dgs.astype(bf16)
        dgc_ref[...] = dgc.astype(bf16)
        sgs_ref[...] += _colsum(dgs)
        sgc_ref[...] += _colsum(dgc)

    blk = pl.BlockSpec((tb, d), lambda i: (i, 0))
    vec = pl.BlockSpec((1, d), _row)
    act = jax.ShapeDtypeStruct((s, d), bf16)
    vshape = jax.ShapeDtypeStruct((1, d), f32)
    return pl.pallas_call(body, name=name, grid=(s // tb,), in_specs=[blk] * 6, out_specs=[blk] * 5 + [vec, vec],
                          out_shape=[act] * 5 + [vshape, vshape], compiler_params=_params("arbitrary"))(dm, ya, yb, ycv, gs, gc)


def _resid_ln_mod(x, o, g, lng, lnb, sc, sh, alpha, *, name):
    s, d = x.shape
    tb = _tile(s, 512)

    def body(x_ref, o_ref, g_ref, lng_ref, lnb_ref, sc_ref, sh_ref, x1_ref, h_ref):
        xh, _ = _ln(alpha * x_ref[...] + g_ref[...] * o_ref[...].astype(f32))
        x1 = xh * lng_ref[...] + lnb_ref[...]
        x1_ref[...] = x1
        xh1, _ = _ln(x1)
        h_ref[...] = (xh1 * (1.0 + sc_ref[...]) + sh_ref[...]).astype(bf16)

    blk = pl.BlockSpec((tb, d), lambda i: (i, 0))
    vec = pl.BlockSpec((1, d), _row)
    return pl.pallas_call(body, name=name, grid=(s // tb,), in_specs=[blk, blk] + [vec] * 5, out_specs=[blk, blk],
                          out_shape=[jax.ShapeDtypeStruct((s, d), f32), jax.ShapeDtypeStruct((s, d), bf16)],
                          compiler_params=_params("parallel"))(x, o, g, lng, lnb, sc, sh)


def _resid_ln_loss(x1, f, w_dn, g, lng, lnb, tgt, alpha, *, name):
    s, d = x1.shape
    tb = _tile(s, 512)

    def body(x1_ref, f_ref, w_ref, g_ref, lng_ref, lnb_ref, t_ref, dr_ref, dy_ref, loss_ref, dlg_ref, dlb_ref, dg_ref):
        @pl.when(pl.program_id(0) == 0)
        def _():
            for r in (loss_ref, dlg_ref, dlb_ref, dg_ref):
                r[...] = jnp.zeros_like(r)

        yv = jnp.dot(f_ref[...], w_ref[...], preferred_element_type=f32)
        xh, rstd = _ln(alpha * x1_ref[...] + g_ref[...] * yv)
        err = xh * lng_ref[...] + lnb_ref[...] - t_ref[...]
        loss_ref[...] += 0.5 * jnp.sum(jnp.sum(err * err, axis=-1, keepdims=True) / d, axis=0, keepdims=True)
        dx2 = err / d
        dlg_ref[...] += _colsum(dx2 * xh)
        dlb_ref[...] += _colsum(dx2)
        dr = _ln_bwd(dx2 * lng_ref[...], xh, rstd)
        dg_ref[...] += _colsum(dr * yv)
        dr_ref[...] = dr
        dy_ref[...] = (g_ref[...] * dr).astype(bf16)

    blk = pl.BlockSpec((tb, d), lambda i: (i, 0))
    vec = pl.BlockSpec((1, d), _row)
    vshape = jax.ShapeDtypeStruct((1, d), f32)
    return pl.pallas_call(
        body, name=name, grid=(s // tb,),
        in_specs=[blk, pl.BlockSpec((tb, f.shape[1]), lambda i: (i, 0)), _full(w_dn.shape), vec, vec, vec, blk],
        out_specs=[blk, blk, pl.BlockSpec((1, 1), _row), vec, vec, vec],
        out_shape=[jax.ShapeDtypeStruct((s, d), f32), jax.ShapeDtypeStruct((s, d), bf16),
                   jax.ShapeDtypeStruct((1, 1), f32), vshape, vshape, vshape],
        compiler_params=_params("arbitrary"))(x1, f, w_dn, g, lng, lnb, tgt)


def _mid_bwd(dh2, x1, dr2, x, o, g, sc, lng, alpha, *, name):
    s, d = x.shape
    tb = _tile(s, 512)

    def body(dh_ref, x1_ref, dr2_ref, x_ref, o_ref, g_ref, sc_ref, lng_ref,
             dr1_ref, do_ref, dsc_ref, dsh_ref, dlg_ref, dlb_ref, dg_ref):
        @pl.when(pl.program_id(0) == 0)
        def _():
            for r in (dsc_ref, dsh_ref, dlg_ref, dlb_ref, dg_ref):
                r[...] = jnp.zeros_like(r)

        dh = dh_ref[...].astype(f32)
        xh1, rstd1 = _ln(x1_ref[...])
        dsc_ref[...] += _colsum(dh * xh1)
        dsh_ref[...] += _colsum(dh)
        dx1 = alpha * dr2_ref[...] + _ln_bwd(dh * (1.0 + sc_ref[...]), xh1, rstd1)
        ov = o_ref[...].astype(f32)
        xhr, rstdr = _ln(alpha * x_ref[...] + g_ref[...] * ov)
        dlg_ref[...] += _colsum(dx1 * xhr)
        dlb_ref[...] += _colsum(dx1)
        dr1 = _ln_bwd(dx1 * lng_ref[...], xhr, rstdr)
        dg_ref[...] += _colsum(dr1 * ov)
        dr1_ref[...] = dr1
        do_ref[...] = (g_ref[...] * dr1).astype(bf16)

    blk = pl.BlockSpec((tb, d), lambda i: (i, 0))
    vec = pl.BlockSpec((1, d), _row)
    vshape = jax.ShapeDtypeStruct((1, d), f32)
    return pl.pallas_call(
        body, name=name, grid=(s // tb,), in_specs=[blk] * 5 + [vec] * 3, out_specs=[blk, blk] + [vec] * 5,
        out_shape=[jax.ShapeDtypeStruct((s, d), f32), jax.ShapeDtypeStruct((s, d), bf16)] + [vshape] * 5,
        compiler_params=_params("arbitrary"))(dh2, x1, dr2, x, o, g, sc, lng)


def _final_bwd(dh1, x, dr1, sc, alpha, *, name, xchg=None):
    s, d = x.shape
    tb = _tile(s, 512)

    def body(dh_ref, x_ref, dr1_ref, sc_ref, dx_ref, dsc_ref, dsh_ref):
        @pl.when(pl.program_id(0) == 0)
        def _():
            dsc_ref[...] = jnp.zeros_like(dsc_ref)
            dsh_ref[...] = jnp.zeros_like(dsh_ref)

        dh = dh_ref[...].astype(f32)
        xh, rstd = _ln(x_ref[...])
        dsc_ref[...] += _colsum(dh * xh)
        dsh_ref[...] += _colsum(dh)
        dx_ref[...] = alpha * dr1_ref[...] + _ln_bwd(dh * (1.0 + sc_ref[...]), xh, rstd)

    blk = pl.BlockSpec((tb, d), lambda i: (i, 0))
    vec = pl.BlockSpec((1, d), _row)
    vshape = jax.ShapeDtypeStruct((1, d), f32)
    res, got = _call(body, (dh1, x, dr1, sc), name=name, grid=(s // tb,), in_specs=[blk, blk, blk, vec], out_specs=[blk, vec, vec],
                     out_shape=[jax.ShapeDtypeStruct((s, d), f32), vshape, vshape], sem=("arbitrary",), xchg=xchg)
    return tuple(res) + (got,)


TALL_ROWS = 64


def _ffn_col_tile(fh):
    return fh // 2 if (fh // 2) % LANES == 0 else fh


def _ffn_specs(s, fh, tb, tc):
    per = tb // FFN_HALO
    blk = pl.BlockSpec((tb, tc), lambda j, i: (i, j))
    prev = pl.BlockSpec((FFN_HALO, tc), lambda j, i: (jnp.maximum(i * per - 1, 0), j))
    nxt = pl.BlockSpec((FFN_HALO, tc), lambda j, i: (jnp.minimum((i + 1) * per, s // FFN_HALO - 1), j))
    taps = pl.BlockSpec((FFN_HALO, tc), lambda j, i: (0, j))
    vec = pl.BlockSpec((1, tc), lambda j, i: (0, j))
    return blk, prev, nxt, taps, vec


def _ffn_mid(upa, upv, wa, wv, ba, bv, *, name, xchg=None):
    s, fh = upa.shape
    tb, tc = _tile(s, 512), _ffn_col_tile(fh)
    blk, prev, _, taps, vec = _ffn_specs(s, fh, tb, tc)
    off = FFN_HALO - FFN_K + 1

    def body(a_ref, v_ref, ah_ref, vh_ref, wa_ref, wv_ref, ba_ref, bv_ref, o_ref):
        first = pl.program_id(1) == 0
        for lt in range(tc // LANES):
            cols = slice(lt * LANES, (lt + 1) * LANES)
            halo_a, halo_v = jnp.where(first, 0.0, ah_ref[:, cols]), jnp.where(first, 0.0, vh_ref[:, cols])
            wa = [wa_ref[pl.ds(k, 1), cols] for k in range(FFN_K)]
            wv = [wv_ref[pl.ds(k, 1), cols] for k in range(FFN_K)]
            ba, bv = ba_ref[:, cols], bv_ref[:, cols]

            def chunk(r0, carry, head=False, cols=cols, halo_a=halo_a, halo_v=halo_v, wa=wa, wv=wv, ba=ba, bv=bv):
                a2 = _taps3(_window_before(a_ref, halo_a, r0, TALL_ROWS, head, cols), wa, off, TALL_ROWS) + ba
                v2 = _taps3(_window_before(v_ref, halo_v, r0, TALL_ROWS, head, cols), wv, off, TALL_ROWS) + bv
                o_ref[pl.ds(r0, TALL_ROWS), cols] = (_gelu(a2) * v2).astype(bf16)
                return carry

            chunk(0, 0, head=True)
            _row_chunks(tb, TALL_ROWS, chunk, 0, start=1)

    (f,), got = _call(
        body, (upa, upv, upa, upv, wa, wv, ba, bv), name=name, grid=(fh // tc, s // tb),
        in_specs=[blk, blk, prev, prev, taps, taps, vec, vec], out_specs=[blk], out_shape=[jax.ShapeDtypeStruct((s, fh), bf16)],
        sem=("parallel", "arbitrary"), xchg=xchg)
    return f, got


def _ffn_mid_bwd_tile(cols, first, last, tb, off, df_ref, dfn_ref, a_ref, v_ref, ah_ref, vh_ref, an_ref, vn_ref, wa_ref, wv_ref,
                      ba_ref, bv_ref, da_ref, dv_ref, dwa_ref, dwv_ref, dba_ref, dbv_ref, dexta, dextv):
    rows_c = TALL_ROWS
    halo_a, halo_v = jnp.where(first, 0.0, ah_ref[:, cols]), jnp.where(first, 0.0, vh_ref[:, cols])
    wa = [wa_ref[pl.ds(k, 1), cols] for k in range(FFN_K)]
    wv = [wv_ref[pl.ds(k, 1), cols] for k in range(FFN_K)]
    ba, bv = ba_ref[:, cols], bv_ref[:, cols]

    def conv_cotangents(r0, rows, xa, xv, dfe):
        sa = [_rows_from(xa, off + k, rows) for k in range(FFN_K)]
        sv = [_rows_from(xv, off + k, rows) for k in range(FFN_K)]
        a2 = sa[0] * wa[0] + sa[1] * wa[1] + sa[2] * wa[2] + ba
        v2 = sv[0] * wv[0] + sv[1] * wv[1] + sv[2] * wv[2] + bv
        cdf = 0.5 * (1.0 + lax.erf(a2 * INV_SQRT2))
        da2 = dfe * v2 * (cdf + a2 * jnp.exp(-0.5 * a2 * a2) * INV_SQRT_2PI)
        dv2 = dfe * (a2 * cdf)
        dexta[pl.ds(r0, rows), cols] = da2
        dextv[pl.ds(r0, rows), cols] = dv2
        return da2, dv2, sa, sv

    def chunk(r0, sums, head=False):
        da2, dv2, sa, sv = conv_cotangents(r0, rows_c, _window_before(a_ref, halo_a, r0, rows_c, head, cols),
                                           _window_before(v_ref, halo_v, r0, rows_c, head, cols), df_ref[pl.ds(r0, rows_c), cols])
        new = [sums[k] + _fold8(da2 * sa[k]) for k in range(FFN_K)] + [sums[FFN_K] + _fold8(da2)]
        new += [sums[FFN_K + 1 + k] + _fold8(dv2 * sv[k]) for k in range(FFN_K)] + [sums[2 * FFN_K + 1] + _fold8(dv2)]
        return tuple(new)

    sums = chunk(0, tuple(jnp.zeros((SUBLANES, LANES), f32) for _ in range(2 * FFN_K + 2)), head=True)
    sums = _row_chunks(tb, rows_c, chunk, sums, start=1)
    conv_cotangents(tb, FFN_HALO,
                    jnp.concatenate([a_ref[pl.ds(tb - FFN_HALO, FFN_HALO), cols], jnp.where(last, 0.0, an_ref[:, cols])], axis=0),
                    jnp.concatenate([v_ref[pl.ds(tb - FFN_HALO, FFN_HALO), cols], jnp.where(last, 0.0, vn_ref[:, cols])], axis=0),
                    jnp.where(last, 0.0, dfn_ref[:, cols]))
    for k in range(FFN_K):
        dwa_ref[pl.ds(k, 1), cols] += _colsum(sums[k])
        dwv_ref[pl.ds(k, 1), cols] += _colsum(sums[FFN_K + 1 + k])
    dba_ref[:, cols] += _colsum(sums[FFN_K])
    dbv_ref[:, cols] += _colsum(sums[2 * FFN_K + 1])

    def back(r0, carry):
        for dext, w, o_ref in ((dexta, wa, da_ref), (dextv, wv, dv_ref)):
            dd = dext[pl.ds(r0, rows_c + FFN_HALO), cols]
            o_ref[pl.ds(r0, rows_c), cols] = (_rows_from(dd, 2, rows_c) * w[0] + _rows_from(dd, 1, rows_c) * w[1]
                                              + dd[0:rows_c] * w[2]).astype(bf16)
        return carry

    _row_chunks(tb, rows_c, back, 0)


def _ffn_mid_bwd(df, upa, upv, wa, wv, ba, bv, *, name, xchg=None):
    s, fh = upa.shape
    tb, tc = _tile(s, 512), _ffn_col_tile(fh)
    nb = s // tb
    blk, prev, nxt, taps, vec = _ffn_specs(s, fh, tb, tc)
    off = FFN_HALO - FFN_K + 1
    te = tb + FFN_HALO

    def body(df_ref, dfn_ref, a_ref, v_ref, ah_ref, vh_ref, an_ref, vn_ref, wa_ref, wv_ref, ba_ref, bv_ref,
             da_ref, dv_ref, dwa_ref, dwv_ref, dba_ref, dbv_ref, dexta, dextv):
        i = pl.program_id(1)

        @pl.when(i == 0)
        def _():
            for r in (dwa_ref, dwv_ref, dba_ref, dbv_ref):
                r[...] = jnp.zeros_like(r)

        last = i == nb - 1
        for lt in range(tc // LANES):
            _ffn_mid_bwd_tile(slice(lt * LANES, (lt + 1) * LANES), i == 0, last, tb, off, df_ref, dfn_ref, a_ref, v_ref,
                              ah_ref, vh_ref, an_ref, vn_ref, wa_ref, wv_ref, ba_ref, bv_ref, da_ref, dv_ref, dwa_ref, dwv_ref,
                              dba_ref, dbv_ref, dexta, dextv)

    act = jax.ShapeDtypeStruct((s, fh), bf16)
    wshape = jax.ShapeDtypeStruct((FFN_HALO, fh), f32)
    vshape = jax.ShapeDtypeStruct((1, fh), f32)
    res, got = _call(
        body, (df, df, upa, upv, upa, upv, upa, upv, wa, wv, ba, bv), name=name, grid=(fh // tc, nb),
        in_specs=[blk, nxt, blk, blk, prev, prev, nxt, nxt, taps, taps, vec, vec],
        out_specs=[blk, blk, taps, taps, vec, vec], out_shape=[act, act, wshape, wshape, vshape, vshape],
        scratch_shapes=[pltpu.VMEM((te, tc), f32)] * 2, sem=("parallel", "arbitrary"), xchg=xchg)
    return tuple(res) + (got,)


def _cols_from_shards(stacked):
    _, k, n = stacked.shape
    return stacked.transpose(1, 0, 2).reshape(k, NDEV * n)


def _pad_rows(w, rows):
    return jnp.pad(w, ((0, rows - w.shape[0]), (0, 0)))


def kernel(x, c, w_cond, b_cond, w_in, b_in, ssm_lambda_re, ssm_lambda_im, ssm_log_dt, ssm_b_re, ssm_b_im, ssm_c_re, ssm_c_im, ssm_d, ssm_glu_w_a, ssm_glu_w_b, cv_dw_w, cv_dw_b, cv_ln_g, cv_ln_b, cv_w_pw, w_out, ln1_g, ln1_b, ffn_w_up, ffn_dw_w, ffn_dw_b, ffn_w_down, ln2_g, ln2_b, loss_target, m_w_cond, m_b_cond, m_w_in, m_b_in, m_ssm_lambda_re, m_ssm_lambda_im, m_ssm_log_dt, m_ssm_b_re, m_ssm_b_im, m_ssm_c_re, m_ssm_c_im, m_ssm_d, m_ssm_glu_w_a, m_ssm_glu_w_b, m_cv_dw_w, m_cv_dw_b, m_cv_ln_g, m_cv_ln_b, m_cv_w_pw, m_w_out, m_ln1_g, m_ln1_b, m_ffn_w_up, m_ffn_dw_w, m_ffn_dw_b, m_ffn_w_down, m_ln2_g, m_ln2_b, v_w_cond, v_b_cond, v_w_in, v_b_in, v_ssm_lambda_re, v_ssm_lambda_im, v_ssm_log_dt, v_ssm_b_re, v_ssm_b_im, v_ssm_c_re, v_ssm_c_im, v_ssm_d, v_ssm_glu_w_a, v_ssm_glu_w_b, v_cv_dw_w, v_cv_dw_b, v_cv_ln_g, v_cv_ln_b, v_cv_w_pw, v_w_out, v_ln1_g, v_ln1_b, v_ffn_w_up, v_ffn_dw_w, v_ffn_dw_b, v_ffn_w_down, v_ln2_g, v_ln2_b):
    weights = dict(w_cond=w_cond, b_cond=b_cond, w_in=w_in, b_in=b_in, ssm_lambda_re=ssm_lambda_re, ssm_lambda_im=ssm_lambda_im, ssm_log_dt=ssm_log_dt, ssm_b_re=ssm_b_re, ssm_b_im=ssm_b_im, ssm_c_re=ssm_c_re, ssm_c_im=ssm_c_im, ssm_d=ssm_d, ssm_glu_w_a=ssm_glu_w_a, ssm_glu_w_b=ssm_glu_w_b, cv_dw_w=cv_dw_w, cv_dw_b=cv_dw_b, cv_ln_g=cv_ln_g, cv_ln_b=cv_ln_b, cv_w_pw=cv_w_pw, w_out=w_out, ln1_g=ln1_g, ln1_b=ln1_b, ffn_w_up=ffn_w_up, ffn_dw_w=ffn_dw_w, ffn_dw_b=ffn_dw_b, ffn_w_down=ffn_w_down, ln2_g=ln2_g, ln2_b=ln2_b)
    mom_m = dict(w_cond=m_w_cond, b_cond=m_b_cond, w_in=m_w_in, b_in=m_b_in, ssm_lambda_re=m_ssm_lambda_re, ssm_lambda_im=m_ssm_lambda_im, ssm_log_dt=m_ssm_log_dt, ssm_b_re=m_ssm_b_re, ssm_b_im=m_ssm_b_im, ssm_c_re=m_ssm_c_re, ssm_c_im=m_ssm_c_im, ssm_d=m_ssm_d, ssm_glu_w_a=m_ssm_glu_w_a, ssm_glu_w_b=m_ssm_glu_w_b, cv_dw_w=m_cv_dw_w, cv_dw_b=m_cv_dw_b, cv_ln_g=m_cv_ln_g, cv_ln_b=m_cv_ln_b, cv_w_pw=m_cv_w_pw, w_out=m_w_out, ln1_g=m_ln1_g, ln1_b=m_ln1_b, ffn_w_up=m_ffn_w_up, ffn_dw_w=m_ffn_dw_w, ffn_dw_b=m_ffn_dw_b, ffn_w_down=m_ffn_w_down, ln2_g=m_ln2_g, ln2_b=m_ln2_b)
    mom_v = dict(w_cond=v_w_cond, b_cond=v_b_cond, w_in=v_w_in, b_in=v_b_in, ssm_lambda_re=v_ssm_lambda_re, ssm_lambda_im=v_ssm_lambda_im, ssm_log_dt=v_ssm_log_dt, ssm_b_re=v_ssm_b_re, ssm_b_im=v_ssm_b_im, ssm_c_re=v_ssm_c_re, ssm_c_im=v_ssm_c_im, ssm_d=v_ssm_d, ssm_glu_w_a=v_ssm_glu_w_a, ssm_glu_w_b=v_ssm_glu_w_b, cv_dw_w=v_cv_dw_w, cv_dw_b=v_cv_dw_b, cv_ln_g=v_cv_ln_g, cv_ln_b=v_cv_ln_b, cv_w_pw=v_cv_w_pw, w_out=v_w_out, ln1_g=v_ln1_g, ln1_b=v_ln1_b, ffn_w_up=v_ffn_w_up, ffn_dw_w=v_ffn_dw_w, ffn_dw_b=v_ffn_dw_b, ffn_w_down=v_ffn_w_down, ln2_g=v_ln2_g, ln2_b=v_ln2_b)
    names = list(weights)

    s, d = x.shape[1], x.shape[2]
    sw = cw = d // 2
    fh = ffn_w_down.shape[1] * NDEV
    ng, nq = sw // SSM_GROUP, sw // QW
    gq = ng // nq
    alpha = 2.0 ** 0.25
    me = 4 * lax.axis_index("x") + 2 * lax.axis_index("y") + lax.axis_index("c")
    xs, tgt = x[0], loss_target[0]

    col_names = ["w_in", "ssm_glu_w_a", "ssm_glu_w_b", "cv_w_pw", "ffn_w_up"]
    row_names = ["w_out", "ffn_w_down"]
    big = col_names + row_names
    sent = lambda ns: [weights[n][0].astype(bf16) for n in ns]
    got_in, got_c, got_cv_taps, got_ffn_taps = _exchange(sent(["w_in"]) + [c, cv_dw_w[0, :, 0], ffn_dw_w[0, :, 0]],
                                                         scatter=False, name="gather_in")
    o1, o2, o3, o4 = sw, sw + cw, sw + 2 * cw, sw + 2 * cw + d
    in_bounds = ((0, o1), (o1, o2), (o2, o3), (o3, o4), (o4, o4 + d))
    w_u, w_cva, w_cvg, w_gs, w_gc = _unshard_cols(got_in, in_bounds, name="unshard_w_in")
    b_u, b_cva, b_cvg, b_gs, b_gc = (b_in[:, a:b] for a, b in in_bounds)
    c_all = got_c.reshape(NDEV, d)
    cv_taps = _cols_from_shards(got_cv_taps)
    ffn_taps = _cols_from_shards(got_ffn_taps)
    cv_w32 = _pad_rows(cv_taps, CONV_HALO)
    ffn_wa, ffn_wv = _pad_rows(ffn_taps[:, :fh], FFN_HALO), _pad_rows(ffn_taps[:, fh:], FFN_HALO)
    ffn_ba, ffn_bv = ffn_dw_b[:, :fh], ffn_dw_b[:, fh:]

    ncond = w_cond.shape[2]
    b_cond_mine = lax.dynamic_slice(b_cond, (0, me * ncond), (1, ncond))
    mod_cols = _cond_fwd(c_all, w_cond[0], b_cond_mine, name="cond_fwd")
    mod_all, = _exchange([mod_cols], scatter=False, name="gather_mod")
    mod_mine = lax.dynamic_slice(mod_all, (0, me, 0), (NDEV, 1, ncond)).reshape(1, 6 * d)
    sh1, sc1, g1, sh2, sc2, g2 = (mod_mine[:, k * d:(k + 1) * d] for k in range(6))

    lam_re, lam_im, log_dt = ssm_lambda_re[0], ssm_lambda_im[0], ssm_log_dt[0][:, None]
    lbr, lbi, cfr, cfi = _ssm_prep(lam_re, lam_im, log_dt, name="ssm_prep")
    rows_q = lambda a: a.reshape(nq, 1, QS)
    eye = jnp.eye(gq, dtype=f32)

    def b_mat(b):
        bt = b.reshape(nq, gq, SSM_STATE, SSM_GROUP).transpose(0, 1, 3, 2)
        return jnp.einsum("qgpn,gh->qgphn", bt, eye).reshape(nq, QW, QS)

    def c_mat(cc):
        ct = cc.reshape(nq, gq, SSM_GROUP, SSM_STATE)
        return jnp.einsum("qgpn,gh->qhngp", ct, eye).reshape(nq, QS, QW)

    def b_unmat(mt):
        return jnp.einsum("qgpgn->qgnp", mt.reshape(nq, gq, SSM_GROUP, gq, SSM_STATE)).reshape(ng, SSM_STATE, SSM_GROUP)

    def c_unmat(mt):
        return jnp.einsum("qgngp->qgpn", mt.reshape(nq, gq, SSM_STATE, gq, SSM_GROUP)).reshape(ng, SSM_GROUP, SSM_STATE)

    ssm_args = (rows_q(lbr), rows_q(lbi), b_mat(ssm_b_re[0]), b_mat(ssm_b_im[0]), c_mat(ssm_c_re[0]), c_mat(ssm_c_im[0]),
                rows_q(cfr), rows_q(cfi), ssm_d[0].reshape(1, sw))

    h1 = _lnmod(xs, sc1, sh1, name="ln_mod1")
    u, cva, cvg, gs, gc = _mm_fanout(h1, [w_u, w_cva, w_cvg, w_gs, w_gc], [b_u, b_cva, b_cvg, b_gs, b_gc],
                                     [f32, f32, f32, bf16, bf16], name="in_proj")
    v4, cv2, (got_a, got_b, got_pw, got_o) = _conv_fwd(
        cva, cvg, cv_w32, cv_dw_b, cv_ln_g, cv_ln_b, name="conv_fwd",
        xchg=(sent(["ssm_glu_w_a", "ssm_glu_w_b", "cv_w_pw", "w_out"]), False))
    h_p, yraw3, y, (got_up,) = _ssm_fwd(u, *ssm_args, name="ssm_fwd", xchg=(sent(["ffn_w_up"]), False))
    w_a, = _unshard_cols(got_a, ((0, d),), name="unshard_glu_a")
    w_b, = _unshard_cols(got_b, ((0, d),), name="unshard_glu_b")
    w_pw, = _unshard_cols(got_pw, ((0, d),), name="unshard_conv_pw")
    w_upa, w_upv = _unshard_cols(got_up, ((0, fh), (fh, 2 * fh)), name="unshard_ffn_up")
    w_o = got_o.reshape(d, d)
    ya, yb = _mm_fanout(y, [w_a, w_b], None, [bf16, bf16], name="glu")
    ycv = _mm([(v4, w_pw)], out_dtype=bf16, name="conv_pw")
    merged = _glu_merge(ya, yb, ycv, gs, gc, name="merge")
    o = _mm([(merged, w_o)], out_dtype=bf16, name="out_proj")
    x1, h2 = _resid_ln_mod(xs, o, g1, ln1_g, ln1_b, sc2, sh2, alpha, name="resid_ln1")
    upa = _mm([(h2, w_upa)], name="ffn_up_a")
    upv = _mm([(h2, w_upv)], name="ffn_up_v")
    f, (got_dn,) = _ffn_mid(upa, upv, ffn_wa, ffn_wv, ffn_ba, ffn_bv, name="ffn_mid", xchg=(sent(["ffn_w_down"]), False))
    w_dn = got_dn.reshape(fh, d)
    dr2, dy2, loss_part, d_ln2_g, d_ln2_b, d_g2 = _resid_ln_loss(x1, f, w_dn, g2, ln2_g, ln2_b, tgt, alpha, name="ffn_down_ln2_loss")

    gw = {}
    df = _mm([(dy2, w_dn)], trans_w=True, name="d_ffn_down")
    gw["ffn_w_down"] = _mm_tn(f, dy2, out_dtype=bf16, name="g_ffn_down").reshape((NDEV,) + ffn_w_down[0].shape)
    received = {}
    dupa, dupv, d_ffn_wa, d_ffn_wv, d_ffn_ba, d_ffn_bv, (received["ffn_w_down"],) = _ffn_mid_bwd(
        df, upa, upv, ffn_wa, ffn_wv, ffn_ba, ffn_bv, name="ffn_mid_bwd", xchg=([gw["ffn_w_down"]], True))
    dh2 = _mm([(dupa, w_upa), (dupv, w_upv)], trans_w=True, out_dtype=bf16, name="d_ffn_up")
    gw["ffn_w_up"] = _shard_cols([_mm_tn(h2, dupa, name="g_ffn_up_a"), _mm_tn(h2, dupv, name="g_ffn_up_v")], out_dtype=bf16,
                                 name="shard_ffn_up")
    dr1, do, d_sc2, d_sh2, d_ln1_g, d_ln1_b, d_g1 = _mid_bwd(dh2, x1, dr2, xs, o, g1, sc2, ln1_g, alpha, name="mid_bwd")
    dmerged = _mm([(do, w_o)], trans_w=True, out_dtype=bf16, name="d_out_proj")
    gw["w_out"] = _mm_tn(merged, do, out_dtype=bf16, name="g_out_proj").reshape((NDEV,) + w_out[0].shape)
    dya, dyb, dycv, dgs, dgc, s_gs, s_gc = _glu_merge_bwd(dmerged, ya, yb, ycv, gs, gc, name="merge_bwd")
    dy = _mm([(dya, w_a), (dyb, w_b)], trans_w=True, out_dtype=bf16, name="d_glu")
    gw["ssm_glu_w_a"] = _mm_tn_sharded(y, [dya], out_dtype=bf16, name="g_glu_a")
    gw["ssm_glu_w_b"] = _mm_tn_sharded(y, [dyb], out_dtype=bf16, name="g_glu_b")
    dv4 = _mm([(dycv, w_pw)], trans_w=True, out_dtype=bf16, name="d_conv_pw")
    gw["cv_w_pw"] = _mm_tn_sharded(v4, [dycv], out_dtype=bf16, name="g_conv_pw")
    dv2, d_cv_ln_g, d_cv_ln_b = _conv_bwd_ln(dv4, cv2, cv_ln_g, cv_ln_b, name="conv_bwd_ln")
    dcva, dcvg, d_cv_w32, d_cv_b, s_cva, s_cvg, (received["ffn_w_up"],) = _conv_bwd_taps(
        dv2, cva, cvg, cv_w32, name="conv_bwd_taps", xchg=([gw["ffn_w_up"]], True))
    late = ["w_out", "ssm_glu_w_a", "ssm_glu_w_b", "cv_w_pw"]
    (du, d_bre_m, d_bim_m, d_cre_m, d_cim_m, d_cfr, d_cfi, d_lbr, d_lbi, d_d, s_u, got_late) = _ssm_bwd(
        dy, yraw3, u, h_p, *ssm_args, name="ssm_bwd", xchg=([gw[n] for n in late], True))
    received.update(zip(late, got_late))
    gshape = lam_re.shape
    d_lam_re, d_lam_im, d_log_dt = _ssm_prep_bwd(
        lam_re, lam_im, log_dt, [a.reshape(gshape) for a in (d_lbr, d_lbi, d_cfr, d_cfi)], name="ssm_prep_bwd")
    small = {
        "b_in": jnp.concatenate([s_u, s_cva, s_cvg, s_gs, s_gc], axis=1),
        "ssm_lambda_re": d_lam_re, "ssm_lambda_im": d_lam_im, "ssm_log_dt": d_log_dt,
        "ssm_b_re": b_unmat(d_bre_m), "ssm_b_im": b_unmat(d_bim_m), "ssm_c_re": c_unmat(d_cre_m), "ssm_c_im": c_unmat(d_cim_m),
        "ssm_d": d_d, "cv_dw_w": d_cv_w32[:CONV_K], "cv_dw_b": d_cv_b, "cv_ln_g": d_cv_ln_g, "cv_ln_b": d_cv_ln_b,
        "ln1_g": d_ln1_g, "ln1_b": d_ln1_b,
        "ffn_dw_w": jnp.concatenate([d_ffn_wa[:FFN_K], d_ffn_wv[:FFN_K]], axis=1),
        "ffn_dw_b": jnp.concatenate([d_ffn_ba, d_ffn_bv], axis=1), "ln2_g": d_ln2_g, "ln2_b": d_ln2_b,
        "mod_g1": d_g1, "mod_sh2": d_sh2, "mod_sc2": d_sc2, "mod_g2": d_g2, "loss": loss_part,
    }
    small_names = list(small)
    small_shapes = [small[n].shape for n in small_names]
    gw["w_in"], (small_all,) = _mm_tn_sharded(h1, [du, dcva, dcvg, dgs, dgc], out_dtype=bf16, name="g_in",
                                              xchg=([_pack([small[n] for n in small_names])], False))
    dh1, (received["w_in"],) = _mm(
        [(du, w_u), (dcva, w_cva), (dcvg, w_cvg), (dgs, w_gs), (dgc, w_gc)], trans_w=True, out_dtype=bf16, name="d_in",
        xchg=([gw["w_in"]], True))
    grad_x, d_sc1, d_sh1, _ = _final_bwd(dh1, xs, dr1, sc1, alpha, name="final_bwd")

    grads, delta, new_m, new_v = {}, {}, {}, {}
    for n in big:
        ride = ([_pack([d_sh1, d_sc1])], False) if n == "w_out" else None
        res = _sum_adamw(received[n], weights[n][0], mom_m[n][0], mom_v[n][0], name="adamw_" + n, xchg=ride)
        grads[n], delta[n], new_m[n], new_v[n] = res[:4]
        if ride is not None:
            last_all, = res[4]

    small_sum = dict(zip(small_names, _unpack(_sum_parts(small_all, name="sum_small").reshape(-1), small_shapes)))
    last_sum = _unpack(_sum_parts(last_all, name="sum_last").reshape(-1), [(1, d), (1, d)])
    per_dev = dict(zip(small_names, _unpack(small_all.reshape(NDEV, -1), small_shapes)))
    last_dev = _unpack(last_all.reshape(NDEV, -1), [(1, d), (1, d)])
    dmod_all = jnp.concatenate(last_dev + [per_dev[k] for k in ("mod_g1", "mod_sh2", "mod_sc2", "mod_g2")], axis=-1).reshape(NDEV, 6 * d)
    dmod_cols = lax.dynamic_slice(dmod_all.reshape(NDEV, NDEV, ncond), (0, me, 0), (NDEV, 1, ncond)).reshape(NDEV, ncond)
    grads["w_cond"] = _cond_bwd(c_all, dmod_cols, name="cond_bwd")
    loss = small_sum.pop("loss").reshape(())
    grads["b_cond"] = jnp.concatenate(last_sum + [small_sum.pop(k) for k in ("mod_g1", "mod_sh2", "mod_sc2", "mod_g2")], axis=1)
    for n, g in small_sum.items():
        grads[n] = g
    ntap = cv_dw_w.shape[3]
    grads["cv_dw_w"] = lax.dynamic_slice(grads["cv_dw_w"], (0, me * ntap), (CONV_K, ntap))
    nffn = ffn_dw_w.shape[3]
    grads["ffn_dw_w"] = lax.dynamic_slice(grads["ffn_dw_w"], (0, me * nffn), (FFN_K, nffn))
    grads = {n: grads[n].reshape(weights[n].shape) for n in names}

    delta["w_cond"], new_m["w_cond"], new_v["w_cond"] = _adamw(w_cond[0], grads["w_cond"][0], m_w_cond[0], v_w_cond[0],
                                                               name="adamw_w_cond")
    rest = [n for n in names if n not in ["w_cond"] + big]
    squeeze = lambda a: a if a.ndim == 2 else a[0]
    results = _adamw_many(*[[squeeze(t[n].reshape(weights[n].shape)) for n in rest] for t in (weights, grads, mom_m, mom_v)],
                          name="adamw_small")
    for n, (dl, nm, nv) in zip(rest, results):
        delta[n], new_m[n], new_v[n] = dl, nm, nv
    shaped = lambda t: [t[n].reshape(weights[n].shape) for n in names]

    return (loss, grad_x[None], *shaped(grads), *shaped(delta), *shaped(new_m), *shaped(new_v))
```

```python
import functools
import math

import jax
import jax.numpy as jnp
from jax import lax
from jax.experimental import pallas as pl
from jax.experimental.pallas import tpu as pltpu

f32 = jnp.float32
bf16 = jnp.bfloat16

NDEV = 8
LANES = 128
SUBLANES = 8
SSM_GROUP = 16
SSM_STATE = 64
QW = 128
QS = 512
CONV_K = 31
CONV_HALO = 32
FFN_K = 3
FFN_HALO = 8
LN_EPS = 1e-5
ADAM_LR, ADAM_B1, ADAM_B2, ADAM_EPS, ADAM_WD, ADAM_STEP = 0.001, 0.9, 0.999, 1e-08, 0.01, 10
VMEM_LIMIT = 56 * 1024 * 1024
W_TILE_BYTES = 6 * 1024 * 1024
MM_ROWS = 1024
EW_BLOCK_BYTES = 2 * 1024 * 1024
INV_SQRT2 = 1.0 / math.sqrt(2.0)
INV_SQRT_2PI = 1.0 / math.sqrt(2.0 * math.pi)
MESH = pl.DeviceIdType.MESH


def _tile(n, want):
    t = min(n, want)
    while n % t:
        t //= 2
    return t


def _col_tile(n, rows, bytes_per):
    best = LANES if n % LANES == 0 else n
    for t in range(LANES, n + 1, LANES):
        if n % t == 0 and rows * t * bytes_per <= W_TILE_BYTES:
            best = t
    return best


def _params(*sem):
    return pltpu.CompilerParams(dimension_semantics=sem, vmem_limit_bytes=VMEM_LIMIT)


def _row(i):
    return (0, 0)


def _full(shape):
    nd = len(shape)
    return pl.BlockSpec(shape, lambda *a: (0,) * nd)


def _ln(x):
    mu = jnp.mean(x, axis=-1, keepdims=True)
    xc = x - mu
    var = jnp.mean(xc * xc, axis=-1, keepdims=True)
    rstd = lax.rsqrt(var + LN_EPS)
    return xc * rstd, rstd


def _ln_bwd(dxhat, xhat, rstd):
    return rstd * (dxhat - jnp.mean(dxhat, axis=-1, keepdims=True) - xhat * jnp.mean(dxhat * xhat, axis=-1, keepdims=True))


def _sig(x):
    return 1.0 / (1.0 + jnp.exp(-x))


def _gelu(x):
    return 0.5 * x * (1.0 + lax.erf(x * INV_SQRT2))


def _gelu_grad(x):
    return 0.5 * (1.0 + lax.erf(x * INV_SQRT2)) + x * jnp.exp(-0.5 * x * x) * INV_SQRT_2PI


def _colsum(x):
    return jnp.sum(x, axis=0, keepdims=True)


def _mm(pairs, bias=None, *, trans_w=False, out_dtype=f32, name, xchg=None):
    n_p = len(pairs)
    m = pairs[0][0].shape[0]
    n = pairs[0][1].shape[0 if trans_w else 1]
    ktot = sum(x.shape[1] for x, _ in pairs)
    tm = _tile(m, MM_ROWS)
    tn = _col_tile(n, ktot, 2)
    dn = (((1,), (1,)), ((), ())) if trans_w else (((1,), (0,)), ((), ()))

    def body(*refs):
        o_ref = refs[-1]
        acc = None
        for xr, wr in zip(refs[:n_p], refs[n_p:2 * n_p]):
            r = lax.dot_general(xr[...].astype(bf16), wr[...].astype(bf16), dn, preferred_element_type=f32)
            acc = r if acc is None else acc + r
        if bias is not None:
            acc = acc + refs[2 * n_p][...]
        o_ref[...] = acc.astype(out_dtype)

    in_specs = [pl.BlockSpec((tm, x.shape[1]), lambda j, i: (i, 0)) for x, _ in pairs]
    if trans_w:
        in_specs += [pl.BlockSpec((tn, w.shape[1]), lambda j, i: (j, 0)) for _, w in pairs]
    else:
        in_specs += [pl.BlockSpec((w.shape[0], tn), lambda j, i: (0, j)) for _, w in pairs]
    args = [x for x, _ in pairs] + [w for _, w in pairs]
    if bias is not None:
        in_specs.append(pl.BlockSpec((1, tn), lambda j, i: (0, j)))
        args.append(bias)
    (out,), got = _call(
        body, args, name=name, grid=(n // tn, m // tm), in_specs=in_specs,
        out_specs=[pl.BlockSpec((tm, tn), lambda j, i: (i, j))], out_shape=[jax.ShapeDtypeStruct((m, n), out_dtype)],
        sem=("parallel", "arbitrary"), xchg=xchg)
    return out if xchg is None else (out, got)


def _mm_fanout(x, ws, biases, out_dtypes, *, name):
    m, k = x.shape
    tm = _tile(m, MM_ROWS)
    n_w = len(ws)
    biases = list(biases or [])

    def body(x_ref, *refs):
        xb = x_ref[...].astype(bf16)
        o_refs = refs[n_w + len(biases):]
        for p, (w_ref, o_ref, dt) in enumerate(zip(refs[:n_w], o_refs, out_dtypes)):
            acc = jnp.dot(xb, w_ref[...], preferred_element_type=f32)
            if biases:
                acc = acc + refs[n_w + p][...]
            o_ref[...] = acc.astype(dt)

    return pl.pallas_call(
        body, name=name, grid=(m // tm,),
        in_specs=[pl.BlockSpec((tm, k), lambda i: (i, 0))] + [_full(w.shape) for w in ws] + [_full(b.shape) for b in biases],
        out_specs=[pl.BlockSpec((tm, w.shape[1]), lambda i: (i, 0)) for w in ws],
        out_shape=[jax.ShapeDtypeStruct((m, w.shape[1]), dt) for w, dt in zip(ws, out_dtypes)],
        compiler_params=_params("parallel"))(x, *ws, *biases)


def _mm_tn(x, dy, *, out_dtype=f32, name):
    m, k = x.shape
    n = dy.shape[1]
    tm = _tile(m, MM_ROWS)
    tn = _col_tile(n, k, 4)
    steps = m // tm

    def body(x_ref, dy_ref, o_ref, *scratch):
        acc = scratch[0] if scratch else o_ref

        @pl.when(pl.program_id(1) == 0)
        def _():
            acc[...] = jnp.zeros_like(acc)

        acc[...] += lax.dot_general(x_ref[...].astype(bf16), dy_ref[...].astype(bf16), (((0,), (0,)), ((), ())),
                                    preferred_element_type=f32)
        if scratch:
            @pl.when(pl.program_id(1) == steps - 1)
            def _():
                o_ref[...] = acc[...].astype(out_dtype)

    return pl.pallas_call(
        body, name=name, grid=(n // tn, steps),
        in_specs=[pl.BlockSpec((tm, k), lambda j, i: (i, 0)), pl.BlockSpec((tm, tn), lambda j, i: (i, j))],
        out_specs=pl.BlockSpec((k, tn), lambda j, i: (0, j)),
        out_shape=jax.ShapeDtypeStruct((k, n), out_dtype),
        scratch_shapes=[] if out_dtype == f32 else [pltpu.VMEM((k, tn), f32)],
        compiler_params=_params("parallel", "arbitrary"),
    )(x, dy)


def _mm_tn_sharded(x, dys, *, out_dtype, name, xchg=None):
    m, k = x.shape
    widths = [dy.shape[1] for dy in dys]
    n = sum(widths) // NDEV
    tm = _tile(m, 512)
    steps = m // tm
    n_d = len(dys)

    def body(x_ref, *refs):
        dy_refs, o_ref, acc = refs[:n_d], refs[n_d], refs[n_d + 1]
        i = pl.program_id(0)

        @pl.when(i == 0)
        def _():
            acc[...] = jnp.zeros_like(acc)

        xb = x_ref[...].astype(bf16)
        off = 0
        for dy_ref, w in zip(dy_refs, widths):
            acc[:, off:off + w] += lax.dot_general(xb, dy_ref[...].astype(bf16), (((0,), (0,)), ((), ())), preferred_element_type=f32)
            off += w

        @pl.when(i == steps - 1)
        def _():
            for j in range(NDEV):
                o_ref[j] = acc[:, n * j:n * (j + 1)].astype(out_dtype)

    (out,), got = _call(
        body, (x, *dys), name=name, grid=(steps,),
        in_specs=[pl.BlockSpec((tm, k), lambda i: (i, 0))] + [pl.BlockSpec((tm, w), lambda i: (i, 0)) for w in widths],
        out_specs=[pl.BlockSpec((NDEV, k, n), lambda i: (0, 0, 0))], out_shape=[jax.ShapeDtypeStruct((NDEV, k, n), out_dtype)],
        scratch_shapes=[pltpu.VMEM((k, sum(widths)), f32)], sem=("arbitrary",), xchg=xchg)
    return out if xchg is None else (out, got)


def _exchange(arrs, *, scatter, name):
    n = len(arrs)

    def body(*refs):
        _exchange_copies(refs[:n], refs[n:2 * n], refs[2 * n:], scatter, True, True)

    return pl.pallas_call(
        body, name=name, in_specs=[HBM_SPEC] * n, out_specs=[HBM_SPEC] * n, out_shape=_exchange_out_shape(arrs, scatter),
        scratch_shapes=_exchange_sems(n),
    )(*arrs)


HBM_SPEC = pl.BlockSpec(memory_space=pltpu.HBM)


def _flags(scatter, n):
    return list(scatter) if isinstance(scatter, (list, tuple)) else [scatter] * n


def _exchange_out_shape(arrs, scatter):
    return [jax.ShapeDtypeStruct(a.shape if sc else (NDEV,) + a.shape, a.dtype) for a, sc in zip(arrs, _flags(scatter, len(arrs)))]


def _exchange_sems(n):
    return [pltpu.SemaphoreType.DMA(((NDEV - 1) * n,)), pltpu.SemaphoreType.DMA(((NDEV - 1) * n,)), pltpu.SemaphoreType.DMA((n,))]


def _exchange_copies(x_refs, o_refs, sems, scatter, start, wait):
    n = len(x_refs)
    flags = _flags(scatter, n)
    send_sems, recv_sems, local_sems = sems
    ix, iy, ic = lax.axis_index("x"), lax.axis_index("y"), lax.axis_index("c")
    me = 4 * ix + 2 * iy + ic
    local = [pltpu.make_async_copy(x.at[me] if sc else x, o.at[me], local_sems.at[a])
             for a, (x, o, sc) in enumerate(zip(x_refs, o_refs, flags))]

    def peer(k):
        return (1 - ix if k & 4 else ix, 1 - iy if k & 2 else iy, 1 - ic if k & 1 else ic)

    def index(p):
        return 4 * p[0] + 2 * p[1] + p[2]

    def copy(a, k, src, dst, to):
        sem = (k - 1) * n + a
        return pltpu.make_async_remote_copy(src_ref=src, dst_ref=dst, send_sem=send_sems.at[sem], recv_sem=recv_sems.at[sem],
                                            device_id=to, device_id_type=MESH)

    sends, arrivals, passed_on = [], [], []
    for a, (x, o, sc) in enumerate(zip(x_refs, o_refs, flags)):
        if sc:
            for k in range(1, NDEV):
                p = peer(k)
                sends.append(copy(a, k, x.at[index(p)], o.at[me], p))
                arrivals.append(copy(a, k, x.at[me], o.at[index(p)], p))
        else:
            sib = peer(1)
            sends.append(copy(a, 1, x, o.at[me], sib))
            arrivals.append(copy(a, 1, x, o.at[index(sib)], sib))
            for k in (2, 4, 6):
                p, q = peer(k), peer(k + 1)
                sends.append(copy(a, k, x, o.at[me], p))
                passed_on.append((copy(a, k, x, o.at[index(p)], p), copy(a, k + 1, o.at[index(p)], o.at[index(p)], sib)))
                arrivals.append(copy(a, k + 1, o.at[index(q)], o.at[index(q)], sib))
    if start:
        for cp in local + sends:
            cp.start()
    if wait:
        for landed, hand_over in passed_on:
            landed.wait_recv()
            hand_over.start()
        for cp in arrivals:
            cp.wait_recv()
        for cp in sends + [hand_over for _, hand_over in passed_on]:
            cp.wait_send()
        for cp in local:
            cp.wait()


def _call(body, args, *, name, grid, in_specs, out_specs, out_shape, scratch_shapes=(), sem, xchg=None):
    if xchg is None:
        return pl.pallas_call(body, name=name, grid=grid, in_specs=in_specs, out_specs=out_specs, out_shape=out_shape,
                              scratch_shapes=list(scratch_shapes), compiler_params=_params(*sem))(*args), None
    arrs, scatter = xchg
    n, ni, no, ns = len(arrs), len(in_specs), len(out_specs), len(scratch_shapes)

    def wrapped(*refs):
        ins, x_refs = refs[:ni], refs[ni:ni + n]
        outs, o_refs = refs[ni + n:ni + n + no], refs[ni + n + no:ni + 2 * n + no]
        scratch, sems = refs[ni + 2 * n + no:ni + 2 * n + no + ns], refs[ni + 2 * n + no + ns:]
        ids = [pl.program_id(a) for a in range(len(grid))]
        first = functools.reduce(jnp.logical_and, [p == 0 for p in ids])
        last = functools.reduce(jnp.logical_and, [p == g - 1 for p, g in zip(ids, grid)])

        @pl.when(first)
        def _():
            _exchange_copies(x_refs, o_refs, sems, scatter, True, False)

        body(*ins, *outs, *scratch)

        @pl.when(last)
        def _():
            _exchange_copies(x_refs, o_refs, sems, scatter, False, True)

    res = pl.pallas_call(
        wrapped, name=name, grid=grid, in_specs=list(in_specs) + [HBM_SPEC] * n, out_specs=list(out_specs) + [HBM_SPEC] * n,
        out_shape=list(out_shape) + _exchange_out_shape(arrs, scatter),
        scratch_shapes=list(scratch_shapes) + _exchange_sems(n),
        compiler_params=_params(*("arbitrary",) * len(grid)))(*args, *arrs)
    return res[:no], res[no:]


def _sum_parts(parts, *, name):
    r = parts.shape[1]

    def body(p_ref, o_ref):
        acc = p_ref[0]
        for j in range(1, NDEV):
            acc = acc + p_ref[j]
        o_ref[...] = acc

    return pl.pallas_call(body, name=name, out_shape=jax.ShapeDtypeStruct((r, LANES), f32), compiler_params=_params())(parts)


def _col_pieces(n, bounds):
    out = []
    for p, (a, b) in enumerate(bounds):
        for j in range(NDEV):
            lo, hi = max(a, n * j), min(b, n * (j + 1))
            if lo < hi:
                out.append((p, j, lo - a, lo - n * j, hi - lo))
    return out


def _unshard_cols(stacked, bounds, *, name):
    _, k, n = stacked.shape
    tk = _tile(k, 256)
    plan = _col_pieces(n, bounds)

    def body(x_ref, *o_refs):
        for p, j, po, so, w in plan:
            o_refs[p][:, po:po + w] = x_ref[j, :, so:so + w]

    return pl.pallas_call(
        body, name=name, grid=(k // tk,), in_specs=[pl.BlockSpec((NDEV, tk, n), lambda i: (0, i, 0))],
        out_specs=[pl.BlockSpec((tk, b - a), lambda i: (i, 0)) for a, b in bounds],
        out_shape=[jax.ShapeDtypeStruct((k, b - a), stacked.dtype) for a, b in bounds],
        compiler_params=_params("parallel"))(stacked)


def _shard_cols(pieces, *, out_dtype, name):
    k = pieces[0].shape[0]
    bounds, off = [], 0
    for p in pieces:
        bounds.append((off, off + p.shape[1]))
        off += p.shape[1]
    n = off // NDEV
    tk = _tile(k, 256)
    plan = _col_pieces(n, bounds)

    def body(*refs):
        o_ref = refs[-1]
        for p, j, po, so, w in plan:
            o_ref[j, :, so:so + w] = refs[p][:, po:po + w].astype(out_dtype)

    return pl.pallas_call(
        body, name=name, grid=(k // tk,), in_specs=[pl.BlockSpec((tk, b - a), lambda i: (i, 0)) for a, b in bounds],
        out_specs=pl.BlockSpec((NDEV, tk, n), lambda i: (0, i, 0)),
        out_shape=jax.ShapeDtypeStruct((NDEV, k, n), out_dtype),
        compiler_params=_params("parallel"))(*pieces)


def _pack(arrs):
    flat = jnp.concatenate([a.reshape(-1) for a in arrs])
    pad = (-flat.shape[0]) % (SUBLANES * LANES)
    return jnp.pad(flat, (0, pad)).reshape(-1, LANES)


def _unpack(flat, shapes):
    out, off = [], 0
    for s in shapes:
        n = math.prod(s)
        out.append(flat[..., off:off + n].reshape(flat.shape[:-1] + tuple(s)))
        off += n
    return out


def _adamw_math(w, gg, m, v):
    nm = ADAM_B1 * m + (1.0 - ADAM_B1) * gg
    nv = ADAM_B2 * v + (1.0 - ADAM_B2) * (gg * gg)
    m_hat = nm / (1.0 - ADAM_B1 ** ADAM_STEP)
    v_hat = nv / (1.0 - ADAM_B2 ** ADAM_STEP)
    return -ADAM_LR * (m_hat / (jnp.sqrt(v_hat) + ADAM_EPS) + ADAM_WD * w), nm, nv


def _row_block(r, c, copies):
    tr = r
    while copies * tr * c * 4 > EW_BLOCK_BYTES and tr % (4 * SUBLANES) == 0:
        tr //= 2
    return tr


def _adamw(w, g, m, v, *, name):
    r, c = w.shape
    tr = _row_block(r, c, 1)

    def body(w_ref, g_ref, m_ref, v_ref, d_ref, nm_ref, nv_ref):
        d_ref[...], nm_ref[...], nv_ref[...] = _adamw_math(w_ref[...], g_ref[...], m_ref[...], v_ref[...])

    spec = pl.BlockSpec((tr, c), lambda i: (i, 0))
    shp = jax.ShapeDtypeStruct((r, c), f32)
    return pl.pallas_call(
        body, name=name, grid=(r // tr,), in_specs=[spec] * 4, out_specs=[spec] * 3, out_shape=[shp] * 3,
        compiler_params=_params("parallel"),
    )(w, g, m, v)


def _adamw_many(ws, gs, ms, vs, *, name):
    n = len(ws)

    def body(*refs):
        outs = refs[4 * n:]
        for i in range(n):
            res = _adamw_math(refs[i][...], refs[n + i][...], refs[2 * n + i][...], refs[3 * n + i][...])
            for o_ref, r in zip(outs[3 * i:3 * i + 3], res):
                o_ref[...] = r

    res = pl.pallas_call(body, name=name, out_shape=[jax.ShapeDtypeStruct(w.shape, f32) for w in ws for _ in range(3)],
                         compiler_params=_params())(*ws, *gs, *ms, *vs)
    return [res[3 * i:3 * i + 3] for i in range(n)]


def _sum_adamw(parts, w, m, v, *, name, xchg=None):
    r, c = w.shape
    tr = _row_block(r, c, NDEV)

    def body(p_ref, w_ref, m_ref, v_ref, g_ref, d_ref, nm_ref, nv_ref):
        gg = p_ref[0].astype(f32)
        for j in range(1, NDEV):
            gg = gg + p_ref[j].astype(f32)
        g_ref[...] = gg
        d_ref[...], nm_ref[...], nv_ref[...] = _adamw_math(w_ref[...], gg, m_ref[...], v_ref[...])

    spec = pl.BlockSpec((tr, c), lambda i: (i, 0))
    shp = jax.ShapeDtypeStruct((r, c), f32)
    res, got = _call(
        body, (parts, w, m, v), name=name, grid=(r // tr,),
        in_specs=[pl.BlockSpec((NDEV, tr, c), lambda i: (0, i, 0))] + [spec] * 3,
        out_specs=[spec] * 4, out_shape=[shp] * 4, sem=("parallel",), xchg=xchg)
    return tuple(res) if xchg is None else tuple(res) + (got,)


def _cond_fwd(c_all, w, b, *, name):
    nb, n = c_all.shape[0], w.shape[1]

    def body(c_ref, w_ref, b_ref, o_ref):
        cc = c_ref[...]
        o_ref[...] = jnp.dot(cc * _sig(cc), w_ref[...], preferred_element_type=f32,
                             precision=lax.Precision.HIGHEST) + b_ref[...]

    return pl.pallas_call(body, name=name, out_shape=jax.ShapeDtypeStruct((nb, n), f32),
                          compiler_params=_params())(c_all, w, b)


def _cond_bwd(c_all, dmod, *, name):
    d, n = c_all.shape[1], dmod.shape[1]

    def body(c_ref, g_ref, o_ref):
        cc = c_ref[...]
        o_ref[...] = lax.dot_general(cc * _sig(cc), g_ref[...], (((0,), (0,)), ((), ())), preferred_element_type=f32,
                                     precision=lax.Precision.HIGHEST)

    return pl.pallas_call(body, name=name, out_shape=jax.ShapeDtypeStruct((d, n), f32),
                          compiler_params=_params())(c_all, dmod)


def _ssm_disc(lam_re, lam_im, log_dt):
    lr = jnp.minimum(lam_re, -1e-4)
    li = lam_im
    dt = jnp.exp(log_dt)
    mag = jnp.exp(lr * dt)
    ang = li * dt
    lbr, lbi = mag * jnp.cos(ang), mag * jnp.sin(ang)
    num_r, num_i = lbr - 1.0, lbi
    den = lr * lr + li * li
    return lbr, lbi, (num_r * lr + num_i * li) / den, (num_i * lr - num_r * li) / den


def _ssm_prep(lam_re, lam_im, log_dt, *, name):
    def body(a, b, c, o1, o2, o3, o4):
        o1[...], o2[...], o3[...], o4[...] = _ssm_disc(a[...], b[...], c[...])

    shp = jax.ShapeDtypeStruct(lam_re.shape, f32)
    return pl.pallas_call(body, name=name, out_shape=[shp] * 4, compiler_params=_params())(lam_re, lam_im, log_dt)


def _ssm_prep_bwd(lam_re, lam_im, log_dt, cts, *, name):
    def body(a, b, c, g1, g2, g3, g4, o1, o2, o3):
        _, vjp = jax.vjp(_ssm_disc, a[...], b[...], c[...])
        o1[...], o2[...], o3[...] = vjp((g1[...], g2[...], g3[...], g4[...]))

    shp = jax.ShapeDtypeStruct(lam_re.shape, f32)
    return pl.pallas_call(body, name=name, out_shape=[shp, shp, jax.ShapeDtypeStruct(log_dt.shape, f32)],
                          compiler_params=_params())(lam_re, lam_im, log_dt, *cts)


S5_ROWS = 512


def _step_major(x3):
    k, nt, c = x3.shape
    return jnp.swapaxes(x3, 0, 1).reshape(k * nt, c)


def _chunk_major(x2, nt):
    return jnp.swapaxes(x2.reshape(nt, SUBLANES, x2.shape[1]), 0, 1)


def _chain_carries(loc_r, loc_i, pr, pi_, forward):
    row = lax.broadcasted_iota(jnp.int32, loc_r.shape, 0)
    shift = 1 if forward else SUBLANES - 1
    order = range(1, SUBLANES) if forward else range(SUBLANES - 2, -1, -1)
    er, ei = loc_r, loc_i
    for k in order:
        sr, si = pltpu.roll(er, shift, 0), pltpu.roll(ei, shift, 0)
        er = jnp.where(row == k, loc_r + pr * sr - pi_ * si, er)
        ei = jnp.where(row == k, loc_i + pr * si + pi_ * sr, ei)
    edge = 0 if forward else SUBLANES - 1
    return (jnp.where(row == edge, 0.0, pltpu.roll(er, shift, 0)), jnp.where(row == edge, 0.0, pltpu.roll(ei, shift, 0)))


def _chunk_power(ar, ai, chunk_len):
    pr, pi_ = ar, ai
    for _ in range(int(math.log2(chunk_len))):
        pr, pi_ = pr * pr - pi_ * pi_, 2.0 * pr * pi_
    return pr, pi_


def _ssm_mats(bre_ref, bim_ref, cre_ref, cim_ref, cfr_ref, cfi_ref, bbar_s, cmat_s, nq):
    for q in range(nq):
        cr, ci, br, bi = cfr_ref[q], cfi_ref[q], bre_ref[q], bim_ref[q]
        bbar_s[q, :, 0:QS] = (cr * br - ci * bi).astype(bf16)
        bbar_s[q, :, QS:2 * QS] = (cr * bi + ci * br).astype(bf16)
        cmat_s[q, 0:QS, :] = cre_ref[q].astype(bf16)
        cmat_s[q, QS:2 * QS, :] = (-cim_ref[q]).astype(bf16)


def _ssm_fwd(u, ar, ai, bre, bim, cre, cim, cfr, cfi, dvec, *, name, xchg=None):
    s, sw = u.shape
    nq = sw // QW
    st = nq * 2 * QS
    tb = _tile(s, S5_ROWS)
    nb, nt, chunk_len = s // tb, tb // SUBLANES, s // SUBLANES
    assert chunk_len & (chunk_len - 1) == 0 and nt % 16 == 0

    def body(u_ref, ar_ref, ai_ref, bre_ref, bim_ref, cre_ref, cim_ref, cfr_ref, cfi_ref, d_ref,
             h_out, yraw_out, y_out, buf, hc, bbar_s, cmat_s):
        ph, i = pl.program_id(0), pl.program_id(1)

        @pl.when(i == 0)
        def _():
            _ssm_mats(bre_ref, bim_ref, cre_ref, cim_ref, cfr_ref, cfi_ref, bbar_s, cmat_s, nq)

        @pl.when((ph == 0) & (i == 0))
        def _():
            hc[...] = jnp.zeros_like(hc)

        @pl.when((ph == 1) & (i == 0))
        def _():
            for q in range(nq):
                o = q * 2 * QS
                pr, pi_ = _chunk_power(ar_ref[q], ai_ref[q], chunk_len)
                sr, si = _chain_carries(hc[:, o:o + QS], hc[:, o + QS:o + 2 * QS], pr, pi_, True)
                hc[:, o:o + QS] = sr
                hc[:, o + QS:o + 2 * QS] = si

        uu = u_ref[...]
        up = _step_major(uu).astype(bf16)
        for q in range(nq):
            o = q * 2 * QS
            buf[:, o:o + 2 * QS] = jnp.dot(up[:, q * QW:(q + 1) * QW], bbar_s[q], preferred_element_type=f32)

        for q in range(nq):
            o = q * 2 * QS
            a_r = jnp.broadcast_to(ar_ref[q], (SUBLANES, QS))
            a_i = jnp.broadcast_to(ai_ref[q], (SUBLANES, QS))

            def step(t, carry, o=o, a_r=a_r, a_i=a_i):
                hr, hi = carry
                r0 = pl.multiple_of(t * SUBLANES, SUBLANES)
                nr = a_r * hr - a_i * hi + buf[pl.ds(r0, SUBLANES), o:o + QS]
                ni = a_r * hi + a_i * hr + buf[pl.ds(r0, SUBLANES), o + QS:o + 2 * QS]
                buf[pl.ds(r0, SUBLANES), o:o + QS] = nr
                buf[pl.ds(r0, SUBLANES), o + QS:o + 2 * QS] = ni
                return nr, ni

            hr, hi = lax.fori_loop(0, nt, step, (hc[:, o:o + QS], hc[:, o + QS:o + 2 * QS]))
            hc[:, o:o + QS] = hr
            hc[:, o + QS:o + 2 * QS] = hi

        @pl.when(ph == 1)
        def _():
            for q in range(nq):
                o = q * 2 * QS
                cs = slice(q * QW, (q + 1) * QW)
                hq = buf[:, o:o + 2 * QS].astype(bf16)
                h_out[:, o:o + 2 * QS] = hq
                yq = _chunk_major(jnp.dot(hq, cmat_s[q], preferred_element_type=f32), nt) + d_ref[:, cs] * uu[:, :, cs]
                yraw_out[:, :, cs] = yq
                y_out[:, :, cs] = _gelu(yq).astype(bf16)

    blk = lambda ph, i: (0, i, 0)
    oblk = lambda ph, i: (0, i * ph, 0)
    act = lambda dt: jax.ShapeDtypeStruct((SUBLANES, chunk_len, sw), dt)
    (h_p, yraw3, y3), got = _call(
        body, (u.reshape(SUBLANES, chunk_len, sw), ar, ai, bre, bim, cre, cim, cfr, cfi, dvec), name=name, grid=(2, nb),
        in_specs=[pl.BlockSpec((SUBLANES, nt, sw), blk), _full(ar.shape), _full(ai.shape), _full(bre.shape), _full(bim.shape),
                  _full(cre.shape), _full(cim.shape), _full(cfr.shape), _full(cfi.shape), _full(dvec.shape)],
        out_specs=[pl.BlockSpec((tb, st), lambda ph, i: (i * ph, 0)), pl.BlockSpec((SUBLANES, nt, sw), oblk),
                   pl.BlockSpec((SUBLANES, nt, sw), oblk)],
        out_shape=[jax.ShapeDtypeStruct((s, st), bf16), act(f32), act(bf16)],
        scratch_shapes=[pltpu.VMEM((tb, st), f32), pltpu.VMEM((SUBLANES, st), f32),
                        pltpu.VMEM((nq, QW, 2 * QS), bf16), pltpu.VMEM((nq, 2 * QS, QW), bf16)],
        sem=("arbitrary", "arbitrary"), xchg=xchg)
    return h_p, yraw3, y3.reshape(s, sw), got


def _ssm_bwd(dy, yraw3, u, h_p, ar, ai, bre, bim, cre, cim, cfr, cfi, dvec, *, name, xchg=None):
    s, sw = u.shape
    nq = sw // QW
    st = nq * 2 * QS
    tb = _tile(s, S5_ROWS)
    nb, nt, chunk_len = s // tb, tb // SUBLANES, s // SUBLANES

    def body(dy_ref, yraw_ref, u_ref, h_ref, ar_ref, ai_ref, bre_ref, bim_ref, cre_ref, cim_ref, cfr_ref, cfi_ref, d_ref,
             du_out, dbre_out, dbim_out, dcre_out, dcim_out, dcfr_out, dcfi_out, dlbr_out, dlbi_out, dd_out, dbu_out,
             buf, rc, acc, dbbar, dcmat, bbar_s, cmat_s):
        ph, i = pl.program_id(0), pl.program_id(1)

        @pl.when(i == 0)
        def _():
            _ssm_mats(bre_ref, bim_ref, cre_ref, cim_ref, cfr_ref, cfi_ref, bbar_s, cmat_s, nq)

        @pl.when((ph == 0) & (i == 0))
        def _():
            rc[...] = jnp.zeros_like(rc)

        @pl.when((ph == 1) & (i == 0))
        def _():
            for q in range(nq):
                o = q * 2 * QS
                pr, pi_ = _chunk_power(ar_ref[q], ai_ref[q], chunk_len)
                sr, si = _chain_carries(rc[:, o:o + QS], rc[:, o + QS:o + 2 * QS], pr, -pi_, False)
                rc[:, o:o + QS] = sr
                rc[:, o + QS:o + 2 * QS] = si
            acc[...] = jnp.zeros_like(acc)
            dbbar[...] = jnp.zeros_like(dbbar)
            dcmat[...] = jnp.zeros_like(dcmat)
            dd_out[...] = jnp.zeros_like(dd_out)
            dbu_out[...] = jnp.zeros_like(dbu_out)

        dyraw = dy_ref[...].astype(f32) * _gelu_grad(yraw_ref[...])
        dyp = _step_major(dyraw).astype(bf16)
        for q in range(nq):
            o = q * 2 * QS
            buf[:, o:o + 2 * QS] = lax.dot_general(dyp[:, q * QW:(q + 1) * QW], cmat_s[q], (((1,), (1,)), ((), ())),
                                                   preferred_element_type=f32)

        def recur(with_grad):
            for q in range(nq):
                o = q * 2 * QS
                a_r = jnp.broadcast_to(ar_ref[q], (SUBLANES, QS))
                a_i = jnp.broadcast_to(ai_ref[q], (SUBLANES, QS))

                def step(j, carry, o=o, a_r=a_r, a_i=a_i):
                    r16 = pl.multiple_of((nt // 2 - 1 - j) * 2 * SUBLANES, 2 * SUBLANES)
                    if with_grad:
                        rr, ri, gr, gi = carry
                        h_re = h_ref[pl.ds(r16, 2 * SUBLANES), o:o + QS].astype(f32)
                        h_im = h_ref[pl.ds(r16, 2 * SUBLANES), o + QS:o + 2 * QS].astype(f32)
                    else:
                        rr, ri = carry
                    for half in (1, 0):
                        rows = pl.ds(pl.multiple_of(r16 + half * SUBLANES, SUBLANES), SUBLANES)
                        if with_grad:
                            hr = h_re[half * SUBLANES:(half + 1) * SUBLANES]
                            hi = h_im[half * SUBLANES:(half + 1) * SUBLANES]
                            gr = gr + hr * rr + hi * ri
                            gi = gi + hr * ri - hi * rr
                        nr = buf[rows, o:o + QS] + a_r * rr + a_i * ri
                        ni = buf[rows, o + QS:o + 2 * QS] + a_r * ri - a_i * rr
                        buf[rows, o:o + QS] = nr
                        buf[rows, o + QS:o + 2 * QS] = ni
                        rr, ri = nr, ni
                    return (rr, ri, gr, gi) if with_grad else (rr, ri)

                init = (rc[:, o:o + QS], rc[:, o + QS:o + 2 * QS])
                if with_grad:
                    init = init + (acc[:, o:o + QS], acc[:, o + QS:o + 2 * QS])
                res = lax.fori_loop(0, nt // 2, step, init)
                rc[:, o:o + QS] = res[0]
                rc[:, o + QS:o + 2 * QS] = res[1]
                if with_grad:
                    acc[:, o:o + QS] = res[2]
                    acc[:, o + QS:o + 2 * QS] = res[3]

        @pl.when(ph == 0)
        def _():
            recur(False)

        @pl.when(ph == 1)
        def _():
            recur(True)
            uu = u_ref[...]
            up = _step_major(uu).astype(bf16)
            dd_out[...] += _colsum((dyraw * uu).reshape(tb, sw))
            for q in range(nq):
                o = q * 2 * QS
                cs = slice(q * QW, (q + 1) * QW)
                lam = buf[:, o:o + 2 * QS].astype(bf16)
                duq = _chunk_major(lax.dot_general(lam, bbar_s[q], (((1,), (1,)), ((), ())), preferred_element_type=f32), nt) \
                    + d_ref[:, cs] * dyraw[:, :, cs]
                du_out[:, :, cs] = duq.astype(bf16)
                dbu_out[:, cs] += _colsum(duq.reshape(tb, QW))
                dbbar[q] += lax.dot_general(up[:, cs], lam, (((0,), (0,)), ((), ())), preferred_element_type=f32)
                dcmat[q] += lax.dot_general(dyp[:, cs], h_ref[:, o:o + 2 * QS], (((0,), (0,)), ((), ())),
                                            preferred_element_type=f32)

        @pl.when((ph == 1) & (i == nb - 1))
        def _():
            for q in range(nq):
                o = q * 2 * QS
                cr, ci, br, bi = cfr_ref[q], cfi_ref[q], bre_ref[q], bim_ref[q]
                gr, gi = dbbar[q, :, 0:QS], dbbar[q, :, QS:2 * QS]
                dbre_out[q] = cr * gr + ci * gi
                dbim_out[q] = cr * gi - ci * gr
                dcfr_out[q] = _colsum(gr * br + gi * bi)
                dcfi_out[q] = _colsum(gi * br - gr * bi)
                dcre_out[q] = dcmat[q, :, 0:QS].T
                dcim_out[q] = -dcmat[q, :, QS:2 * QS].T
                dlbr_out[q] = _colsum(acc[:, o:o + QS])
                dlbi_out[q] = _colsum(acc[:, o + QS:o + 2 * QS])

    blk = lambda ph, i: (0, nb - 1 - i, 0)
    oblk = lambda ph, i: (0, (nb - 1 - i) * ph + (nb - 1) * (1 - ph), 0)
    pshapes = [ar.shape, ai.shape, bre.shape, bim.shape, cre.shape, cim.shape, cfr.shape, cfi.shape, dvec.shape]
    oshapes = [bre.shape, bim.shape, cre.shape, cim.shape, cfr.shape, cfi.shape, ar.shape, ai.shape, dvec.shape, dvec.shape]
    act = pl.BlockSpec((SUBLANES, nt, sw), blk)
    view = lambda a: a.reshape(SUBLANES, chunk_len, sw)
    res, got = _call(
        body, (view(dy), yraw3, view(u), h_p, ar, ai, bre, bim, cre, cim, cfr, cfi, dvec), name=name, grid=(2, nb),
        in_specs=[act, act, act, pl.BlockSpec((tb, st), lambda ph, i: (nb - 1 - i, 0))] + [_full(p) for p in pshapes],
        out_specs=[pl.BlockSpec((SUBLANES, nt, sw), oblk)] + [_full(p) for p in oshapes],
        out_shape=[jax.ShapeDtypeStruct((SUBLANES, chunk_len, sw), bf16)] + [jax.ShapeDtypeStruct(p, f32) for p in oshapes],
        scratch_shapes=[pltpu.VMEM((tb, st), f32),
                        pltpu.VMEM((SUBLANES, st), f32), pltpu.VMEM((SUBLANES, st), f32),
                        pltpu.VMEM((nq, QW, 2 * QS), f32), pltpu.VMEM((nq, QW, 2 * QS), f32),
                        pltpu.VMEM((nq, QW, 2 * QS), bf16), pltpu.VMEM((nq, 2 * QS, QW), bf16)],
        sem=("arbitrary", "arbitrary"), xchg=xchg)
    return (res[0].reshape(s, sw),) + tuple(res[1:]) + (got,)


def _lnmod(x, sc, sh, *, name):
    s, d = x.shape
    tb = _tile(s, 512)

    def body(x_ref, sc_ref, sh_ref, o_ref):
        xh, _ = _ln(x_ref[...])
        o_ref[...] = (xh * (1.0 + sc_ref[...]) + sh_ref[...]).astype(bf16)

    blk = pl.BlockSpec((tb, d), lambda i: (i, 0))
    vec = pl.BlockSpec((1, d), _row)
    return pl.pallas_call(body, name=name, grid=(s // tb,), in_specs=[blk, vec, vec], out_specs=blk,
                          out_shape=jax.ShapeDtypeStruct((s, d), bf16), compiler_params=_params("parallel"))(x, sc, sh)


ROWS = 32


def _row_chunks(n_rows, rows, fn, init, start=0):
    return lax.fori_loop(start, n_rows // rows, lambda c, carry: fn(pl.multiple_of(c * rows, rows), carry), init)


def _rows_from(win, o, rows):
    if o % SUBLANES == 0:
        return win[o:o + rows]
    n = win.shape[0]
    return pltpu.roll(win, (n - o) % n, 0)[0:rows]


def _window_before(ref, halo, r0, rows, first, cols):
    if first:
        return jnp.concatenate([halo, ref[pl.ds(0, rows), cols]], axis=0)
    return ref[pl.ds(pl.multiple_of(r0 - SUBLANES, SUBLANES), rows + SUBLANES), cols]


def _taps3(win, w, off, rows):
    return _rows_from(win, off, rows) * w[0] + _rows_from(win, off + 1, rows) * w[1] + _rows_from(win, off + 2, rows) * w[2]


def _fold8(x):
    acc = x[0:SUBLANES]
    for r in range(1, x.shape[0] // SUBLANES):
        acc = acc + x[r * SUBLANES:(r + 1) * SUBLANES]
    return acc


def _conv_halo_specs(tb, cw, halo, s):
    per = tb // halo
    prev = pl.BlockSpec((halo, cw), lambda i: (jnp.maximum(i * per - 1, 0), 0))
    nxt = pl.BlockSpec((halo, cw), lambda i: (jnp.minimum((i + 1) * per, s // halo - 1), 0))
    return prev, nxt


WIDE_ROWS = 16


def _shift_groups(lo, hi):
    return [(b, [o for o in range(lo, hi + 1) if o % SUBLANES == b]) for b in range(SUBLANES)]


def _shifted(win, b):
    return win if b == 0 else _rows_from(win, b, win.shape[0] - SUBLANES)


def _conv31(win, w_ref, cols, rows, lo, hi, tap_of):
    acc = None
    for b, offs in _shift_groups(lo, hi):
        if offs:
            wb = _shifted(win, b)
            for o in offs:
                term = wb[o - b:o - b + rows] * w_ref[pl.ds(tap_of(o), 1), cols]
                acc = term if acc is None else acc + term
    return acc


def _gate_into(ext, a_ref, g_ref, ah_ref, gh_ref, tb, i):
    ext[pl.ds(0, CONV_HALO), :] = jnp.where(i > 0, ah_ref[...] * _sig(gh_ref[...]), 0.0)

    def chunk(r0, carry):
        ext[pl.ds(pl.multiple_of(r0 + CONV_HALO, SUBLANES), WIDE_ROWS), :] = \
            a_ref[pl.ds(r0, WIDE_ROWS), :] * _sig(g_ref[pl.ds(r0, WIDE_ROWS), :])
        return carry

    _row_chunks(tb, WIDE_ROWS, chunk, 0)


def _causal_conv_into(v2buf, ext, w_ref, b_ref, tb, cw):
    for ct in range(cw // LANES):
        cols = slice(ct * LANES, (ct + 1) * LANES)

        def chunk(r0, carry, cols=cols):
            win = ext[pl.ds(r0, ROWS + CONV_HALO), cols]
            v2buf[pl.ds(r0, ROWS), cols] = _conv31(win, w_ref, cols, ROWS, 2, CONV_K + 1, lambda o: o - 2) + b_ref[:, cols]
            return carry

        _row_chunks(tb, ROWS, chunk, 0)


def _silu_grad(x):
    sg = _sig(x)
    return sg * (1.0 + x * (1.0 - sg))


def _conv_fwd(cva, cvg, w, b, lng, lnb, *, name, xchg=None):
    s, cw = cva.shape
    tb = _tile(s, 256)
    prev, _ = _conv_halo_specs(tb, cw, CONV_HALO, s)

    def body(a_ref, g_ref, ah_ref, gh_ref, w_ref, b_ref, lng_ref, lnb_ref, o_ref, v2_ref, ext):
        _gate_into(ext, a_ref, g_ref, ah_ref, gh_ref, tb, pl.program_id(0))
        _causal_conv_into(v2_ref, ext, w_ref, b_ref, tb, cw)
        xh, _ = _ln(v2_ref[...])
        v3 = xh * lng_ref[...] + lnb_ref[...]
        o_ref[...] = (v3 * _sig(v3)).astype(bf16)

    blk = pl.BlockSpec((tb, cw), lambda i: (i, 0))
    vec = pl.BlockSpec((1, cw), _row)
    (v4, v2), got = _call(
        body, (cva, cvg, cva, cvg, w, b, lng, lnb), name=name, grid=(s // tb,),
        in_specs=[blk, blk, prev, prev, _full(w.shape), vec, vec, vec], out_specs=[blk, blk],
        out_shape=[jax.ShapeDtypeStruct((s, cw), bf16), jax.ShapeDtypeStruct((s, cw), f32)],
        scratch_shapes=[pltpu.VMEM((tb + CONV_HALO, cw), f32)], sem=("parallel",), xchg=xchg)
    return v4, v2, got


def _conv_bwd_ln(dv4, v2, lng, lnb, *, name):
    s, cw = v2.shape
    tb = _tile(s, 256)

    def body(d_ref, v2_ref, lng_ref, lnb_ref, o_ref, dg_ref, db_ref):
        @pl.when(pl.program_id(0) == 0)
        def _():
            dg_ref[...] = jnp.zeros_like(dg_ref)
            db_ref[...] = jnp.zeros_like(db_ref)

        xh, rstd = _ln(v2_ref[...])
        v3 = xh * lng_ref[...] + lnb_ref[...]
        dv3 = d_ref[...].astype(f32) * _silu_grad(v3)
        dg_ref[...] += _colsum(dv3 * xh)
        db_ref[...] += _colsum(dv3)
        o_ref[...] = _ln_bwd(dv3 * lng_ref[...], xh, rstd)

    blk = pl.BlockSpec((tb, cw), lambda i: (i, 0))
    vec = pl.BlockSpec((1, cw), _row)
    vshape = jax.ShapeDtypeStruct((1, cw), f32)
    return pl.pallas_call(
        body, name=name, grid=(s // tb,), in_specs=[blk, blk, vec, vec],
        out_specs=[blk, vec, vec], out_shape=[jax.ShapeDtypeStruct((s, cw), f32), vshape, vshape],
        compiler_params=_params("arbitrary"))(dv4, v2, lng, lnb)


def _conv_bwd_taps(dv2, cva, cvg, w, *, name, xchg=None):
    s, cw = cva.shape
    tb = _tile(s, 256)
    nb = s // tb
    prev, nxt = _conv_halo_specs(tb, cw, CONV_HALO, s)

    def body(d_ref, dn_ref, a_ref, g_ref, ah_ref, gh_ref, w_ref, da_ref, dg_ref, dw_ref, db_ref, sa_ref, sg_ref,
             ext, dext, dvbuf, tap_sums):
        i = pl.program_id(0)

        @pl.when(i == 0)
        def _():
            for r in (dw_ref, db_ref, sa_ref, sg_ref):
                r[...] = jnp.zeros_like(r)

        _gate_into(ext, a_ref, g_ref, ah_ref, gh_ref, tb, i)
        dext[pl.ds(tb, CONV_HALO), :] = jnp.where(i < nb - 1, dn_ref[...], 0.0)

        def copy(r0, carry):
            dext[pl.ds(r0, WIDE_ROWS), :] = d_ref[pl.ds(r0, WIDE_ROWS), :]
            return carry

        _row_chunks(tb, WIDE_ROWS, copy, 0)

        for ct in range(cw // LANES):
            cols = slice(ct * LANES, (ct + 1) * LANES)

            tap_sums[...] = jnp.zeros_like(tap_sums)

            def back(r0, carry, cols=cols):
                win = dext[pl.ds(r0, ROWS + CONV_HALO), cols]
                dvbuf[pl.ds(r0, ROWS), cols] = _conv31(win, w_ref, cols, ROWS, 0, CONV_K - 1, lambda o: CONV_K - 1 - o)
                win = ext[pl.ds(r0, ROWS + CONV_HALO), cols]
                dd = d_ref[pl.ds(r0, ROWS), cols]
                for b, offs in _shift_groups(2, CONV_K + 1):
                    wb = _shifted(win, b)
                    for o in offs:
                        tap_sums[o - 2] += _fold8(dd * wb[o - b:o - b + ROWS])
                return carry

            _row_chunks(tb, ROWS, back, 0)
            for k in range(CONV_K):
                dw_ref[pl.ds(k, 1), cols] += _colsum(tap_sums[k])

        def gate_back(r0, sums):
            rows = pl.ds(r0, WIDE_ROWS)
            aa, sg, dv = a_ref[rows, :], _sig(g_ref[rows, :]), dvbuf[rows, :]
            da = dv * sg
            dgate = dv * aa * sg * (1.0 - sg)
            da_ref[rows, :] = da.astype(bf16)
            dg_ref[rows, :] = dgate.astype(bf16)
            return sums[0] + _fold8(da), sums[1] + _fold8(dgate), sums[2] + _fold8(d_ref[rows, :])

        zero = jnp.zeros((SUBLANES, cw), f32)
        sums = _row_chunks(tb, WIDE_ROWS, gate_back, (zero, zero, zero))
        sa_ref[...] += _colsum(sums[0])
        sg_ref[...] += _colsum(sums[1])
        db_ref[...] += _colsum(sums[2])

    blk = pl.BlockSpec((tb, cw), lambda i: (i, 0))
    vec = pl.BlockSpec((1, cw), _row)
    vshape = jax.ShapeDtypeStruct((1, cw), f32)
    act = jax.ShapeDtypeStruct((s, cw), bf16)
    res, got = _call(
        body, (dv2, dv2, cva, cvg, cva, cvg, w), name=name, grid=(nb,), in_specs=[blk, nxt, blk, blk, prev, prev, _full(w.shape)],
        out_specs=[blk, blk, _full(w.shape), vec, vec, vec],
        out_shape=[act, act, jax.ShapeDtypeStruct(w.shape, f32), vshape, vshape, vshape],
        scratch_shapes=[pltpu.VMEM((tb + CONV_HALO, cw), f32), pltpu.VMEM((tb + CONV_HALO, cw), f32), pltpu.VMEM((tb, cw), f32),
                        pltpu.VMEM((CONV_HALO, SUBLANES, LANES), f32)],
        sem=("arbitrary",), xchg=xchg)
    return tuple(res) + (got,)


def _glu_merge(ya, yb, ycv, gs, gc, *, name):
    s, d = ya.shape
    tb = _tile(s, 512)

    def body(ya_ref, yb_ref, ycv_ref, gs_ref, gc_ref, o_ref):
        ld = lambda r: r[...].astype(f32)
        z = ld(ya_ref) * _sig(ld(yb_ref))
        o_ref[...] = (_sig(ld(gs_ref)) * z + _sig(ld(gc_ref)) * ld(ycv_ref)).astype(bf16)

    blk = pl.BlockSpec((tb, d), lambda i: (i, 0))
    return pl.pallas_call(body, name=name, grid=(s // tb,), in_specs=[blk] * 5, out_specs=blk,
                          out_shape=jax.ShapeDtypeStruct((s, d), bf16), compiler_params=_params("parallel"))(ya, yb, ycv, gs, gc)


def _glu_merge_bwd(dm, ya, yb, ycv, gs, gc, *, name):
    s, d = ya.shape
    tb = _tile(s, 512)

    def body(dm_ref, ya_ref, yb_ref, ycv_ref, gs_ref, gc_ref, dya_ref, dyb_ref, dycv_ref, dgs_ref, dgc_ref, sgs_ref, sgc_ref):
        @pl.when(pl.program_id(0) == 0)
        def _():
            sgs_ref[...] = jnp.zeros_like(sgs_ref)
            sgc_ref[...] = jnp.zeros_like(sgc_ref)

        ld = lambda r: r[...].astype(f32)
        dmv, yav = ld(dm_ref), ld(ya_ref)
        sb, ss, scv = _sig(ld(yb_ref)), _sig(ld(gs_ref)), _sig(ld(gc_ref))
        z = yav * sb
        dz = dmv * ss
        dgs = dmv * z * ss * (1.0 - ss)
        dgc = dmv * ld(ycv_ref) * scv * (1.0 - scv)
        dya_ref[...] = (dz * sb).astype(bf16)
        dyb_ref[...] = (dz * yav * sb * (1.0 - sb)).astype(bf16)
        dycv_ref[...] = (dmv * scv).astype(bf16)
        dgs_ref[...] = dgs.astype(bf16)
        dgc_ref[...] = dgc.astype(bf16)
        sgs_ref[...] += _colsum(dgs)
        sgc_ref[...] += _colsum(dgc)

    blk = pl.BlockSpec((tb, d), lambda i: (i, 0))
    vec = pl.BlockSpec((1, d), _row)
    act = jax.ShapeDtypeStruct((s, d), bf16)
    vshape = jax.ShapeDtypeStruct((1, d), f32)
    return pl.pallas_call(body, name=name, grid=(s // tb,), in_specs=[blk] * 6, out_specs=[blk] * 5 + [vec, vec],
                          out_shape=[act] * 5 + [vshape, vshape], compiler_params=_params("arbitrary"))(dm, ya, yb, ycv, gs, gc)


def _resid_ln_mod(x, o, g, lng, lnb, sc, sh, alpha, *, name):
    s, d = x.shape
    tb = _tile(s, 512)

    def body(x_ref, o_ref, g_ref, lng_ref, lnb_ref, sc_ref, sh_ref, x1_ref, h_ref):
        xh, _ = _ln(alpha * x_ref[...] + g_ref[...] * o_ref[...].astype(f32))
        x1 = xh * lng_ref[...] + lnb_ref[...]
        x1_ref[...] = x1
        xh1, _ = _ln(x1)
        h_ref[...] = (xh1 * (1.0 + sc_ref[...]) + sh_ref[...]).astype(bf16)

    blk = pl.BlockSpec((tb, d), lambda i: (i, 0))
    vec = pl.BlockSpec((1, d), _row)
    return pl.pallas_call(body, name=name, grid=(s // tb,), in_specs=[blk, blk] + [vec] * 5, out_specs=[blk, blk],
                          out_shape=[jax.ShapeDtypeStruct((s, d), f32), jax.ShapeDtypeStruct((s, d), bf16)],
                          compiler_params=_params("parallel"))(x, o, g, lng, lnb, sc, sh)


def _resid_ln_loss(x1, f, w_dn, g, lng, lnb, tgt, alpha, *, name):
    s, d = x1.shape
    tb = _tile(s, 512)

    def body(x1_ref, f_ref, w_ref, g_ref, lng_ref, lnb_ref, t_ref, dr_ref, dy_ref, loss_ref, dlg_ref, dlb_ref, dg_ref):
        @pl.when(pl.program_id(0) == 0)
        def _():
            for r in (loss_ref, dlg_ref, dlb_ref, dg_ref):
                r[...] = jnp.zeros_like(r)

        yv = jnp.dot(f_ref[...], w_ref[...], preferred_element_type=f32)
        xh, rstd = _ln(alpha * x1_ref[...] + g_ref[...] * yv)
        err = xh * lng_ref[...] + lnb_ref[...] - t_ref[...]
        loss_ref[...] += 0.5 * jnp.sum(jnp.sum(err * err, axis=-1, keepdims=True) / d, axis=0, keepdims=True)
        dx2 = err / d
        dlg_ref[...] += _colsum(dx2 * xh)
        dlb_ref[...] += _colsum(dx2)
        dr = _ln_bwd(dx2 * lng_ref[...], xh, rstd)
        dg_ref[...] += _colsum(dr * yv)
        dr_ref[...] = dr
        dy_ref[...] = (g_ref[...] * dr).astype(bf16)

    blk = pl.BlockSpec((tb, d), lambda i: (i, 0))
    vec = pl.BlockSpec((1, d), _row)
    vshape = jax.ShapeDtypeStruct((1, d), f32)
    return pl.pallas_call(
        body, name=name, grid=(s // tb,),
        in_specs=[blk, pl.BlockSpec((tb, f.shape[1]), lambda i: (i, 0)), _full(w_dn.shape), vec, vec, vec, blk],
        out_specs=[blk, blk, pl.BlockSpec((1, 1), _row), vec, vec, vec],
        out_shape=[jax.ShapeDtypeStruct((s, d), f32), jax.ShapeDtypeStruct((s, d), bf16),
                   jax.ShapeDtypeStruct((1, 1), f32), vshape, vshape, vshape],
        compiler_params=_params("arbitrary"))(x1, f, w_dn, g, lng, lnb, tgt)


def _mid_bwd(dh2, x1, dr2, x, o, g, sc, lng, alpha, *, name):
    s, d = x.shape
    tb = _tile(s, 512)

    def body(dh_ref, x1_ref, dr2_ref, x_ref, o_ref, g_ref, sc_ref, lng_ref,
             dr1_ref, do_ref, dsc_ref, dsh_ref, dlg_ref, dlb_ref, dg_ref):
        @pl.when(pl.program_id(0) == 0)
        def _():
            for r in (dsc_ref, dsh_ref, dlg_ref, dlb_ref, dg_ref):
                r[...] = jnp.zeros_like(r)

        dh = dh_ref[...].astype(f32)
        xh1, rstd1 = _ln(x1_ref[...])
        dsc_ref[...] += _colsum(dh * xh1)
        dsh_ref[...] += _colsum(dh)
        dx1 = alpha * dr2_ref[...] + _ln_bwd(dh * (1.0 + sc_ref[...]), xh1, rstd1)
        ov = o_ref[...].astype(f32)
        xhr, rstdr = _ln(alpha * x_ref[...] + g_ref[...] * ov)
        dlg_ref[...] += _colsum(dx1 * xhr)
        dlb_ref[...] += _colsum(dx1)
        dr1 = _ln_bwd(dx1 * lng_ref[...], xhr, rstdr)
        dg_ref[...] += _colsum(dr1 * ov)
        dr1_ref[...] = dr1
        do_ref[...] = (g_ref[...] * dr1).astype(bf16)

    blk = pl.BlockSpec((tb, d), lambda i: (i, 0))
    vec = pl.BlockSpec((1, d), _row)
    vshape = jax.ShapeDtypeStruct((1, d), f32)
    return pl.pallas_call(
        body, name=name, grid=(s // tb,), in_specs=[blk] * 5 + [vec] * 3, out_specs=[blk, blk] + [vec] * 5,
        out_shape=[jax.ShapeDtypeStruct((s, d), f32), jax.ShapeDtypeStruct((s, d), bf16)] + [vshape] * 5,
        compiler_params=_params("arbitrary"))(dh2, x1, dr2, x, o, g, sc, lng)


def _final_bwd(dh1, x, dr1, sc, alpha, *, name, xchg=None):
    s, d = x.shape
    tb = _tile(s, 512)

    def body(dh_ref, x_ref, dr1_ref, sc_ref, dx_ref, dsc_ref, dsh_ref):
        @pl.when(pl.program_id(0) == 0)
        def _():
            dsc_ref[...] = jnp.zeros_like(dsc_ref)
            dsh_ref[...] = jnp.zeros_like(dsh_ref)

        dh = dh_ref[...].astype(f32)
        xh, rstd = _ln(x_ref[...])
        dsc_ref[...] += _colsum(dh * xh)
        dsh_ref[...] += _colsum(dh)
        dx_ref[...] = alpha * dr1_ref[...] + _ln_bwd(dh * (1.0 + sc_ref[...]), xh, rstd)

    blk = pl.BlockSpec((tb, d), lambda i: (i, 0))
    vec = pl.BlockSpec((1, d), _row)
    vshape = jax.ShapeDtypeStruct((1, d), f32)
    res, got = _call(body, (dh1, x, dr1, sc), name=name, grid=(s // tb,), in_specs=[blk, blk, blk, vec], out_specs=[blk, vec, vec],
                     out_shape=[jax.ShapeDtypeStruct((s, d), f32), vshape, vshape], sem=("arbitrary",), xchg=xchg)
    return tuple(res) + (got,)


TALL_ROWS = 64


def _ffn_col_tile(fh):
    return fh // 2 if (fh // 2) % LANES == 0 else fh


def _ffn_specs(s, fh, tb, tc):
    per = tb // FFN_HALO
    blk = pl.BlockSpec((tb, tc), lambda j, i: (i, j))
    prev = pl.BlockSpec((FFN_HALO, tc), lambda j, i: (jnp.maximum(i * per - 1, 0), j))
    nxt = pl.BlockSpec((FFN_HALO, tc), lambda j, i: (jnp.minimum((i + 1) * per, s // FFN_HALO - 1), j))
    taps = pl.BlockSpec((FFN_HALO, tc), lambda j, i: (0, j))
    vec = pl.BlockSpec((1, tc), lambda j, i: (0, j))
    return blk, prev, nxt, taps, vec


def _ffn_mid(upa, upv, wa, wv, ba, bv, *, name, xchg=None):
    s, fh = upa.shape
    tb, tc = _tile(s, 512), _ffn_col_tile(fh)
    blk, prev, _, taps, vec = _ffn_specs(s, fh, tb, tc)
    off = FFN_HALO - FFN_K + 1

    def body(a_ref, v_ref, ah_ref, vh_ref, wa_ref, wv_ref, ba_ref, bv_ref, o_ref):
        first = pl.program_id(1) == 0
        for lt in range(tc // LANES):
            cols = slice(lt * LANES, (lt + 1) * LANES)
            halo_a, halo_v = jnp.where(first, 0.0, ah_ref[:, cols]), jnp.where(first, 0.0, vh_ref[:, cols])
            wa = [wa_ref[pl.ds(k, 1), cols] for k in range(FFN_K)]
            wv = [wv_ref[pl.ds(k, 1), cols] for k in range(FFN_K)]
            ba, bv = ba_ref[:, cols], bv_ref[:, cols]

            def chunk(r0, carry, head=False, cols=cols, halo_a=halo_a, halo_v=halo_v, wa=wa, wv=wv, ba=ba, bv=bv):
                a2 = _taps3(_window_before(a_ref, halo_a, r0, TALL_ROWS, head, cols), wa, off, TALL_ROWS) + ba
                v2 = _taps3(_window_before(v_ref, halo_v, r0, TALL_ROWS, head, cols), wv, off, TALL_ROWS) + bv
                o_ref[pl.ds(r0, TALL_ROWS), cols] = (_gelu(a2) * v2).astype(bf16)
                return carry

            chunk(0, 0, head=True)
            _row_chunks(tb, TALL_ROWS, chunk, 0, start=1)

    (f,), got = _call(
        body, (upa, upv, upa, upv, wa, wv, ba, bv), name=name, grid=(fh // tc, s // tb),
        in_specs=[blk, blk, prev, prev, taps, taps, vec, vec], out_specs=[blk], out_shape=[jax.ShapeDtypeStruct((s, fh), bf16)],
        sem=("parallel", "arbitrary"), xchg=xchg)
    return f, got


def _ffn_mid_bwd_tile(cols, first, last, tb, off, df_ref, dfn_ref, a_ref, v_ref, ah_ref, vh_ref, an_ref, vn_ref, wa_ref, wv_ref,
                      ba_ref, bv_ref, da_ref, dv_ref, dwa_ref, dwv_ref, dba_ref, dbv_ref, dexta, dextv):
    rows_c = TALL_ROWS
    halo_a, halo_v = jnp.where(first, 0.0, ah_ref[:, cols]), jnp.where(first, 0.0, vh_ref[:, cols])
    wa = [wa_ref[pl.ds(k, 1), cols] for k in range(FFN_K)]
    wv = [wv_ref[pl.ds(k, 1), cols] for k in range(FFN_K)]
    ba, bv = ba_ref[:, cols], bv_ref[:, cols]

    def conv_cotangents(r0, rows, xa, xv, dfe):
        sa = [_rows_from(xa, off + k, rows) for k in range(FFN_K)]
        sv = [_rows_from(xv, off + k, rows) for k in range(FFN_K)]
        a2 = sa[0] * wa[0] + sa[1] * wa[1] + sa[2] * wa[2] + ba
        v2 = sv[0] * wv[0] + sv[1] * wv[1] + sv[2] * wv[2] + bv
        cdf = 0.5 * (1.0 + lax.erf(a2 * INV_SQRT2))
        da2 = dfe * v2 * (cdf + a2 * jnp.exp(-0.5 * a2 * a2) * INV_SQRT_2PI)
        dv2 = dfe * (a2 * cdf)
        dexta[pl.ds(r0, rows), cols] = da2
        dextv[pl.ds(r0, rows), cols] = dv2
        return da2, dv2, sa, sv

    def chunk(r0, sums, head=False):
        da2, dv2, sa, sv = conv_cotangents(r0, rows_c, _window_before(a_ref, halo_a, r0, rows_c, head, cols),
                                           _window_before(v_ref, halo_v, r0, rows_c, head, cols), df_ref[pl.ds(r0, rows_c), cols])
        new = [sums[k] + _fold8(da2 * sa[k]) for k in range(FFN_K)] + [sums[FFN_K] + _fold8(da2)]
        new += [sums[FFN_K + 1 + k] + _fold8(dv2 * sv[k]) for k in range(FFN_K)] + [sums[2 * FFN_K + 1] + _fold8(dv2)]
        return tuple(new)

    sums = chunk(0, tuple(jnp.zeros((SUBLANES, LANES), f32) for _ in range(2 * FFN_K + 2)), head=True)
    sums = _row_chunks(tb, rows_c, chunk, sums, start=1)
    conv_cotangents(tb, FFN_HALO,
                    jnp.concatenate([a_ref[pl.ds(tb - FFN_HALO, FFN_HALO), cols], jnp.where(last, 0.0, an_ref[:, cols])], axis=0),
                    jnp.concatenate([v_ref[pl.ds(tb - FFN_HALO, FFN_HALO), cols], jnp.where(last, 0.0, vn_ref[:, cols])], axis=0),
                    jnp.where(last, 0.0, dfn_ref[:, cols]))
    for k in range(FFN_K):
        dwa_ref[pl.ds(k, 1), cols] += _colsum(sums[k])
        dwv_ref[pl.ds(k, 1), cols] += _colsum(sums[FFN_K + 1 + k])
    dba_ref[:, cols] += _colsum(sums[FFN_K])
    dbv_ref[:, cols] += _colsum(sums[2 * FFN_K + 1])

    def back(r0, carry):
        for dext, w, o_ref in ((dexta, wa, da_ref), (dextv, wv, dv_ref)):
            dd = dext[pl.ds(r0, rows_c + FFN_HALO), cols]
            o_ref[pl.ds(r0, rows_c), cols] = (_rows_from(dd, 2, rows_c) * w[0] + _rows_from(dd, 1, rows_c) * w[1]
                                              + dd[0:rows_c] * w[2]).astype(bf16)
        return carry

    _row_chunks(tb, rows_c, back, 0)


def _ffn_mid_bwd(df, upa, upv, wa, wv, ba, bv, *, name, xchg=None):
    s, fh = upa.shape
    tb, tc = _tile(s, 512), _ffn_col_tile(fh)
    nb = s // tb
    blk, prev, nxt, taps, vec = _ffn_specs(s, fh, tb, tc)
    off = FFN_HALO - FFN_K + 1
    te = tb + FFN_HALO

    def body(df_ref, dfn_ref, a_ref, v_ref, ah_ref, vh_ref, an_ref, vn_ref, wa_ref, wv_ref, ba_ref, bv_ref,
             da_ref, dv_ref, dwa_ref, dwv_ref, dba_ref, dbv_ref, dexta, dextv):
        i = pl.program_id(1)

        @pl.when(i == 0)
        def _():
            for r in (dwa_ref, dwv_ref, dba_ref, dbv_ref):
                r[...] = jnp.zeros_like(r)

        last = i == nb - 1
        for lt in range(tc // LANES):
            _ffn_mid_bwd_tile(slice(lt * LANES, (lt + 1) * LANES), i == 0, last, tb, off, df_ref, dfn_ref, a_ref, v_ref,
                              ah_ref, vh_ref, an_ref, vn_ref, wa_ref, wv_ref, ba_ref, bv_ref, da_ref, dv_ref, dwa_ref, dwv_ref,
                              dba_ref, dbv_ref, dexta, dextv)

    act = jax.ShapeDtypeStruct((s, fh), bf16)
    wshape = jax.ShapeDtypeStruct((FFN_HALO, fh), f32)
    vshape = jax.ShapeDtypeStruct((1, fh), f32)
    res, got = _call(
        body, (df, df, upa, upv, upa, upv, upa, upv, wa, wv, ba, bv), name=name, grid=(fh // tc, nb),
        in_specs=[blk, nxt, blk, blk, prev, prev, nxt, nxt, taps, taps, vec, vec],
        out_specs=[blk, blk, taps, taps, vec, vec], out_shape=[act, act, wshape, wshape, vshape, vshape],
        scratch_shapes=[pltpu.VMEM((te, tc), f32)] * 2, sem=("parallel", "arbitrary"), xchg=xchg)
    return tuple(res) + (got,)


def _cols_from_shards(stacked):
    _, k, n = stacked.shape
    return stacked.transpose(1, 0, 2).reshape(k, NDEV * n)


def _pad_rows(w, rows):
    return jnp.pad(w, ((0, rows - w.shape[0]), (0, 0)))


def kernel(x, c, w_cond, b_cond, w_in, b_in, ssm_lambda_re, ssm_lambda_im, ssm_log_dt, ssm_b_re, ssm_b_im, ssm_c_re, ssm_c_im, ssm_d, ssm_glu_w_a, ssm_glu_w_b, cv_dw_w, cv_dw_b, cv_ln_g, cv_ln_b, cv_w_pw, w_out, ln1_g, ln1_b, ffn_w_up, ffn_dw_w, ffn_dw_b, ffn_w_down, ln2_g, ln2_b, loss_target, m_w_cond, m_b_cond, m_w_in, m_b_in, m_ssm_lambda_re, m_ssm_lambda_im, m_ssm_log_dt, m_ssm_b_re, m_ssm_b_im, m_ssm_c_re, m_ssm_c_im, m_ssm_d, m_ssm_glu_w_a, m_ssm_glu_w_b, m_cv_dw_w, m_cv_dw_b, m_cv_ln_g, m_cv_ln_b, m_cv_w_pw, m_w_out, m_ln1_g, m_ln1_b, m_ffn_w_up, m_ffn_dw_w, m_ffn_dw_b, m_ffn_w_down, m_ln2_g, m_ln2_b, v_w_cond, v_b_cond, v_w_in, v_b_in, v_ssm_lambda_re, v_ssm_lambda_im, v_ssm_log_dt, v_ssm_b_re, v_ssm_b_im, v_ssm_c_re, v_ssm_c_im, v_ssm_d, v_ssm_glu_w_a, v_ssm_glu_w_b, v_cv_dw_w, v_cv_dw_b, v_cv_ln_g, v_cv_ln_b, v_cv_w_pw, v_w_out, v_ln1_g, v_ln1_b, v_ffn_w_up, v_ffn_dw_w, v_ffn_dw_b, v_ffn_w_down, v_ln2_g, v_ln2_b):
    weights = dict(w_cond=w_cond, b_cond=b_cond, w_in=w_in, b_in=b_in, ssm_lambda_re=ssm_lambda_re, ssm_lambda_im=ssm_lambda_im, ssm_log_dt=ssm_log_dt, ssm_b_re=ssm_b_re, ssm_b_im=ssm_b_im, ssm_c_re=ssm_c_re, ssm_c_im=ssm_c_im, ssm_d=ssm_d, ssm_glu_w_a=ssm_glu_w_a, ssm_glu_w_b=ssm_glu_w_b, cv_dw_w=cv_dw_w, cv_dw_b=cv_dw_b, cv_ln_g=cv_ln_g, cv_ln_b=cv_ln_b, cv_w_pw=cv_w_pw, w_out=w_out, ln1_g=ln1_g, ln1_b=ln1_b, ffn_w_up=ffn_w_up, ffn_dw_w=ffn_dw_w, ffn_dw_b=ffn_dw_b, ffn_w_down=ffn_w_down, ln2_g=ln2_g, ln2_b=ln2_b)
    mom_m = dict(w_cond=m_w_cond, b_cond=m_b_cond, w_in=m_w_in, b_in=m_b_in, ssm_lambda_re=m_ssm_lambda_re, ssm_lambda_im=m_ssm_lambda_im, ssm_log_dt=m_ssm_log_dt, ssm_b_re=m_ssm_b_re, ssm_b_im=m_ssm_b_im, ssm_c_re=m_ssm_c_re, ssm_c_im=m_ssm_c_im, ssm_d=m_ssm_d, ssm_glu_w_a=m_ssm_glu_w_a, ssm_glu_w_b=m_ssm_glu_w_b, cv_dw_w=m_cv_dw_w, cv_dw_b=m_cv_dw_b, cv_ln_g=m_cv_ln_g, cv_ln_b=m_cv_ln_b, cv_w_pw=m_cv_w_pw, w_out=m_w_out, ln1_g=m_ln1_g, ln1_b=m_ln1_b, ffn_w_up=m_ffn_w_up, ffn_dw_w=m_ffn_dw_w, ffn_dw_b=m_ffn_dw_b, ffn_w_down=m_ffn_w_down, ln2_g=m_ln2_g, ln2_b=m_ln2_b)
    mom_v = dict(w_cond=v_w_cond, b_cond=v_b_cond, w_in=v_w_in, b_in=v_b_in, ssm_lambda_re=v_ssm_lambda_re, ssm_lambda_im=v_ssm_lambda_im, ssm_log_dt=v_ssm_log_dt, ssm_b_re=v_ssm_b_re, ssm_b_im=v_ssm_b_im, ssm_c_re=v_ssm_c_re, ssm_c_im=v_ssm_c_im, ssm_d=v_ssm_d, ssm_glu_w_a=v_ssm_glu_w_a, ssm_glu_w_b=v_ssm_glu_w_b, cv_dw_w=v_cv_dw_w, cv_dw_b=v_cv_dw_b, cv_ln_g=v_cv_ln_g, cv_ln_b=v_cv_ln_b, cv_w_pw=v_cv_w_pw, w_out=v_w_out, ln1_g=v_ln1_g, ln1_b=v_ln1_b, ffn_w_up=v_ffn_w_up, ffn_dw_w=v_ffn_dw_w, ffn_dw_b=v_ffn_dw_b, ffn_w_down=v_ffn_w_down, ln2_g=v_ln2_g, ln2_b=v_ln2_b)
    names = list(weights)

    s, d = x.shape[1], x.shape[2]
    sw = cw = d // 2
    fh = ffn_w_down.shape[1] * NDEV
    ng, nq = sw // SSM_GROUP, sw // QW
    gq = ng // nq
    alpha = 2.0 ** 0.25
    me = 4 * lax.axis_index("x") + 2 * lax.axis_index("y") + lax.axis_index("c")
    xs, tgt = x[0], loss_target[0]

    col_names = ["w_in", "ssm_glu_w_a", "ssm_glu_w_b", "cv_w_pw", "ffn_w_up"]
    row_names = ["w_out", "ffn_w_down"]
    big = col_names + row_names
    sent = lambda ns: [weights[n][0].astype(bf16) for n in ns]
    got_in, got_c, got_cv_taps, got_ffn_taps = _exchange(sent(["w_in"]) + [c, cv_dw_w[0, :, 0], ffn_dw_w[0, :, 0]],
                                                         scatter=False, name="gather_in")
    o1, o2, o3, o4 = sw, sw + cw, sw + 2 * cw, sw + 2 * cw + d
    in_bounds = ((0, o1), (o1, o2), (o2, o3), (o3, o4), (o4, o4 + d))
    w_u, w_cva, w_cvg, w_gs, w_gc = _unshard_cols(got_in, in_bounds, name="unshard_w_in")
    b_u, b_cva, b_cvg, b_gs, b_gc = (b_in[:, a:b] for a, b in in_bounds)
    c_all = got_c.reshape(NDEV, d)
    cv_taps = _cols_from_shards(got_cv_taps)
    ffn_taps = _cols_from_shards(got_ffn_taps)
    cv_w32 = _pad_rows(cv_taps, CONV_HALO)
    ffn_wa, ffn_wv = _pad_rows(ffn_taps[:, :fh], FFN_HALO), _pad_rows(ffn_taps[:, fh:], FFN_HALO)
    ffn_ba, ffn_bv = ffn_dw_b[:, :fh], ffn_dw_b[:, fh:]

    ncond = w_cond.shape[2]
    b_cond_mine = lax.dynamic_slice(b_cond, (0, me * ncond), (1, ncond))
    mod_cols = _cond_fwd(c_all, w_cond[0], b_cond_mine, name="cond_fwd")
    mod_all, = _exchange([mod_cols], scatter=False, name="gather_mod")
    mod_mine = lax.dynamic_slice(mod_all, (0, me, 0), (NDEV, 1, ncond)).reshape(1, 6 * d)
    sh1, sc1, g1, sh2, sc2, g2 = (mod_mine[:, k * d:(k + 1) * d] for k in range(6))

    lam_re, lam_im, log_dt = ssm_lambda_re[0], ssm_lambda_im[0], ssm_log_dt[0][:, None]
    lbr, lbi, cfr, cfi = _ssm_prep(lam_re, lam_im, log_dt, name="ssm_prep")
    rows_q = lambda a: a.reshape(nq, 1, QS)
    eye = jnp.eye(gq, dtype=f32)

    def b_mat(b):
        bt = b.reshape(nq, gq, SSM_STATE, SSM_GROUP).transpose(0, 1, 3, 2)
        return jnp.einsum("qgpn,gh->qgphn", bt, eye).reshape(nq, QW, QS)

    def c_mat(cc):
        ct = cc.reshape(nq, gq, SSM_GROUP, SSM_STATE)
        return jnp.einsum("qgpn,gh->qhngp", ct, eye).reshape(nq, QS, QW)

    def b_unmat(mt):
        return jnp.einsum("qgpgn->qgnp", mt.reshape(nq, gq, SSM_GROUP, gq, SSM_STATE)).reshape(ng, SSM_STATE, SSM_GROUP)

    def c_unmat(mt):
        return jnp.einsum("qgngp->qgpn", mt.reshape(nq, gq, SSM_STATE, gq, SSM_GROUP)).reshape(ng, SSM_GROUP, SSM_STATE)

    ssm_args = (rows_q(lbr), rows_q(lbi), b_mat(ssm_b_re[0]), b_mat(ssm_b_im[0]), c_mat(ssm_c_re[0]), c_mat(ssm_c_im[0]),
                rows_q(cfr), rows_q(cfi), ssm_d[0].reshape(1, sw))

    h1 = _lnmod(xs, sc1, sh1, name="ln_mod1")
    u, cva, cvg, gs, gc = _mm_fanout(h1, [w_u, w_cva, w_cvg, w_gs, w_gc], [b_u, b_cva, b_cvg, b_gs, b_gc],
                                     [f32, f32, f32, bf16, bf16], name="in_proj")
    v4, cv2, (got_a, got_b, got_pw, got_o) = _conv_fwd(
        cva, cvg, cv_w32, cv_dw_b, cv_ln_g, cv_ln_b, name="conv_fwd",
        xchg=(sent(["ssm_glu_w_a", "ssm_glu_w_b", "cv_w_pw", "w_out"]), False))
    h_p, yraw3, y, (got_up,) = _ssm_fwd(u, *ssm_args, name="ssm_fwd", xchg=(sent(["ffn_w_up"]), False))
    w_a, = _unshard_cols(got_a, ((0, d),), name="unshard_glu_a")
    w_b, = _unshard_cols(got_b, ((0, d),), name="unshard_glu_b")
    w_pw, = _unshard_cols(got_pw, ((0, d),), name="unshard_conv_pw")
    w_upa, w_upv = _unshard_cols(got_up, ((0, fh), (fh, 2 * fh)), name="unshard_ffn_up")
    w_o = got_o.reshape(d, d)
    ya, yb = _mm_fanout(y, [w_a, w_b], None, [bf16, bf16], name="glu")
    ycv = _mm([(v4, w_pw)], out_dtype=bf16, name="conv_pw")
    merged = _glu_merge(ya, yb, ycv, gs, gc, name="merge")
    o = _mm([(merged, w_o)], out_dtype=bf16, name="out_proj")
    x1, h2 = _resid_ln_mod(xs, o, g1, ln1_g, ln1_b, sc2, sh2, alpha, name="resid_ln1")
    upa = _mm([(h2, w_upa)], name="ffn_up_a")
    upv = _mm([(h2, w_upv)], name="ffn_up_v")
    f, (got_dn,) = _ffn_mid(upa, upv, ffn_wa, ffn_wv, ffn_ba, ffn_bv, name="ffn_mid", xchg=(sent(["ffn_w_down"]), False))
    w_dn = got_dn.reshape(fh, d)
    dr2, dy2, loss_part, d_ln2_g, d_ln2_b, d_g2 = _resid_ln_loss(x1, f, w_dn, g2, ln2_g, ln2_b, tgt, alpha, name="ffn_down_ln2_loss")

    gw = {}
    df = _mm([(dy2, w_dn)], trans_w=True, name="d_ffn_down")
    gw["ffn_w_down"] = _mm_tn(f, dy2, out_dtype=bf16, name="g_ffn_down").reshape((NDEV,) + ffn_w_down[0].shape)
    received = {}
    dupa, dupv, d_ffn_wa, d_ffn_wv, d_ffn_ba, d_ffn_bv, (received["ffn_w_down"],) = _ffn_mid_bwd(
        df, upa, upv, ffn_wa, ffn_wv, ffn_ba, ffn_bv, name="ffn_mid_bwd", xchg=([gw["ffn_w_down"]], True))
    dh2 = _mm([(dupa, w_upa), (dupv, w_upv)], trans_w=True, out_dtype=bf16, name="d_ffn_up")
    gw["ffn_w_up"] = _shard_cols([_mm_tn(h2, dupa, name="g_ffn_up_a"), _mm_tn(h2, dupv, name="g_ffn_up_v")], out_dtype=bf16,
                                 name="shard_ffn_up")
    dr1, do, d_sc2, d_sh2, d_ln1_g, d_ln1_b, d_g1 = _mid_bwd(dh2, x1, dr2, xs, o, g1, sc2, ln1_g, alpha, name="mid_bwd")
    dmerged = _mm([(do, w_o)], trans_w=True, out_dtype=bf16, name="d_out_proj")
    gw["w_out"] = _mm_tn(merged, do, out_dtype=bf16, name="g_out_proj").reshape((NDEV,) + w_out[0].shape)
    dya, dyb, dycv, dgs, dgc, s_gs, s_gc = _glu_merge_bwd(dmerged, ya, yb, ycv, gs, gc, name="merge_bwd")
    dy = _mm([(dya, w_a), (dyb, w_b)], trans_w=True, out_dtype=bf16, name="d_glu")
    gw["ssm_glu_w_a"] = _mm_tn_sharded(y, [dya], out_dtype=bf16, name="g_glu_a")
    gw["ssm_glu_w_b"] = _mm_tn_sharded(y, [dyb], out_dtype=bf16, name="g_glu_b")
    dv4 = _mm([(dycv, w_pw)], trans_w=True, out_dtype=bf16, name="d_conv_pw")
    gw["cv_w_pw"] = _mm_tn_sharded(v4, [dycv], out_dtype=bf16, name="g_conv_pw")
    dv2, d_cv_ln_g, d_cv_ln_b = _conv_bwd_ln(dv4, cv2, cv_ln_g, cv_ln_b, name="conv_bwd_ln")
    dcva, dcvg, d_cv_w32, d_cv_b, s_cva, s_cvg, (received["ffn_w_up"],) = _conv_bwd_taps(
        dv2, cva, cvg, cv_w32, name="conv_bwd_taps", xchg=([gw["ffn_w_up"]], True))
    late = ["w_out", "ssm_glu_w_a", "ssm_glu_w_b", "cv_w_pw"]
    (du, d_bre_m, d_bim_m, d_cre_m, d_cim_m, d_cfr, d_cfi, d_lbr, d_lbi, d_d, s_u, got_late) = _ssm_bwd(
        dy, yraw3, u, h_p, *ssm_args, name="ssm_bwd", xchg=([gw[n] for n in late], True))
    received.update(zip(late, got_late))
    gshape = lam_re.shape
    d_lam_re, d_lam_im, d_log_dt = _ssm_prep_bwd(
        lam_re, lam_im, log_dt, [a.reshape(gshape) for a in (d_lbr, d_lbi, d_cfr, d_cfi)], name="ssm_prep_bwd")
    small = {
        "b_in": jnp.concatenate([s_u, s_cva, s_cvg, s_gs, s_gc], axis=1),
        "ssm_lambda_re": d_lam_re, "ssm_lambda_im": d_lam_im, "ssm_log_dt": d_log_dt,
        "ssm_b_re": b_unmat(d_bre_m), "ssm_b_im": b_unmat(d_bim_m), "ssm_c_re": c_unmat(d_cre_m), "ssm_c_im": c_unmat(d_cim_m),
        "ssm_d": d_d, "cv_dw_w": d_cv_w32[:CONV_K], "cv_dw_b": d_cv_b, "cv_ln_g": d_cv_ln_g, "cv_ln_b": d_cv_ln_b,
        "ln1_g": d_ln1_g, "ln1_b": d_ln1_b,
        "ffn_dw_w": jnp.concatenate([d_ffn_wa[:FFN_K], d_ffn_wv[:FFN_K]], axis=1),
        "ffn_dw_b": jnp.concatenate([d_ffn_ba, d_ffn_bv], axis=1), "ln2_g": d_ln2_g, "ln2_b": d_ln2_b,
        "mod_g1": d_g1, "mod_sh2": d_sh2, "mod_sc2": d_sc2, "mod_g2": d_g2, "loss": loss_part,
    }
    small_names = list(small)
    small_shapes = [small[n].shape for n in small_names]
    gw["w_in"], (small_all,) = _mm_tn_sharded(h1, [du, dcva, dcvg, dgs, dgc], out_dtype=bf16, name="g_in",
                                              xchg=([_pack([small[n] for n in small_names])], False))
    dh1, (received["w_in"],) = _mm(
        [(du, w_u), (dcva, w_cva), (dcvg, w_cvg), (dgs, w_gs), (dgc, w_gc)], trans_w=True, out_dtype=bf16, name="d_in",
        xchg=([gw["w_in"]], True))
    grad_x, d_sc1, d_sh1, _ = _final_bwd(dh1, xs, dr1, sc1, alpha, name="final_bwd")

    grads, delta, new_m, new_v = {}, {}, {}, {}
    for n in big:
        ride = ([_pack([d_sh1, d_sc1])], False) if n == "w_out" else None
        res = _sum_adamw(received[n], weights[n][0], mom_m[n][0], mom_v[n][0], name="adamw_" + n, xchg=ride)
        grads[n], delta[n], new_m[n], new_v[n] = res[:4]
        if ride is not None:
            last_all, = res[4]

    small_sum = dict(zip(small_names, _unpack(_sum_parts(small_all, name="sum_small").reshape(-1), small_shapes)))
    last_sum = _unpack(_sum_parts(last_all, name="sum_last").reshape(-1), [(1, d), (1, d)])
    per_dev = dict(zip(small_names, _unpack(small_all.reshape(NDEV, -1), small_shapes)))
    last_dev = _unpack(last_all.reshape(NDEV, -1), [(1, d), (1, d)])
    dmod_all = jnp.concatenate(last_dev + [per_dev[k] for k in ("mod_g1", "mod_sh2", "mod_sc2", "mod_g2")], axis=-1).reshape(NDEV, 6 * d)
    dmod_cols = lax.dynamic_slice(dmod_all.reshape(NDEV, NDEV, ncond), (0, me, 0), (NDEV, 1, ncond)).reshape(NDEV, ncond)
    grads["w_cond"] = _cond_bwd(c_all, dmod_cols, name="cond_bwd")
    loss = small_sum.pop("loss").reshape(())
    grads["b_cond"] = jnp.concatenate(last_sum + [small_sum.pop(k) for k in ("mod_g1", "mod_sh2", "mod_sc2", "mod_g2")], axis=1)
    for n, g in small_sum.items():
        grads[n] = g
    ntap = cv_dw_w.shape[3]
    grads["cv_dw_w"] = lax.dynamic_slice(grads["cv_dw_w"], (0, me * ntap), (CONV_K, ntap))
    nffn = ffn_dw_w.shape[3]
    grads["ffn_dw_w"] = lax.dynamic_slice(grads["ffn_dw_w"], (0, me * nffn), (FFN_K, nffn))
    grads = {n: grads[n].reshape(weights[n].shape) for n in names}

    delta["w_cond"], new_m["w_cond"], new_v["w_cond"] = _adamw(w_cond[0], grads["w_cond"][0], m_w_cond[0], v_w_cond[0],
                                                               name="adamw_w_cond")
    rest = [n for n in names if n not in ["w_cond"] + big]
    squeeze = lambda a: a if a.ndim == 2 else a[0]
    results = _adamw_many(*[[squeeze(t[n].reshape(weights[n].shape)) for n in rest] for t in (weights, grads, mom_m, mom_v)],
                          name="adamw_small")
    for n, (dl, nm, nv) in zip(rest, results):
        delta[n], new_m[n], new_v[n] = dl, nm, nv
    shaped = lambda t: [t[n].reshape(weights[n].shape) for n in names]

    return (loss, grad_x[None], *shaped(grads), *shaped(delta), *shaped(new_m), *shaped(new_v))
```

```python
import functools
import math

import jax
import jax.numpy as jnp
from jax import lax
from jax.experimental import pallas as pl
from jax.experimental.pallas import tpu as pltpu

f32 = jnp.float32
bf16 = jnp.bfloat16

NDEV = 8
LANES = 128
SUBLANES = 8
SSM_GROUP = 16
SSM_STATE = 64
QW = 128
QS = 512
CONV_K = 31
CONV_HALO = 32
FFN_K = 3
FFN_HALO = 8
LN_EPS = 1e-5
ADAM_LR, ADAM_B1, ADAM_B2, ADAM_EPS, ADAM_WD, ADAM_STEP = 0.001, 0.9, 0.999, 1e-08, 0.01, 10
VMEM_LIMIT = 56 * 1024 * 1024
W_TILE_BYTES = 6 * 1024 * 1024
MM_ROWS = 1024
EW_BLOCK_BYTES = 2 * 1024 * 1024
INV_SQRT2 = 1.0 / math.sqrt(2.0)
INV_SQRT_2PI = 1.0 / math.sqrt(2.0 * math.pi)
MESH = pl.DeviceIdType.MESH


def _tile(n, want):
    t = min(n, want)
    while n % t:
        t //= 2
    return t


def _col_tile(n, rows, bytes_per):
    best = LANES if n % LANES == 0 else n
    for t in range(LANES, n + 1, LANES):
        if n % t == 0 and rows * t * bytes_per <= W_TILE_BYTES:
            best = t
    return best


def _params(*sem):
    return pltpu.CompilerParams(dimension_semantics=sem, vmem_limit_bytes=VMEM_LIMIT)


def _row(i):
    return (0, 0)


def _full(shape):
    nd = len(shape)
    return pl.BlockSpec(shape, lambda *a: (0,) * nd)


def _ln(x):
    mu = jnp.mean(x, axis=-1, keepdims=True)
    xc = x - mu
    var = jnp.mean(xc * xc, axis=-1, keepdims=True)
    rstd = lax.rsqrt(var + LN_EPS)
    return xc * rstd, rstd


def _ln_bwd(dxhat, xhat, rstd):
    return rstd * (dxhat - jnp.mean(dxhat, axis=-1, keepdims=True) - xhat * jnp.mean(dxhat * xhat, axis=-1, keepdims=True))


def _sig(x):
    return 1.0 / (1.0 + jnp.exp(-x))


def _gelu(x):
    return 0.5 * x * (1.0 + lax.erf(x * INV_SQRT2))


def _gelu_grad(x):
    return 0.5 * (1.0 + lax.erf(x * INV_SQRT2)) + x * jnp.exp(-0.5 * x * x) * INV_SQRT_2PI


def _colsum(x):
    return jnp.sum(x, axis=0, keepdims=True)


def _mm(pairs, bias=None, *, trans_w=False, out_dtype=f32, name, xchg=None):
    n_p = len(pairs)
    m = pairs[0][0].shape[0]
    n = pairs[0][1].shape[0 if trans_w else 1]
    ktot = sum(x.shape[1] for x, _ in pairs)
    tm = _tile(m, MM_ROWS)
    tn = _col_tile(n, ktot, 2)
    dn = (((1,), (1,)), ((), ())) if trans_w else (((1,), (0,)), ((), ()))

    def body(*refs):
        o_ref = refs[-1]
        acc = None
        for xr, wr in zip(refs[:n_p], refs[n_p:2 * n_p]):
            r = lax.dot_general(xr[...].astype(bf16), wr[...].astype(bf16), dn, preferred_element_type=f32)
            acc = r if acc is None else acc + r
        if bias is not None:
            acc = acc + refs[2 * n_p][...]
        o_ref[...] = acc.astype(out_dtype)

    in_specs = [pl.BlockSpec((tm, x.shape[1]), lambda j, i: (i, 0)) for x, _ in pairs]
    if trans_w:
        in_specs += [pl.BlockSpec((tn, w.shape[1]), lambda j, i: (j, 0)) for _, w in pairs]
    else:
        in_specs += [pl.BlockSpec((w.shape[0], tn), lambda j, i: (0, j)) for _, w in pairs]
    args = [x for x, _ in pairs] + [w for _, w in pairs]
    if bias is not None:
        in_specs.append(pl.BlockSpec((1, tn), lambda j, i: (0, j)))
        args.append(bias)
    (out,), got = _call(
        body, args, name=name, grid=(n // tn, m // tm), in_specs=in_specs,
        out_specs=[pl.BlockSpec((tm, tn), lambda j, i: (i, j))], out_shape=[jax.ShapeDtypeStruct((m, n), out_dtype)],
        sem=("parallel", "arbitrary"), xchg=xchg)
    return out if xchg is None else (out, got)


def _mm_fanout(x, ws, biases, out_dtypes, *, name):
    m, k = x.shape
    tm = _tile(m, MM_ROWS)
    n_w = len(ws)
    biases = list(biases or [])

    def body(x_ref, *refs):
        xb = x_ref[...].astype(bf16)
        o_refs = refs[n_w + len(biases):]
        for p, (w_ref, o_ref, dt) in enumerate(zip(refs[:n_w], o_refs, out_dtypes)):
            acc = jnp.dot(xb, w_ref[...], preferred_element_type=f32)
            if biases:
                acc = acc + refs[n_w + p][...]
            o_ref[...] = acc.astype(dt)

    return pl.pallas_call(
        body, name=name, grid=(m // tm,),
        in_specs=[pl.BlockSpec((tm, k), lambda i: (i, 0))] + [_full(w.shape) for w in ws] + [_full(b.shape) for b in biases],
        out_specs=[pl.BlockSpec((tm, w.shape[1]), lambda i: (i, 0)) for w in ws],
        out_shape=[jax.ShapeDtypeStruct((m, w.shape[1]), dt) for w, dt in zip(ws, out_dtypes)],
        compiler_params=_params("parallel"))(x, *ws, *biases)


def _mm_tn(x, dy, *, out_dtype=f32, name):
    m, k = x.shape
    n = dy.shape[1]
    tm = _tile(m, MM_ROWS)
    tn = _col_tile(n, k, 4)
    steps = m // tm

    def body(x_ref, dy_ref, o_ref, *scratch):
        acc = scratch[0] if scratch else o_ref

        @pl.when(pl.program_id(1) == 0)
        def _():
            acc[...] = jnp.zeros_like(acc)

        acc[...] += lax.dot_general(x_ref[...].astype(bf16), dy_ref[...].astype(bf16), (((0,), (0,)), ((), ())),
                                    preferred_element_type=f32)
        if scratch:
            @pl.when(pl.program_id(1) == steps - 1)
            def _():
                o_ref[...] = acc[...].astype(out_dtype)

    return pl.pallas_call(
        body, name=name, grid=(n // tn, steps),
        in_specs=[pl.BlockSpec((tm, k), lambda j, i: (i, 0)), pl.BlockSpec((tm, tn), lambda j, i: (i, j))],
        out_specs=pl.BlockSpec((k, tn), lambda j, i: (0, j)),
        out_shape=jax.ShapeDtypeStruct((k, n), out_dtype),
        scratch_shapes=[] if out_dtype == f32 else [pltpu.VMEM((k, tn), f32)],
        compiler_params=_params("parallel", "arbitrary"),
    )(x, dy)


def _mm_tn_sharded(x, dys, *, out_dtype, name, xchg=None):
    m, k = x.shape
    widths = [dy.shape[1] for dy in dys]
    n = sum(widths) // NDEV
    tm = _tile(m, 512)
    steps = m // tm
    n_d = len(dys)

    def body(x_ref, *refs):
        dy_refs, o_ref, acc = refs[:n_d], refs[n_d], refs[n_d + 1]
        i = pl.program_id(0)

        @pl.when(i == 0)
        def _():
            acc[...] = jnp.zeros_like(acc)

        xb = x_ref[...].astype(bf16)
        off = 0
        for dy_ref, w in zip(dy_refs, widths):
            acc[:, off:off + w] += lax.dot_general(xb, dy_ref[...].astype(bf16), (((0,), (0,)), ((), ())), preferred_element_type=f32)
            off += w

        @pl.when(i == steps - 1)
        def _():
            for j in range(NDEV):
                o_ref[j] = acc[:, n * j:n * (j + 1)].astype(out_dtype)

    (out,), got = _call(
        body, (x, *dys), name=name, grid=(steps,),
        in_specs=[pl.BlockSpec((tm, k), lambda i: (i, 0))] + [pl.BlockSpec((tm, w), lambda i: (i, 0)) for w in widths],
        out_specs=[pl.BlockSpec((NDEV, k, n), lambda i: (0, 0, 0))], out_shape=[jax.ShapeDtypeStruct((NDEV, k, n), out_dtype)],
        scratch_shapes=[pltpu.VMEM((k, sum(widths)), f32)], sem=("arbitrary",), xchg=xchg)
    return out if xchg is None else (out, got)


def _exchange(arrs, *, scatter, name):
    n = len(arrs)

    def body(*refs):
        _exchange_copies(refs[:n], refs[n:2 * n], refs[2 * n:], scatter, True, True)

    return pl.pallas_call(
        body, name=name, in_specs=[HBM_SPEC] * n, out_specs=[HBM_SPEC] * n, out_shape=_exchange_out_shape(arrs, scatter),
        scratch_shapes=_exchange_sems(n),
    )(*arrs)


HBM_SPEC = pl.BlockSpec(memory_space=pltpu.HBM)


def _flags(scatter, n):
    return list(scatter) if isinstance(scatter, (list, tuple)) else [scatter] * n


def _exchange_out_shape(arrs, scatter):
    return [jax.ShapeDtypeStruct(a.shape if sc else (NDEV,) + a.shape, a.dtype) for a, sc in zip(arrs, _flags(scatter, len(arrs)))]


def _exchange_sems(n):
    return [pltpu.SemaphoreType.DMA(((NDEV - 1) * n,)), pltpu.SemaphoreType.DMA(((NDEV - 1) * n,)), pltpu.SemaphoreType.DMA((n,))]


def _exchange_copies(x_refs, o_refs, sems, scatter, start, wait):
    n = len(x_refs)
    flags = _flags(scatter, n)
    send_sems, recv_sems, local_sems = sems
    ix, iy, ic = lax.axis_index("x"), lax.axis_index("y"), lax.axis_index("c")
    me = 4 * ix + 2 * iy + ic
    local = [pltpu.make_async_copy(x.at[me] if sc else x, o.at[me], local_sems.at[a])
             for a, (x, o, sc) in enumerate(zip(x_refs, o_refs, flags))]

    def peer(k):
        return (1 - ix if k & 4 else ix, 1 - iy if k & 2 else iy, 1 - ic if k & 1 else ic)

    def index(p):
        return 4 * p[0] + 2 * p[1] + p[2]

    def copy(a, k, src, dst, to):
        sem = (k - 1) * n + a
        return pltpu.make_async_remote_copy(src_ref=src, dst_ref=dst, send_sem=send_sems.at[sem], recv_sem=recv_sems.at[sem],
                                            device_id=to, device_id_type=MESH)

    sends, arrivals, passed_on = [], [], []
    for a, (x, o, sc) in enumerate(zip(x_refs, o_refs, flags)):
        if sc:
            for k in range(1, NDEV):
                p = peer(k)
                sends.append(copy(a, k, x.at[index(p)], o.at[me], p))
                arrivals.append(copy(a, k, x.at[me], o.at[index(p)], p))
        else:
            sib = peer(1)
            sends.append(copy(a, 1, x, o.at[me], sib))
            arrivals.append(copy(a, 1, x, o.at[index(sib)], sib))
            for k in (2, 4, 6):
                p, q = peer(k), peer(k + 1)
                sends.append(copy(a, k, x, o.at[me], p))
                passed_on.append((copy(a, k, x, o.at[index(p)], p), copy(a, k + 1, o.at[index(p)], o.at[index(p)], sib)))
                arrivals.append(copy(a, k + 1, o.at[index(q)], o.at[index(q)], sib))
    if start:
        for cp in local + sends:
            cp.start()
    if wait:
        for landed, hand_over in passed_on:
            landed.wait_recv()
            hand_over.start()
        for cp in arrivals:
            cp.wait_recv()
        for cp in sends + [hand_over for _, hand_over in passed_on]:
            cp.wait_send()
        for cp in local:
            cp.wait()


def _call(body, args, *, name, grid, in_specs, out_specs, out_shape, scratch_shapes=(), sem, xchg=None):
    if xchg is None:
        return pl.pallas_call(body, name=name, grid=grid, in_specs=in_specs, out_specs=out_specs, out_shape=out_shape,
                              scratch_shapes=list(scratch_shapes), compiler_params=_params(*sem))(*args), None
    arrs, scatter = xchg
    n, ni, no, ns = len(arrs), len(in_specs), len(out_specs), len(scratch_shapes)

    def wrapped(*refs):
        ins, x_refs = refs[:ni], refs[ni:ni + n]
        outs, o_refs = refs[ni + n:ni + n + no], refs[ni + n + no:ni + 2 * n + no]
        scratch, sems = refs[ni + 2 * n + no:ni + 2 * n + no + ns], refs[ni + 2 * n + no + ns:]
        ids = [pl.program_id(a) for a in range(len(grid))]
        first = functools.reduce(jnp.logical_and, [p == 0 for p in ids])
        last = functools.reduce(jnp.logical_and, [p == g - 1 for p, g in zip(ids, grid)])

        @pl.when(first)
        def _():
            _exchange_copies(x_refs, o_refs, sems, scatter, True, False)

        body(*ins, *outs, *scratch)

        @pl.when(last)
        def _():
            _exchange_copies(x_refs, o_refs, sems, scatter, False, True)

    res = pl.pallas_call(
        wrapped, name=name, grid=grid, in_specs=list(in_specs) + [HBM_SPEC] * n, out_specs=list(out_specs) + [HBM_SPEC] * n,
        out_shape=list(out_shape) + _exchange_out_shape(arrs, scatter),
        scratch_shapes=list(scratch_shapes) + _exchange_sems(n),
        compiler_params=_params(*("arbitrary",) * len(grid)))(*args, *arrs)
    return res[:no], res[no:]


def _sum_parts(parts, *, name):
    r = parts.shape[1]

    def body(p_ref, o_ref):
        acc = p_ref[0]
        for j in range(1, NDEV):
            acc = acc + p_ref[j]
        o_ref[...] = acc

    return pl.pallas_call(body, name=name, out_shape=jax.ShapeDtypeStruct((r, LANES), f32), compiler_params=_params())(parts)


def _col_pieces(n, bounds):
    out = []
    for p, (a, b) in enumerate(bounds):
        for j in range(NDEV):
            lo, hi = max(a, n * j), min(b, n * (j + 1))
            if lo < hi:
                out.append((p, j, lo - a, lo - n * j, hi - lo))
    return out


def _unshard_cols(stacked, bounds, *, name):
    _, k, n = stacked.shape
    tk = _tile(k, 256)
    plan = _col_pieces(n, bounds)

    def body(x_ref, *o_refs):
        for p, j, po, so, w in plan:
            o_refs[p][:, po:po + w] = x_ref[j, :, so:so + w]

    return pl.pallas_call(
        body, name=name, grid=(k // tk,), in_specs=[pl.BlockSpec((NDEV, tk, n), lambda i: (0, i, 0))],
        out_specs=[pl.BlockSpec((tk, b - a), lambda i: (i, 0)) for a, b in bounds],
        out_shape=[jax.ShapeDtypeStruct((k, b - a), stacked.dtype) for a, b in bounds],
        compiler_params=_params("parallel"))(stacked)


def _shard_cols(pieces, *, out_dtype, name):
    k = pieces[0].shape[0]
    bounds, off = [], 0
    for p in pieces:
        bounds.append((off, off + p.shape[1]))
        off += p.shape[1]
    n = off // NDEV
    tk = _tile(k, 256)
    plan = _col_pieces(n, bounds)

    def body(*refs):
        o_ref = refs[-1]
        for p, j, po, so, w in plan:
            o_ref[j, :, so:so + w] = refs[p][:, po:po + w].astype(out_dtype)

    return pl.pallas_call(
        body, name=name, grid=(k // tk,), in_specs=[pl.BlockSpec((tk, b - a), lambda i: (i, 0)) for a, b in bounds],
        out_specs=pl.BlockSpec((NDEV, tk, n), lambda i: (0, i, 0)),
        out_shape=jax.ShapeDtypeStruct((NDEV, k, n), out_dtype),
        compiler_params=_params("parallel"))(*pieces)


def _pack(arrs):
    flat = jnp.concatenate([a.reshape(-1) for a in arrs])
    pad = (-flat.shape[0]) % (SUBLANES * LANES)
    return jnp.pad(flat, (0, pad)).reshape(-1, LANES)


def _unpack(flat, shapes):
    out, off = [], 0
    for s in shapes:
        n = math.prod(s)
        out.append(flat[..., off:off + n].reshape(flat.shape[:-1] + tuple(s)))
        off += n
    return out


def _adamw_math(w, gg, m, v):
    nm = ADAM_B1 * m + (1.0 - ADAM_B1) * gg
    nv = ADAM_B2 * v + (1.0 - ADAM_B2) * (gg * gg)
    m_hat = nm / (1.0 - ADAM_B1 ** ADAM_STEP)
    v_hat = nv / (1.0 - ADAM_B2 ** ADAM_STEP)
    return -ADAM_LR * (m_hat / (jnp.sqrt(v_hat) + ADAM_EPS) + ADAM_WD * w), nm, nv


def _row_block(r, c, copies):
    tr = r
    while copies * tr * c * 4 > EW_BLOCK_BYTES and tr % (4 * SUBLANES) == 0:
        tr //= 2
    return tr


def _adamw(w, g, m, v, *, name):
    r, c = w.shape
    tr = _row_block(r, c, 1)

    def body(w_ref, g_ref, m_ref, v_ref, d_ref, nm_ref, nv_ref):
        d_ref[...], nm_ref[...], nv_ref[...] = _adamw_math(w_ref[...], g_ref[...], m_ref[...], v_ref[...])

    spec = pl.BlockSpec((tr, c), lambda i: (i, 0))
    shp = jax.ShapeDtypeStruct((r, c), f32)
    return pl.pallas_call(
        body, name=name, grid=(r // tr,), in_specs=[spec] * 4, out_specs=[spec] * 3, out_shape=[shp] * 3,
        compiler_params=_params("parallel"),
    )(w, g, m, v)


def _adamw_many(ws, gs, ms, vs, *, name):
    n = len(ws)

    def body(*refs):
        outs = refs[4 * n:]
        for i in range(n):
            res = _adamw_math(refs[i][...], refs[n + i][...], refs[2 * n + i][...], refs[3 * n + i][...])
            for o_ref, r in zip(outs[3 * i:3 * i + 3], res):
                o_ref[...] = r

    res = pl.pallas_call(body, name=name, out_shape=[jax.ShapeDtypeStruct(w.shape, f32) for w in ws for _ in range(3)],
                         compiler_params=_params())(*ws, *gs, *ms, *vs)
    return [res[3 * i:3 * i + 3] for i in range(n)]


def _sum_adamw(parts, w, m, v, *, name, xchg=None):
    r, c = w.shape
    tr = _row_block(r, c, NDEV)

    def body(p_ref, w_ref, m_ref, v_ref, g_ref, d_ref, nm_ref, nv_ref):
        gg = p_ref[0].astype(f32)
        for j in range(1, NDEV):
            gg = gg + p_ref[j].astype(f32)
        g_ref[...] = gg
        d_ref[...], nm_ref[...], nv_ref[...] = _adamw_math(w_ref[...], gg, m_ref[...], v_ref[...])

    spec = pl.BlockSpec((tr, c), lambda i: (i, 0))
    shp = jax.ShapeDtypeStruct((r, c), f32)
    res, got = _call(
        body, (parts, w, m, v), name=name, grid=(r // tr,),
        in_specs=[pl.BlockSpec((NDEV, tr, c), lambda i: (0, i, 0))] + [spec] * 3,
        out_specs=[spec] * 4, out_shape=[shp] * 4, sem=("parallel",), xchg=xchg)
    return tuple(res) if xchg is None else tuple(res) + (got,)


def _cond_fwd(c_all, w, b, *, name):
    nb, n = c_all.shape[0], w.shape[1]

    def body(c_ref, w_ref, b_ref, o_ref):
        cc = c_ref[...]
        o_ref[...] = jnp.dot(cc * _sig(cc), w_ref[...], preferred_element_type=f32,
                             precision=lax.Precision.HIGHEST) + b_ref[...]

    return pl.pallas_call(body, name=name, out_shape=jax.ShapeDtypeStruct((nb, n), f32),
                          compiler_params=_params())(c_all, w, b)


def _cond_bwd(c_all, dmod, *, name):
    d, n = c_all.shape[1], dmod.shape[1]

    def body(c_ref, g_ref, o_ref):
        cc = c_ref[...]
        o_ref[...] = lax.dot_general(cc * _sig(cc), g_ref[...], (((0,), (0,)), ((), ())), preferred_element_type=f32,
                                     precision=lax.Precision.HIGHEST)

    return pl.pallas_call(body, name=name, out_shape=jax.ShapeDtypeStruct((d, n), f32),
                          compiler_params=_params())(c_all, dmod)


def _ssm_disc(lam_re, lam_im, log_dt):
    lr = jnp.minimum(lam_re, -1e-4)
    li = lam_im
    dt = jnp.exp(log_dt)
    mag = jnp.exp(lr * dt)
    ang = li * dt
    lbr, lbi = mag * jnp.cos(ang), mag * jnp.sin(ang)
    num_r, num_i = lbr - 1.0, lbi
    den = lr * lr + li * li
    return lbr, lbi, (num_r * lr + num_i * li) / den, (num_i * lr - num_r * li) / den


def _ssm_prep(lam_re, lam_im, log_dt, *, name):
    def body(a, b, c, o1, o2, o3, o4):
        o1[...], o2[...], o3[...], o4[...] = _ssm_disc(a[...], b[...], c[...])

    shp = jax.ShapeDtypeStruct(lam_re.shape, f32)
    return pl.pallas_call(body, name=name, out_shape=[shp] * 4, compiler_params=_params())(lam_re, lam_im, log_dt)


def _ssm_prep_bwd(lam_re, lam_im, log_dt, cts, *, name):
    def body(a, b, c, g1, g2, g3, g4, o1, o2, o3):
        _, vjp = jax.vjp(_ssm_disc, a[...], b[...], c[...])
        o1[...], o2[...], o3[...] = vjp((g1[...], g2[...], g3[...], g4[...]))

    shp = jax.ShapeDtypeStruct(lam_re.shape, f32)
    return pl.pallas_call(body, name=name, out_shape=[shp, shp, jax.ShapeDtypeStruct(log_dt.shape, f32)],
                          compiler_params=_params())(lam_re, lam_im, log_dt, *cts)


S5_ROWS = 512


def _step_major(x3):
    k, nt, c = x3.shape
    return jnp.swapaxes(x3, 0, 1).reshape(k * nt, c)


def _chunk_major(x2, nt):
    return jnp.swapaxes(x2.reshape(nt, SUBLANES, x2.shape[1]), 0, 1)


def _chain_carries(loc_r, loc_i, pr, pi_, forward):
    row = lax.broadcasted_iota(jnp.int32, loc_r.shape, 0)
    shift = 1 if forward else SUBLANES - 1
    order = range(1, SUBLANES) if forward else range(SUBLANES - 2, -1, -1)
    er, ei = loc_r, loc_i
    for k in order:
        sr, si = pltpu.roll(er, shift, 0), pltpu.roll(ei, shift, 0)
        er = jnp.where(row == k, loc_r + pr * sr - pi_ * si, er)
        ei = jnp.where(row == k, loc_i + pr * si + pi_ * sr, ei)
    edge = 0 if forward else SUBLANES - 1
    return (jnp.where(row == edge, 0.0, pltpu.roll(er, shift, 0)), jnp.where(row == edge, 0.0, pltpu.roll(ei, shift, 0)))


def _chunk_power(ar, ai, chunk_len):
    pr, pi_ = ar, ai
    for _ in range(int(math.log2(chunk_len))):
        pr, pi_ = pr * pr - pi_ * pi_, 2.0 * pr * pi_
    return pr, pi_


def _ssm_mats(bre_ref, bim_ref, cre_ref, cim_ref, cfr_ref, cfi_ref, bbar_s, cmat_s, nq):
    for q in range(nq):
        cr, ci, br, bi = cfr_ref[q], cfi_ref[q], bre_ref[q], bim_ref[q]
        bbar_s[q, :, 0:QS] = (cr * br - ci * bi).astype(bf16)
        bbar_s[q, :, QS:2 * QS] = (cr * bi + ci * br).astype(bf16)
        cmat_s[q, 0:QS, :] = cre_ref[q].astype(bf16)
        cmat_s[q, QS:2 * QS, :] = (-cim_ref[q]).astype(bf16)


def _ssm_fwd(u, ar, ai, bre, bim, cre, cim, cfr, cfi, dvec, *, name, xchg=None):
    s, sw = u.shape
    nq = sw // QW
    st = nq * 2 * QS
    tb = _tile(s, S5_ROWS)
    nb, nt, chunk_len = s // tb, tb // SUBLANES, s // SUBLANES
    assert chunk_len & (chunk_len - 1) == 0 and nt % 16 == 0

    def body(u_ref, ar_ref, ai_ref, bre_ref, bim_ref, cre_ref, cim_ref, cfr_ref, cfi_ref, d_ref,
             h_out, yraw_out, y_out, buf, hc, bbar_s, cmat_s):
        ph, i = pl.program_id(0), pl.program_id(1)

        @pl.when(i == 0)
        def _():
            _ssm_mats(bre_ref, bim_ref, cre_ref, cim_ref, cfr_ref, cfi_ref, bbar_s, cmat_s, nq)

        @pl.when((ph == 0) & (i == 0))
        def _():
            hc[...] = jnp.zeros_like(hc)

        @pl.when((ph == 1) & (i == 0))
        def _():
            for q in range(nq):
                o = q * 2 * QS
                pr, pi_ = _chunk_power(ar_ref[q], ai_ref[q], chunk_len)
                sr, si = _chain_carries(hc[:, o:o + QS], hc[:, o + QS:o + 2 * QS], pr, pi_, True)
                hc[:, o:o + QS] = sr
                hc[:, o + QS:o + 2 * QS] = si

        uu = u_ref[...]
        up = _step_major(uu).astype(bf16)
        for q in range(nq):
            o = q * 2 * QS
            buf[:, o:o + 2 * QS] = jnp.dot(up[:, q * QW:(q + 1) * QW], bbar_s[q], preferred_element_type=f32)

        for q in range(nq):
            o = q * 2 * QS
            a_r = jnp.broadcast_to(ar_ref[q], (SUBLANES, QS))
            a_i = jnp.broadcast_to(ai_ref[q], (SUBLANES, QS))

            def step(t, carry, o=o, a_r=a_r, a_i=a_i):
                hr, hi = carry
                r0 = pl.multiple_of(t * SUBLANES, SUBLANES)
                nr = a_r * hr - a_i * hi + buf[pl.ds(r0, SUBLANES), o:o + QS]
                ni = a_r * hi + a_i * hr + buf[pl.ds(r0, SUBLANES), o + QS:o + 2 * QS]
                buf[pl.ds(r0, SUBLANES), o:o + QS] = nr
                buf[pl.ds(r0, SUBLANES), o + QS:o + 2 * QS] = ni
                return nr, ni

            hr, hi = lax.fori_loop(0, nt, step, (hc[:, o:o + QS], hc[:, o + QS:o + 2 * QS]))
            hc[:, o:o + QS] = hr
            hc[:, o + QS:o + 2 * QS] = hi

        @pl.when(ph == 1)
        def _():
            for q in range(nq):
                o = q * 2 * QS
                cs = slice(q * QW, (q + 1) * QW)
                hq = buf[:, o:o + 2 * QS].astype(bf16)
                h_out[:, o:o + 2 * QS] = hq
                yq = _chunk_major(jnp.dot(hq, cmat_s[q], preferred_element_type=f32), nt) + d_ref[:, cs] * uu[:, :, cs]
                yraw_out[:, :, cs] = yq
                y_out[:, :, cs] = _gelu(yq).astype(bf16)

    blk = lambda ph, i: (0, i, 0)
    oblk = lambda ph, i: (0, i * ph, 0)
    act = lambda dt: jax.ShapeDtypeStruct((SUBLANES, chunk_len, sw), dt)
    (h_p, yraw3, y3), got = _call(
        body, (u.reshape(SUBLANES, chunk_len, sw), ar, ai, bre, bim, cre, cim, cfr, cfi, dvec), name=name, grid=(2, nb),
        in_specs=[pl.BlockSpec((SUBLANES, nt, sw), blk), _full(ar.shape), _full(ai.shape), _full(bre.shape), _full(bim.shape),
                  _full(cre.shape), _full(cim.shape), _full(cfr.shape), _full(cfi.shape), _full(dvec.shape)],
        out_specs=[pl.BlockSpec((tb, st), lambda ph, i: (i * ph, 0)), pl.BlockSpec((SUBLANES, nt, sw), oblk),
                   pl.BlockSpec((SUBLANES, nt, sw), oblk)],
        out_shape=[jax.ShapeDtypeStruct((s, st), bf16), act(f32), act(bf16)],
        scratch_shapes=[pltpu.VMEM((tb, st), f32), pltpu.VMEM((SUBLANES, st), f32),
                        pltpu.VMEM((nq, QW, 2 * QS), bf16), pltpu.VMEM((nq, 2 * QS, QW), bf16)],
        sem=("arbitrary", "arbitrary"), xchg=xchg)
    return h_p, yraw3, y3.reshape(s, sw), got


def _ssm_bwd(dy, yraw3, u, h_p, ar, ai, bre, bim, cre, cim, cfr, cfi, dvec, *, name, xchg=None):
    s, sw = u.shape
    nq = sw // QW
    st = nq * 2 * QS
    tb = _tile(s, S5_ROWS)
    nb, nt, chunk_len = s // tb, tb // SUBLANES, s // SUBLANES

    def body(dy_ref, yraw_ref, u_ref, h_ref, ar_ref, ai_ref, bre_ref, bim_ref, cre_ref, cim_ref, cfr_ref, cfi_ref, d_ref,
             du_out, dbre_out, dbim_out, dcre_out, dcim_out, dcfr_out, dcfi_out, dlbr_out, dlbi_out, dd_out, dbu_out,
             buf, rc, acc, dbbar, dcmat, bbar_s, cmat_s):
        ph, i = pl.program_id(0), pl.program_id(1)

        @pl.when(i == 0)
        def _():
            _ssm_mats(bre_ref, bim_ref, cre_ref, cim_ref, cfr_ref, cfi_ref, bbar_s, cmat_s, nq)

        @pl.when((ph == 0) & (i == 0))
        def _():
            rc[...] = jnp.zeros_like(rc)

        @pl.when((ph == 1) & (i == 0))
        def _():
            for q in range(nq):
                o = q * 2 * QS
                pr, pi_ = _chunk_power(ar_ref[q], ai_ref[q], chunk_len)
                sr, si = _chain_carries(rc[:, o:o + QS], rc[:, o + QS:o + 2 * QS], pr, -pi_, False)
                rc[:, o:o + QS] = sr
                rc[:, o + QS:o + 2 * QS] = si
            acc[...] = jnp.zeros_like(acc)
            dbbar[...] = jnp.zeros_like(dbbar)
            dcmat[...] = jnp.zeros_like(dcmat)
            dd_out[...] = jnp.zeros_like(dd_out)
            dbu_out[...] = jnp.zeros_like(dbu_out)

        dyraw = dy_ref[...].astype(f32) * _gelu_grad(yraw_ref[...])
        dyp = _step_major(dyraw).astype(bf16)
        for q in range(nq):
            o = q * 2 * QS
            buf[:, o:o + 2 * QS] = lax.dot_general(dyp[:, q * QW:(q + 1) * QW], cmat_s[q], (((1,), (1,)), ((), ())),
                                                   preferred_element_type=f32)

        def recur(with_grad):
            for q in range(nq):
                o = q * 2 * QS
                a_r = jnp.broadcast_to(ar_ref[q], (SUBLANES, QS))
                a_i = jnp.broadcast_to(ai_ref[q], (SUBLANES, QS))

                def step(j, carry, o=o, a_r=a_r, a_i=a_i):
                    r16 = pl.multiple_of((nt // 2 - 1 - j) * 2 * SUBLANES, 2 * SUBLANES)
                    if with_grad:
                        rr, ri, gr, gi = carry
                        h_re = h_ref[pl.ds(r16, 2 * SUBLANES), o:o + QS].astype(f32)
                        h_im = h_ref[pl.ds(r16, 2 * SUBLANES), o + QS:o + 2 * QS].astype(f32)
                    else:
                        rr, ri = carry
                    for half in (1, 0):
                        rows = pl.ds(pl.multiple_of(r16 + half * SUBLANES, SUBLANES), SUBLANES)
                        if with_grad:
                            hr = h_re[half * SUBLANES:(half + 1) * SUBLANES]
                            hi = h_im[half * SUBLANES:(half + 1) * SUBLANES]
                            gr = gr + hr * rr + hi * ri
                            gi = gi + hr * ri - hi * rr
                        nr = buf[rows, o:o + QS] + a_r * rr + a_i * ri
                        ni = buf[rows, o + QS:o + 2 * QS] + a_r * ri - a_i * rr
                        buf[rows, o:o + QS] = nr
                        buf[rows, o + QS:o + 2 * QS] = ni
                        rr, ri = nr, ni
                    return (rr, ri, gr, gi) if with_grad else (rr, ri)

                init = (rc[:, o:o + QS], rc[:, o + QS:o + 2 * QS])
                if with_grad:
                    init = init + (acc[:, o:o + QS], acc[:, o + QS:o + 2 * QS])
                res = lax.fori_loop(0, nt // 2, step, init)
                rc[:, o:o + QS] = res[0]
                rc[:, o + QS:o + 2 * QS] = res[1]
                if with_grad:
                    acc[:, o:o + QS] = res[2]
                    acc[:, o + QS:o + 2 * QS] = res[3]

        @pl.when(ph == 0)
        def _():
            recur(False)

        @pl.when(ph == 1)
        def _():
            recur(True)
            uu = u_ref[...]
            up = _step_major(uu).astype(bf16)
            dd_out[...] += _colsum((dyraw * uu).reshape(tb, sw))
            for q in range(nq):
                o = q * 2 * QS
                cs = slice(q * QW, (q + 1) * QW)
                lam = buf[:, o:o + 2 * QS].astype(bf16)
                duq = _chunk_major(lax.dot_general(lam, bbar_s[q], (((1,), (1,)), ((), ())), preferred_element_type=f32), nt) \
                    + d_ref[:, cs] * dyraw[:, :, cs]
                du_out[:, :, cs] = duq.astype(bf16)
                dbu_out[:, cs] += _colsum(duq.reshape(tb, QW))
                dbbar[q] += lax.dot_general(up[:, cs], lam, (((0,), (0,)), ((), ())), preferred_element_type=f32)
                dcmat[q] += lax.dot_general(dyp[:, cs], h_ref[:, o:o + 2 * QS], (((0,), (0,)), ((), ())),
                                            preferred_element_type=f32)

        @pl.when((ph == 1) & (i == nb - 1))
        def _():
            for q in range(nq):
                o = q * 2 * QS
                cr, ci, br, bi = cfr_ref[q], cfi_ref[q], bre_ref[q], bim_ref[q]
                gr, gi = dbbar[q, :, 0:QS], dbbar[q, :, QS:2 * QS]
                dbre_out[q] = cr * gr + ci * gi
                dbim_out[q] = cr * gi - ci * gr
                dcfr_out[q] = _colsum(gr * br + gi * bi)
                dcfi_out[q] = _colsum(gi * br - gr * bi)
                dcre_out[q] = dcmat[q, :, 0:QS].T
                dcim_out[q] = -dcmat[q, :, QS:2 * QS].T
                dlbr_out[q] = _colsum(acc[:, o:o + QS])
                dlbi_out[q] = _colsum(acc[:, o + QS:o + 2 * QS])

    blk = lambda ph, i: (0, nb - 1 - i, 0)
    oblk = lambda ph, i: (0, (nb - 1 - i) * ph + (nb - 1) * (1 - ph), 0)
    pshapes = [ar.shape, ai.shape, bre.shape, bim.shape, cre.shape, cim.shape, cfr.shape, cfi.shape, dvec.shape]
    oshapes = [bre.shape, bim.shape, cre.shape, cim.shape, cfr.shape, cfi.shape, ar.shape, ai.shape, dvec.shape, dvec.shape]
    act = pl.BlockSpec((SUBLANES, nt, sw), blk)
    view = lambda a: a.reshape(SUBLANES, chunk_len, sw)
    res, got = _call(
        body, (view(dy), yraw3, view(u), h_p, ar, ai, bre, bim, cre, cim, cfr, cfi, dvec), name=name, grid=(2, nb),
        in_specs=[act, act, act, pl.BlockSpec((tb, st), lambda ph, i: (nb - 1 - i, 0))] + [_full(p) for p in pshapes],
        out_specs=[pl.BlockSpec((SUBLANES, nt, sw), oblk)] + [_full(p) for p in oshapes],
        out_shape=[jax.ShapeDtypeStruct((SUBLANES, chunk_len, sw), bf16)] + [jax.ShapeDtypeStruct(p, f32) for p in oshapes],
        scratch_shapes=[pltpu.VMEM((tb, st), f32),
                        pltpu.VMEM((SUBLANES, st), f32), pltpu.VMEM((SUBLANES, st), f32),
                        pltpu.VMEM((nq, QW, 2 * QS), f32), pltpu.VMEM((nq, QW, 2 * QS), f32),
                        pltpu.VMEM((nq, QW, 2 * QS), bf16), pltpu.VMEM((nq, 2 * QS, QW), bf16)],
        sem=("arbitrary", "arbitrary"), xchg=xchg)
    return (res[0].reshape(s, sw),) + tuple(res[1:]) + (got,)


def _lnmod(x, sc, sh, *, name):
    s, d = x.shape
    tb = _tile(s, MM_ROWS)

    def body(x_ref, sc_ref, sh_ref, o_ref):
        xh, _ = _ln(x_ref[...])
        o_ref[...] = (xh * (1.0 + sc_ref[...]) + sh_ref[...]).astype(bf16)

    blk = pl.BlockSpec((tb, d), lambda i: (i, 0))
    vec = pl.BlockSpec((1, d), _row)
    return pl.pallas_call(body, name=name, grid=(s // tb,), in_specs=[blk, vec, vec], out_specs=blk,
                          out_shape=jax.ShapeDtypeStruct((s, d), bf16), compiler_params=_params("parallel"))(x, sc, sh)


ROWS = 32


def _row_chunks(n_rows, rows, fn, init, start=0):
    return lax.fori_loop(start, n_rows // rows, lambda c, carry: fn(pl.multiple_of(c * rows, rows), carry), init)


def _rows_from(win, o, rows):
    if o % SUBLANES == 0:
        return win[o:o + rows]
    n = win.shape[0]
    return pltpu.roll(win, (n - o) % n, 0)[0:rows]


def _window_before(ref, halo, r0, rows, first, cols):
    if first:
        return jnp.concatenate([halo, ref[pl.ds(0, rows), cols]], axis=0)
    return ref[pl.ds(pl.multiple_of(r0 - SUBLANES, SUBLANES), rows + SUBLANES), cols]


def _taps3(win, w, off, rows):
    return _rows_from(win, off, rows) * w[0] + _rows_from(win, off + 1, rows) * w[1] + _rows_from(win, off + 2, rows) * w[2]


def _fold8(x):
    acc = x[0:SUBLANES]
    for r in range(1, x.shape[0] // SUBLANES):
        acc = acc + x[r * SUBLANES:(r + 1) * SUBLANES]
    return acc


def _conv_halo_specs(tb, cw, halo, s):
    per = tb // halo
    prev = pl.BlockSpec((halo, cw), lambda i: (jnp.maximum(i * per - 1, 0), 0))
    nxt = pl.BlockSpec((halo, cw), lambda i: (jnp.minimum((i + 1) * per, s // halo - 1), 0))
    return prev, nxt


WIDE_ROWS = 16


def _shift_groups(lo, hi):
    return [(b, [o for o in range(lo, hi + 1) if o % SUBLANES == b]) for b in range(SUBLANES)]


def _shifted(win, b):
    return win if b == 0 else _rows_from(win, b, win.shape[0] - SUBLANES)


def _conv31(win, w_ref, cols, rows, lo, hi, tap_of):
    acc = None
    for b, offs in _shift_groups(lo, hi):
        if offs:
            wb = _shifted(win, b)
            for o in offs:
                term = wb[o - b:o - b + rows] * w_ref[pl.ds(tap_of(o), 1), cols]
                acc = term if acc is None else acc + term
    return acc


def _gate_into(ext, a_ref, g_ref, ah_ref, gh_ref, tb, i):
    ext[pl.ds(0, CONV_HALO), :] = jnp.where(i > 0, ah_ref[...] * _sig(gh_ref[...]), 0.0)

    def chunk(r0, carry):
        ext[pl.ds(pl.multiple_of(r0 + CONV_HALO, SUBLANES), WIDE_ROWS), :] = \
            a_ref[pl.ds(r0, WIDE_ROWS), :] * _sig(g_ref[pl.ds(r0, WIDE_ROWS), :])
        return carry

    _row_chunks(tb, WIDE_ROWS, chunk, 0)


def _causal_conv_into(v2buf, ext, w_ref, b_ref, tb, cw):
    for ct in range(cw // LANES):
        cols = slice(ct * LANES, (ct + 1) * LANES)

        def chunk(r0, carry, cols=cols):
            win = ext[pl.ds(r0, ROWS + CONV_HALO), cols]
            v2buf[pl.ds(r0, ROWS), cols] = _conv31(win, w_ref, cols, ROWS, 2, CONV_K + 1, lambda o: o - 2) + b_ref[:, cols]
            return carry

        _row_chunks(tb, ROWS, chunk, 0)


def _silu_grad(x):
    sg = _sig(x)
    return sg * (1.0 + x * (1.0 - sg))


def _conv_fwd(cva, cvg, w, b, lng, lnb, *, name, xchg=None):
    s, cw = cva.shape
    tb = _tile(s, 256)
    prev, _ = _conv_halo_specs(tb, cw, CONV_HALO, s)

    def body(a_ref, g_ref, ah_ref, gh_ref, w_ref, b_ref, lng_ref, lnb_ref, o_ref, v2_ref, ext):
        _gate_into(ext, a_ref, g_ref, ah_ref, gh_ref, tb, pl.program_id(0))
        _causal_conv_into(v2_ref, ext, w_ref, b_ref, tb, cw)
        xh, _ = _ln(v2_ref[...])
        v3 = xh * lng_ref[...] + lnb_ref[...]
        o_ref[...] = (v3 * _sig(v3)).astype(bf16)

    blk = pl.BlockSpec((tb, cw), lambda i: (i, 0))
    vec = pl.BlockSpec((1, cw), _row)
    (v4, v2), got = _call(
        body, (cva, cvg, cva, cvg, w, b, lng, lnb), name=name, grid=(s // tb,),
        in_specs=[blk, blk, prev, prev, _full(w.shape), vec, vec, vec], out_specs=[blk, blk],
        out_shape=[jax.ShapeDtypeStruct((s, cw), bf16), jax.ShapeDtypeStruct((s, cw), f32)],
        scratch_shapes=[pltpu.VMEM((tb + CONV_HALO, cw), f32)], sem=("parallel",), xchg=xchg)
    return v4, v2, got


def _conv_bwd_ln(dv4, v2, lng, lnb, *, name):
    s, cw = v2.shape
    tb = _tile(s, 256)

    def body(d_ref, v2_ref, lng_ref, lnb_ref, o_ref, dg_ref, db_ref):
        @pl.when(pl.program_id(0) == 0)
        def _():
            dg_ref[...] = jnp.zeros_like(dg_ref)
            db_ref[...] = jnp.zeros_like(db_ref)

        xh, rstd = _ln(v2_ref[...])
        v3 = xh * lng_ref[...] + lnb_ref[...]
        dv3 = d_ref[...].astype(f32) * _silu_grad(v3)
        dg_ref[...] += _colsum(dv3 * xh)
        db_ref[...] += _colsum(dv3)
        o_ref[...] = _ln_bwd(dv3 * lng_ref[...], xh, rstd)

    blk = pl.BlockSpec((tb, cw), lambda i: (i, 0))
    vec = pl.BlockSpec((1, cw), _row)
    vshape = jax.ShapeDtypeStruct((1, cw), f32)
    return pl.pallas_call(
        body, name=name, grid=(s // tb,), in_specs=[blk, blk, vec, vec],
        out_specs=[blk, vec, vec], out_shape=[jax.ShapeDtypeStruct((s, cw), f32), vshape, vshape],
        compiler_params=_params("arbitrary"))(dv4, v2, lng, lnb)


def _conv_bwd_taps(dv2, cva, cvg, w, *, name, xchg=None):
    s, cw = cva.shape
    tb = _tile(s, 256)
    nb = s // tb
    prev, nxt = _conv_halo_specs(tb, cw, CONV_HALO, s)

    def body(d_ref, dn_ref, a_ref, g_ref, ah_ref, gh_ref, w_ref, da_ref, dg_ref, dw_ref, db_ref, sa_ref, sg_ref,
             ext, dext, dvbuf, tap_sums):
        i = pl.program_id(0)

        @pl.when(i == 0)
        def _():
            for r in (dw_ref, db_ref, sa_ref, sg_ref):
                r[...] = jnp.zeros_like(r)

        _gate_into(ext, a_ref, g_ref, ah_ref, gh_ref, tb, i)
        dext[pl.ds(tb, CONV_HALO), :] = jnp.where(i < nb - 1, dn_ref[...], 0.0)

        def copy(r0, carry):
            dext[pl.ds(r0, WIDE_ROWS), :] = d_ref[pl.ds(r0, WIDE_ROWS), :]
            return carry

        _row_chunks(tb, WIDE_ROWS, copy, 0)

        for ct in range(cw // LANES):
            cols = slice(ct * LANES, (ct + 1) * LANES)

            tap_sums[...] = jnp.zeros_like(tap_sums)

            def back(r0, carry, cols=cols):
                win = dext[pl.ds(r0, ROWS + CONV_HALO), cols]
                dvbuf[pl.ds(r0, ROWS), cols] = _conv31(win, w_ref, cols, ROWS, 0, CONV_K - 1, lambda o: CONV_K - 1 - o)
                win = ext[pl.ds(r0, ROWS + CONV_HALO), cols]
                dd = d_ref[pl.ds(r0, ROWS), cols]
                for b, offs in _shift_groups(2, CONV_K + 1):
                    wb = _shifted(win, b)
                    for o in offs:
                        tap_sums[o - 2] += _fold8(dd * wb[o - b:o - b + ROWS])
                return carry

            _row_chunks(tb, ROWS, back, 0)
            for k in range(CONV_K):
                dw_ref[pl.ds(k, 1), cols] += _colsum(tap_sums[k])

        def gate_back(r0, sums):
            rows = pl.ds(r0, WIDE_ROWS)
            aa, sg, dv = a_ref[rows, :], _sig(g_ref[rows, :]), dvbuf[rows, :]
            da = dv * sg
            dgate = dv * aa * sg * (1.0 - sg)
            da_ref[rows, :] = da.astype(bf16)
            dg_ref[rows, :] = dgate.astype(bf16)
            return sums[0] + _fold8(da), sums[1] + _fold8(dgate), sums[2] + _fold8(d_ref[rows, :])

        zero = jnp.zeros((SUBLANES, cw), f32)
        sums = _row_chunks(tb, WIDE_ROWS, gate_back, (zero, zero, zero))
        sa_ref[...] += _colsum(sums[0])
        sg_ref[...] += _colsum(sums[1])
        db_ref[...] += _colsum(sums[2])

    blk = pl.BlockSpec((tb, cw), lambda i: (i, 0))
    vec = pl.BlockSpec((1, cw), _row)
    vshape = jax.ShapeDtypeStruct((1, cw), f32)
    act = jax.ShapeDtypeStruct((s, cw), bf16)
    res, got = _call(
        body, (dv2, dv2, cva, cvg, cva, cvg, w), name=name, grid=(nb,), in_specs=[blk, nxt, blk, blk, prev, prev, _full(w.shape)],
        out_specs=[blk, blk, _full(w.shape), vec, vec, vec],
        out_shape=[act, act, jax.ShapeDtypeStruct(w.shape, f32), vshape, vshape, vshape],
        scratch_shapes=[pltpu.VMEM((tb + CONV_HALO, cw), f32), pltpu.VMEM((tb + CONV_HALO, cw), f32), pltpu.VMEM((tb, cw), f32),
                        pltpu.VMEM((CONV_HALO, SUBLANES, LANES), f32)],
        sem=("arbitrary",), xchg=xchg)
    return tuple(res) + (got,)


def _glu_merge(ya, yb, ycv, gs, gc, *, name):
    s, d = ya.shape
    tb = _tile(s, MM_ROWS)

    def body(ya_ref, yb_ref, ycv_ref, gs_ref, gc_ref, o_ref):
        ld = lambda r: r[...].astype(f32)
        z = ld(ya_ref) * _sig(ld(yb_ref))
        o_ref[...] = (_sig(ld(gs_ref)) * z + _sig(ld(gc_ref)) * ld(ycv_ref)).astype(bf16)

    blk = pl.BlockSpec((tb, d), lambda i: (i, 0))
    return pl.pallas_call(body, name=name, grid=(s // tb,), in_specs=[blk] * 5, out_specs=blk,
                          out_shape=jax.ShapeDtypeStruct((s, d), bf16), compiler_params=_params("parallel"))(ya, yb, ycv, gs, gc)


def _glu_merge_bwd(dm, ya, yb, ycv, gs, gc, *, name):
    s, d = ya.shape
    tb = _tile(s, 512)

    def body(dm_ref, ya_ref, yb_ref, ycv_ref, gs_ref, gc_ref, dya_ref, dyb_ref, dycv_ref, dgs_ref, dgc_ref, sgs_ref, sgc_ref):
        @pl.when(pl.program_id(0) == 0)
        def _():
            sgs_ref[...] = jnp.zeros_like(sgs_ref)
            sgc_ref[...] = jnp.zeros_like(sgc_ref)

        ld = lambda r: r[...].astype(f32)
        dmv, yav = ld(dm_ref), ld(ya_ref)
        sb, ss, scv = _sig(ld(yb_ref)), _sig(ld(gs_ref)), _sig(ld(gc_ref))
        z = yav * sb
        dz = dmv * ss
        dgs = dmv * z * ss * (1.0 - ss)
        dgc = dmv * ld(ycv_ref) * scv * (1.0 - scv)
        dya_ref[...] = (dz * sb).astype(bf16)
        dyb_ref[...] = (dz * yav * sb * (1.0 - sb)).astype(bf16)
        dycv_ref[...] = (dmv * scv).astype(bf16)
        dgs_ref[...] = dgs.astype(bf16)
        dgc_ref[...] = dgc.astype(bf16)
        sgs_ref[...] += _colsum(dgs)
        sgc_ref[...] += _colsum(dgc)

    blk = pl.BlockSpec((tb, d), lambda i: (i, 0))
    vec = pl.BlockSpec((1, d), _row)
    act = jax.ShapeDtypeStruct((s, d), bf16)
    vshape = jax.ShapeDtypeStruct((1, d), f32)
    return pl.pallas_call(body, name=name, grid=(s // tb,), in_specs=[blk] * 6, out_specs=[blk] * 5 + [vec, vec],
                          out_shape=[act] * 5 + [vshape, vshape], compiler_params=_params("arbitrary"))(dm, ya, yb, ycv, gs, gc)


def _resid_ln_mod(x, o, g, lng, lnb, sc, sh, alpha, *, name):
    s, d = x.shape
    tb = _tile(s, MM_ROWS)

    def body(x_ref, o_ref, g_ref, lng_ref, lnb_ref, sc_ref, sh_ref, x1_ref, h_ref):
        xh, _ = _ln(alpha * x_ref[...] + g_ref[...] * o_ref[...].astype(f32))
        x1 = xh * lng_ref[...] + lnb_ref[...]
        x1_ref[...] = x1
        xh1, _ = _ln(x1)
        h_ref[...] = (xh1 * (1.0 + sc_ref[...]) + sh_ref[...]).astype(bf16)

    blk = pl.BlockSpec((tb, d), lambda i: (i, 0))
    vec = pl.BlockSpec((1, d), _row)
    return pl.pallas_call(body, name=name, grid=(s // tb,), in_specs=[blk, blk] + [vec] * 5, out_specs=[blk, blk],
                          out_shape=[jax.ShapeDtypeStruct((s, d), f32), jax.ShapeDtypeStruct((s, d), bf16)],
                          compiler_params=_params("parallel"))(x, o, g, lng, lnb, sc, sh)


def _resid_ln_loss(x1, f, w_dn, g, lng, lnb, tgt, alpha, *, name):
    s, d = x1.shape
    tb = _tile(s, 512)

    def body(x1_ref, f_ref, w_ref, g_ref, lng_ref, lnb_ref, t_ref, dr_ref, dy_ref, loss_ref, dlg_ref, dlb_ref, dg_ref):
        @pl.when(pl.program_id(0) == 0)
        def _():
            for r in (loss_ref, dlg_ref, dlb_ref, dg_ref):
                r[...] = jnp.zeros_like(r)

        yv = jnp.dot(f_ref[...], w_ref[...], preferred_element_type=f32)
        xh, rstd = _ln(alpha * x1_ref[...] + g_ref[...] * yv)
        err = xh * lng_ref[...] + lnb_ref[...] - t_ref[...]
        loss_ref[...] += 0.5 * jnp.sum(jnp.sum(err * err, axis=-1, keepdims=True) / d, axis=0, keepdims=True)
        dx2 = err / d
        dlg_ref[...] += _colsum(dx2 * xh)
        dlb_ref[...] += _colsum(dx2)
        dr = _ln_bwd(dx2 * lng_ref[...], xh, rstd)
        dg_ref[...] += _colsum(dr * yv)
        dr_ref[...] = dr
        dy_ref[...] = (g_ref[...] * dr).astype(bf16)

    blk = pl.BlockSpec((tb, d), lambda i: (i, 0))
    vec = pl.BlockSpec((1, d), _row)
    vshape = jax.ShapeDtypeStruct((1, d), f32)
    return pl.pallas_call(
        body, name=name, grid=(s // tb,),
        in_specs=[blk, pl.BlockSpec((tb, f.shape[1]), lambda i: (i, 0)), _full(w_dn.shape), vec, vec, vec, blk],
        out_specs=[blk, blk, pl.BlockSpec((1, 1), _row), vec, vec, vec],
        out_shape=[jax.ShapeDtypeStruct((s, d), f32), jax.ShapeDtypeStruct((s, d), bf16),
                   jax.ShapeDtypeStruct((1, 1), f32), vshape, vshape, vshape],
        compiler_params=_params("arbitrary"))(x1, f, w_dn, g, lng, lnb, tgt)


def _mid_bwd(dh2, x1, dr2, x, o, g, sc, lng, alpha, *, name):
    s, d = x.shape
    tb = _tile(s, 512)

    def body(dh_ref, x1_ref, dr2_ref, x_ref, o_ref, g_ref, sc_ref, lng_ref,
             dr1_ref, do_ref, dsc_ref, dsh_ref, dlg_ref, dlb_ref, dg_ref):
        @pl.when(pl.program_id(0) == 0)
        def _():
            for r in (dsc_ref, dsh_ref, dlg_ref, dlb_ref, dg_ref):
                r[...] = jnp.zeros_like(r)

        dh = dh_ref[...].astype(f32)
        xh1, rstd1 = _ln(x1_ref[...])
        dsc_ref[...] += _colsum(dh * xh1)
        dsh_ref[...] += _colsum(dh)
        dx1 = alpha * dr2_ref[...] + _ln_bwd(dh * (1.0 + sc_ref[...]), xh1, rstd1)
        ov = o_ref[...].astype(f32)
        xhr, rstdr = _ln(alpha * x_ref[...] + g_ref[...] * ov)
        dlg_ref[...] += _colsum(dx1 * xhr)
        dlb_ref[...] += _colsum(dx1)
        dr1 = _ln_bwd(dx1 * lng_ref[...], xhr, rstdr)
        dg_ref[...] += _colsum(dr1 * ov)
        dr1_ref[...] = dr1
        do_ref[...] = (g_ref[...] * dr1).astype(bf16)

    blk = pl.BlockSpec((tb, d), lambda i: (i, 0))
    vec = pl.BlockSpec((1, d), _row)
    vshape = jax.ShapeDtypeStruct((1, d), f32)
    return pl.pallas_call(
        body, name=name, grid=(s // tb,), in_specs=[blk] * 5 + [vec] * 3, out_specs=[blk, blk] + [vec] * 5,
        out_shape=[jax.ShapeDtypeStruct((s, d), f32), jax.ShapeDtypeStruct((s, d), bf16)] + [vshape] * 5,
        compiler_params=_params("arbitrary"))(dh2, x1, dr2, x, o, g, sc, lng)


def _final_bwd(dh1, x, dr1, sc, alpha, *, name, xchg=None):
    s, d = x.shape
    tb = _tile(s, MM_ROWS)

    def body(dh_ref, x_ref, dr1_ref, sc_ref, dx_ref, dsc_ref, dsh_ref):
        @pl.when(pl.program_id(0) == 0)
        def _():
            dsc_ref[...] = jnp.zeros_like(dsc_ref)
            dsh_ref[...] = jnp.zeros_like(dsh_ref)

        dh = dh_ref[...].astype(f32)
        xh, rstd = _ln(x_ref[...])
        dsc_ref[...] += _colsum(dh * xh)
        dsh_ref[...] += _colsum(dh)
        dx_ref[...] = alpha * dr1_ref[...] + _ln_bwd(dh * (1.0 + sc_ref[...]), xh, rstd)

    blk = pl.BlockSpec((tb, d), lambda i: (i, 0))
    vec = pl.BlockSpec((1, d), _row)
    vshape = jax.ShapeDtypeStruct((1, d), f32)
    res, got = _call(body, (dh1, x, dr1, sc), name=name, grid=(s // tb,), in_specs=[blk, blk, blk, vec], out_specs=[blk, vec, vec],
                     out_shape=[jax.ShapeDtypeStruct((s, d), f32), vshape, vshape], sem=("arbitrary",), xchg=xchg)
    return tuple(res) + (got,)


TALL_ROWS = 64


def _ffn_col_tile(fh):
    return fh // 2 if (fh // 2) % LANES == 0 else fh


def _ffn_specs(s, fh, tb, tc):
    per = tb // FFN_HALO
    blk = pl.BlockSpec((tb, tc), lambda j, i: (i, j))
    prev = pl.BlockSpec((FFN_HALO, tc), lambda j, i: (jnp.maximum(i * per - 1, 0), j))
    nxt = pl.BlockSpec((FFN_HALO, tc), lambda j, i: (jnp.minimum((i + 1) * per, s // FFN_HALO - 1), j))
    taps = pl.BlockSpec((FFN_HALO, tc), lambda j, i: (0, j))
    vec = pl.BlockSpec((1, tc), lambda j, i: (0, j))
    return blk, prev, nxt, taps, vec


def _ffn_mid(upa, upv, wa, wv, ba, bv, *, name, xchg=None):
    s, fh = upa.shape
    tb, tc = _tile(s, 512), _ffn_col_tile(fh)
    blk, prev, _, taps, vec = _ffn_specs(s, fh, tb, tc)
    off = FFN_HALO - FFN_K + 1

    def body(a_ref, v_ref, ah_ref, vh_ref, wa_ref, wv_ref, ba_ref, bv_ref, o_ref):
        first = pl.program_id(1) == 0
        for lt in range(tc // LANES):
            cols = slice(lt * LANES, (lt + 1) * LANES)
            halo_a, halo_v = jnp.where(first, 0.0, ah_ref[:, cols]), jnp.where(first, 0.0, vh_ref[:, cols])
            wa = [wa_ref[pl.ds(k, 1), cols] for k in range(FFN_K)]
            wv = [wv_ref[pl.ds(k, 1), cols] for k in range(FFN_K)]
            ba, bv = ba_ref[:, cols], bv_ref[:, cols]

            def chunk(r0, carry, head=False, cols=cols, halo_a=halo_a, halo_v=halo_v, wa=wa, wv=wv, ba=ba, bv=bv):
                a2 = _taps3(_window_before(a_ref, halo_a, r0, TALL_ROWS, head, cols), wa, off, TALL_ROWS) + ba
                v2 = _taps3(_window_before(v_ref, halo_v, r0, TALL_ROWS, head, cols), wv, off, TALL_ROWS) + bv
                o_ref[pl.ds(r0, TALL_ROWS), cols] = (_gelu(a2) * v2).astype(bf16)
                return carry

            chunk(0, 0, head=True)
            _row_chunks(tb, TALL_ROWS, chunk, 0, start=1)

    (f,), got = _call(
        body, (upa, upv, upa, upv, wa, wv, ba, bv), name=name, grid=(fh // tc, s // tb),
        in_specs=[blk, blk, prev, prev, taps, taps, vec, vec], out_specs=[blk], out_shape=[jax.ShapeDtypeStruct((s, fh), bf16)],
        sem=("parallel", "arbitrary"), xchg=xchg)
    return f, got


def _ffn_mid_bwd_tile(cols, first, last, tb, off, df_ref, dfn_ref, a_ref, v_ref, ah_ref, vh_ref, an_ref, vn_ref, wa_ref, wv_ref,
                      ba_ref, bv_ref, da_ref, dv_ref, dwa_ref, dwv_ref, dba_ref, dbv_ref, dexta, dextv):
    rows_c = TALL_ROWS
    halo_a, halo_v = jnp.where(first, 0.0, ah_ref[:, cols]), jnp.where(first, 0.0, vh_ref[:, cols])
    wa = [wa_ref[pl.ds(k, 1), cols] for k in range(FFN_K)]
    wv = [wv_ref[pl.ds(k, 1), cols] for k in range(FFN_K)]
    ba, bv = ba_ref[:, cols], bv_ref[:, cols]

    def conv_cotangents(r0, rows, xa, xv, dfe):
        sa = [_rows_from(xa, off + k, rows) for k in range(FFN_K)]
        sv = [_rows_from(xv, off + k, rows) for k in range(FFN_K)]
        a2 = sa[0] * wa[0] + sa[1] * wa[1] + sa[2] * wa[2] + ba
        v2 = sv[0] * wv[0] + sv[1] * wv[1] + sv[2] * wv[2] + bv
        cdf = 0.5 * (1.0 + lax.erf(a2 * INV_SQRT2))
        da2 = dfe * v2 * (cdf + a2 * jnp.exp(-0.5 * a2 * a2) * INV_SQRT_2PI)
        dv2 = dfe * (a2 * cdf)
        dexta[pl.ds(r0, rows), cols] = da2
        dextv[pl.ds(r0, rows), cols] = dv2
        return da2, dv2, sa, sv

    def chunk(r0, sums, head=False):
        da2, dv2, sa, sv = conv_cotangents(r0, rows_c, _window_before(a_ref, halo_a, r0, rows_c, head, cols),
                                           _window_before(v_ref, halo_v, r0, rows_c, head, cols), df_ref[pl.ds(r0, rows_c), cols])
        new = [sums[k] + _fold8(da2 * sa[k]) for k in range(FFN_K)] + [sums[FFN_K] + _fold8(da2)]
        new += [sums[FFN_K + 1 + k] + _fold8(dv2 * sv[k]) for k in range(FFN_K)] + [sums[2 * FFN_K + 1] + _fold8(dv2)]
        return tuple(new)

    sums = chunk(0, tuple(jnp.zeros((SUBLANES, LANES), f32) for _ in range(2 * FFN_K + 2)), head=True)
    sums = _row_chunks(tb, rows_c, chunk, sums, start=1)
    conv_cotangents(tb, FFN_HALO,
                    jnp.concatenate([a_ref[pl.ds(tb - FFN_HALO, FFN_HALO), cols], jnp.where(last, 0.0, an_ref[:, cols])], axis=0),
                    jnp.concatenate([v_ref[pl.ds(tb - FFN_HALO, FFN_HALO), cols], jnp.where(last, 0.0, vn_ref[:, cols])], axis=0),
                    jnp.where(last, 0.0, dfn_ref[:, cols]))
    for k in range(FFN_K):
        dwa_ref[pl.ds(k, 1), cols] += _colsum(sums[k])
        dwv_ref[pl.ds(k, 1), cols] += _colsum(sums[FFN_K + 1 + k])
    dba_ref[:, cols] += _colsum(sums[FFN_K])
    dbv_ref[:, cols] += _colsum(sums[2 * FFN_K + 1])

    def back(r0, carry):
        for dext, w, o_ref in ((dexta, wa, da_ref), (dextv, wv, dv_ref)):
            dd = dext[pl.ds(r0, rows_c + FFN_HALO), cols]
            o_ref[pl.ds(r0, rows_c), cols] = (_rows_from(dd, 2, rows_c) * w[0] + _rows_from(dd, 1, rows_c) * w[1]
                                              + dd[0:rows_c] * w[2]).astype(bf16)
        return carry

    _row_chunks(tb, rows_c, back, 0)


def _ffn_mid_bwd(df, upa, upv, wa, wv, ba, bv, *, name, xchg=None):
    s, fh = upa.shape
    tb, tc = _tile(s, 512), _ffn_col_tile(fh)
    nb = s // tb
    blk, prev, nxt, taps, vec = _ffn_specs(s, fh, tb, tc)
    off = FFN_HALO - FFN_K + 1
    te = tb + FFN_HALO

    def body(df_ref, dfn_ref, a_ref, v_ref, ah_ref, vh_ref, an_ref, vn_ref, wa_ref, wv_ref, ba_ref, bv_ref,
             da_ref, dv_ref, dwa_ref, dwv_ref, dba_ref, dbv_ref, dexta, dextv):
        i = pl.program_id(1)

        @pl.when(i == 0)
        def _():
            for r in (dwa_ref, dwv_ref, dba_ref, dbv_ref):
                r[...] = jnp.zeros_like(r)

        last = i == nb - 1
        for lt in range(tc // LANES):
            _ffn_mid_bwd_tile(slice(lt * LANES, (lt + 1) * LANES), i == 0, last, tb, off, df_ref, dfn_ref, a_ref, v_ref,
                              ah_ref, vh_ref, an_ref, vn_ref, wa_ref, wv_ref, ba_ref, bv_ref, da_ref, dv_ref, dwa_ref, dwv_ref,
                              dba_ref, dbv_ref, dexta, dextv)

    act = jax.ShapeDtypeStruct((s, fh), bf16)
    wshape = jax.ShapeDtypeStruct((FFN_HALO, fh), f32)
    vshape = jax.ShapeDtypeStruct((1, fh), f32)
    res, got = _call(
        body, (df, df, upa, upv, upa, upv, upa, upv, wa, wv, ba, bv), name=name, grid=(fh // tc, nb),
        in_specs=[blk, nxt, blk, blk, prev, prev, nxt, nxt, taps, taps, vec, vec],
        out_specs=[blk, blk, taps, taps, vec, vec], out_shape=[act, act, wshape, wshape, vshape, vshape],
        scratch_shapes=[pltpu.VMEM((te, tc), f32)] * 2, sem=("parallel", "arbitrary"), xchg=xchg)
    return tuple(res) + (got,)


def _cols_from_shards(stacked):
    _, k, n = stacked.shape
    return stacked.transpose(1, 0, 2).reshape(k, NDEV * n)


def _pad_rows(w, rows):
    return jnp.pad(w, ((0, rows - w.shape[0]), (0, 0)))


def kernel(x, c, w_cond, b_cond, w_in, b_in, ssm_lambda_re, ssm_lambda_im, ssm_log_dt, ssm_b_re, ssm_b_im, ssm_c_re, ssm_c_im, ssm_d, ssm_glu_w_a, ssm_glu_w_b, cv_dw_w, cv_dw_b, cv_ln_g, cv_ln_b, cv_w_pw, w_out, ln1_g, ln1_b, ffn_w_up, ffn_dw_w, ffn_dw_b, ffn_w_down, ln2_g, ln2_b, loss_target, m_w_cond, m_b_cond, m_w_in, m_b_in, m_ssm_lambda_re, m_ssm_lambda_im, m_ssm_log_dt, m_ssm_b_re, m_ssm_b_im, m_ssm_c_re, m_ssm_c_im, m_ssm_d, m_ssm_glu_w_a, m_ssm_glu_w_b, m_cv_dw_w, m_cv_dw_b, m_cv_ln_g, m_cv_ln_b, m_cv_w_pw, m_w_out, m_ln1_g, m_ln1_b, m_ffn_w_up, m_ffn_dw_w, m_ffn_dw_b, m_ffn_w_down, m_ln2_g, m_ln2_b, v_w_cond, v_b_cond, v_w_in, v_b_in, v_ssm_lambda_re, v_ssm_lambda_im, v_ssm_log_dt, v_ssm_b_re, v_ssm_b_im, v_ssm_c_re, v_ssm_c_im, v_ssm_d, v_ssm_glu_w_a, v_ssm_glu_w_b, v_cv_dw_w, v_cv_dw_b, v_cv_ln_g, v_cv_ln_b, v_cv_w_pw, v_w_out, v_ln1_g, v_ln1_b, v_ffn_w_up, v_ffn_dw_w, v_ffn_dw_b, v_ffn_w_down, v_ln2_g, v_ln2_b):
    weights = dict(w_cond=w_cond, b_cond=b_cond, w_in=w_in, b_in=b_in, ssm_lambda_re=ssm_lambda_re, ssm_lambda_im=ssm_lambda_im, ssm_log_dt=ssm_log_dt, ssm_b_re=ssm_b_re, ssm_b_im=ssm_b_im, ssm_c_re=ssm_c_re, ssm_c_im=ssm_c_im, ssm_d=ssm_d, ssm_glu_w_a=ssm_glu_w_a, ssm_glu_w_b=ssm_glu_w_b, cv_dw_w=cv_dw_w, cv_dw_b=cv_dw_b, cv_ln_g=cv_ln_g, cv_ln_b=cv_ln_b, cv_w_pw=cv_w_pw, w_out=w_out, ln1_g=ln1_g, ln1_b=ln1_b, ffn_w_up=ffn_w_up, ffn_dw_w=ffn_dw_w, ffn_dw_b=ffn_dw_b, ffn_w_down=ffn_w_down, ln2_g=ln2_g, ln2_b=ln2_b)
    mom_m = dict(w_cond=m_w_cond, b_cond=m_b_cond, w_in=m_w_in, b_in=m_b_in, ssm_lambda_re=m_ssm_lambda_re, ssm_lambda_im=m_ssm_lambda_im, ssm_log_dt=m_ssm_log_dt, ssm_b_re=m_ssm_b_re, ssm_b_im=m_ssm_b_im, ssm_c_re=m_ssm_c_re, ssm_c_im=m_ssm_c_im, ssm_d=m_ssm_d, ssm_glu_w_a=m_ssm_glu_w_a, ssm_glu_w_b=m_ssm_glu_w_b, cv_dw_w=m_cv_dw_w, cv_dw_b=m_cv_dw_b, cv_ln_g=m_cv_ln_g, cv_ln_b=m_cv_ln_b, cv_w_pw=m_cv_w_pw, w_out=m_w_out, ln1_g=m_ln1_g, ln1_b=m_ln1_b, ffn_w_up=m_ffn_w_up, ffn_dw_w=m_ffn_dw_w, ffn_dw_b=m_ffn_dw_b, ffn_w_down=m_ffn_w_down, ln2_g=m_ln2_g, ln2_b=m_ln2_b)
    mom_v = dict(w_cond=v_w_cond, b_cond=v_b_cond, w_in=v_w_in, b_in=v_b_in, ssm_lambda_re=v_ssm_lambda_re, ssm_lambda_im=v_ssm_lambda_im, ssm_log_dt=v_ssm_log_dt, ssm_b_re=v_ssm_b_re, ssm_b_im=v_ssm_b_im, ssm_c_re=v_ssm_c_re, ssm_c_im=v_ssm_c_im, ssm_d=v_ssm_d, ssm_glu_w_a=v_ssm_glu_w_a, ssm_glu_w_b=v_ssm_glu_w_b, cv_dw_w=v_cv_dw_w, cv_dw_b=v_cv_dw_b, cv_ln_g=v_cv_ln_g, cv_ln_b=v_cv_ln_b, cv_w_pw=v_cv_w_pw, w_out=v_w_out, ln1_g=v_ln1_g, ln1_b=v_ln1_b, ffn_w_up=v_ffn_w_up, ffn_dw_w=v_ffn_dw_w, ffn_dw_b=v_ffn_dw_b, ffn_w_down=v_ffn_w_down, ln2_g=v_ln2_g, ln2_b=v_ln2_b)
    names = list(weights)

    s, d = x.shape[1], x.shape[2]
    sw = cw = d // 2
    fh = ffn_w_down.shape[1] * NDEV
    ng, nq = sw // SSM_GROUP, sw // QW
    gq = ng // nq
    alpha = 2.0 ** 0.25
    me = 4 * lax.axis_index("x") + 2 * lax.axis_index("y") + lax.axis_index("c")
    xs, tgt = x[0], loss_target[0]

    col_names = ["w_in", "ssm_glu_w_a", "ssm_glu_w_b", "cv_w_pw", "ffn_w_up"]
    row_names = ["w_out", "ffn_w_down"]
    big = col_names + row_names
    sent = lambda ns: [weights[n][0].astype(bf16) for n in ns]
    got_in, got_c, got_cv_taps, got_ffn_taps = _exchange(sent(["w_in"]) + [c, cv_dw_w[0, :, 0], ffn_dw_w[0, :, 0]],
                                                         scatter=False, name="gather_in")
    o1, o2, o3, o4 = sw, sw + cw, sw + 2 * cw, sw + 2 * cw + d
    in_bounds = ((0, o1), (o1, o2), (o2, o3), (o3, o4), (o4, o4 + d))
    w_u, w_cva, w_cvg, w_gs, w_gc = _unshard_cols(got_in, in_bounds, name="unshard_w_in")
    b_u, b_cva, b_cvg, b_gs, b_gc = (b_in[:, a:b] for a, b in in_bounds)
    c_all = got_c.reshape(NDEV, d)
    cv_taps = _cols_from_shards(got_cv_taps)
    ffn_taps = _cols_from_shards(got_ffn_taps)
    cv_w32 = _pad_rows(cv_taps, CONV_HALO)
    ffn_wa, ffn_wv = _pad_rows(ffn_taps[:, :fh], FFN_HALO), _pad_rows(ffn_taps[:, fh:], FFN_HALO)
    ffn_ba, ffn_bv = ffn_dw_b[:, :fh], ffn_dw_b[:, fh:]

    ncond = w_cond.shape[2]
    b_cond_mine = lax.dynamic_slice(b_cond, (0, me * ncond), (1, ncond))
    mod_cols = _cond_fwd(c_all, w_cond[0], b_cond_mine, name="cond_fwd")
    mod_all, = _exchange([mod_cols], scatter=False, name="gather_mod")
    mod_mine = lax.dynamic_slice(mod_all, (0, me, 0), (NDEV, 1, ncond)).reshape(1, 6 * d)
    sh1, sc1, g1, sh2, sc2, g2 = (mod_mine[:, k * d:(k + 1) * d] for k in range(6))

    lam_re, lam_im, log_dt = ssm_lambda_re[0], ssm_lambda_im[0], ssm_log_dt[0][:, None]
    lbr, lbi, cfr, cfi = _ssm_prep(lam_re, lam_im, log_dt, name="ssm_prep")
    rows_q = lambda a: a.reshape(nq, 1, QS)
    eye = jnp.eye(gq, dtype=f32)

    def b_mat(b):
        bt = b.reshape(nq, gq, SSM_STATE, SSM_GROUP).transpose(0, 1, 3, 2)
        return jnp.einsum("qgpn,gh->qgphn", bt, eye).reshape(nq, QW, QS)

    def c_mat(cc):
        ct = cc.reshape(nq, gq, SSM_GROUP, SSM_STATE)
        return jnp.einsum("qgpn,gh->qhngp", ct, eye).reshape(nq, QS, QW)

    def b_unmat(mt):
        return jnp.einsum("qgpgn->qgnp", mt.reshape(nq, gq, SSM_GROUP, gq, SSM_STATE)).reshape(ng, SSM_STATE, SSM_GROUP)

    def c_unmat(mt):
        return jnp.einsum("qgngp->qgpn", mt.reshape(nq, gq, SSM_STATE, gq, SSM_GROUP)).reshape(ng, SSM_GROUP, SSM_STATE)

    ssm_args = (rows_q(lbr), rows_q(lbi), b_mat(ssm_b_re[0]), b_mat(ssm_b_im[0]), c_mat(ssm_c_re[0]), c_mat(ssm_c_im[0]),
                rows_q(cfr), rows_q(cfi), ssm_d[0].reshape(1, sw))

    h1 = _lnmod(xs, sc1, sh1, name="ln_mod1")
    u, cva, cvg, gs, gc = _mm_fanout(h1, [w_u, w_cva, w_cvg, w_gs, w_gc], [b_u, b_cva, b_cvg, b_gs, b_gc],
                                     [f32, f32, f32, bf16, bf16], name="in_proj")
    v4, cv2, (got_a, got_b, got_pw, got_o) = _conv_fwd(
        cva, cvg, cv_w32, cv_dw_b, cv_ln_g, cv_ln_b, name="conv_fwd",
        xchg=(sent(["ssm_glu_w_a", "ssm_glu_w_b", "cv_w_pw", "w_out"]), False))
    h_p, yraw3, y, (got_up,) = _ssm_fwd(u, *ssm_args, name="ssm_fwd", xchg=(sent(["ffn_w_up"]), False))
    w_a, = _unshard_cols(got_a, ((0, d),), name="unshard_glu_a")
    w_b, = _unshard_cols(got_b, ((0, d),), name="unshard_glu_b")
    w_pw, = _unshard_cols(got_pw, ((0, d),), name="unshard_conv_pw")
    w_upa, w_upv = _unshard_cols(got_up, ((0, fh), (fh, 2 * fh)), name="unshard_ffn_up")
    w_o = got_o.reshape(d, d)
    ya, yb = _mm_fanout(y, [w_a, w_b], None, [bf16, bf16], name="glu")
    ycv = _mm([(v4, w_pw)], out_dtype=bf16, name="conv_pw")
    merged = _glu_merge(ya, yb, ycv, gs, gc, name="merge")
    o = _mm([(merged, w_o)], out_dtype=bf16, name="out_proj")
    x1, h2 = _resid_ln_mod(xs, o, g1, ln1_g, ln1_b, sc2, sh2, alpha, name="resid_ln1")
    upa = _mm([(h2, w_upa)], name="ffn_up_a")
    upv = _mm([(h2, w_upv)], name="ffn_up_v")
    f, (got_dn,) = _ffn_mid(upa, upv, ffn_wa, ffn_wv, ffn_ba, ffn_bv, name="ffn_mid", xchg=(sent(["ffn_w_down"]), False))
    w_dn = got_dn.reshape(fh, d)
    dr2, dy2, loss_part, d_ln2_g, d_ln2_b, d_g2 = _resid_ln_loss(x1, f, w_dn, g2, ln2_g, ln2_b, tgt, alpha, name="ffn_down_ln2_loss")

    gw = {}
    df = _mm([(dy2, w_dn)], trans_w=True, name="d_ffn_down")
    gw["ffn_w_down"] = _mm_tn(f, dy2, out_dtype=bf16, name="g_ffn_down").reshape((NDEV,) + ffn_w_down[0].shape)
    received = {}
    dupa, dupv, d_ffn_wa, d_ffn_wv, d_ffn_ba, d_ffn_bv, (received["ffn_w_down"],) = _ffn_mid_bwd(
        df, upa, upv, ffn_wa, ffn_wv, ffn_ba, ffn_bv, name="ffn_mid_bwd", xchg=([gw["ffn_w_down"]], True))
    dh2 = _mm([(dupa, w_upa), (dupv, w_upv)], trans_w=True, out_dtype=bf16, name="d_ffn_up")
    gw["ffn_w_up"] = _shard_cols([_mm_tn(h2, dupa, name="g_ffn_up_a"), _mm_tn(h2, dupv, name="g_ffn_up_v")], out_dtype=bf16,
                                 name="shard_ffn_up")
    dr1, do, d_sc2, d_sh2, d_ln1_g, d_ln1_b, d_g1 = _mid_bwd(dh2, x1, dr2, xs, o, g1, sc2, ln1_g, alpha, name="mid_bwd")
    dmerged = _mm([(do, w_o)], trans_w=True, out_dtype=bf16, name="d_out_proj")
    gw["w_out"] = _mm_tn(merged, do, out_dtype=bf16, name="g_out_proj").reshape((NDEV,) + w_out[0].shape)
    dya, dyb, dycv, dgs, dgc, s_gs, s_gc = _glu_merge_bwd(dmerged, ya, yb, ycv, gs, gc, name="merge_bwd")
    dy = _mm([(dya, w_a), (dyb, w_b)], trans_w=True, out_dtype=bf16, name="d_glu")
    gw["ssm_glu_w_a"] = _mm_tn_sharded(y, [dya], out_dtype=bf16, name="g_glu_a")
    gw["ssm_glu_w_b"] = _mm_tn_sharded(y, [dyb], out_dtype=bf16, name="g_glu_b")
    dv4 = _mm([(dycv, w_pw)], trans_w=True, out_dtype=bf16, name="d_conv_pw")
    gw["cv_w_pw"] = _mm_tn_sharded(v4, [dycv], out_dtype=bf16, name="g_conv_pw")
    dv2, d_cv_ln_g, d_cv_ln_b = _conv_bwd_ln(dv4, cv2, cv_ln_g, cv_ln_b, name="conv_bwd_ln")
    dcva, dcvg, d_cv_w32, d_cv_b, s_cva, s_cvg, (received["ffn_w_up"],) = _conv_bwd_taps(
        dv2, cva, cvg, cv_w32, name="conv_bwd_taps", xchg=([gw["ffn_w_up"]], True))
    late = ["w_out", "ssm_glu_w_a", "ssm_glu_w_b", "cv_w_pw"]
    (du, d_bre_m, d_bim_m, d_cre_m, d_cim_m, d_cfr, d_cfi, d_lbr, d_lbi, d_d, s_u, got_late) = _ssm_bwd(
        dy, yraw3, u, h_p, *ssm_args, name="ssm_bwd", xchg=([gw[n] for n in late], True))
    received.update(zip(late, got_late))
    gshape = lam_re.shape
    d_lam_re, d_lam_im, d_log_dt = _ssm_prep_bwd(
        lam_re, lam_im, log_dt, [a.reshape(gshape) for a in (d_lbr, d_lbi, d_cfr, d_cfi)], name="ssm_prep_bwd")
    small = {
        "b_in": jnp.concatenate([s_u, s_cva, s_cvg, s_gs, s_gc], axis=1),
        "ssm_lambda_re": d_lam_re, "ssm_lambda_im": d_lam_im, "ssm_log_dt": d_log_dt,
        "ssm_b_re": b_unmat(d_bre_m), "ssm_b_im": b_unmat(d_bim_m), "ssm_c_re": c_unmat(d_cre_m), "ssm_c_im": c_unmat(d_cim_m),
        "ssm_d": d_d, "cv_dw_w": d_cv_w32[:CONV_K], "cv_dw_b": d_cv_b, "cv_ln_g": d_cv_ln_g, "cv_ln_b": d_cv_ln_b,
        "ln1_g": d_ln1_g, "ln1_b": d_ln1_b,
        "ffn_dw_w": jnp.concatenate([d_ffn_wa[:FFN_K], d_ffn_wv[:FFN_K]], axis=1),
        "ffn_dw_b": jnp.concatenate([d_ffn_ba, d_ffn_bv], axis=1), "ln2_g": d_ln2_g, "ln2_b": d_ln2_b,
        "mod_g1": d_g1, "mod_sh2": d_sh2, "mod_sc2": d_sc2, "mod_g2": d_g2, "loss": loss_part,
    }
    small_names = list(small)
    small_shapes = [small[n].shape for n in small_names]
    gw["w_in"], (small_all,) = _mm_tn_sharded(h1, [du, dcva, dcvg, dgs, dgc], out_dtype=bf16, name="g_in",
                                              xchg=([_pack([small[n] for n in small_names])], False))
    dh1, (received["w_in"],) = _mm(
        [(du, w_u), (dcva, w_cva), (dcvg, w_cvg), (dgs, w_gs), (dgc, w_gc)], trans_w=True, out_dtype=bf16, name="d_in",
        xchg=([gw["w_in"]], True))
    grad_x, d_sc1, d_sh1, _ = _final_bwd(dh1, xs, dr1, sc1, alpha, name="final_bwd")

    grads, delta, new_m, new_v = {}, {}, {}, {}
    for n in big:
        ride = ([_pack([d_sh1, d_sc1])], False) if n == "w_out" else None
        res = _sum_adamw(received[n], weights[n][0], mom_m[n][0], mom_v[n][0], name="adamw_" + n, xchg=ride)
        grads[n], delta[n], new_m[n], new_v[n] = res[:4]
        if ride is not None:
            last_all, = res[4]

    small_sum = dict(zip(small_names, _unpack(_sum_parts(small_all, name="sum_small").reshape(-1), small_shapes)))
    last_sum = _unpack(_sum_parts(last_all, name="sum_last").reshape(-1), [(1, d), (1, d)])
    per_dev = dict(zip(small_names, _unpack(small_all.reshape(NDEV, -1), small_shapes)))
    last_dev = _unpack(last_all.reshape(NDEV, -1), [(1, d), (1, d)])
    dmod_all = jnp.concatenate(last_dev + [per_dev[k] for k in ("mod_g1", "mod_sh2", "mod_sc2", "mod_g2")], axis=-1).reshape(NDEV, 6 * d)
    dmod_cols = lax.dynamic_slice(dmod_all.reshape(NDEV, NDEV, ncond), (0, me, 0), (NDEV, 1, ncond)).reshape(NDEV, ncond)
    grads["w_cond"] = _cond_bwd(c_all, dmod_cols, name="cond_bwd")
    loss = small_sum.pop("loss").reshape(())
    grads["b_cond"] = jnp.concatenate(last_sum + [small_sum.pop(k) for k in ("mod_g1", "mod_sh2", "mod_sc2", "mod_g2")], axis=1)
    for n, g in small_sum.items():
        grads[n] = g
    ntap = cv_dw_w.shape[3]
    grads["cv_dw_w"] = lax.dynamic_slice(grads["cv_dw_w"], (0, me * ntap), (CONV_K, ntap))
    nffn = ffn_dw_w.shape[3]
    grads["ffn_dw_w"] = lax.dynamic_slice(grads["ffn_dw_w"], (0, me * nffn), (FFN_K, nffn))
    grads = {n: grads[n].reshape(weights[n].shape) for n in names}

    delta["w_cond"], new_m["w_cond"], new_v["w_cond"] = _adamw(w_cond[0], grads["w_cond"][0], m_w_cond[0], v_w_cond[0],
                                                               name="adamw_w_cond")
    rest = [n for n in names if n not in ["w_cond"] + big]
    squeeze = lambda a: a if a.ndim == 2 else a[0]
    results = _adamw_many(*[[squeeze(t[n].reshape(weights[n].shape)) for n in rest] for t in (weights, grads, mom_m, mom_v)],
                          name="adamw_small")
    for n, (dl, nm, nv) in zip(rest, results):
        delta[n], new_m[n], new_v[n] = dl, nm, nv
    shaped = lambda t: [t[n].reshape(weights[n].shape) for n in names]

    return (loss, grad_x[None], *shaped(grads), *shaped(delta), *shaped(new_m), *shaped(new_v))
```

```python
import functools
import math

import jax
import jax.numpy as jnp
from jax import lax
from jax.experimental import pallas as pl
from jax.experimental.pallas import tpu as pltpu

f32 = jnp.float32
bf16 = jnp.bfloat16

NDEV = 8
LANES = 128
SUBLANES = 8
SSM_GROUP = 16
SSM_STATE = 64
QW = 128
QS = 512
CONV_K = 31
CONV_HALO = 32
FFN_K = 3
FFN_HALO = 8
LN_EPS = 1e-5
ADAM_LR, ADAM_B1, ADAM_B2, ADAM_EPS, ADAM_WD, ADAM_STEP = 0.001, 0.9, 0.999, 1e-08, 0.01, 10
VMEM_LIMIT = 56 * 1024 * 1024
W_TILE_BYTES = 6 * 1024 * 1024
MM_ROWS = 1024
EW_BLOCK_BYTES = 2 * 1024 * 1024
INV_SQRT2 = 1.0 / math.sqrt(2.0)
INV_SQRT_2PI = 1.0 / math.sqrt(2.0 * math.pi)
MESH = pl.DeviceIdType.MESH


def _tile(n, want):
    t = min(n, want)
    while n % t:
        t //= 2
    return t


def _col_tile(n, rows, bytes_per):
    best = LANES if n % LANES == 0 else n
    for t in range(LANES, n + 1, LANES):
        if n % t == 0 and rows * t * bytes_per <= W_TILE_BYTES:
            best = t
    return best


def _params(*sem):
    return pltpu.CompilerParams(dimension_semantics=sem, vmem_limit_bytes=VMEM_LIMIT)


def _row(i):
    return (0, 0)


def _full(shape):
    nd = len(shape)
    return pl.BlockSpec(shape, lambda *a: (0,) * nd)


def _ln(x):
    mu = jnp.mean(x, axis=-1, keepdims=True)
    xc = x - mu
    var = jnp.mean(xc * xc, axis=-1, keepdims=True)
    rstd = lax.rsqrt(var + LN_EPS)
    return xc * rstd, rstd


def _ln_bwd(dxhat, xhat, rstd):
    return rstd * (dxhat - jnp.mean(dxhat, axis=-1, keepdims=True) - xhat * jnp.mean(dxhat * xhat, axis=-1, keepdims=True))


def _sig(x):
    return 1.0 / (1.0 + jnp.exp(-x))


def _gelu(x):
    return 0.5 * x * (1.0 + lax.erf(x * INV_SQRT2))


def _gelu_grad(x):
    return 0.5 * (1.0 + lax.erf(x * INV_SQRT2)) + x * jnp.exp(-0.5 * x * x) * INV_SQRT_2PI


def _colsum(x):
    return jnp.sum(x, axis=0, keepdims=True)


def _mm(pairs, bias=None, *, trans_w=False, out_dtype=f32, name, xchg=None):
    n_p = len(pairs)
    m = pairs[0][0].shape[0]
    n = pairs[0][1].shape[0 if trans_w else 1]
    ktot = sum(x.shape[1] for x, _ in pairs)
    tm = _tile(m, MM_ROWS)
    tn = _col_tile(n, ktot, 2)
    dn = (((1,), (1,)), ((), ())) if trans_w else (((1,), (0,)), ((), ()))

    def body(*refs):
        o_ref = refs[-1]
        acc = None
        for xr, wr in zip(refs[:n_p], refs[n_p:2 * n_p]):
            r = lax.dot_general(xr[...].astype(bf16), wr[...].astype(bf16), dn, preferred_element_type=f32)
            acc = r if acc is None else acc + r
        if bias is not None:
            acc = acc + refs[2 * n_p][...]
        o_ref[...] = acc.astype(out_dtype)

    in_specs = [pl.BlockSpec((tm, x.shape[1]), lambda j, i: (i, 0)) for x, _ in pairs]
    if trans_w:
        in_specs += [pl.BlockSpec((tn, w.shape[1]), lambda j, i: (j, 0)) for _, w in pairs]
    else:
        in_specs += [pl.BlockSpec((w.shape[0], tn), lambda j, i: (0, j)) for _, w in pairs]
    args = [x for x, _ in pairs] + [w for _, w in pairs]
    if bias is not None:
        in_specs.append(pl.BlockSpec((1, tn), lambda j, i: (0, j)))
        args.append(bias)
    (out,), got = _call(
        body, args, name=name, grid=(n // tn, m // tm), in_specs=in_specs,
        out_specs=[pl.BlockSpec((tm, tn), lambda j, i: (i, j))], out_shape=[jax.ShapeDtypeStruct((m, n), out_dtype)],
        sem=("parallel", "arbitrary"), xchg=xchg)
    return out if xchg is None else (out, got)


def _mm_fanout(x, ws, biases, out_dtypes, *, name):
    m, k = x.shape
    tm = _tile(m, MM_ROWS)
    n_w = len(ws)
    biases = list(biases or [])

    def body(x_ref, *refs):
        xb = x_ref[...].astype(bf16)
        o_refs = refs[n_w + len(biases):]
        for p, (w_ref, o_ref, dt) in enumerate(zip(refs[:n_w], o_refs, out_dtypes)):
            acc = jnp.dot(xb, w_ref[...], preferred_element_type=f32)
            if biases:
                acc = acc + refs[n_w + p][...]
            o_ref[...] = acc.astype(dt)

    return pl.pallas_call(
        body, name=name, grid=(m // tm,),
        in_specs=[pl.BlockSpec((tm, k), lambda i: (i, 0))] + [_full(w.shape) for w in ws] + [_full(b.shape) for b in biases],
        out_specs=[pl.BlockSpec((tm, w.shape[1]), lambda i: (i, 0)) for w in ws],
        out_shape=[jax.ShapeDtypeStruct((m, w.shape[1]), dt) for w, dt in zip(ws, out_dtypes)],
        compiler_params=_params("parallel"))(x, *ws, *biases)


def _mm_tn(x, dy, *, out_dtype=f32, name):
    m, k = x.shape
    n = dy.shape[1]
    tm = _tile(m, MM_ROWS)
    tn = _col_tile(n, k, 4)
    steps = m // tm

    def body(x_ref, dy_ref, o_ref, *scratch):
        acc = scratch[0] if scratch else o_ref

        @pl.when(pl.program_id(1) == 0)
        def _():
            acc[...] = jnp.zeros_like(acc)

        acc[...] += lax.dot_general(x_ref[...].astype(bf16), dy_ref[...].astype(bf16), (((0,), (0,)), ((), ())),
                                    preferred_element_type=f32)
        if scratch:
            @pl.when(pl.program_id(1) == steps - 1)
            def _():
                o_ref[...] = acc[...].astype(out_dtype)

    return pl.pallas_call(
        body, name=name, grid=(n // tn, steps),
        in_specs=[pl.BlockSpec((tm, k), lambda j, i: (i, 0)), pl.BlockSpec((tm, tn), lambda j, i: (i, j))],
        out_specs=pl.BlockSpec((k, tn), lambda j, i: (0, j)),
        out_shape=jax.ShapeDtypeStruct((k, n), out_dtype),
        scratch_shapes=[] if out_dtype == f32 else [pltpu.VMEM((k, tn), f32)],
        compiler_params=_params("parallel", "arbitrary"),
    )(x, dy)


def _mm_tn_sharded(x, dys, *, out_dtype, name, xchg=None):
    m, k = x.shape
    widths = [dy.shape[1] for dy in dys]
    n = sum(widths) // NDEV
    tm = _tile(m, 512)
    steps = m // tm
    n_d = len(dys)

    def body(x_ref, *refs):
        dy_refs, o_ref, acc = refs[:n_d], refs[n_d], refs[n_d + 1]
        i = pl.program_id(0)

        @pl.when(i == 0)
        def _():
            acc[...] = jnp.zeros_like(acc)

        xb = x_ref[...].astype(bf16)
        off = 0
        for dy_ref, w in zip(dy_refs, widths):
            acc[:, off:off + w] += lax.dot_general(xb, dy_ref[...].astype(bf16), (((0,), (0,)), ((), ())), preferred_element_type=f32)
            off += w

        @pl.when(i == steps - 1)
        def _():
            for j in range(NDEV):
                o_ref[j] = acc[:, n * j:n * (j + 1)].astype(out_dtype)

    (out,), got = _call(
        body, (x, *dys), name=name, grid=(steps,),
        in_specs=[pl.BlockSpec((tm, k), lambda i: (i, 0))] + [pl.BlockSpec((tm, w), lambda i: (i, 0)) for w in widths],
        out_specs=[pl.BlockSpec((NDEV, k, n), lambda i: (0, 0, 0))], out_shape=[jax.ShapeDtypeStruct((NDEV, k, n), out_dtype)],
        scratch_shapes=[pltpu.VMEM((k, sum(widths)), f32)], sem=("arbitrary",), xchg=xchg)
    return out if xchg is None else (out, got)


def _exchange(arrs, *, scatter, name):
    n = len(arrs)

    def body(*refs):
        _exchange_copies(refs[:n], refs[n:2 * n], refs[2 * n:], scatter, True, True)

    return pl.pallas_call(
        body, name=name, in_specs=[HBM_SPEC] * n, out_specs=[HBM_SPEC] * n, out_shape=_exchange_out_shape(arrs, scatter),
        scratch_shapes=_exchange_sems(n),
    )(*arrs)


HBM_SPEC = pl.BlockSpec(memory_space=pltpu.HBM)


def _flags(scatter, n):
    return list(scatter) if isinstance(scatter, (list, tuple)) else [scatter] * n


def _exchange_out_shape(arrs, scatter):
    return [jax.ShapeDtypeStruct(a.shape if sc else (NDEV,) + a.shape, a.dtype) for a, sc in zip(arrs, _flags(scatter, len(arrs)))]


def _exchange_sems(n):
    return [pltpu.SemaphoreType.DMA(((NDEV - 1) * n,)), pltpu.SemaphoreType.DMA(((NDEV - 1) * n,)), pltpu.SemaphoreType.DMA((n,))]


def _exchange_copies(x_refs, o_refs, sems, scatter, start, wait):
    n = len(x_refs)
    flags = _flags(scatter, n)
    send_sems, recv_sems, local_sems = sems
    ix, iy, ic = lax.axis_index("x"), lax.axis_index("y"), lax.axis_index("c")
    me = 4 * ix + 2 * iy + ic
    local = [pltpu.make_async_copy(x.at[me] if sc else x, o.at[me], local_sems.at[a])
             for a, (x, o, sc) in enumerate(zip(x_refs, o_refs, flags))]

    def peer(k):
        return (1 - ix if k & 4 else ix, 1 - iy if k & 2 else iy, 1 - ic if k & 1 else ic)

    def index(p):
        return 4 * p[0] + 2 * p[1] + p[2]

    def copy(a, k, src, dst, to):
        sem = (k - 1) * n + a
        return pltpu.make_async_remote_copy(src_ref=src, dst_ref=dst, send_sem=send_sems.at[sem], recv_sem=recv_sems.at[sem],
                                            device_id=to, device_id_type=MESH)

    sends, arrivals, passed_on = [], [], []
    for a, (x, o, sc) in enumerate(zip(x_refs, o_refs, flags)):
        if sc:
            for k in range(1, NDEV):
                p = peer(k)
                sends.append(copy(a, k, x.at[index(p)], o.at[me], p))
                arrivals.append(copy(a, k, x.at[me], o.at[index(p)], p))
        else:
            sib = peer(1)
            sends.append(copy(a, 1, x, o.at[me], sib))
            arrivals.append(copy(a, 1, x, o.at[index(sib)], sib))
            for k in (2, 4, 6):
                p, q = peer(k), peer(k + 1)
                sends.append(copy(a, k, x, o.at[me], p))
                passed_on.append((copy(a, k, x, o.at[index(p)], p), copy(a, k + 1, o.at[index(p)], o.at[index(p)], sib)))
                arrivals.append(copy(a, k + 1, o.at[index(q)], o.at[index(q)], sib))
    if start:
        for cp in local + sends:
            cp.start()
    if wait:
        for landed, hand_over in passed_on:
            landed.wait_recv()
            hand_over.start()
        for cp in arrivals:
            cp.wait_recv()
        for cp in sends + [hand_over for _, hand_over in passed_on]:
            cp.wait_send()
        for cp in local:
            cp.wait()


def _call(body, args, *, name, grid, in_specs, out_specs, out_shape, scratch_shapes=(), sem, xchg=None):
    if xchg is None:
        return pl.pallas_call(body, name=name, grid=grid, in_specs=in_specs, out_specs=out_specs, out_shape=out_shape,
                              scratch_shapes=list(scratch_shapes), compiler_params=_params(*sem))(*args), None
    arrs, scatter = xchg
    n, ni, no, ns = len(arrs), len(in_specs), len(out_specs), len(scratch_shapes)

    def wrapped(*refs):
        ins, x_refs = refs[:ni], refs[ni:ni + n]
        outs, o_refs = refs[ni + n:ni + n + no], refs[ni + n + no:ni + 2 * n + no]
        scratch, sems = refs[ni + 2 * n + no:ni + 2 * n + no + ns], refs[ni + 2 * n + no + ns:]
        ids = [pl.program_id(a) for a in range(len(grid))]
        first = functools.reduce(jnp.logical_and, [p == 0 for p in ids])
        last = functools.reduce(jnp.logical_and, [p == g - 1 for p, g in zip(ids, grid)])

        @pl.when(first)
        def _():
            _exchange_copies(x_refs, o_refs, sems, scatter, True, False)

        body(*ins, *outs, *scratch)

        @pl.when(last)
        def _():
            _exchange_copies(x_refs, o_refs, sems, scatter, False, True)

    res = pl.pallas_call(
        wrapped, name=name, grid=grid, in_specs=list(in_specs) + [HBM_SPEC] * n, out_specs=list(out_specs) + [HBM_SPEC] * n,
        out_shape=list(out_shape) + _exchange_out_shape(arrs, scatter),
        scratch_shapes=list(scratch_shapes) + _exchange_sems(n),
        compiler_params=_params(*("arbitrary",) * len(grid)))(*args, *arrs)
    return res[:no], res[no:]


def _sum_parts(parts, *, name):
    r = parts.shape[1]

    def body(p_ref, o_ref):
        acc = p_ref[0]
        for j in range(1, NDEV):
            acc = acc + p_ref[j]
        o_ref[...] = acc

    return pl.pallas_call(body, name=name, out_shape=jax.ShapeDtypeStruct((r, LANES), f32), compiler_params=_params())(parts)


def _col_pieces(n, bounds):
    out = []
    for p, (a, b) in enumerate(bounds):
        for j in range(NDEV):
            lo, hi = max(a, n * j), min(b, n * (j + 1))
            if lo < hi:
                out.append((p, j, lo - a, lo - n * j, hi - lo))
    return out


def _unshard_cols(stacked, bounds, *, name):
    _, k, n = stacked.shape
    tk = _tile(k, 256)
    plan = _col_pieces(n, bounds)

    def body(x_ref, *o_refs):
        for p, j, po, so, w in plan:
            o_refs[p][:, po:po + w] = x_ref[j, :, so:so + w]

    return pl.pallas_call(
        body, name=name, grid=(k // tk,), in_specs=[pl.BlockSpec((NDEV, tk, n), lambda i: (0, i, 0))],
        out_specs=[pl.BlockSpec((tk, b - a), lambda i: (i, 0)) for a, b in bounds],
        out_shape=[jax.ShapeDtypeStruct((k, b - a), stacked.dtype) for a, b in bounds],
        compiler_params=_params("parallel"))(stacked)


def _shard_cols(pieces, *, out_dtype, name):
    k = pieces[0].shape[0]
    bounds, off = [], 0
    for p in pieces:
        bounds.append((off, off + p.shape[1]))
        off += p.shape[1]
    n = off // NDEV
    tk = _tile(k, 256)
    plan = _col_pieces(n, bounds)

    def body(*refs):
        o_ref = refs[-1]
        for p, j, po, so, w in plan:
            o_ref[j, :, so:so + w] = refs[p][:, po:po + w].astype(out_dtype)

    return pl.pallas_call(
        body, name=name, grid=(k // tk,), in_specs=[pl.BlockSpec((tk, b - a), lambda i: (i, 0)) for a, b in bounds],
        out_specs=pl.BlockSpec((NDEV, tk, n), lambda i: (0, i, 0)),
        out_shape=jax.ShapeDtypeStruct((NDEV, k, n), out_dtype),
        compiler_params=_params("parallel"))(*pieces)


def _pack(arrs):
    flat = jnp.concatenate([a.reshape(-1) for a in arrs])
    pad = (-flat.shape[0]) % (SUBLANES * LANES)
    return jnp.pad(flat, (0, pad)).reshape(-1, LANES)


def _unpack(flat, shapes):
    out, off = [], 0
    for s in shapes:
        n = math.prod(s)
        out.append(flat[..., off:off + n].reshape(flat.shape[:-1] + tuple(s)))
        off += n
    return out


def _adamw_math(w, gg, m, v):
    nm = ADAM_B1 * m + (1.0 - ADAM_B1) * gg
    nv = ADAM_B2 * v + (1.0 - ADAM_B2) * (gg * gg)
    m_hat = nm / (1.0 - ADAM_B1 ** ADAM_STEP)
    v_hat = nv / (1.0 - ADAM_B2 ** ADAM_STEP)
    return -ADAM_LR * (m_hat / (jnp.sqrt(v_hat) + ADAM_EPS) + ADAM_WD * w), nm, nv


def _row_block(r, c, copies):
    tr = r
    while copies * tr * c * 4 > EW_BLOCK_BYTES and tr % (4 * SUBLANES) == 0:
        tr //= 2
    return tr


def _adamw(w, g, m, v, *, name):
    r, c = w.shape
    tr = _row_block(r, c, 1)

    def body(w_ref, g_ref, m_ref, v_ref, d_ref, nm_ref, nv_ref):
        d_ref[...], nm_ref[...], nv_ref[...] = _adamw_math(w_ref[...], g_ref[...], m_ref[...], v_ref[...])

    spec = pl.BlockSpec((tr, c), lambda i: (i, 0))
    shp = jax.ShapeDtypeStruct((r, c), f32)
    return pl.pallas_call(
        body, name=name, grid=(r // tr,), in_specs=[spec] * 4, out_specs=[spec] * 3, out_shape=[shp] * 3,
        compiler_params=_params("parallel"),
    )(w, g, m, v)


def _adamw_many(ws, gs, ms, vs, *, name):
    n = len(ws)

    def body(*refs):
        outs = refs[4 * n:]
        for i in range(n):
            res = _adamw_math(refs[i][...], refs[n + i][...], refs[2 * n + i][...], refs[3 * n + i][...])
            for o_ref, r in zip(outs[3 * i:3 * i + 3], res):
                o_ref[...] = r

    res = pl.pallas_call(body, name=name, out_shape=[jax.ShapeDtypeStruct(w.shape, f32) for w in ws for _ in range(3)],
                         compiler_params=_params())(*ws, *gs, *ms, *vs)
    return [res[3 * i:3 * i + 3] for i in range(n)]


def _sum_adamw(parts, w, m, v, *, name, xchg=None):
    r, c = w.shape
    tr = _row_block(r, c, NDEV)

    def body(p_ref, w_ref, m_ref, v_ref, g_ref, d_ref, nm_ref, nv_ref):
        gg = p_ref[0].astype(f32)
        for j in range(1, NDEV):
            gg = gg + p_ref[j].astype(f32)
        g_ref[...] = gg
        d_ref[...], nm_ref[...], nv_ref[...] = _adamw_math(w_ref[...], gg, m_ref[...], v_ref[...])

    spec = pl.BlockSpec((tr, c), lambda i: (i, 0))
    shp = jax.ShapeDtypeStruct((r, c), f32)
    res, got = _call(
        body, (parts, w, m, v), name=name, grid=(r // tr,),
        in_specs=[pl.BlockSpec((NDEV, tr, c), lambda i: (0, i, 0))] + [spec] * 3,
        out_specs=[spec] * 4, out_shape=[shp] * 4, sem=("parallel",), xchg=xchg)
    return tuple(res) if xchg is None else tuple(res) + (got,)


def _cond_fwd(c_all, w, b, *, name):
    nb, n = c_all.shape[0], w.shape[1]

    def body(c_ref, w_ref, b_ref, o_ref):
        cc = c_ref[...]
        o_ref[...] = jnp.dot(cc * _sig(cc), w_ref[...], preferred_element_type=f32,
                             precision=lax.Precision.HIGHEST) + b_ref[...]

    return pl.pallas_call(body, name=name, out_shape=jax.ShapeDtypeStruct((nb, n), f32),
                          compiler_params=_params())(c_all, w, b)


def _cond_bwd(c_all, dmod, *, name):
    d, n = c_all.shape[1], dmod.shape[1]

    def body(c_ref, g_ref, o_ref):
        cc = c_ref[...]
        o_ref[...] = lax.dot_general(cc * _sig(cc), g_ref[...], (((0,), (0,)), ((), ())), preferred_element_type=f32,
                                     precision=lax.Precision.HIGHEST)

    return pl.pallas_call(body, name=name, out_shape=jax.ShapeDtypeStruct((d, n), f32),
                          compiler_params=_params())(c_all, dmod)


def _ssm_disc(lam_re, lam_im, log_dt):
    lr = jnp.minimum(lam_re, -1e-4)
    li = lam_im
    dt = jnp.exp(log_dt)
    mag = jnp.exp(lr * dt)
    ang = li * dt
    lbr, lbi = mag * jnp.cos(ang), mag * jnp.sin(ang)
    num_r, num_i = lbr - 1.0, lbi
    den = lr * lr + li * li
    return lbr, lbi, (num_r * lr + num_i * li) / den, (num_i * lr - num_r * li) / den


def _ssm_prep(lam_re, lam_im, log_dt, *, name):
    def body(a, b, c, o1, o2, o3, o4):
        o1[...], o2[...], o3[...], o4[...] = _ssm_disc(a[...], b[...], c[...])

    shp = jax.ShapeDtypeStruct(lam_re.shape, f32)
    return pl.pallas_call(body, name=name, out_shape=[shp] * 4, compiler_params=_params())(lam_re, lam_im, log_dt)


def _ssm_prep_bwd(lam_re, lam_im, log_dt, cts, *, name):
    def body(a, b, c, g1, g2, g3, g4, o1, o2, o3):
        _, vjp = jax.vjp(_ssm_disc, a[...], b[...], c[...])
        o1[...], o2[...], o3[...] = vjp((g1[...], g2[...], g3[...], g4[...]))

    shp = jax.ShapeDtypeStruct(lam_re.shape, f32)
    return pl.pallas_call(body, name=name, out_shape=[shp, shp, jax.ShapeDtypeStruct(log_dt.shape, f32)],
                          compiler_params=_params())(lam_re, lam_im, log_dt, *cts)


S5_ROWS = 512


def _step_major(x3):
    k, nt, c = x3.shape
    return jnp.swapaxes(x3, 0, 1).reshape(k * nt, c)


def _chunk_major(x2, nt):
    return jnp.swapaxes(x2.reshape(nt, SUBLANES, x2.shape[1]), 0, 1)


def _chain_carries(loc_r, loc_i, pr, pi_, forward):
    row = lax.broadcasted_iota(jnp.int32, loc_r.shape, 0)
    shift = 1 if forward else SUBLANES - 1
    order = range(1, SUBLANES) if forward else range(SUBLANES - 2, -1, -1)
    er, ei = loc_r, loc_i
    for k in order:
        sr, si = pltpu.roll(er, shift, 0), pltpu.roll(ei, shift, 0)
        er = jnp.where(row == k, loc_r + pr * sr - pi_ * si, er)
        ei = jnp.where(row == k, loc_i + pr * si + pi_ * sr, ei)
    edge = 0 if forward else SUBLANES - 1
    return (jnp.where(row == edge, 0.0, pltpu.roll(er, shift, 0)), jnp.where(row == edge, 0.0, pltpu.roll(ei, shift, 0)))


def _chunk_power(ar, ai, chunk_len):
    pr, pi_ = ar, ai
    for _ in range(int(math.log2(chunk_len))):
        pr, pi_ = pr * pr - pi_ * pi_, 2.0 * pr * pi_
    return pr, pi_


def _ssm_mats(bre_ref, bim_ref, cre_ref, cim_ref, cfr_ref, cfi_ref, bbar_s, cmat_s, nq):
    for q in range(nq):
        cr, ci, br, bi = cfr_ref[q], cfi_ref[q], bre_ref[q], bim_ref[q]
        bbar_s[q, :, 0:QS] = (cr * br - ci * bi).astype(bf16)
        bbar_s[q, :, QS:2 * QS] = (cr * bi + ci * br).astype(bf16)
        cmat_s[q, 0:QS, :] = cre_ref[q].astype(bf16)
        cmat_s[q, QS:2 * QS, :] = (-cim_ref[q]).astype(bf16)


def _ssm_fwd(u, ar, ai, bre, bim, cre, cim, cfr, cfi, dvec, *, name, xchg=None):
    s, sw = u.shape
    nq = sw // QW
    st = nq * 2 * QS
    tb = _tile(s, S5_ROWS)
    nb, nt, chunk_len = s // tb, tb // SUBLANES, s // SUBLANES
    assert chunk_len & (chunk_len - 1) == 0 and nt % 16 == 0

    def body(u_ref, ar_ref, ai_ref, bre_ref, bim_ref, cre_ref, cim_ref, cfr_ref, cfi_ref, d_ref,
             h_out, yraw_out, y_out, buf, hc, bbar_s, cmat_s):
        ph, i = pl.program_id(0), pl.program_id(1)

        @pl.when(i == 0)
        def _():
            _ssm_mats(bre_ref, bim_ref, cre_ref, cim_ref, cfr_ref, cfi_ref, bbar_s, cmat_s, nq)

        @pl.when((ph == 0) & (i == 0))
        def _():
            hc[...] = jnp.zeros_like(hc)

        @pl.when((ph == 1) & (i == 0))
        def _():
            for q in range(nq):
                o = q * 2 * QS
                pr, pi_ = _chunk_power(ar_ref[q], ai_ref[q], chunk_len)
                sr, si = _chain_carries(hc[:, o:o + QS], hc[:, o + QS:o + 2 * QS], pr, pi_, True)
                hc[:, o:o + QS] = sr
                hc[:, o + QS:o + 2 * QS] = si

        uu = u_ref[...]
        up = _step_major(uu).astype(bf16)
        for q in range(nq):
            o = q * 2 * QS
            buf[:, o:o + 2 * QS] = jnp.dot(up[:, q * QW:(q + 1) * QW], bbar_s[q], preferred_element_type=f32)

        for q in range(nq):
            o = q * 2 * QS
            a_r = jnp.broadcast_to(ar_ref[q], (SUBLANES, QS))
            a_i = jnp.broadcast_to(ai_ref[q], (SUBLANES, QS))

            def step(t, carry, o=o, a_r=a_r, a_i=a_i):
                hr, hi = carry
                r0 = pl.multiple_of(t * SUBLANES, SUBLANES)
                nr = a_r * hr - a_i * hi + buf[pl.ds(r0, SUBLANES), o:o + QS]
                ni = a_r * hi + a_i * hr + buf[pl.ds(r0, SUBLANES), o + QS:o + 2 * QS]
                buf[pl.ds(r0, SUBLANES), o:o + QS] = nr
                buf[pl.ds(r0, SUBLANES), o + QS:o + 2 * QS] = ni
                return nr, ni

            hr, hi = lax.fori_loop(0, nt, step, (hc[:, o:o + QS], hc[:, o + QS:o + 2 * QS]))
            hc[:, o:o + QS] = hr
            hc[:, o + QS:o + 2 * QS] = hi

        @pl.when(ph == 1)
        def _():
            for q in range(nq):
                o = q * 2 * QS
                cs = slice(q * QW, (q + 1) * QW)
                hq = buf[:, o:o + 2 * QS].astype(bf16)
                h_out[:, o:o + 2 * QS] = hq
                yq = _chunk_major(jnp.dot(hq, cmat_s[q], preferred_element_type=f32), nt) + d_ref[:, cs] * uu[:, :, cs]
                yraw_out[:, :, cs] = yq
                y_out[:, :, cs] = _gelu(yq).astype(bf16)

    blk = lambda ph, i: (0, i, 0)
    oblk = lambda ph, i: (0, i * ph, 0)
    act = lambda dt: jax.ShapeDtypeStruct((SUBLANES, chunk_len, sw), dt)
    (h_p, yraw3, y3), got = _call(
        body, (u.reshape(SUBLANES, chunk_len, sw), ar, ai, bre, bim, cre, cim, cfr, cfi, dvec), name=name, grid=(2, nb),
        in_specs=[pl.BlockSpec((SUBLANES, nt, sw), blk), _full(ar.shape), _full(ai.shape), _full(bre.shape), _full(bim.shape),
                  _full(cre.shape), _full(cim.shape), _full(cfr.shape), _full(cfi.shape), _full(dvec.shape)],
        out_specs=[pl.BlockSpec((tb, st), lambda ph, i: (i * ph, 0)), pl.BlockSpec((SUBLANES, nt, sw), oblk),
                   pl.BlockSpec((SUBLANES, nt, sw), oblk)],
        out_shape=[jax.ShapeDtypeStruct((s, st), bf16), act(f32), act(bf16)],
        scratch_shapes=[pltpu.VMEM((tb, st), f32), pltpu.VMEM((SUBLANES, st), f32),
                        pltpu.VMEM((nq, QW, 2 * QS), bf16), pltpu.VMEM((nq, 2 * QS, QW), bf16)],
        sem=("arbitrary", "arbitrary"), xchg=xchg)
    return h_p, yraw3, y3.reshape(s, sw), got


def _ssm_bwd(dy, yraw3, u, h_p, ar, ai, bre, bim, cre, cim, cfr, cfi, dvec, *, name, xchg=None):
    s, sw = u.shape
    nq = sw // QW
    st = nq * 2 * QS
    tb = _tile(s, S5_ROWS)
    nb, nt, chunk_len = s // tb, tb // SUBLANES, s // SUBLANES

    def body(dy_ref, yraw_ref, u_ref, h_ref, ar_ref, ai_ref, bre_ref, bim_ref, cre_ref, cim_ref, cfr_ref, cfi_ref, d_ref,
             du_out, dbre_out, dbim_out, dcre_out, dcim_out, dcfr_out, dcfi_out, dlbr_out, dlbi_out, dd_out, dbu_out,
             buf, rc, acc, dbbar, dcmat, bbar_s, cmat_s):
        ph, i = pl.program_id(0), pl.program_id(1)

        @pl.when(i == 0)
        def _():
            _ssm_mats(bre_ref, bim_ref, cre_ref, cim_ref, cfr_ref, cfi_ref, bbar_s, cmat_s, nq)

        @pl.when((ph == 0) & (i == 0))
        def _():
            rc[...] = jnp.zeros_like(rc)

        @pl.when((ph == 1) & (i == 0))
        def _():
            for q in range(nq):
                o = q * 2 * QS
                pr, pi_ = _chunk_power(ar_ref[q], ai_ref[q], chunk_len)
                sr, si = _chain_carries(rc[:, o:o + QS], rc[:, o + QS:o + 2 * QS], pr, -pi_, False)
                rc[:, o:o + QS] = sr
                rc[:, o + QS:o + 2 * QS] = si
            acc[...] = jnp.zeros_like(acc)
            dbbar[...] = jnp.zeros_like(dbbar)
            dcmat[...] = jnp.zeros_like(dcmat)
            dd_out[...] = jnp.zeros_like(dd_out)
            dbu_out[...] = jnp.zeros_like(dbu_out)

        dyraw = dy_ref[...].astype(f32) * _gelu_grad(yraw_ref[...])
        dyp = _step_major(dyraw).astype(bf16)
        for q in range(nq):
            o = q * 2 * QS
            buf[:, o:o + 2 * QS] = lax.dot_general(dyp[:, q * QW:(q + 1) * QW], cmat_s[q], (((1,), (1,)), ((), ())),
                                                   preferred_element_type=f32)

        def recur(with_grad):
            for q in range(nq):
                o = q * 2 * QS
                a_r = jnp.broadcast_to(ar_ref[q], (SUBLANES, QS))
                a_i = jnp.broadcast_to(ai_ref[q], (SUBLANES, QS))

                def step(j, carry, o=o, a_r=a_r, a_i=a_i):
                    r16 = pl.multiple_of((nt // 2 - 1 - j) * 2 * SUBLANES, 2 * SUBLANES)
                    if with_grad:
                        rr, ri, gr, gi = carry
                        h_re = h_ref[pl.ds(r16, 2 * SUBLANES), o:o + QS].astype(f32)
                        h_im = h_ref[pl.ds(r16, 2 * SUBLANES), o + QS:o + 2 * QS].astype(f32)
                    else:
                        rr, ri = carry
                    for half in (1, 0):
                        rows = pl.ds(pl.multiple_of(r16 + half * SUBLANES, SUBLANES), SUBLANES)
                        if with_grad:
                            hr = h_re[half * SUBLANES:(half + 1) * SUBLANES]
                            hi = h_im[half * SUBLANES:(half + 1) * SUBLANES]
                            gr = gr + hr * rr + hi * ri
                            gi = gi + hr * ri - hi * rr
                        nr = buf[rows, o:o + QS] + a_r * rr + a_i * ri
                        ni = buf[rows, o + QS:o + 2 * QS] + a_r * ri - a_i * rr
                        buf[rows, o:o + QS] = nr
                        buf[rows, o + QS:o + 2 * QS] = ni
                        rr, ri = nr, ni
                    return (rr, ri, gr, gi) if with_grad else (rr, ri)

                init = (rc[:, o:o + QS], rc[:, o + QS:o + 2 * QS])
                if with_grad:
                    init = init + (acc[:, o:o + QS], acc[:, o + QS:o + 2 * QS])
                res = lax.fori_loop(0, nt // 2, step, init)
                rc[:, o:o + QS] = res[0]
                rc[:, o + QS:o + 2 * QS] = res[1]
                if with_grad:
                    acc[:, o:o + QS] = res[2]
                    acc[:, o + QS:o + 2 * QS] = res[3]

        @pl.when(ph == 0)
        def _():
            recur(False)

        @pl.when(ph == 1)
        def _():
            recur(True)
            uu = u_ref[...]
            up = _step_major(uu).astype(bf16)
            dd_out[...] += _colsum((dyraw * uu).reshape(tb, sw))
            for q in range(nq):
                o = q * 2 * QS
                cs = slice(q * QW, (q + 1) * QW)
                lam = buf[:, o:o + 2 * QS].astype(bf16)
                duq = _chunk_major(lax.dot_general(lam, bbar_s[q], (((1,), (1,)), ((), ())), preferred_element_type=f32), nt) \
                    + d_ref[:, cs] * dyraw[:, :, cs]
                du_out[:, :, cs] = duq.astype(bf16)
                dbu_out[:, cs] += _colsum(duq.reshape(tb, QW))
                dbbar[q] += lax.dot_general(up[:, cs], lam, (((0,), (0,)), ((), ())), preferred_element_type=f32)
                dcmat[q] += lax.dot_general(dyp[:, cs], h_ref[:, o:o + 2 * QS], (((0,), (0,)), ((), ())),
                                            preferred_element_type=f32)

        @pl.when((ph == 1) & (i == nb - 1))
        def _():
            for q in range(nq):
                o = q * 2 * QS
                cr, ci, br, bi = cfr_ref[q], cfi_ref[q], bre_ref[q], bim_ref[q]
                gr, gi = dbbar[q, :, 0:QS], dbbar[q, :, QS:2 * QS]
                dbre_out[q] = cr * gr + ci * gi
                dbim_out[q] = cr * gi - ci * gr
                dcfr_out[q] = _colsum(gr * br + gi * bi)
                dcfi_out[q] = _colsum(gi * br - gr * bi)
                dcre_out[q] = dcmat[q, :, 0:QS].T
                dcim_out[q] = -dcmat[q, :, QS:2 * QS].T
                dlbr_out[q] = _colsum(acc[:, o:o + QS])
                dlbi_out[q] = _colsum(acc[:, o + QS:o + 2 * QS])

    blk = lambda ph, i: (0, nb - 1 - i, 0)
    oblk = lambda ph, i: (0, (nb - 1 - i) * ph + (nb - 1) * (1 - ph), 0)
    pshapes = [ar.shape, ai.shape, bre.shape, bim.shape, cre.shape, cim.shape, cfr.shape, cfi.shape, dvec.shape]
    oshapes = [bre.shape, bim.shape, cre.shape, cim.shape, cfr.shape, cfi.shape, ar.shape, ai.shape, dvec.shape, dvec.shape]
    act = pl.BlockSpec((SUBLANES, nt, sw), blk)
    view = lambda a: a.reshape(SUBLANES, chunk_len, sw)
    res, got = _call(
        body, (view(dy), yraw3, view(u), h_p, ar, ai, bre, bim, cre, cim, cfr, cfi, dvec), name=name, grid=(2, nb),
        in_specs=[act, act, act, pl.BlockSpec((tb, st), lambda ph, i: (nb - 1 - i, 0))] + [_full(p) for p in pshapes],
        out_specs=[pl.BlockSpec((SUBLANES, nt, sw), oblk)] + [_full(p) for p in oshapes],
        out_shape=[jax.ShapeDtypeStruct((SUBLANES, chunk_len, sw), bf16)] + [jax.ShapeDtypeStruct(p, f32) for p in oshapes],
        scratch_shapes=[pltpu.VMEM((tb, st), f32),
                        pltpu.VMEM((SUBLANES, st), f32), pltpu.VMEM((SUBLANES, st), f32),
                        pltpu.VMEM((nq, QW, 2 * QS), f32), pltpu.VMEM((nq, QW, 2 * QS), f32),
                        pltpu.VMEM((nq, QW, 2 * QS), bf16), pltpu.VMEM((nq, 2 * QS, QW), bf16)],
        sem=("arbitrary", "arbitrary"), xchg=xchg)
    return (res[0].reshape(s, sw),) + tuple(res[1:]) + (got,)


def _lnmod(x, sc, sh, *, name):
    s, d = x.shape
    tb = _tile(s, MM_ROWS)

    def body(x_ref, sc_ref, sh_ref, o_ref):
        xh, _ = _ln(x_ref[...])
        o_ref[...] = (xh * (1.0 + sc_ref[...]) + sh_ref[...]).astype(bf16)

    blk = pl.BlockSpec((tb, d), lambda i: (i, 0))
    vec = pl.BlockSpec((1, d), _row)
    return pl.pallas_call(body, name=name, grid=(s // tb,), in_specs=[blk, vec, vec], out_specs=blk,
                          out_shape=jax.ShapeDtypeStruct((s, d), bf16), compiler_params=_params("parallel"))(x, sc, sh)


ROWS = 32


def _row_chunks(n_rows, rows, fn, init, start=0):
    return lax.fori_loop(start, n_rows // rows, lambda c, carry: fn(pl.multiple_of(c * rows, rows), carry), init)


def _rows_from(win, o, rows):
    if o % SUBLANES == 0:
        return win[o:o + rows]
    n = win.shape[0]
    return pltpu.roll(win, (n - o) % n, 0)[0:rows]


def _window_before(ref, halo, r0, rows, first, cols):
    if first:
        return jnp.concatenate([halo, ref[pl.ds(0, rows), cols]], axis=0)
    return ref[pl.ds(pl.multiple_of(r0 - SUBLANES, SUBLANES), rows + SUBLANES), cols]


def _taps3(win, w, off, rows):
    return _rows_from(win, off, rows) * w[0] + _rows_from(win, off + 1, rows) * w[1] + _rows_from(win, off + 2, rows) * w[2]


def _fold8(x):
    acc = x[0:SUBLANES]
    for r in range(1, x.shape[0] // SUBLANES):
        acc = acc + x[r * SUBLANES:(r + 1) * SUBLANES]
    return acc


def _conv_halo_specs(tb, cw, halo, s):
    per = tb // halo
    prev = pl.BlockSpec((halo, cw), lambda i: (jnp.maximum(i * per - 1, 0), 0))
    nxt = pl.BlockSpec((halo, cw), lambda i: (jnp.minimum((i + 1) * per, s // halo - 1), 0))
    return prev, nxt


WIDE_ROWS = 16


def _shift_groups(lo, hi):
    return [(b, [o for o in range(lo, hi + 1) if o % SUBLANES == b]) for b in range(SUBLANES)]


def _shifted(win, b):
    return win if b == 0 else _rows_from(win, b, win.shape[0] - SUBLANES)


def _conv31(win, w_ref, cols, rows, lo, hi, tap_of):
    acc = None
    for b, offs in _shift_groups(lo, hi):
        if offs:
            wb = _shifted(win, b)
            for o in offs:
                term = wb[o - b:o - b + rows] * w_ref[pl.ds(tap_of(o), 1), cols]
                acc = term if acc is None else acc + term
    return acc


def _gate_into(ext, a_ref, g_ref, ah_ref, gh_ref, tb, i):
    ext[pl.ds(0, CONV_HALO), :] = jnp.where(i > 0, ah_ref[...] * _sig(gh_ref[...]), 0.0)

    def chunk(r0, carry):
        ext[pl.ds(pl.multiple_of(r0 + CONV_HALO, SUBLANES), WIDE_ROWS), :] = \
            a_ref[pl.ds(r0, WIDE_ROWS), :] * _sig(g_ref[pl.ds(r0, WIDE_ROWS), :])
        return carry

    _row_chunks(tb, WIDE_ROWS, chunk, 0)


def _causal_conv_into(v2buf, ext, w_ref, b_ref, tb, cw):
    for ct in range(cw // LANES):
        cols = slice(ct * LANES, (ct + 1) * LANES)

        def chunk(r0, carry, cols=cols):
            win = ext[pl.ds(r0, ROWS + CONV_HALO), cols]
            v2buf[pl.ds(r0, ROWS), cols] = _conv31(win, w_ref, cols, ROWS, 2, CONV_K + 1, lambda o: o - 2) + b_ref[:, cols]
            return carry

        _row_chunks(tb, ROWS, chunk, 0)


def _silu_grad(x):
    sg = _sig(x)
    return sg * (1.0 + x * (1.0 - sg))


def _conv_fwd(cva, cvg, w, b, lng, lnb, *, name, xchg=None):
    s, cw = cva.shape
    tb = _tile(s, 256)
    prev, _ = _conv_halo_specs(tb, cw, CONV_HALO, s)

    def body(a_ref, g_ref, ah_ref, gh_ref, w_ref, b_ref, lng_ref, lnb_ref, o_ref, v2_ref, ext):
        _gate_into(ext, a_ref, g_ref, ah_ref, gh_ref, tb, pl.program_id(0))
        _causal_conv_into(v2_ref, ext, w_ref, b_ref, tb, cw)
        xh, _ = _ln(v2_ref[...])
        v3 = xh * lng_ref[...] + lnb_ref[...]
        o_ref[...] = (v3 * _sig(v3)).astype(bf16)

    blk = pl.BlockSpec((tb, cw), lambda i: (i, 0))
    vec = pl.BlockSpec((1, cw), _row)
    (v4, v2), got = _call(
        body, (cva, cvg, cva, cvg, w, b, lng, lnb), name=name, grid=(s // tb,),
        in_specs=[blk, blk, prev, prev, _full(w.shape), vec, vec, vec], out_specs=[blk, blk],
        out_shape=[jax.ShapeDtypeStruct((s, cw), bf16), jax.ShapeDtypeStruct((s, cw), f32)],
        scratch_shapes=[pltpu.VMEM((tb + CONV_HALO, cw), f32)], sem=("parallel",), xchg=xchg)
    return v4, v2, got


def _conv_bwd_ln(dv4, v2, lng, lnb, *, name):
    s, cw = v2.shape
    tb = _tile(s, 256)

    def body(d_ref, v2_ref, lng_ref, lnb_ref, o_ref, dg_ref, db_ref):
        @pl.when(pl.program_id(0) == 0)
        def _():
            dg_ref[...] = jnp.zeros_like(dg_ref)
            db_ref[...] = jnp.zeros_like(db_ref)

        xh, rstd = _ln(v2_ref[...])
        v3 = xh * lng_ref[...] + lnb_ref[...]
        dv3 = d_ref[...].astype(f32) * _silu_grad(v3)
        dg_ref[...] += _colsum(dv3 * xh)
        db_ref[...] += _colsum(dv3)
        o_ref[...] = _ln_bwd(dv3 * lng_ref[...], xh, rstd)

    blk = pl.BlockSpec((tb, cw), lambda i: (i, 0))
    vec = pl.BlockSpec((1, cw), _row)
    vshape = jax.ShapeDtypeStruct((1, cw), f32)
    return pl.pallas_call(
        body, name=name, grid=(s // tb,), in_specs=[blk, blk, vec, vec],
        out_specs=[blk, vec, vec], out_shape=[jax.ShapeDtypeStruct((s, cw), f32), vshape, vshape],
        compiler_params=_params("arbitrary"))(dv4, v2, lng, lnb)


def _conv_bwd_taps(dv2, cva, cvg, w, *, name, xchg=None):
    s, cw = cva.shape
    tb = _tile(s, 256)
    nb = s // tb
    prev, nxt = _conv_halo_specs(tb, cw, CONV_HALO, s)

    def body(d_ref, dn_ref, a_ref, g_ref, ah_ref, gh_ref, w_ref, da_ref, dg_ref, dw_ref, db_ref, sa_ref, sg_ref,
             ext, dext, dvbuf, tap_sums):
        i = pl.program_id(0)

        @pl.when(i == 0)
        def _():
            for r in (dw_ref, db_ref, sa_ref, sg_ref):
                r[...] = jnp.zeros_like(r)

        _gate_into(ext, a_ref, g_ref, ah_ref, gh_ref, tb, i)
        dext[pl.ds(tb, CONV_HALO), :] = jnp.where(i < nb - 1, dn_ref[...], 0.0)

        def copy(r0, carry):
            dext[pl.ds(r0, WIDE_ROWS), :] = d_ref[pl.ds(r0, WIDE_ROWS), :]
            return carry

        _row_chunks(tb, WIDE_ROWS, copy, 0)

        for ct in range(cw // LANES):
            cols = slice(ct * LANES, (ct + 1) * LANES)

            tap_sums[...] = jnp.zeros_like(tap_sums)

            def back(r0, carry, cols=cols):
                win = dext[pl.ds(r0, ROWS + CONV_HALO), cols]
                dvbuf[pl.ds(r0, ROWS), cols] = _conv31(win, w_ref, cols, ROWS, 0, CONV_K - 1, lambda o: CONV_K - 1 - o)
                win = ext[pl.ds(r0, ROWS + CONV_HALO), cols]
                dd = d_ref[pl.ds(r0, ROWS), cols]
                for b, offs in _shift_groups(2, CONV_K + 1):
                    wb = _shifted(win, b)
                    for o in offs:
                        tap_sums[o - 2] += _fold8(dd * wb[o - b:o - b + ROWS])
                return carry

            _row_chunks(tb, ROWS, back, 0)
            for k in range(CONV_K):
                dw_ref[pl.ds(k, 1), cols] += _colsum(tap_sums[k])

        def gate_back(r0, sums):
            rows = pl.ds(r0, WIDE_ROWS)
            aa, sg, dv = a_ref[rows, :], _sig(g_ref[rows, :]), dvbuf[rows, :]
            da = dv * sg
            dgate = dv * aa * sg * (1.0 - sg)
            da_ref[rows, :] = da.astype(bf16)
            dg_ref[rows, :] = dgate.astype(bf16)
            return sums[0] + _fold8(da), sums[1] + _fold8(dgate), sums[2] + _fold8(d_ref[rows, :])

        zero = jnp.zeros((SUBLANES, cw), f32)
        sums = _row_chunks(tb, WIDE_ROWS, gate_back, (zero, zero, zero))
        sa_ref[...] += _colsum(sums[0])
        sg_ref[...] += _colsum(sums[1])
        db_ref[...] += _colsum(sums[2])

    blk = pl.BlockSpec((tb, cw), lambda i: (i, 0))
    vec = pl.BlockSpec((1, cw), _row)
    vshape = jax.ShapeDtypeStruct((1, cw), f32)
    act = jax.ShapeDtypeStruct((s, cw), bf16)
    res, got = _call(
        body, (dv2, dv2, cva, cvg, cva, cvg, w), name=name, grid=(nb,), in_specs=[blk, nxt, blk, blk, prev, prev, _full(w.shape)],
        out_specs=[blk, blk, _full(w.shape), vec, vec, vec],
        out_shape=[act, act, jax.ShapeDtypeStruct(w.shape, f32), vshape, vshape, vshape],
        scratch_shapes=[pltpu.VMEM((tb + CONV_HALO, cw), f32), pltpu.VMEM((tb + CONV_HALO, cw), f32), pltpu.VMEM((tb, cw), f32),
                        pltpu.VMEM((CONV_HALO, SUBLANES, LANES), f32)],
        sem=("arbitrary",), xchg=xchg)
    return tuple(res) + (got,)


def _glu_merge_out(ya, yb, ycv, gs, gc, w_o, *, name):
    s, d = ya.shape
    tb = _tile(s, 512)

    def body(ya_ref, yb_ref, ycv_ref, gs_ref, gc_ref, w_ref, m_ref, o_ref):
        ld = lambda r: r[...].astype(f32)
        z = ld(ya_ref) * _sig(ld(yb_ref))
        merged = (_sig(ld(gs_ref)) * z + _sig(ld(gc_ref)) * ld(ycv_ref)).astype(bf16)
        m_ref[...] = merged
        o_ref[...] = jnp.dot(merged, w_ref[...], preferred_element_type=f32).astype(bf16)

    blk = pl.BlockSpec((tb, d), lambda i: (i, 0))
    act = jax.ShapeDtypeStruct((s, d), bf16)
    return pl.pallas_call(body, name=name, grid=(s // tb,), in_specs=[blk] * 5 + [_full(w_o.shape)], out_specs=[blk, blk],
                          out_shape=[act, act], compiler_params=_params("parallel"))(ya, yb, ycv, gs, gc, w_o)


def _glu_merge_bwd(dm, ya, yb, ycv, gs, gc, *, name):
    s, d = ya.shape
    tb = _tile(s, 512)

    def body(dm_ref, ya_ref, yb_ref, ycv_ref, gs_ref, gc_ref, dya_ref, dyb_ref, dycv_ref, dgs_ref, dgc_ref, sgs_ref, sgc_ref):
        @pl.when(pl.program_id(0) == 0)
        def _():
            sgs_ref[...] = jnp.zeros_like(sgs_ref)
            sgc_ref[...] = jnp.zeros_like(sgc_ref)

        ld = lambda r: r[...].astype(f32)
        dmv, yav = ld(dm_ref), ld(ya_ref)
        sb, ss, scv = _sig(ld(yb_ref)), _sig(ld(gs_ref)), _sig(ld(gc_ref))
        z = yav * sb
        dz = dmv * ss
        dgs = dmv * z * ss * (1.0 - ss)
        dgc = dmv * ld(ycv_ref) * scv * (1.0 - scv)
        dya_ref[...] = (dz * sb).astype(bf16)
        dyb_ref[...] = (dz * yav * sb * (1.0 - sb)).astype(bf16)
        dycv_ref[...] = (dmv * scv).astype(bf16)
        dgs_ref[...] = dgs.astype(bf16)
        dgc_ref[...] = dgc.astype(bf16)
        sgs_ref[...] += _colsum(dgs)
        sgc_ref[...] += _colsum(dgc)

    blk = pl.BlockSpec((tb, d), lambda i: (i, 0))
    vec = pl.BlockSpec((1, d), _row)
    act = jax.ShapeDtypeStruct((s, d), bf16)
    vshape = jax.ShapeDtypeStruct((1, d), f32)
    return pl.pallas_call(body, name=name, grid=(s // tb,), in_specs=[blk] * 6, out_specs=[blk] * 5 + [vec, vec],
                          out_shape=[act] * 5 + [vshape, vshape], compiler_params=_params("arbitrary"))(dm, ya, yb, ycv, gs, gc)


def _resid_ln_mod(x, o, g, lng, lnb, sc, sh, alpha, *, name):
    s, d = x.shape
    tb = _tile(s, MM_ROWS)

    def body(x_ref, o_ref, g_ref, lng_ref, lnb_ref, sc_ref, sh_ref, x1_ref, h_ref):
        xh, _ = _ln(alpha * x_ref[...] + g_ref[...] * o_ref[...].astype(f32))
        x1 = xh * lng_ref[...] + lnb_ref[...]
        x1_ref[...] = x1
        xh1, _ = _ln(x1)
        h_ref[...] = (xh1 * (1.0 + sc_ref[...]) + sh_ref[...]).astype(bf16)

    blk = pl.BlockSpec((tb, d), lambda i: (i, 0))
    vec = pl.BlockSpec((1, d), _row)
    return pl.pallas_call(body, name=name, grid=(s // tb,), in_specs=[blk, blk] + [vec] * 5, out_specs=[blk, blk],
                          out_shape=[jax.ShapeDtypeStruct((s, d), f32), jax.ShapeDtypeStruct((s, d), bf16)],
                          compiler_params=_params("parallel"))(x, o, g, lng, lnb, sc, sh)


def _resid_ln_loss(x1, f, w_dn, g, lng, lnb, tgt, alpha, *, name):
    s, d = x1.shape
    tb = _tile(s, 512)

    def body(x1_ref, f_ref, w_ref, g_ref, lng_ref, lnb_ref, t_ref, dr_ref, dy_ref, loss_ref, dlg_ref, dlb_ref, dg_ref):
        @pl.when(pl.program_id(0) == 0)
        def _():
            for r in (loss_ref, dlg_ref, dlb_ref, dg_ref):
                r[...] = jnp.zeros_like(r)

        yv = jnp.dot(f_ref[...], w_ref[...], preferred_element_type=f32)
        xh, rstd = _ln(alpha * x1_ref[...] + g_ref[...] * yv)
        err = xh * lng_ref[...] + lnb_ref[...] - t_ref[...]
        loss_ref[...] += 0.5 * jnp.sum(jnp.sum(err * err, axis=-1, keepdims=True) / d, axis=0, keepdims=True)
        dx2 = err / d
        dlg_ref[...] += _colsum(dx2 * xh)
        dlb_ref[...] += _colsum(dx2)
        dr = _ln_bwd(dx2 * lng_ref[...], xh, rstd)
        dg_ref[...] += _colsum(dr * yv)
        dr_ref[...] = dr
        dy_ref[...] = (g_ref[...] * dr).astype(bf16)

    blk = pl.BlockSpec((tb, d), lambda i: (i, 0))
    vec = pl.BlockSpec((1, d), _row)
    vshape = jax.ShapeDtypeStruct((1, d), f32)
    return pl.pallas_call(
        body, name=name, grid=(s // tb,),
        in_specs=[blk, pl.BlockSpec((tb, f.shape[1]), lambda i: (i, 0)), _full(w_dn.shape), vec, vec, vec, blk],
        out_specs=[blk, blk, pl.BlockSpec((1, 1), _row), vec, vec, vec],
        out_shape=[jax.ShapeDtypeStruct((s, d), f32), jax.ShapeDtypeStruct((s, d), bf16),
                   jax.ShapeDtypeStruct((1, 1), f32), vshape, vshape, vshape],
        compiler_params=_params("arbitrary"))(x1, f, w_dn, g, lng, lnb, tgt)


def _mid_bwd(dh2, x1, dr2, x, o, g, sc, lng, alpha, *, name):
    s, d = x.shape
    tb = _tile(s, 512)

    def body(dh_ref, x1_ref, dr2_ref, x_ref, o_ref, g_ref, sc_ref, lng_ref,
             dr1_ref, do_ref, dsc_ref, dsh_ref, dlg_ref, dlb_ref, dg_ref):
        @pl.when(pl.program_id(0) == 0)
        def _():
            for r in (dsc_ref, dsh_ref, dlg_ref, dlb_ref, dg_ref):
                r[...] = jnp.zeros_like(r)

        dh = dh_ref[...].astype(f32)
        xh1, rstd1 = _ln(x1_ref[...])
        dsc_ref[...] += _colsum(dh * xh1)
        dsh_ref[...] += _colsum(dh)
        dx1 = alpha * dr2_ref[...] + _ln_bwd(dh * (1.0 + sc_ref[...]), xh1, rstd1)
        ov = o_ref[...].astype(f32)
        xhr, rstdr = _ln(alpha * x_ref[...] + g_ref[...] * ov)
        dlg_ref[...] += _colsum(dx1 * xhr)
        dlb_ref[...] += _colsum(dx1)
        dr1 = _ln_bwd(dx1 * lng_ref[...], xhr, rstdr)
        dg_ref[...] += _colsum(dr1 * ov)
        dr1_ref[...] = dr1
        do_ref[...] = (g_ref[...] * dr1).astype(bf16)

    blk = pl.BlockSpec((tb, d), lambda i: (i, 0))
    vec = pl.BlockSpec((1, d), _row)
    vshape = jax.ShapeDtypeStruct((1, d), f32)
    return pl.pallas_call(
        body, name=name, grid=(s // tb,), in_specs=[blk] * 5 + [vec] * 3, out_specs=[blk, blk] + [vec] * 5,
        out_shape=[jax.ShapeDtypeStruct((s, d), f32), jax.ShapeDtypeStruct((s, d), bf16)] + [vshape] * 5,
        compiler_params=_params("arbitrary"))(dh2, x1, dr2, x, o, g, sc, lng)


def _final_bwd(dh1, x, dr1, sc, alpha, *, name, xchg=None):
    s, d = x.shape
    tb = _tile(s, MM_ROWS)

    def body(dh_ref, x_ref, dr1_ref, sc_ref, dx_ref, dsc_ref, dsh_ref):
        @pl.when(pl.program_id(0) == 0)
        def _():
            dsc_ref[...] = jnp.zeros_like(dsc_ref)
            dsh_ref[...] = jnp.zeros_like(dsh_ref)

        dh = dh_ref[...].astype(f32)
        xh, rstd = _ln(x_ref[...])
        dsc_ref[...] += _colsum(dh * xh)
        dsh_ref[...] += _colsum(dh)
        dx_ref[...] = alpha * dr1_ref[...] + _ln_bwd(dh * (1.0 + sc_ref[...]), xh, rstd)

    blk = pl.BlockSpec((tb, d), lambda i: (i, 0))
    vec = pl.BlockSpec((1, d), _row)
    vshape = jax.ShapeDtypeStruct((1, d), f32)
    res, got = _call(body, (dh1, x, dr1, sc), name=name, grid=(s // tb,), in_specs=[blk, blk, blk, vec], out_specs=[blk, vec, vec],
                     out_shape=[jax.ShapeDtypeStruct((s, d), f32), vshape, vshape], sem=("arbitrary",), xchg=xchg)
    return tuple(res) + (got,)


TALL_ROWS = 64


def _ffn_col_tile(fh):
    return fh // 2 if (fh // 2) % LANES == 0 else fh


def _ffn_specs(s, fh, tb, tc):
    per = tb // FFN_HALO
    blk = pl.BlockSpec((tb, tc), lambda j, i: (i, j))
    prev = pl.BlockSpec((FFN_HALO, tc), lambda j, i: (jnp.maximum(i * per - 1, 0), j))
    nxt = pl.BlockSpec((FFN_HALO, tc), lambda j, i: (jnp.minimum((i + 1) * per, s // FFN_HALO - 1), j))
    taps = pl.BlockSpec((FFN_HALO, tc), lambda j, i: (0, j))
    vec = pl.BlockSpec((1, tc), lambda j, i: (0, j))
    return blk, prev, nxt, taps, vec


def _ffn_mid(upa, upv, wa, wv, ba, bv, *, name, xchg=None):
    s, fh = upa.shape
    tb, tc = _tile(s, 512), _ffn_col_tile(fh)
    blk, prev, _, taps, vec = _ffn_specs(s, fh, tb, tc)
    off = FFN_HALO - FFN_K + 1

    def body(a_ref, v_ref, ah_ref, vh_ref, wa_ref, wv_ref, ba_ref, bv_ref, o_ref):
        first = pl.program_id(1) == 0
        for lt in range(tc // LANES):
            cols = slice(lt * LANES, (lt + 1) * LANES)
            halo_a, halo_v = jnp.where(first, 0.0, ah_ref[:, cols]), jnp.where(first, 0.0, vh_ref[:, cols])
            wa = [wa_ref[pl.ds(k, 1), cols] for k in range(FFN_K)]
            wv = [wv_ref[pl.ds(k, 1), cols] for k in range(FFN_K)]
            ba, bv = ba_ref[:, cols], bv_ref[:, cols]

            def chunk(r0, carry, head=False, cols=cols, halo_a=halo_a, halo_v=halo_v, wa=wa, wv=wv, ba=ba, bv=bv):
                a2 = _taps3(_window_before(a_ref, halo_a, r0, TALL_ROWS, head, cols), wa, off, TALL_ROWS) + ba
                v2 = _taps3(_window_before(v_ref, halo_v, r0, TALL_ROWS, head, cols), wv, off, TALL_ROWS) + bv
                o_ref[pl.ds(r0, TALL_ROWS), cols] = (_gelu(a2) * v2).astype(bf16)
                return carry

            chunk(0, 0, head=True)
            _row_chunks(tb, TALL_ROWS, chunk, 0, start=1)

    (f,), got = _call(
        body, (upa, upv, upa, upv, wa, wv, ba, bv), name=name, grid=(fh // tc, s // tb),
        in_specs=[blk, blk, prev, prev, taps, taps, vec, vec], out_specs=[blk], out_shape=[jax.ShapeDtypeStruct((s, fh), bf16)],
        sem=("parallel", "arbitrary"), xchg=xchg)
    return f, got


def _ffn_mid_bwd_tile(cols, first, last, tb, off, df_ref, dfn_ref, a_ref, v_ref, ah_ref, vh_ref, an_ref, vn_ref, wa_ref, wv_ref,
                      ba_ref, bv_ref, da_ref, dv_ref, dwa_ref, dwv_ref, dba_ref, dbv_ref, dexta, dextv):
    rows_c = TALL_ROWS
    halo_a, halo_v = jnp.where(first, 0.0, ah_ref[:, cols]), jnp.where(first, 0.0, vh_ref[:, cols])
    wa = [wa_ref[pl.ds(k, 1), cols] for k in range(FFN_K)]
    wv = [wv_ref[pl.ds(k, 1), cols] for k in range(FFN_K)]
    ba, bv = ba_ref[:, cols], bv_ref[:, cols]

    def conv_cotangents(r0, rows, xa, xv, dfe):
        sa = [_rows_from(xa, off + k, rows) for k in range(FFN_K)]
        sv = [_rows_from(xv, off + k, rows) for k in range(FFN_K)]
        a2 = sa[0] * wa[0] + sa[1] * wa[1] + sa[2] * wa[2] + ba
        v2 = sv[0] * wv[0] + sv[1] * wv[1] + sv[2] * wv[2] + bv
        cdf = 0.5 * (1.0 + lax.erf(a2 * INV_SQRT2))
        da2 = dfe * v2 * (cdf + a2 * jnp.exp(-0.5 * a2 * a2) * INV_SQRT_2PI)
        dv2 = dfe * (a2 * cdf)
        dexta[pl.ds(r0, rows), cols] = da2
        dextv[pl.ds(r0, rows), cols] = dv2
        return da2, dv2, sa, sv

    def chunk(r0, sums, head=False):
        da2, dv2, sa, sv = conv_cotangents(r0, rows_c, _window_before(a_ref, halo_a, r0, rows_c, head, cols),
                                           _window_before(v_ref, halo_v, r0, rows_c, head, cols), df_ref[pl.ds(r0, rows_c), cols])
        new = [sums[k] + _fold8(da2 * sa[k]) for k in range(FFN_K)] + [sums[FFN_K] + _fold8(da2)]
        new += [sums[FFN_K + 1 + k] + _fold8(dv2 * sv[k]) for k in range(FFN_K)] + [sums[2 * FFN_K + 1] + _fold8(dv2)]
        return tuple(new)

    sums = chunk(0, tuple(jnp.zeros((SUBLANES, LANES), f32) for _ in range(2 * FFN_K + 2)), head=True)
    sums = _row_chunks(tb, rows_c, chunk, sums, start=1)
    conv_cotangents(tb, FFN_HALO,
                    jnp.concatenate([a_ref[pl.ds(tb - FFN_HALO, FFN_HALO), cols], jnp.where(last, 0.0, an_ref[:, cols])], axis=0),
                    jnp.concatenate([v_ref[pl.ds(tb - FFN_HALO, FFN_HALO), cols], jnp.where(last, 0.0, vn_ref[:, cols])], axis=0),
                    jnp.where(last, 0.0, dfn_ref[:, cols]))
    for k in range(FFN_K):
        dwa_ref[pl.ds(k, 1), cols] += _colsum(sums[k])
        dwv_ref[pl.ds(k, 1), cols] += _colsum(sums[FFN_K + 1 + k])
    dba_ref[:, cols] += _colsum(sums[FFN_K])
    dbv_ref[:, cols] += _colsum(sums[2 * FFN_K + 1])

    def back(r0, carry):
        for dext, w, o_ref in ((dexta, wa, da_ref), (dextv, wv, dv_ref)):
            dd = dext[pl.ds(r0, rows_c + FFN_HALO), cols]
            o_ref[pl.ds(r0, rows_c), cols] = (_rows_from(dd, 2, rows_c) * w[0] + _rows_from(dd, 1, rows_c) * w[1]
                                              + dd[0:rows_c] * w[2]).astype(bf16)
        return carry

    _row_chunks(tb, rows_c, back, 0)


def _ffn_mid_bwd(df, upa, upv, wa, wv, ba, bv, *, name, xchg=None):
    s, fh = upa.shape
    tb, tc = _tile(s, 512), _ffn_col_tile(fh)
    nb = s // tb
    blk, prev, nxt, taps, vec = _ffn_specs(s, fh, tb, tc)
    off = FFN_HALO - FFN_K + 1
    te = tb + FFN_HALO

    def body(df_ref, dfn_ref, a_ref, v_ref, ah_ref, vh_ref, an_ref, vn_ref, wa_ref, wv_ref, ba_ref, bv_ref,
             da_ref, dv_ref, dwa_ref, dwv_ref, dba_ref, dbv_ref, dexta, dextv):
        i = pl.program_id(1)

        @pl.when(i == 0)
        def _():
            for r in (dwa_ref, dwv_ref, dba_ref, dbv_ref):
                r[...] = jnp.zeros_like(r)

        last = i == nb - 1
        for lt in range(tc // LANES):
            _ffn_mid_bwd_tile(slice(lt * LANES, (lt + 1) * LANES), i == 0, last, tb, off, df_ref, dfn_ref, a_ref, v_ref,
                              ah_ref, vh_ref, an_ref, vn_ref, wa_ref, wv_ref, ba_ref, bv_ref, da_ref, dv_ref, dwa_ref, dwv_ref,
                              dba_ref, dbv_ref, dexta, dextv)

    act = jax.ShapeDtypeStruct((s, fh), bf16)
    wshape = jax.ShapeDtypeStruct((FFN_HALO, fh), f32)
    vshape = jax.ShapeDtypeStruct((1, fh), f32)
    res, got = _call(
        body, (df, df, upa, upv, upa, upv, upa, upv, wa, wv, ba, bv), name=name, grid=(fh // tc, nb),
        in_specs=[blk, nxt, blk, blk, prev, prev, nxt, nxt, taps, taps, vec, vec],
        out_specs=[blk, blk, taps, taps, vec, vec], out_shape=[act, act, wshape, wshape, vshape, vshape],
        scratch_shapes=[pltpu.VMEM((te, tc), f32)] * 2, sem=("parallel", "arbitrary"), xchg=xchg)
    return tuple(res) + (got,)


def _cols_from_shards(stacked):
    _, k, n = stacked.shape
    return stacked.transpose(1, 0, 2).reshape(k, NDEV * n)


def _pad_rows(w, rows):
    return jnp.pad(w, ((0, rows - w.shape[0]), (0, 0)))


def kernel(x, c, w_cond, b_cond, w_in, b_in, ssm_lambda_re, ssm_lambda_im, ssm_log_dt, ssm_b_re, ssm_b_im, ssm_c_re, ssm_c_im, ssm_d, ssm_glu_w_a, ssm_glu_w_b, cv_dw_w, cv_dw_b, cv_ln_g, cv_ln_b, cv_w_pw, w_out, ln1_g, ln1_b, ffn_w_up, ffn_dw_w, ffn_dw_b, ffn_w_down, ln2_g, ln2_b, loss_target, m_w_cond, m_b_cond, m_w_in, m_b_in, m_ssm_lambda_re, m_ssm_lambda_im, m_ssm_log_dt, m_ssm_b_re, m_ssm_b_im, m_ssm_c_re, m_ssm_c_im, m_ssm_d, m_ssm_glu_w_a, m_ssm_glu_w_b, m_cv_dw_w, m_cv_dw_b, m_cv_ln_g, m_cv_ln_b, m_cv_w_pw, m_w_out, m_ln1_g, m_ln1_b, m_ffn_w_up, m_ffn_dw_w, m_ffn_dw_b, m_ffn_w_down, m_ln2_g, m_ln2_b, v_w_cond, v_b_cond, v_w_in, v_b_in, v_ssm_lambda_re, v_ssm_lambda_im, v_ssm_log_dt, v_ssm_b_re, v_ssm_b_im, v_ssm_c_re, v_ssm_c_im, v_ssm_d, v_ssm_glu_w_a, v_ssm_glu_w_b, v_cv_dw_w, v_cv_dw_b, v_cv_ln_g, v_cv_ln_b, v_cv_w_pw, v_w_out, v_ln1_g, v_ln1_b, v_ffn_w_up, v_ffn_dw_w, v_ffn_dw_b, v_ffn_w_down, v_ln2_g, v_ln2_b):
    weights = dict(w_cond=w_cond, b_cond=b_cond, w_in=w_in, b_in=b_in, ssm_lambda_re=ssm_lambda_re, ssm_lambda_im=ssm_lambda_im, ssm_log_dt=ssm_log_dt, ssm_b_re=ssm_b_re, ssm_b_im=ssm_b_im, ssm_c_re=ssm_c_re, ssm_c_im=ssm_c_im, ssm_d=ssm_d, ssm_glu_w_a=ssm_glu_w_a, ssm_glu_w_b=ssm_glu_w_b, cv_dw_w=cv_dw_w, cv_dw_b=cv_dw_b, cv_ln_g=cv_ln_g, cv_ln_b=cv_ln_b, cv_w_pw=cv_w_pw, w_out=w_out, ln1_g=ln1_g, ln1_b=ln1_b, ffn_w_up=ffn_w_up, ffn_dw_w=ffn_dw_w, ffn_dw_b=ffn_dw_b, ffn_w_down=ffn_w_down, ln2_g=ln2_g, ln2_b=ln2_b)
    mom_m = dict(w_cond=m_w_cond, b_cond=m_b_cond, w_in=m_w_in, b_in=m_b_in, ssm_lambda_re=m_ssm_lambda_re, ssm_lambda_im=m_ssm_lambda_im, ssm_log_dt=m_ssm_log_dt, ssm_b_re=m_ssm_b_re, ssm_b_im=m_ssm_b_im, ssm_c_re=m_ssm_c_re, ssm_c_im=m_ssm_c_im, ssm_d=m_ssm_d, ssm_glu_w_a=m_ssm_glu_w_a, ssm_glu_w_b=m_ssm_glu_w_b, cv_dw_w=m_cv_dw_w, cv_dw_b=m_cv_dw_b, cv_ln_g=m_cv_ln_g, cv_ln_b=m_cv_ln_b, cv_w_pw=m_cv_w_pw, w_out=m_w_out, ln1_g=m_ln1_g, ln1_b=m_ln1_b, ffn_w_up=m_ffn_w_up, ffn_dw_w=m_ffn_dw_w, ffn_dw_b=m_ffn_dw_b, ffn_w_down=m_ffn_w_down, ln2_g=m_ln2_g, ln2_b=m_ln2_b)
    mom_v = dict(w_cond=v_w_cond, b_cond=v_b_cond, w_in=v_w_in, b_in=v_b_in, ssm_lambda_re=v_ssm_lambda_re, ssm_lambda_im=v_ssm_lambda_im, ssm_log_dt=v_ssm_log_dt, ssm_b_re=v_ssm_b_re, ssm_b_im=v_ssm_b_im, ssm_c_re=v_ssm_c_re, ssm_c_im=v_ssm_c_im, ssm_d=v_ssm_d, ssm_glu_w_a=v_ssm_glu_w_a, ssm_glu_w_b=v_ssm_glu_w_b, cv_dw_w=v_cv_dw_w, cv_dw_b=v_cv_dw_b, cv_ln_g=v_cv_ln_g, cv_ln_b=v_cv_ln_b, cv_w_pw=v_cv_w_pw, w_out=v_w_out, ln1_g=v_ln1_g, ln1_b=v_ln1_b, ffn_w_up=v_ffn_w_up, ffn_dw_w=v_ffn_dw_w, ffn_dw_b=v_ffn_dw_b, ffn_w_down=v_ffn_w_down, ln2_g=v_ln2_g, ln2_b=v_ln2_b)
    names = list(weights)

    s, d = x.shape[1], x.shape[2]
    sw = cw = d // 2
    fh = ffn_w_down.shape[1] * NDEV
    ng, nq = sw // SSM_GROUP, sw // QW
    gq = ng // nq
    alpha = 2.0 ** 0.25
    me = 4 * lax.axis_index("x") + 2 * lax.axis_index("y") + lax.axis_index("c")
    xs, tgt = x[0], loss_target[0]

    col_names = ["w_in", "ssm_glu_w_a", "ssm_glu_w_b", "cv_w_pw", "ffn_w_up"]
    row_names = ["w_out", "ffn_w_down"]
    big = col_names + row_names
    sent = lambda ns: [weights[n][0].astype(bf16) for n in ns]
    got_in, got_c, got_cv_taps, got_ffn_taps = _exchange(sent(["w_in"]) + [c, cv_dw_w[0, :, 0], ffn_dw_w[0, :, 0]],
                                                         scatter=False, name="gather_in")
    o1, o2, o3, o4 = sw, sw + cw, sw + 2 * cw, sw + 2 * cw + d
    in_bounds = ((0, o1), (o1, o2), (o2, o3), (o3, o4), (o4, o4 + d))
    w_u, w_cva, w_cvg, w_gs, w_gc = _unshard_cols(got_in, in_bounds, name="unshard_w_in")
    b_u, b_cva, b_cvg, b_gs, b_gc = (b_in[:, a:b] for a, b in in_bounds)
    c_all = got_c.reshape(NDEV, d)
    cv_taps = _cols_from_shards(got_cv_taps)
    ffn_taps = _cols_from_shards(got_ffn_taps)
    cv_w32 = _pad_rows(cv_taps, CONV_HALO)
    ffn_wa, ffn_wv = _pad_rows(ffn_taps[:, :fh], FFN_HALO), _pad_rows(ffn_taps[:, fh:], FFN_HALO)
    ffn_ba, ffn_bv = ffn_dw_b[:, :fh], ffn_dw_b[:, fh:]

    ncond = w_cond.shape[2]
    b_cond_mine = lax.dynamic_slice(b_cond, (0, me * ncond), (1, ncond))
    mod_cols = _cond_fwd(c_all, w_cond[0], b_cond_mine, name="cond_fwd")
    mod_all, = _exchange([mod_cols], scatter=False, name="gather_mod")
    mod_mine = lax.dynamic_slice(mod_all, (0, me, 0), (NDEV, 1, ncond)).reshape(1, 6 * d)
    sh1, sc1, g1, sh2, sc2, g2 = (mod_mine[:, k * d:(k + 1) * d] for k in range(6))

    lam_re, lam_im, log_dt = ssm_lambda_re[0], ssm_lambda_im[0], ssm_log_dt[0][:, None]
    lbr, lbi, cfr, cfi = _ssm_prep(lam_re, lam_im, log_dt, name="ssm_prep")
    rows_q = lambda a: a.reshape(nq, 1, QS)
    eye = jnp.eye(gq, dtype=f32)

    def b_mat(b):
        bt = b.reshape(nq, gq, SSM_STATE, SSM_GROUP).transpose(0, 1, 3, 2)
        return jnp.einsum("qgpn,gh->qgphn", bt, eye).reshape(nq, QW, QS)

    def c_mat(cc):
        ct = cc.reshape(nq, gq, SSM_GROUP, SSM_STATE)
        return jnp.einsum("qgpn,gh->qhngp", ct, eye).reshape(nq, QS, QW)

    def b_unmat(mt):
        return jnp.einsum("qgpgn->qgnp", mt.reshape(nq, gq, SSM_GROUP, gq, SSM_STATE)).reshape(ng, SSM_STATE, SSM_GROUP)

    def c_unmat(mt):
        return jnp.einsum("qgngp->qgpn", mt.reshape(nq, gq, SSM_STATE, gq, SSM_GROUP)).reshape(ng, SSM_GROUP, SSM_STATE)

    ssm_args = (rows_q(lbr), rows_q(lbi), b_mat(ssm_b_re[0]), b_mat(ssm_b_im[0]), c_mat(ssm_c_re[0]), c_mat(ssm_c_im[0]),
                rows_q(cfr), rows_q(cfi), ssm_d[0].reshape(1, sw))

    h1 = _lnmod(xs, sc1, sh1, name="ln_mod1")
    u, cva, cvg, gs, gc = _mm_fanout(h1, [w_u, w_cva, w_cvg, w_gs, w_gc], [b_u, b_cva, b_cvg, b_gs, b_gc],
                                     [f32, f32, f32, bf16, bf16], name="in_proj")
    v4, cv2, (got_a, got_b, got_pw, got_o) = _conv_fwd(
        cva, cvg, cv_w32, cv_dw_b, cv_ln_g, cv_ln_b, name="conv_fwd",
        xchg=(sent(["ssm_glu_w_a", "ssm_glu_w_b", "cv_w_pw", "w_out"]), False))
    h_p, yraw3, y, (got_up,) = _ssm_fwd(u, *ssm_args, name="ssm_fwd", xchg=(sent(["ffn_w_up"]), False))
    w_a, = _unshard_cols(got_a, ((0, d),), name="unshard_glu_a")
    w_b, = _unshard_cols(got_b, ((0, d),), name="unshard_glu_b")
    w_pw, = _unshard_cols(got_pw, ((0, d),), name="unshard_conv_pw")
    w_upa, w_upv = _unshard_cols(got_up, ((0, fh), (fh, 2 * fh)), name="unshard_ffn_up")
    w_o = got_o.reshape(d, d)
    ya, yb = _mm_fanout(y, [w_a, w_b], None, [bf16, bf16], name="glu")
    ycv = _mm([(v4, w_pw)], out_dtype=bf16, name="conv_pw")
    merged, o = _glu_merge_out(ya, yb, ycv, gs, gc, w_o, name="merge_out_proj")
    x1, h2 = _resid_ln_mod(xs, o, g1, ln1_g, ln1_b, sc2, sh2, alpha, name="resid_ln1")
    upa = _mm([(h2, w_upa)], name="ffn_up_a")
    upv = _mm([(h2, w_upv)], name="ffn_up_v")
    f, (got_dn,) = _ffn_mid(upa, upv, ffn_wa, ffn_wv, ffn_ba, ffn_bv, name="ffn_mid", xchg=(sent(["ffn_w_down"]), False))
    w_dn = got_dn.reshape(fh, d)
    dr2, dy2, loss_part, d_ln2_g, d_ln2_b, d_g2 = _resid_ln_loss(x1, f, w_dn, g2, ln2_g, ln2_b, tgt, alpha, name="ffn_down_ln2_loss")

    gw = {}
    df = _mm([(dy2, w_dn)], trans_w=True, name="d_ffn_down")
    gw["ffn_w_down"] = _mm_tn(f, dy2, out_dtype=bf16, name="g_ffn_down").reshape((NDEV,) + ffn_w_down[0].shape)
    received = {}
    dupa, dupv, d_ffn_wa, d_ffn_wv, d_ffn_ba, d_ffn_bv, (received["ffn_w_down"],) = _ffn_mid_bwd(
        df, upa, upv, ffn_wa, ffn_wv, ffn_ba, ffn_bv, name="ffn_mid_bwd", xchg=([gw["ffn_w_down"]], True))
    dh2 = _mm([(dupa, w_upa), (dupv, w_upv)], trans_w=True, out_dtype=bf16, name="d_ffn_up")
    gw["ffn_w_up"] = _shard_cols([_mm_tn(h2, dupa, name="g_ffn_up_a"), _mm_tn(h2, dupv, name="g_ffn_up_v")], out_dtype=bf16,
                                 name="shard_ffn_up")
    dr1, do, d_sc2, d_sh2, d_ln1_g, d_ln1_b, d_g1 = _mid_bwd(dh2, x1, dr2, xs, o, g1, sc2, ln1_g, alpha, name="mid_bwd")
    dmerged = _mm([(do, w_o)], trans_w=True, out_dtype=bf16, name="d_out_proj")
    gw["w_out"] = _mm_tn(merged, do, out_dtype=bf16, name="g_out_proj").reshape((NDEV,) + w_out[0].shape)
    dya, dyb, dycv, dgs, dgc, s_gs, s_gc = _glu_merge_bwd(dmerged, ya, yb, ycv, gs, gc, name="merge_bwd")
    dy = _mm([(dya, w_a), (dyb, w_b)], trans_w=True, out_dtype=bf16, name="d_glu")
    gw["ssm_glu_w_a"] = _mm_tn_sharded(y, [dya], out_dtype=bf16, name="g_glu_a")
    gw["ssm_glu_w_b"] = _mm_tn_sharded(y, [dyb], out_dtype=bf16, name="g_glu_b")
    dv4 = _mm([(dycv, w_pw)], trans_w=True, out_dtype=bf16, name="d_conv_pw")
    gw["cv_w_pw"] = _mm_tn_sharded(v4, [dycv], out_dtype=bf16, name="g_conv_pw")
    dv2, d_cv_ln_g, d_cv_ln_b = _conv_bwd_ln(dv4, cv2, cv_ln_g, cv_ln_b, name="conv_bwd_ln")
    dcva, dcvg, d_cv_w32, d_cv_b, s_cva, s_cvg, (received["ffn_w_up"],) = _conv_bwd_taps(
        dv2, cva, cvg, cv_w32, name="conv_bwd_taps", xchg=([gw["ffn_w_up"]], True))
    late = ["w_out", "ssm_glu_w_a", "ssm_glu_w_b", "cv_w_pw"]
    (du, d_bre_m, d_bim_m, d_cre_m, d_cim_m, d_cfr, d_cfi, d_lbr, d_lbi, d_d, s_u, got_late) = _ssm_bwd(
        dy, yraw3, u, h_p, *ssm_args, name="ssm_bwd", xchg=([gw[n] for n in late], True))
    received.update(zip(late, got_late))
    gshape = lam_re.shape
    d_lam_re, d_lam_im, d_log_dt = _ssm_prep_bwd(
        lam_re, lam_im, log_dt, [a.reshape(gshape) for a in (d_lbr, d_lbi, d_cfr, d_cfi)], name="ssm_prep_bwd")
    small = {
        "b_in": jnp.concatenate([s_u, s_cva, s_cvg, s_gs, s_gc], axis=1),
        "ssm_lambda_re": d_lam_re, "ssm_lambda_im": d_lam_im, "ssm_log_dt": d_log_dt,
        "ssm_b_re": b_unmat(d_bre_m), "ssm_b_im": b_unmat(d_bim_m), "ssm_c_re": c_unmat(d_cre_m), "ssm_c_im": c_unmat(d_cim_m),
        "ssm_d": d_d, "cv_dw_w": d_cv_w32[:CONV_K], "cv_dw_b": d_cv_b, "cv_ln_g": d_cv_ln_g, "cv_ln_b": d_cv_ln_b,
        "ln1_g": d_ln1_g, "ln1_b": d_ln1_b,
        "ffn_dw_w": jnp.concatenate([d_ffn_wa[:FFN_K], d_ffn_wv[:FFN_K]], axis=1),
        "ffn_dw_b": jnp.concatenate([d_ffn_ba, d_ffn_bv], axis=1), "ln2_g": d_ln2_g, "ln2_b": d_ln2_b,
        "mod_g1": d_g1, "mod_sh2": d_sh2, "mod_sc2": d_sc2, "mod_g2": d_g2, "loss": loss_part,
    }
    small_names = list(small)
    small_shapes = [small[n].shape for n in small_names]
    gw["w_in"], (small_all,) = _mm_tn_sharded(h1, [du, dcva, dcvg, dgs, dgc], out_dtype=bf16, name="g_in",
                                              xchg=([_pack([small[n] for n in small_names])], False))
    dh1, (received["w_in"],) = _mm(
        [(du, w_u), (dcva, w_cva), (dcvg, w_cvg), (dgs, w_gs), (dgc, w_gc)], trans_w=True, out_dtype=bf16, name="d_in",
        xchg=([gw["w_in"]], True))
    grad_x, d_sc1, d_sh1, _ = _final_bwd(dh1, xs, dr1, sc1, alpha, name="final_bwd")

    grads, delta, new_m, new_v = {}, {}, {}, {}
    for n in big:
        ride = ([_pack([d_sh1, d_sc1])], False) if n == "w_out" else None
        res = _sum_adamw(received[n], weights[n][0], mom_m[n][0], mom_v[n][0], name="adamw_" + n, xchg=ride)
        grads[n], delta[n], new_m[n], new_v[n] = res[:4]
        if ride is not None:
            last_all, = res[4]

    small_sum = dict(zip(small_names, _unpack(_sum_parts(small_all, name="sum_small").reshape(-1), small_shapes)))
    last_sum = _unpack(_sum_parts(last_all, name="sum_last").reshape(-1), [(1, d), (1, d)])
    per_dev = dict(zip(small_names, _unpack(small_all.reshape(NDEV, -1), small_shapes)))
    last_dev = _unpack(last_all.reshape(NDEV, -1), [(1, d), (1, d)])
    dmod_all = jnp.concatenate(last_dev + [per_dev[k] for k in ("mod_g1", "mod_sh2", "mod_sc2", "mod_g2")], axis=-1).reshape(NDEV, 6 * d)
    dmod_cols = lax.dynamic_slice(dmod_all.reshape(NDEV, NDEV, ncond), (0, me, 0), (NDEV, 1, ncond)).reshape(NDEV, ncond)
    grads["w_cond"] = _cond_bwd(c_all, dmod_cols, name="cond_bwd")
    loss = small_sum.pop("loss").reshape(())
    grads["b_cond"] = jnp.concatenate(last_sum + [small_sum.pop(k) for k in ("mod_g1", "mod_sh2", "mod_sc2", "mod_g2")], axis=1)
    for n, g in small_sum.items():
        grads[n] = g
    ntap = cv_dw_w.shape[3]
    grads["cv_dw_w"] = lax.dynamic_slice(grads["cv_dw_w"], (0, me * ntap), (CONV_K, ntap))
    nffn = ffn_dw_w.shape[3]
    grads["ffn_dw_w"] = lax.dynamic_slice(grads["ffn_dw_w"], (0, me * nffn), (FFN_K, nffn))
    grads = {n: grads[n].reshape(weights[n].shape) for n in names}

    delta["w_cond"], new_m["w_cond"], new_v["w_cond"] = _adamw(w_cond[0], grads["w_cond"][0], m_w_cond[0], v_w_cond[0],
                                                               name="adamw_w_cond")
    rest = [n for n in names if n not in ["w_cond"] + big]
    squeeze = lambda a: a if a.ndim == 2 else a[0]
    results = _adamw_many(*[[squeeze(t[n].reshape(weights[n].shape)) for n in rest] for t in (weights, grads, mom_m, mom_v)],
                          name="adamw_small")
    for n, (dl, nm, nv) in zip(rest, results):
        delta[n], new_m[n], new_v[n] = dl, nm, nv
    shaped = lambda t: [t[n].reshape(weights[n].shape) for n in names]

    return (loss, grad_x[None], *shaped(grads), *shaped(delta), *shaped(new_m), *shaped(new_v))
```

```python
import functools
import math

import jax
import jax.numpy as jnp
from jax import lax
from jax.experimental import pallas as pl
from jax.experimental.pallas import tpu as pltpu

f32 = jnp.float32
bf16 = jnp.bfloat16

NDEV = 8
LANES = 128
SUBLANES = 8
SSM_GROUP = 16
SSM_STATE = 64
QW = 128
QS = 512
CONV_K = 31
CONV_HALO = 32
FFN_K = 3
FFN_HALO = 8
LN_EPS = 1e-5
ADAM_LR, ADAM_B1, ADAM_B2, ADAM_EPS, ADAM_WD, ADAM_STEP = 0.001, 0.9, 0.999, 1e-08, 0.01, 10
VMEM_LIMIT = 56 * 1024 * 1024
W_TILE_BYTES = 6 * 1024 * 1024
MM_ROWS = 1024
EW_BLOCK_BYTES = 2 * 1024 * 1024
INV_SQRT2 = 1.0 / math.sqrt(2.0)
INV_SQRT_2PI = 1.0 / math.sqrt(2.0 * math.pi)
MESH = pl.DeviceIdType.MESH


def _tile(n, want):
    t = min(n, want)
    while n % t:
        t //= 2
    return t


def _col_tile(n, rows, bytes_per):
    best = LANES if n % LANES == 0 else n
    for t in range(LANES, n + 1, LANES):
        if n % t == 0 and rows * t * bytes_per <= W_TILE_BYTES:
            best = t
    return best


def _params(*sem):
    return pltpu.CompilerParams(dimension_semantics=sem, vmem_limit_bytes=VMEM_LIMIT)


def _row(i):
    return (0, 0)


def _full(shape):
    nd = len(shape)
    return pl.BlockSpec(shape, lambda *a: (0,) * nd)


def _ln(x):
    mu = jnp.mean(x, axis=-1, keepdims=True)
    xc = x - mu
    var = jnp.mean(xc * xc, axis=-1, keepdims=True)
    rstd = lax.rsqrt(var + LN_EPS)
    return xc * rstd, rstd


def _ln_bwd(dxhat, xhat, rstd):
    return rstd * (dxhat - jnp.mean(dxhat, axis=-1, keepdims=True) - xhat * jnp.mean(dxhat * xhat, axis=-1, keepdims=True))


def _sig(x):
    return 1.0 / (1.0 + jnp.exp(-x))


def _gelu(x):
    return 0.5 * x * (1.0 + lax.erf(x * INV_SQRT2))


def _gelu_grad(x):
    return 0.5 * (1.0 + lax.erf(x * INV_SQRT2)) + x * jnp.exp(-0.5 * x * x) * INV_SQRT_2PI


def _colsum(x):
    return jnp.sum(x, axis=0, keepdims=True)


def _mm(pairs, bias=None, *, trans_w=False, out_dtype=f32, name, xchg=None):
    n_p = len(pairs)
    m = pairs[0][0].shape[0]
    n = pairs[0][1].shape[0 if trans_w else 1]
    ktot = sum(x.shape[1] for x, _ in pairs)
    tm = _tile(m, MM_ROWS)
    tn = _col_tile(n, ktot, 2)
    dn = (((1,), (1,)), ((), ())) if trans_w else (((1,), (0,)), ((), ()))

    def body(*refs):
        o_ref = refs[-1]
        acc = None
        for xr, wr in zip(refs[:n_p], refs[n_p:2 * n_p]):
            r = lax.dot_general(xr[...].astype(bf16), wr[...].astype(bf16), dn, preferred_element_type=f32)
            acc = r if acc is None else acc + r
        if bias is not None:
            acc = acc + refs[2 * n_p][...]
        o_ref[...] = acc.astype(out_dtype)

    in_specs = [pl.BlockSpec((tm, x.shape[1]), lambda j, i: (i, 0)) for x, _ in pairs]
    if trans_w:
        in_specs += [pl.BlockSpec((tn, w.shape[1]), lambda j, i: (j, 0)) for _, w in pairs]
    else:
        in_specs += [pl.BlockSpec((w.shape[0], tn), lambda j, i: (0, j)) for _, w in pairs]
    args = [x for x, _ in pairs] + [w for _, w in pairs]
    if bias is not None:
        in_specs.append(pl.BlockSpec((1, tn), lambda j, i: (0, j)))
        args.append(bias)
    (out,), got = _call(
        body, args, name=name, grid=(n // tn, m // tm), in_specs=in_specs,
        out_specs=[pl.BlockSpec((tm, tn), lambda j, i: (i, j))], out_shape=[jax.ShapeDtypeStruct((m, n), out_dtype)],
        sem=("parallel", "arbitrary"), xchg=xchg)
    return out if xchg is None else (out, got)


def _mm_fanout(x, ws, biases, out_dtypes, *, name):
    m, k = x.shape
    tm = _tile(m, MM_ROWS)
    n_w = len(ws)
    biases = list(biases or [])

    def body(x_ref, *refs):
        xb = x_ref[...].astype(bf16)
        o_refs = refs[n_w + len(biases):]
        for p, (w_ref, o_ref, dt) in enumerate(zip(refs[:n_w], o_refs, out_dtypes)):
            acc = jnp.dot(xb, w_ref[...], preferred_element_type=f32)
            if biases:
                acc = acc + refs[n_w + p][...]
            o_ref[...] = acc.astype(dt)

    return pl.pallas_call(
        body, name=name, grid=(m // tm,),
        in_specs=[pl.BlockSpec((tm, k), lambda i: (i, 0))] + [_full(w.shape) for w in ws] + [_full(b.shape) for b in biases],
        out_specs=[pl.BlockSpec((tm, w.shape[1]), lambda i: (i, 0)) for w in ws],
        out_shape=[jax.ShapeDtypeStruct((m, w.shape[1]), dt) for w, dt in zip(ws, out_dtypes)],
        compiler_params=_params("parallel"))(x, *ws, *biases)


def _ln_mod_fanout(x, sc, sh, ws, biases, out_dtypes, *, name):
    m, k = x.shape
    tm = _tile(m, 512)
    n_w = len(ws)

    def body(x_ref, sc_ref, sh_ref, *refs):
        xh, _ = _ln(x_ref[...])
        hb = (xh * (1.0 + sc_ref[...]) + sh_ref[...]).astype(bf16)
        h_ref, o_refs = refs[2 * n_w], refs[2 * n_w + 1:]
        h_ref[...] = hb
        for w_ref, b_ref, o_ref, dt in zip(refs[:n_w], refs[n_w:2 * n_w], o_refs, out_dtypes):
            o_ref[...] = (jnp.dot(hb, w_ref[...], preferred_element_type=f32) + b_ref[...]).astype(dt)

    row = pl.BlockSpec((tm, k), lambda i: (i, 0))
    vec = pl.BlockSpec((1, k), _row)
    res = pl.pallas_call(
        body, name=name, grid=(m // tm,),
        in_specs=[row, vec, vec] + [_full(w.shape) for w in ws] + [_full(b.shape) for b in biases],
        out_specs=[row] + [pl.BlockSpec((tm, w.shape[1]), lambda i: (i, 0)) for w in ws],
        out_shape=[jax.ShapeDtypeStruct((m, k), bf16)] + [jax.ShapeDtypeStruct((m, w.shape[1]), dt) for w, dt in zip(ws, out_dtypes)],
        compiler_params=_params("parallel"))(x, sc, sh, *ws, *biases)
    return res[0], res[1:]


def _mm_tn(x, dy, *, out_dtype=f32, name):
    m, k = x.shape
    n = dy.shape[1]
    tm = _tile(m, MM_ROWS)
    tn = _col_tile(n, k, 4)
    steps = m // tm

    def body(x_ref, dy_ref, o_ref, *scratch):
        acc = scratch[0] if scratch else o_ref

        @pl.when(pl.program_id(1) == 0)
        def _():
            acc[...] = jnp.zeros_like(acc)

        acc[...] += lax.dot_general(x_ref[...].astype(bf16), dy_ref[...].astype(bf16), (((0,), (0,)), ((), ())),
                                    preferred_element_type=f32)
        if scratch:
            @pl.when(pl.program_id(1) == steps - 1)
            def _():
                o_ref[...] = acc[...].astype(out_dtype)

    return pl.pallas_call(
        body, name=name, grid=(n // tn, steps),
        in_specs=[pl.BlockSpec((tm, k), lambda j, i: (i, 0)), pl.BlockSpec((tm, tn), lambda j, i: (i, j))],
        out_specs=pl.BlockSpec((k, tn), lambda j, i: (0, j)),
        out_shape=jax.ShapeDtypeStruct((k, n), out_dtype),
        scratch_shapes=[] if out_dtype == f32 else [pltpu.VMEM((k, tn), f32)],
        compiler_params=_params("parallel", "arbitrary"),
    )(x, dy)


def _mm_tn_sharded(x, dys, *, out_dtype, name, xchg=None):
    m, k = x.shape
    widths = [dy.shape[1] for dy in dys]
    n = sum(widths) // NDEV
    tm = _tile(m, 512)
    steps = m // tm
    n_d = len(dys)

    def body(x_ref, *refs):
        dy_refs, o_ref, acc = refs[:n_d], refs[n_d], refs[n_d + 1]
        i = pl.program_id(0)

        @pl.when(i == 0)
        def _():
            acc[...] = jnp.zeros_like(acc)

        xb = x_ref[...].astype(bf16)
        off = 0
        for dy_ref, w in zip(dy_refs, widths):
            acc[:, off:off + w] += lax.dot_general(xb, dy_ref[...].astype(bf16), (((0,), (0,)), ((), ())), preferred_element_type=f32)
            off += w

        @pl.when(i == steps - 1)
        def _():
            for j in range(NDEV):
                o_ref[j] = acc[:, n * j:n * (j + 1)].astype(out_dtype)

    (out,), got = _call(
        body, (x, *dys), name=name, grid=(steps,),
        in_specs=[pl.BlockSpec((tm, k), lambda i: (i, 0))] + [pl.BlockSpec((tm, w), lambda i: (i, 0)) for w in widths],
        out_specs=[pl.BlockSpec((NDEV, k, n), lambda i: (0, 0, 0))], out_shape=[jax.ShapeDtypeStruct((NDEV, k, n), out_dtype)],
        scratch_shapes=[pltpu.VMEM((k, sum(widths)), f32)], sem=("arbitrary",), xchg=xchg)
    return out if xchg is None else (out, got)


def _exchange(arrs, *, scatter, name):
    n = len(arrs)

    def body(*refs):
        _exchange_copies(refs[:n], refs[n:2 * n], refs[2 * n:], scatter, True, True)

    return pl.pallas_call(
        body, name=name, in_specs=[HBM_SPEC] * n, out_specs=[HBM_SPEC] * n, out_shape=_exchange_out_shape(arrs, scatter),
        scratch_shapes=_exchange_sems(n),
    )(*arrs)


HBM_SPEC = pl.BlockSpec(memory_space=pltpu.HBM)


def _flags(scatter, n):
    return list(scatter) if isinstance(scatter, (list, tuple)) else [scatter] * n


def _exchange_out_shape(arrs, scatter):
    return [jax.ShapeDtypeStruct(a.shape if sc else (NDEV,) + a.shape, a.dtype) for a, sc in zip(arrs, _flags(scatter, len(arrs)))]


def _exchange_sems(n):
    return [pltpu.SemaphoreType.DMA(((NDEV - 1) * n,)), pltpu.SemaphoreType.DMA(((NDEV - 1) * n,)), pltpu.SemaphoreType.DMA((n,))]


def _exchange_copies(x_refs, o_refs, sems, scatter, start, wait):
    n = len(x_refs)
    flags = _flags(scatter, n)
    send_sems, recv_sems, local_sems = sems
    ix, iy, ic = lax.axis_index("x"), lax.axis_index("y"), lax.axis_index("c")
    me = 4 * ix + 2 * iy + ic
    local = [pltpu.make_async_copy(x.at[me] if sc else x, o.at[me], local_sems.at[a])
             for a, (x, o, sc) in enumerate(zip(x_refs, o_refs, flags))]

    def peer(k):
        return (1 - ix if k & 4 else ix, 1 - iy if k & 2 else iy, 1 - ic if k & 1 else ic)

    def index(p):
        return 4 * p[0] + 2 * p[1] + p[2]

    def copy(a, k, src, dst, to):
        sem = (k - 1) * n + a
        return pltpu.make_async_remote_copy(src_ref=src, dst_ref=dst, send_sem=send_sems.at[sem], recv_sem=recv_sems.at[sem],
                                            device_id=to, device_id_type=MESH)

    sends, arrivals, passed_on = [], [], []
    for a, (x, o, sc) in enumerate(zip(x_refs, o_refs, flags)):
        if sc:
            for k in range(1, NDEV):
                p = peer(k)
                sends.append(copy(a, k, x.at[index(p)], o.at[me], p))
                arrivals.append(copy(a, k, x.at[me], o.at[index(p)], p))
        else:
            sib = peer(1)
            sends.append(copy(a, 1, x, o.at[me], sib))
            arrivals.append(copy(a, 1, x, o.at[index(sib)], sib))
            for k in (2, 4, 6):
                p, q = peer(k), peer(k + 1)
                sends.append(copy(a, k, x, o.at[me], p))
                passed_on.append((copy(a, k, x, o.at[index(p)], p), copy(a, k + 1, o.at[index(p)], o.at[index(p)], sib)))
                arrivals.append(copy(a, k + 1, o.at[index(q)], o.at[index(q)], sib))
    if start:
        for cp in local + sends:
            cp.start()
    if wait:
        for landed, hand_over in passed_on:
            landed.wait_recv()
            hand_over.start()
        for cp in arrivals:
            cp.wait_recv()
        for cp in sends + [hand_over for _, hand_over in passed_on]:
            cp.wait_send()
        for cp in local:
            cp.wait()


def _call(body, args, *, name, grid, in_specs, out_specs, out_shape, scratch_shapes=(), sem, xchg=None):
    if xchg is None:
        return pl.pallas_call(body, name=name, grid=grid, in_specs=in_specs, out_specs=out_specs, out_shape=out_shape,
                              scratch_shapes=list(scratch_shapes), compiler_params=_params(*sem))(*args), None
    arrs, scatter = xchg
    n, ni, no, ns = len(arrs), len(in_specs), len(out_specs), len(scratch_shapes)

    def wrapped(*refs):
        ins, x_refs = refs[:ni], refs[ni:ni + n]
        outs, o_refs = refs[ni + n:ni + n + no], refs[ni + n + no:ni + 2 * n + no]
        scratch, sems = refs[ni + 2 * n + no:ni + 2 * n + no + ns], refs[ni + 2 * n + no + ns:]
        ids = [pl.program_id(a) for a in range(len(grid))]
        first = functools.reduce(jnp.logical_and, [p == 0 for p in ids])
        last = functools.reduce(jnp.logical_and, [p == g - 1 for p, g in zip(ids, grid)])

        @pl.when(first)
        def _():
            _exchange_copies(x_refs, o_refs, sems, scatter, True, False)

        body(*ins, *outs, *scratch)

        @pl.when(last)
        def _():
            _exchange_copies(x_refs, o_refs, sems, scatter, False, True)

    res = pl.pallas_call(
        wrapped, name=name, grid=grid, in_specs=list(in_specs) + [HBM_SPEC] * n, out_specs=list(out_specs) + [HBM_SPEC] * n,
        out_shape=list(out_shape) + _exchange_out_shape(arrs, scatter),
        scratch_shapes=list(scratch_shapes) + _exchange_sems(n),
        compiler_params=_params(*("arbitrary",) * len(grid)))(*args, *arrs)
    return res[:no], res[no:]


def _sum_parts(parts, *, name):
    r = parts.shape[1]

    def body(p_ref, o_ref):
        acc = p_ref[0]
        for j in range(1, NDEV):
            acc = acc + p_ref[j]
        o_ref[...] = acc

    return pl.pallas_call(body, name=name, out_shape=jax.ShapeDtypeStruct((r, LANES), f32), compiler_params=_params())(parts)


def _col_pieces(n, bounds):
    out = []
    for p, (a, b) in enumerate(bounds):
        for j in range(NDEV):
            lo, hi = max(a, n * j), min(b, n * (j + 1))
            if lo < hi:
                out.append((p, j, lo - a, lo - n * j, hi - lo))
    return out


def _unshard_cols(stacked, bounds, *, name):
    _, k, n = stacked.shape
    tk = _tile(k, 256)
    plan = _col_pieces(n, bounds)

    def body(x_ref, *o_refs):
        for p, j, po, so, w in plan:
            o_refs[p][:, po:po + w] = x_ref[j, :, so:so + w]

    return pl.pallas_call(
        body, name=name, grid=(k // tk,), in_specs=[pl.BlockSpec((NDEV, tk, n), lambda i: (0, i, 0))],
        out_specs=[pl.BlockSpec((tk, b - a), lambda i: (i, 0)) for a, b in bounds],
        out_shape=[jax.ShapeDtypeStruct((k, b - a), stacked.dtype) for a, b in bounds],
        compiler_params=_params("parallel"))(stacked)


def _shard_cols(pieces, *, out_dtype, name):
    k = pieces[0].shape[0]
    bounds, off = [], 0
    for p in pieces:
        bounds.append((off, off + p.shape[1]))
        off += p.shape[1]
    n = off // NDEV
    tk = _tile(k, 256)
    plan = _col_pieces(n, bounds)

    def body(*refs):
        o_ref = refs[-1]
        for p, j, po, so, w in plan:
            o_ref[j, :, so:so + w] = refs[p][:, po:po + w].astype(out_dtype)

    return pl.pallas_call(
        body, name=name, grid=(k // tk,), in_specs=[pl.BlockSpec((tk, b - a), lambda i: (i, 0)) for a, b in bounds],
        out_specs=pl.BlockSpec((NDEV, tk, n), lambda i: (0, i, 0)),
        out_shape=jax.ShapeDtypeStruct((NDEV, k, n), out_dtype),
        compiler_params=_params("parallel"))(*pieces)


def _pack(arrs):
    flat = jnp.concatenate([a.reshape(-1) for a in arrs])
    pad = (-flat.shape[0]) % (SUBLANES * LANES)
    return jnp.pad(flat, (0, pad)).reshape(-1, LANES)


def _unpack(flat, shapes):
    out, off = [], 0
    for s in shapes:
        n = math.prod(s)
        out.append(flat[..., off:off + n].reshape(flat.shape[:-1] + tuple(s)))
        off += n
    return out


def _adamw_math(w, gg, m, v):
    nm = ADAM_B1 * m + (1.0 - ADAM_B1) * gg
    nv = ADAM_B2 * v + (1.0 - ADAM_B2) * (gg * gg)
    m_hat = nm / (1.0 - ADAM_B1 ** ADAM_STEP)
    v_hat = nv / (1.0 - ADAM_B2 ** ADAM_STEP)
    return -ADAM_LR * (m_hat / (jnp.sqrt(v_hat) + ADAM_EPS) + ADAM_WD * w), nm, nv


def _row_block(r, c, copies):
    tr = r
    while copies * tr * c * 4 > EW_BLOCK_BYTES and tr % (4 * SUBLANES) == 0:
        tr //= 2
    return tr


def _adamw(w, g, m, v, *, name):
    r, c = w.shape
    tr = _row_block(r, c, 1)

    def body(w_ref, g_ref, m_ref, v_ref, d_ref, nm_ref, nv_ref):
        d_ref[...], nm_ref[...], nv_ref[...] = _adamw_math(w_ref[...], g_ref[...], m_ref[...], v_ref[...])

    spec = pl.BlockSpec((tr, c), lambda i: (i, 0))
    shp = jax.ShapeDtypeStruct((r, c), f32)
    return pl.pallas_call(
        body, name=name, grid=(r // tr,), in_specs=[spec] * 4, out_specs=[spec] * 3, out_shape=[shp] * 3,
        compiler_params=_params("parallel"),
    )(w, g, m, v)


def _adamw_many(ws, gs, ms, vs, *, name):
    n = len(ws)

    def body(*refs):
        outs = refs[4 * n:]
        for i in range(n):
            res = _adamw_math(refs[i][...], refs[n + i][...], refs[2 * n + i][...], refs[3 * n + i][...])
            for o_ref, r in zip(outs[3 * i:3 * i + 3], res):
                o_ref[...] = r

    res = pl.pallas_call(body, name=name, out_shape=[jax.ShapeDtypeStruct(w.shape, f32) for w in ws for _ in range(3)],
                         compiler_params=_params())(*ws, *gs, *ms, *vs)
    return [res[3 * i:3 * i + 3] for i in range(n)]


def _sum_adamw(parts, w, m, v, *, name, xchg=None):
    r, c = w.shape
    tr = _row_block(r, c, NDEV)

    def body(p_ref, w_ref, m_ref, v_ref, g_ref, d_ref, nm_ref, nv_ref):
        gg = p_ref[0].astype(f32)
        for j in range(1, NDEV):
            gg = gg + p_ref[j].astype(f32)
        g_ref[...] = gg
        d_ref[...], nm_ref[...], nv_ref[...] = _adamw_math(w_ref[...], gg, m_ref[...], v_ref[...])

    spec = pl.BlockSpec((tr, c), lambda i: (i, 0))
    shp = jax.ShapeDtypeStruct((r, c), f32)
    res, got = _call(
        body, (parts, w, m, v), name=name, grid=(r // tr,),
        in_specs=[pl.BlockSpec((NDEV, tr, c), lambda i: (0, i, 0))] + [spec] * 3,
        out_specs=[spec] * 4, out_shape=[shp] * 4, sem=("parallel",), xchg=xchg)
    return tuple(res) if xchg is None else tuple(res) + (got,)


def _cond_fwd(c_all, w, b, *, name):
    nb, n = c_all.shape[0], w.shape[1]

    def body(c_ref, w_ref, b_ref, o_ref):
        cc = c_ref[...]
        o_ref[...] = jnp.dot(cc * _sig(cc), w_ref[...], preferred_element_type=f32,
                             precision=lax.Precision.HIGHEST) + b_ref[...]

    return pl.pallas_call(body, name=name, out_shape=jax.ShapeDtypeStruct((nb, n), f32),
                          compiler_params=_params())(c_all, w, b)


def _cond_bwd(c_all, dmod, *, name):
    d, n = c_all.shape[1], dmod.shape[1]

    def body(c_ref, g_ref, o_ref):
        cc = c_ref[...]
        o_ref[...] = lax.dot_general(cc * _sig(cc), g_ref[...], (((0,), (0,)), ((), ())), preferred_element_type=f32,
                                     precision=lax.Precision.HIGHEST)

    return pl.pallas_call(body, name=name, out_shape=jax.ShapeDtypeStruct((d, n), f32),
                          compiler_params=_params())(c_all, dmod)


def _ssm_disc(lam_re, lam_im, log_dt):
    lr = jnp.minimum(lam_re, -1e-4)
    li = lam_im
    dt = jnp.exp(log_dt)
    mag = jnp.exp(lr * dt)
    ang = li * dt
    lbr, lbi = mag * jnp.cos(ang), mag * jnp.sin(ang)
    num_r, num_i = lbr - 1.0, lbi
    den = lr * lr + li * li
    return lbr, lbi, (num_r * lr + num_i * li) / den, (num_i * lr - num_r * li) / den


def _ssm_prep(lam_re, lam_im, log_dt, *, name):
    def body(a, b, c, o1, o2, o3, o4):
        o1[...], o2[...], o3[...], o4[...] = _ssm_disc(a[...], b[...], c[...])

    shp = jax.ShapeDtypeStruct(lam_re.shape, f32)
    return pl.pallas_call(body, name=name, out_shape=[shp] * 4, compiler_params=_params())(lam_re, lam_im, log_dt)


def _ssm_prep_bwd(lam_re, lam_im, log_dt, cts, *, name):
    def body(a, b, c, g1, g2, g3, g4, o1, o2, o3):
        _, vjp = jax.vjp(_ssm_disc, a[...], b[...], c[...])
        o1[...], o2[...], o3[...] = vjp((g1[...], g2[...], g3[...], g4[...]))

    shp = jax.ShapeDtypeStruct(lam_re.shape, f32)
    return pl.pallas_call(body, name=name, out_shape=[shp, shp, jax.ShapeDtypeStruct(log_dt.shape, f32)],
                          compiler_params=_params())(lam_re, lam_im, log_dt, *cts)


S5_ROWS = 512


def _step_major(x3):
    k, nt, c = x3.shape
    return jnp.swapaxes(x3, 0, 1).reshape(k * nt, c)


def _chunk_major(x2, nt):
    return jnp.swapaxes(x2.reshape(nt, SUBLANES, x2.shape[1]), 0, 1)


def _chain_carries(loc_r, loc_i, pr, pi_, forward):
    row = lax.broadcasted_iota(jnp.int32, loc_r.shape, 0)
    shift = 1 if forward else SUBLANES - 1
    order = range(1, SUBLANES) if forward else range(SUBLANES - 2, -1, -1)
    er, ei = loc_r, loc_i
    for k in order:
        sr, si = pltpu.roll(er, shift, 0), pltpu.roll(ei, shift, 0)
        er = jnp.where(row == k, loc_r + pr * sr - pi_ * si, er)
        ei = jnp.where(row == k, loc_i + pr * si + pi_ * sr, ei)
    edge = 0 if forward else SUBLANES - 1
    return (jnp.where(row == edge, 0.0, pltpu.roll(er, shift, 0)), jnp.where(row == edge, 0.0, pltpu.roll(ei, shift, 0)))


def _chunk_power(ar, ai, chunk_len):
    pr, pi_ = ar, ai
    for _ in range(int(math.log2(chunk_len))):
        pr, pi_ = pr * pr - pi_ * pi_, 2.0 * pr * pi_
    return pr, pi_


def _ssm_mats(bre_ref, bim_ref, cre_ref, cim_ref, cfr_ref, cfi_ref, bbar_s, cmat_s, nq):
    for q in range(nq):
        cr, ci, br, bi = cfr_ref[q], cfi_ref[q], bre_ref[q], bim_ref[q]
        bbar_s[q, :, 0:QS] = (cr * br - ci * bi).astype(bf16)
        bbar_s[q, :, QS:2 * QS] = (cr * bi + ci * br).astype(bf16)
        cmat_s[q, 0:QS, :] = cre_ref[q].astype(bf16)
        cmat_s[q, QS:2 * QS, :] = (-cim_ref[q]).astype(bf16)


def _ssm_fwd(u, ar, ai, bre, bim, cre, cim, cfr, cfi, dvec, *, name, xchg=None):
    s, sw = u.shape
    nq = sw // QW
    st = nq * 2 * QS
    tb = _tile(s, S5_ROWS)
    nb, nt, chunk_len = s // tb, tb // SUBLANES, s // SUBLANES
    assert chunk_len & (chunk_len - 1) == 0 and nt % 16 == 0

    def body(u_ref, ar_ref, ai_ref, bre_ref, bim_ref, cre_ref, cim_ref, cfr_ref, cfi_ref, d_ref,
             h_out, yraw_out, y_out, buf, hc, bbar_s, cmat_s):
        ph, i = pl.program_id(0), pl.program_id(1)

        @pl.when(i == 0)
        def _():
            _ssm_mats(bre_ref, bim_ref, cre_ref, cim_ref, cfr_ref, cfi_ref, bbar_s, cmat_s, nq)

        @pl.when((ph == 0) & (i == 0))
        def _():
            hc[...] = jnp.zeros_like(hc)

        @pl.when((ph == 1) & (i == 0))
        def _():
            for q in range(nq):
                o = q * 2 * QS
                pr, pi_ = _chunk_power(ar_ref[q], ai_ref[q], chunk_len)
                sr, si = _chain_carries(hc[:, o:o + QS], hc[:, o + QS:o + 2 * QS], pr, pi_, True)
                hc[:, o:o + QS] = sr
                hc[:, o + QS:o + 2 * QS] = si

        uu = u_ref[...]
        up = _step_major(uu).astype(bf16)
        for q in range(nq):
            o = q * 2 * QS
            buf[:, o:o + 2 * QS] = jnp.dot(up[:, q * QW:(q + 1) * QW], bbar_s[q], preferred_element_type=f32)

        for q in range(nq):
            o = q * 2 * QS
            a_r = jnp.broadcast_to(ar_ref[q], (SUBLANES, QS))
            a_i = jnp.broadcast_to(ai_ref[q], (SUBLANES, QS))

            def step(t, carry, o=o, a_r=a_r, a_i=a_i):
                hr, hi = carry
                r0 = pl.multiple_of(t * SUBLANES, SUBLANES)
                nr = a_r * hr - a_i * hi + buf[pl.ds(r0, SUBLANES), o:o + QS]
                ni = a_r * hi + a_i * hr + buf[pl.ds(r0, SUBLANES), o + QS:o + 2 * QS]
                buf[pl.ds(r0, SUBLANES), o:o + QS] = nr
                buf[pl.ds(r0, SUBLANES), o + QS:o + 2 * QS] = ni
                return nr, ni

            hr, hi = lax.fori_loop(0, nt, step, (hc[:, o:o + QS], hc[:, o + QS:o + 2 * QS]))
            hc[:, o:o + QS] = hr
            hc[:, o + QS:o + 2 * QS] = hi

        @pl.when(ph == 1)
        def _():
            for q in range(nq):
                o = q * 2 * QS
                cs = slice(q * QW, (q + 1) * QW)
                hq = buf[:, o:o + 2 * QS].astype(bf16)
                h_out[:, o:o + 2 * QS] = hq
                yq = _chunk_major(jnp.dot(hq, cmat_s[q], preferred_element_type=f32), nt) + d_ref[:, cs] * uu[:, :, cs]
                yraw_out[:, :, cs] = yq
                y_out[:, :, cs] = _gelu(yq).astype(bf16)

    blk = lambda ph, i: (0, i, 0)
    oblk = lambda ph, i: (0, i * ph, 0)
    act = lambda dt: jax.ShapeDtypeStruct((SUBLANES, chunk_len, sw), dt)
    (h_p, yraw3, y3), got = _call(
        body, (u.reshape(SUBLANES, chunk_len, sw), ar, ai, bre, bim, cre, cim, cfr, cfi, dvec), name=name, grid=(2, nb),
        in_specs=[pl.BlockSpec((SUBLANES, nt, sw), blk), _full(ar.shape), _full(ai.shape), _full(bre.shape), _full(bim.shape),
                  _full(cre.shape), _full(cim.shape), _full(cfr.shape), _full(cfi.shape), _full(dvec.shape)],
        out_specs=[pl.BlockSpec((tb, st), lambda ph, i: (i * ph, 0)), pl.BlockSpec((SUBLANES, nt, sw), oblk),
                   pl.BlockSpec((SUBLANES, nt, sw), oblk)],
        out_shape=[jax.ShapeDtypeStruct((s, st), bf16), act(f32), act(bf16)],
        scratch_shapes=[pltpu.VMEM((tb, st), f32), pltpu.VMEM((SUBLANES, st), f32),
                        pltpu.VMEM((nq, QW, 2 * QS), bf16), pltpu.VMEM((nq, 2 * QS, QW), bf16)],
        sem=("arbitrary", "arbitrary"), xchg=xchg)
    return h_p, yraw3, y3.reshape(s, sw), got


def _ssm_bwd(dy, yraw3, u, h_p, ar, ai, bre, bim, cre, cim, cfr, cfi, dvec, *, name, xchg=None):
    s, sw = u.shape
    nq = sw // QW
    st = nq * 2 * QS
    tb = _tile(s, S5_ROWS)
    nb, nt, chunk_len = s // tb, tb // SUBLANES, s // SUBLANES

    def body(dy_ref, yraw_ref, u_ref, h_ref, ar_ref, ai_ref, bre_ref, bim_ref, cre_ref, cim_ref, cfr_ref, cfi_ref, d_ref,
             du_out, dbre_out, dbim_out, dcre_out, dcim_out, dcfr_out, dcfi_out, dlbr_out, dlbi_out, dd_out, dbu_out,
             buf, rc, acc, dbbar, dcmat, bbar_s, cmat_s):
        ph, i = pl.program_id(0), pl.program_id(1)

        @pl.when(i == 0)
        def _():
            _ssm_mats(bre_ref, bim_ref, cre_ref, cim_ref, cfr_ref, cfi_ref, bbar_s, cmat_s, nq)

        @pl.when((ph == 0) & (i == 0))
        def _():
            rc[...] = jnp.zeros_like(rc)

        @pl.when((ph == 1) & (i == 0))
        def _():
            for q in range(nq):
                o = q * 2 * QS
                pr, pi_ = _chunk_power(ar_ref[q], ai_ref[q], chunk_len)
                sr, si = _chain_carries(rc[:, o:o + QS], rc[:, o + QS:o + 2 * QS], pr, -pi_, False)
                rc[:, o:o + QS] = sr
                rc[:, o + QS:o + 2 * QS] = si
            acc[...] = jnp.zeros_like(acc)
            dbbar[...] = jnp.zeros_like(dbbar)
            dcmat[...] = jnp.zeros_like(dcmat)
            dd_out[...] = jnp.zeros_like(dd_out)
            dbu_out[...] = jnp.zeros_like(dbu_out)

        dyraw = dy_ref[...].astype(f32) * _gelu_grad(yraw_ref[...])
        dyp = _step_major(dyraw).astype(bf16)
        for q in range(nq):
            o = q * 2 * QS
            buf[:, o:o + 2 * QS] = lax.dot_general(dyp[:, q * QW:(q + 1) * QW], cmat_s[q], (((1,), (1,)), ((), ())),
                                                   preferred_element_type=f32)

        def recur(with_grad):
            for q in range(nq):
                o = q * 2 * QS
                a_r = jnp.broadcast_to(ar_ref[q], (SUBLANES, QS))
                a_i = jnp.broadcast_to(ai_ref[q], (SUBLANES, QS))

                def step(j, carry, o=o, a_r=a_r, a_i=a_i):
                    r16 = pl.multiple_of((nt // 2 - 1 - j) * 2 * SUBLANES, 2 * SUBLANES)
                    if with_grad:
                        rr, ri, gr, gi = carry
                        h_re = h_ref[pl.ds(r16, 2 * SUBLANES), o:o + QS].astype(f32)
                        h_im = h_ref[pl.ds(r16, 2 * SUBLANES), o + QS:o + 2 * QS].astype(f32)
                    else:
                        rr, ri = carry
                    for half in (1, 0):
                        rows = pl.ds(pl.multiple_of(r16 + half * SUBLANES, SUBLANES), SUBLANES)
                        if with_grad:
                            hr = h_re[half * SUBLANES:(half + 1) * SUBLANES]
                            hi = h_im[half * SUBLANES:(half + 1) * SUBLANES]
                            gr = gr + hr * rr + hi * ri
                            gi = gi + hr * ri - hi * rr
                        nr = buf[rows, o:o + QS] + a_r * rr + a_i * ri
                        ni = buf[rows, o + QS:o + 2 * QS] + a_r * ri - a_i * rr
                        buf[rows, o:o + QS] = nr
                        buf[rows, o + QS:o + 2 * QS] = ni
                        rr, ri = nr, ni
                    return (rr, ri, gr, gi) if with_grad else (rr, ri)

                init = (rc[:, o:o + QS], rc[:, o + QS:o + 2 * QS])
                if with_grad:
                    init = init + (acc[:, o:o + QS], acc[:, o + QS:o + 2 * QS])
                res = lax.fori_loop(0, nt // 2, step, init)
                rc[:, o:o + QS] = res[0]
                rc[:, o + QS:o + 2 * QS] = res[1]
                if with_grad:
                    acc[:, o:o + QS] = res[2]
                    acc[:, o + QS:o + 2 * QS] = res[3]

        @pl.when(ph == 0)
        def _():
            recur(False)

        @pl.when(ph == 1)
        def _():
            recur(True)
            uu = u_ref[...]
            up = _step_major(uu).astype(bf16)
            dd_out[...] += _colsum((dyraw * uu).reshape(tb, sw))
            for q in range(nq):
                o = q * 2 * QS
                cs = slice(q * QW, (q + 1) * QW)
                lam = buf[:, o:o + 2 * QS].astype(bf16)
                duq = _chunk_major(lax.dot_general(lam, bbar_s[q], (((1,), (1,)), ((), ())), preferred_element_type=f32), nt) \
                    + d_ref[:, cs] * dyraw[:, :, cs]
                du_out[:, :, cs] = duq.astype(bf16)
                dbu_out[:, cs] += _colsum(duq.reshape(tb, QW))
                dbbar[q] += lax.dot_general(up[:, cs], lam, (((0,), (0,)), ((), ())), preferred_element_type=f32)
                dcmat[q] += lax.dot_general(dyp[:, cs], h_ref[:, o:o + 2 * QS], (((0,), (0,)), ((), ())),
                                            preferred_element_type=f32)

        @pl.when((ph == 1) & (i == nb - 1))
        def _():
            for q in range(nq):
                o = q * 2 * QS
                cr, ci, br, bi = cfr_ref[q], cfi_ref[q], bre_ref[q], bim_ref[q]
                gr, gi = dbbar[q, :, 0:QS], dbbar[q, :, QS:2 * QS]
                dbre_out[q] = cr * gr + ci * gi
                dbim_out[q] = cr * gi - ci * gr
                dcfr_out[q] = _colsum(gr * br + gi * bi)
                dcfi_out[q] = _colsum(gi * br - gr * bi)
                dcre_out[q] = dcmat[q, :, 0:QS].T
                dcim_out[q] = -dcmat[q, :, QS:2 * QS].T
                dlbr_out[q] = _colsum(acc[:, o:o + QS])
                dlbi_out[q] = _colsum(acc[:, o + QS:o + 2 * QS])

    blk = lambda ph, i: (0, nb - 1 - i, 0)
    oblk = lambda ph, i: (0, (nb - 1 - i) * ph + (nb - 1) * (1 - ph), 0)
    pshapes = [ar.shape, ai.shape, bre.shape, bim.shape, cre.shape, cim.shape, cfr.shape, cfi.shape, dvec.shape]
    oshapes = [bre.shape, bim.shape, cre.shape, cim.shape, cfr.shape, cfi.shape, ar.shape, ai.shape, dvec.shape, dvec.shape]
    act = pl.BlockSpec((SUBLANES, nt, sw), blk)
    view = lambda a: a.reshape(SUBLANES, chunk_len, sw)
    res, got = _call(
        body, (view(dy), yraw3, view(u), h_p, ar, ai, bre, bim, cre, cim, cfr, cfi, dvec), name=name, grid=(2, nb),
        in_specs=[act, act, act, pl.BlockSpec((tb, st), lambda ph, i: (nb - 1 - i, 0))] + [_full(p) for p in pshapes],
        out_specs=[pl.BlockSpec((SUBLANES, nt, sw), oblk)] + [_full(p) for p in oshapes],
        out_shape=[jax.ShapeDtypeStruct((SUBLANES, chunk_len, sw), bf16)] + [jax.ShapeDtypeStruct(p, f32) for p in oshapes],
        scratch_shapes=[pltpu.VMEM((tb, st), f32),
                        pltpu.VMEM((SUBLANES, st), f32), pltpu.VMEM((SUBLANES, st), f32),
                        pltpu.VMEM((nq, QW, 2 * QS), f32), pltpu.VMEM((nq, QW, 2 * QS), f32),
                        pltpu.VMEM((nq, QW, 2 * QS), bf16), pltpu.VMEM((nq, 2 * QS, QW), bf16)],
        sem=("arbitrary", "arbitrary"), xchg=xchg)
    return (res[0].reshape(s, sw),) + tuple(res[1:]) + (got,)


def _lnmod(x, sc, sh, *, name):
    s, d = x.shape
    tb = _tile(s, MM_ROWS)

    def body(x_ref, sc_ref, sh_ref, o_ref):
        xh, _ = _ln(x_ref[...])
        o_ref[...] = (xh * (1.0 + sc_ref[...]) + sh_ref[...]).astype(bf16)

    blk = pl.BlockSpec((tb, d), lambda i: (i, 0))
    vec = pl.BlockSpec((1, d), _row)
    return pl.pallas_call(body, name=name, grid=(s // tb,), in_specs=[blk, vec, vec], out_specs=blk,
                          out_shape=jax.ShapeDtypeStruct((s, d), bf16), compiler_params=_params("parallel"))(x, sc, sh)


ROWS = 32


def _row_chunks(n_rows, rows, fn, init, start=0):
    return lax.fori_loop(start, n_rows // rows, lambda c, carry: fn(pl.multiple_of(c * rows, rows), carry), init)


def _rows_from(win, o, rows):
    if o % SUBLANES == 0:
        return win[o:o + rows]
    n = win.shape[0]
    return pltpu.roll(win, (n - o) % n, 0)[0:rows]


def _window_before(ref, halo, r0, rows, first, cols):
    if first:
        return jnp.concatenate([halo, ref[pl.ds(0, rows), cols]], axis=0)
    return ref[pl.ds(pl.multiple_of(r0 - SUBLANES, SUBLANES), rows + SUBLANES), cols]


def _taps3(win, w, off, rows):
    return _rows_from(win, off, rows) * w[0] + _rows_from(win, off + 1, rows) * w[1] + _rows_from(win, off + 2, rows) * w[2]


def _fold8(x):
    acc = x[0:SUBLANES]
    for r in range(1, x.shape[0] // SUBLANES):
        acc = acc + x[r * SUBLANES:(r + 1) * SUBLANES]
    return acc


def _conv_halo_specs(tb, cw, halo, s):
    per = tb // halo
    prev = pl.BlockSpec((halo, cw), lambda i: (jnp.maximum(i * per - 1, 0), 0))
    nxt = pl.BlockSpec((halo, cw), lambda i: (jnp.minimum((i + 1) * per, s // halo - 1), 0))
    return prev, nxt


WIDE_ROWS = 16


def _shift_groups(lo, hi):
    return [(b, [o for o in range(lo, hi + 1) if o % SUBLANES == b]) for b in range(SUBLANES)]


def _shifted(win, b):
    return win if b == 0 else _rows_from(win, b, win.shape[0] - SUBLANES)


def _conv31(win, w_ref, cols, rows, lo, hi, tap_of):
    acc = None
    for b, offs in _shift_groups(lo, hi):
        if offs:
            wb = _shifted(win, b)
            for o in offs:
                term = wb[o - b:o - b + rows] * w_ref[pl.ds(tap_of(o), 1), cols]
                acc = term if acc is None else acc + term
    return acc


def _gate_into(ext, a_ref, g_ref, ah_ref, gh_ref, tb, i):
    ext[pl.ds(0, CONV_HALO), :] = jnp.where(i > 0, ah_ref[...] * _sig(gh_ref[...]), 0.0)

    def chunk(r0, carry):
        ext[pl.ds(pl.multiple_of(r0 + CONV_HALO, SUBLANES), WIDE_ROWS), :] = \
            a_ref[pl.ds(r0, WIDE_ROWS), :] * _sig(g_ref[pl.ds(r0, WIDE_ROWS), :])
        return carry

    _row_chunks(tb, WIDE_ROWS, chunk, 0)


def _causal_conv_into(v2buf, ext, w_ref, b_ref, tb, cw):
    for ct in range(cw // LANES):
        cols = slice(ct * LANES, (ct + 1) * LANES)

        def chunk(r0, carry, cols=cols):
            win = ext[pl.ds(r0, ROWS + CONV_HALO), cols]
            v2buf[pl.ds(r0, ROWS), cols] = _conv31(win, w_ref, cols, ROWS, 2, CONV_K + 1, lambda o: o - 2) + b_ref[:, cols]
            return carry

        _row_chunks(tb, ROWS, chunk, 0)


def _silu_grad(x):
    sg = _sig(x)
    return sg * (1.0 + x * (1.0 - sg))


def _conv_fwd(cva, cvg, w, b, lng, lnb, *, name, xchg=None):
    s, cw = cva.shape
    tb = _tile(s, 256)
    prev, _ = _conv_halo_specs(tb, cw, CONV_HALO, s)

    def body(a_ref, g_ref, ah_ref, gh_ref, w_ref, b_ref, lng_ref, lnb_ref, o_ref, v2_ref, ext):
        _gate_into(ext, a_ref, g_ref, ah_ref, gh_ref, tb, pl.program_id(0))
        _causal_conv_into(v2_ref, ext, w_ref, b_ref, tb, cw)
        xh, _ = _ln(v2_ref[...])
        v3 = xh * lng_ref[...] + lnb_ref[...]
        o_ref[...] = (v3 * _sig(v3)).astype(bf16)

    blk = pl.BlockSpec((tb, cw), lambda i: (i, 0))
    vec = pl.BlockSpec((1, cw), _row)
    (v4, v2), got = _call(
        body, (cva, cvg, cva, cvg, w, b, lng, lnb), name=name, grid=(s // tb,),
        in_specs=[blk, blk, prev, prev, _full(w.shape), vec, vec, vec], out_specs=[blk, blk],
        out_shape=[jax.ShapeDtypeStruct((s, cw), bf16), jax.ShapeDtypeStruct((s, cw), f32)],
        scratch_shapes=[pltpu.VMEM((tb + CONV_HALO, cw), f32)], sem=("parallel",), xchg=xchg)
    return v4, v2, got


def _conv_bwd_ln(dv4, v2, lng, lnb, *, name):
    s, cw = v2.shape
    tb = _tile(s, 256)

    def body(d_ref, v2_ref, lng_ref, lnb_ref, o_ref, dg_ref, db_ref):
        @pl.when(pl.program_id(0) == 0)
        def _():
            dg_ref[...] = jnp.zeros_like(dg_ref)
            db_ref[...] = jnp.zeros_like(db_ref)

        xh, rstd = _ln(v2_ref[...])
        v3 = xh * lng_ref[...] + lnb_ref[...]
        dv3 = d_ref[...].astype(f32) * _silu_grad(v3)
        dg_ref[...] += _colsum(dv3 * xh)
        db_ref[...] += _colsum(dv3)
        o_ref[...] = _ln_bwd(dv3 * lng_ref[...], xh, rstd)

    blk = pl.BlockSpec((tb, cw), lambda i: (i, 0))
    vec = pl.BlockSpec((1, cw), _row)
    vshape = jax.ShapeDtypeStruct((1, cw), f32)
    return pl.pallas_call(
        body, name=name, grid=(s // tb,), in_specs=[blk, blk, vec, vec],
        out_specs=[blk, vec, vec], out_shape=[jax.ShapeDtypeStruct((s, cw), f32), vshape, vshape],
        compiler_params=_params("arbitrary"))(dv4, v2, lng, lnb)


def _conv_bwd_taps(dv2, cva, cvg, w, *, name, xchg=None):
    s, cw = cva.shape
    tb = _tile(s, 256)
    nb = s // tb
    prev, nxt = _conv_halo_specs(tb, cw, CONV_HALO, s)

    def body(d_ref, dn_ref, a_ref, g_ref, ah_ref, gh_ref, w_ref, da_ref, dg_ref, dw_ref, db_ref, sa_ref, sg_ref,
             ext, dext, dvbuf, tap_sums):
        i = pl.program_id(0)

        @pl.when(i == 0)
        def _():
            for r in (dw_ref, db_ref, sa_ref, sg_ref):
                r[...] = jnp.zeros_like(r)

        _gate_into(ext, a_ref, g_ref, ah_ref, gh_ref, tb, i)
        dext[pl.ds(tb, CONV_HALO), :] = jnp.where(i < nb - 1, dn_ref[...], 0.0)

        def copy(r0, carry):
            dext[pl.ds(r0, WIDE_ROWS), :] = d_ref[pl.ds(r0, WIDE_ROWS), :]
            return carry

        _row_chunks(tb, WIDE_ROWS, copy, 0)

        for ct in range(cw // LANES):
            cols = slice(ct * LANES, (ct + 1) * LANES)

            tap_sums[...] = jnp.zeros_like(tap_sums)

            def back(r0, carry, cols=cols):
                win = dext[pl.ds(r0, ROWS + CONV_HALO), cols]
                dvbuf[pl.ds(r0, ROWS), cols] = _conv31(win, w_ref, cols, ROWS, 0, CONV_K - 1, lambda o: CONV_K - 1 - o)
                win = ext[pl.ds(r0, ROWS + CONV_HALO), cols]
                dd = d_ref[pl.ds(r0, ROWS), cols]
                for b, offs in _shift_groups(2, CONV_K + 1):
                    wb = _shifted(win, b)
                    for o in offs:
                        tap_sums[o - 2] += _fold8(dd * wb[o - b:o - b + ROWS])
                return carry

            _row_chunks(tb, ROWS, back, 0)
            for k in range(CONV_K):
                dw_ref[pl.ds(k, 1), cols] += _colsum(tap_sums[k])

        def gate_back(r0, sums):
            rows = pl.ds(r0, WIDE_ROWS)
            aa, sg, dv = a_ref[rows, :], _sig(g_ref[rows, :]), dvbuf[rows, :]
            da = dv * sg
            dgate = dv * aa * sg * (1.0 - sg)
            da_ref[rows, :] = da.astype(bf16)
            dg_ref[rows, :] = dgate.astype(bf16)
            return sums[0] + _fold8(da), sums[1] + _fold8(dgate), sums[2] + _fold8(d_ref[rows, :])

        zero = jnp.zeros((SUBLANES, cw), f32)
        sums = _row_chunks(tb, WIDE_ROWS, gate_back, (zero, zero, zero))
        sa_ref[...] += _colsum(sums[0])
        sg_ref[...] += _colsum(sums[1])
        db_ref[...] += _colsum(sums[2])

    blk = pl.BlockSpec((tb, cw), lambda i: (i, 0))
    vec = pl.BlockSpec((1, cw), _row)
    vshape = jax.ShapeDtypeStruct((1, cw), f32)
    act = jax.ShapeDtypeStruct((s, cw), bf16)
    res, got = _call(
        body, (dv2, dv2, cva, cvg, cva, cvg, w), name=name, grid=(nb,), in_specs=[blk, nxt, blk, blk, prev, prev, _full(w.shape)],
        out_specs=[blk, blk, _full(w.shape), vec, vec, vec],
        out_shape=[act, act, jax.ShapeDtypeStruct(w.shape, f32), vshape, vshape, vshape],
        scratch_shapes=[pltpu.VMEM((tb + CONV_HALO, cw), f32), pltpu.VMEM((tb + CONV_HALO, cw), f32), pltpu.VMEM((tb, cw), f32),
                        pltpu.VMEM((CONV_HALO, SUBLANES, LANES), f32)],
        sem=("arbitrary",), xchg=xchg)
    return tuple(res) + (got,)


def _glu_merge_out(ya, yb, ycv, gs, gc, w_o, *, name):
    s, d = ya.shape
    tb = _tile(s, 512)

    def body(ya_ref, yb_ref, ycv_ref, gs_ref, gc_ref, w_ref, m_ref, o_ref):
        ld = lambda r: r[...].astype(f32)
        z = ld(ya_ref) * _sig(ld(yb_ref))
        merged = (_sig(ld(gs_ref)) * z + _sig(ld(gc_ref)) * ld(ycv_ref)).astype(bf16)
        m_ref[...] = merged
        o_ref[...] = jnp.dot(merged, w_ref[...], preferred_element_type=f32).astype(bf16)

    blk = pl.BlockSpec((tb, d), lambda i: (i, 0))
    act = jax.ShapeDtypeStruct((s, d), bf16)
    return pl.pallas_call(body, name=name, grid=(s // tb,), in_specs=[blk] * 5 + [_full(w_o.shape)], out_specs=[blk, blk],
                          out_shape=[act, act], compiler_params=_params("parallel"))(ya, yb, ycv, gs, gc, w_o)


def _glu_merge_bwd(dm, ya, yb, ycv, gs, gc, *, name):
    s, d = ya.shape
    tb = _tile(s, 512)

    def body(dm_ref, ya_ref, yb_ref, ycv_ref, gs_ref, gc_ref, dya_ref, dyb_ref, dycv_ref, dgs_ref, dgc_ref, sgs_ref, sgc_ref):
        @pl.when(pl.program_id(0) == 0)
        def _():
            sgs_ref[...] = jnp.zeros_like(sgs_ref)
            sgc_ref[...] = jnp.zeros_like(sgc_ref)

        ld = lambda r: r[...].astype(f32)
        dmv, yav = ld(dm_ref), ld(ya_ref)
        sb, ss, scv = _sig(ld(yb_ref)), _sig(ld(gs_ref)), _sig(ld(gc_ref))
        z = yav * sb
        dz = dmv * ss
        dgs = dmv * z * ss * (1.0 - ss)
        dgc = dmv * ld(ycv_ref) * scv * (1.0 - scv)
        dya_ref[...] = (dz * sb).astype(bf16)
        dyb_ref[...] = (dz * yav * sb * (1.0 - sb)).astype(bf16)
        dycv_ref[...] = (dmv * scv).astype(bf16)
        dgs_ref[...] = dgs.astype(bf16)
        dgc_ref[...] = dgc.astype(bf16)
        sgs_ref[...] += _colsum(dgs)
        sgc_ref[...] += _colsum(dgc)

    blk = pl.BlockSpec((tb, d), lambda i: (i, 0))
    vec = pl.BlockSpec((1, d), _row)
    act = jax.ShapeDtypeStruct((s, d), bf16)
    vshape = jax.ShapeDtypeStruct((1, d), f32)
    return pl.pallas_call(body, name=name, grid=(s // tb,), in_specs=[blk] * 6, out_specs=[blk] * 5 + [vec, vec],
                          out_shape=[act] * 5 + [vshape, vshape], compiler_params=_params("arbitrary"))(dm, ya, yb, ycv, gs, gc)


def _resid_ln_mod(x, o, g, lng, lnb, sc, sh, alpha, *, name):
    s, d = x.shape
    tb = _tile(s, MM_ROWS)

    def body(x_ref, o_ref, g_ref, lng_ref, lnb_ref, sc_ref, sh_ref, x1_ref, h_ref):
        xh, _ = _ln(alpha * x_ref[...] + g_ref[...] * o_ref[...].astype(f32))
        x1 = xh * lng_ref[...] + lnb_ref[...]
        x1_ref[...] = x1
        xh1, _ = _ln(x1)
        h_ref[...] = (xh1 * (1.0 + sc_ref[...]) + sh_ref[...]).astype(bf16)

    blk = pl.BlockSpec((tb, d), lambda i: (i, 0))
    vec = pl.BlockSpec((1, d), _row)
    return pl.pallas_call(body, name=name, grid=(s // tb,), in_specs=[blk, blk] + [vec] * 5, out_specs=[blk, blk],
                          out_shape=[jax.ShapeDtypeStruct((s, d), f32), jax.ShapeDtypeStruct((s, d), bf16)],
                          compiler_params=_params("parallel"))(x, o, g, lng, lnb, sc, sh)


def _resid_ln_loss(x1, f, w_dn, g, lng, lnb, tgt, alpha, *, name):
    s, d = x1.shape
    tb = _tile(s, 512)

    def body(x1_ref, f_ref, w_ref, g_ref, lng_ref, lnb_ref, t_ref, dr_ref, dy_ref, loss_ref, dlg_ref, dlb_ref, dg_ref):
        @pl.when(pl.program_id(0) == 0)
        def _():
            for r in (loss_ref, dlg_ref, dlb_ref, dg_ref):
                r[...] = jnp.zeros_like(r)

        yv = jnp.dot(f_ref[...], w_ref[...], preferred_element_type=f32)
        xh, rstd = _ln(alpha * x1_ref[...] + g_ref[...] * yv)
        err = xh * lng_ref[...] + lnb_ref[...] - t_ref[...]
        loss_ref[...] += 0.5 * jnp.sum(jnp.sum(err * err, axis=-1, keepdims=True) / d, axis=0, keepdims=True)
        dx2 = err / d
        dlg_ref[...] += _colsum(dx2 * xh)
        dlb_ref[...] += _colsum(dx2)
        dr = _ln_bwd(dx2 * lng_ref[...], xh, rstd)
        dg_ref[...] += _colsum(dr * yv)
        dr_ref[...] = dr
        dy_ref[...] = (g_ref[...] * dr).astype(bf16)

    blk = pl.BlockSpec((tb, d), lambda i: (i, 0))
    vec = pl.BlockSpec((1, d), _row)
    vshape = jax.ShapeDtypeStruct((1, d), f32)
    return pl.pallas_call(
        body, name=name, grid=(s // tb,),
        in_specs=[blk, pl.BlockSpec((tb, f.shape[1]), lambda i: (i, 0)), _full(w_dn.shape), vec, vec, vec, blk],
        out_specs=[blk, blk, pl.BlockSpec((1, 1), _row), vec, vec, vec],
        out_shape=[jax.ShapeDtypeStruct((s, d), f32), jax.ShapeDtypeStruct((s, d), bf16),
                   jax.ShapeDtypeStruct((1, 1), f32), vshape, vshape, vshape],
        compiler_params=_params("arbitrary"))(x1, f, w_dn, g, lng, lnb, tgt)


def _mid_bwd(dh2, x1, dr2, x, o, g, sc, lng, alpha, *, name):
    s, d = x.shape
    tb = _tile(s, 512)

    def body(dh_ref, x1_ref, dr2_ref, x_ref, o_ref, g_ref, sc_ref, lng_ref,
             dr1_ref, do_ref, dsc_ref, dsh_ref, dlg_ref, dlb_ref, dg_ref):
        @pl.when(pl.program_id(0) == 0)
        def _():
            for r in (dsc_ref, dsh_ref, dlg_ref, dlb_ref, dg_ref):
                r[...] = jnp.zeros_like(r)

        dh = dh_ref[...].astype(f32)
        xh1, rstd1 = _ln(x1_ref[...])
        dsc_ref[...] += _colsum(dh * xh1)
        dsh_ref[...] += _colsum(dh)
        dx1 = alpha * dr2_ref[...] + _ln_bwd(dh * (1.0 + sc_ref[...]), xh1, rstd1)
        ov = o_ref[...].astype(f32)
        xhr, rstdr = _ln(alpha * x_ref[...] + g_ref[...] * ov)
        dlg_ref[...] += _colsum(dx1 * xhr)
        dlb_ref[...] += _colsum(dx1)
        dr1 = _ln_bwd(dx1 * lng_ref[...], xhr, rstdr)
        dg_ref[...] += _colsum(dr1 * ov)
        dr1_ref[...] = dr1
        do_ref[...] = (g_ref[...] * dr1).astype(bf16)

    blk = pl.BlockSpec((tb, d), lambda i: (i, 0))
    vec = pl.BlockSpec((1, d), _row)
    vshape = jax.ShapeDtypeStruct((1, d), f32)
    return pl.pallas_call(
        body, name=name, grid=(s // tb,), in_specs=[blk] * 5 + [vec] * 3, out_specs=[blk, blk] + [vec] * 5,
        out_shape=[jax.ShapeDtypeStruct((s, d), f32), jax.ShapeDtypeStruct((s, d), bf16)] + [vshape] * 5,
        compiler_params=_params("arbitrary"))(dh2, x1, dr2, x, o, g, sc, lng)


def _final_bwd(dh1, x, dr1, sc, alpha, *, name, xchg=None):
    s, d = x.shape
    tb = _tile(s, MM_ROWS)

    def body(dh_ref, x_ref, dr1_ref, sc_ref, dx_ref, dsc_ref, dsh_ref):
        @pl.when(pl.program_id(0) == 0)
        def _():
            dsc_ref[...] = jnp.zeros_like(dsc_ref)
            dsh_ref[...] = jnp.zeros_like(dsh_ref)

        dh = dh_ref[...].astype(f32)
        xh, rstd = _ln(x_ref[...])
        dsc_ref[...] += _colsum(dh * xh)
        dsh_ref[...] += _colsum(dh)
        dx_ref[...] = alpha * dr1_ref[...] + _ln_bwd(dh * (1.0 + sc_ref[...]), xh, rstd)

    blk = pl.BlockSpec((tb, d), lambda i: (i, 0))
    vec = pl.BlockSpec((1, d), _row)
    vshape = jax.ShapeDtypeStruct((1, d), f32)
    res, got = _call(body, (dh1, x, dr1, sc), name=name, grid=(s // tb,), in_specs=[blk, blk, blk, vec], out_specs=[blk, vec, vec],
                     out_shape=[jax.ShapeDtypeStruct((s, d), f32), vshape, vshape], sem=("arbitrary",), xchg=xchg)
    return tuple(res) + (got,)


TALL_ROWS = 64


def _ffn_col_tile(fh):
    return fh // 2 if (fh // 2) % LANES == 0 else fh


def _ffn_specs(s, fh, tb, tc):
    per = tb // FFN_HALO
    blk = pl.BlockSpec((tb, tc), lambda j, i: (i, j))
    prev = pl.BlockSpec((FFN_HALO, tc), lambda j, i: (jnp.maximum(i * per - 1, 0), j))
    nxt = pl.BlockSpec((FFN_HALO, tc), lambda j, i: (jnp.minimum((i + 1) * per, s // FFN_HALO - 1), j))
    taps = pl.BlockSpec((FFN_HALO, tc), lambda j, i: (0, j))
    vec = pl.BlockSpec((1, tc), lambda j, i: (0, j))
    return blk, prev, nxt, taps, vec


def _ffn_mid(upa, upv, wa, wv, ba, bv, *, name, xchg=None):
    s, fh = upa.shape
    tb, tc = _tile(s, 512), _ffn_col_tile(fh)
    blk, prev, _, taps, vec = _ffn_specs(s, fh, tb, tc)
    off = FFN_HALO - FFN_K + 1

    def body(a_ref, v_ref, ah_ref, vh_ref, wa_ref, wv_ref, ba_ref, bv_ref, o_ref):
        first = pl.program_id(1) == 0
        for lt in range(tc // LANES):
            cols = slice(lt * LANES, (lt + 1) * LANES)
            halo_a, halo_v = jnp.where(first, 0.0, ah_ref[:, cols]), jnp.where(first, 0.0, vh_ref[:, cols])
            wa = [wa_ref[pl.ds(k, 1), cols] for k in range(FFN_K)]
            wv = [wv_ref[pl.ds(k, 1), cols] for k in range(FFN_K)]
            ba, bv = ba_ref[:, cols], bv_ref[:, cols]

            def chunk(r0, carry, head=False, cols=cols, halo_a=halo_a, halo_v=halo_v, wa=wa, wv=wv, ba=ba, bv=bv):
                a2 = _taps3(_window_before(a_ref, halo_a, r0, TALL_ROWS, head, cols), wa, off, TALL_ROWS) + ba
                v2 = _taps3(_window_before(v_ref, halo_v, r0, TALL_ROWS, head, cols), wv, off, TALL_ROWS) + bv
                o_ref[pl.ds(r0, TALL_ROWS), cols] = (_gelu(a2) * v2).astype(bf16)
                return carry

            chunk(0, 0, head=True)
            _row_chunks(tb, TALL_ROWS, chunk, 0, start=1)

    (f,), got = _call(
        body, (upa, upv, upa, upv, wa, wv, ba, bv), name=name, grid=(fh // tc, s // tb),
        in_specs=[blk, blk, prev, prev, taps, taps, vec, vec], out_specs=[blk], out_shape=[jax.ShapeDtypeStruct((s, fh), bf16)],
        sem=("parallel", "arbitrary"), xchg=xchg)
    return f, got


def _ffn_mid_bwd_tile(cols, first, last, tb, off, df_ref, dfn_ref, a_ref, v_ref, ah_ref, vh_ref, an_ref, vn_ref, wa_ref, wv_ref,
                      ba_ref, bv_ref, da_ref, dv_ref, dwa_ref, dwv_ref, dba_ref, dbv_ref, dexta, dextv):
    rows_c = TALL_ROWS
    halo_a, halo_v = jnp.where(first, 0.0, ah_ref[:, cols]), jnp.where(first, 0.0, vh_ref[:, cols])
    wa = [wa_ref[pl.ds(k, 1), cols] for k in range(FFN_K)]
    wv = [wv_ref[pl.ds(k, 1), cols] for k in range(FFN_K)]
    ba, bv = ba_ref[:, cols], bv_ref[:, cols]

    def conv_cotangents(r0, rows, xa, xv, dfe):
        sa = [_rows_from(xa, off + k, rows) for k in range(FFN_K)]
        sv = [_rows_from(xv, off + k, rows) for k in range(FFN_K)]
        a2 = sa[0] * wa[0] + sa[1] * wa[1] + sa[2] * wa[2] + ba
        v2 = sv[0] * wv[0] + sv[1] * wv[1] + sv[2] * wv[2] + bv
        cdf = 0.5 * (1.0 + lax.erf(a2 * INV_SQRT2))
        da2 = dfe * v2 * (cdf + a2 * jnp.exp(-0.5 * a2 * a2) * INV_SQRT_2PI)
        dv2 = dfe * (a2 * cdf)
        dexta[pl.ds(r0, rows), cols] = da2
        dextv[pl.ds(r0, rows), cols] = dv2
        return da2, dv2, sa, sv

    def chunk(r0, sums, head=False):
        da2, dv2, sa, sv = conv_cotangents(r0, rows_c, _window_before(a_ref, halo_a, r0, rows_c, head, cols),
                                           _window_before(v_ref, halo_v, r0, rows_c, head, cols), df_ref[pl.ds(r0, rows_c), cols])
        new = [sums[k] + _fold8(da2 * sa[k]) for k in range(FFN_K)] + [sums[FFN_K] + _fold8(da2)]
        new += [sums[FFN_K + 1 + k] + _fold8(dv2 * sv[k]) for k in range(FFN_K)] + [sums[2 * FFN_K + 1] + _fold8(dv2)]
        return tuple(new)

    sums = chunk(0, tuple(jnp.zeros((SUBLANES, LANES), f32) for _ in range(2 * FFN_K + 2)), head=True)
    sums = _row_chunks(tb, rows_c, chunk, sums, start=1)
    conv_cotangents(tb, FFN_HALO,
                    jnp.concatenate([a_ref[pl.ds(tb - FFN_HALO, FFN_HALO), cols], jnp.where(last, 0.0, an_ref[:, cols])], axis=0),
                    jnp.concatenate([v_ref[pl.ds(tb - FFN_HALO, FFN_HALO), cols], jnp.where(last, 0.0, vn_ref[:, cols])], axis=0),
                    jnp.where(last, 0.0, dfn_ref[:, cols]))
    for k in range(FFN_K):
        dwa_ref[pl.ds(k, 1), cols] += _colsum(sums[k])
        dwv_ref[pl.ds(k, 1), cols] += _colsum(sums[FFN_K + 1 + k])
    dba_ref[:, cols] += _colsum(sums[FFN_K])
    dbv_ref[:, cols] += _colsum(sums[2 * FFN_K + 1])

    def back(r0, carry):
        for dext, w, o_ref in ((dexta, wa, da_ref), (dextv, wv, dv_ref)):
            dd = dext[pl.ds(r0, rows_c + FFN_HALO), cols]
            o_ref[pl.ds(r0, rows_c), cols] = (_rows_from(dd, 2, rows_c) * w[0] + _rows_from(dd, 1, rows_c) * w[1]
                                              + dd[0:rows_c] * w[2]).astype(bf16)
        return carry

    _row_chunks(tb, rows_c, back, 0)


def _ffn_mid_bwd(df, upa, upv, wa, wv, ba, bv, *, name, xchg=None):
    s, fh = upa.shape
    tb, tc = _tile(s, 512), _ffn_col_tile(fh)
    nb = s // tb
    blk, prev, nxt, taps, vec = _ffn_specs(s, fh, tb, tc)
    off = FFN_HALO - FFN_K + 1
    te = tb + FFN_HALO

    def body(df_ref, dfn_ref, a_ref, v_ref, ah_ref, vh_ref, an_ref, vn_ref, wa_ref, wv_ref, ba_ref, bv_ref,
             da_ref, dv_ref, dwa_ref, dwv_ref, dba_ref, dbv_ref, dexta, dextv):
        i = pl.program_id(1)

        @pl.when(i == 0)
        def _():
            for r in (dwa_ref, dwv_ref, dba_ref, dbv_ref):
                r[...] = jnp.zeros_like(r)

        last = i == nb - 1
        for lt in range(tc // LANES):
            _ffn_mid_bwd_tile(slice(lt * LANES, (lt + 1) * LANES), i == 0, last, tb, off, df_ref, dfn_ref, a_ref, v_ref,
                              ah_ref, vh_ref, an_ref, vn_ref, wa_ref, wv_ref, ba_ref, bv_ref, da_ref, dv_ref, dwa_ref, dwv_ref,
                              dba_ref, dbv_ref, dexta, dextv)

    act = jax.ShapeDtypeStruct((s, fh), bf16)
    wshape = jax.ShapeDtypeStruct((FFN_HALO, fh), f32)
    vshape = jax.ShapeDtypeStruct((1, fh), f32)
    res, got = _call(
        body, (df, df, upa, upv, upa, upv, upa, upv, wa, wv, ba, bv), name=name, grid=(fh // tc, nb),
        in_specs=[blk, nxt, blk, blk, prev, prev, nxt, nxt, taps, taps, vec, vec],
        out_specs=[blk, blk, taps, taps, vec, vec], out_shape=[act, act, wshape, wshape, vshape, vshape],
        scratch_shapes=[pltpu.VMEM((te, tc), f32)] * 2, sem=("parallel", "arbitrary"), xchg=xchg)
    return tuple(res) + (got,)


def _cols_from_shards(stacked):
    _, k, n = stacked.shape
    return stacked.transpose(1, 0, 2).reshape(k, NDEV * n)


def _pad_rows(w, rows):
    return jnp.pad(w, ((0, rows - w.shape[0]), (0, 0)))


def kernel(x, c, w_cond, b_cond, w_in, b_in, ssm_lambda_re, ssm_lambda_im, ssm_log_dt, ssm_b_re, ssm_b_im, ssm_c_re, ssm_c_im, ssm_d, ssm_glu_w_a, ssm_glu_w_b, cv_dw_w, cv_dw_b, cv_ln_g, cv_ln_b, cv_w_pw, w_out, ln1_g, ln1_b, ffn_w_up, ffn_dw_w, ffn_dw_b, ffn_w_down, ln2_g, ln2_b, loss_target, m_w_cond, m_b_cond, m_w_in, m_b_in, m_ssm_lambda_re, m_ssm_lambda_im, m_ssm_log_dt, m_ssm_b_re, m_ssm_b_im, m_ssm_c_re, m_ssm_c_im, m_ssm_d, m_ssm_glu_w_a, m_ssm_glu_w_b, m_cv_dw_w, m_cv_dw_b, m_cv_ln_g, m_cv_ln_b, m_cv_w_pw, m_w_out, m_ln1_g, m_ln1_b, m_ffn_w_up, m_ffn_dw_w, m_ffn_dw_b, m_ffn_w_down, m_ln2_g, m_ln2_b, v_w_cond, v_b_cond, v_w_in, v_b_in, v_ssm_lambda_re, v_ssm_lambda_im, v_ssm_log_dt, v_ssm_b_re, v_ssm_b_im, v_ssm_c_re, v_ssm_c_im, v_ssm_d, v_ssm_glu_w_a, v_ssm_glu_w_b, v_cv_dw_w, v_cv_dw_b, v_cv_ln_g, v_cv_ln_b, v_cv_w_pw, v_w_out, v_ln1_g, v_ln1_b, v_ffn_w_up, v_ffn_dw_w, v_ffn_dw_b, v_ffn_w_down, v_ln2_g, v_ln2_b):
    weights = dict(w_cond=w_cond, b_cond=b_cond, w_in=w_in, b_in=b_in, ssm_lambda_re=ssm_lambda_re, ssm_lambda_im=ssm_lambda_im, ssm_log_dt=ssm_log_dt, ssm_b_re=ssm_b_re, ssm_b_im=ssm_b_im, ssm_c_re=ssm_c_re, ssm_c_im=ssm_c_im, ssm_d=ssm_d, ssm_glu_w_a=ssm_glu_w_a, ssm_glu_w_b=ssm_glu_w_b, cv_dw_w=cv_dw_w, cv_dw_b=cv_dw_b, cv_ln_g=cv_ln_g, cv_ln_b=cv_ln_b, cv_w_pw=cv_w_pw, w_out=w_out, ln1_g=ln1_g, ln1_b=ln1_b, ffn_w_up=ffn_w_up, ffn_dw_w=ffn_dw_w, ffn_dw_b=ffn_dw_b, ffn_w_down=ffn_w_down, ln2_g=ln2_g, ln2_b=ln2_b)
    mom_m = dict(w_cond=m_w_cond, b_cond=m_b_cond, w_in=m_w_in, b_in=m_b_in, ssm_lambda_re=m_ssm_lambda_re, ssm_lambda_im=m_ssm_lambda_im, ssm_log_dt=m_ssm_log_dt, ssm_b_re=m_ssm_b_re, ssm_b_im=m_ssm_b_im, ssm_c_re=m_ssm_c_re, ssm_c_im=m_ssm_c_im, ssm_d=m_ssm_d, ssm_glu_w_a=m_ssm_glu_w_a, ssm_glu_w_b=m_ssm_glu_w_b, cv_dw_w=m_cv_dw_w, cv_dw_b=m_cv_dw_b, cv_ln_g=m_cv_ln_g, cv_ln_b=m_cv_ln_b, cv_w_pw=m_cv_w_pw, w_out=m_w_out, ln1_g=m_ln1_g, ln1_b=m_ln1_b, ffn_w_up=m_ffn_w_up, ffn_dw_w=m_ffn_dw_w, ffn_dw_b=m_ffn_dw_b, ffn_w_down=m_ffn_w_down, ln2_g=m_ln2_g, ln2_b=m_ln2_b)
    mom_v = dict(w_cond=v_w_cond, b_cond=v_b_cond, w_in=v_w_in, b_in=v_b_in, ssm_lambda_re=v_ssm_lambda_re, ssm_lambda_im=v_ssm_lambda_im, ssm_log_dt=v_ssm_log_dt, ssm_b_re=v_ssm_b_re, ssm_b_im=v_ssm_b_im, ssm_c_re=v_ssm_c_re, ssm_c_im=v_ssm_c_im, ssm_d=v_ssm_d, ssm_glu_w_a=v_ssm_glu_w_a, ssm_glu_w_b=v_ssm_glu_w_b, cv_dw_w=v_cv_dw_w, cv_dw_b=v_cv_dw_b, cv_ln_g=v_cv_ln_g, cv_ln_b=v_cv_ln_b, cv_w_pw=v_cv_w_pw, w_out=v_w_out, ln1_g=v_ln1_g, ln1_b=v_ln1_b, ffn_w_up=v_ffn_w_up, ffn_dw_w=v_ffn_dw_w, ffn_dw_b=v_ffn_dw_b, ffn_w_down=v_ffn_w_down, ln2_g=v_ln2_g, ln2_b=v_ln2_b)
    names = list(weights)

    s, d = x.shape[1], x.shape[2]
    sw = cw = d // 2
    fh = ffn_w_down.shape[1] * NDEV
    ng, nq = sw // SSM_GROUP, sw // QW
    gq = ng // nq
    alpha = 2.0 ** 0.25
    me = 4 * lax.axis_index("x") + 2 * lax.axis_index("y") + lax.axis_index("c")
    xs, tgt = x[0], loss_target[0]

    col_names = ["w_in", "ssm_glu_w_a", "ssm_glu_w_b", "cv_w_pw", "ffn_w_up"]
    row_names = ["w_out", "ffn_w_down"]
    big = col_names + row_names
    sent = lambda ns: [weights[n][0].astype(bf16) for n in ns]
    got_in, got_c, got_cv_taps, got_ffn_taps = _exchange(sent(["w_in"]) + [c, cv_dw_w[0, :, 0], ffn_dw_w[0, :, 0]],
                                                         scatter=False, name="gather_in")
    o1, o2, o3, o4 = sw, sw + cw, sw + 2 * cw, sw + 2 * cw + d
    in_bounds = ((0, o1), (o1, o2), (o2, o3), (o3, o4), (o4, o4 + d))
    w_u, w_cva, w_cvg, w_gs, w_gc = _unshard_cols(got_in, in_bounds, name="unshard_w_in")
    b_u, b_cva, b_cvg, b_gs, b_gc = (b_in[:, a:b] for a, b in in_bounds)
    c_all = got_c.reshape(NDEV, d)
    cv_taps = _cols_from_shards(got_cv_taps)
    ffn_taps = _cols_from_shards(got_ffn_taps)
    cv_w32 = _pad_rows(cv_taps, CONV_HALO)
    ffn_wa, ffn_wv = _pad_rows(ffn_taps[:, :fh], FFN_HALO), _pad_rows(ffn_taps[:, fh:], FFN_HALO)
    ffn_ba, ffn_bv = ffn_dw_b[:, :fh], ffn_dw_b[:, fh:]

    ncond = w_cond.shape[2]
    b_cond_mine = lax.dynamic_slice(b_cond, (0, me * ncond), (1, ncond))
    mod_cols = _cond_fwd(c_all, w_cond[0], b_cond_mine, name="cond_fwd")
    mod_all, = _exchange([mod_cols], scatter=False, name="gather_mod")
    mod_mine = lax.dynamic_slice(mod_all, (0, me, 0), (NDEV, 1, ncond)).reshape(1, 6 * d)
    sh1, sc1, g1, sh2, sc2, g2 = (mod_mine[:, k * d:(k + 1) * d] for k in range(6))

    lam_re, lam_im, log_dt = ssm_lambda_re[0], ssm_lambda_im[0], ssm_log_dt[0][:, None]
    lbr, lbi, cfr, cfi = _ssm_prep(lam_re, lam_im, log_dt, name="ssm_prep")
    rows_q = lambda a: a.reshape(nq, 1, QS)
    eye = jnp.eye(gq, dtype=f32)

    def b_mat(b):
        bt = b.reshape(nq, gq, SSM_STATE, SSM_GROUP).transpose(0, 1, 3, 2)
        return jnp.einsum("qgpn,gh->qgphn", bt, eye).reshape(nq, QW, QS)

    def c_mat(cc):
        ct = cc.reshape(nq, gq, SSM_GROUP, SSM_STATE)
        return jnp.einsum("qgpn,gh->qhngp", ct, eye).reshape(nq, QS, QW)

    def b_unmat(mt):
        return jnp.einsum("qgpgn->qgnp", mt.reshape(nq, gq, SSM_GROUP, gq, SSM_STATE)).reshape(ng, SSM_STATE, SSM_GROUP)

    def c_unmat(mt):
        return jnp.einsum("qgngp->qgpn", mt.reshape(nq, gq, SSM_STATE, gq, SSM_GROUP)).reshape(ng, SSM_GROUP, SSM_STATE)

    ssm_args = (rows_q(lbr), rows_q(lbi), b_mat(ssm_b_re[0]), b_mat(ssm_b_im[0]), c_mat(ssm_c_re[0]), c_mat(ssm_c_im[0]),
                rows_q(cfr), rows_q(cfi), ssm_d[0].reshape(1, sw))

    h1, (u, cva, cvg, gs, gc) = _ln_mod_fanout(xs, sc1, sh1, [w_u, w_cva, w_cvg, w_gs, w_gc], [b_u, b_cva, b_cvg, b_gs, b_gc],
                                               [f32, f32, f32, bf16, bf16], name="ln_mod_in_proj")
    v4, cv2, (got_a, got_b, got_pw, got_o) = _conv_fwd(
        cva, cvg, cv_w32, cv_dw_b, cv_ln_g, cv_ln_b, name="conv_fwd",
        xchg=(sent(["ssm_glu_w_a", "ssm_glu_w_b", "cv_w_pw", "w_out"]), False))
    h_p, yraw3, y, (got_up,) = _ssm_fwd(u, *ssm_args, name="ssm_fwd", xchg=(sent(["ffn_w_up"]), False))
    w_a, = _unshard_cols(got_a, ((0, d),), name="unshard_glu_a")
    w_b, = _unshard_cols(got_b, ((0, d),), name="unshard_glu_b")
    w_pw, = _unshard_cols(got_pw, ((0, d),), name="unshard_conv_pw")
    w_upa, w_upv = _unshard_cols(got_up, ((0, fh), (fh, 2 * fh)), name="unshard_ffn_up")
    w_o = got_o.reshape(d, d)
    ya, yb = _mm_fanout(y, [w_a, w_b], None, [bf16, bf16], name="glu")
    ycv = _mm([(v4, w_pw)], out_dtype=bf16, name="conv_pw")
    merged, o = _glu_merge_out(ya, yb, ycv, gs, gc, w_o, name="merge_out_proj")
    x1, h2 = _resid_ln_mod(xs, o, g1, ln1_g, ln1_b, sc2, sh2, alpha, name="resid_ln1")
    upa = _mm([(h2, w_upa)], name="ffn_up_a")
    upv = _mm([(h2, w_upv)], name="ffn_up_v")
    f, (got_dn,) = _ffn_mid(upa, upv, ffn_wa, ffn_wv, ffn_ba, ffn_bv, name="ffn_mid", xchg=(sent(["ffn_w_down"]), False))
    w_dn = got_dn.reshape(fh, d)
    dr2, dy2, loss_part, d_ln2_g, d_ln2_b, d_g2 = _resid_ln_loss(x1, f, w_dn, g2, ln2_g, ln2_b, tgt, alpha, name="ffn_down_ln2_loss")

    gw = {}
    df = _mm([(dy2, w_dn)], trans_w=True, name="d_ffn_down")
    gw["ffn_w_down"] = _mm_tn(f, dy2, out_dtype=bf16, name="g_ffn_down").reshape((NDEV,) + ffn_w_down[0].shape)
    received = {}
    dupa, dupv, d_ffn_wa, d_ffn_wv, d_ffn_ba, d_ffn_bv, (received["ffn_w_down"],) = _ffn_mid_bwd(
        df, upa, upv, ffn_wa, ffn_wv, ffn_ba, ffn_bv, name="ffn_mid_bwd", xchg=([gw["ffn_w_down"]], True))
    dh2 = _mm([(dupa, w_upa), (dupv, w_upv)], trans_w=True, out_dtype=bf16, name="d_ffn_up")
    gw["ffn_w_up"] = _shard_cols([_mm_tn(h2, dupa, name="g_ffn_up_a"), _mm_tn(h2, dupv, name="g_ffn_up_v")], out_dtype=bf16,
                                 name="shard_ffn_up")
    dr1, do, d_sc2, d_sh2, d_ln1_g, d_ln1_b, d_g1 = _mid_bwd(dh2, x1, dr2, xs, o, g1, sc2, ln1_g, alpha, name="mid_bwd")
    dmerged = _mm([(do, w_o)], trans_w=True, out_dtype=bf16, name="d_out_proj")
    gw["w_out"] = _mm_tn(merged, do, out_dtype=bf16, name="g_out_proj").reshape((NDEV,) + w_out[0].shape)
    dya, dyb, dycv, dgs, dgc, s_gs, s_gc = _glu_merge_bwd(dmerged, ya, yb, ycv, gs, gc, name="merge_bwd")
    dy = _mm([(dya, w_a), (dyb, w_b)], trans_w=True, out_dtype=bf16, name="d_glu")
    gw["ssm_glu_w_a"] = _mm_tn_sharded(y, [dya], out_dtype=bf16, name="g_glu_a")
    gw["ssm_glu_w_b"] = _mm_tn_sharded(y, [dyb], out_dtype=bf16, name="g_glu_b")
    dv4 = _mm([(dycv, w_pw)], trans_w=True, out_dtype=bf16, name="d_conv_pw")
    gw["cv_w_pw"] = _mm_tn_sharded(v4, [dycv], out_dtype=bf16, name="g_conv_pw")
    dv2, d_cv_ln_g, d_cv_ln_b = _conv_bwd_ln(dv4, cv2, cv_ln_g, cv_ln_b, name="conv_bwd_ln")
    dcva, dcvg, d_cv_w32, d_cv_b, s_cva, s_cvg, (received["ffn_w_up"],) = _conv_bwd_taps(
        dv2, cva, cvg, cv_w32, name="conv_bwd_taps", xchg=([gw["ffn_w_up"]], True))
    late = ["w_out", "ssm_glu_w_a", "ssm_glu_w_b", "cv_w_pw"]
    (du, d_bre_m, d_bim_m, d_cre_m, d_cim_m, d_cfr, d_cfi, d_lbr, d_lbi, d_d, s_u, got_late) = _ssm_bwd(
        dy, yraw3, u, h_p, *ssm_args, name="ssm_bwd", xchg=([gw[n] for n in late], True))
    received.update(zip(late, got_late))
    gshape = lam_re.shape
    d_lam_re, d_lam_im, d_log_dt = _ssm_prep_bwd(
        lam_re, lam_im, log_dt, [a.reshape(gshape) for a in (d_lbr, d_lbi, d_cfr, d_cfi)], name="ssm_prep_bwd")
    small = {
        "b_in": jnp.concatenate([s_u, s_cva, s_cvg, s_gs, s_gc], axis=1),
        "ssm_lambda_re": d_lam_re, "ssm_lambda_im": d_lam_im, "ssm_log_dt": d_log_dt,
        "ssm_b_re": b_unmat(d_bre_m), "ssm_b_im": b_unmat(d_bim_m), "ssm_c_re": c_unmat(d_cre_m), "ssm_c_im": c_unmat(d_cim_m),
        "ssm_d": d_d, "cv_dw_w": d_cv_w32[:CONV_K], "cv_dw_b": d_cv_b, "cv_ln_g": d_cv_ln_g, "cv_ln_b": d_cv_ln_b,
        "ln1_g": d_ln1_g, "ln1_b": d_ln1_b,
        "ffn_dw_w": jnp.concatenate([d_ffn_wa[:FFN_K], d_ffn_wv[:FFN_K]], axis=1),
        "ffn_dw_b": jnp.concatenate([d_ffn_ba, d_ffn_bv], axis=1), "ln2_g": d_ln2_g, "ln2_b": d_ln2_b,
        "mod_g1": d_g1, "mod_sh2": d_sh2, "mod_sc2": d_sc2, "mod_g2": d_g2, "loss": loss_part,
    }
    small_names = list(small)
    small_shapes = [small[n].shape for n in small_names]
    gw["w_in"], (small_all,) = _mm_tn_sharded(h1, [du, dcva, dcvg, dgs, dgc], out_dtype=bf16, name="g_in",
                                              xchg=([_pack([small[n] for n in small_names])], False))
    dh1, (received["w_in"],) = _mm(
        [(du, w_u), (dcva, w_cva), (dcvg, w_cvg), (dgs, w_gs), (dgc, w_gc)], trans_w=True, out_dtype=bf16, name="d_in",
        xchg=([gw["w_in"]], True))
    grad_x, d_sc1, d_sh1, _ = _final_bwd(dh1, xs, dr1, sc1, alpha, name="final_bwd")

    grads, delta, new_m, new_v = {}, {}, {}, {}
    for n in big:
        ride = ([_pack([d_sh1, d_sc1])], False) if n == "w_out" else None
        res = _sum_adamw(received[n], weights[n][0], mom_m[n][0], mom_v[n][0], name="adamw_" + n, xchg=ride)
        grads[n], delta[n], new_m[n], new_v[n] = res[:4]
        if ride is not None:
            last_all, = res[4]

    small_sum = dict(zip(small_names, _unpack(_sum_parts(small_all, name="sum_small").reshape(-1), small_shapes)))
    last_sum = _unpack(_sum_parts(last_all, name="sum_last").reshape(-1), [(1, d), (1, d)])
    per_dev = dict(zip(small_names, _unpack(small_all.reshape(NDEV, -1), small_shapes)))
    last_dev = _unpack(last_all.reshape(NDEV, -1), [(1, d), (1, d)])
    dmod_all = jnp.concatenate(last_dev + [per_dev[k] for k in ("mod_g1", "mod_sh2", "mod_sc2", "mod_g2")], axis=-1).reshape(NDEV, 6 * d)
    dmod_cols = lax.dynamic_slice(dmod_all.reshape(NDEV, NDEV, ncond), (0, me, 0), (NDEV, 1, ncond)).reshape(NDEV, ncond)
    grads["w_cond"] = _cond_bwd(c_all, dmod_cols, name="cond_bwd")
    loss = small_sum.pop("loss").reshape(())
    grads["b_cond"] = jnp.concatenate(last_sum + [small_sum.pop(k) for k in ("mod_g1", "mod_sh2", "mod_sc2", "mod_g2")], axis=1)
    for n, g in small_sum.items():
        grads[n] = g
    ntap = cv_dw_w.shape[3]
    grads["cv_dw_w"] = lax.dynamic_slice(grads["cv_dw_w"], (0, me * ntap), (CONV_K, ntap))
    nffn = ffn_dw_w.shape[3]
    grads["ffn_dw_w"] = lax.dynamic_slice(grads["ffn_dw_w"], (0, me * nffn), (FFN_K, nffn))
    grads = {n: grads[n].reshape(weights[n].shape) for n in names}

    delta["w_cond"], new_m["w_cond"], new_v["w_cond"] = _adamw(w_cond[0], grads["w_cond"][0], m_w_cond[0], v_w_cond[0],
                                                               name="adamw_w_cond")
    rest = [n for n in names if n not in ["w_cond"] + big]
    squeeze = lambda a: a if a.ndim == 2 else a[0]
    results = _adamw_many(*[[squeeze(t[n].reshape(weights[n].shape)) for n in rest] for t in (weights, grads, mom_m, mom_v)],
                          name="adamw_small")
    for n, (dl, nm, nv) in zip(rest, results):
        delta[n], new_m[n], new_v[n] = dl, nm, nv
    shaped = lambda t: [t[n].reshape(weights[n].shape) for n in names]

    return (loss, grad_x[None], *shaped(grads), *shaped(delta), *shaped(new_m), *shaped(new_v))
```
